```python
import jax, jax.numpy as jnp
from jax import lax
import numpy as np

D_MODEL = 1024
BATCH = 8
SEQ = 4096
DEPTH = 2

GRID_W = 64
CTX_LEN = 256
POOL_WIDTH = 512
POOL_GROUPS = 4
POOL_GROUP_DIM = POOL_WIDTH // POOL_GROUPS
POOL_WINDOWS = (2, 4, 8, 16)
N_HEADS = 8
N_KV_HEADS = 2
HEAD_DIM = 64
Q_GROUP = N_HEADS // N_KV_HEADS
ATTN_WIDTH = N_HEADS * HEAD_DIM
KV_WIDTH = N_KV_HEADS * HEAD_DIM
MIX_WIDTH = POOL_WIDTH + ATTN_WIDTH
PROJ_WIDTH = POOL_WIDTH + ATTN_WIDTH + 2 * KV_WIDTH
WINDOW = 128
BLOCK = 128
ROPE_BASE = 10000.0
ROPE_AXIS_DIM = HEAD_DIM // 2
D_FF = 2816
N_MOD = 9
EPS = 1e-6
NEG_INF = -1e30

kernel_name = "hybrid_pool_swa_macaron_dit_block"


def rmsnorm(x, g):
    xf = x.astype(jnp.float32)
    y = xf * lax.rsqrt(jnp.mean(xf * xf, axis=-1, keepdims=True) + EPS)
    return (y * g.astype(jnp.float32)).astype(x.dtype)


def norm_modulate(x, g, shift, scale):
    return rmsnorm(x, g) * (1 + scale) + shift


def swiglu(n, w_in, w_out):
    a, b = jnp.split(n @ w_in, 2, axis=-1)
    return (jax.nn.silu(a) * b) @ w_out


def axial_rope_tables(T):
    rows = T // GRID_W
    row = jnp.repeat(jnp.arange(rows), GRID_W).astype(jnp.float32)
    col = jnp.tile(jnp.arange(GRID_W), rows).astype(jnp.float32)
    inv = ROPE_BASE ** (-jnp.arange(0, ROPE_AXIS_DIM, 2, dtype=jnp.float32) / ROPE_AXIS_DIM)
    ang = jnp.concatenate([row[:, None] * inv, col[:, None] * inv], axis=-1)
    return jnp.cos(ang), jnp.sin(ang)


def apply_rope(x, cos, sin):
    xf = x.astype(jnp.float32)
    x1, x2 = xf[..., :HEAD_DIM // 2], xf[..., HEAD_DIM // 2:]
    c, s = cos[None, :, None, :], sin[None, :, None, :]
    return jnp.concatenate([x1 * c - x2 * s, x2 * c + x1 * s], axis=-1).astype(x.dtype)


def pool_mixer(u, w_pool, pool_scale):
    B, T, _ = u.shape
    uf = u.astype(jnp.float32)
    cs = jnp.pad(jnp.cumsum(uf, axis=1), ((0, 0), (1, 0), (0, 0)))
    t = jnp.arange(T)
    outs = []
    for g, w in enumerate(POOL_WINDOWS):
        lo = jnp.clip(t - w // 2, 0, T)
        hi = jnp.clip(t + w - w // 2, 0, T)
        csg = cs[..., g * POOL_GROUP_DIM:(g + 1) * POOL_GROUP_DIM]
        outs.append((csg[:, hi] - csg[:, lo]) / (hi - lo).astype(jnp.float32)[None, :, None])
    pooled = (jnp.concatenate(outs, axis=-1) - uf).astype(u.dtype)
    pooled = pooled.reshape(B, T, POOL_GROUPS, POOL_GROUP_DIM)
    mixed = jnp.einsum("btgc,gcd->btgd", pooled, w_pool).reshape(B, T, POOL_WIDTH)
    return mixed * pool_scale


def latent_attention(q, k, v, kc, vc, sink):
    B, T = q.shape[:2]
    nb = T // BLOCK
    scale = HEAD_DIM ** -0.5
    qb = q.reshape(B, nb, BLOCK, N_KV_HEADS, Q_GROUP, HEAD_DIM)
    pad = ((0, 0), (BLOCK, BLOCK), (0, 0), (0, 0))
    kp = jnp.pad(k, pad).reshape(B, nb + 2, BLOCK, N_KV_HEADS, HEAD_DIM)
    vp = jnp.pad(v, pad).reshape(B, nb + 2, BLOCK, N_KV_HEADS, HEAD_DIM)
    kb = jnp.concatenate([kp[:, :-2], kp[:, 1:-1], kp[:, 2:]], axis=2)
    vb = jnp.concatenate([vp[:, :-2], vp[:, 1:-1], vp[:, 2:]], axis=2)
    s_loc = jnp.einsum("bnqhgd,bnkhd->bnhgqk", qb, kb).astype(jnp.float32) * scale
    n_idx = jnp.arange(nb)[:, None, None]
    qpos = n_idx * BLOCK + jnp.arange(BLOCK)[None, :, None]
    kpos = n_idx * BLOCK + jnp.arange(3 * BLOCK)[None, None, :] - BLOCK
    valid = (kpos >= 0) & (kpos < T) & (jnp.abs(kpos - qpos) <= WINDOW)
    s_loc = jnp.where(valid[None, :, None, None], s_loc, NEG_INF)
    s_ctx = jnp.einsum("bnqhgd,bmhd->bnhgqm", qb, kc).astype(jnp.float32) * scale
    s_sink = jnp.broadcast_to(sink.astype(jnp.float32).reshape(1, 1, N_KV_HEADS, Q_GROUP, 1, 1),
                              s_loc.shape[:-1] + (1,))
    p = jax.nn.softmax(jnp.concatenate([s_loc, s_ctx, s_sink], axis=-1), axis=-1)
    L = kc.shape[1]
    p_loc = p[..., :3 * BLOCK].astype(v.dtype)
    p_ctx = p[..., 3 * BLOCK:3 * BLOCK + L].astype(v.dtype)
    o = jnp.einsum("bnhgqk,bnkhd->bnqhgd", p_loc, vb) + jnp.einsum("bnhgqm,bmhd->bnqhgd", p_ctx, vc)
    return o.reshape(B, T, ATTN_WIDTH)


def context_attention(qc, kc, vc, sink):
    B, L = qc.shape[:2]
    qg = qc.reshape(B, L, N_KV_HEADS, Q_GROUP, HEAD_DIM)
    s = jnp.einsum("blhgd,bmhd->bhglm", qg, kc).astype(jnp.float32) * HEAD_DIM ** -0.5
    s_sink = jnp.broadcast_to(sink.astype(jnp.float32).reshape(1, N_KV_HEADS, Q_GROUP, 1, 1),
                              s.shape[:-1] + (1,))
    p = jax.nn.softmax(jnp.concatenate([s, s_sink], axis=-1), axis=-1)[..., :L].astype(vc.dtype)
    return jnp.einsum("bhglm,bmhd->blhgd", p, vc).reshape(B, L, ATTN_WIDTH)


def context_kv(nc, w_in):
    B, L = nc.shape[:2]
    kc, vc = jnp.split(nc @ w_in[:, MIX_WIDTH:], 2, axis=-1)
    return (kc.reshape(B, L, N_KV_HEADS, HEAD_DIM), vc.reshape(B, L, N_KV_HEADS, HEAD_DIM))


def mix_latent(n, kc, vc, w_in, w_pool, pool_scale, sink, w_out, cos, sin):
    B, T = n.shape[:2]
    u, q, k, v = jnp.split(n @ w_in, [POOL_WIDTH, MIX_WIDTH, MIX_WIDTH + KV_WIDTH], axis=-1)
    pool_out = pool_mixer(u, w_pool, pool_scale)
    q = apply_rope(q.reshape(B, T, N_HEADS, HEAD_DIM), cos, sin)
    k = apply_rope(k.reshape(B, T, N_KV_HEADS, HEAD_DIM), cos, sin)
    v = v.reshape(B, T, N_KV_HEADS, HEAD_DIM)
    attn_out = latent_attention(q, k, v, kc, vc, sink)
    return jnp.concatenate([pool_out, attn_out], axis=-1) @ w_out


def mix_context(nc, kc, vc, w_in, w_pool, pool_scale, sink, w_out):
    B, L = nc.shape[:2]
    u, q = jnp.split(nc @ w_in[:, :MIX_WIDTH], [POOL_WIDTH], axis=-1)
    pool_out = pool_mixer(u, w_pool, pool_scale)
    attn_out = context_attention(q.reshape(B, L, N_HEADS, HEAD_DIM), kc, vc, sink)
    return jnp.concatenate([pool_out, attn_out], axis=-1) @ w_out


def _fwd_setup_inputs(seed: int = 0) -> dict:
    key = jax.random.key(seed)
    ks = jax.random.split(key, 24)
    f32 = jnp.float32
    nrm = lambda k, shape, s: jax.random.normal(k, shape, f32) * s
    gain = lambda k, shape: 1.0 + 0.1 * jax.random.normal(k, shape, f32)
    return {
        "x": nrm(ks[0], (BATCH, SEQ, D_MODEL), 1.0),
        "c": nrm(ks[1], (BATCH, D_MODEL), 1.0),
        "ctx": nrm(ks[2], (BATCH, CTX_LEN, D_MODEL), 1.0),
        "c_ctx": nrm(ks[3], (D_MODEL,), 1.0),
        "w_mod": nrm(ks[4], (DEPTH, D_MODEL, N_MOD * D_MODEL), D_MODEL ** -0.5),
        "b_mod": nrm(ks[5], (DEPTH, N_MOD * D_MODEL), 0.02),
        "norm_ffn1": gain(ks[6], (DEPTH, D_MODEL)),
        "w_ffn1_in": nrm(ks[7], (DEPTH, D_MODEL, 2 * D_FF), D_MODEL ** -0.5),
        "w_ffn1_out": nrm(ks[8], (DEPTH, D_FF, D_MODEL), D_FF ** -0.5),
        "norm_mix": gain(ks[9], (DEPTH, D_MODEL)),
        "w_in": nrm(ks[10], (DEPTH, D_MODEL, PROJ_WIDTH), D_MODEL ** -0.5),
        "w_pool": nrm(ks[11], (DEPTH, POOL_GROUPS, POOL_GROUP_DIM, POOL_GROUP_DIM), POOL_GROUP_DIM ** -0.5),
        "pool_scale": gain(ks[12], (DEPTH, POOL_WIDTH)),
        "sink": nrm(ks[13], (DEPTH, N_HEADS), 1.0),
        "w_out": nrm(ks[14], (DEPTH, MIX_WIDTH, D_MODEL), MIX_WIDTH ** -0.5),
        "norm_ffn2": gain(ks[15], (DEPTH, D_MODEL)),
        "w_ffn2_in": nrm(ks[16], (DEPTH, D_MODEL, 2 * D_FF), D_MODEL ** -0.5),
        "w_ffn2_out": nrm(ks[17], (DEPTH, D_FF, D_MODEL), D_FF ** -0.5),
        "norm_final": gain(ks[18], (D_MODEL,)),
    }


def _fwd_reference(x, c, ctx, c_ctx, w_mod, b_mod, norm_ffn1, w_ffn1_in, w_ffn1_out, norm_mix, w_in,
              w_pool, pool_scale, sink, w_out, norm_ffn2, w_ffn2_in, w_ffn2_out, norm_final):
    B = x.shape[0]
    cos, sin = axial_rope_tables(x.shape[1])
    h, hc = x, ctx
    for l in range(DEPTH):
        last = l == DEPTH - 1
        mx = (jax.nn.silu(c) @ w_mod[l] + b_mod[l]).reshape(B, N_MOD, 1, D_MODEL)
        mc = (jax.nn.silu(c_ctx) @ w_mod[l] + b_mod[l]).reshape(N_MOD, D_MODEL)
        h = h + 0.5 * mx[:, 2] * swiglu(norm_modulate(h, norm_ffn1[l], mx[:, 0], mx[:, 1]),
                                         w_ffn1_in[l], w_ffn1_out[l])
        hc = hc + 0.5 * mc[2] * swiglu(norm_modulate(hc, norm_ffn1[l], mc[0], mc[1]),
                                        w_ffn1_in[l], w_ffn1_out[l])
        n = norm_modulate(h, norm_mix[l], mx[:, 3], mx[:, 4])
        nc = norm_modulate(hc, norm_mix[l], mc[3], mc[4])
        kc, vc = context_kv(nc, w_in[l])
        h = h + mx[:, 5] * mix_latent(n, kc, vc, w_in[l], w_pool[l], pool_scale[l], sink[l], w_out[l], cos, sin)
        if not last:
            hc = hc + mc[5] * mix_context(nc, kc, vc, w_in[l], w_pool[l], pool_scale[l], sink[l], w_out[l])
            hc = hc + 0.5 * mc[8] * swiglu(norm_modulate(hc, norm_ffn2[l], mc[6], mc[7]),
                                            w_ffn2_in[l], w_ffn2_out[l])
        h = h + 0.5 * mx[:, 8] * swiglu(norm_modulate(h, norm_ffn2[l], mx[:, 6], mx[:, 7]),
                                         w_ffn2_in[l], w_ffn2_out[l])
    return rmsnorm(h, norm_final)


import jax as _jax
import jax.numpy as _jnp

TWIN_FORMAT = 'train_step'
FWD_PARAMS = ['x', 'c', 'ctx', 'c_ctx', 'w_mod', 'b_mod', 'norm_ffn1', 'w_ffn1_in', 'w_ffn1_out', 'norm_mix', 'w_in', 'w_pool', 'pool_scale', 'sink', 'w_out', 'norm_ffn2', 'w_ffn2_in', 'w_ffn2_out', 'norm_final']
TWIN_WEIGHTS = ['c_ctx', 'w_mod', 'b_mod', 'norm_ffn1', 'w_ffn1_in', 'w_ffn1_out', 'norm_mix', 'w_in', 'w_pool', 'pool_scale', 'sink', 'w_out', 'norm_ffn2', 'w_ffn2_in', 'w_ffn2_out', 'norm_final']
TWIN_DIFF_INPUT = 'x'
TWIN_INPUTS = ['x', 'c', 'ctx', 'c_ctx', 'w_mod', 'b_mod', 'norm_ffn1', 'w_ffn1_in', 'w_ffn1_out', 'norm_mix', 'w_in', 'w_pool', 'pool_scale', 'sink', 'w_out', 'norm_ffn2', 'w_ffn2_in', 'w_ffn2_out', 'norm_final', 'loss_target', 'm_c_ctx', 'm_w_mod', 'm_b_mod', 'm_norm_ffn1', 'm_w_ffn1_in', 'm_w_ffn1_out', 'm_norm_mix', 'm_w_in', 'm_w_pool', 'm_pool_scale', 'm_sink', 'm_w_out', 'm_norm_ffn2', 'm_w_ffn2_in', 'm_w_ffn2_out', 'm_norm_final', 'v_c_ctx', 'v_w_mod', 'v_b_mod', 'v_norm_ffn1', 'v_w_ffn1_in', 'v_w_ffn1_out', 'v_norm_mix', 'v_w_in', 'v_w_pool', 'v_pool_scale', 'v_sink', 'v_w_out', 'v_norm_ffn2', 'v_w_ffn2_in', 'v_w_ffn2_out', 'v_norm_final']
TWIN_OUTPUTS = ['loss', 'grad_x', 'grad_c_ctx', 'grad_w_mod', 'grad_b_mod', 'grad_norm_ffn1', 'grad_w_ffn1_in', 'grad_w_ffn1_out', 'grad_norm_mix', 'grad_w_in', 'grad_w_pool', 'grad_pool_scale', 'grad_sink', 'grad_w_out', 'grad_norm_ffn2', 'grad_w_ffn2_in', 'grad_w_ffn2_out', 'grad_norm_final', 'delta_c_ctx', 'delta_w_mod', 'delta_b_mod', 'delta_norm_ffn1', 'delta_w_ffn1_in', 'delta_w_ffn1_out', 'delta_norm_mix', 'delta_w_in', 'delta_w_pool', 'delta_pool_scale', 'delta_sink', 'delta_w_out', 'delta_norm_ffn2', 'delta_w_ffn2_in', 'delta_w_ffn2_out', 'delta_norm_final', 'new_m_c_ctx', 'new_m_w_mod', 'new_m_b_mod', 'new_m_norm_ffn1', 'new_m_w_ffn1_in', 'new_m_w_ffn1_out', 'new_m_norm_mix', 'new_m_w_in', 'new_m_w_pool', 'new_m_pool_scale', 'new_m_sink', 'new_m_w_out', 'new_m_norm_ffn2', 'new_m_w_ffn2_in', 'new_m_w_ffn2_out', 'new_m_norm_final', 'new_v_c_ctx', 'new_v_w_mod', 'new_v_b_mod', 'new_v_norm_ffn1', 'new_v_w_ffn1_in', 'new_v_w_ffn1_out', 'new_v_norm_mix', 'new_v_w_in', 'new_v_w_pool', 'new_v_pool_scale', 'new_v_sink', 'new_v_w_out', 'new_v_norm_ffn2', 'new_v_w_ffn2_in', 'new_v_w_ffn2_out', 'new_v_norm_final']
TWIN_LEAF_KINDS = {'loss': 'loss', 'grad_x': 'grad_x', 'grad_c_ctx': 'grad_w', 'grad_w_mod': 'grad_w', 'grad_b_mod': 'grad_w', 'grad_norm_ffn1': 'grad_w', 'grad_w_ffn1_in': 'grad_w', 'grad_w_ffn1_out': 'grad_w', 'grad_norm_mix': 'grad_w', 'grad_w_in': 'grad_w', 'grad_w_pool': 'grad_w', 'grad_pool_scale': 'grad_w', 'grad_sink': 'grad_w', 'grad_w_out': 'grad_w', 'grad_norm_ffn2': 'grad_w', 'grad_w_ffn2_in': 'grad_w', 'grad_w_ffn2_out': 'grad_w', 'grad_norm_final': 'grad_w', 'delta_c_ctx': 'delta_w', 'delta_w_mod': 'delta_w', 'delta_b_mod': 'delta_w', 'delta_norm_ffn1': 'delta_w', 'delta_w_ffn1_in': 'delta_w', 'delta_w_ffn1_out': 'delta_w', 'delta_norm_mix': 'delta_w', 'delta_w_in': 'delta_w', 'delta_w_pool': 'delta_w', 'delta_pool_scale': 'delta_w', 'delta_sink': 'delta_w', 'delta_w_out': 'delta_w', 'delta_norm_ffn2': 'delta_w', 'delta_w_ffn2_in': 'delta_w', 'delta_w_ffn2_out': 'delta_w', 'delta_norm_final': 'delta_w', 'new_m_c_ctx': 'new_m', 'new_m_w_mod': 'new_m', 'new_m_b_mod': 'new_m', 'new_m_norm_ffn1': 'new_m', 'new_m_w_ffn1_in': 'new_m', 'new_m_w_ffn1_out': 'new_m', 'new_m_norm_mix': 'new_m', 'new_m_w_in': 'new_m', 'new_m_w_pool': 'new_m', 'new_m_pool_scale': 'new_m', 'new_m_sink': 'new_m', 'new_m_w_out': 'new_m', 'new_m_norm_ffn2': 'new_m', 'new_m_w_ffn2_in': 'new_m', 'new_m_w_ffn2_out': 'new_m', 'new_m_norm_final': 'new_m', 'new_v_c_ctx': 'new_v', 'new_v_w_mod': 'new_v', 'new_v_b_mod': 'new_v', 'new_v_norm_ffn1': 'new_v', 'new_v_w_ffn1_in': 'new_v', 'new_v_w_ffn1_out': 'new_v', 'new_v_norm_mix': 'new_v', 'new_v_w_in': 'new_v', 'new_v_w_pool': 'new_v', 'new_v_pool_scale': 'new_v', 'new_v_sink': 'new_v', 'new_v_w_out': 'new_v', 'new_v_norm_ffn2': 'new_v', 'new_v_w_ffn2_in': 'new_v', 'new_v_w_ffn2_out': 'new_v', 'new_v_norm_final': 'new_v'}


def _forward(args):
    return _fwd_reference(*[args[k] for k in FWD_PARAMS])


def _output_shape():
    def fwd():
        inp = _fwd_setup_inputs(0)
        return _fwd_reference(*[inp[k] for k in FWD_PARAMS])
    out = _jax.eval_shape(fwd)
    return out.shape, out.dtype

N_MICROBATCH = 1
ADAM_LR = 0.001
ADAM_B1 = 0.9
ADAM_B2 = 0.999
ADAM_EPS = 1e-08
ADAM_WD = 0.01
ADAM_STEP = 10
PER_EXAMPLE_BATCH_AXIS = {'x': 0, 'c': 0, 'ctx': 0, 'loss_target': 0}
SHARED_INPUTS = []
_WEIGHT_DTYPES = {'c_ctx': _jnp.float32, 'w_mod': _jnp.float32, 'b_mod': _jnp.float32, 'norm_ffn1': _jnp.float32, 'w_ffn1_in': _jnp.float32, 'w_ffn1_out': _jnp.float32, 'norm_mix': _jnp.float32, 'w_in': _jnp.float32, 'w_pool': _jnp.float32, 'pool_scale': _jnp.float32, 'sink': _jnp.float32, 'w_out': _jnp.float32, 'norm_ffn2': _jnp.float32, 'w_ffn2_in': _jnp.float32, 'w_ffn2_out': _jnp.float32, 'norm_final': _jnp.float32}
MOMENT_SCALE = {'c_ctx': 1.090555e-01, 'w_mod': 1.128156e-01, 'b_mod': 2.468637e-01, 'norm_ffn1': 7.083061e-02, 'w_ffn1_in': 3.694271e-02, 'w_ffn1_out': 6.217133e-02, 'norm_mix': 8.390151e-02, 'w_in': 1.092532e-01, 'w_pool': 1.071329e-01, 'pool_scale': 1.107612e-01, 'sink': 2.974955e-03, 'w_out': 1.200196e-01, 'norm_ffn2': 6.314845e-02, 'w_ffn2_in': 3.203753e-02, 'w_ffn2_out': 5.459690e-02, 'norm_final': 3.235419e+01}


def _to_microbatches(a, axis):
    t = _jnp.moveaxis(a, axis, 0)
    t = t.reshape((N_MICROBATCH, t.shape[0] // N_MICROBATCH) + t.shape[1:])
    return _jnp.moveaxis(t, 1, axis + 1)


def setup_inputs(seed: int = 0) -> dict:
    inp = _fwd_setup_inputs(seed)
    key = _jax.random.fold_in(_jax.random.key(seed), 7919)
    shape, _ = _output_shape()
    out = dict(inp)
    out["loss_target"] = _jax.random.normal(_jax.random.fold_in(key, 0), shape, _jnp.float32)
    for i, name in enumerate(TWIN_WEIGHTS):
        w = inp[name].astype(_jnp.float32)
        if MOMENT_SCALE is None:
            s = _jnp.sqrt(_jnp.mean(_jnp.square(w)) + 1e-30)
        else:
            s = MOMENT_SCALE[name]
        km, kv = _jax.random.split(_jax.random.fold_in(key, i + 1))
        out[name] = w
        out["m_" + name] = s * _jax.random.normal(km, w.shape, _jnp.float32)
        out["v_" + name] = (s * s) * _jax.random.uniform(kv, w.shape, _jnp.float32, 0.5, 1.5)
    if N_MICROBATCH > 1:
        for name, axis in PER_EXAMPLE_BATCH_AXIS.items():
            out[name] = _to_microbatches(out[name], axis)
    return {'x': out['x'], 'c': out['c'], 'ctx': out['ctx'], 'c_ctx': out['c_ctx'], 'w_mod': out['w_mod'], 'b_mod': out['b_mod'], 'norm_ffn1': out['norm_ffn1'], 'w_ffn1_in': out['w_ffn1_in'], 'w_ffn1_out': out['w_ffn1_out'], 'norm_mix': out['norm_mix'], 'w_in': out['w_in'], 'w_pool': out['w_pool'], 'pool_scale': out['pool_scale'], 'sink': out['sink'], 'w_out': out['w_out'], 'norm_ffn2': out['norm_ffn2'], 'w_ffn2_in': out['w_ffn2_in'], 'w_ffn2_out': out['w_ffn2_out'], 'norm_final': out['norm_final'], 'loss_target': out['loss_target'], 'm_c_ctx': out['m_c_ctx'], 'm_w_mod': out['m_w_mod'], 'm_b_mod': out['m_b_mod'], 'm_norm_ffn1': out['m_norm_ffn1'], 'm_w_ffn1_in': out['m_w_ffn1_in'], 'm_w_ffn1_out': out['m_w_ffn1_out'], 'm_norm_mix': out['m_norm_mix'], 'm_w_in': out['m_w_in'], 'm_w_pool': out['m_w_pool'], 'm_pool_scale': out['m_pool_scale'], 'm_sink': out['m_sink'], 'm_w_out': out['m_w_out'], 'm_norm_ffn2': out['m_norm_ffn2'], 'm_w_ffn2_in': out['m_w_ffn2_in'], 'm_w_ffn2_out': out['m_w_ffn2_out'], 'm_norm_final': out['m_norm_final'], 'v_c_ctx': out['v_c_ctx'], 'v_w_mod': out['v_w_mod'], 'v_b_mod': out['v_b_mod'], 'v_norm_ffn1': out['v_norm_ffn1'], 'v_w_ffn1_in': out['v_w_ffn1_in'], 'v_w_ffn1_out': out['v_w_ffn1_out'], 'v_norm_mix': out['v_norm_mix'], 'v_w_in': out['v_w_in'], 'v_w_pool': out['v_w_pool'], 'v_pool_scale': out['v_pool_scale'], 'v_sink': out['v_sink'], 'v_w_out': out['v_w_out'], 'v_norm_ffn2': out['v_norm_ffn2'], 'v_w_ffn2_in': out['v_w_ffn2_in'], 'v_w_ffn2_out': out['v_w_ffn2_out'], 'v_norm_final': out['v_norm_final']}


def _loss(weights, diff, rest, loss_target):
    with _jax.named_scope("forward"):
        args = {**rest, TWIN_DIFF_INPUT: diff, **{k: w.astype(_WEIGHT_DTYPES[k]) for k, w in weights.items()}}
        y = _forward(args)
    with _jax.named_scope("loss_head"):
        err = _jnp.square(y.astype(_jnp.float32) - loss_target)
        return 0.5 * _jnp.sum(_jnp.mean(err, axis=-1)) if err.ndim else 0.5 * err


def _adamw(w, g, m, v):
    m = ADAM_B1 * m + (1.0 - ADAM_B1) * g
    v = ADAM_B2 * v + (1.0 - ADAM_B2) * _jnp.square(g)
    m_hat = m / (1.0 - ADAM_B1 ** ADAM_STEP)
    v_hat = v / (1.0 - ADAM_B2 ** ADAM_STEP)
    delta = -ADAM_LR * (m_hat / (_jnp.sqrt(v_hat) + ADAM_EPS) + ADAM_WD * w)
    return delta, m, v


def reference(x, c, ctx, c_ctx, w_mod, b_mod, norm_ffn1, w_ffn1_in, w_ffn1_out, norm_mix, w_in, w_pool, pool_scale, sink, w_out, norm_ffn2, w_ffn2_in, w_ffn2_out, norm_final, loss_target, m_c_ctx, m_w_mod, m_b_mod, m_norm_ffn1, m_w_ffn1_in, m_w_ffn1_out, m_norm_mix, m_w_in, m_w_pool, m_pool_scale, m_sink, m_w_out, m_norm_ffn2, m_w_ffn2_in, m_w_ffn2_out, m_norm_final, v_c_ctx, v_w_mod, v_b_mod, v_norm_ffn1, v_w_ffn1_in, v_w_ffn1_out, v_norm_mix, v_w_in, v_w_pool, v_pool_scale, v_sink, v_w_out, v_norm_ffn2, v_w_ffn2_in, v_w_ffn2_out, v_norm_final):
    given = dict(x=x, c=c, ctx=ctx, c_ctx=c_ctx, w_mod=w_mod, b_mod=b_mod, norm_ffn1=norm_ffn1, w_ffn1_in=w_ffn1_in, w_ffn1_out=w_ffn1_out, norm_mix=norm_mix, w_in=w_in, w_pool=w_pool, pool_scale=pool_scale, sink=sink, w_out=w_out, norm_ffn2=norm_ffn2, w_ffn2_in=w_ffn2_in, w_ffn2_out=w_ffn2_out, norm_final=norm_final, loss_target=loss_target, m_c_ctx=m_c_ctx, m_w_mod=m_w_mod, m_b_mod=m_b_mod, m_norm_ffn1=m_norm_ffn1, m_w_ffn1_in=m_w_ffn1_in, m_w_ffn1_out=m_w_ffn1_out, m_norm_mix=m_norm_mix, m_w_in=m_w_in, m_w_pool=m_w_pool, m_pool_scale=m_pool_scale, m_sink=m_sink, m_w_out=m_w_out, m_norm_ffn2=m_norm_ffn2, m_w_ffn2_in=m_w_ffn2_in, m_w_ffn2_out=m_w_ffn2_out, m_norm_final=m_norm_final, v_c_ctx=v_c_ctx, v_w_mod=v_w_mod, v_b_mod=v_b_mod, v_norm_ffn1=v_norm_ffn1, v_w_ffn1_in=v_w_ffn1_in, v_w_ffn1_out=v_w_ffn1_out, v_norm_mix=v_norm_mix, v_w_in=v_w_in, v_w_pool=v_w_pool, v_pool_scale=v_pool_scale, v_sink=v_sink, v_w_out=v_w_out, v_norm_ffn2=v_norm_ffn2, v_w_ffn2_in=v_w_ffn2_in, v_w_ffn2_out=v_w_ffn2_out, v_norm_final=v_norm_final)
    weights = {n: given[n] for n in TWIN_WEIGHTS}
    shared = {n: given[n] for n in SHARED_INPUTS}
    per_example = {n: given[n] for n in ['x', 'c', 'ctx']}
    grad_fn = _jax.value_and_grad(_loss, argnums=(0, 1))

    def one_microbatch(ex, loss_target):
        ex = dict(ex)
        diff = ex.pop(TWIN_DIFF_INPUT)
        return grad_fn(weights, diff, {**shared, **ex}, loss_target)

    if N_MICROBATCH == 1:
        loss, (grad_w, grad_x) = one_microbatch(per_example, given["loss_target"])
    else:
        def body(carry, xs):
            loss_sum, grad_sum = carry
            l_k, (gw_k, gx_k) = one_microbatch(xs[0], xs[1])
            with _jax.named_scope("update"):
                return (loss_sum + l_k, _jax.tree.map(_jnp.add, grad_sum, gw_k)), gx_k

        init = (_jnp.zeros((), _jnp.float32), _jax.tree.map(_jnp.zeros_like, weights))
        (loss, grad_w), grad_x = _jax.lax.scan(body, init, (per_example, given["loss_target"]))
    with _jax.named_scope("update"):
        delta_w, new_m, new_v = {}, {}, {}
        for n in TWIN_WEIGHTS:
            delta_w[n], new_m[n], new_v[n] = _adamw(weights[n], grad_w[n], given["m_" + n], given["v_" + n])
    return (loss, grad_x, *[grad_w[n] for n in TWIN_WEIGHTS], *[delta_w[n] for n in TWIN_WEIGHTS],
            *[new_m[n] for n in TWIN_WEIGHTS], *[new_v[n] for n in TWIN_WEIGHTS])
```

```python
import jax
import jax.numpy as jnp
from jax import lax
from jax.experimental import pallas as pl
from jax.experimental.pallas import tpu as pltpu

F32, BF16 = jnp.float32, jnp.bfloat16
D = 1024
D_FF = 2816
N_SLOT = 4
FF_COLS = 2 * D_FF // N_SLOT
N_MOD = 9
MOD_COLS = N_MOD * D // N_SLOT
POOL_W, ATTN_W, KV_W = 512, 512, 128
PROJ_W = POOL_W + ATTN_W + 2 * KV_W
N_HEADS, Q_GROUP, HEAD = 8, 4, 64
GROUP = 128
POOL_WINDOWS = (2, 4, 8, 16)
BLK = 128
GRID_W = 64
ROPE_BASE = 10000.0
EPS = 1e-6
NEG_INF = -1e30
TM = 256
N_DEV = 8
VMEM_LIMIT_BYTES = 56 * 1024 * 1024
ADAM_LR, ADAM_B1, ADAM_B2, ADAM_EPS, ADAM_WD, ADAM_STEP = 0.001, 0.9, 0.999, 1e-08, 0.01, 10
MESH = pl.DeviceIdType.MESH
NT = (((1,), (1,)), ((), ()))
TN = (((0,), (0,)), ((), ()))


def _params(*sem):
    return pltpu.CompilerParams(dimension_semantics=sem, vmem_limit_bytes=VMEM_LIMIT_BYTES)


def _whole(shape, lead=()):
    idx = tuple(lead) + (0,) * len(shape)
    return pl.BlockSpec((None,) * len(lead) + tuple(shape), lambda *_: idx, pipeline_mode=pl.Buffered(1))


def _rows(cols, tm=TM):
    return pl.BlockSpec((tm, cols), lambda i: (i, 0))


def _mods_spec(layer, n_lat):
    return pl.BlockSpec((None, None, 16, D), lambda i: (layer, (i >= n_lat).astype(jnp.int32), 0, 0))


def _acc_spec(n_lat):
    return pl.BlockSpec((None, 8, D), lambda i: ((i >= n_lat).astype(jnp.int32), 0, 0))


def _dot(a, b):
    return jnp.dot(a, b, preferred_element_type=F32)


def _dotg(a, b, dims):
    return lax.dot_general(a, b, dims, preferred_element_type=F32)


def _sum0(v):
    return jnp.sum(v, axis=0, keepdims=True)


def _norm_mod(h, g, shift, scale):
    r = lax.rsqrt(jnp.mean(h * h, axis=-1, keepdims=True) + EPS)
    xhat = h * r
    y = xhat * g
    return y * (1 + scale) + shift, xhat, r, y


def _norm_mod_bwd(dn, xhat, r, y, g, scale):
    dy = dn * (1 + scale)
    dx = dy * g
    dh = r * (dx - xhat * jnp.mean(dx * xhat, axis=-1, keepdims=True))
    return _sum0(dn), _sum0(dn * y), _sum0(dy * xhat), dh


def _swap_halves(v):
    w = v.shape[1]
    lane = lax.broadcasted_iota(jnp.int32, v.shape, 1)
    return jnp.where(lane % HEAD < HEAD // 2, pltpu.roll(v, w - HEAD // 2, axis=1), pltpu.roll(v, HEAD // 2, axis=1))


def _tile_lanes(t, width):
    return t if width == t.shape[1] else jnp.concatenate([t] * (width // t.shape[1]), axis=1)


def _rope(v, cos, sin):
    return v * _tile_lanes(cos, v.shape[1]) + _swap_halves(v) * _tile_lanes(sin, v.shape[1])


def _unrope(g, cos, sin):
    return g * _tile_lanes(cos, g.shape[1]) + _swap_halves(g * _tile_lanes(sin, g.shape[1]))


def ffn_fwd(h, mods, g, w4, wo, layer, k0, n_lat, name):
    s = h.shape[0]

    def body(h_ref, m_ref, g_ref, w_ref, wo_ref, ho_ref, ab_ref, f_ref):
        hh = h_ref[...]
        n, _, _, _ = _norm_mod(hh, g_ref[...], m_ref[k0:k0 + 1, :], m_ref[k0 + 1:k0 + 2, :])
        nb = n.astype(BF16)
        acc = jnp.zeros((TM, D), F32)
        for j in range(2):
            a = _dot(nb, w_ref[j])
            b = _dot(nb, w_ref[2 + j])
            ab_ref[:, j * FF_COLS:(j + 1) * FF_COLS] = a.astype(BF16)
            ab_ref[:, (2 + j) * FF_COLS:(3 + j) * FF_COLS] = b.astype(BF16)
            act = (a * jax.nn.sigmoid(a) * b).astype(BF16)
            acc = acc + _dot(act, wo_ref[j * FF_COLS:(j + 1) * FF_COLS, :])
        f_ref[...] = acc
        ho_ref[...] = hh + 0.5 * m_ref[k0 + 2:k0 + 3, :] * acc

    return pl.pallas_call(
        body, name=name, grid=(s // TM,),
        in_specs=[_rows(D), _mods_spec(layer, n_lat), _whole((1, D), (layer,)),
                  _whole((N_SLOT, D, FF_COLS), (layer,)), _whole((D_FF, D), (layer,))],
        out_specs=[_rows(D), _rows(2 * D_FF), _rows(D)],
        out_shape=[jax.ShapeDtypeStruct((s, D), F32), jax.ShapeDtypeStruct((s, 2 * D_FF), BF16),
                   jax.ShapeDtypeStruct((s, D), F32)],
        compiler_params=_params("parallel"),
    )(h, mods, g, w4, wo)


def ffn_bwd(h, ab, f, dh, mods, g, w4, wo, layer, k0, n_lat, name):
    s = h.shape[0]

    def body(h_ref, ab_ref, f_ref, dh_ref, m_ref, g_ref, w_ref, wo_ref, dhi_ref, dab_ref, df_ref, n_ref, act_ref, dm_ref):
        i = pl.program_id(0)

        @pl.when((i == 0) | (i == n_lat))
        def _():
            dm_ref[...] = jnp.zeros_like(dm_ref)

        hh, dho, gg = h_ref[...], dh_ref[...], g_ref[...]
        scale, gate = m_ref[k0 + 1:k0 + 2, :], m_ref[k0 + 2:k0 + 3, :]
        n, xhat, r, y = _norm_mod(hh, gg, m_ref[k0:k0 + 1, :], scale)
        n_ref[...] = n.astype(BF16)
        dgate = _sum0(dho * (0.5 * f_ref[...]))
        dfb = ((0.5 * gate) * dho).astype(BF16)
        df_ref[...] = dfb
        dn = jnp.zeros((TM, D), F32)
        for j in range(2):
            a = ab_ref[:, j * FF_COLS:(j + 1) * FF_COLS].astype(F32)
            b = ab_ref[:, (2 + j) * FF_COLS:(3 + j) * FF_COLS].astype(F32)
            sg = jax.nn.sigmoid(a)
            sa = a * sg
            act_ref[:, j * FF_COLS:(j + 1) * FF_COLS] = (sa * b).astype(BF16)
            dact = _dotg(dfb, wo_ref[j * FF_COLS:(j + 1) * FF_COLS, :], NT)
            da = (dact * b * (sg * (1 + a * (1 - sg)))).astype(BF16)
            db = (dact * sa).astype(BF16)
            dab_ref[:, j * FF_COLS:(j + 1) * FF_COLS] = da
            dab_ref[:, (2 + j) * FF_COLS:(3 + j) * FF_COLS] = db
            dn = dn + _dotg(da, w_ref[j], NT) + _dotg(db, w_ref[2 + j], NT)
        dsh, dsc, dg, dhn = _norm_mod_bwd(dn, xhat, r, y, gg, scale)
        dhi_ref[...] = dho + dhn
        dm_ref[0:1, :] += dsh
        dm_ref[1:2, :] += dsc
        dm_ref[2:3, :] += dgate
        dm_ref[3:4, :] += dg

    return pl.pallas_call(
        body, name=name, grid=(s // TM,),
        in_specs=[_rows(D), _rows(2 * D_FF), _rows(D), _rows(D), _mods_spec(layer, n_lat), _whole((1, D), (layer,)),
                  _whole((N_SLOT, D, FF_COLS), (layer,)), _whole((D_FF, D), (layer,))],
        out_specs=[_rows(D), _rows(2 * D_FF), _rows(D), _rows(D), _rows(D_FF), _acc_spec(n_lat)],
        out_shape=[jax.ShapeDtypeStruct((s, D), F32), jax.ShapeDtypeStruct((s, 2 * D_FF), BF16),
                   jax.ShapeDtypeStruct((s, D), BF16), jax.ShapeDtypeStruct((s, D), BF16),
                   jax.ShapeDtypeStruct((s, D_FF), BF16), jax.ShapeDtypeStruct((2, 8, D), F32)],
        compiler_params=_params("arbitrary"),
    )(h, ab, f, dh, mods, g, w4, wo)


def _token_tile(s, limit=1088):
    return max(ts for ts in range(16, limit + 1, 16) if s % ts == 0)


def wgrad(a, b, tk, tn, slot_cols, name):
    s, k = a.shape
    n = b.shape[1]
    ts = _token_tile(s)
    steps = s // ts

    def body(a_ref, b_ref, o_ref, o16_ref):
        r = _dotg(a_ref[...], b_ref[...], TN)
        si = pl.program_id(2)

        @pl.when(si == 0)
        def _():
            o_ref[...] = r

        @pl.when(si > 0)
        def _():
            o_ref[...] += r

        @pl.when(si == steps - 1)
        def _():
            o16_ref[...] = o_ref[...].astype(BF16)

    if slot_cols is None:
        shape, spec = (k, n), pl.BlockSpec((tk, tn), lambda i, j, si: (i, j))
    else:
        per = slot_cols // tn
        shape, spec = (n // slot_cols, k, slot_cols), pl.BlockSpec((None, tk, tn), lambda i, j, si: (lax.div(j, per), i, lax.rem(j, per)))
    return pl.pallas_call(
        body, name=name, grid=(k // tk, n // tn, steps),
        in_specs=[pl.BlockSpec((ts, tk), lambda i, j, si: (si, i)), pl.BlockSpec((ts, tn), lambda i, j, si: (si, j))],
        out_specs=[spec, spec],
        out_shape=[jax.ShapeDtypeStruct(shape, F32), jax.ShapeDtypeStruct(shape, BF16)],
        compiler_params=_params("parallel", "parallel", "arbitrary"),
    )(a, b)


def proj_fwd(h, mods, g, w_in, cos, sin, layer, n_lat, name):
    s = h.shape[0]

    def body(h_ref, m_ref, g_ref, w_ref, cos_ref, sin_ref, u_ref, q_ref, k_ref, v_ref):
        n, _, _, _ = _norm_mod(h_ref[...], g_ref[...], m_ref[3:4, :], m_ref[4:5, :])
        p = _dot(n.astype(BF16), w_ref[...])
        cs, sn = cos_ref[...], sin_ref[...]
        u_ref[...] = p[:, :POOL_W]
        q_ref[...] = (_rope(p[:, POOL_W:POOL_W + ATTN_W], cs, sn) * HEAD ** -0.5).astype(BF16)
        k_ref[...] = _rope(p[:, POOL_W + ATTN_W:POOL_W + ATTN_W + KV_W], cs, sn).astype(BF16)
        v_ref[...] = p[:, POOL_W + ATTN_W + KV_W:].astype(BF16)

    return pl.pallas_call(
        body, name=name, grid=(s // TM,),
        in_specs=[_rows(D), _mods_spec(layer, n_lat), _whole((1, D), (layer,)), _whole((D, PROJ_W), (layer,)),
                  _rows(BLK), _rows(BLK)],
        out_specs=[_rows(POOL_W), _rows(ATTN_W), _rows(KV_W), _rows(KV_W)],
        out_shape=[jax.ShapeDtypeStruct((s, POOL_W), F32), jax.ShapeDtypeStruct((s, ATTN_W), BF16),
                   jax.ShapeDtypeStruct((s, KV_W), BF16), jax.ShapeDtypeStruct((s, KV_W), BF16)],
        compiler_params=_params("parallel"),
    )(h, mods, g, w_in, cos, sin)


def proj_bwd(h, du, dq, dk, dv, dh, mods, g, w_in, cos, sin, layer, n_lat, name):
    s = h.shape[0]

    def body(h_ref, du_ref, dq_ref, dk_ref, dv_ref, dh_ref, m_ref, g_ref, w_ref, cos_ref, sin_ref,
             dhi_ref, dp_ref, n_ref, dm_ref):
        i = pl.program_id(0)

        @pl.when((i == 0) | (i == n_lat))
        def _():
            dm_ref[...] = jnp.zeros_like(dm_ref)

        gg, scale = g_ref[...], m_ref[4:5, :]
        n, xhat, r, y = _norm_mod(h_ref[...], gg, m_ref[3:4, :], scale)
        n_ref[...] = n.astype(BF16)
        cs, sn = cos_ref[...], sin_ref[...]
        dp = jnp.concatenate([du_ref[...], _unrope(dq_ref[...], cs, sn) * HEAD ** -0.5, _unrope(dk_ref[...], cs, sn),
                              dv_ref[...]], axis=1).astype(BF16)
        dp_ref[...] = dp
        dsh, dsc, dg, dhn = _norm_mod_bwd(_dotg(dp, w_ref[...], NT), xhat, r, y, gg, scale)
        dhi_ref[...] = dh_ref[...] + dhn
        dm_ref[0:1, :] += dsh
        dm_ref[1:2, :] += dsc
        dm_ref[3:4, :] += dg

    return pl.pallas_call(
        body, name=name, grid=(s // TM,),
        in_specs=[_rows(D), _rows(POOL_W), _rows(ATTN_W), _rows(KV_W), _rows(KV_W), _rows(D), _mods_spec(layer, n_lat),
                  _whole((1, D), (layer,)), _whole((D, PROJ_W), (layer,)), _rows(BLK), _rows(BLK)],
        out_specs=[_rows(D), _rows(PROJ_W), _rows(D), _acc_spec(n_lat)],
        out_shape=[jax.ShapeDtypeStruct((s, D), F32), jax.ShapeDtypeStruct((s, PROJ_W), BF16),
                   jax.ShapeDtypeStruct((s, D), BF16), jax.ShapeDtypeStruct((2, 8, D), F32)],
        compiler_params=_params("arbitrary"),
    )(h, du, dq, dk, dv, dh, mods, g, w_in, cos, sin)


def _window(i, n_lat_blk):
    return pl.multiple_of(jnp.clip(i - 1, 0, n_lat_blk - 1) * BLK, BLK)


def _pool_band(i, ws, w, seq_lo, seq_hi, transposed):
    shape = (3 * BLK, BLK) if transposed else (BLK, 3 * BLK)
    q = i * BLK + lax.broadcasted_iota(jnp.int32, shape, 1 if transposed else 0)
    k = ws + lax.broadcasted_iota(jnp.int32, shape, 0 if transposed else 1)
    band = (k >= jnp.maximum(q - w // 2, seq_lo)) & (k < jnp.minimum(q + w - w // 2, seq_hi))
    qc = i * BLK + lax.broadcasted_iota(jnp.int32, (BLK, 1), 0)
    cnt = jnp.minimum(qc + w - w // 2, seq_hi) - jnp.maximum(qc - w // 2, seq_lo)
    return jnp.where(band, 1.0, 0.0).astype(BF16), cnt.astype(F32)


def _split_dot(band, v):
    hi = v.astype(BF16)
    return _dot(band, hi) + _dot(band, (v - hi.astype(F32)).astype(BF16))


def _pooled(u_ref, i, ws, seq_lo, seq_hi, gi):
    band, cnt = _pool_band(i, ws, POOL_WINDOWS[gi], seq_lo, seq_hi, False)
    cols = slice(gi * GROUP, (gi + 1) * GROUP)
    mean = _split_dot(band, u_ref[pl.ds(ws, 3 * BLK), cols]) / cnt
    return mean - u_ref[pl.ds(pl.multiple_of(i * BLK, BLK), BLK), cols]


def _local_valid(i, ws, t):
    q = i * BLK + lax.broadcasted_iota(jnp.int32, (BLK, 3 * BLK), 0)
    k = ws + lax.broadcasted_iota(jnp.int32, (BLK, 3 * BLK), 1)
    return (i * BLK < t) & (k < t) & (jnp.abs(k - q) <= BLK)


def _head_cols(hd):
    return slice(hd * HEAD, (hd + 1) * HEAD)


def _lane_place(cols, width=BLK):
    lane = lax.broadcasted_iota(jnp.int32, (cols[0].shape[0], width), 1)
    out = jnp.zeros((cols[0].shape[0], width), F32)
    for hd, c in enumerate(cols):
        out = jnp.where(lane == hd, c, out)
    return out


def mix_fwd(h, q, k, v, u, w_pool, pool_scale, sink, w_out, mods, layer, t, name):
    s = h.shape[0]
    n_lat_blk = t // BLK

    def body(h_ref, q_ref, k_ref, v_ref, u_ref, wp_ref, ps_ref, sink_ref, wo_ref, m_ref, ho_ref, cat_ref, lse_ref):
        i = pl.program_id(0)
        ws = _window(i, n_lat_blk)
        is_lat = i < n_lat_blk
        seq_lo, seq_hi = jnp.where(is_lat, 0, t), jnp.where(is_lat, t, s)
        for gi in range(len(POOL_WINDOWS)):
            mixed = _dot(_pooled(u_ref, i, ws, seq_lo, seq_hi, gi).astype(BF16), wp_ref[gi])
            cat_ref[:, gi * GROUP:(gi + 1) * GROUP] = (mixed * ps_ref[:, gi * GROUP:(gi + 1) * GROUP]).astype(BF16)
        valid = _local_valid(i, ws, t)
        kw, vw = k_ref[pl.ds(ws, 3 * BLK), :], v_ref[pl.ds(ws, 3 * BLK), :]
        kc, vc = k_ref[t:s, :], v_ref[t:s, :]
        lses = []
        for hd in range(N_HEADS):
            kv = _head_cols(hd // Q_GROUP)
            qh = q_ref[:, _head_cols(hd)]
            sl = jnp.where(valid, _dotg(qh, kw[:, kv], NT), NEG_INF)
            sc = _dotg(qh, kc[:, kv], NT)
            sk = sink_ref[layer, hd]
            m = jnp.maximum(jnp.maximum(jnp.max(sl, axis=1, keepdims=True), jnp.max(sc, axis=1, keepdims=True)), sk)
            el, ec = jnp.exp(sl - m), jnp.exp(sc - m)
            l = jnp.sum(el, axis=1, keepdims=True) + jnp.sum(ec, axis=1, keepdims=True) + jnp.exp(sk - m)
            inv = 1.0 / l
            o = _dot((el * inv).astype(BF16), vw[:, kv]) + _dot((ec * inv).astype(BF16), vc[:, kv])
            cat_ref[:, POOL_W + hd * HEAD:POOL_W + (hd + 1) * HEAD] = o.astype(BF16)
            lses.append(m + jnp.log(l))
        lse_ref[...] = _lane_place(lses)
        ho_ref[...] = h_ref[...] + m_ref[5:6, :] * _dot(cat_ref[...], wo_ref[...])

    blk = lambda cols: _rows(cols, BLK)
    return pl.pallas_call(
        body, name=name, grid=(s // BLK,),
        in_specs=[blk(D), blk(ATTN_W), _whole((s, KV_W)), _whole((s, KV_W)), _whole((s, POOL_W)),
                  _whole((len(POOL_WINDOWS), GROUP, GROUP), (layer,)), _whole((1, POOL_W), (layer,)),
                  pl.BlockSpec(memory_space=pltpu.SMEM), _whole((POOL_W + ATTN_W, D), (layer,)),
                  _mods_spec(layer, n_lat_blk)],
        out_specs=[blk(D), blk(POOL_W + ATTN_W), blk(BLK)],
        out_shape=[jax.ShapeDtypeStruct((s, D), F32), jax.ShapeDtypeStruct((s, POOL_W + ATTN_W), BF16),
                   jax.ShapeDtypeStruct((s, BLK), F32)],
        compiler_params=_params("parallel"),
    )(h, q, k, v, u, w_pool, pool_scale, sink, w_out, mods)


def mix_bwd(dh, cat, q, k, v, u, lse, w_pool, pool_scale, sink, w_out, mods, layer, t, name):
    s = dh.shape[0]
    n_lat_blk = t // BLK
    n_grp = len(POOL_WINDOWS)

    def body(dh_ref, cat_ref, q_ref, k_ref, v_ref, u_ref, lse_ref, wp_ref, ps_ref, sink_ref, wo_ref, m_ref,
             dq_ref, dk_ref, dv_ref, du_ref, dmo_ref, dwp_ref, dps_ref, dsink_ref, dm_ref):
        i = pl.program_id(0)

        @pl.when(i == 0)
        def _():
            for ref in (dk_ref, dv_ref, du_ref, dwp_ref, dps_ref, dsink_ref):
                ref[...] = jnp.zeros_like(ref)

        @pl.when((i == 0) | (i == n_lat_blk))
        def _():
            dm_ref[...] = jnp.zeros_like(dm_ref)

        ws = _window(i, n_lat_blk)
        here = pl.ds(pl.multiple_of(i * BLK, BLK), BLK)
        is_lat = i < n_lat_blk
        seq_lo, seq_hi = jnp.where(is_lat, 0, t), jnp.where(is_lat, t, s)
        dho = dh_ref[...]
        dm_ref[2:3, :] += _sum0(dho * _dot(cat_ref[...], wo_ref[...]))
        dmo = (m_ref[5:6, :] * dho).astype(BF16)
        dmo_ref[...] = dmo
        dcat = _dotg(dmo, wo_ref[...], NT)

        for gi in range(n_grp):
            cols = slice(gi * GROUP, (gi + 1) * GROUP)
            pooled = _pooled(u_ref, i, ws, seq_lo, seq_hi, gi).astype(BF16)
            dpo = dcat[:, cols]
            dps_ref[0:1, cols] += _sum0(dpo * _dot(pooled, wp_ref[gi]))
            dmixed = (dpo * ps_ref[:, cols]).astype(BF16)
            dwp_ref[gi] += _dotg(pooled, dmixed, TN)
            dpooled = _dotg(dmixed, wp_ref[gi], NT)
            band_t, cnt = _pool_band(i, ws, POOL_WINDOWS[gi], seq_lo, seq_hi, True)
            du_ref[pl.ds(ws, 3 * BLK), cols] += _split_dot(band_t, dpooled / cnt)
            du_ref[here, cols] -= dpooled

        valid = _local_valid(i, ws, t)
        kw, vw = k_ref[pl.ds(ws, 3 * BLK), :], v_ref[pl.ds(ws, 3 * BLK), :]
        kc, vc = k_ref[t:s, :], v_ref[t:s, :]
        dqs, dsinks = [], []
        dkw, dvw, dkc, dvc = [], [], [], []
        for hd in range(N_HEADS):
            kv = _head_cols(hd // Q_GROUP)
            qh = q_ref[:, _head_cols(hd)]
            lse_h = lse_ref[:, hd:hd + 1]
            pl_ = jnp.exp(jnp.where(valid, _dotg(qh, kw[:, kv], NT), NEG_INF) - lse_h)
            pc = jnp.exp(_dotg(qh, kc[:, kv], NT) - lse_h)
            do = dcat[:, POOL_W + hd * HEAD:POOL_W + (hd + 1) * HEAD].astype(BF16)
            dpl, dpc = _dotg(do, vw[:, kv], NT), _dotg(do, vc[:, kv], NT)
            delta = jnp.sum(pl_ * dpl, axis=1, keepdims=True) + jnp.sum(pc * dpc, axis=1, keepdims=True)
            dsl, dsc = (pl_ * (dpl - delta)).astype(BF16), (pc * (dpc - delta)).astype(BF16)
            dsinks.append(_sum0(-jnp.exp(sink_ref[layer, hd] - lse_h) * delta))
            dqs.append(_dot(dsl, kw[:, kv]) + _dot(dsc, kc[:, kv]))
            parts = (_dotg(dsl, qh, TN), _dotg(pl_.astype(BF16), do, TN), _dotg(dsc, qh, TN), _dotg(pc.astype(BF16), do, TN))
            for acc, part in zip((dkw, dvw, dkc, dvc), parts):
                if hd % Q_GROUP == 0:
                    acc.append(part)
                else:
                    acc[-1] = acc[-1] + part
        dq_ref[...] = jnp.concatenate(dqs, axis=1)
        dk_ref[pl.ds(ws, 3 * BLK), :] += jnp.concatenate(dkw, axis=1)
        dv_ref[pl.ds(ws, 3 * BLK), :] += jnp.concatenate(dvw, axis=1)
        dk_ref[t:s, :] += jnp.concatenate(dkc, axis=1)
        dv_ref[t:s, :] += jnp.concatenate(dvc, axis=1)
        dsink_ref[0:1, :] += _lane_place(dsinks)

    blk = lambda cols: _rows(cols, BLK)
    full = lambda shape: pl.BlockSpec(shape, lambda i: (0,) * len(shape))
    return pl.pallas_call(
        body, name=name, grid=(s // BLK,),
        in_specs=[blk(D), blk(POOL_W + ATTN_W), blk(ATTN_W), _whole((s, KV_W)), _whole((s, KV_W)), _whole((s, POOL_W)),
                  blk(BLK), _whole((n_grp, GROUP, GROUP), (layer,)), _whole((1, POOL_W), (layer,)),
                  pl.BlockSpec(memory_space=pltpu.SMEM), _whole((POOL_W + ATTN_W, D), (layer,)),
                  _mods_spec(layer, n_lat_blk)],
        out_specs=[blk(ATTN_W), full((s, KV_W)), full((s, KV_W)), full((s, POOL_W)), blk(D),
                   full((n_grp, GROUP, GROUP)), full((8, POOL_W)), full((8, BLK)), _acc_spec(n_lat_blk)],
        out_shape=[jax.ShapeDtypeStruct((s, ATTN_W), F32), jax.ShapeDtypeStruct((s, KV_W), F32),
                   jax.ShapeDtypeStruct((s, KV_W), F32), jax.ShapeDtypeStruct((s, POOL_W), F32),
                   jax.ShapeDtypeStruct((s, D), BF16), jax.ShapeDtypeStruct((n_grp, GROUP, GROUP), F32),
                   jax.ShapeDtypeStruct((8, POOL_W), F32), jax.ShapeDtypeStruct((8, BLK), F32),
                   jax.ShapeDtypeStruct((2, 8, D), F32)],
        compiler_params=_params("arbitrary"),
    )(dh, cat, q, k, v, u, lse, w_pool, pool_scale, sink, w_out, mods)


def loss_head(h, target, g, t, name):
    s = h.shape[0]
    n_lat = t // TM

    def body(h_ref, t_ref, g_ref, dh_ref, acc_ref):
        i = pl.program_id(0)

        @pl.when(i == 0)
        def _():
            acc_ref[...] = jnp.zeros_like(acc_ref)

        @pl.when(i < n_lat)
        def _():
            hh, gg = h_ref[...], g_ref[...]
            r = lax.rsqrt(jnp.mean(hh * hh, axis=-1, keepdims=True) + EPS)
            xhat = hh * r
            err = xhat * gg - t_ref[...]
            dy = err * (1.0 / D)
            dx = dy * gg
            dh_ref[...] = r * (dx - xhat * jnp.mean(dx * xhat, axis=-1, keepdims=True))
            acc_ref[0:1, :] += _sum0(dy * xhat)
            acc_ref[1:2, :] += _sum0(err * err)

        @pl.when(i >= n_lat)
        def _():
            dh_ref[...] = jnp.zeros_like(dh_ref)

    return pl.pallas_call(
        body, name=name, grid=(s // TM,),
        in_specs=[_rows(D), pl.BlockSpec((TM, D), lambda i: (jnp.minimum(i, n_lat - 1), 0)), _whole((1, D))],
        out_specs=[_rows(D), pl.BlockSpec((8, D), lambda i: (0, 0))],
        out_shape=[jax.ShapeDtypeStruct((s, D), F32), jax.ShapeDtypeStruct((8, D), F32)],
        compiler_params=_params("arbitrary"),
    )(h, target, g)


def mod_rows(c_all, w_mod, b_cols, name):
    def body(c_ref, w_ref, b_ref, o_ref):
        cc = c_ref[...]
        o_ref[...] = _dot((cc * jax.nn.sigmoid(cc)).astype(BF16), w_ref[...].astype(BF16)) + b_ref[...]

    return pl.pallas_call(
        body, name=name, grid=(2,),
        in_specs=[pl.BlockSpec((16, D), lambda l: (0, 0)), pl.BlockSpec((None, D, MOD_COLS), lambda l: (l, 0, 0)),
                  pl.BlockSpec((None, 1, MOD_COLS), lambda l: (l, 0, 0))],
        out_specs=pl.BlockSpec((None, 16, MOD_COLS), lambda l: (l, 0, 0)),
        out_shape=jax.ShapeDtypeStruct((2, 16, MOD_COLS), F32),
        compiler_params=_params("parallel"),
    )(c_all, w_mod, b_cols)


def mod_grads(c_all, dmod_cols, w_mod, name):
    def body(c_ref, d_ref, w_ref, dw_ref, dc_ref):
        @pl.when(pl.program_id(0) == 0)
        def _():
            dc_ref[...] = jnp.zeros_like(dc_ref)

        cc = c_ref[...]
        dd = d_ref[...].astype(BF16)
        dw_ref[...] = _dotg((cc * jax.nn.sigmoid(cc)).astype(BF16), dd, TN)
        dc_ref[...] += _dotg(dd, w_ref[...].astype(BF16), NT)

    return pl.pallas_call(
        body, name=name, grid=(2,),
        in_specs=[pl.BlockSpec((16, D), lambda l: (0, 0)), pl.BlockSpec((None, 16, MOD_COLS), lambda l: (l, 0, 0)),
                  pl.BlockSpec((None, D, MOD_COLS), lambda l: (l, 0, 0))],
        out_specs=[pl.BlockSpec((None, D, MOD_COLS), lambda l: (l, 0, 0)), pl.BlockSpec((16, D), lambda l: (0, 0))],
        out_shape=[jax.ShapeDtypeStruct((2, D, MOD_COLS), F32), jax.ShapeDtypeStruct((16, D), F32)],
        compiler_params=_params("arbitrary"),
    )(c_all, dmod_cols, w_mod)


def _row_tile(rows, cols, n_arrays):
    budget = VMEM_LIMIT_BYTES // 4 // (2 * 4 * n_arrays * cols)
    best = None
    for tr in range(16, rows + 1, 16):
        if rows % tr == 0 and tr <= budget:
            best = tr
    return best if best is not None else rows


def elementwise(fn, ins, out_dtypes, name):
    rows, cols = ins[0].shape
    tr = _row_tile(rows, cols, len(ins) + len(out_dtypes))

    def body(*refs):
        outs = fn(*[r[...] for r in refs[:len(ins)]])
        for o_ref, o in zip(refs[len(ins):], outs):
            o_ref[...] = o.astype(o_ref.dtype)

    spec = pl.BlockSpec((tr, cols), lambda i: (i, 0))
    return pl.pallas_call(
        body, name=name, grid=(rows // tr,), in_specs=[spec] * len(ins), out_specs=[spec] * len(out_dtypes),
        out_shape=[jax.ShapeDtypeStruct((rows, cols), dt) for dt in out_dtypes],
        compiler_params=_params("parallel"),
    )(*ins)


def _adamw_tile(w, g, m, v):
    m = ADAM_B1 * m + (1.0 - ADAM_B1) * g
    v = ADAM_B2 * v + (1.0 - ADAM_B2) * (g * g)
    m_hat = m / (1.0 - ADAM_B1 ** ADAM_STEP)
    v_hat = v / (1.0 - ADAM_B2 ** ADAM_STEP)
    return -ADAM_LR * (m_hat / (jnp.sqrt(v_hat) + ADAM_EPS) + ADAM_WD * w), m, v


def adamw(w, g, m, v, name):
    shape = w.shape
    two_d = (-1, shape[-1]) if w.ndim > 1 else (1, -1)
    outs = elementwise(_adamw_tile, [a.reshape(two_d) for a in (w, g, m, v)], [F32] * 3, name)
    return [o.reshape(shape) for o in outs]


def sum8(gathered, name):
    def body(*refs):
        n = len(refs) // 2
        for g_ref, o_ref in zip(refs[:n], refs[n:]):
            acc = g_ref[0]
            for dev in range(1, N_DEV):
                acc = acc + g_ref[dev]
            o_ref[...] = acc

    return pl.pallas_call(
        body, name=name,
        out_shape=[jax.ShapeDtypeStruct(a.shape[1:], F32) for a in gathered],
        compiler_params=_params(),
    )(*gathered)


def _place():
    return lax.axis_index("x"), lax.axis_index("y"), lax.axis_index("c")


def _any(n):
    return [pl.BlockSpec(memory_space=pl.ANY)] * n


def all_gather(blocks, name):
    n = len(blocks)

    def body(*refs):
        ins, outs = refs[:n], refs[n:2 * n]
        send_sems, recv_sems, local_sems = refs[2 * n:]
        x, y, c = _place()
        me, sibling = (x, y, c), (x, y, 1 - c)
        chips = [(1 - x, y), (x, 1 - y), (1 - x, 1 - y)]

        def copy(ti, k, block, to, src=None):
            dst = outs[ti].at[4 * block[0] + 2 * block[1] + block[2]]
            return pltpu.make_async_remote_copy(src_ref=dst if src is None else src, dst_ref=dst, send_sem=send_sems.at[ti, k],
                                                recv_sem=recv_sems.at[ti, k], device_id=to, device_id_type=MESH)

        local, sent = [], []
        for ti in range(n):
            local.append(pltpu.make_async_copy(ins[ti], outs[ti].at[4 * x + 2 * y + c], local_sems.at[ti]))
            sent.append(copy(ti, 0, me, sibling, src=ins[ti]))
            sent += [copy(ti, 1 + j, me, (*chip, c), src=ins[ti]) for j, chip in enumerate(chips)]
        for cp in local + sent:
            cp.start()
        for ti in range(n):
            for j, chip in enumerate(chips):
                copy(ti, 1 + j, (*chip, c), me).wait_recv()
                sent.append(copy(ti, 4 + j, (*chip, c), sibling))
                sent[-1].start()
        for ti in range(n):
            copy(ti, 0, sibling, me).wait_recv()
            for j, chip in enumerate(chips):
                copy(ti, 4 + j, (*chip, 1 - c), me).wait_recv()
        for cp in sent:
            cp.wait_send()
        for cp in local:
            cp.wait()

    return pl.pallas_call(
        body, name=name, in_specs=_any(n), out_specs=_any(n),
        out_shape=[jax.ShapeDtypeStruct((N_DEV,) + b.shape, b.dtype) for b in blocks],
        scratch_shapes=[pltpu.SemaphoreType.DMA((n, 7)), pltpu.SemaphoreType.DMA((n, 7)), pltpu.SemaphoreType.DMA((n,))],
    )(*blocks)


def _three_chips(x, y):
    return [(1 - x, y), (x, 1 - y), (1 - x, 1 - y)]


def gather_weights(shards, name):
    n = len(shards)

    def body(*refs):
        ins, outs = refs[:n], refs[n:2 * n]
        send_sems, recv_sems, local_sems = refs[2 * n:]
        x, y, c = _place()
        sibling = (x, y, 1 - c)
        chips = _three_chips(x, y)

        def half(ti, chip, core):
            rh = ins[ti].shape[1] // 2
            return outs[ti].at[:, 2 * chip[0] + chip[1], pl.ds(core * rh, rh), :]

        def copy(ti, k, chip, core, to, src=None):
            dst = half(ti, chip, core)
            return pltpu.make_async_remote_copy(src_ref=dst if src is None else src, dst_ref=dst, send_sem=send_sems.at[ti, k],
                                                recv_sem=recv_sems.at[ti, k], device_id=to, device_id_type=MESH)

        local, sent = [], []
        for ti in range(n):
            rh = ins[ti].shape[1] // 2
            local.append(pltpu.make_async_copy(ins[ti], outs[ti].at[:, 2 * x + y], local_sems.at[ti]))
            sent += [copy(ti, k, (x, y), c, (*chip, c), src=ins[ti].at[:, pl.ds(c * rh, rh), :]) for k, chip in enumerate(chips)]
        for cp in local + sent:
            cp.start()
        for ti in range(n):
            for k, chip in enumerate(chips):
                copy(ti, k, chip, c, (x, y, c)).wait_recv()
                sent.append(copy(ti, 3 + k, chip, c, sibling))
                sent[-1].start()
        for ti in range(n):
            for k, chip in enumerate(chips):
                copy(ti, 3 + k, chip, 1 - c, (x, y, c)).wait_recv()
        for cp in sent:
            cp.wait_send()
        for cp in local:
            cp.wait()

    return pl.pallas_call(
        body, name=name, in_specs=_any(n), out_specs=_any(n),
        out_shape=[jax.ShapeDtypeStruct((2, N_SLOT) + w.shape[1:], w.dtype) for w in shards],
        scratch_shapes=[pltpu.SemaphoreType.DMA((n, 6)), pltpu.SemaphoreType.DMA((n, 6)), pltpu.SemaphoreType.DMA((n,))],
    )(*shards)


def pair_exchange(g16, g32, name):
    n = len(g16)

    def body(*refs):
        a16, a32, got, own = refs[:n], refs[n:2 * n], refs[2 * n:3 * n], refs[3 * n:4 * n]
        send_sems, recv_sems, local_sems = refs[4 * n:]
        x, y, c = _place()
        copies = []
        for ti in range(n):
            rh = a16[ti].shape[1] // 2
            copies.append(pltpu.make_async_remote_copy(
                src_ref=a16[ti].at[:, pl.ds((1 - c) * rh, rh), :], dst_ref=got[ti], send_sem=send_sems.at[ti],
                recv_sem=recv_sems.at[ti], device_id=(x, y, 1 - c), device_id_type=MESH))
            copies.append(pltpu.make_async_copy(a32[ti].at[:, pl.ds(c * rh, rh), :], own[ti], local_sems.at[ti]))
        for cp in copies:
            cp.start()
        for cp in copies:
            cp.wait()

    halves = [(a.shape[0], a.shape[1] // 2, a.shape[2]) for a in g16]
    return pl.pallas_call(
        body, name=name, in_specs=_any(2 * n), out_specs=_any(2 * n),
        out_shape=[jax.ShapeDtypeStruct(h, BF16) for h in halves] + [jax.ShapeDtypeStruct(h, F32) for h in halves],
        scratch_shapes=[pltpu.SemaphoreType.DMA((n,)), pltpu.SemaphoreType.DMA((n,)), pltpu.SemaphoreType.DMA((n,))],
    )(*g16, *g32)


def chip_scatter(p16, p32, name):
    n = len(p16)

    def body(*refs):
        a16, a32, got, own = refs[:n], refs[n:2 * n], refs[2 * n:5 * n], refs[5 * n:6 * n]
        send_sems, recv_sems, local_sems = refs[6 * n:]
        x, y, c = _place()
        copies = []
        for ti in range(n):
            for k, chip in enumerate(_three_chips(x, y)):
                copies.append(pltpu.make_async_remote_copy(
                    src_ref=a16[ti].at[2 * chip[0] + chip[1]], dst_ref=got[3 * ti + k], send_sem=send_sems.at[ti, k],
                    recv_sem=recv_sems.at[ti, k], device_id=(*chip, c), device_id_type=MESH))
            copies.append(pltpu.make_async_copy(a32[ti].at[2 * x + y], own[ti], local_sems.at[ti]))
        for cp in copies:
            cp.start()
        for cp in copies:
            cp.wait()

    return pl.pallas_call(
        body, name=name, in_specs=_any(2 * n), out_specs=_any(4 * n),
        out_shape=[jax.ShapeDtypeStruct(a.shape[1:], BF16) for a in p16 for _ in range(3)]
        + [jax.ShapeDtypeStruct(a.shape[1:], F32) for a in p32],
        scratch_shapes=[pltpu.SemaphoreType.DMA((n, 3)), pltpu.SemaphoreType.DMA((n, 3)), pltpu.SemaphoreType.DMA((n,))],
    )(*p16, *p32)


def pair_gather(halves, name):
    n = len(halves)

    def body(*refs):
        ins, outs = refs[:n], refs[n:n + n // 2]
        send_sems, recv_sems, local_sems = refs[n + n // 2:]
        x, y, c = _place()
        copies = []
        for ti in range(n):
            rh = ins[ti].shape[0]
            dst = outs[ti // 2].at[ti % 2, pl.ds(c * rh, rh), :]
            copies.append(pltpu.make_async_remote_copy(src_ref=ins[ti], dst_ref=dst, send_sem=send_sems.at[ti],
                                                       recv_sem=recv_sems.at[ti], device_id=(x, y, 1 - c), device_id_type=MESH))
            copies.append(pltpu.make_async_copy(ins[ti], dst, local_sems.at[ti]))
        for cp in copies:
            cp.start()
        for cp in copies:
            cp.wait()

    return pl.pallas_call(
        body, name=name, in_specs=_any(n), out_specs=_any(n // 2),
        out_shape=[jax.ShapeDtypeStruct((2, 2 * halves[2 * i].shape[0], halves[2 * i].shape[1]), F32) for i in range(n // 2)],
        scratch_shapes=[pltpu.SemaphoreType.DMA((n,)), pltpu.SemaphoreType.DMA((n,)), pltpu.SemaphoreType.DMA((n,))],
    )(*halves)


def reduce_small(dml, loss_blk, name):
    def body(d_ref, l_ref, tot_ref, rows_ref, fin_ref):
        rows_ref[...] = jnp.zeros_like(rows_ref)
        for l in range(2):
            ctx = d_ref[0, l, 1]
            lat = d_ref[0, l, 0]
            rows_ref[l, 0] = lat
            for dev in range(1, N_DEV):
                rows_ref[l, dev] = d_ref[dev, l, 0]
                ctx = ctx + d_ref[dev, l, 1]
                lat = lat + d_ref[dev, l, 0]
            rows_ref[l, N_DEV] = ctx
            tot_ref[l] = lat + ctx
        acc = l_ref[0]
        for dev in range(1, N_DEV):
            acc = acc + l_ref[dev]
        loss = (0.5 / D) * jnp.sum(acc[1:2, :], axis=1, keepdims=True)
        row = lax.broadcasted_iota(jnp.int32, (8, D), 0)
        fin_ref[...] = jnp.where(row == 0, acc[0:1, :], loss)

    return pl.pallas_call(
        body, name=name,
        out_shape=[jax.ShapeDtypeStruct((2, 16, D), F32), jax.ShapeDtypeStruct((2, 16, 16, D), F32),
                   jax.ShapeDtypeStruct((8, D), F32)],
        compiler_params=_params(),
    )(dml, loss_blk)


def rope_tables(t, s):
    rows = t // GRID_W
    row = jnp.repeat(jnp.arange(rows), GRID_W).astype(F32)
    col = jnp.tile(jnp.arange(GRID_W), rows).astype(F32)
    inv = ROPE_BASE ** (-jnp.arange(0, HEAD // 2, 2, dtype=F32) / (HEAD // 2))
    ang = jnp.concatenate([row[:, None] * inv, col[:, None] * inv], axis=-1)
    cos, sin = jnp.cos(ang), jnp.sin(ang)
    cos = jnp.concatenate([jnp.tile(cos, (1, 4)), jnp.ones((s - t, BLK), F32)], axis=0)
    sin = jnp.concatenate([jnp.tile(jnp.concatenate([-sin, sin], axis=1), (1, 2)), jnp.zeros((s - t, BLK), F32)], axis=0)
    return cos, sin


def local_step(x1, ctx1, target, mods, norms, nfinal, wts, w_pool, pool_scale, sink):
    t, s = x1.shape[0], x1.shape[0] + ctx1.shape[0]
    n_lat = t // TM
    cos, sin = rope_tables(t, s)
    h = jnp.concatenate([x1, ctx1], axis=0)
    saved = []
    for l in range(2):
        h0 = h
        h1, ab1, f1 = ffn_fwd(h0, mods, norms[0], wts["ffn1_in"], wts["ffn1_out"], l, 0, n_lat, f"ffn1_fwd_{l}")
        u, q, k, v = proj_fwd(h1, mods, norms[1], wts["w_in"], cos, sin, l, n_lat, f"proj_fwd_{l}")
        h2, cat, lse = mix_fwd(h1, q, k, v, u, w_pool, pool_scale, sink, wts["w_out"], mods, l, t, f"mix_fwd_{l}")
        h, ab2, f2 = ffn_fwd(h2, mods, norms[2], wts["ffn2_in"], wts["ffn2_out"], l, 6, n_lat, f"ffn2_fwd_{l}")
        saved.append((h0, ab1, f1, h1, u, q, k, v, cat, lse, h2, ab2, f2))
    dh, loss_blk = loss_head(h, target, nfinal, t, "loss_head")
    big, small = [None, None], [None, None]
    for l in (1, 0):
        h0, ab1, f1, h1, u, q, k, v, cat, lse, h2, ab2, f2 = saved[l]
        dh, dab, df, n, act, dm_f2 = ffn_bwd(h2, ab2, f2, dh, mods, norms[2], wts["ffn2_in"], wts["ffn2_out"], l, 6, n_lat,
                                             f"ffn2_bwd_{l}")
        g_f2i = wgrad(n, dab, D, FF_COLS, FF_COLS, f"ffn2_in_wgrad_{l}")
        g_f2o = wgrad(act, df, D_FF // 2, D, None, f"ffn2_out_wgrad_{l}")
        dq, dk, dv, du, dmo, dwp, dps, dsink, dm_gate = mix_bwd(dh, cat, q, k, v, u, lse, w_pool, pool_scale, sink,
                                                               wts["w_out"], mods, l, t, f"mix_bwd_{l}")
        g_wo = wgrad(cat, dmo, POOL_W + ATTN_W, D, None, f"w_out_wgrad_{l}")
        dh, dp, n, dm_mix = proj_bwd(h1, du, dq, dk, dv, dh, mods, norms[1], wts["w_in"], cos, sin, l, n_lat, f"proj_bwd_{l}")
        g_wi = wgrad(n, dp, D, PROJ_W // 2, None, f"w_in_wgrad_{l}")
        dh, dab, df, n, act, dm_f1 = ffn_bwd(h0, ab1, f1, dh, mods, norms[0], wts["ffn1_in"], wts["ffn1_out"], l, 0, n_lat,
                                             f"ffn1_bwd_{l}")
        g_f1i = wgrad(n, dab, D, FF_COLS, FF_COLS, f"ffn1_in_wgrad_{l}")
        g_f1o = wgrad(act, df, D_FF // 2, D, None, f"ffn1_out_wgrad_{l}")
        big[l] = dict(ffn1_in=g_f1i, ffn1_out=g_f1o, w_in=g_wi, w_out=g_wo, ffn2_in=g_f2i, ffn2_out=g_f2o)
        dml = jnp.concatenate([dm_f1[:, 0:3], dm_mix[:, 0:2], dm_gate[:, 2:3], dm_f2[:, 0:3],
                               dm_f1[:, 3:4], dm_mix[:, 3:4], dm_f2[:, 3:4], jnp.zeros((2, 4, D), F32)], axis=1)
        small[l] = dict(dml=dml, dwp=dwp, dps=dps, dsink=dsink)
    return dh[:t], loss_blk, big, small


BIG = ("ffn1_in", "ffn1_out", "w_in", "w_out", "ffn2_in", "ffn2_out")


def _slot_major(name, g):
    if name == "w_in":
        return jnp.stack(jnp.split(g, N_SLOT, axis=1), axis=0)
    if name in ("ffn1_in", "ffn2_in"):
        return g
    return g.reshape(N_SLOT, g.shape[0] // N_SLOT, g.shape[1])


def reduce_scatter(big):
    g32 = [_slot_major(name, big[l][name][0]) for name in BIG for l in range(2)]
    g16 = [_slot_major(name, big[l][name][1]) for name in BIG for l in range(2)]
    n = len(g32)
    out = pair_exchange(g16, g32, "grad_pair_exchange")
    got, own = out[:n], out[n:]
    p32, p16 = [], []
    for ti in range(n):
        shape = own[ti].shape
        a, b = elementwise(lambda o, r: (o + r.astype(F32),) * 2, [own[ti].reshape(-1, shape[2]), got[ti].reshape(-1, shape[2])],
                           [F32, BF16], f"grad_pair_sum_{ti}")
        p32.append(a.reshape(shape))
        p16.append(b.reshape(shape))
    out = chip_scatter(p16, p32, "grad_chip_scatter")
    got, own = out[:3 * n], out[3 * n:]
    halves = [elementwise(lambda m, r0, r1, r2: (m + r0.astype(F32) + r1.astype(F32) + r2.astype(F32),),
                          [own[ti], got[3 * ti], got[3 * ti + 1], got[3 * ti + 2]], [F32], f"grad_chip_sum_{ti}")[0]
              for ti in range(n)]
    return pair_gather(halves, "grad_pair_gather")


def _silu_grad(z):
    sg = jax.nn.sigmoid(z)
    return sg * (1 + z * (1 - sg))


def kernel(x, c, ctx, c_ctx, w_mod, b_mod, norm_ffn1, w_ffn1_in, w_ffn1_out, norm_mix, w_in, w_pool, pool_scale, sink, w_out, norm_ffn2, w_ffn2_in, w_ffn2_out, norm_final, loss_target, m_c_ctx, m_w_mod, m_b_mod, m_norm_ffn1, m_w_ffn1_in, m_w_ffn1_out, m_norm_mix, m_w_in, m_w_pool, m_pool_scale, m_sink, m_w_out, m_norm_ffn2, m_w_ffn2_in, m_w_ffn2_out, m_norm_final, v_c_ctx, v_w_mod, v_b_mod, v_norm_ffn1, v_w_ffn1_in, v_w_ffn1_out, v_norm_mix, v_w_in, v_w_pool, v_pool_scale, v_sink, v_w_out, v_norm_ffn2, v_w_ffn2_in, v_w_ffn2_out, v_norm_final):
    px, py, pc = _place()
    slot, me = 2 * px + py, 4 * px + 2 * py + pc
    n_grp = len(POOL_WINDOWS)

    (c_rows,) = all_gather([c.reshape(8, D // 8)], "gather_c")
    c_all = jnp.concatenate([c_rows.reshape(N_DEV, D), c_ctx.reshape(1, D), jnp.zeros((16 - N_DEV - 1, D), F32)], axis=0)
    b_cols = lax.dynamic_slice(b_mod, (0, slot * MOD_COLS), (2, MOD_COLS)).reshape(2, 1, MOD_COLS)
    (mod_parts,) = all_gather([mod_rows(c_all, w_mod, b_cols, "mod_rows")], "gather_mods")
    mods_all = mod_parts[0::2].transpose(1, 2, 0, 3).reshape(2, 16, N_MOD * D)
    mx = lax.dynamic_slice(mods_all, (0, me, 0), (2, 1, N_MOD * D)).reshape(2, N_MOD, D)
    mc = mods_all[:, N_DEV].reshape(2, N_MOD, D)
    pad = jnp.zeros((2, 16 - N_MOD, D), F32)
    mods = jnp.stack([jnp.concatenate([mx, pad], axis=1), jnp.concatenate([mc, pad], axis=1)], axis=1)

    shards = dict(ffn1_in=w_ffn1_in, ffn1_out=w_ffn1_out, w_in=w_in, w_out=w_out, ffn2_in=w_ffn2_in, ffn2_out=w_ffn2_out)
    whole = dict(zip(BIG, gather_weights([shards[name].astype(BF16) for name in BIG], "gather_weights")))
    wts = dict(ffn1_in=whole["ffn1_in"], ffn2_in=whole["ffn2_in"],
               ffn1_out=whole["ffn1_out"].reshape(2, D_FF, D), ffn2_out=whole["ffn2_out"].reshape(2, D_FF, D),
               w_in=whole["w_in"].transpose(0, 2, 1, 3).reshape(2, D, PROJ_W), w_out=whole["w_out"].reshape(2, POOL_W + ATTN_W, D))

    norms = [g.reshape(2, 1, D) for g in (norm_ffn1, norm_mix, norm_ffn2)]
    dx, loss_blk, big, small = local_step(x[0], ctx[0], loss_target[0], mods, norms, norm_final.reshape(1, D), wts,
                                          w_pool.astype(BF16), pool_scale.reshape(2, 1, POOL_W), sink)

    grads = dict(zip(("w_ffn1_in", "w_ffn1_out", "w_in", "w_out", "w_ffn2_in", "w_ffn2_out"), reduce_scatter(big)))

    stacked = {k: jnp.stack([small[0][k], small[1][k]]) for k in ("dml", "dwp", "dps", "dsink")}
    g_dml, g_dwp, g_dps, g_dsink, g_loss = all_gather(
        [stacked["dml"].reshape(64, D), stacked["dwp"].reshape(2 * n_grp * GROUP, GROUP), stacked["dps"].reshape(16, POOL_W),
         stacked["dsink"].reshape(16, BLK), loss_blk], "gather_small")
    tot, rows, fin = reduce_small(g_dml.reshape(N_DEV, 2, 2, 16, D), g_loss, "reduce_small")
    s_dwp, s_dps, s_dsink = sum8([g_dwp, g_dps, g_dsink], "sum_pool_sink")
    grads.update(
        w_pool=s_dwp.reshape(2, n_grp, GROUP, GROUP), pool_scale=s_dps.reshape(2, 8, POOL_W)[:, 0],
        sink=s_dsink.reshape(2, 8, BLK)[:, 0, :N_HEADS], b_mod=tot[:, :N_MOD].reshape(2, N_MOD * D),
        norm_ffn1=tot[:, N_MOD], norm_mix=tot[:, N_MOD + 1], norm_ffn2=tot[:, N_MOD + 2], norm_final=fin[0])
    loss = fin[1, 0]

    dmod_cols = lax.dynamic_slice(rows[:, :, :N_MOD, :].reshape(2, 16, N_MOD * D), (0, 0, slot * MOD_COLS), (2, 16, MOD_COLS))
    grads["w_mod"], dc = mod_grads(c_all, dmod_cols, w_mod, "mod_grads")
    (g_dc,) = all_gather([dc], "gather_dc")
    (s_dc,) = sum8([g_dc], "sum_dc")
    (d_c_ctx,) = elementwise(lambda d, z: (0.5 * d * _silu_grad(z),), [s_dc[N_DEV:N_DEV + 1], c_ctx.reshape(1, D)], [F32], "c_ctx_grad")
    grads["c_ctx"] = d_c_ctx.reshape(D)

    given = dict(c_ctx=(c_ctx, m_c_ctx, v_c_ctx), w_mod=(w_mod, m_w_mod, v_w_mod), b_mod=(b_mod, m_b_mod, v_b_mod),
                 norm_ffn1=(norm_ffn1, m_norm_ffn1, v_norm_ffn1), w_ffn1_in=(w_ffn1_in, m_w_ffn1_in, v_w_ffn1_in),
                 w_ffn1_out=(w_ffn1_out, m_w_ffn1_out, v_w_ffn1_out), norm_mix=(norm_mix, m_norm_mix, v_norm_mix),
                 w_in=(w_in, m_w_in, v_w_in), w_pool=(w_pool, m_w_pool, v_w_pool),
                 pool_scale=(pool_scale, m_pool_scale, v_pool_scale), sink=(sink, m_sink, v_sink), w_out=(w_out, m_w_out, v_w_out),
                 norm_ffn2=(norm_ffn2, m_norm_ffn2, v_norm_ffn2), w_ffn2_in=(w_ffn2_in, m_w_ffn2_in, v_w_ffn2_in),
                 w_ffn2_out=(w_ffn2_out, m_w_ffn2_out, v_w_ffn2_out), norm_final=(norm_final, m_norm_final, v_norm_final))
    g_out, d_out, m_out, v_out = [], [], [], []
    for name, (w, m, v) in given.items():
        delta, new_m, new_v = adamw(w, grads[name], m, v, f"adamw_{name}")
        g_out.append(grads[name])
        d_out.append(delta)
        m_out.append(new_m)
        v_out.append(new_v)
    return (loss, dx[None], *g_out, *d_out, *m_out, *v_out)
```

```python
import jax
import jax.numpy as jnp
from jax import lax
from jax.experimental import pallas as pl
from jax.experimental.pallas import tpu as pltpu

F32, BF16 = jnp.float32, jnp.bfloat16
D = 1024
D_FF = 2816
N_SLOT = 4
FF_COLS = 2 * D_FF // N_SLOT
N_MOD = 9
MOD_COLS = N_MOD * D // N_SLOT
POOL_W, ATTN_W, KV_W = 512, 512, 128
PROJ_W = POOL_W + ATTN_W + 2 * KV_W
N_HEADS, Q_GROUP, HEAD = 8, 4, 64
GROUP = 128
POOL_WINDOWS = (2, 4, 8, 16)
BLK = 128
GRID_W = 64
ROPE_BASE = 10000.0
EPS = 1e-6
NEG_INF = -1e30
TM = 256
N_DEV = 8
VMEM_LIMIT_BYTES = 56 * 1024 * 1024
ADAM_LR, ADAM_B1, ADAM_B2, ADAM_EPS, ADAM_WD, ADAM_STEP = 0.001, 0.9, 0.999, 1e-08, 0.01, 10
MESH = pl.DeviceIdType.MESH
NT = (((1,), (1,)), ((), ()))
TN = (((0,), (0,)), ((), ()))


def _params(*sem):
    return pltpu.CompilerParams(dimension_semantics=sem, vmem_limit_bytes=VMEM_LIMIT_BYTES)


def _whole(shape, lead=()):
    idx = tuple(lead) + (0,) * len(shape)
    return pl.BlockSpec((None,) * len(lead) + tuple(shape), lambda *_: idx, pipeline_mode=pl.Buffered(1))


def _rows(cols, tm=TM):
    return pl.BlockSpec((tm, cols), lambda i: (i, 0))


def _mods_spec(layer, n_lat):
    return pl.BlockSpec((None, None, 16, D), lambda i: (layer, (i >= n_lat).astype(jnp.int32), 0, 0))


def _acc_spec(n_lat):
    return pl.BlockSpec((None, 8, D), lambda i: ((i >= n_lat).astype(jnp.int32), 0, 0))


def _dot(a, b):
    return jnp.dot(a, b, preferred_element_type=F32)


def _dotg(a, b, dims):
    return lax.dot_general(a, b, dims, preferred_element_type=F32)


def _sum0(v):
    return jnp.sum(v, axis=0, keepdims=True)


def _norm_mod(h, g, shift, scale):
    r = lax.rsqrt(jnp.mean(h * h, axis=-1, keepdims=True) + EPS)
    xhat = h * r
    y = xhat * g
    return y * (1 + scale) + shift, xhat, r, y


def _norm_mod_bwd(dn, xhat, r, y, g, scale):
    dy = dn * (1 + scale)
    dx = dy * g
    dh = r * (dx - xhat * jnp.mean(dx * xhat, axis=-1, keepdims=True))
    return _sum0(dn), _sum0(dn * y), _sum0(dy * xhat), dh


def _swap_halves(v):
    w = v.shape[1]
    lane = lax.broadcasted_iota(jnp.int32, v.shape, 1)
    return jnp.where(lane % HEAD < HEAD // 2, pltpu.roll(v, w - HEAD // 2, axis=1), pltpu.roll(v, HEAD // 2, axis=1))


def _tile_lanes(t, width):
    return t if width == t.shape[1] else jnp.concatenate([t] * (width // t.shape[1]), axis=1)


def _rope(v, cos, sin):
    return v * _tile_lanes(cos, v.shape[1]) + _swap_halves(v) * _tile_lanes(sin, v.shape[1])


def _unrope(g, cos, sin):
    return g * _tile_lanes(cos, g.shape[1]) + _swap_halves(g * _tile_lanes(sin, g.shape[1]))


def ffn_fwd(h, mods, g, w4, wo, layer, k0, n_lat, name):
    s = h.shape[0]

    def body(h_ref, m_ref, g_ref, w_ref, wo_ref, ho_ref, ab_ref, f_ref):
        hh = h_ref[...]
        n, _, _, _ = _norm_mod(hh, g_ref[...], m_ref[k0:k0 + 1, :], m_ref[k0 + 1:k0 + 2, :])
        nb = n.astype(BF16)
        acc = jnp.zeros((TM, D), F32)
        for j in range(2):
            a = _dot(nb, w_ref[j])
            b = _dot(nb, w_ref[2 + j])
            ab_ref[:, j * FF_COLS:(j + 1) * FF_COLS] = a.astype(BF16)
            ab_ref[:, (2 + j) * FF_COLS:(3 + j) * FF_COLS] = b.astype(BF16)
            act = (a * jax.nn.sigmoid(a) * b).astype(BF16)
            acc = acc + _dot(act, wo_ref[j * FF_COLS:(j + 1) * FF_COLS, :])
        f_ref[...] = acc
        ho_ref[...] = hh + 0.5 * m_ref[k0 + 2:k0 + 3, :] * acc

    return pl.pallas_call(
        body, name=name, grid=(s // TM,),
        in_specs=[_rows(D), _mods_spec(layer, n_lat), _whole((1, D), (layer,)),
                  _whole((N_SLOT, D, FF_COLS), (layer,)), _whole((D_FF, D), (layer,))],
        out_specs=[_rows(D), _rows(2 * D_FF), _rows(D)],
        out_shape=[jax.ShapeDtypeStruct((s, D), F32), jax.ShapeDtypeStruct((s, 2 * D_FF), BF16),
                   jax.ShapeDtypeStruct((s, D), F32)],
        compiler_params=_params("parallel"),
    )(h, mods, g, w4, wo)


def ffn_bwd(h, ab, f, dh, mods, g, w4, wo, layer, k0, n_lat, name):
    s = h.shape[0]

    def body(h_ref, ab_ref, f_ref, dh_ref, m_ref, g_ref, w_ref, wo_ref, dhi_ref, dab_ref, df_ref, n_ref, act_ref, dm_ref):
        i = pl.program_id(0)

        @pl.when((i == 0) | (i == n_lat))
        def _():
            dm_ref[...] = jnp.zeros_like(dm_ref)

        hh, dho, gg = h_ref[...], dh_ref[...], g_ref[...]
        scale, gate = m_ref[k0 + 1:k0 + 2, :], m_ref[k0 + 2:k0 + 3, :]
        n, xhat, r, y = _norm_mod(hh, gg, m_ref[k0:k0 + 1, :], scale)
        n_ref[...] = n.astype(BF16)
        dgate = _sum0(dho * (0.5 * f_ref[...]))
        dfb = ((0.5 * gate) * dho).astype(BF16)
        df_ref[...] = dfb
        dn = jnp.zeros((TM, D), F32)
        for j in range(2):
            a = ab_ref[:, j * FF_COLS:(j + 1) * FF_COLS].astype(F32)
            b = ab_ref[:, (2 + j) * FF_COLS:(3 + j) * FF_COLS].astype(F32)
            sg = jax.nn.sigmoid(a)
            sa = a * sg
            act_ref[:, j * FF_COLS:(j + 1) * FF_COLS] = (sa * b).astype(BF16)
            dact = _dotg(dfb, wo_ref[j * FF_COLS:(j + 1) * FF_COLS, :], NT)
            da = (dact * b * (sg * (1 + a * (1 - sg)))).astype(BF16)
            db = (dact * sa).astype(BF16)
            dab_ref[:, j * FF_COLS:(j + 1) * FF_COLS] = da
            dab_ref[:, (2 + j) * FF_COLS:(3 + j) * FF_COLS] = db
            dn = dn + _dotg(da, w_ref[j], NT) + _dotg(db, w_ref[2 + j], NT)
        dsh, dsc, dg, dhn = _norm_mod_bwd(dn, xhat, r, y, gg, scale)
        dhi_ref[...] = dho + dhn
        dm_ref[0:1, :] += dsh
        dm_ref[1:2, :] += dsc
        dm_ref[2:3, :] += dgate
        dm_ref[3:4, :] += dg

    return pl.pallas_call(
        body, name=name, grid=(s // TM,),
        in_specs=[_rows(D), _rows(2 * D_FF), _rows(D), _rows(D), _mods_spec(layer, n_lat), _whole((1, D), (layer,)),
                  _whole((N_SLOT, D, FF_COLS), (layer,)), _whole((D_FF, D), (layer,))],
        out_specs=[_rows(D), _rows(2 * D_FF), _rows(D), _rows(D), _rows(D_FF), _acc_spec(n_lat)],
        out_shape=[jax.ShapeDtypeStruct((s, D), F32), jax.ShapeDtypeStruct((s, 2 * D_FF), BF16),
                   jax.ShapeDtypeStruct((s, D), BF16), jax.ShapeDtypeStruct((s, D), BF16),
                   jax.ShapeDtypeStruct((s, D_FF), BF16), jax.ShapeDtypeStruct((2, 8, D), F32)],
        compiler_params=_params("arbitrary"),
    )(h, ab, f, dh, mods, g, w4, wo)


def _token_tile(s, limit=1088):
    return max(ts for ts in range(16, limit + 1, 16) if s % ts == 0)


def wgrad(a, b, tk, tn, slot_cols, name):
    s, k = a.shape
    n = b.shape[1]
    ts = _token_tile(s)
    steps = s // ts

    def body(a_ref, b_ref, o_ref, o16_ref):
        r = _dotg(a_ref[...], b_ref[...], TN)
        si = pl.program_id(2)

        @pl.when(si == 0)
        def _():
            o_ref[...] = r

        @pl.when(si > 0)
        def _():
            o_ref[...] += r

        @pl.when(si == steps - 1)
        def _():
            o16_ref[...] = o_ref[...].astype(BF16)

    if slot_cols is None:
        shape, spec = (k, n), pl.BlockSpec((tk, tn), lambda i, j, si: (i, j))
    else:
        per = slot_cols // tn
        shape, spec = (n // slot_cols, k, slot_cols), pl.BlockSpec((None, tk, tn), lambda i, j, si: (lax.div(j, per), i, lax.rem(j, per)))
    return pl.pallas_call(
        body, name=name, grid=(k // tk, n // tn, steps),
        in_specs=[pl.BlockSpec((ts, tk), lambda i, j, si: (si, i)), pl.BlockSpec((ts, tn), lambda i, j, si: (si, j))],
        out_specs=[spec, spec],
        out_shape=[jax.ShapeDtypeStruct(shape, F32), jax.ShapeDtypeStruct(shape, BF16)],
        compiler_params=_params("parallel", "parallel", "arbitrary"),
    )(a, b)


def proj_fwd(h, mods, g, w_in, cos, sin, layer, n_lat, name):
    s = h.shape[0]

    def body(h_ref, m_ref, g_ref, w_ref, cos_ref, sin_ref, u_ref, q_ref, k_ref, v_ref):
        n, _, _, _ = _norm_mod(h_ref[...], g_ref[...], m_ref[3:4, :], m_ref[4:5, :])
        p = _dot(n.astype(BF16), w_ref[...])
        cs, sn = cos_ref[...], sin_ref[...]
        u_ref[...] = p[:, :POOL_W]
        q_ref[...] = (_rope(p[:, POOL_W:POOL_W + ATTN_W], cs, sn) * HEAD ** -0.5).astype(BF16)
        k_ref[...] = _rope(p[:, POOL_W + ATTN_W:POOL_W + ATTN_W + KV_W], cs, sn).astype(BF16)
        v_ref[...] = p[:, POOL_W + ATTN_W + KV_W:].astype(BF16)

    return pl.pallas_call(
        body, name=name, grid=(s // TM,),
        in_specs=[_rows(D), _mods_spec(layer, n_lat), _whole((1, D), (layer,)), _whole((D, PROJ_W), (layer,)),
                  _rows(BLK), _rows(BLK)],
        out_specs=[_rows(POOL_W), _rows(ATTN_W), _rows(KV_W), _rows(KV_W)],
        out_shape=[jax.ShapeDtypeStruct((s, POOL_W), F32), jax.ShapeDtypeStruct((s, ATTN_W), BF16),
                   jax.ShapeDtypeStruct((s, KV_W), BF16), jax.ShapeDtypeStruct((s, KV_W), BF16)],
        compiler_params=_params("parallel"),
    )(h, mods, g, w_in, cos, sin)


def proj_bwd(h, du, dq, dk, dv, dh, mods, g, w_in, cos, sin, layer, n_lat, name):
    s = h.shape[0]

    def body(h_ref, du_ref, dq_ref, dk_ref, dv_ref, dh_ref, m_ref, g_ref, w_ref, cos_ref, sin_ref,
             dhi_ref, dp_ref, n_ref, dm_ref):
        i = pl.program_id(0)

        @pl.when((i == 0) | (i == n_lat))
        def _():
            dm_ref[...] = jnp.zeros_like(dm_ref)

        gg, scale = g_ref[...], m_ref[4:5, :]
        n, xhat, r, y = _norm_mod(h_ref[...], gg, m_ref[3:4, :], scale)
        n_ref[...] = n.astype(BF16)
        cs, sn = cos_ref[...], sin_ref[...]
        dp = jnp.concatenate([du_ref[...], _unrope(dq_ref[...], cs, sn) * HEAD ** -0.5, _unrope(dk_ref[...], cs, sn),
                              dv_ref[...]], axis=1).astype(BF16)
        dp_ref[...] = dp
        dsh, dsc, dg, dhn = _norm_mod_bwd(_dotg(dp, w_ref[...], NT), xhat, r, y, gg, scale)
        dhi_ref[...] = dh_ref[...] + dhn
        dm_ref[0:1, :] += dsh
        dm_ref[1:2, :] += dsc
        dm_ref[3:4, :] += dg

    return pl.pallas_call(
        body, name=name, grid=(s // TM,),
        in_specs=[_rows(D), _rows(POOL_W), _rows(ATTN_W), _rows(KV_W), _rows(KV_W), _rows(D), _mods_spec(layer, n_lat),
                  _whole((1, D), (layer,)), _whole((D, PROJ_W), (layer,)), _rows(BLK), _rows(BLK)],
        out_specs=[_rows(D), _rows(PROJ_W), _rows(D), _acc_spec(n_lat)],
        out_shape=[jax.ShapeDtypeStruct((s, D), F32), jax.ShapeDtypeStruct((s, PROJ_W), BF16),
                   jax.ShapeDtypeStruct((s, D), BF16), jax.ShapeDtypeStruct((2, 8, D), F32)],
        compiler_params=_params("arbitrary"),
    )(h, du, dq, dk, dv, dh, mods, g, w_in, cos, sin)


def _window(i, n_lat_blk):
    return pl.multiple_of(jnp.clip(i - 1, 0, n_lat_blk - 1) * BLK, BLK)


def _pool_band(i, ws, w, seq_lo, seq_hi, transposed):
    shape = (3 * BLK, BLK) if transposed else (BLK, 3 * BLK)
    q = i * BLK + lax.broadcasted_iota(jnp.int32, shape, 1 if transposed else 0)
    k = ws + lax.broadcasted_iota(jnp.int32, shape, 0 if transposed else 1)
    band = (k >= jnp.maximum(q - w // 2, seq_lo)) & (k < jnp.minimum(q + w - w // 2, seq_hi))
    qc = i * BLK + lax.broadcasted_iota(jnp.int32, (BLK, 1), 0)
    cnt = jnp.minimum(qc + w - w // 2, seq_hi) - jnp.maximum(qc - w // 2, seq_lo)
    return jnp.where(band, 1.0, 0.0).astype(BF16), cnt.astype(F32)


def _split_dot(band, v):
    hi = v.astype(BF16)
    return _dot(band, hi) + _dot(band, (v - hi.astype(F32)).astype(BF16))


def _pooled(u_ref, i, ws, seq_lo, seq_hi, gi):
    band, cnt = _pool_band(i, ws, POOL_WINDOWS[gi], seq_lo, seq_hi, False)
    cols = slice(gi * GROUP, (gi + 1) * GROUP)
    mean = _split_dot(band, u_ref[pl.ds(ws, 3 * BLK), cols]) / cnt
    return mean - u_ref[pl.ds(pl.multiple_of(i * BLK, BLK), BLK), cols]


def _local_valid(i, ws, t):
    q = i * BLK + lax.broadcasted_iota(jnp.int32, (BLK, 3 * BLK), 0)
    k = ws + lax.broadcasted_iota(jnp.int32, (BLK, 3 * BLK), 1)
    return (i * BLK < t) & (k < t) & (jnp.abs(k - q) <= BLK)


def _head_cols(hd):
    return slice(hd * HEAD, (hd + 1) * HEAD)


def _lane_place(cols, width=BLK):
    lane = lax.broadcasted_iota(jnp.int32, (cols[0].shape[0], width), 1)
    out = jnp.zeros((cols[0].shape[0], width), F32)
    for hd, c in enumerate(cols):
        out = jnp.where(lane == hd, c, out)
    return out


def mix_fwd(h, q, k, v, u, w_pool, pool_scale, sink, w_out, mods, layer, t, name):
    s = h.shape[0]
    n_lat_blk = t // BLK

    def body(h_ref, q_ref, k_ref, v_ref, u_ref, wp_ref, ps_ref, sink_ref, wo_ref, m_ref, ho_ref, cat_ref, lse_ref):
        i = pl.program_id(0)
        ws = _window(i, n_lat_blk)
        is_lat = i < n_lat_blk
        seq_lo, seq_hi = jnp.where(is_lat, 0, t), jnp.where(is_lat, t, s)
        for gi in range(len(POOL_WINDOWS)):
            mixed = _dot(_pooled(u_ref, i, ws, seq_lo, seq_hi, gi).astype(BF16), wp_ref[gi])
            cat_ref[:, gi * GROUP:(gi + 1) * GROUP] = (mixed * ps_ref[:, gi * GROUP:(gi + 1) * GROUP]).astype(BF16)
        valid = _local_valid(i, ws, t)
        kw, vw = k_ref[pl.ds(ws, 3 * BLK), :], v_ref[pl.ds(ws, 3 * BLK), :]
        kc, vc = k_ref[t:s, :], v_ref[t:s, :]
        lses = []
        for hd in range(N_HEADS):
            kv = _head_cols(hd // Q_GROUP)
            qh = q_ref[:, _head_cols(hd)]
            sl = jnp.where(valid, _dotg(qh, kw[:, kv], NT), NEG_INF)
            sc = _dotg(qh, kc[:, kv], NT)
            sk = sink_ref[layer, hd]
            m = jnp.maximum(jnp.maximum(jnp.max(sl, axis=1, keepdims=True), jnp.max(sc, axis=1, keepdims=True)), sk)
            el, ec = jnp.exp(sl - m), jnp.exp(sc - m)
            l = jnp.sum(el, axis=1, keepdims=True) + jnp.sum(ec, axis=1, keepdims=True) + jnp.exp(sk - m)
            inv = 1.0 / l
            o = _dot((el * inv).astype(BF16), vw[:, kv]) + _dot((ec * inv).astype(BF16), vc[:, kv])
            cat_ref[:, POOL_W + hd * HEAD:POOL_W + (hd + 1) * HEAD] = o.astype(BF16)
            lses.append(m + jnp.log(l))
        lse_ref[...] = _lane_place(lses)
        ho_ref[...] = h_ref[...] + m_ref[5:6, :] * _dot(cat_ref[...], wo_ref[...])

    blk = lambda cols: _rows(cols, BLK)
    return pl.pallas_call(
        body, name=name, grid=(s // BLK,),
        in_specs=[blk(D), blk(ATTN_W), _whole((s, KV_W)), _whole((s, KV_W)), _whole((s, POOL_W)),
                  _whole((len(POOL_WINDOWS), GROUP, GROUP), (layer,)), _whole((1, POOL_W), (layer,)),
                  pl.BlockSpec(memory_space=pltpu.SMEM), _whole((POOL_W + ATTN_W, D), (layer,)),
                  _mods_spec(layer, n_lat_blk)],
        out_specs=[blk(D), blk(POOL_W + ATTN_W), blk(BLK)],
        out_shape=[jax.ShapeDtypeStruct((s, D), F32), jax.ShapeDtypeStruct((s, POOL_W + ATTN_W), BF16),
                   jax.ShapeDtypeStruct((s, BLK), F32)],
        compiler_params=_params("parallel"),
    )(h, q, k, v, u, w_pool, pool_scale, sink, w_out, mods)


def mix_bwd(dh, cat, q, k, v, u, lse, w_pool, pool_scale, sink, w_out, mods, layer, t, name):
    s = dh.shape[0]
    n_lat_blk = t // BLK
    n_grp = len(POOL_WINDOWS)

    def body(dh_ref, cat_ref, q_ref, k_ref, v_ref, u_ref, lse_ref, wp_ref, ps_ref, sink_ref, wo_ref, m_ref,
             dq_ref, dk_ref, dv_ref, du_ref, dmo_ref, dwp_ref, dps_ref, dsink_ref, dm_ref):
        i = pl.program_id(0)

        @pl.when(i == 0)
        def _():
            for ref in (dk_ref, dv_ref, du_ref, dwp_ref, dps_ref, dsink_ref):
                ref[...] = jnp.zeros_like(ref)

        @pl.when((i == 0) | (i == n_lat_blk))
        def _():
            dm_ref[...] = jnp.zeros_like(dm_ref)

        ws = _window(i, n_lat_blk)
        here = pl.ds(pl.multiple_of(i * BLK, BLK), BLK)
        is_lat = i < n_lat_blk
        seq_lo, seq_hi = jnp.where(is_lat, 0, t), jnp.where(is_lat, t, s)
        dho = dh_ref[...]
        dm_ref[2:3, :] += _sum0(dho * _dot(cat_ref[...], wo_ref[...]))
        dmo = (m_ref[5:6, :] * dho).astype(BF16)
        dmo_ref[...] = dmo
        dcat = _dotg(dmo, wo_ref[...], NT)

        for gi in range(n_grp):
            cols = slice(gi * GROUP, (gi + 1) * GROUP)
            pooled = _pooled(u_ref, i, ws, seq_lo, seq_hi, gi).astype(BF16)
            dpo = dcat[:, cols]
            dps_ref[0:1, cols] += _sum0(dpo * _dot(pooled, wp_ref[gi]))
            dmixed = (dpo * ps_ref[:, cols]).astype(BF16)
            dwp_ref[gi] += _dotg(pooled, dmixed, TN)
            dpooled = _dotg(dmixed, wp_ref[gi], NT)
            band_t, cnt = _pool_band(i, ws, POOL_WINDOWS[gi], seq_lo, seq_hi, True)
            du_ref[pl.ds(ws, 3 * BLK), cols] += _split_dot(band_t, dpooled / cnt)
            du_ref[here, cols] -= dpooled

        valid = _local_valid(i, ws, t)
        kw, vw = k_ref[pl.ds(ws, 3 * BLK), :], v_ref[pl.ds(ws, 3 * BLK), :]
        kc, vc = k_ref[t:s, :], v_ref[t:s, :]
        dqs, dsinks = [], []
        dkw, dvw, dkc, dvc = [], [], [], []
        for hd in range(N_HEADS):
            kv = _head_cols(hd // Q_GROUP)
            qh = q_ref[:, _head_cols(hd)]
            lse_h = lse_ref[:, hd:hd + 1]
            pl_ = jnp.exp(jnp.where(valid, _dotg(qh, kw[:, kv], NT), NEG_INF) - lse_h)
            pc = jnp.exp(_dotg(qh, kc[:, kv], NT) - lse_h)
            do = dcat[:, POOL_W + hd * HEAD:POOL_W + (hd + 1) * HEAD].astype(BF16)
            dpl, dpc = _dotg(do, vw[:, kv], NT), _dotg(do, vc[:, kv], NT)
            delta = jnp.sum(pl_ * dpl, axis=1, keepdims=True) + jnp.sum(pc * dpc, axis=1, keepdims=True)
            dsl, dsc = (pl_ * (dpl - delta)).astype(BF16), (pc * (dpc - delta)).astype(BF16)
            dsinks.append(_sum0(-jnp.exp(sink_ref[layer, hd] - lse_h) * delta))
            dqs.append(_dot(dsl, kw[:, kv]) + _dot(dsc, kc[:, kv]))
            parts = (_dotg(dsl, qh, TN), _dotg(pl_.astype(BF16), do, TN), _dotg(dsc, qh, TN), _dotg(pc.astype(BF16), do, TN))
            for acc, part in zip((dkw, dvw, dkc, dvc), parts):
                if hd % Q_GROUP == 0:
                    acc.append(part)
                else:
                    acc[-1] = acc[-1] + part
        dq_ref[...] = jnp.concatenate(dqs, axis=1)
        dk_ref[pl.ds(ws, 3 * BLK), :] += jnp.concatenate(dkw, axis=1)
        dv_ref[pl.ds(ws, 3 * BLK), :] += jnp.concatenate(dvw, axis=1)
        dk_ref[t:s, :] += jnp.concatenate(dkc, axis=1)
        dv_ref[t:s, :] += jnp.concatenate(dvc, axis=1)
        dsink_ref[0:1, :] += _lane_place(dsinks)

    blk = lambda cols: _rows(cols, BLK)
    full = lambda shape: pl.BlockSpec(shape, lambda i: (0,) * len(shape))
    return pl.pallas_call(
        body, name=name, grid=(s // BLK,),
        in_specs=[blk(D), blk(POOL_W + ATTN_W), blk(ATTN_W), _whole((s, KV_W)), _whole((s, KV_W)), _whole((s, POOL_W)),
                  blk(BLK), _whole((n_grp, GROUP, GROUP), (layer,)), _whole((1, POOL_W), (layer,)),
                  pl.BlockSpec(memory_space=pltpu.SMEM), _whole((POOL_W + ATTN_W, D), (layer,)),
                  _mods_spec(layer, n_lat_blk)],
        out_specs=[blk(ATTN_W), full((s, KV_W)), full((s, KV_W)), full((s, POOL_W)), blk(D),
                   full((n_grp, GROUP, GROUP)), full((8, POOL_W)), full((8, BLK)), _acc_spec(n_lat_blk)],
        out_shape=[jax.ShapeDtypeStruct((s, ATTN_W), F32), jax.ShapeDtypeStruct((s, KV_W), F32),
                   jax.ShapeDtypeStruct((s, KV_W), F32), jax.ShapeDtypeStruct((s, POOL_W), F32),
                   jax.ShapeDtypeStruct((s, D), BF16), jax.ShapeDtypeStruct((n_grp, GROUP, GROUP), F32),
                   jax.ShapeDtypeStruct((8, POOL_W), F32), jax.ShapeDtypeStruct((8, BLK), F32),
                   jax.ShapeDtypeStruct((2, 8, D), F32)],
        compiler_params=_params("arbitrary"),
    )(dh, cat, q, k, v, u, lse, w_pool, pool_scale, sink, w_out, mods)


def loss_head(h, target, g, t, name):
    s = h.shape[0]
    n_lat = t // TM

    def body(h_ref, t_ref, g_ref, dh_ref, acc_ref):
        i = pl.program_id(0)

        @pl.when(i == 0)
        def _():
            acc_ref[...] = jnp.zeros_like(acc_ref)

        @pl.when(i < n_lat)
        def _():
            hh, gg = h_ref[...], g_ref[...]
            r = lax.rsqrt(jnp.mean(hh * hh, axis=-1, keepdims=True) + EPS)
            xhat = hh * r
            err = xhat * gg - t_ref[...]
            dy = err * (1.0 / D)
            dx = dy * gg
            dh_ref[...] = r * (dx - xhat * jnp.mean(dx * xhat, axis=-1, keepdims=True))
            acc_ref[0:1, :] += _sum0(dy * xhat)
            acc_ref[1:2, :] += _sum0(err * err)

        @pl.when(i >= n_lat)
        def _():
            dh_ref[...] = jnp.zeros_like(dh_ref)

    return pl.pallas_call(
        body, name=name, grid=(s // TM,),
        in_specs=[_rows(D), pl.BlockSpec((TM, D), lambda i: (jnp.minimum(i, n_lat - 1), 0)), _whole((1, D))],
        out_specs=[_rows(D), pl.BlockSpec((8, D), lambda i: (0, 0))],
        out_shape=[jax.ShapeDtypeStruct((s, D), F32), jax.ShapeDtypeStruct((8, D), F32)],
        compiler_params=_params("arbitrary"),
    )(h, target, g)


def mod_rows(c_all, w_mod, b_cols, name):
    def body(c_ref, w_ref, b_ref, o_ref):
        cc = c_ref[...]
        o_ref[...] = _dot((cc * jax.nn.sigmoid(cc)).astype(BF16), w_ref[...].astype(BF16)) + b_ref[...]

    return pl.pallas_call(
        body, name=name, grid=(2,),
        in_specs=[pl.BlockSpec((16, D), lambda l: (0, 0)), pl.BlockSpec((None, D, MOD_COLS), lambda l: (l, 0, 0)),
                  pl.BlockSpec((None, 1, MOD_COLS), lambda l: (l, 0, 0))],
        out_specs=pl.BlockSpec((None, 16, MOD_COLS), lambda l: (l, 0, 0)),
        out_shape=jax.ShapeDtypeStruct((2, 16, MOD_COLS), F32),
        compiler_params=_params("parallel"),
    )(c_all, w_mod, b_cols)


def mod_grads(c_all, dmod_cols, w_mod, name):
    def body(c_ref, d_ref, w_ref, dw_ref, dc_ref):
        @pl.when(pl.program_id(0) == 0)
        def _():
            dc_ref[...] = jnp.zeros_like(dc_ref)

        cc = c_ref[...]
        dd = d_ref[...].astype(BF16)
        dw_ref[...] = _dotg((cc * jax.nn.sigmoid(cc)).astype(BF16), dd, TN)
        dc_ref[...] += _dotg(dd, w_ref[...].astype(BF16), NT)

    return pl.pallas_call(
        body, name=name, grid=(2,),
        in_specs=[pl.BlockSpec((16, D), lambda l: (0, 0)), pl.BlockSpec((None, 16, MOD_COLS), lambda l: (l, 0, 0)),
                  pl.BlockSpec((None, D, MOD_COLS), lambda l: (l, 0, 0))],
        out_specs=[pl.BlockSpec((None, D, MOD_COLS), lambda l: (l, 0, 0)), pl.BlockSpec((16, D), lambda l: (0, 0))],
        out_shape=[jax.ShapeDtypeStruct((2, D, MOD_COLS), F32), jax.ShapeDtypeStruct((16, D), F32)],
        compiler_params=_params("arbitrary"),
    )(c_all, dmod_cols, w_mod)


def _row_tile(rows, cols, n_arrays):
    budget = VMEM_LIMIT_BYTES // 4 // (2 * 4 * n_arrays * cols)
    best = None
    for tr in range(16, rows + 1, 16):
        if rows % tr == 0 and tr <= budget:
            best = tr
    return best if best is not None else rows


def elementwise(fn, ins, out_dtypes, name):
    rows, cols = ins[0].shape
    tr = _row_tile(rows, cols, len(ins) + len(out_dtypes))

    def body(*refs):
        outs = fn(*[r[...] for r in refs[:len(ins)]])
        for o_ref, o in zip(refs[len(ins):], outs):
            o_ref[...] = o.astype(o_ref.dtype)

    spec = pl.BlockSpec((tr, cols), lambda i: (i, 0))
    return pl.pallas_call(
        body, name=name, grid=(rows // tr,), in_specs=[spec] * len(ins), out_specs=[spec] * len(out_dtypes),
        out_shape=[jax.ShapeDtypeStruct((rows, cols), dt) for dt in out_dtypes],
        compiler_params=_params("parallel"),
    )(*ins)


def _adamw_tile(w, g, m, v):
    m = ADAM_B1 * m + (1.0 - ADAM_B1) * g
    v = ADAM_B2 * v + (1.0 - ADAM_B2) * (g * g)
    m_hat = m / (1.0 - ADAM_B1 ** ADAM_STEP)
    v_hat = v / (1.0 - ADAM_B2 ** ADAM_STEP)
    return -ADAM_LR * (m_hat / (jnp.sqrt(v_hat) + ADAM_EPS) + ADAM_WD * w), m, v


def adamw(w, g, m, v, name):
    shape = w.shape
    two_d = (-1, shape[-1]) if w.ndim > 1 else (1, -1)
    outs = elementwise(_adamw_tile, [a.reshape(two_d) for a in (w, g, m, v)], [F32] * 3, name)
    return [o.reshape(shape) for o in outs]


def _prefetch_call(body, name, grid, in_specs, out_specs, out_shape, place, args, aliases=None):
    spec = pltpu.PrefetchScalarGridSpec(num_scalar_prefetch=1, grid=grid, in_specs=in_specs, out_specs=out_specs)
    return pl.pallas_call(body, name=name, grid_spec=spec, out_shape=out_shape, input_output_aliases=aliases or {},
                          compiler_params=_params(*["parallel"] * len(grid)))(place, *args)


def cast_place(w, place, name):
    _, r, c = w.shape
    tr = _row_tile(r, c, 2)

    def body(p_ref, w_ref, o_ref):
        o_ref[...] = w_ref[...].astype(BF16)

    return _prefetch_call(
        body, name, (2, r // tr), [pl.BlockSpec((None, tr, c), lambda l, i, p: (l, i, 0))],
        pl.BlockSpec((None, None, tr, c), lambda l, i, p: (l, p[1], i, 0)), jax.ShapeDtypeStruct((2, N_SLOT, r, c), BF16),
        place, [w])


def pair_sum(g32, got, place, name):
    n_slot, rh, c = got.shape
    tr = _row_tile(rh, c, 4)
    per = rh // tr

    def body(p_ref, a_ref, b_ref, o_ref, o16_ref):
        r = a_ref[...] + b_ref[...].astype(F32)
        o_ref[...] = r
        o16_ref[...] = r.astype(BF16)

    half = pl.BlockSpec((None, tr, c), lambda s, i, p: (s, i, 0))
    return _prefetch_call(
        body, name, (n_slot, per), [pl.BlockSpec((None, tr, c), lambda s, i, p: (s, p[0] * per + i, 0)), half], [half, half],
        [jax.ShapeDtypeStruct(got.shape, F32), jax.ShapeDtypeStruct(got.shape, BF16)], place, [g32, got])


def chip_sum(p32, got, place, name):
    _, rh, c = p32.shape
    tr = _row_tile(rh, c, 5)
    per = rh // tr

    def body(p_ref, m_ref, r0_ref, r1_ref, r2_ref, o_ref):
        o_ref[...] = m_ref[...] + r0_ref[...].astype(F32) + r1_ref[...].astype(F32) + r2_ref[...].astype(F32)

    part = pl.BlockSpec((tr, c), lambda i, p: (i, 0))
    return _prefetch_call(
        body, name, (per,), [pl.BlockSpec((None, tr, c), lambda i, p: (p[1], i, 0)), part, part, part],
        pl.BlockSpec((tr, c), lambda i, p: (p[0] * per + i, 0)), jax.ShapeDtypeStruct((2 * rh, c), F32), place, [p32, *got])


def adamw_layers(w, g0, g1, m, v, name):
    _, r, c = w.shape
    tr = _row_tile(r, c, 10)

    def body(w_ref, g0_ref, g1_ref, m_ref, v_ref, g_ref, d_ref, mo_ref, vo_ref):
        g = jnp.where(pl.program_id(0) == 0, g0_ref[...], g1_ref[...])
        g_ref[...] = g
        d_ref[...], mo_ref[...], vo_ref[...] = _adamw_tile(w_ref[...], g, m_ref[...], v_ref[...])

    stacked = pl.BlockSpec((None, tr, c), lambda l, i: (l, i, 0))
    layer = pl.BlockSpec((tr, c), lambda l, i: (i, 0))
    return pl.pallas_call(
        body, name=name, grid=(2, r // tr), in_specs=[stacked, layer, layer, stacked, stacked], out_specs=[stacked] * 4,
        out_shape=[jax.ShapeDtypeStruct(w.shape, F32)] * 4, compiler_params=_params("parallel", "parallel"),
    )(w, g0, g1, m, v)


def sum8(gathered, name):
    def body(*refs):
        n = len(refs) // 2
        for g_ref, o_ref in zip(refs[:n], refs[n:]):
            acc = g_ref[0]
            for dev in range(1, N_DEV):
                acc = acc + g_ref[dev]
            o_ref[...] = acc

    return pl.pallas_call(
        body, name=name,
        out_shape=[jax.ShapeDtypeStruct(a.shape[1:], F32) for a in gathered],
        compiler_params=_params(),
    )(*gathered)


def _place():
    return lax.axis_index("x"), lax.axis_index("y"), lax.axis_index("c")


def _any(n):
    return [pl.BlockSpec(memory_space=pl.ANY)] * n


def all_gather(blocks, name):
    n = len(blocks)

    def body(*refs):
        ins, outs = refs[:n], refs[n:2 * n]
        send_sems, recv_sems, local_sems = refs[2 * n:]
        x, y, c = _place()
        me, sibling = (x, y, c), (x, y, 1 - c)
        chips = [(1 - x, y), (x, 1 - y), (1 - x, 1 - y)]

        def copy(ti, k, block, to, src=None):
            dst = outs[ti].at[4 * block[0] + 2 * block[1] + block[2]]
            return pltpu.make_async_remote_copy(src_ref=dst if src is None else src, dst_ref=dst, send_sem=send_sems.at[ti, k],
                                                recv_sem=recv_sems.at[ti, k], device_id=to, device_id_type=MESH)

        local, sent = [], []
        for ti in range(n):
            local.append(pltpu.make_async_copy(ins[ti], outs[ti].at[4 * x + 2 * y + c], local_sems.at[ti]))
            sent.append(copy(ti, 0, me, sibling, src=ins[ti]))
            sent += [copy(ti, 1 + j, me, (*chip, c), src=ins[ti]) for j, chip in enumerate(chips)]
        for cp in local + sent:
            cp.start()
        for ti in range(n):
            for j, chip in enumerate(chips):
                copy(ti, 1 + j, (*chip, c), me).wait_recv()
                sent.append(copy(ti, 4 + j, (*chip, c), sibling))
                sent[-1].start()
        for ti in range(n):
            copy(ti, 0, sibling, me).wait_recv()
            for j, chip in enumerate(chips):
                copy(ti, 4 + j, (*chip, 1 - c), me).wait_recv()
        for cp in sent:
            cp.wait_send()
        for cp in local:
            cp.wait()

    return pl.pallas_call(
        body, name=name, in_specs=_any(n), out_specs=_any(n),
        out_shape=[jax.ShapeDtypeStruct((N_DEV,) + b.shape, b.dtype) for b in blocks],
        scratch_shapes=[pltpu.SemaphoreType.DMA((n, 7)), pltpu.SemaphoreType.DMA((n, 7)), pltpu.SemaphoreType.DMA((n,))],
    )(*blocks)


def _three_chips(x, y):
    return [(1 - x, y), (x, 1 - y), (1 - x, 1 - y)]


def gather_weights(placed, name):
    n = len(placed)

    def body(*refs):
        bufs = refs[n:2 * n]
        send_sems, recv_sems = refs[2 * n:]
        x, y, c = _place()
        sibling = (x, y, 1 - c)
        chips = _three_chips(x, y)

        def copy(ti, k, chip, core, to):
            rh = bufs[ti].shape[2] // 2
            half = bufs[ti].at[:, 2 * chip[0] + chip[1], pl.ds(core * rh, rh), :]
            return pltpu.make_async_remote_copy(src_ref=half, dst_ref=half, send_sem=send_sems.at[ti, k],
                                                recv_sem=recv_sems.at[ti, k], device_id=to, device_id_type=MESH)

        sent = [copy(ti, k, (x, y), c, (*chip, c)) for ti in range(n) for k, chip in enumerate(chips)]
        for cp in sent:
            cp.start()
        for ti in range(n):
            for k, chip in enumerate(chips):
                copy(ti, k, chip, c, (x, y, c)).wait_recv()
                sent.append(copy(ti, 3 + k, chip, c, sibling))
                sent[-1].start()
        for ti in range(n):
            for k, chip in enumerate(chips):
                copy(ti, 3 + k, chip, 1 - c, (x, y, c)).wait_recv()
        for cp in sent:
            cp.wait_send()

    return pl.pallas_call(
        body, name=name, in_specs=_any(n), out_specs=_any(n), input_output_aliases={i: i for i in range(n)},
        out_shape=[jax.ShapeDtypeStruct(w.shape, w.dtype) for w in placed],
        scratch_shapes=[pltpu.SemaphoreType.DMA((n, 6)), pltpu.SemaphoreType.DMA((n, 6))],
    )(*placed)


def pair_exchange(g16, name):
    n = len(g16)

    def body(*refs):
        a16, got = refs[:n], refs[n:2 * n]
        send_sems, recv_sems = refs[2 * n:]
        x, y, c = _place()
        copies = []
        for ti in range(n):
            rh = a16[ti].shape[1] // 2
            copies.append(pltpu.make_async_remote_copy(
                src_ref=a16[ti].at[:, pl.ds((1 - c) * rh, rh), :], dst_ref=got[ti], send_sem=send_sems.at[ti],
                recv_sem=recv_sems.at[ti], device_id=(x, y, 1 - c), device_id_type=MESH))
        for cp in copies:
            cp.start()
        for cp in copies:
            cp.wait()

    return pl.pallas_call(
        body, name=name, in_specs=_any(n), out_specs=_any(n),
        out_shape=[jax.ShapeDtypeStruct((a.shape[0], a.shape[1] // 2, a.shape[2]), BF16) for a in g16],
        scratch_shapes=[pltpu.SemaphoreType.DMA((n,)), pltpu.SemaphoreType.DMA((n,))],
    )(*g16)


def chip_scatter(p16, name):
    n = len(p16)

    def body(*refs):
        a16, got = refs[:n], refs[n:4 * n]
        send_sems, recv_sems = refs[4 * n:]
        x, y, c = _place()
        copies = []
        for ti in range(n):
            for k, chip in enumerate(_three_chips(x, y)):
                copies.append(pltpu.make_async_remote_copy(
                    src_ref=a16[ti].at[2 * chip[0] + chip[1]], dst_ref=got[3 * ti + k], send_sem=send_sems.at[ti, k],
                    recv_sem=recv_sems.at[ti, k], device_id=(*chip, c), device_id_type=MESH))
        for cp in copies:
            cp.start()
        for cp in copies:
            cp.wait()

    return pl.pallas_call(
        body, name=name, in_specs=_any(n), out_specs=_any(3 * n),
        out_shape=[jax.ShapeDtypeStruct(a.shape[1:], BF16) for a in p16 for _ in range(3)],
        scratch_shapes=[pltpu.SemaphoreType.DMA((n, 3)), pltpu.SemaphoreType.DMA((n, 3))],
    )(*p16)


def pair_gather(halves, name):
    n = len(halves)

    def body(*refs):
        bufs = refs[n:2 * n]
        send_sems, recv_sems = refs[2 * n:]
        x, y, c = _place()
        copies = []
        for ti in range(n):
            rh = bufs[ti].shape[0] // 2
            rows = bufs[ti].at[pl.ds(c * rh, rh), :]
            copies.append(pltpu.make_async_remote_copy(src_ref=rows, dst_ref=rows, send_sem=send_sems.at[ti],
                                                       recv_sem=recv_sems.at[ti], device_id=(x, y, 1 - c), device_id_type=MESH))
        for cp in copies:
            cp.start()
        for ti, cp in enumerate(copies):
            cp.wait_send()
            rh = bufs[ti].shape[0] // 2
            theirs = bufs[ti].at[pl.ds((1 - c) * rh, rh), :]
            pltpu.make_async_remote_copy(src_ref=theirs, dst_ref=theirs, send_sem=send_sems.at[ti], recv_sem=recv_sems.at[ti],
                                         device_id=(x, y, 1 - c), device_id_type=MESH).wait_recv()

    return pl.pallas_call(
        body, name=name, in_specs=_any(n), out_specs=_any(n), input_output_aliases={i: i for i in range(n)},
        out_shape=[jax.ShapeDtypeStruct(a.shape, a.dtype) for a in halves],
        scratch_shapes=[pltpu.SemaphoreType.DMA((n,)), pltpu.SemaphoreType.DMA((n,))],
    )(*halves)


def reduce_small(dml, loss_blk, name):
    def body(d_ref, l_ref, tot_ref, rows_ref, fin_ref):
        rows_ref[...] = jnp.zeros_like(rows_ref)
        for l in range(2):
            ctx = d_ref[0, l, 1]
            lat = d_ref[0, l, 0]
            rows_ref[l, 0] = lat
            for dev in range(1, N_DEV):
                rows_ref[l, dev] = d_ref[dev, l, 0]
                ctx = ctx + d_ref[dev, l, 1]
                lat = lat + d_ref[dev, l, 0]
            rows_ref[l, N_DEV] = ctx
            tot_ref[l] = lat + ctx
        acc = l_ref[0]
        for dev in range(1, N_DEV):
            acc = acc + l_ref[dev]
        loss = (0.5 / D) * jnp.sum(acc[1:2, :], axis=1, keepdims=True)
        row = lax.broadcasted_iota(jnp.int32, (8, D), 0)
        fin_ref[...] = jnp.where(row == 0, acc[0:1, :], loss)

    return pl.pallas_call(
        body, name=name,
        out_shape=[jax.ShapeDtypeStruct((2, 16, D), F32), jax.ShapeDtypeStruct((2, 16, 16, D), F32),
                   jax.ShapeDtypeStruct((8, D), F32)],
        compiler_params=_params(),
    )(dml, loss_blk)


def rope_tables(t, s):
    rows = t // GRID_W
    row = jnp.repeat(jnp.arange(rows), GRID_W).astype(F32)
    col = jnp.tile(jnp.arange(GRID_W), rows).astype(F32)
    inv = ROPE_BASE ** (-jnp.arange(0, HEAD // 2, 2, dtype=F32) / (HEAD // 2))
    ang = jnp.concatenate([row[:, None] * inv, col[:, None] * inv], axis=-1)
    cos, sin = jnp.cos(ang), jnp.sin(ang)
    cos = jnp.concatenate([jnp.tile(cos, (1, 4)), jnp.ones((s - t, BLK), F32)], axis=0)
    sin = jnp.concatenate([jnp.tile(jnp.concatenate([-sin, sin], axis=1), (1, 2)), jnp.zeros((s - t, BLK), F32)], axis=0)
    return cos, sin


def local_step(x1, ctx1, target, mods, norms, nfinal, wts, w_pool, pool_scale, sink):
    t, s = x1.shape[0], x1.shape[0] + ctx1.shape[0]
    n_lat = t // TM
    cos, sin = rope_tables(t, s)
    h = jnp.concatenate([x1, ctx1], axis=0)
    saved = []
    for l in range(2):
        h0 = h
        h1, ab1, f1 = ffn_fwd(h0, mods, norms[0], wts["ffn1_in"], wts["ffn1_out"], l, 0, n_lat, f"ffn1_fwd_{l}")
        u, q, k, v = proj_fwd(h1, mods, norms[1], wts["w_in"], cos, sin, l, n_lat, f"proj_fwd_{l}")
        h2, cat, lse = mix_fwd(h1, q, k, v, u, w_pool, pool_scale, sink, wts["w_out"], mods, l, t, f"mix_fwd_{l}")
        h, ab2, f2 = ffn_fwd(h2, mods, norms[2], wts["ffn2_in"], wts["ffn2_out"], l, 6, n_lat, f"ffn2_fwd_{l}")
        saved.append((h0, ab1, f1, h1, u, q, k, v, cat, lse, h2, ab2, f2))
    dh, loss_blk = loss_head(h, target, nfinal, t, "loss_head")
    big, small = [None, None], [None, None]
    for l in (1, 0):
        h0, ab1, f1, h1, u, q, k, v, cat, lse, h2, ab2, f2 = saved[l]
        dh, dab, df, n, act, dm_f2 = ffn_bwd(h2, ab2, f2, dh, mods, norms[2], wts["ffn2_in"], wts["ffn2_out"], l, 6, n_lat,
                                             f"ffn2_bwd_{l}")
        g_f2i = wgrad(n, dab, D, FF_COLS, FF_COLS, f"ffn2_in_wgrad_{l}")
        g_f2o = wgrad(act, df, D_FF // 2, D, None, f"ffn2_out_wgrad_{l}")
        dq, dk, dv, du, dmo, dwp, dps, dsink, dm_gate = mix_bwd(dh, cat, q, k, v, u, lse, w_pool, pool_scale, sink,
                                                               wts["w_out"], mods, l, t, f"mix_bwd_{l}")
        g_wo = wgrad(cat, dmo, POOL_W + ATTN_W, D, None, f"w_out_wgrad_{l}")
        dh, dp, n, dm_mix = proj_bwd(h1, du, dq, dk, dv, dh, mods, norms[1], wts["w_in"], cos, sin, l, n_lat, f"proj_bwd_{l}")
        g_wi = wgrad(n, dp, D, PROJ_W // 2, None, f"w_in_wgrad_{l}")
        dh, dab, df, n, act, dm_f1 = ffn_bwd(h0, ab1, f1, dh, mods, norms[0], wts["ffn1_in"], wts["ffn1_out"], l, 0, n_lat,
                                             f"ffn1_bwd_{l}")
        g_f1i = wgrad(n, dab, D, FF_COLS, FF_COLS, f"ffn1_in_wgrad_{l}")
        g_f1o = wgrad(act, df, D_FF // 2, D, None, f"ffn1_out_wgrad_{l}")
        big[l] = dict(ffn1_in=g_f1i, ffn1_out=g_f1o, w_in=g_wi, w_out=g_wo, ffn2_in=g_f2i, ffn2_out=g_f2o)
        dml = jnp.concatenate([dm_f1[:, 0:3], dm_mix[:, 0:2], dm_gate[:, 2:3], dm_f2[:, 0:3],
                               dm_f1[:, 3:4], dm_mix[:, 3:4], dm_f2[:, 3:4], jnp.zeros((2, 4, D), F32)], axis=1)
        small[l] = dict(dml=dml, dwp=dwp, dps=dps, dsink=dsink)
    return dh[:t], loss_blk, big, small


BIG = ("ffn1_in", "ffn1_out", "w_in", "w_out", "ffn2_in", "ffn2_out")


def _slot_major(name, g):
    if name == "w_in":
        return jnp.stack(jnp.split(g, N_SLOT, axis=1), axis=0)
    if name in ("ffn1_in", "ffn2_in"):
        return g
    return g.reshape(N_SLOT, g.shape[0] // N_SLOT, g.shape[1])


def reduce_scatter(big, place):
    g32 = [_slot_major(name, big[l][name][0]) for name in BIG for l in range(2)]
    g16 = [_slot_major(name, big[l][name][1]) for name in BIG for l in range(2)]
    n = len(g32)
    got = pair_exchange(g16, "grad_pair_exchange")
    pairs = [pair_sum(g32[ti], got[ti], place, f"grad_pair_sum_{ti}") for ti in range(n)]
    got = chip_scatter([p16 for _, p16 in pairs], "grad_chip_scatter")
    halves = [chip_sum(pairs[ti][0], got[3 * ti:3 * ti + 3], place, f"grad_chip_sum_{ti}") for ti in range(n)]
    return pair_gather(halves, "grad_pair_gather")


def _silu_grad(z):
    sg = jax.nn.sigmoid(z)
    return sg * (1 + z * (1 - sg))


def kernel(x, c, ctx, c_ctx, w_mod, b_mod, norm_ffn1, w_ffn1_in, w_ffn1_out, norm_mix, w_in, w_pool, pool_scale, sink, w_out, norm_ffn2, w_ffn2_in, w_ffn2_out, norm_final, loss_target, m_c_ctx, m_w_mod, m_b_mod, m_norm_ffn1, m_w_ffn1_in, m_w_ffn1_out, m_norm_mix, m_w_in, m_w_pool, m_pool_scale, m_sink, m_w_out, m_norm_ffn2, m_w_ffn2_in, m_w_ffn2_out, m_norm_final, v_c_ctx, v_w_mod, v_b_mod, v_norm_ffn1, v_w_ffn1_in, v_w_ffn1_out, v_norm_mix, v_w_in, v_w_pool, v_pool_scale, v_sink, v_w_out, v_norm_ffn2, v_w_ffn2_in, v_w_ffn2_out, v_norm_final):
    px, py, pc = _place()
    slot, me = 2 * px + py, 4 * px + 2 * py + pc
    n_grp = len(POOL_WINDOWS)

    (c_rows,) = all_gather([c.reshape(8, D // 8)], "gather_c")
    c_all = jnp.concatenate([c_rows.reshape(N_DEV, D), c_ctx.reshape(1, D), jnp.zeros((16 - N_DEV - 1, D), F32)], axis=0)
    b_cols = lax.dynamic_slice(b_mod, (0, slot * MOD_COLS), (2, MOD_COLS)).reshape(2, 1, MOD_COLS)
    (mod_parts,) = all_gather([mod_rows(c_all, w_mod, b_cols, "mod_rows")], "gather_mods")
    mods_all = mod_parts[0::2].transpose(1, 2, 0, 3).reshape(2, 16, N_MOD * D)
    mx = lax.dynamic_slice(mods_all, (0, me, 0), (2, 1, N_MOD * D)).reshape(2, N_MOD, D)
    mc = mods_all[:, N_DEV].reshape(2, N_MOD, D)
    pad = jnp.zeros((2, 16 - N_MOD, D), F32)
    mods = jnp.stack([jnp.concatenate([mx, pad], axis=1), jnp.concatenate([mc, pad], axis=1)], axis=1)

    place = jnp.stack([pc, slot]).astype(jnp.int32)
    shards = dict(ffn1_in=w_ffn1_in, ffn1_out=w_ffn1_out, w_in=w_in, w_out=w_out, ffn2_in=w_ffn2_in, ffn2_out=w_ffn2_out)
    whole = dict(zip(BIG, gather_weights([cast_place(shards[name], place, f"cast_{name}") for name in BIG], "gather_weights")))
    wts = dict(ffn1_in=whole["ffn1_in"], ffn2_in=whole["ffn2_in"],
               ffn1_out=whole["ffn1_out"].reshape(2, D_FF, D), ffn2_out=whole["ffn2_out"].reshape(2, D_FF, D),
               w_in=whole["w_in"].transpose(0, 2, 1, 3).reshape(2, D, PROJ_W), w_out=whole["w_out"].reshape(2, POOL_W + ATTN_W, D))

    norms = [g.reshape(2, 1, D) for g in (norm_ffn1, norm_mix, norm_ffn2)]
    dx, loss_blk, big, small = local_step(x[0], ctx[0], loss_target[0], mods, norms, norm_final.reshape(1, D), wts,
                                          w_pool.astype(BF16), pool_scale.reshape(2, 1, POOL_W), sink)

    big_grads = dict(zip(("w_ffn1_in", "w_ffn1_out", "w_in", "w_out", "w_ffn2_in", "w_ffn2_out"),
                         zip(*[iter(reduce_scatter(big, place))] * 2)))
    grads = {}

    stacked = {k: jnp.stack([small[0][k], small[1][k]]) for k in ("dml", "dwp", "dps", "dsink")}
    g_dml, g_dwp, g_dps, g_dsink, g_loss = all_gather(
        [stacked["dml"].reshape(64, D), stacked["dwp"].reshape(2 * n_grp * GROUP, GROUP), stacked["dps"].reshape(16, POOL_W),
         stacked["dsink"].reshape(16, BLK), loss_blk], "gather_small")
    tot, rows, fin = reduce_small(g_dml.reshape(N_DEV, 2, 2, 16, D), g_loss, "reduce_small")
    s_dwp, s_dps, s_dsink = sum8([g_dwp, g_dps, g_dsink], "sum_pool_sink")
    grads.update(
        w_pool=s_dwp.reshape(2, n_grp, GROUP, GROUP), pool_scale=s_dps.reshape(2, 8, POOL_W)[:, 0],
        sink=s_dsink.reshape(2, 8, BLK)[:, 0, :N_HEADS], b_mod=tot[:, :N_MOD].reshape(2, N_MOD * D),
        norm_ffn1=tot[:, N_MOD], norm_mix=tot[:, N_MOD + 1], norm_ffn2=tot[:, N_MOD + 2], norm_final=fin[0])
    loss = fin[1, 0]

    dmod_cols = lax.dynamic_slice(rows[:, :, :N_MOD, :].reshape(2, 16, N_MOD * D), (0, 0, slot * MOD_COLS), (2, 16, MOD_COLS))
    grads["w_mod"], dc = mod_grads(c_all, dmod_cols, w_mod, "mod_grads")
    (g_dc,) = all_gather([dc], "gather_dc")
    (s_dc,) = sum8([g_dc], "sum_dc")
    (d_c_ctx,) = elementwise(lambda d, z: (0.5 * d * _silu_grad(z),), [s_dc[N_DEV:N_DEV + 1], c_ctx.reshape(1, D)], [F32], "c_ctx_grad")
    grads["c_ctx"] = d_c_ctx.reshape(D)

    given = dict(c_ctx=(c_ctx, m_c_ctx, v_c_ctx), w_mod=(w_mod, m_w_mod, v_w_mod), b_mod=(b_mod, m_b_mod, v_b_mod),
                 norm_ffn1=(norm_ffn1, m_norm_ffn1, v_norm_ffn1), w_ffn1_in=(w_ffn1_in, m_w_ffn1_in, v_w_ffn1_in),
                 w_ffn1_out=(w_ffn1_out, m_w_ffn1_out, v_w_ffn1_out), norm_mix=(norm_mix, m_norm_mix, v_norm_mix),
                 w_in=(w_in, m_w_in, v_w_in), w_pool=(w_pool, m_w_pool, v_w_pool),
                 pool_scale=(pool_scale, m_pool_scale, v_pool_scale), sink=(sink, m_sink, v_sink), w_out=(w_out, m_w_out, v_w_out),
                 norm_ffn2=(norm_ffn2, m_norm_ffn2, v_norm_ffn2), w_ffn2_in=(w_ffn2_in, m_w_ffn2_in, v_w_ffn2_in),
                 w_ffn2_out=(w_ffn2_out, m_w_ffn2_out, v_w_ffn2_out), norm_final=(norm_final, m_norm_final, v_norm_final))
    g_out, d_out, m_out, v_out = [], [], [], []
    for name, (w, m, v) in given.items():
        if name in big_grads:
            grad, delta, new_m, new_v = adamw_layers(w, *big_grads[name], m, v, f"adamw_{name}")
        else:
            grad = grads[name]
            delta, new_m, new_v = adamw(w, grad, m, v, f"adamw_{name}")
        g_out.append(grad)
        d_out.append(delta)
        m_out.append(new_m)
        v_out.append(new_v)
    return (loss, dx[None], *g_out, *d_out, *m_out, *v_out)
```

```python
import jax
import jax.numpy as jnp
from jax import lax
from jax.experimental import pallas as pl
from jax.experimental.pallas import tpu as pltpu

F32, BF16 = jnp.float32, jnp.bfloat16
D = 1024
D_FF = 2816
N_SLOT = 4
FF_COLS = 2 * D_FF // N_SLOT
N_MOD = 9
MOD_COLS = N_MOD * D // N_SLOT
POOL_W, ATTN_W, KV_W = 512, 512, 128
PROJ_W = POOL_W + ATTN_W + 2 * KV_W
N_HEADS, Q_GROUP, HEAD = 8, 4, 64
GROUP = 128
POOL_WINDOWS = (2, 4, 8, 16)
BLK = 128
GRID_W = 64
ROPE_BASE = 10000.0
EPS = 1e-6
NEG_INF = -1e30
TM = 256
N_DEV = 8
VMEM_LIMIT_BYTES = 56 * 1024 * 1024
ADAM_LR, ADAM_B1, ADAM_B2, ADAM_EPS, ADAM_WD, ADAM_STEP = 0.001, 0.9, 0.999, 1e-08, 0.01, 10
MESH = pl.DeviceIdType.MESH
NT = (((1,), (1,)), ((), ()))
TN = (((0,), (0,)), ((), ()))


def _params(*sem):
    return pltpu.CompilerParams(dimension_semantics=sem, vmem_limit_bytes=VMEM_LIMIT_BYTES)


def _whole(shape, lead=()):
    idx = tuple(lead) + (0,) * len(shape)
    return pl.BlockSpec((None,) * len(lead) + tuple(shape), lambda *_: idx, pipeline_mode=pl.Buffered(1))


def _rows(cols, tm=TM):
    return pl.BlockSpec((tm, cols), lambda i: (i, 0))


def _mods_spec(layer, n_lat):
    return pl.BlockSpec((None, None, 16, D), lambda i: (layer, (i >= n_lat).astype(jnp.int32), 0, 0))


def _acc_spec(n_lat):
    return pl.BlockSpec((None, 8, D), lambda i: ((i >= n_lat).astype(jnp.int32), 0, 0))


def _dot(a, b):
    return jnp.dot(a, b, preferred_element_type=F32)


def _dotg(a, b, dims):
    return lax.dot_general(a, b, dims, preferred_element_type=F32)


def _sum0(v):
    return jnp.sum(v, axis=0, keepdims=True)


def _norm_mod(h, g, shift, scale):
    r = lax.rsqrt(jnp.mean(h * h, axis=-1, keepdims=True) + EPS)
    xhat = h * r
    y = xhat * g
    return y * (1 + scale) + shift, xhat, r, y


def _norm_mod_bwd(dn, xhat, r, y, g, scale):
    dy = dn * (1 + scale)
    dx = dy * g
    dh = r * (dx - xhat * jnp.mean(dx * xhat, axis=-1, keepdims=True))
    return _sum0(dn), _sum0(dn * y), _sum0(dy * xhat), dh


def _swap_halves(v):
    w = v.shape[1]
    lane = lax.broadcasted_iota(jnp.int32, v.shape, 1)
    return jnp.where(lane % HEAD < HEAD // 2, pltpu.roll(v, w - HEAD // 2, axis=1), pltpu.roll(v, HEAD // 2, axis=1))


def _tile_lanes(t, width):
    return t if width == t.shape[1] else jnp.concatenate([t] * (width // t.shape[1]), axis=1)


def _rope(v, cos, sin):
    return v * _tile_lanes(cos, v.shape[1]) + _swap_halves(v) * _tile_lanes(sin, v.shape[1])


def _unrope(g, cos, sin):
    return g * _tile_lanes(cos, g.shape[1]) + _swap_halves(g * _tile_lanes(sin, g.shape[1]))


def ffn_fwd(h, mods, g, w4, wo, layer, k0, n_lat, name, ex=None):
    s = h.shape[0]

    def body(h_ref, m_ref, g_ref, w_ref, wo_ref, ho_ref, ab_ref, f_ref):
        hh = h_ref[...]
        n, _, _, _ = _norm_mod(hh, g_ref[...], m_ref[k0:k0 + 1, :], m_ref[k0 + 1:k0 + 2, :])
        nb = n.astype(BF16)
        acc = jnp.zeros((TM, D), F32)
        for j in range(2):
            a = _dot(nb, w_ref[j])
            b = _dot(nb, w_ref[2 + j])
            ab_ref[:, j * FF_COLS:(j + 1) * FF_COLS] = a.astype(BF16)
            ab_ref[:, (2 + j) * FF_COLS:(3 + j) * FF_COLS] = b.astype(BF16)
            act = (a * jax.nn.sigmoid(a) * b).astype(BF16)
            acc = acc + _dot(act, wo_ref[j * FF_COLS:(j + 1) * FF_COLS, :])
        f_ref[...] = acc
        ho_ref[...] = hh + 0.5 * m_ref[k0 + 2:k0 + 3, :] * acc

    return _grid_call(
        body, name, s // TM,
        [_rows(D), _mods_spec(layer, n_lat), _whole((1, D), (layer,)), _whole((N_SLOT, D, FF_COLS)), _whole((D_FF, D))],
        [_rows(D), _rows(2 * D_FF), _rows(D)],
        [jax.ShapeDtypeStruct((s, D), F32), jax.ShapeDtypeStruct((s, 2 * D_FF), BF16), jax.ShapeDtypeStruct((s, D), F32)],
        (h, mods, g, w4, wo), "parallel", ex)


def ffn_bwd(h, ab, f, dh, mods, g, w4, wo, layer, k0, n_lat, name, ex=None):
    s = h.shape[0]

    def body(h_ref, ab_ref, f_ref, dh_ref, m_ref, g_ref, w_ref, wo_ref, dhi_ref, dab_ref, df_ref, n_ref, act_ref, dm_ref):
        i = pl.program_id(0)

        @pl.when((i == 0) | (i == n_lat))
        def _():
            dm_ref[...] = jnp.zeros_like(dm_ref)

        hh, dho, gg = h_ref[...], dh_ref[...], g_ref[...]
        scale, gate = m_ref[k0 + 1:k0 + 2, :], m_ref[k0 + 2:k0 + 3, :]
        n, xhat, r, y = _norm_mod(hh, gg, m_ref[k0:k0 + 1, :], scale)
        n_ref[...] = n.astype(BF16)
        dgate = _sum0(dho * (0.5 * f_ref[...]))
        dfb = ((0.5 * gate) * dho).astype(BF16)
        df_ref[...] = dfb
        dn = jnp.zeros((TM, D), F32)
        for j in range(2):
            a = ab_ref[:, j * FF_COLS:(j + 1) * FF_COLS].astype(F32)
            b = ab_ref[:, (2 + j) * FF_COLS:(3 + j) * FF_COLS].astype(F32)
            sg = jax.nn.sigmoid(a)
            sa = a * sg
            act_ref[:, j * FF_COLS:(j + 1) * FF_COLS] = (sa * b).astype(BF16)
            dact = _dotg(dfb, wo_ref[j * FF_COLS:(j + 1) * FF_COLS, :], NT)
            da = (dact * b * (sg * (1 + a * (1 - sg)))).astype(BF16)
            db = (dact * sa).astype(BF16)
            dab_ref[:, j * FF_COLS:(j + 1) * FF_COLS] = da
            dab_ref[:, (2 + j) * FF_COLS:(3 + j) * FF_COLS] = db
            dn = dn + _dotg(da, w_ref[j], NT) + _dotg(db, w_ref[2 + j], NT)
        dsh, dsc, dg, dhn = _norm_mod_bwd(dn, xhat, r, y, gg, scale)
        dhi_ref[...] = dho + dhn
        dm_ref[0:1, :] += dsh
        dm_ref[1:2, :] += dsc
        dm_ref[2:3, :] += dgate
        dm_ref[3:4, :] += dg

    return _grid_call(
        body, name, s // TM,
        [_rows(D), _rows(2 * D_FF), _rows(D), _rows(D), _mods_spec(layer, n_lat), _whole((1, D), (layer,)),
         _whole((N_SLOT, D, FF_COLS)), _whole((D_FF, D))],
        [_rows(D), _rows(2 * D_FF), _rows(D), _rows(D), _rows(D_FF), _acc_spec(n_lat)],
        [jax.ShapeDtypeStruct((s, D), F32), jax.ShapeDtypeStruct((s, 2 * D_FF), BF16), jax.ShapeDtypeStruct((s, D), BF16),
         jax.ShapeDtypeStruct((s, D), BF16), jax.ShapeDtypeStruct((s, D_FF), BF16), jax.ShapeDtypeStruct((2, 8, D), F32)],
        (h, ab, f, dh, mods, g, w4, wo), "arbitrary", ex)


def _token_tile(s, limit=1088):
    return max(ts for ts in range(16, limit + 1, 16) if s % ts == 0)


def wgrad(a, b, tk, tn, slot_cols, name):
    s, k = a.shape
    n = b.shape[1]
    ts = _token_tile(s)
    steps = s // ts

    def body(a_ref, b_ref, o_ref, o16_ref):
        r = _dotg(a_ref[...], b_ref[...], TN)
        si = pl.program_id(2)

        @pl.when(si == 0)
        def _():
            o_ref[...] = r

        @pl.when(si > 0)
        def _():
            o_ref[...] += r

        @pl.when(si == steps - 1)
        def _():
            o16_ref[...] = o_ref[...].astype(BF16)

    if slot_cols is None:
        shape, spec = (k, n), pl.BlockSpec((tk, tn), lambda i, j, si: (i, j))
    else:
        per = slot_cols // tn
        shape, spec = (n // slot_cols, k, slot_cols), pl.BlockSpec((None, tk, tn), lambda i, j, si: (lax.div(j, per), i, lax.rem(j, per)))
    return pl.pallas_call(
        body, name=name, grid=(k // tk, n // tn, steps),
        in_specs=[pl.BlockSpec((ts, tk), lambda i, j, si: (si, i)), pl.BlockSpec((ts, tn), lambda i, j, si: (si, j))],
        out_specs=[spec, spec],
        out_shape=[jax.ShapeDtypeStruct(shape, F32), jax.ShapeDtypeStruct(shape, BF16)],
        compiler_params=_params("parallel", "parallel", "arbitrary"),
    )(a, b)


def proj_fwd(h, mods, g, w_in, cos, sin, layer, n_lat, name):
    s = h.shape[0]

    def body(h_ref, m_ref, g_ref, w_ref, cos_ref, sin_ref, u_ref, q_ref, k_ref, v_ref):
        n, _, _, _ = _norm_mod(h_ref[...], g_ref[...], m_ref[3:4, :], m_ref[4:5, :])
        p = _dot(n.astype(BF16), w_ref[...])
        cs, sn = cos_ref[...], sin_ref[...]
        u_ref[...] = p[:, :POOL_W]
        q_ref[...] = (_rope(p[:, POOL_W:POOL_W + ATTN_W], cs, sn) * HEAD ** -0.5).astype(BF16)
        k_ref[...] = _rope(p[:, POOL_W + ATTN_W:POOL_W + ATTN_W + KV_W], cs, sn).astype(BF16)
        v_ref[...] = p[:, POOL_W + ATTN_W + KV_W:].astype(BF16)

    return pl.pallas_call(
        body, name=name, grid=(s // TM,),
        in_specs=[_rows(D), _mods_spec(layer, n_lat), _whole((1, D), (layer,)), _whole((D, PROJ_W)),
                  _rows(BLK), _rows(BLK)],
        out_specs=[_rows(POOL_W), _rows(ATTN_W), _rows(KV_W), _rows(KV_W)],
        out_shape=[jax.ShapeDtypeStruct((s, POOL_W), F32), jax.ShapeDtypeStruct((s, ATTN_W), BF16),
                   jax.ShapeDtypeStruct((s, KV_W), BF16), jax.ShapeDtypeStruct((s, KV_W), BF16)],
        compiler_params=_params("parallel"),
    )(h, mods, g, w_in, cos, sin)


def proj_bwd(h, du, dq, dk, dv, dh, mods, g, w_in, cos, sin, layer, n_lat, name):
    s = h.shape[0]

    def body(h_ref, du_ref, dq_ref, dk_ref, dv_ref, dh_ref, m_ref, g_ref, w_ref, cos_ref, sin_ref,
             dhi_ref, dp_ref, n_ref, dm_ref):
        i = pl.program_id(0)

        @pl.when((i == 0) | (i == n_lat))
        def _():
            dm_ref[...] = jnp.zeros_like(dm_ref)

        gg, scale = g_ref[...], m_ref[4:5, :]
        n, xhat, r, y = _norm_mod(h_ref[...], gg, m_ref[3:4, :], scale)
        n_ref[...] = n.astype(BF16)
        cs, sn = cos_ref[...], sin_ref[...]
        dp = jnp.concatenate([du_ref[...], _unrope(dq_ref[...], cs, sn) * HEAD ** -0.5, _unrope(dk_ref[...], cs, sn),
                              dv_ref[...]], axis=1).astype(BF16)
        dp_ref[...] = dp
        dsh, dsc, dg, dhn = _norm_mod_bwd(_dotg(dp, w_ref[...], NT), xhat, r, y, gg, scale)
        dhi_ref[...] = dh_ref[...] + dhn
        dm_ref[0:1, :] += dsh
        dm_ref[1:2, :] += dsc
        dm_ref[3:4, :] += dg

    return pl.pallas_call(
        body, name=name, grid=(s // TM,),
        in_specs=[_rows(D), _rows(POOL_W), _rows(ATTN_W), _rows(KV_W), _rows(KV_W), _rows(D), _mods_spec(layer, n_lat),
                  _whole((1, D), (layer,)), _whole((D, PROJ_W)), _rows(BLK), _rows(BLK)],
        out_specs=[_rows(D), _rows(PROJ_W), _rows(D), _acc_spec(n_lat)],
        out_shape=[jax.ShapeDtypeStruct((s, D), F32), jax.ShapeDtypeStruct((s, PROJ_W), BF16),
                   jax.ShapeDtypeStruct((s, D), BF16), jax.ShapeDtypeStruct((2, 8, D), F32)],
        compiler_params=_params("arbitrary"),
    )(h, du, dq, dk, dv, dh, mods, g, w_in, cos, sin)


def _window(i, n_lat_blk):
    return pl.multiple_of(jnp.clip(i - 1, 0, n_lat_blk - 1) * BLK, BLK)


def _pool_band(i, ws, w, seq_lo, seq_hi, transposed):
    shape = (3 * BLK, BLK) if transposed else (BLK, 3 * BLK)
    q = i * BLK + lax.broadcasted_iota(jnp.int32, shape, 1 if transposed else 0)
    k = ws + lax.broadcasted_iota(jnp.int32, shape, 0 if transposed else 1)
    band = (k >= jnp.maximum(q - w // 2, seq_lo)) & (k < jnp.minimum(q + w - w // 2, seq_hi))
    qc = i * BLK + lax.broadcasted_iota(jnp.int32, (BLK, 1), 0)
    cnt = jnp.minimum(qc + w - w // 2, seq_hi) - jnp.maximum(qc - w // 2, seq_lo)
    return jnp.where(band, 1.0, 0.0).astype(BF16), cnt.astype(F32)


def _split_dot(band, v):
    hi = v.astype(BF16)
    return _dot(band, hi) + _dot(band, (v - hi.astype(F32)).astype(BF16))


def _pooled(u_ref, i, ws, seq_lo, seq_hi, gi):
    band, cnt = _pool_band(i, ws, POOL_WINDOWS[gi], seq_lo, seq_hi, False)
    cols = slice(gi * GROUP, (gi + 1) * GROUP)
    mean = _split_dot(band, u_ref[pl.ds(ws, 3 * BLK), cols]) / cnt
    return mean - u_ref[pl.ds(pl.multiple_of(i * BLK, BLK), BLK), cols]


def _local_valid(i, ws, t):
    q = i * BLK + lax.broadcasted_iota(jnp.int32, (BLK, 3 * BLK), 0)
    k = ws + lax.broadcasted_iota(jnp.int32, (BLK, 3 * BLK), 1)
    return (i * BLK < t) & (k < t) & (jnp.abs(k - q) <= BLK)


def _head_cols(hd):
    return slice(hd * HEAD, (hd + 1) * HEAD)


def _lane_place(cols, width=BLK):
    lane = lax.broadcasted_iota(jnp.int32, (cols[0].shape[0], width), 1)
    out = jnp.zeros((cols[0].shape[0], width), F32)
    for hd, c in enumerate(cols):
        out = jnp.where(lane == hd, c, out)
    return out


def mix_fwd(h, q, k, v, u, w_pool, pool_scale, sink, w_out, mods, layer, t, name, ex=None):
    s = h.shape[0]
    n_lat_blk = t // BLK

    def body(h_ref, q_ref, k_ref, v_ref, u_ref, wp_ref, ps_ref, sink_ref, wo_ref, m_ref, ho_ref, cat_ref, lse_ref):
        i = pl.program_id(0)
        ws = _window(i, n_lat_blk)
        is_lat = i < n_lat_blk
        seq_lo, seq_hi = jnp.where(is_lat, 0, t), jnp.where(is_lat, t, s)
        for gi in range(len(POOL_WINDOWS)):
            mixed = _dot(_pooled(u_ref, i, ws, seq_lo, seq_hi, gi).astype(BF16), wp_ref[gi])
            cat_ref[:, gi * GROUP:(gi + 1) * GROUP] = (mixed * ps_ref[:, gi * GROUP:(gi + 1) * GROUP]).astype(BF16)
        valid = _local_valid(i, ws, t)
        kw, vw = k_ref[pl.ds(ws, 3 * BLK), :], v_ref[pl.ds(ws, 3 * BLK), :]
        kc, vc = k_ref[t:s, :], v_ref[t:s, :]
        lses = []
        for hd in range(N_HEADS):
            kv = _head_cols(hd // Q_GROUP)
            qh = q_ref[:, _head_cols(hd)]
            sl = jnp.where(valid, _dotg(qh, kw[:, kv], NT), NEG_INF)
            sc = _dotg(qh, kc[:, kv], NT)
            sk = sink_ref[layer, hd]
            m = jnp.maximum(jnp.maximum(jnp.max(sl, axis=1, keepdims=True), jnp.max(sc, axis=1, keepdims=True)), sk)
            el, ec = jnp.exp(sl - m), jnp.exp(sc - m)
            l = jnp.sum(el, axis=1, keepdims=True) + jnp.sum(ec, axis=1, keepdims=True) + jnp.exp(sk - m)
            inv = 1.0 / l
            o = _dot((el * inv).astype(BF16), vw[:, kv]) + _dot((ec * inv).astype(BF16), vc[:, kv])
            cat_ref[:, POOL_W + hd * HEAD:POOL_W + (hd + 1) * HEAD] = o.astype(BF16)
            lses.append(m + jnp.log(l))
        lse_ref[...] = _lane_place(lses)
        ho_ref[...] = h_ref[...] + m_ref[5:6, :] * _dot(cat_ref[...], wo_ref[...])

    blk = lambda cols: _rows(cols, BLK)
    return _grid_call(
        body, name, s // BLK,
        [blk(D), blk(ATTN_W), _whole((s, KV_W)), _whole((s, KV_W)), _whole((s, POOL_W)),
         _whole((len(POOL_WINDOWS), GROUP, GROUP), (layer,)), _whole((1, POOL_W), (layer,)),
         pl.BlockSpec(memory_space=pltpu.SMEM), _whole((POOL_W + ATTN_W, D)), _mods_spec(layer, n_lat_blk)],
        [blk(D), blk(POOL_W + ATTN_W), blk(BLK)],
        [jax.ShapeDtypeStruct((s, D), F32), jax.ShapeDtypeStruct((s, POOL_W + ATTN_W), BF16), jax.ShapeDtypeStruct((s, BLK), F32)],
        (h, q, k, v, u, w_pool, pool_scale, sink, w_out, mods), "parallel", ex)


def mix_bwd(dh, cat, q, k, v, u, lse, w_pool, pool_scale, sink, w_out, mods, layer, t, name, ex=None):
    s = dh.shape[0]
    n_lat_blk = t // BLK
    n_grp = len(POOL_WINDOWS)

    def body(dh_ref, cat_ref, q_ref, k_ref, v_ref, u_ref, lse_ref, wp_ref, ps_ref, sink_ref, wo_ref, m_ref,
             dq_ref, dk_ref, dv_ref, du_ref, dmo_ref, dwp_ref, dps_ref, dsink_ref, dm_ref):
        i = pl.program_id(0)

        @pl.when(i == 0)
        def _():
            for ref in (dk_ref, dv_ref, du_ref, dwp_ref, dps_ref, dsink_ref):
                ref[...] = jnp.zeros_like(ref)

        @pl.when((i == 0) | (i == n_lat_blk))
        def _():
            dm_ref[...] = jnp.zeros_like(dm_ref)

        ws = _window(i, n_lat_blk)
        here = pl.ds(pl.multiple_of(i * BLK, BLK), BLK)
        is_lat = i < n_lat_blk
        seq_lo, seq_hi = jnp.where(is_lat, 0, t), jnp.where(is_lat, t, s)
        dho = dh_ref[...]
        dm_ref[2:3, :] += _sum0(dho * _dot(cat_ref[...], wo_ref[...]))
        dmo = (m_ref[5:6, :] * dho).astype(BF16)
        dmo_ref[...] = dmo
        dcat = _dotg(dmo, wo_ref[...], NT)

        for gi in range(n_grp):
            cols = slice(gi * GROUP, (gi + 1) * GROUP)
            pooled = _pooled(u_ref, i, ws, seq_lo, seq_hi, gi).astype(BF16)
            dpo = dcat[:, cols]
            dps_ref[0:1, cols] += _sum0(dpo * _dot(pooled, wp_ref[gi]))
            dmixed = (dpo * ps_ref[:, cols]).astype(BF16)
            dwp_ref[gi] += _dotg(pooled, dmixed, TN)
            dpooled = _dotg(dmixed, wp_ref[gi], NT)
            band_t, cnt = _pool_band(i, ws, POOL_WINDOWS[gi], seq_lo, seq_hi, True)
            du_ref[pl.ds(ws, 3 * BLK), cols] += _split_dot(band_t, dpooled / cnt)
            du_ref[here, cols] -= dpooled

        valid = _local_valid(i, ws, t)
        kw, vw = k_ref[pl.ds(ws, 3 * BLK), :], v_ref[pl.ds(ws, 3 * BLK), :]
        kc, vc = k_ref[t:s, :], v_ref[t:s, :]
        dqs, dsinks = [], []
        dkw, dvw, dkc, dvc = [], [], [], []
        for hd in range(N_HEADS):
            kv = _head_cols(hd // Q_GROUP)
            qh = q_ref[:, _head_cols(hd)]
            lse_h = lse_ref[:, hd:hd + 1]
            pl_ = jnp.exp(jnp.where(valid, _dotg(qh, kw[:, kv], NT), NEG_INF) - lse_h)
            pc = jnp.exp(_dotg(qh, kc[:, kv], NT) - lse_h)
            do = dcat[:, POOL_W + hd * HEAD:POOL_W + (hd + 1) * HEAD].astype(BF16)
            dpl, dpc = _dotg(do, vw[:, kv], NT), _dotg(do, vc[:, kv], NT)
            delta = jnp.sum(pl_ * dpl, axis=1, keepdims=True) + jnp.sum(pc * dpc, axis=1, keepdims=True)
            dsl, dsc = (pl_ * (dpl - delta)).astype(BF16), (pc * (dpc - delta)).astype(BF16)
            dsinks.append(_sum0(-jnp.exp(sink_ref[layer, hd] - lse_h) * delta))
            dqs.append(_dot(dsl, kw[:, kv]) + _dot(dsc, kc[:, kv]))
            parts = (_dotg(dsl, qh, TN), _dotg(pl_.astype(BF16), do, TN), _dotg(dsc, qh, TN), _dotg(pc.astype(BF16), do, TN))
            for acc, part in zip((dkw, dvw, dkc, dvc), parts):
                if hd % Q_GROUP == 0:
                    acc.append(part)
                else:
                    acc[-1] = acc[-1] + part
        dq_ref[...] = jnp.concatenate(dqs, axis=1)
        dk_ref[pl.ds(ws, 3 * BLK), :] += jnp.concatenate(dkw, axis=1)
        dv_ref[pl.ds(ws, 3 * BLK), :] += jnp.concatenate(dvw, axis=1)
        dk_ref[t:s, :] += jnp.concatenate(dkc, axis=1)
        dv_ref[t:s, :] += jnp.concatenate(dvc, axis=1)
        dsink_ref[0:1, :] += _lane_place(dsinks)

    blk = lambda cols: _rows(cols, BLK)
    full = lambda shape: pl.BlockSpec(shape, lambda i: (0,) * len(shape))
    return _grid_call(
        body, name, s // BLK,
        [blk(D), blk(POOL_W + ATTN_W), blk(ATTN_W), _whole((s, KV_W)), _whole((s, KV_W)), _whole((s, POOL_W)),
         blk(BLK), _whole((n_grp, GROUP, GROUP), (layer,)), _whole((1, POOL_W), (layer,)),
         pl.BlockSpec(memory_space=pltpu.SMEM), _whole((POOL_W + ATTN_W, D)), _mods_spec(layer, n_lat_blk)],
        [blk(ATTN_W), full((s, KV_W)), full((s, KV_W)), full((s, POOL_W)), blk(D),
         full((n_grp, GROUP, GROUP)), full((8, POOL_W)), full((8, BLK)), _acc_spec(n_lat_blk)],
        [jax.ShapeDtypeStruct((s, ATTN_W), F32), jax.ShapeDtypeStruct((s, KV_W), F32),
         jax.ShapeDtypeStruct((s, KV_W), F32), jax.ShapeDtypeStruct((s, POOL_W), F32),
         jax.ShapeDtypeStruct((s, D), BF16), jax.ShapeDtypeStruct((n_grp, GROUP, GROUP), F32),
         jax.ShapeDtypeStruct((8, POOL_W), F32), jax.ShapeDtypeStruct((8, BLK), F32), jax.ShapeDtypeStruct((2, 8, D), F32)],
        (dh, cat, q, k, v, u, lse, w_pool, pool_scale, sink, w_out, mods), "arbitrary", ex)


def loss_head(h, target, g, t, name):
    s = h.shape[0]
    n_lat = t // TM

    def body(h_ref, t_ref, g_ref, dh_ref, acc_ref):
        i = pl.program_id(0)

        @pl.when(i == 0)
        def _():
            acc_ref[...] = jnp.zeros_like(acc_ref)

        @pl.when(i < n_lat)
        def _():
            hh, gg = h_ref[...], g_ref[...]
            r = lax.rsqrt(jnp.mean(hh * hh, axis=-1, keepdims=True) + EPS)
            xhat = hh * r
            err = xhat * gg - t_ref[...]
            dy = err * (1.0 / D)
            dx = dy * gg
            dh_ref[...] = r * (dx - xhat * jnp.mean(dx * xhat, axis=-1, keepdims=True))
            acc_ref[0:1, :] += _sum0(dy * xhat)
            acc_ref[1:2, :] += _sum0(err * err)

        @pl.when(i >= n_lat)
        def _():
            dh_ref[...] = jnp.zeros_like(dh_ref)

    return pl.pallas_call(
        body, name=name, grid=(s // TM,),
        in_specs=[_rows(D), pl.BlockSpec((TM, D), lambda i: (jnp.minimum(i, n_lat - 1), 0)), _whole((1, D))],
        out_specs=[_rows(D), pl.BlockSpec((8, D), lambda i: (0, 0))],
        out_shape=[jax.ShapeDtypeStruct((s, D), F32), jax.ShapeDtypeStruct((8, D), F32)],
        compiler_params=_params("arbitrary"),
    )(h, target, g)


def mod_rows(c_all, w_mod, b_cols, name):
    def body(c_ref, w_ref, b_ref, o_ref):
        cc = c_ref[...]
        o_ref[...] = _dot((cc * jax.nn.sigmoid(cc)).astype(BF16), w_ref[...].astype(BF16)) + b_ref[...]

    return pl.pallas_call(
        body, name=name, grid=(2,),
        in_specs=[pl.BlockSpec((16, D), lambda l: (0, 0)), pl.BlockSpec((None, D, MOD_COLS), lambda l: (l, 0, 0)),
                  pl.BlockSpec((None, 1, MOD_COLS), lambda l: (l, 0, 0))],
        out_specs=pl.BlockSpec((None, 16, MOD_COLS), lambda l: (l, 0, 0)),
        out_shape=jax.ShapeDtypeStruct((2, 16, MOD_COLS), F32),
        compiler_params=_params("parallel"),
    )(c_all, w_mod, b_cols)


def mod_grads(c_all, dmod_cols, w_mod, name):
    def body(c_ref, d_ref, w_ref, dw_ref, dc_ref):
        @pl.when(pl.program_id(0) == 0)
        def _():
            dc_ref[...] = jnp.zeros_like(dc_ref)

        cc = c_ref[...]
        dd = d_ref[...].astype(BF16)
        dw_ref[...] = _dotg((cc * jax.nn.sigmoid(cc)).astype(BF16), dd, TN)
        dc_ref[...] += _dotg(dd, w_ref[...].astype(BF16), NT)

    return pl.pallas_call(
        body, name=name, grid=(2,),
        in_specs=[pl.BlockSpec((16, D), lambda l: (0, 0)), pl.BlockSpec((None, 16, MOD_COLS), lambda l: (l, 0, 0)),
                  pl.BlockSpec((None, D, MOD_COLS), lambda l: (l, 0, 0))],
        out_specs=[pl.BlockSpec((None, D, MOD_COLS), lambda l: (l, 0, 0)), pl.BlockSpec((16, D), lambda l: (0, 0))],
        out_shape=[jax.ShapeDtypeStruct((2, D, MOD_COLS), F32), jax.ShapeDtypeStruct((16, D), F32)],
        compiler_params=_params("arbitrary"),
    )(c_all, dmod_cols, w_mod)


def _row_tile(rows, cols, n_arrays):
    budget = VMEM_LIMIT_BYTES // 4 // (2 * 4 * n_arrays * cols)
    best = None
    for tr in range(16, rows + 1, 16):
        if rows % tr == 0 and tr <= budget:
            best = tr
    return best if best is not None else rows


def elementwise(fn, ins, out_dtypes, name):
    rows, cols = ins[0].shape
    tr = _row_tile(rows, cols, len(ins) + len(out_dtypes))

    def body(*refs):
        outs = fn(*[r[...] for r in refs[:len(ins)]])
        for o_ref, o in zip(refs[len(ins):], outs):
            o_ref[...] = o.astype(o_ref.dtype)

    spec = pl.BlockSpec((tr, cols), lambda i: (i, 0))
    return pl.pallas_call(
        body, name=name, grid=(rows // tr,), in_specs=[spec] * len(ins), out_specs=[spec] * len(out_dtypes),
        out_shape=[jax.ShapeDtypeStruct((rows, cols), dt) for dt in out_dtypes],
        compiler_params=_params("parallel"),
    )(*ins)


def _adamw_tile(w, g, m, v):
    m = ADAM_B1 * m + (1.0 - ADAM_B1) * g
    v = ADAM_B2 * v + (1.0 - ADAM_B2) * (g * g)
    m_hat = m / (1.0 - ADAM_B1 ** ADAM_STEP)
    v_hat = v / (1.0 - ADAM_B2 ** ADAM_STEP)
    return -ADAM_LR * (m_hat / (jnp.sqrt(v_hat) + ADAM_EPS) + ADAM_WD * w), m, v


def adamw(w, g, m, v, name):
    shape = w.shape
    two_d = (-1, shape[-1]) if w.ndim > 1 else (1, -1)
    outs = elementwise(_adamw_tile, [a.reshape(two_d) for a in (w, g, m, v)], [F32] * 3, name)
    return [o.reshape(shape) for o in outs]


def _prefetch_call(body, name, grid, in_specs, out_specs, out_shape, place, args, aliases=None):
    spec = pltpu.PrefetchScalarGridSpec(num_scalar_prefetch=1, grid=grid, in_specs=in_specs, out_specs=out_specs)
    return pl.pallas_call(body, name=name, grid_spec=spec, out_shape=out_shape, input_output_aliases=aliases or {},
                          compiler_params=_params(*["parallel"] * len(grid)))(place, *args)


def cast_place(w, layer, place, name):
    _, r, c = w.shape
    tr = _row_tile(r, c, 2)

    def body(p_ref, w_ref, o_ref):
        o_ref[...] = w_ref[...].astype(BF16)

    return _prefetch_call(
        body, name, (r // tr,), [pl.BlockSpec((None, tr, c), lambda i, p: (layer, i, 0))],
        pl.BlockSpec((None, tr, c), lambda i, p: (p[1], i, 0)), jax.ShapeDtypeStruct((N_SLOT, r, c), BF16), place, [w])


def pair_sum(g32, got, place, name):
    n_slot, rh, c = got.shape
    tr = _row_tile(rh, c, 4)
    per = rh // tr

    def body(p_ref, a_ref, b_ref, o_ref, o16_ref):
        r = a_ref[...] + b_ref[...].astype(F32)
        o_ref[...] = r
        o16_ref[...] = r.astype(BF16)

    half = pl.BlockSpec((None, tr, c), lambda s, i, p: (s, i, 0))
    return _prefetch_call(
        body, name, (n_slot, per), [pl.BlockSpec((None, tr, c), lambda s, i, p: (s, p[0] * per + i, 0)), half], [half, half],
        [jax.ShapeDtypeStruct(got.shape, F32), jax.ShapeDtypeStruct(got.shape, BF16)], place, [g32, got])


def chip_sum(p32, got, place, name):
    _, rh, c = p32.shape
    tr = _row_tile(rh, c, 5)
    per = rh // tr

    def body(p_ref, m_ref, r0_ref, r1_ref, r2_ref, o_ref):
        o_ref[...] = m_ref[...] + r0_ref[...].astype(F32) + r1_ref[...].astype(F32) + r2_ref[...].astype(F32)

    part = pl.BlockSpec((tr, c), lambda i, p: (i, 0))
    return _prefetch_call(
        body, name, (per,), [pl.BlockSpec((None, tr, c), lambda i, p: (p[1], i, 0)), part, part, part],
        pl.BlockSpec((tr, c), lambda i, p: (p[0] * per + i, 0)), jax.ShapeDtypeStruct((2 * rh, c), F32), place, [p32, *got])


def adamw_layers(w, g0, g1, m, v, name):
    _, r, c = w.shape
    tr = _row_tile(r, c, 10)

    def body(w_ref, g0_ref, g1_ref, m_ref, v_ref, g_ref, d_ref, mo_ref, vo_ref):
        g = jnp.where(pl.program_id(0) == 0, g0_ref[...], g1_ref[...])
        g_ref[...] = g
        d_ref[...], mo_ref[...], vo_ref[...] = _adamw_tile(w_ref[...], g, m_ref[...], v_ref[...])

    stacked = pl.BlockSpec((None, tr, c), lambda l, i: (l, i, 0))
    layer = pl.BlockSpec((tr, c), lambda l, i: (i, 0))
    return pl.pallas_call(
        body, name=name, grid=(2, r // tr), in_specs=[stacked, layer, layer, stacked, stacked], out_specs=[stacked] * 4,
        out_shape=[jax.ShapeDtypeStruct(w.shape, F32)] * 4, compiler_params=_params("parallel", "parallel"),
    )(w, g0, g1, m, v)


def sum8(gathered, name):
    def body(*refs):
        n = len(refs) // 2
        for g_ref, o_ref in zip(refs[:n], refs[n:]):
            acc = g_ref[0]
            for dev in range(1, N_DEV):
                acc = acc + g_ref[dev]
            o_ref[...] = acc

    return pl.pallas_call(
        body, name=name,
        out_shape=[jax.ShapeDtypeStruct(a.shape[1:], F32) for a in gathered],
        compiler_params=_params(),
    )(*gathered)


def _place():
    return lax.axis_index("x"), lax.axis_index("y"), lax.axis_index("c")


def _any(n):
    return [pl.BlockSpec(memory_space=pl.ANY)] * n


def all_gather(blocks, name):
    n = len(blocks)

    def body(*refs):
        ins, outs = refs[:n], refs[n:2 * n]
        send_sems, recv_sems, local_sems = refs[2 * n:]
        x, y, c = _place()
        me, sibling = (x, y, c), (x, y, 1 - c)
        chips = [(1 - x, y), (x, 1 - y), (1 - x, 1 - y)]

        def copy(ti, k, block, to, src=None):
            dst = outs[ti].at[4 * block[0] + 2 * block[1] + block[2]]
            return pltpu.make_async_remote_copy(src_ref=dst if src is None else src, dst_ref=dst, send_sem=send_sems.at[ti, k],
                                                recv_sem=recv_sems.at[ti, k], device_id=to, device_id_type=MESH)

        local, sent = [], []
        for ti in range(n):
            local.append(pltpu.make_async_copy(ins[ti], outs[ti].at[4 * x + 2 * y + c], local_sems.at[ti]))
            sent.append(copy(ti, 0, me, sibling, src=ins[ti]))
            sent += [copy(ti, 1 + j, me, (*chip, c), src=ins[ti]) for j, chip in enumerate(chips)]
        for cp in local + sent:
            cp.start()
        for ti in range(n):
            for j, chip in enumerate(chips):
                copy(ti, 1 + j, (*chip, c), me).wait_recv()
                sent.append(copy(ti, 4 + j, (*chip, c), sibling))
                sent[-1].start()
        for ti in range(n):
            copy(ti, 0, sibling, me).wait_recv()
            for j, chip in enumerate(chips):
                copy(ti, 4 + j, (*chip, 1 - c), me).wait_recv()
        for cp in sent:
            cp.wait_send()
        for cp in local:
            cp.wait()

    return pl.pallas_call(
        body, name=name, in_specs=_any(n), out_specs=_any(n),
        out_shape=[jax.ShapeDtypeStruct((N_DEV,) + b.shape, b.dtype) for b in blocks],
        scratch_shapes=[pltpu.SemaphoreType.DMA((n, 7)), pltpu.SemaphoreType.DMA((n, 7)), pltpu.SemaphoreType.DMA((n,))],
    )(*blocks)


def _three_chips(x, y):
    return [(1 - x, y), (x, 1 - y), (1 - x, 1 - y)]


def gather_exchange(placed):
    n = len(placed)

    def copy(bufs, sems, ti, k, chip, core, to):
        rh = bufs[ti].shape[1] // 2
        half = bufs[ti].at[2 * chip[0] + chip[1], pl.ds(core * rh, rh), :]
        return pltpu.make_async_remote_copy(src_ref=half, dst_ref=half, send_sem=sems[0].at[ti, k], recv_sem=sems[1].at[ti, k],
                                            device_id=to, device_id_type=MESH)

    def sends(bufs, sems):
        x, y, c = _place()
        return [copy(bufs, sems, ti, k, (x, y), c, (*chip, c)) for ti in range(n) for k, chip in enumerate(_three_chips(x, y))]

    def start(ins, bufs, sems):
        for cp in sends(bufs, sems):
            cp.start()

    def finish(ins, bufs, sems):
        x, y, c = _place()
        chips = _three_chips(x, y)
        passed = []
        for ti in range(n):
            for k, chip in enumerate(chips):
                copy(bufs, sems, ti, k, chip, c, (x, y, c)).wait_recv()
                passed.append(copy(bufs, sems, ti, 3 + k, chip, c, (x, y, 1 - c)))
                passed[-1].start()
        for ti in range(n):
            for k, chip in enumerate(chips):
                copy(bufs, sems, ti, 3 + k, chip, 1 - c, (x, y, c)).wait_recv()
        for cp in sends(bufs, sems) + passed:
            cp.wait_send()

    return dict(ins=list(placed), out_shape=[jax.ShapeDtypeStruct(w.shape, w.dtype) for w in placed],
                aliases={i: i for i in range(n)}, start=start, finish=finish,
                scratch=[pltpu.SemaphoreType.DMA((n, 6)), pltpu.SemaphoreType.DMA((n, 6))])


def scatter_exchange(p16):
    n = len(p16)

    def copies(ins, got, sems):
        x, y, c = _place()
        return [pltpu.make_async_remote_copy(src_ref=ins[ti].at[2 * chip[0] + chip[1]], dst_ref=got[3 * ti + k],
                                             send_sem=sems[0].at[ti, k], recv_sem=sems[1].at[ti, k], device_id=(*chip, c),
                                             device_id_type=MESH)
                for ti in range(n) for k, chip in enumerate(_three_chips(x, y))]

    def start(ins, got, sems):
        for cp in copies(ins, got, sems):
            cp.start()

    def finish(ins, got, sems):
        for cp in copies(ins, got, sems):
            cp.wait()

    return dict(ins=list(p16), out_shape=[jax.ShapeDtypeStruct(a.shape[1:], BF16) for a in p16 for _ in range(3)], aliases={},
                start=start, finish=finish, scratch=[pltpu.SemaphoreType.DMA((n, 3)), pltpu.SemaphoreType.DMA((n, 3))])


def run_exchange(ex, name):
    ci, co = len(ex["ins"]), len(ex["out_shape"])

    def body(*refs):
        ins, outs, sems = refs[:ci], refs[ci:ci + co], refs[ci + co:]
        ex["start"](ins, outs, sems)
        ex["finish"](ins, outs, sems)

    return pl.pallas_call(body, name=name, in_specs=_any(ci), out_specs=_any(co), out_shape=ex["out_shape"],
                          input_output_aliases=ex["aliases"], scratch_shapes=ex["scratch"])(*ex["ins"])


def _grid_call(body, name, steps, in_specs, out_specs, out_shape, args, sem, ex=None):
    n_in, n_out = len(in_specs), len(out_specs)
    if ex is None:
        return pl.pallas_call(body, name=name, grid=(steps,), in_specs=in_specs, out_specs=out_specs, out_shape=out_shape,
                              compiler_params=_params(sem))(*args), []
    ci, co = len(ex["ins"]), len(ex["out_shape"])

    def carrying(*refs):
        c_in, c_out = refs[n_in:n_in + ci], refs[n_in + ci + n_out:n_in + ci + n_out + co]
        sems = refs[n_in + ci + n_out + co:]

        @pl.when(pl.program_id(0) == 0)
        def _():
            ex["start"](c_in, c_out, sems)

        body(*refs[:n_in], *refs[n_in + ci:n_in + ci + n_out])

        @pl.when(pl.program_id(0) == steps - 1)
        def _():
            ex["finish"](c_in, c_out, sems)

    outs = pl.pallas_call(
        carrying, name=name, grid=(steps,), in_specs=list(in_specs) + _any(ci), out_specs=list(out_specs) + _any(co),
        out_shape=list(out_shape) + ex["out_shape"], scratch_shapes=ex["scratch"],
        input_output_aliases={n_in + i: n_out + j for i, j in ex["aliases"].items()}, compiler_params=_params("arbitrary"),
    )(*args, *ex["ins"])
    return outs[:n_out], outs[n_out:]


def pair_exchange(g16, name):
    n = len(g16)

    def body(*refs):
        a16, got = refs[:n], refs[n:2 * n]
        send_sems, recv_sems = refs[2 * n:]
        x, y, c = _place()
        copies = []
        for ti in range(n):
            rh = a16[ti].shape[1] // 2
            copies.append(pltpu.make_async_remote_copy(
                src_ref=a16[ti].at[:, pl.ds((1 - c) * rh, rh), :], dst_ref=got[ti], send_sem=send_sems.at[ti],
                recv_sem=recv_sems.at[ti], device_id=(x, y, 1 - c), device_id_type=MESH))
        for cp in copies:
            cp.start()
        for cp in copies:
            cp.wait()

    return pl.pallas_call(
        body, name=name, in_specs=_any(n), out_specs=_any(n),
        out_shape=[jax.ShapeDtypeStruct((a.shape[0], a.shape[1] // 2, a.shape[2]), BF16) for a in g16],
        scratch_shapes=[pltpu.SemaphoreType.DMA((n,)), pltpu.SemaphoreType.DMA((n,))],
    )(*g16)


def pair_gather(halves, name):
    n = len(halves)

    def body(*refs):
        bufs = refs[n:2 * n]
        send_sems, recv_sems = refs[2 * n:]
        x, y, c = _place()
        copies = []
        for ti in range(n):
            rh = bufs[ti].shape[0] // 2
            rows = bufs[ti].at[pl.ds(c * rh, rh), :]
            copies.append(pltpu.make_async_remote_copy(src_ref=rows, dst_ref=rows, send_sem=send_sems.at[ti],
                                                       recv_sem=recv_sems.at[ti], device_id=(x, y, 1 - c), device_id_type=MESH))
        for cp in copies:
            cp.start()
        for ti, cp in enumerate(copies):
            cp.wait_send()
            rh = bufs[ti].shape[0] // 2
            theirs = bufs[ti].at[pl.ds((1 - c) * rh, rh), :]
            pltpu.make_async_remote_copy(src_ref=theirs, dst_ref=theirs, send_sem=send_sems.at[ti], recv_sem=recv_sems.at[ti],
                                         device_id=(x, y, 1 - c), device_id_type=MESH).wait_recv()

    return pl.pallas_call(
        body, name=name, in_specs=_any(n), out_specs=_any(n), input_output_aliases={i: i for i in range(n)},
        out_shape=[jax.ShapeDtypeStruct(a.shape, a.dtype) for a in halves],
        scratch_shapes=[pltpu.SemaphoreType.DMA((n,)), pltpu.SemaphoreType.DMA((n,))],
    )(*halves)


def reduce_small(dml, loss_blk, name):
    def body(d_ref, l_ref, tot_ref, rows_ref, fin_ref):
        rows_ref[...] = jnp.zeros_like(rows_ref)
        for l in range(2):
            ctx = d_ref[0, l, 1]
            lat = d_ref[0, l, 0]
            rows_ref[l, 0] = lat
            for dev in range(1, N_DEV):
                rows_ref[l, dev] = d_ref[dev, l, 0]
                ctx = ctx + d_ref[dev, l, 1]
                lat = lat + d_ref[dev, l, 0]
            rows_ref[l, N_DEV] = ctx
            tot_ref[l] = lat + ctx
        acc = l_ref[0]
        for dev in range(1, N_DEV):
            acc = acc + l_ref[dev]
        loss = (0.5 / D) * jnp.sum(acc[1:2, :], axis=1, keepdims=True)
        row = lax.broadcasted_iota(jnp.int32, (8, D), 0)
        fin_ref[...] = jnp.where(row == 0, acc[0:1, :], loss)

    return pl.pallas_call(
        body, name=name,
        out_shape=[jax.ShapeDtypeStruct((2, 16, D), F32), jax.ShapeDtypeStruct((2, 16, 16, D), F32),
                   jax.ShapeDtypeStruct((8, D), F32)],
        compiler_params=_params(),
    )(dml, loss_blk)


def rope_tables(t, s):
    rows = t // GRID_W
    row = jnp.repeat(jnp.arange(rows), GRID_W).astype(F32)
    col = jnp.tile(jnp.arange(GRID_W), rows).astype(F32)
    inv = ROPE_BASE ** (-jnp.arange(0, HEAD // 2, 2, dtype=F32) / (HEAD // 2))
    ang = jnp.concatenate([row[:, None] * inv, col[:, None] * inv], axis=-1)
    cos, sin = jnp.cos(ang), jnp.sin(ang)
    cos = jnp.concatenate([jnp.tile(cos, (1, 4)), jnp.ones((s - t, BLK), F32)], axis=0)
    sin = jnp.concatenate([jnp.tile(jnp.concatenate([-sin, sin], axis=1), (1, 2)), jnp.zeros((s - t, BLK), F32)], axis=0)
    return cos, sin


BIG = ("ffn1_in", "ffn1_out", "w_in", "w_out", "ffn2_in", "ffn2_out")
GROUPS = dict(ffn1=("ffn1_in", "ffn1_out"), mix=("w_in", "w_out"), ffn2=("ffn2_in", "ffn2_out"))
GATHER_BEHIND = {("ffn1", 0): [("mix", 0), ("ffn2", 0)], ("mix", 0): [("ffn1", 1)], ("ffn2", 0): [("mix", 1)],
                 ("ffn1", 1): [("ffn2", 1)]}


def _slot_major(name, g):
    if name == "w_in":
        return jnp.stack(jnp.split(g, N_SLOT, axis=1), axis=0)
    if name in ("ffn1_in", "ffn2_in"):
        return g
    return g.reshape(N_SLOT, g.shape[0] // N_SLOT, g.shape[1])


def _whole_weight(name, buf):
    if name == "w_in":
        return buf.transpose(1, 0, 2).reshape(D, PROJ_W)
    if name in ("ffn1_in", "ffn2_in"):
        return buf
    return buf.reshape(-1, buf.shape[2])


def local_step(x1, ctx1, target, mods, norms, nfinal, placed, w_pool, pool_scale, sink, place):
    t, s = x1.shape[0], x1.shape[0] + ctx1.shape[0]
    n_lat = t // TM
    cos, sin = rope_tables(t, s)
    wts = {name: list(pair) for name, pair in placed.items()}

    def gather(groups):
        return gather_exchange([wts[name][l] for grp, l in groups for name in GROUPS[grp]])

    def gathered(groups, arrays):
        arrays = list(arrays)
        for grp, l in groups:
            for name in GROUPS[grp]:
                wts[name][l] = arrays.pop(0)

    def weight(name, l):
        return _whole_weight(name, wts[name][l])

    def fwd_ex(grp, l):
        groups = GATHER_BEHIND.get((grp, l))
        return (groups, gather(groups)) if groups else (None, None)

    gathered([("ffn1", 0)], run_exchange(gather([("ffn1", 0)]), "gather_first"))
    h = jnp.concatenate([x1, ctx1], axis=0)
    saved = []
    for l in range(2):
        h0 = h
        groups, ex = fwd_ex("ffn1", l)
        (h1, ab1, f1), got = ffn_fwd(h0, mods, norms[0], weight("ffn1_in", l), weight("ffn1_out", l), l, 0, n_lat, f"ffn1_fwd_{l}", ex)
        gathered(groups or [], got)
        u, q, k, v = proj_fwd(h1, mods, norms[1], weight("w_in", l), cos, sin, l, n_lat, f"proj_fwd_{l}")
        groups, ex = fwd_ex("mix", l)
        (h2, cat, lse), got = mix_fwd(h1, q, k, v, u, w_pool, pool_scale, sink, weight("w_out", l), mods, l, t, f"mix_fwd_{l}", ex)
        gathered(groups or [], got)
        groups, ex = fwd_ex("ffn2", l)
        (h, ab2, f2), got = ffn_fwd(h2, mods, norms[2], weight("ffn2_in", l), weight("ffn2_out", l), l, 6, n_lat, f"ffn2_fwd_{l}", ex)
        gathered(groups or [], got)
        saved.append((h0, ab1, f1, h1, u, q, k, v, cat, lse, h2, ab2, f2))
    dh, loss_blk = loss_head(h, target, nfinal, t, "loss_head")

    halves = {name: [None, None] for name in BIG}
    pending = []

    def summed_in_pair(grp, l, grads):
        names = GROUPS[grp]
        got = pair_exchange([_slot_major(n, g[1]) for n, g in zip(names, grads)], f"pair_exchange_{grp}_{l}")
        pairs = [pair_sum(_slot_major(n, g[0]), r, place, f"pair_sum_{n}_{l}") for n, g, r in zip(names, grads, got)]
        pending.append((grp, l, pairs))

    def scatter():
        return scatter_exchange([p16 for _, p16 in pending[0][2]]) if pending else None

    def scattered(got):
        if pending:
            grp, l, pairs = pending.pop(0)
            for i, name in enumerate(GROUPS[grp]):
                halves[name][l] = chip_sum(pairs[i][0], got[3 * i:3 * i + 3], place, f"chip_sum_{name}_{l}")

    small = [None, None]
    for l in (1, 0):
        h0, ab1, f1, h1, u, q, k, v, cat, lse, h2, ab2, f2 = saved[l]
        (dh, dab, df, n, act, dm_f2), got = ffn_bwd(h2, ab2, f2, dh, mods, norms[2], weight("ffn2_in", l), weight("ffn2_out", l),
                                                    l, 6, n_lat, f"ffn2_bwd_{l}", scatter())
        scattered(got)
        summed_in_pair("ffn2", l, [wgrad(n, dab, D, FF_COLS, FF_COLS, f"ffn2_in_wgrad_{l}"),
                                   wgrad(act, df, D_FF // 2, D, None, f"ffn2_out_wgrad_{l}")])
        (dq, dk, dv, du, dmo, dwp, dps, dsink, dm_gate), got = mix_bwd(
            dh, cat, q, k, v, u, lse, w_pool, pool_scale, sink, weight("w_out", l), mods, l, t, f"mix_bwd_{l}", scatter())
        scattered(got)
        g_wo = wgrad(cat, dmo, POOL_W + ATTN_W, D, None, f"w_out_wgrad_{l}")
        dh, dp, n, dm_mix = proj_bwd(h1, du, dq, dk, dv, dh, mods, norms[1], weight("w_in", l), cos, sin, l, n_lat, f"proj_bwd_{l}")
        summed_in_pair("mix", l, [wgrad(n, dp, D, PROJ_W // 2, None, f"w_in_wgrad_{l}"), g_wo])
        (dh, dab, df, n, act, dm_f1), got = ffn_bwd(h0, ab1, f1, dh, mods, norms[0], weight("ffn1_in", l), weight("ffn1_out", l),
                                                    l, 0, n_lat, f"ffn1_bwd_{l}", scatter())
        scattered(got)
        summed_in_pair("ffn1", l, [wgrad(n, dab, D, FF_COLS, FF_COLS, f"ffn1_in_wgrad_{l}"),
                                   wgrad(act, df, D_FF // 2, D, None, f"ffn1_out_wgrad_{l}")])
        dml = jnp.concatenate([dm_f1[:, 0:3], dm_mix[:, 0:2], dm_gate[:, 2:3], dm_f2[:, 0:3],
                               dm_f1[:, 3:4], dm_mix[:, 3:4], dm_f2[:, 3:4], jnp.zeros((2, 4, D), F32)], axis=1)
        small[l] = dict(dml=dml, dwp=dwp, dps=dps, dsink=dsink)
    scattered(run_exchange(scatter(), "scatter_last"))
    return dh[:t], loss_blk, halves, small


def _silu_grad(z):
    sg = jax.nn.sigmoid(z)
    return sg * (1 + z * (1 - sg))


def kernel(x, c, ctx, c_ctx, w_mod, b_mod, norm_ffn1, w_ffn1_in, w_ffn1_out, norm_mix, w_in, w_pool, pool_scale, sink, w_out, norm_ffn2, w_ffn2_in, w_ffn2_out, norm_final, loss_target, m_c_ctx, m_w_mod, m_b_mod, m_norm_ffn1, m_w_ffn1_in, m_w_ffn1_out, m_norm_mix, m_w_in, m_w_pool, m_pool_scale, m_sink, m_w_out, m_norm_ffn2, m_w_ffn2_in, m_w_ffn2_out, m_norm_final, v_c_ctx, v_w_mod, v_b_mod, v_norm_ffn1, v_w_ffn1_in, v_w_ffn1_out, v_norm_mix, v_w_in, v_w_pool, v_pool_scale, v_sink, v_w_out, v_norm_ffn2, v_w_ffn2_in, v_w_ffn2_out, v_norm_final):
    px, py, pc = _place()
    slot, me = 2 * px + py, 4 * px + 2 * py + pc
    n_grp = len(POOL_WINDOWS)

    (c_rows,) = all_gather([c.reshape(8, D // 8)], "gather_c")
    c_all = jnp.concatenate([c_rows.reshape(N_DEV, D), c_ctx.reshape(1, D), jnp.zeros((16 - N_DEV - 1, D), F32)], axis=0)
    b_cols = lax.dynamic_slice(b_mod, (0, slot * MOD_COLS), (2, MOD_COLS)).reshape(2, 1, MOD_COLS)
    (mod_parts,) = all_gather([mod_rows(c_all, w_mod, b_cols, "mod_rows")], "gather_mods")
    mods_all = mod_parts[0::2].transpose(1, 2, 0, 3).reshape(2, 16, N_MOD * D)
    mx = lax.dynamic_slice(mods_all, (0, me, 0), (2, 1, N_MOD * D)).reshape(2, N_MOD, D)
    mc = mods_all[:, N_DEV].reshape(2, N_MOD, D)
    pad = jnp.zeros((2, 16 - N_MOD, D), F32)
    mods = jnp.stack([jnp.concatenate([mx, pad], axis=1), jnp.concatenate([mc, pad], axis=1)], axis=1)

    place = jnp.stack([pc, slot]).astype(jnp.int32)
    shards = dict(ffn1_in=w_ffn1_in, ffn1_out=w_ffn1_out, w_in=w_in, w_out=w_out, ffn2_in=w_ffn2_in, ffn2_out=w_ffn2_out)
    placed = {name: [cast_place(shards[name], l, place, f"cast_{name}_{l}") for l in range(2)] for name in BIG}
    norms = [g.reshape(2, 1, D) for g in (norm_ffn1, norm_mix, norm_ffn2)]
    dx, loss_blk, halves, small = local_step(x[0], ctx[0], loss_target[0], mods, norms, norm_final.reshape(1, D), placed,
                                             w_pool.astype(BF16), pool_scale.reshape(2, 1, POOL_W), sink, place)
    shard_grads = pair_gather([halves[name][l] for name in BIG for l in range(2)], "grad_pair_gather")
    big_grads = dict(zip(("w_ffn1_in", "w_ffn1_out", "w_in", "w_out", "w_ffn2_in", "w_ffn2_out"), zip(*[iter(shard_grads)] * 2)))
    grads = {}

    stacked = {k: jnp.stack([small[0][k], small[1][k]]) for k in ("dml", "dwp", "dps", "dsink")}
    g_dml, g_dwp, g_dps, g_dsink, g_loss = all_gather(
        [stacked["dml"].reshape(64, D), stacked["dwp"].reshape(2 * n_grp * GROUP, GROUP), stacked["dps"].reshape(16, POOL_W),
         stacked["dsink"].reshape(16, BLK), loss_blk], "gather_small")
    tot, rows, fin = reduce_small(g_dml.reshape(N_DEV, 2, 2, 16, D), g_loss, "reduce_small")
    s_dwp, s_dps, s_dsink = sum8([g_dwp, g_dps, g_dsink], "sum_pool_sink")
    grads.update(
        w_pool=s_dwp.reshape(2, n_grp, GROUP, GROUP), pool_scale=s_dps.reshape(2, 8, POOL_W)[:, 0],
        sink=s_dsink.reshape(2, 8, BLK)[:, 0, :N_HEADS], b_mod=tot[:, :N_MOD].reshape(2, N_MOD * D),
        norm_ffn1=tot[:, N_MOD], norm_mix=tot[:, N_MOD + 1], norm_ffn2=tot[:, N_MOD + 2], norm_final=fin[0])
    loss = fin[1, 0]

    dmod_cols = lax.dynamic_slice(rows[:, :, :N_MOD, :].reshape(2, 16, N_MOD * D), (0, 0, slot * MOD_COLS), (2, 16, MOD_COLS))
    grads["w_mod"], dc = mod_grads(c_all, dmod_cols, w_mod, "mod_grads")
    (g_dc,) = all_gather([dc], "gather_dc")
    (s_dc,) = sum8([g_dc], "sum_dc")
    (d_c_ctx,) = elementwise(lambda d, z: (0.5 * d * _silu_grad(z),), [s_dc[N_DEV:N_DEV + 1], c_ctx.reshape(1, D)], [F32], "c_ctx_grad")
    grads["c_ctx"] = d_c_ctx.reshape(D)

    given = dict(c_ctx=(c_ctx, m_c_ctx, v_c_ctx), w_mod=(w_mod, m_w_mod, v_w_mod), b_mod=(b_mod, m_b_mod, v_b_mod),
                 norm_ffn1=(norm_ffn1, m_norm_ffn1, v_norm_ffn1), w_ffn1_in=(w_ffn1_in, m_w_ffn1_in, v_w_ffn1_in),
                 w_ffn1_out=(w_ffn1_out, m_w_ffn1_out, v_w_ffn1_out), norm_mix=(norm_mix, m_norm_mix, v_norm_mix),
                 w_in=(w_in, m_w_in, v_w_in), w_pool=(w_pool, m_w_pool, v_w_pool),
                 pool_scale=(pool_scale, m_pool_scale, v_pool_scale), sink=(sink, m_sink, v_sink), w_out=(w_out, m_w_out, v_w_out),
                 norm_ffn2=(norm_ffn2, m_norm_ffn2, v_norm_ffn2), w_ffn2_in=(w_ffn2_in, m_w_ffn2_in, v_w_ffn2_in),
                 w_ffn2_out=(w_ffn2_out, m_w_ffn2_out, v_w_ffn2_out), norm_final=(norm_final, m_norm_final, v_norm_final))
    g_out, d_out, m_out, v_out = [], [], [], []
    for name, (w, m, v) in given.items():
        if name in big_grads:
            grad, delta, new_m, new_v = adamw_layers(w, *big_grads[name], m, v, f"adamw_{name}")
        else:
            grad = grads[name]
            delta, new_m, new_v = adamw(w, grad, m, v, f"adamw_{name}")
        g_out.append(grad)
        d_out.append(delta)
        m_out.append(new_m)
        v_out.append(new_v)
    return (loss, dx[None], *g_out, *d_out, *m_out, *v_out)
```

```python
import jax
import jax.numpy as jnp
from jax import lax
from jax.experimental import pallas as pl
from jax.experimental.pallas import tpu as pltpu

F32, BF16 = jnp.float32, jnp.bfloat16
D = 1024
D_FF = 2816
N_SLOT = 4
FF_COLS = 2 * D_FF // N_SLOT
N_MOD = 9
MOD_COLS = N_MOD * D // N_SLOT
POOL_W, ATTN_W, KV_W = 512, 512, 128
PROJ_W = POOL_W + ATTN_W + 2 * KV_W
N_HEADS, Q_GROUP, HEAD = 8, 4, 64
GROUP = 128
POOL_WINDOWS = (2, 4, 8, 16)
BLK = 128
GRID_W = 64
ROPE_BASE = 10000.0
EPS = 1e-6
NEG_INF = -1e30
TM = 256
N_DEV = 8
VMEM_LIMIT_BYTES = 56 * 1024 * 1024
ADAM_LR, ADAM_B1, ADAM_B2, ADAM_EPS, ADAM_WD, ADAM_STEP = 0.001, 0.9, 0.999, 1e-08, 0.01, 10
MESH = pl.DeviceIdType.MESH
NT = (((1,), (1,)), ((), ()))
TN = (((0,), (0,)), ((), ()))


def _params(*sem):
    return pltpu.CompilerParams(dimension_semantics=sem, vmem_limit_bytes=VMEM_LIMIT_BYTES)


def _whole(shape, lead=()):
    idx = tuple(lead) + (0,) * len(shape)
    return pl.BlockSpec((None,) * len(lead) + tuple(shape), lambda *_: idx, pipeline_mode=pl.Buffered(1))


def _rows(cols, tm=TM):
    return pl.BlockSpec((tm, cols), lambda i: (i, 0))


def _mods_spec(layer, n_lat):
    return pl.BlockSpec((None, None, 16, D), lambda i: (layer, (i >= n_lat).astype(jnp.int32), 0, 0))


def _acc_spec(n_lat):
    return pl.BlockSpec((None, 8, D), lambda i: ((i >= n_lat).astype(jnp.int32), 0, 0))


def _dot(a, b):
    return jnp.dot(a, b, preferred_element_type=F32)


def _dotg(a, b, dims):
    return lax.dot_general(a, b, dims, preferred_element_type=F32)


def _sum0(v):
    return jnp.sum(v, axis=0, keepdims=True)


def _norm_mod(h, g, shift, scale):
    r = lax.rsqrt(jnp.mean(h * h, axis=-1, keepdims=True) + EPS)
    xhat = h * r
    y = xhat * g
    return y * (1 + scale) + shift, xhat, r, y


def _norm_mod_bwd(dn, xhat, r, y, g, scale):
    dy = dn * (1 + scale)
    dx = dy * g
    dh = r * (dx - xhat * jnp.mean(dx * xhat, axis=-1, keepdims=True))
    return _sum0(dn), _sum0(dn * y), _sum0(dy * xhat), dh


def _swap_halves(v):
    w = v.shape[1]
    lane = lax.broadcasted_iota(jnp.int32, v.shape, 1)
    return jnp.where(lane % HEAD < HEAD // 2, pltpu.roll(v, w - HEAD // 2, axis=1), pltpu.roll(v, HEAD // 2, axis=1))


def _tile_lanes(t, width):
    return t if width == t.shape[1] else jnp.concatenate([t] * (width // t.shape[1]), axis=1)


def _rope(v, cos, sin):
    return v * _tile_lanes(cos, v.shape[1]) + _swap_halves(v) * _tile_lanes(sin, v.shape[1])


def _unrope(g, cos, sin):
    return g * _tile_lanes(cos, g.shape[1]) + _swap_halves(g * _tile_lanes(sin, g.shape[1]))


def ffn_fwd(h, mods, g, w4, wo, layer, k0, n_lat, name, ex=None):
    s = h.shape[0]

    def body(h_ref, m_ref, g_ref, w_ref, wo_ref, ho_ref, ab_ref, f_ref):
        hh = h_ref[...]
        n, _, _, _ = _norm_mod(hh, g_ref[...], m_ref[k0:k0 + 1, :], m_ref[k0 + 1:k0 + 2, :])
        nb = n.astype(BF16)
        acc = jnp.zeros((TM, D), F32)
        for j in range(2):
            a = _dot(nb, w_ref[j])
            b = _dot(nb, w_ref[2 + j])
            ab_ref[:, j * FF_COLS:(j + 1) * FF_COLS] = a.astype(BF16)
            ab_ref[:, (2 + j) * FF_COLS:(3 + j) * FF_COLS] = b.astype(BF16)
            act = (a * jax.nn.sigmoid(a) * b).astype(BF16)
            acc = acc + _dot(act, wo_ref[j * FF_COLS:(j + 1) * FF_COLS, :])
        f_ref[...] = acc
        ho_ref[...] = hh + 0.5 * m_ref[k0 + 2:k0 + 3, :] * acc

    return _grid_call(
        body, name, s // TM,
        [_rows(D), _mods_spec(layer, n_lat), _whole((1, D), (layer,)), _whole((N_SLOT, D, FF_COLS)), _whole((D_FF, D))],
        [_rows(D), _rows(2 * D_FF), _rows(D)],
        [jax.ShapeDtypeStruct((s, D), F32), jax.ShapeDtypeStruct((s, 2 * D_FF), BF16), jax.ShapeDtypeStruct((s, D), F32)],
        (h, mods, g, w4, wo), "parallel", ex)


def ffn_bwd(h, ab, f, dh, mods, g, w4, wo, layer, k0, n_lat, name, ex=None):
    s = h.shape[0]

    def body(h_ref, ab_ref, f_ref, dh_ref, m_ref, g_ref, w_ref, wo_ref, dhi_ref, dab_ref, df_ref, n_ref, act_ref, dm_ref):
        i = pl.program_id(0)

        @pl.when((i == 0) | (i == n_lat))
        def _():
            dm_ref[...] = jnp.zeros_like(dm_ref)

        hh, dho, gg = h_ref[...], dh_ref[...], g_ref[...]
        scale, gate = m_ref[k0 + 1:k0 + 2, :], m_ref[k0 + 2:k0 + 3, :]
        n, xhat, r, y = _norm_mod(hh, gg, m_ref[k0:k0 + 1, :], scale)
        n_ref[...] = n.astype(BF16)
        dgate = _sum0(dho * (0.5 * f_ref[...]))
        dfb = ((0.5 * gate) * dho).astype(BF16)
        df_ref[...] = dfb
        dn = jnp.zeros((TM, D), F32)
        for j in range(2):
            a = ab_ref[:, j * FF_COLS:(j + 1) * FF_COLS].astype(F32)
            b = ab_ref[:, (2 + j) * FF_COLS:(3 + j) * FF_COLS].astype(F32)
            sg = jax.nn.sigmoid(a)
            sa = a * sg
            act_ref[:, j * FF_COLS:(j + 1) * FF_COLS] = (sa * b).astype(BF16)
            dact = _dotg(dfb, wo_ref[j * FF_COLS:(j + 1) * FF_COLS, :], NT)
            da = (dact * b * (sg * (1 + a * (1 - sg)))).astype(BF16)
            db = (dact * sa).astype(BF16)
            dab_ref[:, j * FF_COLS:(j + 1) * FF_COLS] = da
            dab_ref[:, (2 + j) * FF_COLS:(3 + j) * FF_COLS] = db
            dn = dn + _dotg(da, w_ref[j], NT) + _dotg(db, w_ref[2 + j], NT)
        dsh, dsc, dg, dhn = _norm_mod_bwd(dn, xhat, r, y, gg, scale)
        dhi_ref[...] = dho + dhn
        dm_ref[0:1, :] += dsh
        dm_ref[1:2, :] += dsc
        dm_ref[2:3, :] += dgate
        dm_ref[3:4, :] += dg

    return _grid_call(
        body, name, s // TM,
        [_rows(D), _rows(2 * D_FF), _rows(D), _rows(D), _mods_spec(layer, n_lat), _whole((1, D), (layer,)),
         _whole((N_SLOT, D, FF_COLS)), _whole((D_FF, D))],
        [_rows(D), _rows(2 * D_FF), _rows(D), _rows(D), _rows(D_FF), _acc_spec(n_lat)],
        [jax.ShapeDtypeStruct((s, D), F32), jax.ShapeDtypeStruct((s, 2 * D_FF), BF16), jax.ShapeDtypeStruct((s, D), BF16),
         jax.ShapeDtypeStruct((s, D), BF16), jax.ShapeDtypeStruct((s, D_FF), BF16), jax.ShapeDtypeStruct((2, 8, D), F32)],
        (h, ab, f, dh, mods, g, w4, wo), "arbitrary", ex)


def _token_tile(s, limit=2176):
    return max(ts for ts in range(16, limit + 1, 16) if s % ts == 0)


def wgrad(a, b, tk, tn, slot_cols, name):
    s, k = a.shape
    n = b.shape[1]
    ts = _token_tile(s)
    steps = s // ts

    def body(a_ref, b_ref, o_ref, o16_ref):
        r = _dotg(a_ref[...], b_ref[...], TN)
        si = pl.program_id(2)

        @pl.when(si == 0)
        def _():
            o_ref[...] = r

        @pl.when(si > 0)
        def _():
            o_ref[...] += r

        @pl.when(si == steps - 1)
        def _():
            o16_ref[...] = o_ref[...].astype(BF16)

    if slot_cols is None:
        shape, spec = (k, n), pl.BlockSpec((tk, tn), lambda i, j, si: (i, j))
    else:
        per = slot_cols // tn
        shape, spec = (n // slot_cols, k, slot_cols), pl.BlockSpec((None, tk, tn), lambda i, j, si: (lax.div(j, per), i, lax.rem(j, per)))
    return pl.pallas_call(
        body, name=name, grid=(k // tk, n // tn, steps),
        in_specs=[pl.BlockSpec((ts, tk), lambda i, j, si: (si, i)), pl.BlockSpec((ts, tn), lambda i, j, si: (si, j))],
        out_specs=[spec, spec],
        out_shape=[jax.ShapeDtypeStruct(shape, F32), jax.ShapeDtypeStruct(shape, BF16)],
        compiler_params=_params("parallel", "parallel", "arbitrary"),
    )(a, b)


def proj_fwd(h, mods, g, w_in, cos, sin, layer, n_lat, name):
    s = h.shape[0]

    def body(h_ref, m_ref, g_ref, w_ref, cos_ref, sin_ref, u_ref, q_ref, k_ref, v_ref):
        n, _, _, _ = _norm_mod(h_ref[...], g_ref[...], m_ref[3:4, :], m_ref[4:5, :])
        p = _dot(n.astype(BF16), w_ref[...])
        cs, sn = cos_ref[...], sin_ref[...]
        u_ref[...] = p[:, :POOL_W]
        q_ref[...] = (_rope(p[:, POOL_W:POOL_W + ATTN_W], cs, sn) * HEAD ** -0.5).astype(BF16)
        k_ref[...] = _rope(p[:, POOL_W + ATTN_W:POOL_W + ATTN_W + KV_W], cs, sn).astype(BF16)
        v_ref[...] = p[:, POOL_W + ATTN_W + KV_W:].astype(BF16)

    return pl.pallas_call(
        body, name=name, grid=(s // TM,),
        in_specs=[_rows(D), _mods_spec(layer, n_lat), _whole((1, D), (layer,)), _whole((D, PROJ_W)),
                  _rows(BLK), _rows(BLK)],
        out_specs=[_rows(POOL_W), _rows(ATTN_W), _rows(KV_W), _rows(KV_W)],
        out_shape=[jax.ShapeDtypeStruct((s, POOL_W), F32), jax.ShapeDtypeStruct((s, ATTN_W), BF16),
                   jax.ShapeDtypeStruct((s, KV_W), BF16), jax.ShapeDtypeStruct((s, KV_W), BF16)],
        compiler_params=_params("parallel"),
    )(h, mods, g, w_in, cos, sin)


def proj_bwd(h, du, dq, dk, dv, dh, mods, g, w_in, cos, sin, layer, n_lat, name):
    s = h.shape[0]

    def body(h_ref, du_ref, dq_ref, dk_ref, dv_ref, dh_ref, m_ref, g_ref, w_ref, cos_ref, sin_ref,
             dhi_ref, dp_ref, n_ref, dm_ref):
        i = pl.program_id(0)

        @pl.when((i == 0) | (i == n_lat))
        def _():
            dm_ref[...] = jnp.zeros_like(dm_ref)

        gg, scale = g_ref[...], m_ref[4:5, :]
        n, xhat, r, y = _norm_mod(h_ref[...], gg, m_ref[3:4, :], scale)
        n_ref[...] = n.astype(BF16)
        cs, sn = cos_ref[...], sin_ref[...]
        dp = jnp.concatenate([du_ref[...], _unrope(dq_ref[...], cs, sn) * HEAD ** -0.5, _unrope(dk_ref[...], cs, sn),
                              dv_ref[...]], axis=1).astype(BF16)
        dp_ref[...] = dp
        dsh, dsc, dg, dhn = _norm_mod_bwd(_dotg(dp, w_ref[...], NT), xhat, r, y, gg, scale)
        dhi_ref[...] = dh_ref[...] + dhn
        dm_ref[0:1, :] += dsh
        dm_ref[1:2, :] += dsc
        dm_ref[3:4, :] += dg

    return pl.pallas_call(
        body, name=name, grid=(s // TM,),
        in_specs=[_rows(D), _rows(POOL_W), _rows(ATTN_W), _rows(KV_W), _rows(KV_W), _rows(D), _mods_spec(layer, n_lat),
                  _whole((1, D), (layer,)), _whole((D, PROJ_W)), _rows(BLK), _rows(BLK)],
        out_specs=[_rows(D), _rows(PROJ_W), _rows(D), _acc_spec(n_lat)],
        out_shape=[jax.ShapeDtypeStruct((s, D), F32), jax.ShapeDtypeStruct((s, PROJ_W), BF16),
                   jax.ShapeDtypeStruct((s, D), BF16), jax.ShapeDtypeStruct((2, 8, D), F32)],
        compiler_params=_params("arbitrary"),
    )(h, du, dq, dk, dv, dh, mods, g, w_in, cos, sin)


def _window(i, n_lat_blk):
    return pl.multiple_of(jnp.clip(i - 1, 0, n_lat_blk - 1) * BLK, BLK)


def _pool_band(i, ws, w, seq_lo, seq_hi, transposed):
    shape = (3 * BLK, BLK) if transposed else (BLK, 3 * BLK)
    q = i * BLK + lax.broadcasted_iota(jnp.int32, shape, 1 if transposed else 0)
    k = ws + lax.broadcasted_iota(jnp.int32, shape, 0 if transposed else 1)
    band = (k >= jnp.maximum(q - w // 2, seq_lo)) & (k < jnp.minimum(q + w - w // 2, seq_hi))
    qc = i * BLK + lax.broadcasted_iota(jnp.int32, (BLK, 1), 0)
    cnt = jnp.minimum(qc + w - w // 2, seq_hi) - jnp.maximum(qc - w // 2, seq_lo)
    return jnp.where(band, 1.0, 0.0).astype(BF16), cnt.astype(F32)


def _split_dot(band, v):
    hi = v.astype(BF16)
    return _dot(band, hi) + _dot(band, (v - hi.astype(F32)).astype(BF16))


def _pooled(u_ref, i, ws, seq_lo, seq_hi, gi):
    band, cnt = _pool_band(i, ws, POOL_WINDOWS[gi], seq_lo, seq_hi, False)
    cols = slice(gi * GROUP, (gi + 1) * GROUP)
    mean = _split_dot(band, u_ref[pl.ds(ws, 3 * BLK), cols]) / cnt
    return mean - u_ref[pl.ds(pl.multiple_of(i * BLK, BLK), BLK), cols]


def _local_valid(i, ws, t):
    q = i * BLK + lax.broadcasted_iota(jnp.int32, (BLK, 3 * BLK), 0)
    k = ws + lax.broadcasted_iota(jnp.int32, (BLK, 3 * BLK), 1)
    return (i * BLK < t) & (k < t) & (jnp.abs(k - q) <= BLK)


def _head_cols(hd):
    return slice(hd * HEAD, (hd + 1) * HEAD)


def _keys_bias(i, ws, t, n_ctx):
    return jnp.concatenate([jnp.where(_local_valid(i, ws, t), 0.0, NEG_INF), jnp.zeros((BLK, n_ctx), F32)], axis=1)


def _stack_heads(x, hk, first=0):
    return jnp.concatenate([x[:, first + (Q_GROUP * hk + g) * HEAD:first + (Q_GROUP * hk + g + 1) * HEAD]
                            for g in range(Q_GROUP)], axis=0)


def _biased(scores, bias):
    return (scores.reshape(Q_GROUP, BLK, -1) + bias).reshape(Q_GROUP * BLK, -1)


def _group_column(vals):
    row = lax.broadcasted_iota(jnp.int32, (Q_GROUP * BLK, 1), 0)
    out = jnp.full((Q_GROUP * BLK, 1), vals[Q_GROUP - 1], F32)
    for g in range(Q_GROUP - 2, -1, -1):
        out = jnp.where(row < (g + 1) * BLK, vals[g], out)
    return out


def _lane_place(cols, width=BLK):
    lane = lax.broadcasted_iota(jnp.int32, (cols[0].shape[0], width), 1)
    out = jnp.zeros((cols[0].shape[0], width), F32)
    for hd, c in enumerate(cols):
        out = jnp.where(lane == hd, c, out)
    return out


def mix_fwd(h, q, k, v, u, w_pool, pool_scale, sink, w_out, mods, layer, t, name, ex=None):
    s = h.shape[0]
    n_lat_blk = t // BLK

    def body(h_ref, q_ref, k_ref, v_ref, u_ref, wp_ref, ps_ref, sink_ref, wo_ref, m_ref, ho_ref, cat_ref, lse_ref):
        i = pl.program_id(0)
        ws = _window(i, n_lat_blk)
        is_lat = i < n_lat_blk
        seq_lo, seq_hi = jnp.where(is_lat, 0, t), jnp.where(is_lat, t, s)
        for gi in range(len(POOL_WINDOWS)):
            mixed = _dot(_pooled(u_ref, i, ws, seq_lo, seq_hi, gi).astype(BF16), wp_ref[gi])
            cat_ref[:, gi * GROUP:(gi + 1) * GROUP] = (mixed * ps_ref[:, gi * GROUP:(gi + 1) * GROUP]).astype(BF16)
        bias = _keys_bias(i, ws, t, s - t)
        k_all = jnp.concatenate([k_ref[pl.ds(ws, 3 * BLK), :], k_ref[t:s, :]], axis=0)
        v_all = jnp.concatenate([v_ref[pl.ds(ws, 3 * BLK), :], v_ref[t:s, :]], axis=0)
        lses = []
        for hk in range(N_HEADS // Q_GROUP):
            kv = _head_cols(hk)
            sc = _biased(_dotg(_stack_heads(q_ref[...], hk), k_all[:, kv], NT), bias)
            sk = _group_column([sink_ref[layer, Q_GROUP * hk + g] for g in range(Q_GROUP)])
            m = jnp.maximum(jnp.max(sc, axis=1, keepdims=True), sk)
            e = jnp.exp(sc - m)
            l = jnp.sum(e, axis=1, keepdims=True) + jnp.exp(sk - m)
            o = _dot(e.astype(BF16), v_all[:, kv]) * (1.0 / l)
            lse = m + jnp.log(l)
            for g in range(Q_GROUP):
                hd = Q_GROUP * hk + g
                cat_ref[:, POOL_W + hd * HEAD:POOL_W + (hd + 1) * HEAD] = o[g * BLK:(g + 1) * BLK].astype(BF16)
                lses.append(lse[g * BLK:(g + 1) * BLK])
        lse_ref[...] = _lane_place(lses)
        ho_ref[...] = h_ref[...] + m_ref[5:6, :] * _dot(cat_ref[...], wo_ref[...])

    blk = lambda cols: _rows(cols, BLK)
    return _grid_call(
        body, name, s // BLK,
        [blk(D), blk(ATTN_W), _whole((s, KV_W)), _whole((s, KV_W)), _whole((s, POOL_W)),
         _whole((len(POOL_WINDOWS), GROUP, GROUP), (layer,)), _whole((1, POOL_W), (layer,)),
         pl.BlockSpec(memory_space=pltpu.SMEM), _whole((POOL_W + ATTN_W, D)), _mods_spec(layer, n_lat_blk)],
        [blk(D), blk(POOL_W + ATTN_W), blk(BLK)],
        [jax.ShapeDtypeStruct((s, D), F32), jax.ShapeDtypeStruct((s, POOL_W + ATTN_W), BF16), jax.ShapeDtypeStruct((s, BLK), F32)],
        (h, q, k, v, u, w_pool, pool_scale, sink, w_out, mods), "parallel", ex)


def mix_bwd(dh, cat, q, k, v, u, lse, w_pool, pool_scale, sink, w_out, mods, layer, t, name, ex=None):
    s = dh.shape[0]
    n_lat_blk = t // BLK
    n_grp = len(POOL_WINDOWS)

    def body(dh_ref, cat_ref, q_ref, k_ref, v_ref, u_ref, lse_ref, wp_ref, ps_ref, sink_ref, wo_ref, m_ref,
             dq_ref, dk_ref, dv_ref, du_ref, dmo_ref, dwp_ref, dps_ref, dsink_ref, dm_ref):
        i = pl.program_id(0)

        @pl.when(i == 0)
        def _():
            for ref in (dk_ref, dv_ref, du_ref, dwp_ref, dps_ref, dsink_ref):
                ref[...] = jnp.zeros_like(ref)

        @pl.when((i == 0) | (i == n_lat_blk))
        def _():
            dm_ref[...] = jnp.zeros_like(dm_ref)

        ws = _window(i, n_lat_blk)
        here = pl.ds(pl.multiple_of(i * BLK, BLK), BLK)
        is_lat = i < n_lat_blk
        seq_lo, seq_hi = jnp.where(is_lat, 0, t), jnp.where(is_lat, t, s)
        dho = dh_ref[...]
        dm_ref[2:3, :] += _sum0(dho * _dot(cat_ref[...], wo_ref[...]))
        dmo = (m_ref[5:6, :] * dho).astype(BF16)
        dmo_ref[...] = dmo
        dcat = _dotg(dmo, wo_ref[...], NT)

        for gi in range(n_grp):
            cols = slice(gi * GROUP, (gi + 1) * GROUP)
            pooled = _pooled(u_ref, i, ws, seq_lo, seq_hi, gi).astype(BF16)
            dpo = dcat[:, cols]
            dps_ref[0:1, cols] += _sum0(dpo * _dot(pooled, wp_ref[gi]))
            dmixed = (dpo * ps_ref[:, cols]).astype(BF16)
            dwp_ref[gi] += _dotg(pooled, dmixed, TN)
            dpooled = _dotg(dmixed, wp_ref[gi], NT)
            band_t, cnt = _pool_band(i, ws, POOL_WINDOWS[gi], seq_lo, seq_hi, True)
            du_ref[pl.ds(ws, 3 * BLK), cols] += _split_dot(band_t, dpooled / cnt)
            du_ref[here, cols] -= dpooled

        bias = _keys_bias(i, ws, t, s - t)
        k_all = jnp.concatenate([k_ref[pl.ds(ws, 3 * BLK), :], k_ref[t:s, :]], axis=0)
        v_all = jnp.concatenate([v_ref[pl.ds(ws, 3 * BLK), :], v_ref[t:s, :]], axis=0)
        qq, lse_all = q_ref[...], lse_ref[...]
        dqs, dsinks, dks, dvs = [], [], [], []
        for hk in range(N_HEADS // Q_GROUP):
            kv = _head_cols(hk)
            q4 = _stack_heads(qq, hk)
            lse = jnp.concatenate([lse_all[:, Q_GROUP * hk + g:Q_GROUP * hk + g + 1] for g in range(Q_GROUP)], axis=0)
            p = jnp.exp(_biased(_dotg(q4, k_all[:, kv], NT), bias) - lse)
            do = _stack_heads(dcat, hk, POOL_W).astype(BF16)
            dp = _dotg(do, v_all[:, kv], NT)
            delta = jnp.sum(p * dp, axis=1, keepdims=True)
            ds = (p * (dp - delta)).astype(BF16)
            sk = _group_column([sink_ref[layer, Q_GROUP * hk + g] for g in range(Q_GROUP)])
            dsk = -jnp.exp(sk - lse) * delta
            dq = _dot(ds, k_all[:, kv])
            for g in range(Q_GROUP):
                dqs.append(dq[g * BLK:(g + 1) * BLK])
                dsinks.append(_sum0(dsk[g * BLK:(g + 1) * BLK]))
            dks.append(_dotg(ds, q4, TN))
            dvs.append(_dotg(p.astype(BF16), do, TN))
        dq_ref[...] = jnp.concatenate(dqs, axis=1)
        dk, dv = jnp.concatenate(dks, axis=1), jnp.concatenate(dvs, axis=1)
        dk_ref[pl.ds(ws, 3 * BLK), :] += dk[:3 * BLK]
        dv_ref[pl.ds(ws, 3 * BLK), :] += dv[:3 * BLK]
        dk_ref[t:s, :] += dk[3 * BLK:]
        dv_ref[t:s, :] += dv[3 * BLK:]
        dsink_ref[0:1, :] += _lane_place(dsinks)

    blk = lambda cols: _rows(cols, BLK)
    full = lambda shape: pl.BlockSpec(shape, lambda i: (0,) * len(shape))
    return _grid_call(
        body, name, s // BLK,
        [blk(D), blk(POOL_W + ATTN_W), blk(ATTN_W), _whole((s, KV_W)), _whole((s, KV_W)), _whole((s, POOL_W)),
         blk(BLK), _whole((n_grp, GROUP, GROUP), (layer,)), _whole((1, POOL_W), (layer,)),
         pl.BlockSpec(memory_space=pltpu.SMEM), _whole((POOL_W + ATTN_W, D)), _mods_spec(layer, n_lat_blk)],
        [blk(ATTN_W), full((s, KV_W)), full((s, KV_W)), full((s, POOL_W)), blk(D),
         full((n_grp, GROUP, GROUP)), full((8, POOL_W)), full((8, BLK)), _acc_spec(n_lat_blk)],
        [jax.ShapeDtypeStruct((s, ATTN_W), F32), jax.ShapeDtypeStruct((s, KV_W), F32),
         jax.ShapeDtypeStruct((s, KV_W), F32), jax.ShapeDtypeStruct((s, POOL_W), F32),
         jax.ShapeDtypeStruct((s, D), BF16), jax.ShapeDtypeStruct((n_grp, GROUP, GROUP), F32),
         jax.ShapeDtypeStruct((8, POOL_W), F32), jax.ShapeDtypeStruct((8, BLK), F32), jax.ShapeDtypeStruct((2, 8, D), F32)],
        (dh, cat, q, k, v, u, lse, w_pool, pool_scale, sink, w_out, mods), "arbitrary", ex)


def loss_head(h, target, g, t, name):
    s = h.shape[0]
    n_lat = t // TM

    def body(h_ref, t_ref, g_ref, dh_ref, acc_ref):
        i = pl.program_id(0)

        @pl.when(i == 0)
        def _():
            acc_ref[...] = jnp.zeros_like(acc_ref)

        @pl.when(i < n_lat)
        def _():
            hh, gg = h_ref[...], g_ref[...]
            r = lax.rsqrt(jnp.mean(hh * hh, axis=-1, keepdims=True) + EPS)
            xhat = hh * r
            err = xhat * gg - t_ref[...]
            dy = err * (1.0 / D)
            dx = dy * gg
            dh_ref[...] = r * (dx - xhat * jnp.mean(dx * xhat, axis=-1, keepdims=True))
            acc_ref[0:1, :] += _sum0(dy * xhat)
            acc_ref[1:2, :] += _sum0(err * err)

        @pl.when(i >= n_lat)
        def _():
            dh_ref[...] = jnp.zeros_like(dh_ref)

    return pl.pallas_call(
        body, name=name, grid=(s // TM,),
        in_specs=[_rows(D), pl.BlockSpec((TM, D), lambda i: (jnp.minimum(i, n_lat - 1), 0)), _whole((1, D))],
        out_specs=[_rows(D), pl.BlockSpec((8, D), lambda i: (0, 0))],
        out_shape=[jax.ShapeDtypeStruct((s, D), F32), jax.ShapeDtypeStruct((8, D), F32)],
        compiler_params=_params("arbitrary"),
    )(h, target, g)


def mod_rows(c_all, w_mod, b_cols, name):
    def body(c_ref, w_ref, b_ref, o_ref):
        cc = c_ref[...]
        o_ref[...] = _dot((cc * jax.nn.sigmoid(cc)).astype(BF16), w_ref[...].astype(BF16)) + b_ref[...]

    return pl.pallas_call(
        body, name=name, grid=(2,),
        in_specs=[pl.BlockSpec((16, D), lambda l: (0, 0)), pl.BlockSpec((None, D, MOD_COLS), lambda l: (l, 0, 0)),
                  pl.BlockSpec((None, 1, MOD_COLS), lambda l: (l, 0, 0))],
        out_specs=pl.BlockSpec((None, 16, MOD_COLS), lambda l: (l, 0, 0)),
        out_shape=jax.ShapeDtypeStruct((2, 16, MOD_COLS), F32),
        compiler_params=_params("parallel"),
    )(c_all, w_mod, b_cols)


def mod_grads(c_all, dmod_cols, w_mod, name):
    def body(c_ref, d_ref, w_ref, dw_ref, dc_ref):
        @pl.when(pl.program_id(0) == 0)
        def _():
            dc_ref[...] = jnp.zeros_like(dc_ref)

        cc = c_ref[...]
        dd = d_ref[...].astype(BF16)
        dw_ref[...] = _dotg((cc * jax.nn.sigmoid(cc)).astype(BF16), dd, TN)
        dc_ref[...] += _dotg(dd, w_ref[...].astype(BF16), NT)

    return pl.pallas_call(
        body, name=name, grid=(2,),
        in_specs=[pl.BlockSpec((16, D), lambda l: (0, 0)), pl.BlockSpec((None, 16, MOD_COLS), lambda l: (l, 0, 0)),
                  pl.BlockSpec((None, D, MOD_COLS), lambda l: (l, 0, 0))],
        out_specs=[pl.BlockSpec((None, D, MOD_COLS), lambda l: (l, 0, 0)), pl.BlockSpec((16, D), lambda l: (0, 0))],
        out_shape=[jax.ShapeDtypeStruct((2, D, MOD_COLS), F32), jax.ShapeDtypeStruct((16, D), F32)],
        compiler_params=_params("arbitrary"),
    )(c_all, dmod_cols, w_mod)


def _row_tile(rows, cols, n_arrays):
    budget = VMEM_LIMIT_BYTES // 4 // (2 * 4 * n_arrays * cols)
    best = None
    for tr in range(16, rows + 1, 16):
        if rows % tr == 0 and tr <= budget:
            best = tr
    return best if best is not None else rows


def elementwise(fn, ins, out_dtypes, name):
    rows, cols = ins[0].shape
    tr = _row_tile(rows, cols, len(ins) + len(out_dtypes))

    def body(*refs):
        outs = fn(*[r[...] for r in refs[:len(ins)]])
        for o_ref, o in zip(refs[len(ins):], outs):
            o_ref[...] = o.astype(o_ref.dtype)

    spec = pl.BlockSpec((tr, cols), lambda i: (i, 0))
    return pl.pallas_call(
        body, name=name, grid=(rows // tr,), in_specs=[spec] * len(ins), out_specs=[spec] * len(out_dtypes),
        out_shape=[jax.ShapeDtypeStruct((rows, cols), dt) for dt in out_dtypes],
        compiler_params=_params("parallel"),
    )(*ins)


def _adamw_tile(w, g, m, v):
    m = ADAM_B1 * m + (1.0 - ADAM_B1) * g
    v = ADAM_B2 * v + (1.0 - ADAM_B2) * (g * g)
    m_hat = m / (1.0 - ADAM_B1 ** ADAM_STEP)
    v_hat = v / (1.0 - ADAM_B2 ** ADAM_STEP)
    return -ADAM_LR * (m_hat / (jnp.sqrt(v_hat) + ADAM_EPS) + ADAM_WD * w), m, v


def adamw(w, g, m, v, name):
    shape = w.shape
    two_d = (-1, shape[-1]) if w.ndim > 1 else (1, -1)
    outs = elementwise(_adamw_tile, [a.reshape(two_d) for a in (w, g, m, v)], [F32] * 3, name)
    return [o.reshape(shape) for o in outs]


def _prefetch_call(body, name, grid, in_specs, out_specs, out_shape, place, args, aliases=None):
    spec = pltpu.PrefetchScalarGridSpec(num_scalar_prefetch=1, grid=grid, in_specs=in_specs, out_specs=out_specs)
    return pl.pallas_call(body, name=name, grid_spec=spec, out_shape=out_shape, input_output_aliases=aliases or {},
                          compiler_params=_params(*["parallel"] * len(grid)))(place, *args)


def cast_place(w, layer, place, name):
    _, r, c = w.shape
    tr = _row_tile(r, c, 2)

    def body(p_ref, w_ref, o_ref):
        o_ref[...] = w_ref[...].astype(BF16)

    return _prefetch_call(
        body, name, (r // tr,), [pl.BlockSpec((None, tr, c), lambda i, p: (layer, i, 0))],
        pl.BlockSpec((None, tr, c), lambda i, p: (p[1], i, 0)), jax.ShapeDtypeStruct((N_SLOT, r, c), BF16), place, [w])


def pair_sum(g32, got, place, name):
    n_slot, rh, c = got.shape
    tr = _row_tile(rh, c, 4)
    per = rh // tr

    def body(p_ref, a_ref, b_ref, o_ref, o16_ref):
        r = a_ref[...] + b_ref[...].astype(F32)
        o_ref[...] = r
        o16_ref[...] = r.astype(BF16)

    half = pl.BlockSpec((None, tr, c), lambda s, i, p: (s, i, 0))
    return _prefetch_call(
        body, name, (n_slot, per), [pl.BlockSpec((None, tr, c), lambda s, i, p: (s, p[0] * per + i, 0)), half], [half, half],
        [jax.ShapeDtypeStruct(got.shape, F32), jax.ShapeDtypeStruct(got.shape, BF16)], place, [g32, got])


def chip_sum(p32, got, place, name):
    _, rh, c = p32.shape
    tr = _row_tile(rh, c, 5)
    per = rh // tr

    def body(p_ref, m_ref, r0_ref, r1_ref, r2_ref, o_ref):
        o_ref[...] = m_ref[...] + r0_ref[...].astype(F32) + r1_ref[...].astype(F32) + r2_ref[...].astype(F32)

    part = pl.BlockSpec((tr, c), lambda i, p: (i, 0))
    return _prefetch_call(
        body, name, (per,), [pl.BlockSpec((None, tr, c), lambda i, p: (p[1], i, 0)), part, part, part],
        pl.BlockSpec((tr, c), lambda i, p: (p[0] * per + i, 0)), jax.ShapeDtypeStruct((2 * rh, c), F32), place, [p32, *got])


def adamw_layers(w, g0, g1, m, v, name):
    _, r, c = w.shape
    tr = _row_tile(r, c, 10)

    def body(w_ref, g0_ref, g1_ref, m_ref, v_ref, g_ref, d_ref, mo_ref, vo_ref):
        g = jnp.where(pl.program_id(0) == 0, g0_ref[...], g1_ref[...])
        g_ref[...] = g
        d_ref[...], mo_ref[...], vo_ref[...] = _adamw_tile(w_ref[...], g, m_ref[...], v_ref[...])

    stacked = pl.BlockSpec((None, tr, c), lambda l, i: (l, i, 0))
    layer = pl.BlockSpec((tr, c), lambda l, i: (i, 0))
    return pl.pallas_call(
        body, name=name, grid=(2, r // tr), in_specs=[stacked, layer, layer, stacked, stacked], out_specs=[stacked] * 4,
        out_shape=[jax.ShapeDtypeStruct(w.shape, F32)] * 4, compiler_params=_params("parallel", "parallel"),
    )(w, g0, g1, m, v)


def sum8(gathered, name):
    def body(*refs):
        n = len(refs) // 2
        for g_ref, o_ref in zip(refs[:n], refs[n:]):
            acc = g_ref[0]
            for dev in range(1, N_DEV):
                acc = acc + g_ref[dev]
            o_ref[...] = acc

    return pl.pallas_call(
        body, name=name,
        out_shape=[jax.ShapeDtypeStruct(a.shape[1:], F32) for a in gathered],
        compiler_params=_params(),
    )(*gathered)


def _place():
    return lax.axis_index("x"), lax.axis_index("y"), lax.axis_index("c")


def _any(n):
    return [pl.BlockSpec(memory_space=pl.ANY)] * n


def all_gather(blocks, name):
    n = len(blocks)

    def body(*refs):
        ins, outs = refs[:n], refs[n:2 * n]
        send_sems, recv_sems, local_sems = refs[2 * n:]
        x, y, c = _place()
        me, sibling = (x, y, c), (x, y, 1 - c)
        chips = [(1 - x, y), (x, 1 - y), (1 - x, 1 - y)]

        def copy(ti, k, block, to, src=None):
            dst = outs[ti].at[4 * block[0] + 2 * block[1] + block[2]]
            return pltpu.make_async_remote_copy(src_ref=dst if src is None else src, dst_ref=dst, send_sem=send_sems.at[ti, k],
                                                recv_sem=recv_sems.at[ti, k], device_id=to, device_id_type=MESH)

        local, sent = [], []
        for ti in range(n):
            local.append(pltpu.make_async_copy(ins[ti], outs[ti].at[4 * x + 2 * y + c], local_sems.at[ti]))
            sent.append(copy(ti, 0, me, sibling, src=ins[ti]))
            sent += [copy(ti, 1 + j, me, (*chip, c), src=ins[ti]) for j, chip in enumerate(chips)]
        for cp in local + sent:
            cp.start()
        for ti in range(n):
            for j, chip in enumerate(chips):
                copy(ti, 1 + j, (*chip, c), me).wait_recv()
                sent.append(copy(ti, 4 + j, (*chip, c), sibling))
                sent[-1].start()
        for ti in range(n):
            copy(ti, 0, sibling, me).wait_recv()
            for j, chip in enumerate(chips):
                copy(ti, 4 + j, (*chip, 1 - c), me).wait_recv()
        for cp in sent:
            cp.wait_send()
        for cp in local:
            cp.wait()

    return pl.pallas_call(
        body, name=name, in_specs=_any(n), out_specs=_any(n),
        out_shape=[jax.ShapeDtypeStruct((N_DEV,) + b.shape, b.dtype) for b in blocks],
        scratch_shapes=[pltpu.SemaphoreType.DMA((n, 7)), pltpu.SemaphoreType.DMA((n, 7)), pltpu.SemaphoreType.DMA((n,))],
    )(*blocks)


def _three_chips(x, y):
    return [(1 - x, y), (x, 1 - y), (1 - x, 1 - y)]


def gather_exchange(placed):
    n = len(placed)

    def copy(bufs, sems, ti, k, chip, core, to):
        rh = bufs[ti].shape[1] // 2
        half = bufs[ti].at[2 * chip[0] + chip[1], pl.ds(core * rh, rh), :]
        return pltpu.make_async_remote_copy(src_ref=half, dst_ref=half, send_sem=sems[0].at[ti, k], recv_sem=sems[1].at[ti, k],
                                            device_id=to, device_id_type=MESH)

    def sends(bufs, sems):
        x, y, c = _place()
        return [copy(bufs, sems, ti, k, (x, y), c, (*chip, c)) for ti in range(n) for k, chip in enumerate(_three_chips(x, y))]

    def start(ins, bufs, sems):
        for cp in sends(bufs, sems):
            cp.start()

    def finish(ins, bufs, sems):
        x, y, c = _place()
        chips = _three_chips(x, y)
        passed = []
        for ti in range(n):
            for k, chip in enumerate(chips):
                copy(bufs, sems, ti, k, chip, c, (x, y, c)).wait_recv()
                passed.append(copy(bufs, sems, ti, 3 + k, chip, c, (x, y, 1 - c)))
                passed[-1].start()
        for ti in range(n):
            for k, chip in enumerate(chips):
                copy(bufs, sems, ti, 3 + k, chip, 1 - c, (x, y, c)).wait_recv()
        for cp in sends(bufs, sems) + passed:
            cp.wait_send()

    return dict(ins=list(placed), out_shape=[jax.ShapeDtypeStruct(w.shape, w.dtype) for w in placed],
                aliases={i: i for i in range(n)}, start=start, finish=finish,
                scratch=[pltpu.SemaphoreType.DMA((n, 6)), pltpu.SemaphoreType.DMA((n, 6))])


def scatter_exchange(p16):
    n = len(p16)

    def copies(ins, got, sems):
        x, y, c = _place()
        return [pltpu.make_async_remote_copy(src_ref=ins[ti].at[2 * chip[0] + chip[1]], dst_ref=got[3 * ti + k],
                                             send_sem=sems[0].at[ti, k], recv_sem=sems[1].at[ti, k], device_id=(*chip, c),
                                             device_id_type=MESH)
                for ti in range(n) for k, chip in enumerate(_three_chips(x, y))]

    def start(ins, got, sems):
        for cp in copies(ins, got, sems):
            cp.start()

    def finish(ins, got, sems):
        for cp in copies(ins, got, sems):
            cp.wait()

    return dict(ins=list(p16), out_shape=[jax.ShapeDtypeStruct(a.shape[1:], BF16) for a in p16 for _ in range(3)], aliases={},
                start=start, finish=finish, scratch=[pltpu.SemaphoreType.DMA((n, 3)), pltpu.SemaphoreType.DMA((n, 3))])


def run_exchange(ex, name):
    ci, co = len(ex["ins"]), len(ex["out_shape"])

    def body(*refs):
        ins, outs, sems = refs[:ci], refs[ci:ci + co], refs[ci + co:]
        ex["start"](ins, outs, sems)
        ex["finish"](ins, outs, sems)

    return pl.pallas_call(body, name=name, in_specs=_any(ci), out_specs=_any(co), out_shape=ex["out_shape"],
                          input_output_aliases=ex["aliases"], scratch_shapes=ex["scratch"])(*ex["ins"])


def _grid_call(body, name, steps, in_specs, out_specs, out_shape, args, sem, ex=None):
    n_in, n_out = len(in_specs), len(out_specs)
    if ex is None:
        return pl.pallas_call(body, name=name, grid=(steps,), in_specs=in_specs, out_specs=out_specs, out_shape=out_shape,
                              compiler_params=_params(sem))(*args), []
    ci, co = len(ex["ins"]), len(ex["out_shape"])

    def carrying(*refs):
        c_in, c_out = refs[n_in:n_in + ci], refs[n_in + ci + n_out:n_in + ci + n_out + co]
        sems = refs[n_in + ci + n_out + co:]

        @pl.when(pl.program_id(0) == 0)
        def _():
            ex["start"](c_in, c_out, sems)

        body(*refs[:n_in], *refs[n_in + ci:n_in + ci + n_out])

        @pl.when(pl.program_id(0) == steps - 1)
        def _():
            ex["finish"](c_in, c_out, sems)

    outs = pl.pallas_call(
        carrying, name=name, grid=(steps,), in_specs=list(in_specs) + _any(ci), out_specs=list(out_specs) + _any(co),
        out_shape=list(out_shape) + ex["out_shape"], scratch_shapes=ex["scratch"],
        input_output_aliases={n_in + i: n_out + j for i, j in ex["aliases"].items()}, compiler_params=_params("arbitrary"),
    )(*args, *ex["ins"])
    return outs[:n_out], outs[n_out:]


def pair_exchange(g16, name):
    n = len(g16)

    def body(*refs):
        a16, got = refs[:n], refs[n:2 * n]
        send_sems, recv_sems = refs[2 * n:]
        x, y, c = _place()
        copies = []
        for ti in range(n):
            rh = a16[ti].shape[1] // 2
            copies.append(pltpu.make_async_remote_copy(
                src_ref=a16[ti].at[:, pl.ds((1 - c) * rh, rh), :], dst_ref=got[ti], send_sem=send_sems.at[ti],
                recv_sem=recv_sems.at[ti], device_id=(x, y, 1 - c), device_id_type=MESH))
        for cp in copies:
            cp.start()
        for cp in copies:
            cp.wait()

    return pl.pallas_call(
        body, name=name, in_specs=_any(n), out_specs=_any(n),
        out_shape=[jax.ShapeDtypeStruct((a.shape[0], a.shape[1] // 2, a.shape[2]), BF16) for a in g16],
        scratch_shapes=[pltpu.SemaphoreType.DMA((n,)), pltpu.SemaphoreType.DMA((n,))],
    )(*g16)


def pair_gather(halves, name):
    n = len(halves)

    def body(*refs):
        bufs = refs[n:2 * n]
        send_sems, recv_sems = refs[2 * n:]
        x, y, c = _place()
        copies = []
        for ti in range(n):
            rh = bufs[ti].shape[0] // 2
            rows = bufs[ti].at[pl.ds(c * rh, rh), :]
            copies.append(pltpu.make_async_remote_copy(src_ref=rows, dst_ref=rows, send_sem=send_sems.at[ti],
                                                       recv_sem=recv_sems.at[ti], device_id=(x, y, 1 - c), device_id_type=MESH))
        for cp in copies:
            cp.start()
        for ti, cp in enumerate(copies):
            cp.wait_send()
            rh = bufs[ti].shape[0] // 2
            theirs = bufs[ti].at[pl.ds((1 - c) * rh, rh), :]
            pltpu.make_async_remote_copy(src_ref=theirs, dst_ref=theirs, send_sem=send_sems.at[ti], recv_sem=recv_sems.at[ti],
                                         device_id=(x, y, 1 - c), device_id_type=MESH).wait_recv()

    return pl.pallas_call(
        body, name=name, in_specs=_any(n), out_specs=_any(n), input_output_aliases={i: i for i in range(n)},
        out_shape=[jax.ShapeDtypeStruct(a.shape, a.dtype) for a in halves],
        scratch_shapes=[pltpu.SemaphoreType.DMA((n,)), pltpu.SemaphoreType.DMA((n,))],
    )(*halves)


def reduce_small(dm_f1, dm_mix, dm_gate, dm_f2, loss_blk, name):
    def body(f1_ref, mix_ref, gate_ref, f2_ref, l_ref, tot_ref, rows_ref, fin_ref):
        rows_ref[...] = jnp.zeros_like(rows_ref)
        tot_ref[...] = jnp.zeros_like(tot_ref)
        mod_src = [(f1_ref, 0), (f1_ref, 1), (f1_ref, 2), (mix_ref, 0), (mix_ref, 1), (gate_ref, 2),
                   (f2_ref, 0), (f2_ref, 1), (f2_ref, 2)]
        norm_src = [(f1_ref, 3), (mix_ref, 3), (f2_ref, 3)]
        for l in range(2):
            for k, (ref, r) in enumerate(mod_src + norm_src):
                lat = ref[0, l, 0, r:r + 1, :]
                ctx = ref[0, l, 1, r:r + 1, :]
                for dev in range(N_DEV):
                    if dev:
                        lat = lat + ref[dev, l, 0, r:r + 1, :]
                        ctx = ctx + ref[dev, l, 1, r:r + 1, :]
                    if k < N_MOD:
                        rows_ref[l, dev, k:k + 1, :] = ref[dev, l, 0, r:r + 1, :]
                if k < N_MOD:
                    rows_ref[l, N_DEV, k:k + 1, :] = ctx
                tot_ref[l, k:k + 1, :] = lat + ctx
        acc = l_ref[0]
        for dev in range(1, N_DEV):
            acc = acc + l_ref[dev]
        loss = (0.5 / D) * jnp.sum(acc[1:2, :], axis=1, keepdims=True)
        row = lax.broadcasted_iota(jnp.int32, (8, D), 0)
        fin_ref[...] = jnp.where(row == 0, acc[0:1, :], loss)

    return pl.pallas_call(
        body, name=name,
        out_shape=[jax.ShapeDtypeStruct((2, 16, D), F32), jax.ShapeDtypeStruct((2, 16, 16, D), F32),
                   jax.ShapeDtypeStruct((8, D), F32)],
        compiler_params=_params(),
    )(dm_f1, dm_mix, dm_gate, dm_f2, loss_blk)


def rope_tables(t, s):
    rows = t // GRID_W
    row = jnp.repeat(jnp.arange(rows), GRID_W).astype(F32)
    col = jnp.tile(jnp.arange(GRID_W), rows).astype(F32)
    inv = ROPE_BASE ** (-jnp.arange(0, HEAD // 2, 2, dtype=F32) / (HEAD // 2))
    ang = jnp.concatenate([row[:, None] * inv, col[:, None] * inv], axis=-1)
    cos, sin = jnp.cos(ang), jnp.sin(ang)
    cos = jnp.concatenate([jnp.tile(cos, (1, 4)), jnp.ones((s - t, BLK), F32)], axis=0)
    sin = jnp.concatenate([jnp.tile(jnp.concatenate([-sin, sin], axis=1), (1, 2)), jnp.zeros((s - t, BLK), F32)], axis=0)
    return cos, sin


BIG = ("ffn1_in", "ffn1_out", "w_in", "w_out", "ffn2_in", "ffn2_out")
GROUPS = dict(ffn1=("ffn1_in", "ffn1_out"), mix=("w_in", "w_out"), ffn2=("ffn2_in", "ffn2_out"))
GATHER_BEHIND = {("ffn1", 0): [("mix", 0), ("ffn2", 0)], ("mix", 0): [("ffn1", 1)], ("ffn2", 0): [("mix", 1)],
                 ("ffn1", 1): [("ffn2", 1)]}


def _slot_major(name, g):
    if name == "w_in":
        return jnp.stack(jnp.split(g, N_SLOT, axis=1), axis=0)
    if name in ("ffn1_in", "ffn2_in"):
        return g
    return g.reshape(N_SLOT, g.shape[0] // N_SLOT, g.shape[1])


def _whole_weight(name, buf):
    if name == "w_in":
        return buf.transpose(1, 0, 2).reshape(D, PROJ_W)
    if name in ("ffn1_in", "ffn2_in"):
        return buf
    return buf.reshape(-1, buf.shape[2])


def local_step(x1, ctx1, target, mods, norms, nfinal, placed, w_pool, pool_scale, sink, place):
    t, s = x1.shape[0], x1.shape[0] + ctx1.shape[0]
    n_lat = t // TM
    cos, sin = rope_tables(t, s)
    wts = {name: list(pair) for name, pair in placed.items()}

    def gather(groups):
        return gather_exchange([wts[name][l] for grp, l in groups for name in GROUPS[grp]])

    def gathered(groups, arrays):
        arrays = list(arrays)
        for grp, l in groups:
            for name in GROUPS[grp]:
                wts[name][l] = arrays.pop(0)

    def weight(name, l):
        return _whole_weight(name, wts[name][l])

    def fwd_ex(grp, l):
        groups = GATHER_BEHIND.get((grp, l))
        return (groups, gather(groups)) if groups else (None, None)

    gathered([("ffn1", 0)], run_exchange(gather([("ffn1", 0)]), "gather_first"))
    h = jnp.concatenate([x1, ctx1], axis=0)
    saved = []
    for l in range(2):
        h0 = h
        groups, ex = fwd_ex("ffn1", l)
        (h1, ab1, f1), got = ffn_fwd(h0, mods, norms[0], weight("ffn1_in", l), weight("ffn1_out", l), l, 0, n_lat, f"ffn1_fwd_{l}", ex)
        gathered(groups or [], got)
        u, q, k, v = proj_fwd(h1, mods, norms[1], weight("w_in", l), cos, sin, l, n_lat, f"proj_fwd_{l}")
        groups, ex = fwd_ex("mix", l)
        (h2, cat, lse), got = mix_fwd(h1, q, k, v, u, w_pool, pool_scale, sink, weight("w_out", l), mods, l, t, f"mix_fwd_{l}", ex)
        gathered(groups or [], got)
        groups, ex = fwd_ex("ffn2", l)
        (h, ab2, f2), got = ffn_fwd(h2, mods, norms[2], weight("ffn2_in", l), weight("ffn2_out", l), l, 6, n_lat, f"ffn2_fwd_{l}", ex)
        gathered(groups or [], got)
        saved.append((h0, ab1, f1, h1, u, q, k, v, cat, lse, h2, ab2, f2))
    dh, loss_blk = loss_head(h, target, nfinal, t, "loss_head")

    halves = {name: [None, None] for name in BIG}
    pending = []

    def summed_in_pair(grp, l, grads):
        names = GROUPS[grp]
        got = pair_exchange([_slot_major(n, g[1]) for n, g in zip(names, grads)], f"pair_exchange_{grp}_{l}")
        pairs = [pair_sum(_slot_major(n, g[0]), r, place, f"pair_sum_{n}_{l}") for n, g, r in zip(names, grads, got)]
        pending.append((grp, l, pairs))

    def scatter():
        return scatter_exchange([p16 for _, p16 in pending[0][2]]) if pending else None

    def scattered(got):
        if pending:
            grp, l, pairs = pending.pop(0)
            for i, name in enumerate(GROUPS[grp]):
                halves[name][l] = chip_sum(pairs[i][0], got[3 * i:3 * i + 3], place, f"chip_sum_{name}_{l}")

    small = [None, None]
    for l in (1, 0):
        h0, ab1, f1, h1, u, q, k, v, cat, lse, h2, ab2, f2 = saved[l]
        (dh, dab, df, n, act, dm_f2), got = ffn_bwd(h2, ab2, f2, dh, mods, norms[2], weight("ffn2_in", l), weight("ffn2_out", l),
                                                    l, 6, n_lat, f"ffn2_bwd_{l}", scatter())
        scattered(got)
        summed_in_pair("ffn2", l, [wgrad(n, dab, D, FF_COLS, FF_COLS, f"ffn2_in_wgrad_{l}"),
                                   wgrad(act, df, D_FF // 2, D, None, f"ffn2_out_wgrad_{l}")])
        (dq, dk, dv, du, dmo, dwp, dps, dsink, dm_gate), got = mix_bwd(
            dh, cat, q, k, v, u, lse, w_pool, pool_scale, sink, weight("w_out", l), mods, l, t, f"mix_bwd_{l}", scatter())
        scattered(got)
        g_wo = wgrad(cat, dmo, POOL_W + ATTN_W, D, None, f"w_out_wgrad_{l}")
        dh, dp, n, dm_mix = proj_bwd(h1, du, dq, dk, dv, dh, mods, norms[1], weight("w_in", l), cos, sin, l, n_lat, f"proj_bwd_{l}")
        summed_in_pair("mix", l, [wgrad(n, dp, D, PROJ_W // 2, None, f"w_in_wgrad_{l}"), g_wo])
        (dh, dab, df, n, act, dm_f1), got = ffn_bwd(h0, ab1, f1, dh, mods, norms[0], weight("ffn1_in", l), weight("ffn1_out", l),
                                                    l, 0, n_lat, f"ffn1_bwd_{l}", scatter())
        scattered(got)
        summed_in_pair("ffn1", l, [wgrad(n, dab, D, FF_COLS, FF_COLS, f"ffn1_in_wgrad_{l}"),
                                   wgrad(act, df, D_FF // 2, D, None, f"ffn1_out_wgrad_{l}")])
        small[l] = dict(dm_f1=dm_f1, dm_mix=dm_mix, dm_gate=dm_gate, dm_f2=dm_f2, dwp=dwp, dps=dps, dsink=dsink)
    scattered(run_exchange(scatter(), "scatter_last"))
    return dh[:t], loss_blk, halves, small


def _silu_grad(z):
    sg = jax.nn.sigmoid(z)
    return sg * (1 + z * (1 - sg))


def kernel(x, c, ctx, c_ctx, w_mod, b_mod, norm_ffn1, w_ffn1_in, w_ffn1_out, norm_mix, w_in, w_pool, pool_scale, sink, w_out, norm_ffn2, w_ffn2_in, w_ffn2_out, norm_final, loss_target, m_c_ctx, m_w_mod, m_b_mod, m_norm_ffn1, m_w_ffn1_in, m_w_ffn1_out, m_norm_mix, m_w_in, m_w_pool, m_pool_scale, m_sink, m_w_out, m_norm_ffn2, m_w_ffn2_in, m_w_ffn2_out, m_norm_final, v_c_ctx, v_w_mod, v_b_mod, v_norm_ffn1, v_w_ffn1_in, v_w_ffn1_out, v_norm_mix, v_w_in, v_w_pool, v_pool_scale, v_sink, v_w_out, v_norm_ffn2, v_w_ffn2_in, v_w_ffn2_out, v_norm_final):
    px, py, pc = _place()
    slot, me = 2 * px + py, 4 * px + 2 * py + pc
    n_grp = len(POOL_WINDOWS)

    (c_rows,) = all_gather([c.reshape(8, D // 8)], "gather_c")
    c_all = jnp.concatenate([c_rows.reshape(N_DEV, D), c_ctx.reshape(1, D), jnp.zeros((16 - N_DEV - 1, D), F32)], axis=0)
    b_cols = lax.dynamic_slice(b_mod, (0, slot * MOD_COLS), (2, MOD_COLS)).reshape(2, 1, MOD_COLS)
    (mod_parts,) = all_gather([mod_rows(c_all, w_mod, b_cols, "mod_rows")], "gather_mods")
    mods_all = mod_parts[0::2].transpose(1, 2, 0, 3).reshape(2, 16, N_MOD * D)
    mx = lax.dynamic_slice(mods_all, (0, me, 0), (2, 1, N_MOD * D)).reshape(2, N_MOD, D)
    mc = mods_all[:, N_DEV].reshape(2, N_MOD, D)
    pad = jnp.zeros((2, 16 - N_MOD, D), F32)
    mods = jnp.stack([jnp.concatenate([mx, pad], axis=1), jnp.concatenate([mc, pad], axis=1)], axis=1)

    place = jnp.stack([pc, slot]).astype(jnp.int32)
    shards = dict(ffn1_in=w_ffn1_in, ffn1_out=w_ffn1_out, w_in=w_in, w_out=w_out, ffn2_in=w_ffn2_in, ffn2_out=w_ffn2_out)
    placed = {name: [cast_place(shards[name], l, place, f"cast_{name}_{l}") for l in range(2)] for name in BIG}
    norms = [g.reshape(2, 1, D) for g in (norm_ffn1, norm_mix, norm_ffn2)]
    dx, loss_blk, halves, small = local_step(x[0], ctx[0], loss_target[0], mods, norms, norm_final.reshape(1, D), placed,
                                             w_pool.astype(BF16), pool_scale.reshape(2, 1, POOL_W), sink, place)
    shard_grads = pair_gather([halves[name][l] for name in BIG for l in range(2)], "grad_pair_gather")
    big_grads = dict(zip(("w_ffn1_in", "w_ffn1_out", "w_in", "w_out", "w_ffn2_in", "w_ffn2_out"), zip(*[iter(shard_grads)] * 2)))
    grads = {}

    row_sums = ("dm_f1", "dm_mix", "dm_gate", "dm_f2")
    stacked = {k: jnp.stack([small[0][k], small[1][k]]) for k in row_sums + ("dwp", "dps", "dsink")}
    *g_dm, g_dwp, g_dps, g_dsink, g_loss = all_gather(
        [stacked[k].reshape(32, D) for k in row_sums]
        + [stacked["dwp"].reshape(2 * n_grp * GROUP, GROUP), stacked["dps"].reshape(16, POOL_W), stacked["dsink"].reshape(16, BLK),
           loss_blk], "gather_small")
    tot, rows, fin = reduce_small(*[g.reshape(N_DEV, 2, 2, 8, D) for g in g_dm], g_loss, "reduce_small")
    s_dwp, s_dps, s_dsink = sum8([g_dwp, g_dps, g_dsink], "sum_pool_sink")
    grads.update(
        w_pool=s_dwp.reshape(2, n_grp, GROUP, GROUP), pool_scale=s_dps.reshape(2, 8, POOL_W)[:, 0],
        sink=s_dsink.reshape(2, 8, BLK)[:, 0, :N_HEADS], b_mod=tot[:, :N_MOD].reshape(2, N_MOD * D),
        norm_ffn1=tot[:, N_MOD], norm_mix=tot[:, N_MOD + 1], norm_ffn2=tot[:, N_MOD + 2], norm_final=fin[0])
    loss = fin[1, 0]

    dmod_cols = lax.dynamic_slice(rows[:, :, :N_MOD, :].reshape(2, 16, N_MOD * D), (0, 0, slot * MOD_COLS), (2, 16, MOD_COLS))
    grads["w_mod"], dc = mod_grads(c_all, dmod_cols, w_mod, "mod_grads")
    (g_dc,) = all_gather([dc], "gather_dc")
    (s_dc,) = sum8([g_dc], "sum_dc")
    (d_c_ctx,) = elementwise(lambda d, z: (0.5 * d * _silu_grad(z),), [s_dc[N_DEV:N_DEV + 1], c_ctx.reshape(1, D)], [F32], "c_ctx_grad")
    grads["c_ctx"] = d_c_ctx.reshape(D)

    given = dict(c_ctx=(c_ctx, m_c_ctx, v_c_ctx), w_mod=(w_mod, m_w_mod, v_w_mod), b_mod=(b_mod, m_b_mod, v_b_mod),
                 norm_ffn1=(norm_ffn1, m_norm_ffn1, v_norm_ffn1), w_ffn1_in=(w_ffn1_in, m_w_ffn1_in, v_w_ffn1_in),
                 w_ffn1_out=(w_ffn1_out, m_w_ffn1_out, v_w_ffn1_out), norm_mix=(norm_mix, m_norm_mix, v_norm_mix),
                 w_in=(w_in, m_w_in, v_w_in), w_pool=(w_pool, m_w_pool, v_w_pool),
                 pool_scale=(pool_scale, m_pool_scale, v_pool_scale), sink=(sink, m_sink, v_sink), w_out=(w_out, m_w_out, v_w_out),
                 norm_ffn2=(norm_ffn2, m_norm_ffn2, v_norm_ffn2), w_ffn2_in=(w_ffn2_in, m_w_ffn2_in, v_w_ffn2_in),
                 w_ffn2_out=(w_ffn2_out, m_w_ffn2_out, v_w_ffn2_out), norm_final=(norm_final, m_norm_final, v_norm_final))
    g_out, d_out, m_out, v_out = [], [], [], []
    for name, (w, m, v) in given.items():
        if name in big_grads:
            grad, delta, new_m, new_v = adamw_layers(w, *big_grads[name], m, v, f"adamw_{name}")
        else:
            grad = grads[name]
            delta, new_m, new_v = adamw(w, grad, m, v, f"adamw_{name}")
        g_out.append(grad)
        d_out.append(delta)
        m_out.append(new_m)
        v_out.append(new_v)
    return (loss, dx[None], *g_out, *d_out, *m_out, *v_out)
```

```python
import functools

import jax
import jax.numpy as jnp
from jax import lax
from jax.experimental import pallas as pl
from jax.experimental.pallas import tpu as pltpu

F32, BF16 = jnp.float32, jnp.bfloat16
D = 1024
D_FF = 2816
N_SLOT = 4
FF_COLS = 2 * D_FF // N_SLOT
N_MOD = 9
MOD_COLS = N_MOD * D // N_SLOT
POOL_W, ATTN_W, KV_W = 512, 512, 128
PROJ_W = POOL_W + ATTN_W + 2 * KV_W
N_HEADS, Q_GROUP, HEAD = 8, 4, 64
GROUP = 128
POOL_WINDOWS = (2, 4, 8, 16)
BLK = 128
GRID_W = 64
ROPE_BASE = 10000.0
EPS = 1e-6
NEG_INF = -1e30
TM = 256
N_DEV = 8
VMEM_LIMIT_BYTES = 56 * 1024 * 1024
ADAM_LR, ADAM_B1, ADAM_B2, ADAM_EPS, ADAM_WD, ADAM_STEP = 0.001, 0.9, 0.999, 1e-08, 0.01, 10
MESH = pl.DeviceIdType.MESH
NT = (((1,), (1,)), ((), ()))
TN = (((0,), (0,)), ((), ()))


def _params(*sem):
    return pltpu.CompilerParams(dimension_semantics=sem, vmem_limit_bytes=VMEM_LIMIT_BYTES)


def _whole(shape, lead=()):
    idx = tuple(lead) + (0,) * len(shape)
    return pl.BlockSpec((None,) * len(lead) + tuple(shape), lambda *_: idx, pipeline_mode=pl.Buffered(1))


def _rows(cols, tm=TM):
    return pl.BlockSpec((tm, cols), lambda i: (i, 0))


def _mods_spec(layer, n_lat):
    return pl.BlockSpec((None, None, 16, D), lambda i: (layer, (i >= n_lat).astype(jnp.int32), 0, 0))


def _acc_spec(n_lat):
    return pl.BlockSpec((None, 8, D), lambda i: ((i >= n_lat).astype(jnp.int32), 0, 0))


def _dot(a, b):
    return jnp.dot(a, b, preferred_element_type=F32)


def _dotg(a, b, dims):
    return lax.dot_general(a, b, dims, preferred_element_type=F32)


def _sum0(v):
    return jnp.sum(v, axis=0, keepdims=True)


def _norm_mod(h, g, shift, scale):
    r = lax.rsqrt(jnp.mean(h * h, axis=-1, keepdims=True) + EPS)
    xhat = h * r
    y = xhat * g
    return y * (1 + scale) + shift, xhat, r, y


def _norm_mod_bwd(dn, xhat, r, y, g, scale):
    dy = dn * (1 + scale)
    dx = dy * g
    dh = r * (dx - xhat * jnp.mean(dx * xhat, axis=-1, keepdims=True))
    return _sum0(dn), _sum0(dn * y), _sum0(dy * xhat), dh


def _swap_halves(v):
    w = v.shape[1]
    lane = lax.broadcasted_iota(jnp.int32, v.shape, 1)
    return jnp.where(lane % HEAD < HEAD // 2, pltpu.roll(v, w - HEAD // 2, axis=1), pltpu.roll(v, HEAD // 2, axis=1))


def _tile_lanes(t, width):
    return t if width == t.shape[1] else jnp.concatenate([t] * (width // t.shape[1]), axis=1)


def _rope(v, cos, sin):
    return v * _tile_lanes(cos, v.shape[1]) + _swap_halves(v) * _tile_lanes(sin, v.shape[1])


def _unrope(g, cos, sin):
    return g * _tile_lanes(cos, g.shape[1]) + _swap_halves(g * _tile_lanes(sin, g.shape[1]))


def ffn_fwd(h, mods, g, w4, wo, layer, k0, n_lat, name, ex=None):
    s = h.shape[0]

    def body(h_ref, m_ref, g_ref, w_ref, wo_ref, ho_ref, ab_ref, f_ref):
        hh = h_ref[...]
        n, _, _, _ = _norm_mod(hh, g_ref[...], m_ref[k0:k0 + 1, :], m_ref[k0 + 1:k0 + 2, :])
        nb = n.astype(BF16)
        acc = jnp.zeros((TM, D), F32)
        for j in range(2):
            a = _dot(nb, w_ref[j])
            b = _dot(nb, w_ref[2 + j])
            ab_ref[:, j * FF_COLS:(j + 1) * FF_COLS] = a.astype(BF16)
            ab_ref[:, (2 + j) * FF_COLS:(3 + j) * FF_COLS] = b.astype(BF16)
            act = (a * jax.nn.sigmoid(a) * b).astype(BF16)
            acc = acc + _dot(act, wo_ref[j * FF_COLS:(j + 1) * FF_COLS, :])
        f_ref[...] = acc
        ho_ref[...] = hh + 0.5 * m_ref[k0 + 2:k0 + 3, :] * acc

    return _grid_call(
        body, name, s // TM,
        [_rows(D), _mods_spec(layer, n_lat), _whole((1, D), (layer,)), _whole((N_SLOT, D, FF_COLS)), _whole((D_FF, D))],
        [_rows(D), _rows(2 * D_FF), _rows(D)],
        [jax.ShapeDtypeStruct((s, D), F32), jax.ShapeDtypeStruct((s, 2 * D_FF), BF16), jax.ShapeDtypeStruct((s, D), F32)],
        (h, mods, g, w4, wo), "parallel", ex)


def ffn_bwd(h, ab, f, dh, mods, g, w4, wo, layer, k0, n_lat, name, ex=None):
    s = h.shape[0]

    def body(h_ref, ab_ref, f_ref, dh_ref, m_ref, g_ref, w_ref, wo_ref, dhi_ref, dab_ref, df_ref, n_ref, act_ref, dm_ref):
        i = pl.program_id(0)

        @pl.when((i == 0) | (i == n_lat))
        def _():
            dm_ref[...] = jnp.zeros_like(dm_ref)

        hh, dho, gg = h_ref[...], dh_ref[...], g_ref[...]
        scale, gate = m_ref[k0 + 1:k0 + 2, :], m_ref[k0 + 2:k0 + 3, :]
        n, xhat, r, y = _norm_mod(hh, gg, m_ref[k0:k0 + 1, :], scale)
        n_ref[...] = n.astype(BF16)
        dgate = _sum0(dho * (0.5 * f_ref[...]))
        dfb = ((0.5 * gate) * dho).astype(BF16)
        df_ref[...] = dfb
        dn = jnp.zeros((TM, D), F32)
        for j in range(2):
            a = ab_ref[:, j * FF_COLS:(j + 1) * FF_COLS].astype(F32)
            b = ab_ref[:, (2 + j) * FF_COLS:(3 + j) * FF_COLS].astype(F32)
            sg = jax.nn.sigmoid(a)
            sa = a * sg
            act_ref[:, j * FF_COLS:(j + 1) * FF_COLS] = (sa * b).astype(BF16)
            dact = _dotg(dfb, wo_ref[j * FF_COLS:(j + 1) * FF_COLS, :], NT)
            da = (dact * b * (sg * (1 + a * (1 - sg)))).astype(BF16)
            db = (dact * sa).astype(BF16)
            dab_ref[:, j * FF_COLS:(j + 1) * FF_COLS] = da
            dab_ref[:, (2 + j) * FF_COLS:(3 + j) * FF_COLS] = db
            dn = dn + _dotg(da, w_ref[j], NT) + _dotg(db, w_ref[2 + j], NT)
        dsh, dsc, dg, dhn = _norm_mod_bwd(dn, xhat, r, y, gg, scale)
        dhi_ref[...] = dho + dhn
        dm_ref[0:1, :] += dsh
        dm_ref[1:2, :] += dsc
        dm_ref[2:3, :] += dgate
        dm_ref[3:4, :] += dg

    return _grid_call(
        body, name, s // TM,
        [_rows(D), _rows(2 * D_FF), _rows(D), _rows(D), _mods_spec(layer, n_lat), _whole((1, D), (layer,)),
         _whole((N_SLOT, D, FF_COLS)), _whole((D_FF, D))],
        [_rows(D), _rows(2 * D_FF), _rows(D), _rows(D), _rows(D_FF), _acc_spec(n_lat)],
        [jax.ShapeDtypeStruct((s, D), F32), jax.ShapeDtypeStruct((s, 2 * D_FF), BF16), jax.ShapeDtypeStruct((s, D), BF16),
         jax.ShapeDtypeStruct((s, D), BF16), jax.ShapeDtypeStruct((s, D_FF), BF16), jax.ShapeDtypeStruct((2, 8, D), F32)],
        (h, ab, f, dh, mods, g, w4, wo), "arbitrary", ex)


def _token_tile(s, limit=2176):
    return max(ts for ts in range(16, limit + 1, 16) if s % ts == 0)


def wgrad(a, b, tk, tn, slot_cols, name, ex=None):
    s, k = a.shape
    n = b.shape[1]
    ts = _token_tile(s)
    steps = s // ts

    def body(a_ref, b_ref, o_ref, o16_ref):
        r = _dotg(a_ref[...], b_ref[...], TN)
        si = pl.program_id(2)

        @pl.when(si == 0)
        def _():
            o_ref[...] = r

        @pl.when(si > 0)
        def _():
            o_ref[...] += r

        @pl.when(si == steps - 1)
        def _():
            o16_ref[...] = o_ref[...].astype(BF16)

    if slot_cols is None:
        shape, spec = (k, n), pl.BlockSpec((tk, tn), lambda i, j, si: (i, j))
    else:
        per = slot_cols // tn
        shape, spec = (n // slot_cols, k, slot_cols), pl.BlockSpec((None, tk, tn), lambda i, j, si: (lax.div(j, per), i, lax.rem(j, per)))
    return _grid_call(
        body, name, (k // tk, n // tn, steps),
        [pl.BlockSpec((ts, tk), lambda i, j, si: (si, i)), pl.BlockSpec((ts, tn), lambda i, j, si: (si, j))], [spec, spec],
        [jax.ShapeDtypeStruct(shape, F32), jax.ShapeDtypeStruct(shape, BF16)], (a, b), ("parallel", "parallel", "arbitrary"), ex)


def proj_fwd(h, mods, g, w_in, cos, sin, layer, n_lat, name):
    s = h.shape[0]

    def body(h_ref, m_ref, g_ref, w_ref, cos_ref, sin_ref, u_ref, q_ref, k_ref, v_ref):
        n, _, _, _ = _norm_mod(h_ref[...], g_ref[...], m_ref[3:4, :], m_ref[4:5, :])
        p = _dot(n.astype(BF16), w_ref[...])
        cs, sn = cos_ref[...], sin_ref[...]
        u_ref[...] = p[:, :POOL_W]
        q_ref[...] = (_rope(p[:, POOL_W:POOL_W + ATTN_W], cs, sn) * HEAD ** -0.5).astype(BF16)
        k_ref[...] = _rope(p[:, POOL_W + ATTN_W:POOL_W + ATTN_W + KV_W], cs, sn).astype(BF16)
        v_ref[...] = p[:, POOL_W + ATTN_W + KV_W:].astype(BF16)

    return pl.pallas_call(
        body, name=name, grid=(s // TM,),
        in_specs=[_rows(D), _mods_spec(layer, n_lat), _whole((1, D), (layer,)), _whole((D, PROJ_W)),
                  _rows(BLK), _rows(BLK)],
        out_specs=[_rows(POOL_W), _rows(ATTN_W), _rows(KV_W), _rows(KV_W)],
        out_shape=[jax.ShapeDtypeStruct((s, POOL_W), F32), jax.ShapeDtypeStruct((s, ATTN_W), BF16),
                   jax.ShapeDtypeStruct((s, KV_W), BF16), jax.ShapeDtypeStruct((s, KV_W), BF16)],
        compiler_params=_params("parallel"),
    )(h, mods, g, w_in, cos, sin)


def proj_bwd(h, du, dq, dk, dv, dh, mods, g, w_in, cos, sin, layer, n_lat, name):
    s = h.shape[0]

    def body(h_ref, du_ref, dq_ref, dk_ref, dv_ref, dh_ref, m_ref, g_ref, w_ref, cos_ref, sin_ref,
             dhi_ref, dp_ref, n_ref, dm_ref):
        i = pl.program_id(0)

        @pl.when((i == 0) | (i == n_lat))
        def _():
            dm_ref[...] = jnp.zeros_like(dm_ref)

        gg, scale = g_ref[...], m_ref[4:5, :]
        n, xhat, r, y = _norm_mod(h_ref[...], gg, m_ref[3:4, :], scale)
        n_ref[...] = n.astype(BF16)
        cs, sn = cos_ref[...], sin_ref[...]
        dp = jnp.concatenate([du_ref[...], _unrope(dq_ref[...], cs, sn) * HEAD ** -0.5, _unrope(dk_ref[...], cs, sn),
                              dv_ref[...]], axis=1).astype(BF16)
        dp_ref[...] = dp
        dsh, dsc, dg, dhn = _norm_mod_bwd(_dotg(dp, w_ref[...], NT), xhat, r, y, gg, scale)
        dhi_ref[...] = dh_ref[...] + dhn
        dm_ref[0:1, :] += dsh
        dm_ref[1:2, :] += dsc
        dm_ref[3:4, :] += dg

    return pl.pallas_call(
        body, name=name, grid=(s // TM,),
        in_specs=[_rows(D), _rows(POOL_W), _rows(ATTN_W), _rows(KV_W), _rows(KV_W), _rows(D), _mods_spec(layer, n_lat),
                  _whole((1, D), (layer,)), _whole((D, PROJ_W)), _rows(BLK), _rows(BLK)],
        out_specs=[_rows(D), _rows(PROJ_W), _rows(D), _acc_spec(n_lat)],
        out_shape=[jax.ShapeDtypeStruct((s, D), F32), jax.ShapeDtypeStruct((s, PROJ_W), BF16),
                   jax.ShapeDtypeStruct((s, D), BF16), jax.ShapeDtypeStruct((2, 8, D), F32)],
        compiler_params=_params("arbitrary"),
    )(h, du, dq, dk, dv, dh, mods, g, w_in, cos, sin)


def _window(i, n_lat_blk):
    return pl.multiple_of(jnp.clip(i - 1, 0, n_lat_blk - 1) * BLK, BLK)


def _pool_band(i, ws, w, seq_lo, seq_hi, transposed):
    shape = (3 * BLK, BLK) if transposed else (BLK, 3 * BLK)
    q = i * BLK + lax.broadcasted_iota(jnp.int32, shape, 1 if transposed else 0)
    k = ws + lax.broadcasted_iota(jnp.int32, shape, 0 if transposed else 1)
    band = (k >= jnp.maximum(q - w // 2, seq_lo)) & (k < jnp.minimum(q + w - w // 2, seq_hi))
    qc = i * BLK + lax.broadcasted_iota(jnp.int32, (BLK, 1), 0)
    cnt = jnp.minimum(qc + w - w // 2, seq_hi) - jnp.maximum(qc - w // 2, seq_lo)
    return jnp.where(band, 1.0, 0.0).astype(BF16), cnt.astype(F32)


def _split_dot(band, v):
    hi = v.astype(BF16)
    return _dot(band, hi) + _dot(band, (v - hi.astype(F32)).astype(BF16))


def _pooled(u_ref, i, ws, seq_lo, seq_hi, gi):
    band, cnt = _pool_band(i, ws, POOL_WINDOWS[gi], seq_lo, seq_hi, False)
    cols = slice(gi * GROUP, (gi + 1) * GROUP)
    mean = _split_dot(band, u_ref[pl.ds(ws, 3 * BLK), cols]) / cnt
    return mean - u_ref[pl.ds(pl.multiple_of(i * BLK, BLK), BLK), cols]


def _local_valid(i, ws, t):
    q = i * BLK + lax.broadcasted_iota(jnp.int32, (BLK, 3 * BLK), 0)
    k = ws + lax.broadcasted_iota(jnp.int32, (BLK, 3 * BLK), 1)
    return (i * BLK < t) & (k < t) & (jnp.abs(k - q) <= BLK)


def _head_cols(hd):
    return slice(hd * HEAD, (hd + 1) * HEAD)


def _keys_bias(i, ws, t, n_ctx):
    return jnp.concatenate([jnp.where(_local_valid(i, ws, t), 0.0, NEG_INF), jnp.zeros((BLK, n_ctx), F32)], axis=1)


def _stack_heads(x, hk, first=0):
    return jnp.concatenate([x[:, first + (Q_GROUP * hk + g) * HEAD:first + (Q_GROUP * hk + g + 1) * HEAD]
                            for g in range(Q_GROUP)], axis=0)


def _biased(scores, bias):
    return (scores.reshape(Q_GROUP, BLK, -1) + bias).reshape(Q_GROUP * BLK, -1)


def _group_column(vals):
    row = lax.broadcasted_iota(jnp.int32, (Q_GROUP * BLK, 1), 0)
    out = jnp.full((Q_GROUP * BLK, 1), vals[Q_GROUP - 1], F32)
    for g in range(Q_GROUP - 2, -1, -1):
        out = jnp.where(row < (g + 1) * BLK, vals[g], out)
    return out


def _lane_place(cols, width=BLK):
    lane = lax.broadcasted_iota(jnp.int32, (cols[0].shape[0], width), 1)
    out = jnp.zeros((cols[0].shape[0], width), F32)
    for hd, c in enumerate(cols):
        out = jnp.where(lane == hd, c, out)
    return out


def mix_fwd(h, q, k, v, u, w_pool, pool_scale, sink, w_out, mods, layer, t, name, ex=None):
    s = h.shape[0]
    n_lat_blk = t // BLK

    def body(h_ref, q_ref, k_ref, v_ref, u_ref, wp_ref, ps_ref, sink_ref, wo_ref, m_ref, ho_ref, cat_ref, lse_ref):
        i = pl.program_id(0)
        ws = _window(i, n_lat_blk)
        is_lat = i < n_lat_blk
        seq_lo, seq_hi = jnp.where(is_lat, 0, t), jnp.where(is_lat, t, s)
        for gi in range(len(POOL_WINDOWS)):
            mixed = _dot(_pooled(u_ref, i, ws, seq_lo, seq_hi, gi).astype(BF16), wp_ref[gi])
            cat_ref[:, gi * GROUP:(gi + 1) * GROUP] = (mixed * ps_ref[:, gi * GROUP:(gi + 1) * GROUP]).astype(BF16)
        bias = _keys_bias(i, ws, t, s - t)
        k_all = jnp.concatenate([k_ref[pl.ds(ws, 3 * BLK), :], k_ref[t:s, :]], axis=0)
        v_all = jnp.concatenate([v_ref[pl.ds(ws, 3 * BLK), :], v_ref[t:s, :]], axis=0)
        lses = []
        for hk in range(N_HEADS // Q_GROUP):
            kv = _head_cols(hk)
            sc = _biased(_dotg(_stack_heads(q_ref[...], hk), k_all[:, kv], NT), bias)
            sk = _group_column([sink_ref[layer, Q_GROUP * hk + g] for g in range(Q_GROUP)])
            m = jnp.maximum(jnp.max(sc, axis=1, keepdims=True), sk)
            e = jnp.exp(sc - m)
            l = jnp.sum(e, axis=1, keepdims=True) + jnp.exp(sk - m)
            o = _dot(e.astype(BF16), v_all[:, kv]) * (1.0 / l)
            lse = m + jnp.log(l)
            for g in range(Q_GROUP):
                hd = Q_GROUP * hk + g
                cat_ref[:, POOL_W + hd * HEAD:POOL_W + (hd + 1) * HEAD] = o[g * BLK:(g + 1) * BLK].astype(BF16)
                lses.append(lse[g * BLK:(g + 1) * BLK])
        lse_ref[...] = _lane_place(lses)
        ho_ref[...] = h_ref[...] + m_ref[5:6, :] * _dot(cat_ref[...], wo_ref[...])

    blk = lambda cols: _rows(cols, BLK)
    return _grid_call(
        body, name, s // BLK,
        [blk(D), blk(ATTN_W), _whole((s, KV_W)), _whole((s, KV_W)), _whole((s, POOL_W)),
         _whole((len(POOL_WINDOWS), GROUP, GROUP), (layer,)), _whole((1, POOL_W), (layer,)),
         pl.BlockSpec(memory_space=pltpu.SMEM), _whole((POOL_W + ATTN_W, D)), _mods_spec(layer, n_lat_blk)],
        [blk(D), blk(POOL_W + ATTN_W), blk(BLK)],
        [jax.ShapeDtypeStruct((s, D), F32), jax.ShapeDtypeStruct((s, POOL_W + ATTN_W), BF16), jax.ShapeDtypeStruct((s, BLK), F32)],
        (h, q, k, v, u, w_pool, pool_scale, sink, w_out, mods), "parallel", ex)


def mix_bwd(dh, cat, q, k, v, u, lse, w_pool, pool_scale, sink, w_out, mods, layer, t, name, ex=None):
    s = dh.shape[0]
    n_lat_blk = t // BLK
    n_grp = len(POOL_WINDOWS)

    def body(dh_ref, cat_ref, q_ref, k_ref, v_ref, u_ref, lse_ref, wp_ref, ps_ref, sink_ref, wo_ref, m_ref,
             dq_ref, dk_ref, dv_ref, du_ref, dmo_ref, dwp_ref, dps_ref, dsink_ref, dm_ref):
        i = pl.program_id(0)

        @pl.when(i == 0)
        def _():
            for ref in (dk_ref, dv_ref, du_ref, dwp_ref, dps_ref, dsink_ref):
                ref[...] = jnp.zeros_like(ref)

        @pl.when((i == 0) | (i == n_lat_blk))
        def _():
            dm_ref[...] = jnp.zeros_like(dm_ref)

        ws = _window(i, n_lat_blk)
        here = pl.ds(pl.multiple_of(i * BLK, BLK), BLK)
        is_lat = i < n_lat_blk
        seq_lo, seq_hi = jnp.where(is_lat, 0, t), jnp.where(is_lat, t, s)
        dho = dh_ref[...]
        dm_ref[2:3, :] += _sum0(dho * _dot(cat_ref[...], wo_ref[...]))
        dmo = (m_ref[5:6, :] * dho).astype(BF16)
        dmo_ref[...] = dmo
        dcat = _dotg(dmo, wo_ref[...], NT)

        for gi in range(n_grp):
            cols = slice(gi * GROUP, (gi + 1) * GROUP)
            pooled = _pooled(u_ref, i, ws, seq_lo, seq_hi, gi).astype(BF16)
            dpo = dcat[:, cols]
            dps_ref[0:1, cols] += _sum0(dpo * _dot(pooled, wp_ref[gi]))
            dmixed = (dpo * ps_ref[:, cols]).astype(BF16)
            dwp_ref[gi] += _dotg(pooled, dmixed, TN)
            dpooled = _dotg(dmixed, wp_ref[gi], NT)
            band_t, cnt = _pool_band(i, ws, POOL_WINDOWS[gi], seq_lo, seq_hi, True)
            du_ref[pl.ds(ws, 3 * BLK), cols] += _split_dot(band_t, dpooled / cnt)
            du_ref[here, cols] -= dpooled

        bias = _keys_bias(i, ws, t, s - t)
        k_all = jnp.concatenate([k_ref[pl.ds(ws, 3 * BLK), :], k_ref[t:s, :]], axis=0)
        v_all = jnp.concatenate([v_ref[pl.ds(ws, 3 * BLK), :], v_ref[t:s, :]], axis=0)
        qq, lse_all = q_ref[...], lse_ref[...]
        dqs, dsinks, dks, dvs = [], [], [], []
        for hk in range(N_HEADS // Q_GROUP):
            kv = _head_cols(hk)
            q4 = _stack_heads(qq, hk)
            lse = jnp.concatenate([lse_all[:, Q_GROUP * hk + g:Q_GROUP * hk + g + 1] for g in range(Q_GROUP)], axis=0)
            p = jnp.exp(_biased(_dotg(q4, k_all[:, kv], NT), bias) - lse)
            do = _stack_heads(dcat, hk, POOL_W).astype(BF16)
            dp = _dotg(do, v_all[:, kv], NT)
            delta = jnp.sum(p * dp, axis=1, keepdims=True)
            ds = (p * (dp - delta)).astype(BF16)
            sk = _group_column([sink_ref[layer, Q_GROUP * hk + g] for g in range(Q_GROUP)])
            dsk = -jnp.exp(sk - lse) * delta
            dq = _dot(ds, k_all[:, kv])
            for g in range(Q_GROUP):
                dqs.append(dq[g * BLK:(g + 1) * BLK])
                dsinks.append(_sum0(dsk[g * BLK:(g + 1) * BLK]))
            dks.append(_dotg(ds, q4, TN))
            dvs.append(_dotg(p.astype(BF16), do, TN))
        dq_ref[...] = jnp.concatenate(dqs, axis=1)
        dk, dv = jnp.concatenate(dks, axis=1), jnp.concatenate(dvs, axis=1)
        dk_ref[pl.ds(ws, 3 * BLK), :] += dk[:3 * BLK]
        dv_ref[pl.ds(ws, 3 * BLK), :] += dv[:3 * BLK]
        dk_ref[t:s, :] += dk[3 * BLK:]
        dv_ref[t:s, :] += dv[3 * BLK:]
        dsink_ref[0:1, :] += _lane_place(dsinks)

    blk = lambda cols: _rows(cols, BLK)
    full = lambda shape: pl.BlockSpec(shape, lambda i: (0,) * len(shape))
    return _grid_call(
        body, name, s // BLK,
        [blk(D), blk(POOL_W + ATTN_W), blk(ATTN_W), _whole((s, KV_W)), _whole((s, KV_W)), _whole((s, POOL_W)),
         blk(BLK), _whole((n_grp, GROUP, GROUP), (layer,)), _whole((1, POOL_W), (layer,)),
         pl.BlockSpec(memory_space=pltpu.SMEM), _whole((POOL_W + ATTN_W, D)), _mods_spec(layer, n_lat_blk)],
        [blk(ATTN_W), full((s, KV_W)), full((s, KV_W)), full((s, POOL_W)), blk(D),
         full((n_grp, GROUP, GROUP)), full((8, POOL_W)), full((8, BLK)), _acc_spec(n_lat_blk)],
        [jax.ShapeDtypeStruct((s, ATTN_W), F32), jax.ShapeDtypeStruct((s, KV_W), F32),
         jax.ShapeDtypeStruct((s, KV_W), F32), jax.ShapeDtypeStruct((s, POOL_W), F32),
         jax.ShapeDtypeStruct((s, D), BF16), jax.ShapeDtypeStruct((n_grp, GROUP, GROUP), F32),
         jax.ShapeDtypeStruct((8, POOL_W), F32), jax.ShapeDtypeStruct((8, BLK), F32), jax.ShapeDtypeStruct((2, 8, D), F32)],
        (dh, cat, q, k, v, u, lse, w_pool, pool_scale, sink, w_out, mods), "arbitrary", ex)


def loss_head(h, target, g, t, name):
    s = h.shape[0]
    n_lat = t // TM

    def body(h_ref, t_ref, g_ref, dh_ref, acc_ref):
        i = pl.program_id(0)

        @pl.when(i == 0)
        def _():
            acc_ref[...] = jnp.zeros_like(acc_ref)

        @pl.when(i < n_lat)
        def _():
            hh, gg = h_ref[...], g_ref[...]
            r = lax.rsqrt(jnp.mean(hh * hh, axis=-1, keepdims=True) + EPS)
            xhat = hh * r
            err = xhat * gg - t_ref[...]
            dy = err * (1.0 / D)
            dx = dy * gg
            dh_ref[...] = r * (dx - xhat * jnp.mean(dx * xhat, axis=-1, keepdims=True))
            acc_ref[0:1, :] += _sum0(dy * xhat)
            acc_ref[1:2, :] += _sum0(err * err)

        @pl.when(i >= n_lat)
        def _():
            dh_ref[...] = jnp.zeros_like(dh_ref)

    return pl.pallas_call(
        body, name=name, grid=(s // TM,),
        in_specs=[_rows(D), pl.BlockSpec((TM, D), lambda i: (jnp.minimum(i, n_lat - 1), 0)), _whole((1, D))],
        out_specs=[_rows(D), pl.BlockSpec((8, D), lambda i: (0, 0))],
        out_shape=[jax.ShapeDtypeStruct((s, D), F32), jax.ShapeDtypeStruct((8, D), F32)],
        compiler_params=_params("arbitrary"),
    )(h, target, g)


def mod_rows(c_all, w_mod, b_cols, name):
    def body(c_ref, w_ref, b_ref, o_ref):
        cc = c_ref[...]
        o_ref[...] = _dot((cc * jax.nn.sigmoid(cc)).astype(BF16), w_ref[...].astype(BF16)) + b_ref[...]

    return pl.pallas_call(
        body, name=name, grid=(2,),
        in_specs=[pl.BlockSpec((16, D), lambda l: (0, 0)), pl.BlockSpec((None, D, MOD_COLS), lambda l: (l, 0, 0)),
                  pl.BlockSpec((None, 1, MOD_COLS), lambda l: (l, 0, 0))],
        out_specs=pl.BlockSpec((None, 16, MOD_COLS), lambda l: (l, 0, 0)),
        out_shape=jax.ShapeDtypeStruct((2, 16, MOD_COLS), F32),
        compiler_params=_params("parallel"),
    )(c_all, w_mod, b_cols)


def mod_grads(c_all, dmod_cols, w_mod, name):
    def body(c_ref, d_ref, w_ref, dw_ref, dc_ref):
        @pl.when(pl.program_id(0) == 0)
        def _():
            dc_ref[...] = jnp.zeros_like(dc_ref)

        cc = c_ref[...]
        dd = d_ref[...].astype(BF16)
        dw_ref[...] = _dotg((cc * jax.nn.sigmoid(cc)).astype(BF16), dd, TN)
        dc_ref[...] += _dotg(dd, w_ref[...].astype(BF16), NT)

    return pl.pallas_call(
        body, name=name, grid=(2,),
        in_specs=[pl.BlockSpec((16, D), lambda l: (0, 0)), pl.BlockSpec((None, 16, MOD_COLS), lambda l: (l, 0, 0)),
                  pl.BlockSpec((None, D, MOD_COLS), lambda l: (l, 0, 0))],
        out_specs=[pl.BlockSpec((None, D, MOD_COLS), lambda l: (l, 0, 0)), pl.BlockSpec((16, D), lambda l: (0, 0))],
        out_shape=[jax.ShapeDtypeStruct((2, D, MOD_COLS), F32), jax.ShapeDtypeStruct((16, D), F32)],
        compiler_params=_params("arbitrary"),
    )(c_all, dmod_cols, w_mod)


def _row_tile(rows, cols, n_arrays):
    budget = VMEM_LIMIT_BYTES // 4 // (2 * 4 * n_arrays * cols)
    best = None
    for tr in range(16, rows + 1, 16):
        if rows % tr == 0 and tr <= budget:
            best = tr
    return best if best is not None else rows


def elementwise(fn, ins, out_dtypes, name, ex=None):
    rows, cols = ins[0].shape
    tr = _row_tile(rows, cols, len(ins) + len(out_dtypes))

    def body(*refs):
        outs = fn(*[r[...] for r in refs[:len(ins)]])
        for o_ref, o in zip(refs[len(ins):], outs):
            o_ref[...] = o.astype(o_ref.dtype)

    spec = pl.BlockSpec((tr, cols), lambda i: (i, 0))
    outs, got = _grid_call(body, name, rows // tr, [spec] * len(ins), [spec] * len(out_dtypes),
                           [jax.ShapeDtypeStruct((rows, cols), dt) for dt in out_dtypes], ins, "parallel", ex)
    return outs if ex is None else (outs, got)


def _adamw_tile(w, g, m, v):
    m = ADAM_B1 * m + (1.0 - ADAM_B1) * g
    v = ADAM_B2 * v + (1.0 - ADAM_B2) * (g * g)
    m_hat = m / (1.0 - ADAM_B1 ** ADAM_STEP)
    v_hat = v / (1.0 - ADAM_B2 ** ADAM_STEP)
    return -ADAM_LR * (m_hat / (jnp.sqrt(v_hat) + ADAM_EPS) + ADAM_WD * w), m, v


def adamw(w, g, m, v, name, ex=None):
    shape = w.shape
    two_d = (-1, shape[-1]) if w.ndim > 1 else (1, -1)
    outs = elementwise(_adamw_tile, [a.reshape(two_d) for a in (w, g, m, v)], [F32] * 3, name, ex)
    outs, got = outs if ex is not None else (outs, None)
    outs = [o.reshape(shape) for o in outs]
    return outs if ex is None else (outs, got)


def _prefetch_call(body, name, grid, in_specs, out_specs, out_shape, place, args, aliases=None):
    spec = pltpu.PrefetchScalarGridSpec(num_scalar_prefetch=1, grid=grid, in_specs=in_specs, out_specs=out_specs)
    return pl.pallas_call(body, name=name, grid_spec=spec, out_shape=out_shape, input_output_aliases=aliases or {},
                          compiler_params=_params(*["parallel"] * len(grid)))(place, *args)


def cast_place(w, layer, place, name):
    _, r, c = w.shape
    tr = _row_tile(r, c, 2)

    def body(p_ref, w_ref, o_ref):
        o_ref[...] = w_ref[...].astype(BF16)

    return _prefetch_call(
        body, name, (r // tr,), [pl.BlockSpec((None, tr, c), lambda i, p: (layer, i, 0))],
        pl.BlockSpec((None, tr, c), lambda i, p: (p[1], i, 0)), jax.ShapeDtypeStruct((N_SLOT, r, c), BF16), place, [w])


def pair_sum(g32, got, place, name):
    n_slot, rh, c = got.shape
    tr = _row_tile(rh, c, 4)
    per = rh // tr

    def body(p_ref, a_ref, b_ref, o_ref, o16_ref):
        r = a_ref[...] + b_ref[...].astype(F32)
        o_ref[...] = r
        o16_ref[...] = r.astype(BF16)

    half = pl.BlockSpec((None, tr, c), lambda s, i, p: (s, i, 0))
    return _prefetch_call(
        body, name, (n_slot, per), [pl.BlockSpec((None, tr, c), lambda s, i, p: (s, p[0] * per + i, 0)), half], [half, half],
        [jax.ShapeDtypeStruct(got.shape, F32), jax.ShapeDtypeStruct(got.shape, BF16)], place, [g32, got])


def chip_sum(p32, got, place, name):
    _, rh, c = p32.shape
    tr = _row_tile(rh, c, 5)
    per = rh // tr

    def body(p_ref, m_ref, r0_ref, r1_ref, r2_ref, o_ref):
        o_ref[...] = m_ref[...] + r0_ref[...].astype(F32) + r1_ref[...].astype(F32) + r2_ref[...].astype(F32)

    part = pl.BlockSpec((tr, c), lambda i, p: (i, 0))
    return _prefetch_call(
        body, name, (per,), [pl.BlockSpec((None, tr, c), lambda i, p: (p[1], i, 0)), part, part, part],
        pl.BlockSpec((tr, c), lambda i, p: (p[0] * per + i, 0)), jax.ShapeDtypeStruct((2 * rh, c), F32), place, [p32, *got])


def adamw_layers(w, g0, g1, m, v, name, ex=None):
    _, r, c = w.shape
    tr = _row_tile(r, c, 10)

    def body(w_ref, g0_ref, g1_ref, m_ref, v_ref, g_ref, d_ref, mo_ref, vo_ref):
        g = jnp.where(pl.program_id(0) == 0, g0_ref[...], g1_ref[...])
        g_ref[...] = g
        d_ref[...], mo_ref[...], vo_ref[...] = _adamw_tile(w_ref[...], g, m_ref[...], v_ref[...])

    stacked = pl.BlockSpec((None, tr, c), lambda l, i: (l, i, 0))
    layer = pl.BlockSpec((tr, c), lambda l, i: (i, 0))
    outs, got = _grid_call(body, name, (2, r // tr), [stacked, layer, layer, stacked, stacked], [stacked] * 4,
                           [jax.ShapeDtypeStruct(w.shape, F32)] * 4, (w, g0, g1, m, v), "parallel", ex)
    return outs if ex is None else (outs, got)


def sum8(gathered, name):
    def body(*refs):
        n = len(refs) // 2
        for g_ref, o_ref in zip(refs[:n], refs[n:]):
            acc = g_ref[0]
            for dev in range(1, N_DEV):
                acc = acc + g_ref[dev]
            o_ref[...] = acc

    return pl.pallas_call(
        body, name=name,
        out_shape=[jax.ShapeDtypeStruct(a.shape[1:], F32) for a in gathered],
        compiler_params=_params(),
    )(*gathered)


def _place():
    return lax.axis_index("x"), lax.axis_index("y"), lax.axis_index("c")


def _any(n):
    return [pl.BlockSpec(memory_space=pl.ANY)] * n


def gather8_exchange(blocks):
    n = len(blocks)

    def copy(outs, sems, ti, k, block, to, src=None):
        dst = outs[ti].at[4 * block[0] + 2 * block[1] + block[2]]
        return pltpu.make_async_remote_copy(src_ref=dst if src is None else src, dst_ref=dst, send_sem=sems[0].at[ti, k],
                                            recv_sem=sems[1].at[ti, k], device_id=to, device_id_type=MESH)

    def first(ins, outs, sems):
        x, y, c = _place()
        local, sent = [], []
        for ti in range(n):
            local.append(pltpu.make_async_copy(ins[ti], outs[ti].at[4 * x + 2 * y + c], sems[2].at[ti]))
            sent.append(copy(outs, sems, ti, 0, (x, y, c), (x, y, 1 - c), src=ins[ti]))
            sent += [copy(outs, sems, ti, 1 + j, (x, y, c), (*chip, c), src=ins[ti]) for j, chip in enumerate(_three_chips(x, y))]
        return local, sent

    def start(ins, outs, sems):
        local, sent = first(ins, outs, sems)
        for cp in local + sent:
            cp.start()

    def finish(ins, outs, sems):
        x, y, c = _place()
        me, sibling, chips = (x, y, c), (x, y, 1 - c), _three_chips(x, y)
        local, sent = first(ins, outs, sems)
        for ti in range(n):
            for j, chip in enumerate(chips):
                copy(outs, sems, ti, 1 + j, (*chip, c), me).wait_recv()
                sent.append(copy(outs, sems, ti, 4 + j, (*chip, c), sibling))
                sent[-1].start()
        for ti in range(n):
            copy(outs, sems, ti, 0, sibling, me).wait_recv()
            for j, chip in enumerate(chips):
                copy(outs, sems, ti, 4 + j, (*chip, 1 - c), me).wait_recv()
        for cp in sent:
            cp.wait_send()
        for cp in local:
            cp.wait()

    return dict(ins=list(blocks), out_shape=[jax.ShapeDtypeStruct((N_DEV,) + b.shape, b.dtype) for b in blocks], aliases={},
                start=start, finish=finish,
                scratch=[pltpu.SemaphoreType.DMA((n, 7)), pltpu.SemaphoreType.DMA((n, 7)), pltpu.SemaphoreType.DMA((n,))])


def all_gather(blocks, name):
    return run_exchange(gather8_exchange(blocks), name)


def _three_chips(x, y):
    return [(1 - x, y), (x, 1 - y), (1 - x, 1 - y)]


def gather_exchange(placed):
    n = len(placed)

    def copy(bufs, sems, ti, k, chip, core, to):
        rh = bufs[ti].shape[1] // 2
        half = bufs[ti].at[2 * chip[0] + chip[1], pl.ds(core * rh, rh), :]
        return pltpu.make_async_remote_copy(src_ref=half, dst_ref=half, send_sem=sems[0].at[ti, k], recv_sem=sems[1].at[ti, k],
                                            device_id=to, device_id_type=MESH)

    def sends(bufs, sems):
        x, y, c = _place()
        return [copy(bufs, sems, ti, k, (x, y), c, (*chip, c)) for ti in range(n) for k, chip in enumerate(_three_chips(x, y))]

    def start(ins, bufs, sems):
        for cp in sends(bufs, sems):
            cp.start()

    def finish(ins, bufs, sems):
        x, y, c = _place()
        chips = _three_chips(x, y)
        passed = []
        for ti in range(n):
            for k, chip in enumerate(chips):
                copy(bufs, sems, ti, k, chip, c, (x, y, c)).wait_recv()
                passed.append(copy(bufs, sems, ti, 3 + k, chip, c, (x, y, 1 - c)))
                passed[-1].start()
        for ti in range(n):
            for k, chip in enumerate(chips):
                copy(bufs, sems, ti, 3 + k, chip, 1 - c, (x, y, c)).wait_recv()
        for cp in sends(bufs, sems) + passed:
            cp.wait_send()

    return dict(ins=list(placed), out_shape=[jax.ShapeDtypeStruct(w.shape, w.dtype) for w in placed],
                aliases={i: i for i in range(n)}, start=start, finish=finish,
                scratch=[pltpu.SemaphoreType.DMA((n, 6)), pltpu.SemaphoreType.DMA((n, 6))])


def scatter_exchange(p16):
    n = len(p16)

    def copies(ins, got, sems):
        x, y, c = _place()
        return [pltpu.make_async_remote_copy(src_ref=ins[ti].at[2 * chip[0] + chip[1]], dst_ref=got[3 * ti + k],
                                             send_sem=sems[0].at[ti, k], recv_sem=sems[1].at[ti, k], device_id=(*chip, c),
                                             device_id_type=MESH)
                for ti in range(n) for k, chip in enumerate(_three_chips(x, y))]

    def start(ins, got, sems):
        for cp in copies(ins, got, sems):
            cp.start()

    def finish(ins, got, sems):
        for cp in copies(ins, got, sems):
            cp.wait()

    return dict(ins=list(p16), out_shape=[jax.ShapeDtypeStruct(a.shape[1:], BF16) for a in p16 for _ in range(3)], aliases={},
                start=start, finish=finish, scratch=[pltpu.SemaphoreType.DMA((n, 3)), pltpu.SemaphoreType.DMA((n, 3))])


def run_exchange(ex, name):
    ci, co = len(ex["ins"]), len(ex["out_shape"])

    def body(*refs):
        ins, outs, sems = refs[:ci], refs[ci:ci + co], refs[ci + co:]
        ex["start"](ins, outs, sems)
        ex["finish"](ins, outs, sems)

    return pl.pallas_call(body, name=name, in_specs=_any(ci), out_specs=_any(co), out_shape=ex["out_shape"],
                          input_output_aliases=ex["aliases"], scratch_shapes=ex["scratch"])(*ex["ins"])


def _grid_call(body, name, grid, in_specs, out_specs, out_shape, args, sem, ex=None):
    grid = (grid,) if isinstance(grid, int) else tuple(grid)
    sems_of = (sem,) * len(grid) if isinstance(sem, str) else tuple(sem)
    n_in, n_out = len(in_specs), len(out_specs)
    if ex is None:
        return pl.pallas_call(body, name=name, grid=grid, in_specs=in_specs, out_specs=out_specs, out_shape=out_shape,
                              compiler_params=_params(*sems_of))(*args), []
    ci, co = len(ex["ins"]), len(ex["out_shape"])

    def at(step):
        where = [pl.program_id(ax) == (0 if step == "first" else grid[ax] - 1) for ax in range(len(grid))]
        return functools.reduce(jnp.logical_and, where)

    def carrying(*refs):
        c_in, c_out = refs[n_in:n_in + ci], refs[n_in + ci + n_out:n_in + ci + n_out + co]
        sems = refs[n_in + ci + n_out + co:]

        @pl.when(at("first"))
        def _():
            ex["start"](c_in, c_out, sems)

        body(*refs[:n_in], *refs[n_in + ci:n_in + ci + n_out])

        @pl.when(at("last"))
        def _():
            ex["finish"](c_in, c_out, sems)

    outs = pl.pallas_call(
        carrying, name=name, grid=grid, in_specs=list(in_specs) + _any(ci), out_specs=list(out_specs) + _any(co),
        out_shape=list(out_shape) + ex["out_shape"], scratch_shapes=ex["scratch"],
        input_output_aliases={n_in + i: n_out + j for i, j in ex["aliases"].items()},
        compiler_params=_params(*["arbitrary"] * len(grid)),
    )(*args, *ex["ins"])
    return outs[:n_out], outs[n_out:]


def both(*exchanges):
    exchanges = [ex for ex in exchanges if ex is not None]
    if len(exchanges) < 2:
        return exchanges[0] if exchanges else None
    n_ins = [len(ex["ins"]) for ex in exchanges]
    n_outs = [len(ex["out_shape"]) for ex in exchanges]
    n_sems = [len(ex["scratch"]) for ex in exchanges]

    def parts(seq, counts, k):
        first = sum(counts[:k])
        return seq[first:first + counts[k]]

    def run(phase):
        def go(ins, outs, sems):
            for k, ex in enumerate(exchanges):
                ex[phase](parts(ins, n_ins, k), parts(outs, n_outs, k), parts(sems, n_sems, k))
        return go

    aliases = {sum(n_ins[:k]) + i: sum(n_outs[:k]) + j for k, ex in enumerate(exchanges) for i, j in ex["aliases"].items()}
    return dict(ins=[a for ex in exchanges for a in ex["ins"]], out_shape=[o for ex in exchanges for o in ex["out_shape"]],
                aliases=aliases, start=run("start"), finish=run("finish"), scratch=[s for ex in exchanges for s in ex["scratch"]])


def split_outputs(got, *exchanges):
    got, out = list(got), []
    for ex in exchanges:
        n = len(ex["out_shape"]) if ex is not None else 0
        out.append(got[:n])
        got = got[n:]
    return out


def pair_exchange(g16):
    n = len(g16)

    def copies(a16, got, sems):
        x, y, c = _place()
        out = []
        for ti in range(n):
            rh = a16[ti].shape[1] // 2
            out.append(pltpu.make_async_remote_copy(
                src_ref=a16[ti].at[:, pl.ds((1 - c) * rh, rh), :], dst_ref=got[ti], send_sem=sems[0].at[ti],
                recv_sem=sems[1].at[ti], device_id=(x, y, 1 - c), device_id_type=MESH))
        return out

    def start(a16, got, sems):
        for cp in copies(a16, got, sems):
            cp.start()

    def finish(a16, got, sems):
        for cp in copies(a16, got, sems):
            cp.wait()

    return dict(ins=list(g16), out_shape=[jax.ShapeDtypeStruct((a.shape[0], a.shape[1] // 2, a.shape[2]), BF16) for a in g16],
                aliases={}, start=start, finish=finish, scratch=[pltpu.SemaphoreType.DMA((n,)), pltpu.SemaphoreType.DMA((n,))])


def pair_gather(halves, name):
    n = len(halves)

    def body(*refs):
        bufs = refs[n:2 * n]
        send_sems, recv_sems = refs[2 * n:]
        x, y, c = _place()
        copies = []
        for ti in range(n):
            rh = bufs[ti].shape[0] // 2
            rows = bufs[ti].at[pl.ds(c * rh, rh), :]
            copies.append(pltpu.make_async_remote_copy(src_ref=rows, dst_ref=rows, send_sem=send_sems.at[ti],
                                                       recv_sem=recv_sems.at[ti], device_id=(x, y, 1 - c), device_id_type=MESH))
        for cp in copies:
            cp.start()
        for ti, cp in enumerate(copies):
            cp.wait_send()
            rh = bufs[ti].shape[0] // 2
            theirs = bufs[ti].at[pl.ds((1 - c) * rh, rh), :]
            pltpu.make_async_remote_copy(src_ref=theirs, dst_ref=theirs, send_sem=send_sems.at[ti], recv_sem=recv_sems.at[ti],
                                         device_id=(x, y, 1 - c), device_id_type=MESH).wait_recv()

    return pl.pallas_call(
        body, name=name, in_specs=_any(n), out_specs=_any(n), input_output_aliases={i: i for i in range(n)},
        out_shape=[jax.ShapeDtypeStruct(a.shape, a.dtype) for a in halves],
        scratch_shapes=[pltpu.SemaphoreType.DMA((n,)), pltpu.SemaphoreType.DMA((n,))],
    )(*halves)


def reduce_small(dm_f1, dm_mix, dm_gate, dm_f2, loss_blk, name):
    def body(f1_ref, mix_ref, gate_ref, f2_ref, l_ref, tot_ref, rows_ref, fin_ref):
        rows_ref[...] = jnp.zeros_like(rows_ref)
        tot_ref[...] = jnp.zeros_like(tot_ref)
        mod_src = [(f1_ref, 0), (f1_ref, 1), (f1_ref, 2), (mix_ref, 0), (mix_ref, 1), (gate_ref, 2),
                   (f2_ref, 0), (f2_ref, 1), (f2_ref, 2)]
        norm_src = [(f1_ref, 3), (mix_ref, 3), (f2_ref, 3)]
        for l in range(2):
            for k, (ref, r) in enumerate(mod_src + norm_src):
                lat = ref[0, l, 0, r:r + 1, :]
                ctx = ref[0, l, 1, r:r + 1, :]
                for dev in range(N_DEV):
                    if dev:
                        lat = lat + ref[dev, l, 0, r:r + 1, :]
                        ctx = ctx + ref[dev, l, 1, r:r + 1, :]
                    if k < N_MOD:
                        rows_ref[l, dev, k:k + 1, :] = ref[dev, l, 0, r:r + 1, :]
                if k < N_MOD:
                    rows_ref[l, N_DEV, k:k + 1, :] = ctx
                tot_ref[l, k:k + 1, :] = lat + ctx
        acc = l_ref[0]
        for dev in range(1, N_DEV):
            acc = acc + l_ref[dev]
        loss = (0.5 / D) * jnp.sum(acc[1:2, :], axis=1, keepdims=True)
        row = lax.broadcasted_iota(jnp.int32, (8, D), 0)
        fin_ref[...] = jnp.where(row == 0, acc[0:1, :], loss)

    return pl.pallas_call(
        body, name=name,
        out_shape=[jax.ShapeDtypeStruct((2, 16, D), F32), jax.ShapeDtypeStruct((2, 16, 16, D), F32),
                   jax.ShapeDtypeStruct((8, D), F32)],
        compiler_params=_params(),
    )(dm_f1, dm_mix, dm_gate, dm_f2, loss_blk)


def rope_tables(t, s):
    rows = t // GRID_W
    row = jnp.repeat(jnp.arange(rows), GRID_W).astype(F32)
    col = jnp.tile(jnp.arange(GRID_W), rows).astype(F32)
    inv = ROPE_BASE ** (-jnp.arange(0, HEAD // 2, 2, dtype=F32) / (HEAD // 2))
    ang = jnp.concatenate([row[:, None] * inv, col[:, None] * inv], axis=-1)
    cos, sin = jnp.cos(ang), jnp.sin(ang)
    cos = jnp.concatenate([jnp.tile(cos, (1, 4)), jnp.ones((s - t, BLK), F32)], axis=0)
    sin = jnp.concatenate([jnp.tile(jnp.concatenate([-sin, sin], axis=1), (1, 2)), jnp.zeros((s - t, BLK), F32)], axis=0)
    return cos, sin


BIG = ("ffn1_in", "ffn1_out", "w_in", "w_out", "ffn2_in", "ffn2_out")
GROUPS = dict(ffn1=("ffn1_in", "ffn1_out"), mix=("w_in", "w_out"), ffn2=("ffn2_in", "ffn2_out"))
GATHER_BEHIND = {("ffn1", 0): [("mix", 0)], ("mix", 0): [("ffn2", 0)], ("ffn2", 0): [("ffn1", 1)], ("ffn1", 1): [("mix", 1)],
                 ("mix", 1): [("ffn2", 1)]}


def _slot_major(name, g):
    if name == "w_in":
        return jnp.stack(jnp.split(g, N_SLOT, axis=1), axis=0)
    if name in ("ffn1_in", "ffn2_in"):
        return g
    return g.reshape(N_SLOT, g.shape[0] // N_SLOT, g.shape[1])


def _whole_weight(name, buf):
    if name == "w_in":
        return buf.transpose(1, 0, 2).reshape(D, PROJ_W)
    if name in ("ffn1_in", "ffn2_in"):
        return buf
    return buf.reshape(-1, buf.shape[2])


def local_step(x1, ctx1, target, mods, norms, nfinal, placed, w_pool, pool_scale, sink, place, small_blocks):
    t, s = x1.shape[0], x1.shape[0] + ctx1.shape[0]
    n_lat = t // TM
    cos, sin = rope_tables(t, s)
    wts = {name: list(pair) for name, pair in placed.items()}

    def gather(groups):
        return gather_exchange([wts[name][l] for grp, l in groups for name in GROUPS[grp]])

    def gathered(groups, arrays):
        arrays = list(arrays)
        for grp, l in groups:
            for name in GROUPS[grp]:
                wts[name][l] = arrays.pop(0)

    def weight(name, l):
        return _whole_weight(name, wts[name][l])

    def fwd_ex(grp, l):
        groups = GATHER_BEHIND.get((grp, l))
        return (groups, gather(groups)) if groups else (None, None)

    gathered([("ffn1", 0)], run_exchange(gather([("ffn1", 0)]), "gather_first"))
    h = jnp.concatenate([x1, ctx1], axis=0)
    saved = []
    for l in range(2):
        h0 = h
        groups, ex = fwd_ex("ffn1", l)
        (h1, ab1, f1), got = ffn_fwd(h0, mods, norms[0], weight("ffn1_in", l), weight("ffn1_out", l), l, 0, n_lat, f"ffn1_fwd_{l}", ex)
        gathered(groups or [], got)
        u, q, k, v = proj_fwd(h1, mods, norms[1], weight("w_in", l), cos, sin, l, n_lat, f"proj_fwd_{l}")
        groups, ex = fwd_ex("mix", l)
        (h2, cat, lse), got = mix_fwd(h1, q, k, v, u, w_pool, pool_scale, sink, weight("w_out", l), mods, l, t, f"mix_fwd_{l}", ex)
        gathered(groups or [], got)
        groups, ex = fwd_ex("ffn2", l)
        (h, ab2, f2), got = ffn_fwd(h2, mods, norms[2], weight("ffn2_in", l), weight("ffn2_out", l), l, 6, n_lat, f"ffn2_fwd_{l}", ex)
        gathered(groups or [], got)
        saved.append((h0, ab1, f1, h1, u, q, k, v, cat, lse, h2, ab2, f2))
    dh, loss_blk = loss_head(h, target, nfinal, t, "loss_head")

    halves = {name: [None, None] for name in BIG}
    pending = []

    def summed_in_pair(grp, l, name_a, g_a, name_b, wgrad_b):
        g_b, got_a = wgrad_b(pair_exchange([_slot_major(name_a, g_a[1])]))
        got_b = run_exchange(pair_exchange([_slot_major(name_b, g_b[1])]), f"pair_exchange_{name_b}_{l}")
        by = {name_a: (g_a[0], got_a[0]), name_b: (g_b[0], got_b[0])}
        pending.append((grp, l, [pair_sum(_slot_major(n, by[n][0]), by[n][1], place, f"pair_sum_{n}_{l}") for n in GROUPS[grp]]))

    def scatter():
        return scatter_exchange([p16 for _, p16 in pending[0][2]]) if pending else None

    def scattered(got):
        if pending:
            grp, l, pairs = pending.pop(0)
            for i, name in enumerate(GROUPS[grp]):
                halves[name][l] = chip_sum(pairs[i][0], got[3 * i:3 * i + 3], place, f"chip_sum_{name}_{l}")

    small = [None, None]
    for l in (1, 0):
        h0, ab1, f1, h1, u, q, k, v, cat, lse, h2, ab2, f2 = saved[l]
        (dh, dab, df, n, act, dm_f2), got = ffn_bwd(h2, ab2, f2, dh, mods, norms[2], weight("ffn2_in", l), weight("ffn2_out", l),
                                                    l, 6, n_lat, f"ffn2_bwd_{l}", scatter())
        scattered(got)
        g_in, _ = wgrad(n, dab, D, FF_COLS, FF_COLS, f"ffn2_in_wgrad_{l}")
        summed_in_pair("ffn2", l, "ffn2_in", g_in, "ffn2_out",
                       lambda ex, a=act, b=df: wgrad(a, b, D_FF // 2, D, None, f"ffn2_out_wgrad_{l}", ex))
        (dq, dk, dv, du, dmo, dwp, dps, dsink, dm_gate), got = mix_bwd(
            dh, cat, q, k, v, u, lse, w_pool, pool_scale, sink, weight("w_out", l), mods, l, t, f"mix_bwd_{l}", scatter())
        scattered(got)
        g_wo, _ = wgrad(cat, dmo, POOL_W + ATTN_W, D, None, f"w_out_wgrad_{l}")
        dh, dp, n, dm_mix = proj_bwd(h1, du, dq, dk, dv, dh, mods, norms[1], weight("w_in", l), cos, sin, l, n_lat, f"proj_bwd_{l}")
        summed_in_pair("mix", l, "w_out", g_wo, "w_in",
                       lambda ex, a=n, b=dp: wgrad(a, b, D, PROJ_W // 2, None, f"w_in_wgrad_{l}", ex))
        (dh, dab, df, n, act, dm_f1), got = ffn_bwd(h0, ab1, f1, dh, mods, norms[0], weight("ffn1_in", l), weight("ffn1_out", l),
                                                    l, 0, n_lat, f"ffn1_bwd_{l}", scatter())
        scattered(got)
        small[l] = dict(dm_f1=dm_f1, dm_mix=dm_mix, dm_gate=dm_gate, dm_f2=dm_f2, dwp=dwp, dps=dps, dsink=dsink)
        g_in, small_all = wgrad(n, dab, D, FF_COLS, FF_COLS, f"ffn1_in_wgrad_{l}", None if l else gather8_exchange(small_blocks(small, loss_blk)))
        summed_in_pair("ffn1", l, "ffn1_in", g_in, "ffn1_out",
                       lambda ex, a=act, b=df: wgrad(a, b, D_FF // 2, D, None, f"ffn1_out_wgrad_{l}", ex))
    return dh[:t], halves, pending.pop(0)[2], small_all


def _silu_grad(z):
    sg = jax.nn.sigmoid(z)
    return sg * (1 + z * (1 - sg))


def kernel(x, c, ctx, c_ctx, w_mod, b_mod, norm_ffn1, w_ffn1_in, w_ffn1_out, norm_mix, w_in, w_pool, pool_scale, sink, w_out, norm_ffn2, w_ffn2_in, w_ffn2_out, norm_final, loss_target, m_c_ctx, m_w_mod, m_b_mod, m_norm_ffn1, m_w_ffn1_in, m_w_ffn1_out, m_norm_mix, m_w_in, m_w_pool, m_pool_scale, m_sink, m_w_out, m_norm_ffn2, m_w_ffn2_in, m_w_ffn2_out, m_norm_final, v_c_ctx, v_w_mod, v_b_mod, v_norm_ffn1, v_w_ffn1_in, v_w_ffn1_out, v_norm_mix, v_w_in, v_w_pool, v_pool_scale, v_sink, v_w_out, v_norm_ffn2, v_w_ffn2_in, v_w_ffn2_out, v_norm_final):
    px, py, pc = _place()
    slot, me = 2 * px + py, 4 * px + 2 * py + pc
    n_grp = len(POOL_WINDOWS)

    (c_rows,) = all_gather([c.reshape(8, D // 8)], "gather_c")
    c_all = jnp.concatenate([c_rows.reshape(N_DEV, D), c_ctx.reshape(1, D), jnp.zeros((16 - N_DEV - 1, D), F32)], axis=0)
    b_cols = lax.dynamic_slice(b_mod, (0, slot * MOD_COLS), (2, MOD_COLS)).reshape(2, 1, MOD_COLS)
    (mod_parts,) = all_gather([mod_rows(c_all, w_mod, b_cols, "mod_rows")], "gather_mods")
    mods_all = mod_parts[0::2].transpose(1, 2, 0, 3).reshape(2, 16, N_MOD * D)
    mx = lax.dynamic_slice(mods_all, (0, me, 0), (2, 1, N_MOD * D)).reshape(2, N_MOD, D)
    mc = mods_all[:, N_DEV].reshape(2, N_MOD, D)
    pad = jnp.zeros((2, 16 - N_MOD, D), F32)
    mods = jnp.stack([jnp.concatenate([mx, pad], axis=1), jnp.concatenate([mc, pad], axis=1)], axis=1)

    place = jnp.stack([pc, slot]).astype(jnp.int32)
    shards = dict(ffn1_in=w_ffn1_in, ffn1_out=w_ffn1_out, w_in=w_in, w_out=w_out, ffn2_in=w_ffn2_in, ffn2_out=w_ffn2_out)
    placed = {name: [cast_place(shards[name], l, place, f"cast_{name}_{l}") for l in range(2)] for name in BIG}
    norms = [g.reshape(2, 1, D) for g in (norm_ffn1, norm_mix, norm_ffn2)]
    row_sums = ("dm_f1", "dm_mix", "dm_gate", "dm_f2")

    def small_blocks(small, loss_blk):
        stacked = {k: jnp.stack([small[0][k], small[1][k]]) for k in row_sums + ("dwp", "dps", "dsink")}
        return ([stacked[k].reshape(32, D) for k in row_sums]
                + [stacked["dwp"].reshape(2 * n_grp * GROUP, GROUP), stacked["dps"].reshape(16, POOL_W),
                   stacked["dsink"].reshape(16, BLK), loss_blk])

    dx, halves, last_pairs, small_all = local_step(x[0], ctx[0], loss_target[0], mods, norms, norm_final.reshape(1, D), placed,
                                                   w_pool.astype(BF16), pool_scale.reshape(2, 1, POOL_W), sink, place, small_blocks)
    grads = {}

    *g_dm, g_dwp, g_dps, g_dsink, g_loss = small_all
    tot, rows, fin = reduce_small(*[g.reshape(N_DEV, 2, 2, 8, D) for g in g_dm], g_loss, "reduce_small")
    s_dwp, s_dps, s_dsink = sum8([g_dwp, g_dps, g_dsink], "sum_pool_sink")
    grads.update(
        w_pool=s_dwp.reshape(2, n_grp, GROUP, GROUP), pool_scale=s_dps.reshape(2, 8, POOL_W)[:, 0],
        sink=s_dsink.reshape(2, 8, BLK)[:, 0, :N_HEADS], b_mod=tot[:, :N_MOD].reshape(2, N_MOD * D),
        norm_ffn1=tot[:, N_MOD], norm_mix=tot[:, N_MOD + 1], norm_ffn2=tot[:, N_MOD + 2], norm_final=fin[0])
    loss = fin[1, 0]

    dmod_cols = lax.dynamic_slice(rows[:, :, :N_MOD, :].reshape(2, 16, N_MOD * D), (0, 0, slot * MOD_COLS), (2, 16, MOD_COLS))
    grads["w_mod"], dc = mod_grads(c_all, dmod_cols, w_mod, "mod_grads")
    (g_dc,) = all_gather([dc], "gather_dc")
    (s_dc,) = sum8([g_dc], "sum_dc")
    (d_c_ctx,) = elementwise(lambda d, z: (0.5 * d * _silu_grad(z),), [s_dc[N_DEV:N_DEV + 1], c_ctx.reshape(1, D)], [F32], "c_ctx_grad")
    grads["c_ctx"] = d_c_ctx.reshape(D)

    given = dict(c_ctx=(c_ctx, m_c_ctx, v_c_ctx), w_mod=(w_mod, m_w_mod, v_w_mod), b_mod=(b_mod, m_b_mod, v_b_mod),
                 norm_ffn1=(norm_ffn1, m_norm_ffn1, v_norm_ffn1), w_ffn1_in=(w_ffn1_in, m_w_ffn1_in, v_w_ffn1_in),
                 w_ffn1_out=(w_ffn1_out, m_w_ffn1_out, v_w_ffn1_out), norm_mix=(norm_mix, m_norm_mix, v_norm_mix),
                 w_in=(w_in, m_w_in, v_w_in), w_pool=(w_pool, m_w_pool, v_w_pool),
                 pool_scale=(pool_scale, m_pool_scale, v_pool_scale), sink=(sink, m_sink, v_sink), w_out=(w_out, m_w_out, v_w_out),
                 norm_ffn2=(norm_ffn2, m_norm_ffn2, v_norm_ffn2), w_ffn2_in=(w_ffn2_in, m_w_ffn2_in, v_w_ffn2_in),
                 w_ffn2_out=(w_ffn2_out, m_w_ffn2_out, v_w_ffn2_out), norm_final=(norm_final, m_norm_final, v_norm_final))
    done = {}
    scatter_in, scatter_out = scatter_exchange([last_pairs[0][1]]), scatter_exchange([last_pairs[1][1]])
    w, m, v = given["w_mod"]
    done["w_mod"], got_in = adamw(w, grads["w_mod"], m, v, "adamw_w_mod", scatter_in)
    ready = [(name, l) for name in BIG for l in range(2) if halves[name][l] is not None]
    shard = dict(zip(ready, pair_gather([halves[name][l] for name, l in ready], "grad_pair_gather")))
    w, m, v = given["w_ffn2_in"]
    done["w_ffn2_in"], got_out = adamw_layers(w, shard["ffn2_in", 0], shard["ffn2_in", 1], m, v, "adamw_w_ffn2_in", scatter_out)
    last_halves = [chip_sum(last_pairs[0][0], got_in, place, "chip_sum_ffn1_in_0"),
                   chip_sum(last_pairs[1][0], got_out, place, "chip_sum_ffn1_out_0")]
    shard["ffn1_in", 0], shard["ffn1_out", 0] = pair_gather(last_halves, "grad_pair_gather_last")

    g_out, d_out, m_out, v_out = [], [], [], []
    for name, (w, m, v) in given.items():
        if name in done:
            outs = done[name]
            grad, delta, new_m, new_v = outs if len(outs) == 4 else (grads[name], *outs)
        elif name in BIG or name[2:] in BIG:
            key = name if name in BIG else name[2:]
            grad, delta, new_m, new_v = adamw_layers(w, shard[key, 0], shard[key, 1], m, v, f"adamw_{name}")
        else:
            grad = grads[name]
            delta, new_m, new_v = adamw(w, grad, m, v, f"adamw_{name}")
        g_out.append(grad)
        d_out.append(delta)
        m_out.append(new_m)
        v_out.append(new_v)
    return (loss, dx[None], *g_out, *d_out, *m_out, *v_out)
```

```python
import functools

import jax
import jax.numpy as jnp
from jax import lax
from jax.experimental import pallas as pl
from jax.experimental.pallas import tpu as pltpu

F32, BF16 = jnp.float32, jnp.bfloat16
D = 1024
D_FF = 2816
N_SLOT = 4
FF_COLS = 2 * D_FF // N_SLOT
N_MOD = 9
MOD_COLS = N_MOD * D // N_SLOT
POOL_W, ATTN_W, KV_W = 512, 512, 128
PROJ_W = POOL_W + ATTN_W + 2 * KV_W
N_HEADS, Q_GROUP, HEAD = 8, 4, 64
GROUP = 128
POOL_WINDOWS = (2, 4, 8, 16)
BLK = 128
QB = 256
WIN = QB + 2 * BLK
GRID_W = 64
ROPE_BASE = 10000.0
EPS = 1e-6
NEG_INF = -1e30
TM = 256
N_DEV = 8
VMEM_LIMIT_BYTES = 56 * 1024 * 1024
ADAM_LR, ADAM_B1, ADAM_B2, ADAM_EPS, ADAM_WD, ADAM_STEP = 0.001, 0.9, 0.999, 1e-08, 0.01, 10
MESH = pl.DeviceIdType.MESH
NT = (((1,), (1,)), ((), ()))
TN = (((0,), (0,)), ((), ()))


def _params(*sem):
    return pltpu.CompilerParams(dimension_semantics=sem, vmem_limit_bytes=VMEM_LIMIT_BYTES)


def _whole(shape, lead=()):
    idx = tuple(lead) + (0,) * len(shape)
    return pl.BlockSpec((None,) * len(lead) + tuple(shape), lambda *_: idx, pipeline_mode=pl.Buffered(1))


def _rows(cols, tm=TM):
    return pl.BlockSpec((tm, cols), lambda i: (i, 0))


def _mods_spec(layer, n_lat):
    return pl.BlockSpec((None, None, 16, D), lambda i: (layer, (i >= n_lat).astype(jnp.int32), 0, 0))


def _acc_spec(n_lat):
    return pl.BlockSpec((None, 8, D), lambda i: ((i >= n_lat).astype(jnp.int32), 0, 0))


def _dot(a, b):
    return jnp.dot(a, b, preferred_element_type=F32)


def _dotg(a, b, dims):
    return lax.dot_general(a, b, dims, preferred_element_type=F32)


def _sum0(v):
    return jnp.sum(v, axis=0, keepdims=True)


def _norm_mod(h, g, shift, scale):
    r = lax.rsqrt(jnp.mean(h * h, axis=-1, keepdims=True) + EPS)
    xhat = h * r
    y = xhat * g
    return y * (1 + scale) + shift, xhat, r, y


def _norm_mod_bwd(dn, xhat, r, y, g, scale):
    dy = dn * (1 + scale)
    dx = dy * g
    dh = r * (dx - xhat * jnp.mean(dx * xhat, axis=-1, keepdims=True))
    return _sum0(dn), _sum0(dn * y), _sum0(dy * xhat), dh


def _swap_halves(v):
    w = v.shape[1]
    lane = lax.broadcasted_iota(jnp.int32, v.shape, 1)
    return jnp.where(lane % HEAD < HEAD // 2, pltpu.roll(v, w - HEAD // 2, axis=1), pltpu.roll(v, HEAD // 2, axis=1))


def _tile_lanes(t, width):
    return t if width == t.shape[1] else jnp.concatenate([t] * (width // t.shape[1]), axis=1)


def _rope(v, cos, sin):
    return v * _tile_lanes(cos, v.shape[1]) + _swap_halves(v) * _tile_lanes(sin, v.shape[1])


def _unrope(g, cos, sin):
    return g * _tile_lanes(cos, g.shape[1]) + _swap_halves(g * _tile_lanes(sin, g.shape[1]))


def ffn_fwd(h, mods, g, w4, wo, layer, k0, n_lat, name, ex=None):
    s = h.shape[0]

    def body(h_ref, m_ref, g_ref, w_ref, wo_ref, ho_ref, ab_ref, f_ref):
        hh = h_ref[...]
        n, _, _, _ = _norm_mod(hh, g_ref[...], m_ref[k0:k0 + 1, :], m_ref[k0 + 1:k0 + 2, :])
        nb = n.astype(BF16)
        acc = jnp.zeros((TM, D), F32)
        for j in range(2):
            a = _dot(nb, w_ref[j])
            b = _dot(nb, w_ref[2 + j])
            ab_ref[:, j * FF_COLS:(j + 1) * FF_COLS] = a.astype(BF16)
            ab_ref[:, (2 + j) * FF_COLS:(3 + j) * FF_COLS] = b.astype(BF16)
            act = (a * jax.nn.sigmoid(a) * b).astype(BF16)
            acc = acc + _dot(act, wo_ref[j * FF_COLS:(j + 1) * FF_COLS, :])
        f_ref[...] = acc
        ho_ref[...] = hh + 0.5 * m_ref[k0 + 2:k0 + 3, :] * acc

    return _grid_call(
        body, name, s // TM,
        [_rows(D), _mods_spec(layer, n_lat), _whole((1, D), (layer,)), _whole((N_SLOT, D, FF_COLS)), _whole((D_FF, D))],
        [_rows(D), _rows(2 * D_FF), _rows(D)],
        [jax.ShapeDtypeStruct((s, D), F32), jax.ShapeDtypeStruct((s, 2 * D_FF), BF16), jax.ShapeDtypeStruct((s, D), F32)],
        (h, mods, g, w4, wo), "parallel", ex)


def ffn_bwd(h, ab, f, dh, mods, g, w4, wo, layer, k0, n_lat, name, ex=None):
    s = h.shape[0]

    def body(h_ref, ab_ref, f_ref, dh_ref, m_ref, g_ref, w_ref, wo_ref, dhi_ref, dab_ref, df_ref, n_ref, act_ref, dm_ref):
        i = pl.program_id(0)

        @pl.when((i == 0) | (i == n_lat))
        def _():
            dm_ref[...] = jnp.zeros_like(dm_ref)

        hh, dho, gg = h_ref[...], dh_ref[...], g_ref[...]
        scale, gate = m_ref[k0 + 1:k0 + 2, :], m_ref[k0 + 2:k0 + 3, :]
        n, xhat, r, y = _norm_mod(hh, gg, m_ref[k0:k0 + 1, :], scale)
        n_ref[...] = n.astype(BF16)
        dgate = _sum0(dho * (0.5 * f_ref[...]))
        dfb = ((0.5 * gate) * dho).astype(BF16)
        df_ref[...] = dfb
        dn = jnp.zeros((TM, D), F32)
        for j in range(2):
            a = ab_ref[:, j * FF_COLS:(j + 1) * FF_COLS].astype(F32)
            b = ab_ref[:, (2 + j) * FF_COLS:(3 + j) * FF_COLS].astype(F32)
            sg = jax.nn.sigmoid(a)
            sa = a * sg
            act_ref[:, j * FF_COLS:(j + 1) * FF_COLS] = (sa * b).astype(BF16)
            dact = _dotg(dfb, wo_ref[j * FF_COLS:(j + 1) * FF_COLS, :], NT)
            da = (dact * b * (sg * (1 + a * (1 - sg)))).astype(BF16)
            db = (dact * sa).astype(BF16)
            dab_ref[:, j * FF_COLS:(j + 1) * FF_COLS] = da
            dab_ref[:, (2 + j) * FF_COLS:(3 + j) * FF_COLS] = db
            dn = dn + _dotg(da, w_ref[j], NT) + _dotg(db, w_ref[2 + j], NT)
        dsh, dsc, dg, dhn = _norm_mod_bwd(dn, xhat, r, y, gg, scale)
        dhi_ref[...] = dho + dhn
        dm_ref[0:1, :] += dsh
        dm_ref[1:2, :] += dsc
        dm_ref[2:3, :] += dgate
        dm_ref[3:4, :] += dg

    return _grid_call(
        body, name, s // TM,
        [_rows(D), _rows(2 * D_FF), _rows(D), _rows(D), _mods_spec(layer, n_lat), _whole((1, D), (layer,)),
         _whole((N_SLOT, D, FF_COLS)), _whole((D_FF, D))],
        [_rows(D), _rows(2 * D_FF), _rows(D), _rows(D), _rows(D_FF), _acc_spec(n_lat)],
        [jax.ShapeDtypeStruct((s, D), F32), jax.ShapeDtypeStruct((s, 2 * D_FF), BF16), jax.ShapeDtypeStruct((s, D), BF16),
         jax.ShapeDtypeStruct((s, D), BF16), jax.ShapeDtypeStruct((s, D_FF), BF16), jax.ShapeDtypeStruct((2, 8, D), F32)],
        (h, ab, f, dh, mods, g, w4, wo), "arbitrary", ex)


def _token_tile(s, limit=2176):
    return max(ts for ts in range(16, limit + 1, 16) if s % ts == 0)


def wgrad(a, b, tk, tn, slot_cols, name, ex=None):
    s, k = a.shape
    n = b.shape[1]
    ts = _token_tile(s)
    steps = s // ts

    def body(a_ref, b_ref, o_ref, o16_ref):
        r = _dotg(a_ref[...], b_ref[...], TN)
        si = pl.program_id(2)

        @pl.when(si == 0)
        def _():
            o_ref[...] = r

        @pl.when(si > 0)
        def _():
            o_ref[...] += r

        @pl.when(si == steps - 1)
        def _():
            o16_ref[...] = o_ref[...].astype(BF16)

    if slot_cols is None:
        shape, spec = (k, n), pl.BlockSpec((tk, tn), lambda i, j, si: (i, j))
    else:
        per = slot_cols // tn
        shape, spec = (n // slot_cols, k, slot_cols), pl.BlockSpec((None, tk, tn), lambda i, j, si: (lax.div(j, per), i, lax.rem(j, per)))
    return _grid_call(
        body, name, (k // tk, n // tn, steps),
        [pl.BlockSpec((ts, tk), lambda i, j, si: (si, i)), pl.BlockSpec((ts, tn), lambda i, j, si: (si, j))], [spec, spec],
        [jax.ShapeDtypeStruct(shape, F32), jax.ShapeDtypeStruct(shape, BF16)], (a, b), ("parallel", "parallel", "arbitrary"), ex)


def proj_fwd(h, mods, g, w_in, cos, sin, layer, n_lat, name):
    s = h.shape[0]

    def body(h_ref, m_ref, g_ref, w_ref, cos_ref, sin_ref, u_ref, q_ref, k_ref, v_ref):
        n, _, _, _ = _norm_mod(h_ref[...], g_ref[...], m_ref[3:4, :], m_ref[4:5, :])
        p = _dot(n.astype(BF16), w_ref[...])
        cs, sn = cos_ref[...], sin_ref[...]
        u_ref[...] = p[:, :POOL_W]
        q_ref[...] = (_rope(p[:, POOL_W:POOL_W + ATTN_W], cs, sn) * HEAD ** -0.5).astype(BF16)
        k_ref[...] = _rope(p[:, POOL_W + ATTN_W:POOL_W + ATTN_W + KV_W], cs, sn).astype(BF16)
        v_ref[...] = p[:, POOL_W + ATTN_W + KV_W:].astype(BF16)

    return pl.pallas_call(
        body, name=name, grid=(s // TM,),
        in_specs=[_rows(D), _mods_spec(layer, n_lat), _whole((1, D), (layer,)), _whole((D, PROJ_W)),
                  _rows(BLK), _rows(BLK)],
        out_specs=[_rows(POOL_W), _rows(ATTN_W), _rows(KV_W), _rows(KV_W)],
        out_shape=[jax.ShapeDtypeStruct((s, POOL_W), F32), jax.ShapeDtypeStruct((s, ATTN_W), BF16),
                   jax.ShapeDtypeStruct((s, KV_W), BF16), jax.ShapeDtypeStruct((s, KV_W), BF16)],
        compiler_params=_params("parallel"),
    )(h, mods, g, w_in, cos, sin)


def proj_bwd(h, du, dq, dk, dv, dh, mods, g, w_in, cos, sin, layer, n_lat, name):
    s = h.shape[0]

    def body(h_ref, du_ref, dq_ref, dk_ref, dv_ref, dh_ref, m_ref, g_ref, w_ref, cos_ref, sin_ref,
             dhi_ref, dp_ref, n_ref, dm_ref):
        i = pl.program_id(0)

        @pl.when((i == 0) | (i == n_lat))
        def _():
            dm_ref[...] = jnp.zeros_like(dm_ref)

        gg, scale = g_ref[...], m_ref[4:5, :]
        n, xhat, r, y = _norm_mod(h_ref[...], gg, m_ref[3:4, :], scale)
        n_ref[...] = n.astype(BF16)
        cs, sn = cos_ref[...], sin_ref[...]
        dp = jnp.concatenate([du_ref[...], _unrope(dq_ref[...], cs, sn) * HEAD ** -0.5, _unrope(dk_ref[...], cs, sn),
                              dv_ref[...]], axis=1).astype(BF16)
        dp_ref[...] = dp
        dsh, dsc, dg, dhn = _norm_mod_bwd(_dotg(dp, w_ref[...], NT), xhat, r, y, gg, scale)
        dhi_ref[...] = dh_ref[...] + dhn
        dm_ref[0:1, :] += dsh
        dm_ref[1:2, :] += dsc
        dm_ref[3:4, :] += dg

    return pl.pallas_call(
        body, name=name, grid=(s // TM,),
        in_specs=[_rows(D), _rows(POOL_W), _rows(ATTN_W), _rows(KV_W), _rows(KV_W), _rows(D), _mods_spec(layer, n_lat),
                  _whole((1, D), (layer,)), _whole((D, PROJ_W)), _rows(BLK), _rows(BLK)],
        out_specs=[_rows(D), _rows(PROJ_W), _rows(D), _acc_spec(n_lat)],
        out_shape=[jax.ShapeDtypeStruct((s, D), F32), jax.ShapeDtypeStruct((s, PROJ_W), BF16),
                   jax.ShapeDtypeStruct((s, D), BF16), jax.ShapeDtypeStruct((2, 8, D), F32)],
        compiler_params=_params("arbitrary"),
    )(h, du, dq, dk, dv, dh, mods, g, w_in, cos, sin)


def _window(i, s):
    return pl.multiple_of(jnp.clip(i * QB - BLK, 0, s - WIN), BLK)


def mix_tables(t, s):
    n_lat = t // QB
    blocks = jnp.array([0, 1, n_lat - 1] + list(range(n_lat, s // QB)))[:, None, None]
    ws = jnp.clip(blocks * QB - BLK, 0, s - WIN)
    q = blocks * QB + jnp.arange(QB)[None, :, None]
    k = ws + jnp.arange(WIN)[None, None, :]
    is_lat = blocks < n_lat
    local = jnp.where(is_lat & (k < t) & (jnp.abs(k - q) <= BLK), 0.0, NEG_INF).astype(F32)
    bias = jnp.concatenate([local, jnp.zeros(local.shape[:2] + (s - t,), F32)], axis=2)
    seq_lo, seq_hi = jnp.where(is_lat, 0, t), jnp.where(is_lat, t, s)
    bands, counts = [], []
    for w in POOL_WINDOWS:
        lo, hi = jnp.maximum(q - w // 2, seq_lo), jnp.minimum(q + w - w // 2, seq_hi)
        bands.append((k >= lo) & (k < hi))
        counts.append((hi - lo).astype(F32))
    band = jnp.stack(bands, axis=1).astype(BF16)
    count = jnp.concatenate(counts + [jnp.ones(counts[0].shape[:2] + (BLK - len(counts),), F32)], axis=2)
    return dict(bias=bias, band=band, band_t=band.transpose(0, 1, 3, 2), count=count)


def _case_spec(table, n_lat_blk):
    def kind(i):
        return jnp.where(i < n_lat_blk - 1, jnp.minimum(i, 1), i - n_lat_blk + 3)

    shape = table.shape[1:]
    return pl.BlockSpec((None,) + shape, lambda i: (kind(i),) + (0,) * len(shape))


def _split_dot(band, v):
    hi = v.astype(BF16)
    return _dot(band, hi) + _dot(band, (v - hi.astype(F32)).astype(BF16))


def _pooled(u_ref, band_ref, cnt_ref, i, ws, gi):
    cols = slice(gi * GROUP, (gi + 1) * GROUP)
    mean = _split_dot(band_ref[gi], u_ref[pl.ds(ws, WIN), cols]) / cnt_ref[:, gi:gi + 1]
    return mean - u_ref[pl.ds(pl.multiple_of(i * QB, QB), QB), cols]


def _head_cols(hd):
    return slice(hd * HEAD, (hd + 1) * HEAD)


def _stack_heads(x, hk, first=0):
    return jnp.concatenate([x[:, first + (Q_GROUP * hk + g) * HEAD:first + (Q_GROUP * hk + g + 1) * HEAD]
                            for g in range(Q_GROUP)], axis=0)


def _biased(scores, bias):
    return (scores.reshape(Q_GROUP, QB, -1) + bias).reshape(Q_GROUP * QB, -1)


def _group_column(vals):
    row = lax.broadcasted_iota(jnp.int32, (Q_GROUP * QB, 1), 0)
    out = jnp.full((Q_GROUP * QB, 1), vals[Q_GROUP - 1], F32)
    for g in range(Q_GROUP - 2, -1, -1):
        out = jnp.where(row < (g + 1) * QB, vals[g], out)
    return out


def _lane_place(cols, width=BLK):
    lane = lax.broadcasted_iota(jnp.int32, (cols[0].shape[0], width), 1)
    out = jnp.zeros((cols[0].shape[0], width), F32)
    for hd, c in enumerate(cols):
        out = jnp.where(lane == hd, c, out)
    return out


def mix_fwd(h, q, k, v, u, w_pool, pool_scale, sink, w_out, mods, tables, layer, t, name, ex=None):
    s = h.shape[0]
    n_lat_blk = t // QB

    def body(h_ref, q_ref, k_ref, v_ref, u_ref, wp_ref, ps_ref, sink_ref, wo_ref, m_ref, bias_ref, band_ref, cnt_ref,
             ho_ref, cat_ref, lse_ref, mo_ref):
        i = pl.program_id(0)
        ws = _window(i, s)
        for gi in range(len(POOL_WINDOWS)):
            mixed = _dot(_pooled(u_ref, band_ref, cnt_ref, i, ws, gi).astype(BF16), wp_ref[gi])
            cat_ref[:, gi * GROUP:(gi + 1) * GROUP] = (mixed * ps_ref[:, gi * GROUP:(gi + 1) * GROUP]).astype(BF16)
        bias = bias_ref[...]
        k_all = jnp.concatenate([k_ref[pl.ds(ws, WIN), :], k_ref[t:s, :]], axis=0)
        v_all = jnp.concatenate([v_ref[pl.ds(ws, WIN), :], v_ref[t:s, :]], axis=0)
        lses = []
        for hk in range(N_HEADS // Q_GROUP):
            kv = _head_cols(hk)
            sc = _biased(_dotg(_stack_heads(q_ref[...], hk), k_all[:, kv], NT), bias)
            sk = _group_column([sink_ref[layer, Q_GROUP * hk + g] for g in range(Q_GROUP)])
            m = jnp.maximum(jnp.max(sc, axis=1, keepdims=True), sk)
            e = jnp.exp(sc - m)
            l = jnp.sum(e, axis=1, keepdims=True) + jnp.exp(sk - m)
            o = _dot(e.astype(BF16), v_all[:, kv]) * (1.0 / l)
            lse = m + jnp.log(l)
            for g in range(Q_GROUP):
                hd = Q_GROUP * hk + g
                cat_ref[:, POOL_W + hd * HEAD:POOL_W + (hd + 1) * HEAD] = o[g * QB:(g + 1) * QB].astype(BF16)
                lses.append(lse[g * QB:(g + 1) * QB])
        lse_ref[...] = _lane_place(lses)
        mo = _dot(cat_ref[...], wo_ref[...])
        mo_ref[...] = mo
        ho_ref[...] = h_ref[...] + m_ref[5:6, :] * mo

    blk = lambda cols: _rows(cols, QB)
    return _grid_call(
        body, name, s // QB,
        [blk(D), blk(ATTN_W), _whole((s, KV_W)), _whole((s, KV_W)), _whole((s, POOL_W)),
         _whole((len(POOL_WINDOWS), GROUP, GROUP), (layer,)), _whole((1, POOL_W), (layer,)),
         pl.BlockSpec(memory_space=pltpu.SMEM), _whole((POOL_W + ATTN_W, D)), _mods_spec(layer, n_lat_blk),
         _case_spec(tables["bias"], n_lat_blk), _case_spec(tables["band"], n_lat_blk), _case_spec(tables["count"], n_lat_blk)],
        [blk(D), blk(POOL_W + ATTN_W), blk(BLK), blk(D)],
        [jax.ShapeDtypeStruct((s, D), F32), jax.ShapeDtypeStruct((s, POOL_W + ATTN_W), BF16), jax.ShapeDtypeStruct((s, BLK), F32),
         jax.ShapeDtypeStruct((s, D), F32)],
        (h, q, k, v, u, w_pool, pool_scale, sink, w_out, mods, tables["bias"], tables["band"], tables["count"]), "parallel", ex)


def mix_bwd(dh, mo, q, k, v, u, lse, w_pool, pool_scale, sink, w_out, mods, tables, layer, t, name, ex=None):
    s = dh.shape[0]
    n_lat_blk = t // QB
    n_grp = len(POOL_WINDOWS)

    def body(dh_ref, mo_ref, q_ref, k_ref, v_ref, u_ref, lse_ref, wp_ref, ps_ref, sink_ref, wo_ref, m_ref,
             bias_ref, band_ref, band_t_ref, cnt_ref,
             dq_ref, dk_ref, dv_ref, du_ref, dmo_ref, dwp_ref, dps_ref, dsink_ref, dm_ref):
        i = pl.program_id(0)

        @pl.when(i == 0)
        def _():
            for ref in (dk_ref, dv_ref, du_ref, dwp_ref, dps_ref, dsink_ref):
                ref[...] = jnp.zeros_like(ref)

        @pl.when((i == 0) | (i == n_lat_blk))
        def _():
            dm_ref[...] = jnp.zeros_like(dm_ref)

        ws = _window(i, s)
        here = pl.ds(pl.multiple_of(i * QB, QB), QB)
        dho = dh_ref[...]
        dm_ref[2:3, :] += _sum0(dho * mo_ref[...])
        dmo = (m_ref[5:6, :] * dho).astype(BF16)
        dmo_ref[...] = dmo
        dcat = _dotg(dmo, wo_ref[...], NT)

        for gi in range(n_grp):
            cols = slice(gi * GROUP, (gi + 1) * GROUP)
            pooled = _pooled(u_ref, band_ref, cnt_ref, i, ws, gi).astype(BF16)
            dpo = dcat[:, cols]
            dps_ref[0:1, cols] += _sum0(dpo * _dot(pooled, wp_ref[gi]))
            dmixed = (dpo * ps_ref[:, cols]).astype(BF16)
            dwp_ref[gi] += _dotg(pooled, dmixed, TN)
            dpooled = _dotg(dmixed, wp_ref[gi], NT)
            du_ref[pl.ds(ws, WIN), cols] += _split_dot(band_t_ref[gi], dpooled / cnt_ref[:, gi:gi + 1])
            du_ref[here, cols] -= dpooled

        bias = bias_ref[...]
        k_all = jnp.concatenate([k_ref[pl.ds(ws, WIN), :], k_ref[t:s, :]], axis=0)
        v_all = jnp.concatenate([v_ref[pl.ds(ws, WIN), :], v_ref[t:s, :]], axis=0)
        qq, lse_all = q_ref[...], lse_ref[...]
        dqs, dsinks, dks, dvs = [], [], [], []
        for hk in range(N_HEADS // Q_GROUP):
            kv = _head_cols(hk)
            q4 = _stack_heads(qq, hk)
            lse = jnp.concatenate([lse_all[:, Q_GROUP * hk + g:Q_GROUP * hk + g + 1] for g in range(Q_GROUP)], axis=0)
            p = jnp.exp(_biased(_dotg(q4, k_all[:, kv], NT), bias) - lse)
            do = _stack_heads(dcat, hk, POOL_W).astype(BF16)
            dp = _dotg(do, v_all[:, kv], NT)
            delta = jnp.sum(p * dp, axis=1, keepdims=True)
            ds = (p * (dp - delta)).astype(BF16)
            sk = _group_column([sink_ref[layer, Q_GROUP * hk + g] for g in range(Q_GROUP)])
            dsk = -jnp.exp(sk - lse) * delta
            dq = _dot(ds, k_all[:, kv])
            for g in range(Q_GROUP):
                dqs.append(dq[g * QB:(g + 1) * QB])
                dsinks.append(_sum0(dsk[g * QB:(g + 1) * QB]))
            dks.append(_dotg(ds, q4, TN))
            dvs.append(_dotg(p.astype(BF16), do, TN))
        dq_ref[...] = jnp.concatenate(dqs, axis=1)
        dk, dv = jnp.concatenate(dks, axis=1), jnp.concatenate(dvs, axis=1)
        dk_ref[pl.ds(ws, WIN), :] += dk[:WIN]
        dv_ref[pl.ds(ws, WIN), :] += dv[:WIN]
        dk_ref[t:s, :] += dk[WIN:]
        dv_ref[t:s, :] += dv[WIN:]
        dsink_ref[0:1, :] += _lane_place(dsinks)

    blk = lambda cols: _rows(cols, QB)
    full = lambda shape: pl.BlockSpec(shape, lambda i: (0,) * len(shape))
    return _grid_call(
        body, name, s // QB,
        [blk(D), blk(D), blk(ATTN_W), _whole((s, KV_W)), _whole((s, KV_W)), _whole((s, POOL_W)),
         blk(BLK), _whole((n_grp, GROUP, GROUP), (layer,)), _whole((1, POOL_W), (layer,)),
         pl.BlockSpec(memory_space=pltpu.SMEM), _whole((POOL_W + ATTN_W, D)), _mods_spec(layer, n_lat_blk)]
        + [_case_spec(tables[key], n_lat_blk) for key in ("bias", "band", "band_t", "count")],
        [blk(ATTN_W), full((s, KV_W)), full((s, KV_W)), full((s, POOL_W)), blk(D),
         full((n_grp, GROUP, GROUP)), full((8, POOL_W)), full((8, BLK)), _acc_spec(n_lat_blk)],
        [jax.ShapeDtypeStruct((s, ATTN_W), F32), jax.ShapeDtypeStruct((s, KV_W), F32),
         jax.ShapeDtypeStruct((s, KV_W), F32), jax.ShapeDtypeStruct((s, POOL_W), F32),
         jax.ShapeDtypeStruct((s, D), BF16), jax.ShapeDtypeStruct((n_grp, GROUP, GROUP), F32),
         jax.ShapeDtypeStruct((8, POOL_W), F32), jax.ShapeDtypeStruct((8, BLK), F32), jax.ShapeDtypeStruct((2, 8, D), F32)],
        (dh, mo, q, k, v, u, lse, w_pool, pool_scale, sink, w_out, mods, tables["bias"], tables["band"], tables["band_t"],
         tables["count"]), "arbitrary", ex)


def loss_head(h, target, g, t, name):
    s = h.shape[0]
    n_lat = t // TM

    def body(h_ref, t_ref, g_ref, dh_ref, acc_ref):
        i = pl.program_id(0)

        @pl.when(i == 0)
        def _():
            acc_ref[...] = jnp.zeros_like(acc_ref)

        @pl.when(i < n_lat)
        def _():
            hh, gg = h_ref[...], g_ref[...]
            r = lax.rsqrt(jnp.mean(hh * hh, axis=-1, keepdims=True) + EPS)
            xhat = hh * r
            err = xhat * gg - t_ref[...]
            dy = err * (1.0 / D)
            dx = dy * gg
            dh_ref[...] = r * (dx - xhat * jnp.mean(dx * xhat, axis=-1, keepdims=True))
            acc_ref[0:1, :] += _sum0(dy * xhat)
            acc_ref[1:2, :] += _sum0(err * err)

        @pl.when(i >= n_lat)
        def _():
            dh_ref[...] = jnp.zeros_like(dh_ref)

    return pl.pallas_call(
        body, name=name, grid=(s // TM,),
        in_specs=[_rows(D), pl.BlockSpec((TM, D), lambda i: (jnp.minimum(i, n_lat - 1), 0)), _whole((1, D))],
        out_specs=[_rows(D), pl.BlockSpec((8, D), lambda i: (0, 0))],
        out_shape=[jax.ShapeDtypeStruct((s, D), F32), jax.ShapeDtypeStruct((8, D), F32)],
        compiler_params=_params("arbitrary"),
    )(h, target, g)


def mod_rows(c_all, w_mod, b_cols, name):
    def body(c_ref, w_ref, b_ref, o_ref):
        cc = c_ref[...]
        o_ref[...] = _dot((cc * jax.nn.sigmoid(cc)).astype(BF16), w_ref[...].astype(BF16)) + b_ref[...]

    return pl.pallas_call(
        body, name=name, grid=(2,),
        in_specs=[pl.BlockSpec((16, D), lambda l: (0, 0)), pl.BlockSpec((None, D, MOD_COLS), lambda l: (l, 0, 0)),
                  pl.BlockSpec((None, 1, MOD_COLS), lambda l: (l, 0, 0))],
        out_specs=pl.BlockSpec((None, 16, MOD_COLS), lambda l: (l, 0, 0)),
        out_shape=jax.ShapeDtypeStruct((2, 16, MOD_COLS), F32),
        compiler_params=_params("parallel"),
    )(c_all, w_mod, b_cols)


def mod_grads(c_all, dmod_cols, w_mod, name):
    def body(c_ref, d_ref, w_ref, dw_ref, dc_ref):
        @pl.when(pl.program_id(0) == 0)
        def _():
            dc_ref[...] = jnp.zeros_like(dc_ref)

        cc = c_ref[...]
        dd = d_ref[...].astype(BF16)
        dw_ref[...] = _dotg((cc * jax.nn.sigmoid(cc)).astype(BF16), dd, TN)
        dc_ref[...] += _dotg(dd, w_ref[...].astype(BF16), NT)

    return pl.pallas_call(
        body, name=name, grid=(2,),
        in_specs=[pl.BlockSpec((16, D), lambda l: (0, 0)), pl.BlockSpec((None, 16, MOD_COLS), lambda l: (l, 0, 0)),
                  pl.BlockSpec((None, D, MOD_COLS), lambda l: (l, 0, 0))],
        out_specs=[pl.BlockSpec((None, D, MOD_COLS), lambda l: (l, 0, 0)), pl.BlockSpec((16, D), lambda l: (0, 0))],
        out_shape=[jax.ShapeDtypeStruct((2, D, MOD_COLS), F32), jax.ShapeDtypeStruct((16, D), F32)],
        compiler_params=_params("arbitrary"),
    )(c_all, dmod_cols, w_mod)


def _row_tile(rows, cols, n_arrays):
    budget = VMEM_LIMIT_BYTES // 4 // (2 * 4 * n_arrays * cols)
    best = None
    for tr in range(16, rows + 1, 16):
        if rows % tr == 0 and tr <= budget:
            best = tr
    return best if best is not None else rows


def elementwise(fn, ins, out_dtypes, name, ex=None):
    rows, cols = ins[0].shape
    tr = _row_tile(rows, cols, len(ins) + len(out_dtypes))

    def body(*refs):
        outs = fn(*[r[...] for r in refs[:len(ins)]])
        for o_ref, o in zip(refs[len(ins):], outs):
            o_ref[...] = o.astype(o_ref.dtype)

    spec = pl.BlockSpec((tr, cols), lambda i: (i, 0))
    outs, got = _grid_call(body, name, rows // tr, [spec] * len(ins), [spec] * len(out_dtypes),
                           [jax.ShapeDtypeStruct((rows, cols), dt) for dt in out_dtypes], ins, "parallel", ex)
    return outs if ex is None else (outs, got)


def _adamw_tile(w, g, m, v):
    m = ADAM_B1 * m + (1.0 - ADAM_B1) * g
    v = ADAM_B2 * v + (1.0 - ADAM_B2) * (g * g)
    m_hat = m / (1.0 - ADAM_B1 ** ADAM_STEP)
    v_hat = v / (1.0 - ADAM_B2 ** ADAM_STEP)
    return -ADAM_LR * (m_hat / (jnp.sqrt(v_hat) + ADAM_EPS) + ADAM_WD * w), m, v


def adamw(w, g, m, v, name, ex=None):
    shape = w.shape
    two_d = (-1, shape[-1]) if w.ndim > 1 else (1, -1)
    outs = elementwise(_adamw_tile, [a.reshape(two_d) for a in (w, g, m, v)], [F32] * 3, name, ex)
    outs, got = outs if ex is not None else (outs, None)
    outs = [o.reshape(shape) for o in outs]
    return outs if ex is None else (outs, got)


def _prefetch_call(body, name, grid, in_specs, out_specs, out_shape, place, args, aliases=None):
    spec = pltpu.PrefetchScalarGridSpec(num_scalar_prefetch=1, grid=grid, in_specs=in_specs, out_specs=out_specs)
    return pl.pallas_call(body, name=name, grid_spec=spec, out_shape=out_shape, input_output_aliases=aliases or {},
                          compiler_params=_params(*["parallel"] * len(grid)))(place, *args)


def cast_place(w, layer, place, name):
    _, r, c = w.shape
    tr = _row_tile(r, c, 2)

    def body(p_ref, w_ref, o_ref):
        o_ref[...] = w_ref[...].astype(BF16)

    return _prefetch_call(
        body, name, (r // tr,), [pl.BlockSpec((None, tr, c), lambda i, p: (layer, i, 0))],
        pl.BlockSpec((None, tr, c), lambda i, p: (p[1], i, 0)), jax.ShapeDtypeStruct((N_SLOT, r, c), BF16), place, [w])


def pair_sum(g32, got, place, name):
    n_slot, rh, c = got.shape
    tr = _row_tile(rh, c, 4)
    per = rh // tr

    def body(p_ref, a_ref, b_ref, o_ref, o16_ref):
        r = a_ref[...] + b_ref[...].astype(F32)
        o_ref[...] = r
        o16_ref[...] = r.astype(BF16)

    half = pl.BlockSpec((None, tr, c), lambda s, i, p: (s, i, 0))
    return _prefetch_call(
        body, name, (n_slot, per), [pl.BlockSpec((None, tr, c), lambda s, i, p: (s, p[0] * per + i, 0)), half], [half, half],
        [jax.ShapeDtypeStruct(got.shape, F32), jax.ShapeDtypeStruct(got.shape, BF16)], place, [g32, got])


def chip_sum(p32, got, place, name):
    _, rh, c = p32.shape
    tr = _row_tile(rh, c, 5)
    per = rh // tr

    def body(p_ref, m_ref, r0_ref, r1_ref, r2_ref, o_ref):
        o_ref[...] = m_ref[...] + r0_ref[...].astype(F32) + r1_ref[...].astype(F32) + r2_ref[...].astype(F32)

    part = pl.BlockSpec((tr, c), lambda i, p: (i, 0))
    return _prefetch_call(
        body, name, (per,), [pl.BlockSpec((None, tr, c), lambda i, p: (p[1], i, 0)), part, part, part],
        pl.BlockSpec((tr, c), lambda i, p: (p[0] * per + i, 0)), jax.ShapeDtypeStruct((2 * rh, c), F32), place, [p32, *got])


def adamw_layers(w, g0, g1, m, v, name, ex=None):
    _, r, c = w.shape
    tr = _row_tile(r, c, 10)

    def body(w_ref, g0_ref, g1_ref, m_ref, v_ref, g_ref, d_ref, mo_ref, vo_ref):
        g = jnp.where(pl.program_id(0) == 0, g0_ref[...], g1_ref[...])
        g_ref[...] = g
        d_ref[...], mo_ref[...], vo_ref[...] = _adamw_tile(w_ref[...], g, m_ref[...], v_ref[...])

    stacked = pl.BlockSpec((None, tr, c), lambda l, i: (l, i, 0))
    layer = pl.BlockSpec((tr, c), lambda l, i: (i, 0))
    outs, got = _grid_call(body, name, (2, r // tr), [stacked, layer, layer, stacked, stacked], [stacked] * 4,
                           [jax.ShapeDtypeStruct(w.shape, F32)] * 4, (w, g0, g1, m, v), "parallel", ex)
    return outs if ex is None else (outs, got)


def sum8(gathered, name):
    def body(*refs):
        n = len(refs) // 2
        for g_ref, o_ref in zip(refs[:n], refs[n:]):
            acc = g_ref[0]
            for dev in range(1, N_DEV):
                acc = acc + g_ref[dev]
            o_ref[...] = acc

    return pl.pallas_call(
        body, name=name,
        out_shape=[jax.ShapeDtypeStruct(a.shape[1:], F32) for a in gathered],
        compiler_params=_params(),
    )(*gathered)


def _place():
    return lax.axis_index("x"), lax.axis_index("y"), lax.axis_index("c")


def _any(n):
    return [pl.BlockSpec(memory_space=pl.ANY)] * n


def gather8_exchange(blocks):
    n = len(blocks)

    def copy(outs, sems, ti, k, block, to, src=None):
        dst = outs[ti].at[4 * block[0] + 2 * block[1] + block[2]]
        return pltpu.make_async_remote_copy(src_ref=dst if src is None else src, dst_ref=dst, send_sem=sems[0].at[ti, k],
                                            recv_sem=sems[1].at[ti, k], device_id=to, device_id_type=MESH)

    def first(ins, outs, sems):
        x, y, c = _place()
        local, sent = [], []
        for ti in range(n):
            local.append(pltpu.make_async_copy(ins[ti], outs[ti].at[4 * x + 2 * y + c], sems[2].at[ti]))
            sent.append(copy(outs, sems, ti, 0, (x, y, c), (x, y, 1 - c), src=ins[ti]))
            sent += [copy(outs, sems, ti, 1 + j, (x, y, c), (*chip, c), src=ins[ti]) for j, chip in enumerate(_three_chips(x, y))]
        return local, sent

    def start(ins, outs, sems):
        local, sent = first(ins, outs, sems)
        for cp in local + sent:
            cp.start()

    def finish(ins, outs, sems):
        x, y, c = _place()
        me, sibling, chips = (x, y, c), (x, y, 1 - c), _three_chips(x, y)
        local, sent = first(ins, outs, sems)
        for ti in range(n):
            for j, chip in enumerate(chips):
                copy(outs, sems, ti, 1 + j, (*chip, c), me).wait_recv()
                sent.append(copy(outs, sems, ti, 4 + j, (*chip, c), sibling))
                sent[-1].start()
        for ti in range(n):
            copy(outs, sems, ti, 0, sibling, me).wait_recv()
            for j, chip in enumerate(chips):
                copy(outs, sems, ti, 4 + j, (*chip, 1 - c), me).wait_recv()
        for cp in sent:
            cp.wait_send()
        for cp in local:
            cp.wait()

    return dict(ins=list(blocks), out_shape=[jax.ShapeDtypeStruct((N_DEV,) + b.shape, b.dtype) for b in blocks], aliases={},
                start=start, finish=finish,
                scratch=[pltpu.SemaphoreType.DMA((n, 7)), pltpu.SemaphoreType.DMA((n, 7)), pltpu.SemaphoreType.DMA((n,))])


def all_gather(blocks, name):
    return run_exchange(gather8_exchange(blocks), name)


def _three_chips(x, y):
    return [(1 - x, y), (x, 1 - y), (1 - x, 1 - y)]


def gather_exchange(placed):
    n = len(placed)

    def copy(bufs, sems, ti, k, chip, core, to):
        rh = bufs[ti].shape[1] // 2
        half = bufs[ti].at[2 * chip[0] + chip[1], pl.ds(core * rh, rh), :]
        return pltpu.make_async_remote_copy(src_ref=half, dst_ref=half, send_sem=sems[0].at[ti, k], recv_sem=sems[1].at[ti, k],
                                            device_id=to, device_id_type=MESH)

    def sends(bufs, sems):
        x, y, c = _place()
        return [copy(bufs, sems, ti, k, (x, y), c, (*chip, c)) for ti in range(n) for k, chip in enumerate(_three_chips(x, y))]

    def start(ins, bufs, sems):
        for cp in sends(bufs, sems):
            cp.start()

    def finish(ins, bufs, sems):
        x, y, c = _place()
        chips = _three_chips(x, y)
        passed = []
        for ti in range(n):
            for k, chip in enumerate(chips):
                copy(bufs, sems, ti, k, chip, c, (x, y, c)).wait_recv()
                passed.append(copy(bufs, sems, ti, 3 + k, chip, c, (x, y, 1 - c)))
                passed[-1].start()
        for ti in range(n):
            for k, chip in enumerate(chips):
                copy(bufs, sems, ti, 3 + k, chip, 1 - c, (x, y, c)).wait_recv()
        for cp in sends(bufs, sems) + passed:
            cp.wait_send()

    return dict(ins=list(placed), out_shape=[jax.ShapeDtypeStruct(w.shape, w.dtype) for w in placed],
                aliases={i: i for i in range(n)}, start=start, finish=finish,
                scratch=[pltpu.SemaphoreType.DMA((n, 6)), pltpu.SemaphoreType.DMA((n, 6))])


def scatter_exchange(p16):
    n = len(p16)

    def copies(ins, got, sems):
        x, y, c = _place()
        return [pltpu.make_async_remote_copy(src_ref=ins[ti].at[2 * chip[0] + chip[1]], dst_ref=got[3 * ti + k],
                                             send_sem=sems[0].at[ti, k], recv_sem=sems[1].at[ti, k], device_id=(*chip, c),
                                             device_id_type=MESH)
                for ti in range(n) for k, chip in enumerate(_three_chips(x, y))]

    def start(ins, got, sems):
        for cp in copies(ins, got, sems):
            cp.start()

    def finish(ins, got, sems):
        for cp in copies(ins, got, sems):
            cp.wait()

    return dict(ins=list(p16), out_shape=[jax.ShapeDtypeStruct(a.shape[1:], BF16) for a in p16 for _ in range(3)], aliases={},
                start=start, finish=finish, scratch=[pltpu.SemaphoreType.DMA((n, 3)), pltpu.SemaphoreType.DMA((n, 3))])


def run_exchange(ex, name):
    ci, co = len(ex["ins"]), len(ex["out_shape"])

    def body(*refs):
        ins, outs, sems = refs[:ci], refs[ci:ci + co], refs[ci + co:]
        ex["start"](ins, outs, sems)
        ex["finish"](ins, outs, sems)

    return pl.pallas_call(body, name=name, in_specs=_any(ci), out_specs=_any(co), out_shape=ex["out_shape"],
                          input_output_aliases=ex["aliases"], scratch_shapes=ex["scratch"])(*ex["ins"])


def _grid_call(body, name, grid, in_specs, out_specs, out_shape, args, sem, ex=None):
    grid = (grid,) if isinstance(grid, int) else tuple(grid)
    sems_of = (sem,) * len(grid) if isinstance(sem, str) else tuple(sem)
    n_in, n_out = len(in_specs), len(out_specs)
    if ex is None:
        return pl.pallas_call(body, name=name, grid=grid, in_specs=in_specs, out_specs=out_specs, out_shape=out_shape,
                              compiler_params=_params(*sems_of))(*args), []
    ci, co = len(ex["ins"]), len(ex["out_shape"])

    def at(step):
        where = [pl.program_id(ax) == (0 if step == "first" else grid[ax] - 1) for ax in range(len(grid))]
        return functools.reduce(jnp.logical_and, where)

    def carrying(*refs):
        c_in, c_out = refs[n_in:n_in + ci], refs[n_in + ci + n_out:n_in + ci + n_out + co]
        sems = refs[n_in + ci + n_out + co:]

        @pl.when(at("first"))
        def _():
            ex["start"](c_in, c_out, sems)

        body(*refs[:n_in], *refs[n_in + ci:n_in + ci + n_out])

        @pl.when(at("last"))
        def _():
            ex["finish"](c_in, c_out, sems)

    outs = pl.pallas_call(
        carrying, name=name, grid=grid, in_specs=list(in_specs) + _any(ci), out_specs=list(out_specs) + _any(co),
        out_shape=list(out_shape) + ex["out_shape"], scratch_shapes=ex["scratch"],
        input_output_aliases={n_in + i: n_out + j for i, j in ex["aliases"].items()},
        compiler_params=_params(*["arbitrary"] * len(grid)),
    )(*args, *ex["ins"])
    return outs[:n_out], outs[n_out:]


def both(*exchanges):
    exchanges = [ex for ex in exchanges if ex is not None]
    if len(exchanges) < 2:
        return exchanges[0] if exchanges else None
    n_ins = [len(ex["ins"]) for ex in exchanges]
    n_outs = [len(ex["out_shape"]) for ex in exchanges]
    n_sems = [len(ex["scratch"]) for ex in exchanges]

    def parts(seq, counts, k):
        first = sum(counts[:k])
        return seq[first:first + counts[k]]

    def run(phase):
        def go(ins, outs, sems):
            for k, ex in enumerate(exchanges):
                ex[phase](parts(ins, n_ins, k), parts(outs, n_outs, k), parts(sems, n_sems, k))
        return go

    aliases = {sum(n_ins[:k]) + i: sum(n_outs[:k]) + j for k, ex in enumerate(exchanges) for i, j in ex["aliases"].items()}
    return dict(ins=[a for ex in exchanges for a in ex["ins"]], out_shape=[o for ex in exchanges for o in ex["out_shape"]],
                aliases=aliases, start=run("start"), finish=run("finish"), scratch=[s for ex in exchanges for s in ex["scratch"]])


def split_outputs(got, *exchanges):
    got, out = list(got), []
    for ex in exchanges:
        n = len(ex["out_shape"]) if ex is not None else 0
        out.append(got[:n])
        got = got[n:]
    return out


def pair_exchange(g16):
    n = len(g16)

    def copies(a16, got, sems):
        x, y, c = _place()
        out = []
        for ti in range(n):
            rh = a16[ti].shape[1] // 2
            out.append(pltpu.make_async_remote_copy(
                src_ref=a16[ti].at[:, pl.ds((1 - c) * rh, rh), :], dst_ref=got[ti], send_sem=sems[0].at[ti],
                recv_sem=sems[1].at[ti], device_id=(x, y, 1 - c), device_id_type=MESH))
        return out

    def start(a16, got, sems):
        for cp in copies(a16, got, sems):
            cp.start()

    def finish(a16, got, sems):
        for cp in copies(a16, got, sems):
            cp.wait()

    return dict(ins=list(g16), out_shape=[jax.ShapeDtypeStruct((a.shape[0], a.shape[1] // 2, a.shape[2]), BF16) for a in g16],
                aliases={}, start=start, finish=finish, scratch=[pltpu.SemaphoreType.DMA((n,)), pltpu.SemaphoreType.DMA((n,))])


def pair_gather(halves, name):
    n = len(halves)

    def body(*refs):
        bufs = refs[n:2 * n]
        send_sems, recv_sems = refs[2 * n:]
        x, y, c = _place()
        copies = []
        for ti in range(n):
            rh = bufs[ti].shape[0] // 2
            rows = bufs[ti].at[pl.ds(c * rh, rh), :]
            copies.append(pltpu.make_async_remote_copy(src_ref=rows, dst_ref=rows, send_sem=send_sems.at[ti],
                                                       recv_sem=recv_sems.at[ti], device_id=(x, y, 1 - c), device_id_type=MESH))
        for cp in copies:
            cp.start()
        for ti, cp in enumerate(copies):
            cp.wait_send()
            rh = bufs[ti].shape[0] // 2
            theirs = bufs[ti].at[pl.ds((1 - c) * rh, rh), :]
            pltpu.make_async_remote_copy(src_ref=theirs, dst_ref=theirs, send_sem=send_sems.at[ti], recv_sem=recv_sems.at[ti],
                                         device_id=(x, y, 1 - c), device_id_type=MESH).wait_recv()

    return pl.pallas_call(
        body, name=name, in_specs=_any(n), out_specs=_any(n), input_output_aliases={i: i for i in range(n)},
        out_shape=[jax.ShapeDtypeStruct(a.shape, a.dtype) for a in halves],
        scratch_shapes=[pltpu.SemaphoreType.DMA((n,)), pltpu.SemaphoreType.DMA((n,))],
    )(*halves)


def reduce_small(dm_f1, dm_mix, dm_gate, dm_f2, loss_blk, name):
    def body(f1_ref, mix_ref, gate_ref, f2_ref, l_ref, tot_ref, rows_ref, fin_ref):
        rows_ref[...] = jnp.zeros_like(rows_ref)
        tot_ref[...] = jnp.zeros_like(tot_ref)
        mod_src = [(f1_ref, 0), (f1_ref, 1), (f1_ref, 2), (mix_ref, 0), (mix_ref, 1), (gate_ref, 2),
                   (f2_ref, 0), (f2_ref, 1), (f2_ref, 2)]
        norm_src = [(f1_ref, 3), (mix_ref, 3), (f2_ref, 3)]
        for l in range(2):
            for k, (ref, r) in enumerate(mod_src + norm_src):
                lat = ref[0, l, 0, r:r + 1, :]
                ctx = ref[0, l, 1, r:r + 1, :]
                for dev in range(N_DEV):
                    if dev:
                        lat = lat + ref[dev, l, 0, r:r + 1, :]
                        ctx = ctx + ref[dev, l, 1, r:r + 1, :]
                    if k < N_MOD:
                        rows_ref[l, dev, k:k + 1, :] = ref[dev, l, 0, r:r + 1, :]
                if k < N_MOD:
                    rows_ref[l, N_DEV, k:k + 1, :] = ctx
                tot_ref[l, k:k + 1, :] = lat + ctx
        acc = l_ref[0]
        for dev in range(1, N_DEV):
            acc = acc + l_ref[dev]
        loss = (0.5 / D) * jnp.sum(acc[1:2, :], axis=1, keepdims=True)
        row = lax.broadcasted_iota(jnp.int32, (8, D), 0)
        fin_ref[...] = jnp.where(row == 0, acc[0:1, :], loss)

    return pl.pallas_call(
        body, name=name,
        out_shape=[jax.ShapeDtypeStruct((2, 16, D), F32), jax.ShapeDtypeStruct((2, 16, 16, D), F32),
                   jax.ShapeDtypeStruct((8, D), F32)],
        compiler_params=_params(),
    )(dm_f1, dm_mix, dm_gate, dm_f2, loss_blk)


def rope_tables(t, s):
    rows = t // GRID_W
    row = jnp.repeat(jnp.arange(rows), GRID_W).astype(F32)
    col = jnp.tile(jnp.arange(GRID_W), rows).astype(F32)
    inv = ROPE_BASE ** (-jnp.arange(0, HEAD // 2, 2, dtype=F32) / (HEAD // 2))
    ang = jnp.concatenate([row[:, None] * inv, col[:, None] * inv], axis=-1)
    cos, sin = jnp.cos(ang), jnp.sin(ang)
    cos = jnp.concatenate([jnp.tile(cos, (1, 4)), jnp.ones((s - t, BLK), F32)], axis=0)
    sin = jnp.concatenate([jnp.tile(jnp.concatenate([-sin, sin], axis=1), (1, 2)), jnp.zeros((s - t, BLK), F32)], axis=0)
    return cos, sin


BIG = ("ffn1_in", "ffn1_out", "w_in", "w_out", "ffn2_in", "ffn2_out")
GROUPS = dict(ffn1=("ffn1_in", "ffn1_out"), mix=("w_in", "w_out"), ffn2=("ffn2_in", "ffn2_out"))
GATHER_BEHIND = {("ffn1", 0): [("mix", 0)], ("mix", 0): [("ffn2", 0)], ("ffn2", 0): [("ffn1", 1)], ("ffn1", 1): [("mix", 1)],
                 ("mix", 1): [("ffn2", 1)]}


def _slot_major(name, g):
    if name == "w_in":
        return jnp.stack(jnp.split(g, N_SLOT, axis=1), axis=0)
    if name in ("ffn1_in", "ffn2_in"):
        return g
    return g.reshape(N_SLOT, g.shape[0] // N_SLOT, g.shape[1])


def _whole_weight(name, buf):
    if name == "w_in":
        return buf.transpose(1, 0, 2).reshape(D, PROJ_W)
    if name in ("ffn1_in", "ffn2_in"):
        return buf
    return buf.reshape(-1, buf.shape[2])


def local_step(x1, ctx1, target, mods, norms, nfinal, placed, w_pool, pool_scale, sink, place, small_blocks):
    t, s = x1.shape[0], x1.shape[0] + ctx1.shape[0]
    n_lat = t // TM
    cos, sin = rope_tables(t, s)
    tables = mix_tables(t, s)
    wts ={name: list(pair) for name, pair in placed.items()}

    def gather(groups):
        return gather_exchange([wts[name][l] for grp, l in groups for name in GROUPS[grp]])

    def gathered(groups, arrays):
        arrays = list(arrays)
        for grp, l in groups:
            for name in GROUPS[grp]:
                wts[name][l] = arrays.pop(0)

    def weight(name, l):
        return _whole_weight(name, wts[name][l])

    def fwd_ex(grp, l):
        groups = GATHER_BEHIND.get((grp, l))
        return (groups, gather(groups)) if groups else (None, None)

    gathered([("ffn1", 0)], run_exchange(gather([("ffn1", 0)]), "gather_first"))
    h = jnp.concatenate([x1, ctx1], axis=0)
    saved = []
    for l in range(2):
        h0 = h
        groups, ex = fwd_ex("ffn1", l)
        (h1, ab1, f1), got = ffn_fwd(h0, mods, norms[0], weight("ffn1_in", l), weight("ffn1_out", l), l, 0, n_lat, f"ffn1_fwd_{l}", ex)
        gathered(groups or [], got)
        u, q, k, v = proj_fwd(h1, mods, norms[1], weight("w_in", l), cos, sin, l, n_lat, f"proj_fwd_{l}")
        groups, ex = fwd_ex("mix", l)
        (h2, cat, lse, mo), got = mix_fwd(h1, q, k, v, u, w_pool, pool_scale, sink, weight("w_out", l), mods, tables, l, t,
                                          f"mix_fwd_{l}", ex)
        gathered(groups or [], got)
        groups, ex = fwd_ex("ffn2", l)
        (h, ab2, f2), got = ffn_fwd(h2, mods, norms[2], weight("ffn2_in", l), weight("ffn2_out", l), l, 6, n_lat, f"ffn2_fwd_{l}", ex)
        gathered(groups or [], got)
        saved.append((h0, ab1, f1, h1, u, q, k, v, cat, lse, mo, h2, ab2, f2))
    dh, loss_blk = loss_head(h, target, nfinal, t, "loss_head")

    halves = {name: [None, None] for name in BIG}
    pending = []

    def summed_in_pair(grp, l, name_a, g_a, name_b, wgrad_b):
        g_b, got_a = wgrad_b(pair_exchange([_slot_major(name_a, g_a[1])]))
        got_b = run_exchange(pair_exchange([_slot_major(name_b, g_b[1])]), f"pair_exchange_{name_b}_{l}")
        by = {name_a: (g_a[0], got_a[0]), name_b: (g_b[0], got_b[0])}
        pending.append((grp, l, [pair_sum(_slot_major(n, by[n][0]), by[n][1], place, f"pair_sum_{n}_{l}") for n in GROUPS[grp]]))

    def scatter():
        return scatter_exchange([p16 for _, p16 in pending[0][2]]) if pending else None

    def scattered(got):
        if pending:
            grp, l, pairs = pending.pop(0)
            for i, name in enumerate(GROUPS[grp]):
                halves[name][l] = chip_sum(pairs[i][0], got[3 * i:3 * i + 3], place, f"chip_sum_{name}_{l}")

    small = [None, None]
    for l in (1, 0):
        h0, ab1, f1, h1, u, q, k, v, cat, lse, mo, h2, ab2, f2 = saved[l]
        (dh, dab, df, n, act, dm_f2), got = ffn_bwd(h2, ab2, f2, dh, mods, norms[2], weight("ffn2_in", l), weight("ffn2_out", l),
                                                    l, 6, n_lat, f"ffn2_bwd_{l}", scatter())
        scattered(got)
        g_in, _ = wgrad(n, dab, D, FF_COLS, FF_COLS, f"ffn2_in_wgrad_{l}")
        summed_in_pair("ffn2", l, "ffn2_in", g_in, "ffn2_out",
                       lambda ex, a=act, b=df: wgrad(a, b, D_FF // 2, D, None, f"ffn2_out_wgrad_{l}", ex))
        (dq, dk, dv, du, dmo, dwp, dps, dsink, dm_gate), got = mix_bwd(
            dh, mo, q, k, v, u, lse, w_pool, pool_scale, sink, weight("w_out", l), mods, tables, l, t, f"mix_bwd_{l}", scatter())
        scattered(got)
        g_wo, _ = wgrad(cat, dmo, POOL_W + ATTN_W, D, None, f"w_out_wgrad_{l}")
        dh, dp, n, dm_mix = proj_bwd(h1, du, dq, dk, dv, dh, mods, norms[1], weight("w_in", l), cos, sin, l, n_lat, f"proj_bwd_{l}")
        summed_in_pair("mix", l, "w_out", g_wo, "w_in",
                       lambda ex, a=n, b=dp: wgrad(a, b, D, PROJ_W // 2, None, f"w_in_wgrad_{l}", ex))
        (dh, dab, df, n, act, dm_f1), got = ffn_bwd(h0, ab1, f1, dh, mods, norms[0], weight("ffn1_in", l), weight("ffn1_out", l),
                                                    l, 0, n_lat, f"ffn1_bwd_{l}", scatter())
        scattered(got)
        small[l] = dict(dm_f1=dm_f1, dm_mix=dm_mix, dm_gate=dm_gate, dm_f2=dm_f2, dwp=dwp, dps=dps, dsink=dsink)
        g_in, small_all = wgrad(n, dab, D, FF_COLS, FF_COLS, f"ffn1_in_wgrad_{l}", None if l else gather8_exchange(small_blocks(small, loss_blk)))
        summed_in_pair("ffn1", l, "ffn1_in", g_in, "ffn1_out",
                       lambda ex, a=act, b=df: wgrad(a, b, D_FF // 2, D, None, f"ffn1_out_wgrad_{l}", ex))
    return dh[:t], halves, pending.pop(0)[2], small_all


def _silu_grad(z):
    sg = jax.nn.sigmoid(z)
    return sg * (1 + z * (1 - sg))


def kernel(x, c, ctx, c_ctx, w_mod, b_mod, norm_ffn1, w_ffn1_in, w_ffn1_out, norm_mix, w_in, w_pool, pool_scale, sink, w_out, norm_ffn2, w_ffn2_in, w_ffn2_out, norm_final, loss_target, m_c_ctx, m_w_mod, m_b_mod, m_norm_ffn1, m_w_ffn1_in, m_w_ffn1_out, m_norm_mix, m_w_in, m_w_pool, m_pool_scale, m_sink, m_w_out, m_norm_ffn2, m_w_ffn2_in, m_w_ffn2_out, m_norm_final, v_c_ctx, v_w_mod, v_b_mod, v_norm_ffn1, v_w_ffn1_in, v_w_ffn1_out, v_norm_mix, v_w_in, v_w_pool, v_pool_scale, v_sink, v_w_out, v_norm_ffn2, v_w_ffn2_in, v_w_ffn2_out, v_norm_final):
    px, py, pc = _place()
    slot, me = 2 * px + py, 4 * px + 2 * py + pc
    n_grp = len(POOL_WINDOWS)

    (c_rows,) = all_gather([c.reshape(8, D // 8)], "gather_c")
    c_all = jnp.concatenate([c_rows.reshape(N_DEV, D), c_ctx.reshape(1, D), jnp.zeros((16 - N_DEV - 1, D), F32)], axis=0)
    b_cols = lax.dynamic_slice(b_mod, (0, slot * MOD_COLS), (2, MOD_COLS)).reshape(2, 1, MOD_COLS)
    (mod_parts,) = all_gather([mod_rows(c_all, w_mod, b_cols, "mod_rows")], "gather_mods")
    mods_all = mod_parts[0::2].transpose(1, 2, 0, 3).reshape(2, 16, N_MOD * D)
    mx = lax.dynamic_slice(mods_all, (0, me, 0), (2, 1, N_MOD * D)).reshape(2, N_MOD, D)
    mc = mods_all[:, N_DEV].reshape(2, N_MOD, D)
    pad = jnp.zeros((2, 16 - N_MOD, D), F32)
    mods = jnp.stack([jnp.concatenate([mx, pad], axis=1), jnp.concatenate([mc, pad], axis=1)], axis=1)

    place = jnp.stack([pc, slot]).astype(jnp.int32)
    shards = dict(ffn1_in=w_ffn1_in, ffn1_out=w_ffn1_out, w_in=w_in, w_out=w_out, ffn2_in=w_ffn2_in, ffn2_out=w_ffn2_out)
    placed = {name: [cast_place(shards[name], l, place, f"cast_{name}_{l}") for l in range(2)] for name in BIG}
    norms = [g.reshape(2, 1, D) for g in (norm_ffn1, norm_mix, norm_ffn2)]
    row_sums = ("dm_f1", "dm_mix", "dm_gate", "dm_f2")

    def small_blocks(small, loss_blk):
        stacked = {k: jnp.stack([small[0][k], small[1][k]]) for k in row_sums + ("dwp", "dps", "dsink")}
        return ([stacked[k].reshape(32, D) for k in row_sums]
                + [stacked["dwp"].reshape(2 * n_grp * GROUP, GROUP), stacked["dps"].reshape(16, POOL_W),
                   stacked["dsink"].reshape(16, BLK), loss_blk])

    dx, halves, last_pairs, small_all = local_step(x[0], ctx[0], loss_target[0], mods, norms, norm_final.reshape(1, D), placed,
                                                   w_pool.astype(BF16), pool_scale.reshape(2, 1, POOL_W), sink, place, small_blocks)
    grads = {}

    *g_dm, g_dwp, g_dps, g_dsink, g_loss = small_all
    tot, rows, fin = reduce_small(*[g.reshape(N_DEV, 2, 2, 8, D) for g in g_dm], g_loss, "reduce_small")
    s_dwp, s_dps, s_dsink = sum8([g_dwp, g_dps, g_dsink], "sum_pool_sink")
    grads.update(
        w_pool=s_dwp.reshape(2, n_grp, GROUP, GROUP), pool_scale=s_dps.reshape(2, 8, POOL_W)[:, 0],
        sink=s_dsink.reshape(2, 8, BLK)[:, 0, :N_HEADS], b_mod=tot[:, :N_MOD].reshape(2, N_MOD * D),
        norm_ffn1=tot[:, N_MOD], norm_mix=tot[:, N_MOD + 1], norm_ffn2=tot[:, N_MOD + 2], norm_final=fin[0])
    loss = fin[1, 0]

    dmod_cols = lax.dynamic_slice(rows[:, :, :N_MOD, :].reshape(2, 16, N_MOD * D), (0, 0, slot * MOD_COLS), (2, 16, MOD_COLS))
    grads["w_mod"], dc = mod_grads(c_all, dmod_cols, w_mod, "mod_grads")
    (g_dc,) = all_gather([dc], "gather_dc")
    (s_dc,) = sum8([g_dc], "sum_dc")
    (d_c_ctx,) = elementwise(lambda d, z: (0.5 * d * _silu_grad(z),), [s_dc[N_DEV:N_DEV + 1], c_ctx.reshape(1, D)], [F32], "c_ctx_grad")
    grads["c_ctx"] = d_c_ctx.reshape(D)

    given = dict(c_ctx=(c_ctx, m_c_ctx, v_c_ctx), w_mod=(w_mod, m_w_mod, v_w_mod), b_mod=(b_mod, m_b_mod, v_b_mod),
                 norm_ffn1=(norm_ffn1, m_norm_ffn1, v_norm_ffn1), w_ffn1_in=(w_ffn1_in, m_w_ffn1_in, v_w_ffn1_in),
                 w_ffn1_out=(w_ffn1_out, m_w_ffn1_out, v_w_ffn1_out), norm_mix=(norm_mix, m_norm_mix, v_norm_mix),
                 w_in=(w_in, m_w_in, v_w_in), w_pool=(w_pool, m_w_pool, v_w_pool),
                 pool_scale=(pool_scale, m_pool_scale, v_pool_scale), sink=(sink, m_sink, v_sink), w_out=(w_out, m_w_out, v_w_out),
                 norm_ffn2=(norm_ffn2, m_norm_ffn2, v_norm_ffn2), w_ffn2_in=(w_ffn2_in, m_w_ffn2_in, v_w_ffn2_in),
                 w_ffn2_out=(w_ffn2_out, m_w_ffn2_out, v_w_ffn2_out), norm_final=(norm_final, m_norm_final, v_norm_final))
    got = run_exchange(scatter_exchange([p16 for _, p16 in last_pairs]), "scatter_last")
    halves["ffn1_in"][0] = chip_sum(last_pairs[0][0], got[0:3], place, "chip_sum_ffn1_in_0")
    halves["ffn1_out"][0] = chip_sum(last_pairs[1][0], got[3:6], place, "chip_sum_ffn1_out_0")
    order = [(name, l) for name in BIG for l in range(2)]
    shard = dict(zip(order, pair_gather([halves[name][l] for name, l in order], "grad_pair_gather")))

    g_out, d_out, m_out, v_out = [], [], [], []
    for name, (w, m, v) in given.items():
        if name in BIG or name[2:] in BIG:
            key = name if name in BIG else name[2:]
            grad, delta, new_m, new_v = adamw_layers(w, shard[key, 0], shard[key, 1], m, v, f"adamw_{name}")
        else:
            grad = grads[name]
            delta, new_m, new_v = adamw(w, grad, m, v, f"adamw_{name}")
        g_out.append(grad)
        d_out.append(delta)
        m_out.append(new_m)
        v_out.append(new_v)
    return (loss, dx[None], *g_out, *d_out, *m_out, *v_out)
```

```python
import functools

import jax
import jax.numpy as jnp
from jax import lax
from jax.experimental import pallas as pl
from jax.experimental.pallas import tpu as pltpu

F32, BF16 = jnp.float32, jnp.bfloat16
D = 1024
D_FF = 2816
N_SLOT = 4
FF_COLS = 2 * D_FF // N_SLOT
N_MOD = 9
MOD_COLS = N_MOD * D // N_SLOT
POOL_W, ATTN_W, KV_W = 512, 512, 128
PROJ_W = POOL_W + ATTN_W + 2 * KV_W
N_HEADS, Q_GROUP, HEAD = 8, 4, 64
GROUP = 128
POOL_WINDOWS = (2, 4, 8, 16)
BLK = 128
QB = 256
WIN = QB + 2 * BLK
GRID_W = 64
ROPE_BASE = 10000.0
EPS = 1e-6
NEG_INF = -1e30
TM = 256
N_DEV = 8
VMEM_LIMIT_BYTES = 56 * 1024 * 1024
ADAM_LR, ADAM_B1, ADAM_B2, ADAM_EPS, ADAM_WD, ADAM_STEP = 0.001, 0.9, 0.999, 1e-08, 0.01, 10
MESH = pl.DeviceIdType.MESH
NT = (((1,), (1,)), ((), ()))
TN = (((0,), (0,)), ((), ()))


def _params(*sem):
    return pltpu.CompilerParams(dimension_semantics=sem, vmem_limit_bytes=VMEM_LIMIT_BYTES)


def _whole(shape, lead=()):
    idx = tuple(lead) + (0,) * len(shape)
    return pl.BlockSpec((None,) * len(lead) + tuple(shape), lambda *_: idx, pipeline_mode=pl.Buffered(1))


def _rows(cols, tm=TM):
    return pl.BlockSpec((tm, cols), lambda i: (i, 0))


def _mods_spec(layer, n_lat):
    return pl.BlockSpec((None, None, 16, D), lambda i: (layer, (i >= n_lat).astype(jnp.int32), 0, 0))


def _acc_spec(n_lat):
    return pl.BlockSpec((None, 8, D), lambda i: ((i >= n_lat).astype(jnp.int32), 0, 0))


def _dot(a, b):
    return jnp.dot(a, b, preferred_element_type=F32)


def _dotg(a, b, dims):
    return lax.dot_general(a, b, dims, preferred_element_type=F32)


def _sum0(v):
    return jnp.sum(v, axis=0, keepdims=True)


def _norm_mod(h, g, shift, scale):
    r = lax.rsqrt(jnp.mean(h * h, axis=-1, keepdims=True) + EPS)
    xhat = h * r
    y = xhat * g
    return y * (1 + scale) + shift, xhat, r, y


def _norm_mod_bwd(dn, xhat, r, y, g, scale):
    dy = dn * (1 + scale)
    dx = dy * g
    dh = r * (dx - xhat * jnp.mean(dx * xhat, axis=-1, keepdims=True))
    return _sum0(dn), _sum0(dn * y), _sum0(dy * xhat), dh


def _swap_halves(v):
    w = v.shape[1]
    lane = lax.broadcasted_iota(jnp.int32, v.shape, 1)
    return jnp.where(lane % HEAD < HEAD // 2, pltpu.roll(v, w - HEAD // 2, axis=1), pltpu.roll(v, HEAD // 2, axis=1))


def _tile_lanes(t, width):
    return t if width == t.shape[1] else jnp.concatenate([t] * (width // t.shape[1]), axis=1)


def _rope(v, cos, sin):
    return v * _tile_lanes(cos, v.shape[1]) + _swap_halves(v) * _tile_lanes(sin, v.shape[1])


def _unrope(g, cos, sin):
    return g * _tile_lanes(cos, g.shape[1]) + _swap_halves(g * _tile_lanes(sin, g.shape[1]))


def ffn_fwd(h, mods, g, w4, wo, layer, k0, n_lat, name, ex=None):
    s = h.shape[0]

    def body(h_ref, m_ref, g_ref, w_ref, wo_ref, ho_ref, ab_ref, f_ref):
        hh = h_ref[...]
        n, _, _, _ = _norm_mod(hh, g_ref[...], m_ref[k0:k0 + 1, :], m_ref[k0 + 1:k0 + 2, :])
        nb = n.astype(BF16)
        acc = jnp.zeros((TM, D), F32)
        for j in range(2):
            a = _dot(nb, w_ref[j])
            b = _dot(nb, w_ref[2 + j])
            ab_ref[:, j * FF_COLS:(j + 1) * FF_COLS] = a.astype(BF16)
            ab_ref[:, (2 + j) * FF_COLS:(3 + j) * FF_COLS] = b.astype(BF16)
            act = (a * jax.nn.sigmoid(a) * b).astype(BF16)
            acc = acc + _dot(act, wo_ref[j * FF_COLS:(j + 1) * FF_COLS, :])
        f_ref[...] = acc
        ho_ref[...] = hh + 0.5 * m_ref[k0 + 2:k0 + 3, :] * acc

    return _grid_call(
        body, name, s // TM,
        [_rows(D), _mods_spec(layer, n_lat), _whole((1, D), (layer,)), _whole((N_SLOT, D, FF_COLS)), _whole((D_FF, D))],
        [_rows(D), _rows(2 * D_FF), _rows(D)],
        [jax.ShapeDtypeStruct((s, D), F32), jax.ShapeDtypeStruct((s, 2 * D_FF), BF16), jax.ShapeDtypeStruct((s, D), F32)],
        (h, mods, g, w4, wo), "parallel", ex)


def ffn_bwd(h, ab, f, dh, mods, g, w4, wo, layer, k0, n_lat, name, ex=None):
    s = h.shape[0]

    def body(h_ref, ab_ref, f_ref, dh_ref, m_ref, g_ref, w_ref, wo_ref, dhi_ref, dab_ref, df_ref, n_ref, act_ref, dm_ref):
        i = pl.program_id(0)

        @pl.when((i == 0) | (i == n_lat))
        def _():
            dm_ref[...] = jnp.zeros_like(dm_ref)

        hh, dho, gg = h_ref[...], dh_ref[...], g_ref[...]
        scale, gate = m_ref[k0 + 1:k0 + 2, :], m_ref[k0 + 2:k0 + 3, :]
        n, xhat, r, y = _norm_mod(hh, gg, m_ref[k0:k0 + 1, :], scale)
        n_ref[...] = n.astype(BF16)
        dgate = _sum0(dho * (0.5 * f_ref[...]))
        dfb = ((0.5 * gate) * dho).astype(BF16)
        df_ref[...] = dfb
        dn = jnp.zeros((TM, D), F32)
        for j in range(2):
            a = ab_ref[:, j * FF_COLS:(j + 1) * FF_COLS].astype(F32)
            b = ab_ref[:, (2 + j) * FF_COLS:(3 + j) * FF_COLS].astype(F32)
            sg = jax.nn.sigmoid(a)
            sa = a * sg
            act_ref[:, j * FF_COLS:(j + 1) * FF_COLS] = (sa * b).astype(BF16)
            dact = _dotg(dfb, wo_ref[j * FF_COLS:(j + 1) * FF_COLS, :], NT)
            da = (dact * b * (sg * (1 + a * (1 - sg)))).astype(BF16)
            db = (dact * sa).astype(BF16)
            dab_ref[:, j * FF_COLS:(j + 1) * FF_COLS] = da
            dab_ref[:, (2 + j) * FF_COLS:(3 + j) * FF_COLS] = db
            dn = dn + _dotg(da, w_ref[j], NT) + _dotg(db, w_ref[2 + j], NT)
        dsh, dsc, dg, dhn = _norm_mod_bwd(dn, xhat, r, y, gg, scale)
        dhi_ref[...] = dho + dhn
        dm_ref[0:1, :] += dsh
        dm_ref[1:2, :] += dsc
        dm_ref[2:3, :] += dgate
        dm_ref[3:4, :] += dg

    return _grid_call(
        body, name, s // TM,
        [_rows(D), _rows(2 * D_FF), _rows(D), _rows(D), _mods_spec(layer, n_lat), _whole((1, D), (layer,)),
         _whole((N_SLOT, D, FF_COLS)), _whole((D_FF, D))],
        [_rows(D), _rows(2 * D_FF), _rows(D), _rows(D), _rows(D_FF), _acc_spec(n_lat)],
        [jax.ShapeDtypeStruct((s, D), F32), jax.ShapeDtypeStruct((s, 2 * D_FF), BF16), jax.ShapeDtypeStruct((s, D), BF16),
         jax.ShapeDtypeStruct((s, D), BF16), jax.ShapeDtypeStruct((s, D_FF), BF16), jax.ShapeDtypeStruct((2, 8, D), F32)],
        (h, ab, f, dh, mods, g, w4, wo), "arbitrary", ex)


def _token_tile(s, limit=2176):
    return max(ts for ts in range(16, limit + 1, 16) if s % ts == 0)


def wgrad(a, b, tk, tn, slot_cols, name, ex=None):
    s, k = a.shape
    n = b.shape[1]
    ts = _token_tile(s)
    steps = s // ts

    def body(a_ref, b_ref, o_ref, o16_ref):
        r = _dotg(a_ref[...], b_ref[...], TN)
        si = pl.program_id(2)

        @pl.when(si == 0)
        def _():
            o_ref[...] = r

        @pl.when(si > 0)
        def _():
            o_ref[...] += r

        @pl.when(si == steps - 1)
        def _():
            o16_ref[...] = o_ref[...].astype(BF16)

    if slot_cols is None:
        shape, spec = (k, n), pl.BlockSpec((tk, tn), lambda i, j, si: (i, j))
    else:
        per = slot_cols // tn
        shape, spec = (n // slot_cols, k, slot_cols), pl.BlockSpec((None, tk, tn), lambda i, j, si: (lax.div(j, per), i, lax.rem(j, per)))
    return _grid_call(
        body, name, (k // tk, n // tn, steps),
        [pl.BlockSpec((ts, tk), lambda i, j, si: (si, i)), pl.BlockSpec((ts, tn), lambda i, j, si: (si, j))], [spec, spec],
        [jax.ShapeDtypeStruct(shape, F32), jax.ShapeDtypeStruct(shape, BF16)], (a, b), ("parallel", "parallel", "arbitrary"), ex)


def proj_fwd(h, mods, g, w_in, cos, sin, layer, n_lat, name):
    s = h.shape[0]

    def body(h_ref, m_ref, g_ref, w_ref, cos_ref, sin_ref, u_ref, q_ref, k_ref, v_ref):
        n, _, _, _ = _norm_mod(h_ref[...], g_ref[...], m_ref[3:4, :], m_ref[4:5, :])
        p = _dot(n.astype(BF16), w_ref[...])
        cs, sn = cos_ref[...], sin_ref[...]
        u_ref[...] = p[:, :POOL_W]
        q_ref[...] = (_rope(p[:, POOL_W:POOL_W + ATTN_W], cs, sn) * HEAD ** -0.5).astype(BF16)
        k_ref[...] = _rope(p[:, POOL_W + ATTN_W:POOL_W + ATTN_W + KV_W], cs, sn).astype(BF16)
        v_ref[...] = p[:, POOL_W + ATTN_W + KV_W:].astype(BF16)

    return pl.pallas_call(
        body, name=name, grid=(s // TM,),
        in_specs=[_rows(D), _mods_spec(layer, n_lat), _whole((1, D), (layer,)), _whole((D, PROJ_W)),
                  _rows(BLK), _rows(BLK)],
        out_specs=[_rows(POOL_W), _rows(ATTN_W), _rows(KV_W), _rows(KV_W)],
        out_shape=[jax.ShapeDtypeStruct((s, POOL_W), F32), jax.ShapeDtypeStruct((s, ATTN_W), BF16),
                   jax.ShapeDtypeStruct((s, KV_W), BF16), jax.ShapeDtypeStruct((s, KV_W), BF16)],
        compiler_params=_params("parallel"),
    )(h, mods, g, w_in, cos, sin)


def proj_bwd(h, du, dq, dk, dv, dh, mods, g, w_in, cos, sin, layer, n_lat, name):
    s = h.shape[0]

    def body(h_ref, du_ref, dq_ref, dk_ref, dv_ref, dh_ref, m_ref, g_ref, w_ref, cos_ref, sin_ref,
             dhi_ref, dp_ref, n_ref, dm_ref):
        i = pl.program_id(0)

        @pl.when((i == 0) | (i == n_lat))
        def _():
            dm_ref[...] = jnp.zeros_like(dm_ref)

        gg, scale = g_ref[...], m_ref[4:5, :]
        n, xhat, r, y = _norm_mod(h_ref[...], gg, m_ref[3:4, :], scale)
        n_ref[...] = n.astype(BF16)
        cs, sn = cos_ref[...], sin_ref[...]
        dp = jnp.concatenate([du_ref[...], _unrope(dq_ref[...], cs, sn) * HEAD ** -0.5, _unrope(dk_ref[...], cs, sn),
                              dv_ref[...]], axis=1).astype(BF16)
        dp_ref[...] = dp
        dsh, dsc, dg, dhn = _norm_mod_bwd(_dotg(dp, w_ref[...], NT), xhat, r, y, gg, scale)
        dhi_ref[...] = dh_ref[...] + dhn
        dm_ref[0:1, :] += dsh
        dm_ref[1:2, :] += dsc
        dm_ref[3:4, :] += dg

    return pl.pallas_call(
        body, name=name, grid=(s // TM,),
        in_specs=[_rows(D), _rows(POOL_W), _rows(ATTN_W), _rows(KV_W), _rows(KV_W), _rows(D), _mods_spec(layer, n_lat),
                  _whole((1, D), (layer,)), _whole((D, PROJ_W)), _rows(BLK), _rows(BLK)],
        out_specs=[_rows(D), _rows(PROJ_W), _rows(D), _acc_spec(n_lat)],
        out_shape=[jax.ShapeDtypeStruct((s, D), F32), jax.ShapeDtypeStruct((s, PROJ_W), BF16),
                   jax.ShapeDtypeStruct((s, D), BF16), jax.ShapeDtypeStruct((2, 8, D), F32)],
        compiler_params=_params("arbitrary"),
    )(h, du, dq, dk, dv, dh, mods, g, w_in, cos, sin)


def _window(i, s):
    return pl.multiple_of(jnp.clip(i * QB - BLK, 0, s - WIN), BLK)


def mix_tables(t, s):
    n_lat = t // QB
    blocks = jnp.array([0, 1, n_lat - 1] + list(range(n_lat, s // QB)))[:, None, None]
    ws = jnp.clip(blocks * QB - BLK, 0, s - WIN)
    q = blocks * QB + jnp.arange(QB)[None, :, None]
    k = ws + jnp.arange(WIN)[None, None, :]
    is_lat = blocks < n_lat
    local = jnp.where(is_lat & (k < t) & (jnp.abs(k - q) <= BLK), 0.0, NEG_INF).astype(F32)
    bias = jnp.concatenate([local, jnp.zeros(local.shape[:2] + (s - t,), F32)], axis=2)
    seq_lo, seq_hi = jnp.where(is_lat, 0, t), jnp.where(is_lat, t, s)
    bands, counts = [], []
    for w in POOL_WINDOWS:
        lo, hi = jnp.maximum(q - w // 2, seq_lo), jnp.minimum(q + w - w // 2, seq_hi)
        bands.append((k >= lo) & (k < hi))
        counts.append((hi - lo).astype(F32))
    band = jnp.stack(bands, axis=1).astype(BF16)
    count = jnp.concatenate(counts + [jnp.ones(counts[0].shape[:2] + (BLK - len(counts),), F32)], axis=2)
    return dict(bias=bias, band=band, band_t=band.transpose(0, 1, 3, 2), count=count)


def _case_spec(table, n_lat_blk):
    def kind(i):
        return jnp.where(i < n_lat_blk - 1, jnp.minimum(i, 1), i - n_lat_blk + 3)

    shape = table.shape[1:]
    return pl.BlockSpec((None,) + shape, lambda i: (kind(i),) + (0,) * len(shape))


def _split_dot(band, v):
    return _dot(band, v.astype(BF16))


def _pooled(u_ref, band_ref, cnt_ref, i, ws, gi):
    cols = slice(gi * GROUP, (gi + 1) * GROUP)
    mean = _split_dot(band_ref[gi], u_ref[pl.ds(ws, WIN), cols]) / cnt_ref[:, gi:gi + 1]
    return mean - u_ref[pl.ds(pl.multiple_of(i * QB, QB), QB), cols]


def _head_cols(hd):
    return slice(hd * HEAD, (hd + 1) * HEAD)


def _stack_heads(x, hk, first=0):
    return jnp.concatenate([x[:, first + (Q_GROUP * hk + g) * HEAD:first + (Q_GROUP * hk + g + 1) * HEAD]
                            for g in range(Q_GROUP)], axis=0)


def _biased(scores, bias):
    return (scores.reshape(Q_GROUP, QB, -1) + bias).reshape(Q_GROUP * QB, -1)


def _group_column(vals):
    row = lax.broadcasted_iota(jnp.int32, (Q_GROUP * QB, 1), 0)
    out = jnp.full((Q_GROUP * QB, 1), vals[Q_GROUP - 1], F32)
    for g in range(Q_GROUP - 2, -1, -1):
        out = jnp.where(row < (g + 1) * QB, vals[g], out)
    return out


def _lane_place(cols, width=BLK):
    lane = lax.broadcasted_iota(jnp.int32, (cols[0].shape[0], width), 1)
    out = jnp.zeros((cols[0].shape[0], width), F32)
    for hd, c in enumerate(cols):
        out = jnp.where(lane == hd, c, out)
    return out


def mix_fwd(h, q, k, v, u, w_pool, pool_scale, sink, w_out, mods, tables, layer, t, name, ex=None):
    s = h.shape[0]
    n_lat_blk = t // QB

    def body(h_ref, q_ref, k_ref, v_ref, u_ref, wp_ref, ps_ref, sink_ref, wo_ref, m_ref, bias_ref, band_ref, cnt_ref,
             ho_ref, cat_ref, lse_ref, mo_ref):
        i = pl.program_id(0)
        ws = _window(i, s)
        for gi in range(len(POOL_WINDOWS)):
            mixed = _dot(_pooled(u_ref, band_ref, cnt_ref, i, ws, gi).astype(BF16), wp_ref[gi])
            cat_ref[:, gi * GROUP:(gi + 1) * GROUP] = (mixed * ps_ref[:, gi * GROUP:(gi + 1) * GROUP]).astype(BF16)
        bias = bias_ref[...]
        k_all = jnp.concatenate([k_ref[pl.ds(ws, WIN), :], k_ref[t:s, :]], axis=0)
        v_all = jnp.concatenate([v_ref[pl.ds(ws, WIN), :], v_ref[t:s, :]], axis=0)
        lses = []
        for hk in range(N_HEADS // Q_GROUP):
            kv = _head_cols(hk)
            sc = _biased(_dotg(_stack_heads(q_ref[...], hk), k_all[:, kv], NT), bias)
            sk = _group_column([sink_ref[layer, Q_GROUP * hk + g] for g in range(Q_GROUP)])
            m = jnp.maximum(jnp.max(sc, axis=1, keepdims=True), sk)
            e = jnp.exp(sc - m)
            l = jnp.sum(e, axis=1, keepdims=True) + jnp.exp(sk - m)
            o = _dot(e.astype(BF16), v_all[:, kv]) * (1.0 / l)
            lse = m + jnp.log(l)
            for g in range(Q_GROUP):
                hd = Q_GROUP * hk + g
                cat_ref[:, POOL_W + hd * HEAD:POOL_W + (hd + 1) * HEAD] = o[g * QB:(g + 1) * QB].astype(BF16)
                lses.append(lse[g * QB:(g + 1) * QB])
        lse_ref[...] = _lane_place(lses)
        mo = _dot(cat_ref[...], wo_ref[...])
        mo_ref[...] = mo
        ho_ref[...] = h_ref[...] + m_ref[5:6, :] * mo

    blk = lambda cols: _rows(cols, QB)
    return _grid_call(
        body, name, s // QB,
        [blk(D), blk(ATTN_W), _whole((s, KV_W)), _whole((s, KV_W)), _whole((s, POOL_W)),
         _whole((len(POOL_WINDOWS), GROUP, GROUP), (layer,)), _whole((1, POOL_W), (layer,)),
         pl.BlockSpec(memory_space=pltpu.SMEM), _whole((POOL_W + ATTN_W, D)), _mods_spec(layer, n_lat_blk),
         _case_spec(tables["bias"], n_lat_blk), _case_spec(tables["band"], n_lat_blk), _case_spec(tables["count"], n_lat_blk)],
        [blk(D), blk(POOL_W + ATTN_W), blk(BLK), blk(D)],
        [jax.ShapeDtypeStruct((s, D), F32), jax.ShapeDtypeStruct((s, POOL_W + ATTN_W), BF16), jax.ShapeDtypeStruct((s, BLK), F32),
         jax.ShapeDtypeStruct((s, D), F32)],
        (h, q, k, v, u, w_pool, pool_scale, sink, w_out, mods, tables["bias"], tables["band"], tables["count"]), "parallel", ex)


def mix_bwd(dh, mo, q, k, v, u, lse, w_pool, pool_scale, sink, w_out, mods, tables, layer, t, name, ex=None):
    s = dh.shape[0]
    n_lat_blk = t // QB
    n_grp = len(POOL_WINDOWS)

    def body(dh_ref, mo_ref, q_ref, k_ref, v_ref, u_ref, lse_ref, wp_ref, ps_ref, sink_ref, wo_ref, m_ref,
             bias_ref, band_ref, band_t_ref, cnt_ref,
             dq_ref, dk_ref, dv_ref, du_ref, dmo_ref, dwp_ref, dps_ref, dsink_ref, dm_ref):
        i = pl.program_id(0)

        @pl.when(i == 0)
        def _():
            for ref in (dk_ref, dv_ref, du_ref, dwp_ref, dps_ref, dsink_ref):
                ref[...] = jnp.zeros_like(ref)

        @pl.when((i == 0) | (i == n_lat_blk))
        def _():
            dm_ref[...] = jnp.zeros_like(dm_ref)

        ws = _window(i, s)
        here = pl.ds(pl.multiple_of(i * QB, QB), QB)
        dho = dh_ref[...]
        dm_ref[2:3, :] += _sum0(dho * mo_ref[...])
        dmo = (m_ref[5:6, :] * dho).astype(BF16)
        dmo_ref[...] = dmo
        dcat = _dotg(dmo, wo_ref[...], NT)

        for gi in range(n_grp):
            cols = slice(gi * GROUP, (gi + 1) * GROUP)
            pooled = _pooled(u_ref, band_ref, cnt_ref, i, ws, gi).astype(BF16)
            dpo = dcat[:, cols]
            dps_ref[0:1, cols] += _sum0(dpo * _dot(pooled, wp_ref[gi]))
            dmixed = (dpo * ps_ref[:, cols]).astype(BF16)
            dwp_ref[gi] += _dotg(pooled, dmixed, TN)
            dpooled = _dotg(dmixed, wp_ref[gi], NT)
            du_ref[pl.ds(ws, WIN), cols] += _split_dot(band_t_ref[gi], dpooled / cnt_ref[:, gi:gi + 1])
            du_ref[here, cols] -= dpooled

        bias = bias_ref[...]
        k_all = jnp.concatenate([k_ref[pl.ds(ws, WIN), :], k_ref[t:s, :]], axis=0)
        v_all = jnp.concatenate([v_ref[pl.ds(ws, WIN), :], v_ref[t:s, :]], axis=0)
        qq, lse_all = q_ref[...], lse_ref[...]
        dqs, dsinks, dks, dvs = [], [], [], []
        for hk in range(N_HEADS // Q_GROUP):
            kv = _head_cols(hk)
            q4 = _stack_heads(qq, hk)
            lse = jnp.concatenate([lse_all[:, Q_GROUP * hk + g:Q_GROUP * hk + g + 1] for g in range(Q_GROUP)], axis=0)
            p = jnp.exp(_biased(_dotg(q4, k_all[:, kv], NT), bias) - lse)
            do = _stack_heads(dcat, hk, POOL_W).astype(BF16)
            dp = _dotg(do, v_all[:, kv], NT)
            delta = jnp.sum(p * dp, axis=1, keepdims=True)
            ds = (p * (dp - delta)).astype(BF16)
            sk = _group_column([sink_ref[layer, Q_GROUP * hk + g] for g in range(Q_GROUP)])
            dsk = -jnp.exp(sk - lse) * delta
            dq = _dot(ds, k_all[:, kv])
            for g in range(Q_GROUP):
                dqs.append(dq[g * QB:(g + 1) * QB])
                dsinks.append(_sum0(dsk[g * QB:(g + 1) * QB]))
            dks.append(_dotg(ds, q4, TN))
            dvs.append(_dotg(p.astype(BF16), do, TN))
        dq_ref[...] = jnp.concatenate(dqs, axis=1)
        dk, dv = jnp.concatenate(dks, axis=1), jnp.concatenate(dvs, axis=1)
        dk_ref[pl.ds(ws, WIN), :] += dk[:WIN]
        dv_ref[pl.ds(ws, WIN), :] += dv[:WIN]
        dk_ref[t:s, :] += dk[WIN:]
        dv_ref[t:s, :] += dv[WIN:]
        dsink_ref[0:1, :] += _lane_place(dsinks)

    blk = lambda cols: _rows(cols, QB)
    full = lambda shape: pl.BlockSpec(shape, lambda i: (0,) * len(shape))
    return _grid_call(
        body, name, s // QB,
        [blk(D), blk(D), blk(ATTN_W), _whole((s, KV_W)), _whole((s, KV_W)), _whole((s, POOL_W)),
         blk(BLK), _whole((n_grp, GROUP, GROUP), (layer,)), _whole((1, POOL_W), (layer,)),
         pl.BlockSpec(memory_space=pltpu.SMEM), _whole((POOL_W + ATTN_W, D)), _mods_spec(layer, n_lat_blk)]
        + [_case_spec(tables[key], n_lat_blk) for key in ("bias", "band", "band_t", "count")],
        [blk(ATTN_W), full((s, KV_W)), full((s, KV_W)), full((s, POOL_W)), blk(D),
         full((n_grp, GROUP, GROUP)), full((8, POOL_W)), full((8, BLK)), _acc_spec(n_lat_blk)],
        [jax.ShapeDtypeStruct((s, ATTN_W), F32), jax.ShapeDtypeStruct((s, KV_W), F32),
         jax.ShapeDtypeStruct((s, KV_W), F32), jax.ShapeDtypeStruct((s, POOL_W), F32),
         jax.ShapeDtypeStruct((s, D), BF16), jax.ShapeDtypeStruct((n_grp, GROUP, GROUP), F32),
         jax.ShapeDtypeStruct((8, POOL_W), F32), jax.ShapeDtypeStruct((8, BLK), F32), jax.ShapeDtypeStruct((2, 8, D), F32)],
        (dh, mo, q, k, v, u, lse, w_pool, pool_scale, sink, w_out, mods, tables["bias"], tables["band"], tables["band_t"],
         tables["count"]), "arbitrary", ex)


def loss_head(h, target, g, t, name):
    s = h.shape[0]
    n_lat = t // TM

    def body(h_ref, t_ref, g_ref, dh_ref, acc_ref):
        i = pl.program_id(0)

        @pl.when(i == 0)
        def _():
            acc_ref[...] = jnp.zeros_like(acc_ref)

        @pl.when(i < n_lat)
        def _():
            hh, gg = h_ref[...], g_ref[...]
            r = lax.rsqrt(jnp.mean(hh * hh, axis=-1, keepdims=True) + EPS)
            xhat = hh * r
            err = xhat * gg - t_ref[...]
            dy = err * (1.0 / D)
            dx = dy * gg
            dh_ref[...] = r * (dx - xhat * jnp.mean(dx * xhat, axis=-1, keepdims=True))
            acc_ref[0:1, :] += _sum0(dy * xhat)
            acc_ref[1:2, :] += _sum0(err * err)

        @pl.when(i >= n_lat)
        def _():
            dh_ref[...] = jnp.zeros_like(dh_ref)

    return pl.pallas_call(
        body, name=name, grid=(s // TM,),
        in_specs=[_rows(D), pl.BlockSpec((TM, D), lambda i: (jnp.minimum(i, n_lat - 1), 0)), _whole((1, D))],
        out_specs=[_rows(D), pl.BlockSpec((8, D), lambda i: (0, 0))],
        out_shape=[jax.ShapeDtypeStruct((s, D), F32), jax.ShapeDtypeStruct((8, D), F32)],
        compiler_params=_params("arbitrary"),
    )(h, target, g)


def mod_rows(c_all, w_mod, b_cols, name):
    def body(c_ref, w_ref, b_ref, o_ref):
        cc = c_ref[...]
        o_ref[...] = _dot((cc * jax.nn.sigmoid(cc)).astype(BF16), w_ref[...].astype(BF16)) + b_ref[...]

    return pl.pallas_call(
        body, name=name, grid=(2,),
        in_specs=[pl.BlockSpec((16, D), lambda l: (0, 0)), pl.BlockSpec((None, D, MOD_COLS), lambda l: (l, 0, 0)),
                  pl.BlockSpec((None, 1, MOD_COLS), lambda l: (l, 0, 0))],
        out_specs=pl.BlockSpec((None, 16, MOD_COLS), lambda l: (l, 0, 0)),
        out_shape=jax.ShapeDtypeStruct((2, 16, MOD_COLS), F32),
        compiler_params=_params("parallel"),
    )(c_all, w_mod, b_cols)


def mod_grads(c_all, dmod_cols, w_mod, name):
    def body(c_ref, d_ref, w_ref, dw_ref, dc_ref):
        @pl.when(pl.program_id(0) == 0)
        def _():
            dc_ref[...] = jnp.zeros_like(dc_ref)

        cc = c_ref[...]
        dd = d_ref[...].astype(BF16)
        dw_ref[...] = _dotg((cc * jax.nn.sigmoid(cc)).astype(BF16), dd, TN)
        dc_ref[...] += _dotg(dd, w_ref[...].astype(BF16), NT)

    return pl.pallas_call(
        body, name=name, grid=(2,),
        in_specs=[pl.BlockSpec((16, D), lambda l: (0, 0)), pl.BlockSpec((None, 16, MOD_COLS), lambda l: (l, 0, 0)),
                  pl.BlockSpec((None, D, MOD_COLS), lambda l: (l, 0, 0))],
        out_specs=[pl.BlockSpec((None, D, MOD_COLS), lambda l: (l, 0, 0)), pl.BlockSpec((16, D), lambda l: (0, 0))],
        out_shape=[jax.ShapeDtypeStruct((2, D, MOD_COLS), F32), jax.ShapeDtypeStruct((16, D), F32)],
        compiler_params=_params("arbitrary"),
    )(c_all, dmod_cols, w_mod)


def _row_tile(rows, cols, n_arrays):
    budget = VMEM_LIMIT_BYTES // 4 // (2 * 4 * n_arrays * cols)
    best = None
    for tr in range(16, rows + 1, 16):
        if rows % tr == 0 and tr <= budget:
            best = tr
    return best if best is not None else rows


def elementwise(fn, ins, out_dtypes, name, ex=None):
    rows, cols = ins[0].shape
    tr = _row_tile(rows, cols, len(ins) + len(out_dtypes))

    def body(*refs):
        outs = fn(*[r[...] for r in refs[:len(ins)]])
        for o_ref, o in zip(refs[len(ins):], outs):
            o_ref[...] = o.astype(o_ref.dtype)

    spec = pl.BlockSpec((tr, cols), lambda i: (i, 0))
    outs, got = _grid_call(body, name, rows // tr, [spec] * len(ins), [spec] * len(out_dtypes),
                           [jax.ShapeDtypeStruct((rows, cols), dt) for dt in out_dtypes], ins, "parallel", ex)
    return outs if ex is None else (outs, got)


def _adamw_tile(w, g, m, v):
    m = ADAM_B1 * m + (1.0 - ADAM_B1) * g
    v = ADAM_B2 * v + (1.0 - ADAM_B2) * (g * g)
    m_hat = m / (1.0 - ADAM_B1 ** ADAM_STEP)
    v_hat = v / (1.0 - ADAM_B2 ** ADAM_STEP)
    return -ADAM_LR * (m_hat / (jnp.sqrt(v_hat) + ADAM_EPS) + ADAM_WD * w), m, v


def adamw(w, g, m, v, name, ex=None):
    shape = w.shape
    two_d = (-1, shape[-1]) if w.ndim > 1 else (1, -1)
    outs = elementwise(_adamw_tile, [a.reshape(two_d) for a in (w, g, m, v)], [F32] * 3, name, ex)
    outs, got = outs if ex is not None else (outs, None)
    outs = [o.reshape(shape) for o in outs]
    return outs if ex is None else (outs, got)


def _prefetch_call(body, name, grid, in_specs, out_specs, out_shape, place, args, aliases=None):
    spec = pltpu.PrefetchScalarGridSpec(num_scalar_prefetch=1, grid=grid, in_specs=in_specs, out_specs=out_specs)
    return pl.pallas_call(body, name=name, grid_spec=spec, out_shape=out_shape, input_output_aliases=aliases or {},
                          compiler_params=_params(*["parallel"] * len(grid)))(place, *args)


def cast_place(w, layer, place, name):
    _, r, c = w.shape
    tr = _row_tile(r, c, 2)

    def body(p_ref, w_ref, o_ref):
        o_ref[...] = w_ref[...].astype(BF16)

    return _prefetch_call(
        body, name, (r // tr,), [pl.BlockSpec((None, tr, c), lambda i, p: (layer, i, 0))],
        pl.BlockSpec((None, tr, c), lambda i, p: (p[1], i, 0)), jax.ShapeDtypeStruct((N_SLOT, r, c), BF16), place, [w])


def pair_sum(g32, got, place, name):
    n_slot, rh, c = got.shape
    tr = _row_tile(rh, c, 4)
    per = rh // tr

    def body(p_ref, a_ref, b_ref, o_ref, o16_ref):
        r = a_ref[...] + b_ref[...].astype(F32)
        o_ref[...] = r
        o16_ref[...] = r.astype(BF16)

    half = pl.BlockSpec((None, tr, c), lambda s, i, p: (s, i, 0))
    return _prefetch_call(
        body, name, (n_slot, per), [pl.BlockSpec((None, tr, c), lambda s, i, p: (s, p[0] * per + i, 0)), half], [half, half],
        [jax.ShapeDtypeStruct(got.shape, F32), jax.ShapeDtypeStruct(got.shape, BF16)], place, [g32, got])


def chip_sum(p32, got, place, name):
    _, rh, c = p32.shape
    tr = _row_tile(rh, c, 5)
    per = rh // tr

    def body(p_ref, m_ref, r0_ref, r1_ref, r2_ref, o_ref):
        o_ref[...] = m_ref[...] + r0_ref[...].astype(F32) + r1_ref[...].astype(F32) + r2_ref[...].astype(F32)

    part = pl.BlockSpec((tr, c), lambda i, p: (i, 0))
    return _prefetch_call(
        body, name, (per,), [pl.BlockSpec((None, tr, c), lambda i, p: (p[1], i, 0)), part, part, part],
        pl.BlockSpec((tr, c), lambda i, p: (p[0] * per + i, 0)), jax.ShapeDtypeStruct((2 * rh, c), F32), place, [p32, *got])


def adamw_layers(w, g0, g1, m, v, name, ex=None):
    _, r, c = w.shape
    tr = _row_tile(r, c, 10)

    def body(w_ref, g0_ref, g1_ref, m_ref, v_ref, g_ref, d_ref, mo_ref, vo_ref):
        g = jnp.where(pl.program_id(0) == 0, g0_ref[...], g1_ref[...])
        g_ref[...] = g
        d_ref[...], mo_ref[...], vo_ref[...] = _adamw_tile(w_ref[...], g, m_ref[...], v_ref[...])

    stacked = pl.BlockSpec((None, tr, c), lambda l, i: (l, i, 0))
    layer = pl.BlockSpec((tr, c), lambda l, i: (i, 0))
    outs, got = _grid_call(body, name, (2, r // tr), [stacked, layer, layer, stacked, stacked], [stacked] * 4,
                           [jax.ShapeDtypeStruct(w.shape, F32)] * 4, (w, g0, g1, m, v), "parallel", ex)
    return outs if ex is None else (outs, got)


def sum8(gathered, name):
    def body(*refs):
        n = len(refs) // 2
        for g_ref, o_ref in zip(refs[:n], refs[n:]):
            acc = g_ref[0]
            for dev in range(1, N_DEV):
                acc = acc + g_ref[dev]
            o_ref[...] = acc

    return pl.pallas_call(
        body, name=name,
        out_shape=[jax.ShapeDtypeStruct(a.shape[1:], F32) for a in gathered],
        compiler_params=_params(),
    )(*gathered)


def _place():
    return lax.axis_index("x"), lax.axis_index("y"), lax.axis_index("c")


def _any(n):
    return [pl.BlockSpec(memory_space=pl.ANY)] * n


def gather8_exchange(blocks):
    n = len(blocks)

    def copy(outs, sems, ti, k, block, to, src=None):
        dst = outs[ti].at[4 * block[0] + 2 * block[1] + block[2]]
        return pltpu.make_async_remote_copy(src_ref=dst if src is None else src, dst_ref=dst, send_sem=sems[0].at[ti, k],
                                            recv_sem=sems[1].at[ti, k], device_id=to, device_id_type=MESH)

    def first(ins, outs, sems):
        x, y, c = _place()
        local, sent = [], []
        for ti in range(n):
            local.append(pltpu.make_async_copy(ins[ti], outs[ti].at[4 * x + 2 * y + c], sems[2].at[ti]))
            sent.append(copy(outs, sems, ti, 0, (x, y, c), (x, y, 1 - c), src=ins[ti]))
            sent += [copy(outs, sems, ti, 1 + j, (x, y, c), (*chip, c), src=ins[ti]) for j, chip in enumerate(_three_chips(x, y))]
        return local, sent

    def start(ins, outs, sems):
        local, sent = first(ins, outs, sems)
        for cp in local + sent:
            cp.start()

    def finish(ins, outs, sems):
        x, y, c = _place()
        me, sibling, chips = (x, y, c), (x, y, 1 - c), _three_chips(x, y)
        local, sent = first(ins, outs, sems)
        for ti in range(n):
            for j, chip in enumerate(chips):
                copy(outs, sems, ti, 1 + j, (*chip, c), me).wait_recv()
                sent.append(copy(outs, sems, ti, 4 + j, (*chip, c), sibling))
                sent[-1].start()
        for ti in range(n):
            copy(outs, sems, ti, 0, sibling, me).wait_recv()
            for j, chip in enumerate(chips):
                copy(outs, sems, ti, 4 + j, (*chip, 1 - c), me).wait_recv()
        for cp in sent:
            cp.wait_send()
        for cp in local:
            cp.wait()

    return dict(ins=list(blocks), out_shape=[jax.ShapeDtypeStruct((N_DEV,) + b.shape, b.dtype) for b in blocks], aliases={},
                start=start, finish=finish,
                scratch=[pltpu.SemaphoreType.DMA((n, 7)), pltpu.SemaphoreType.DMA((n, 7)), pltpu.SemaphoreType.DMA((n,))])


def all_gather(blocks, name):
    return run_exchange(gather8_exchange(blocks), name)


def _three_chips(x, y):
    return [(1 - x, y), (x, 1 - y), (1 - x, 1 - y)]


def gather_exchange(placed):
    n = len(placed)

    def copy(bufs, sems, ti, k, chip, core, to):
        rh = bufs[ti].shape[1] // 2
        half = bufs[ti].at[2 * chip[0] + chip[1], pl.ds(core * rh, rh), :]
        return pltpu.make_async_remote_copy(src_ref=half, dst_ref=half, send_sem=sems[0].at[ti, k], recv_sem=sems[1].at[ti, k],
                                            device_id=to, device_id_type=MESH)

    def sends(bufs, sems):
        x, y, c = _place()
        return [copy(bufs, sems, ti, k, (x, y), c, (*chip, c)) for ti in range(n) for k, chip in enumerate(_three_chips(x, y))]

    def start(ins, bufs, sems):
        for cp in sends(bufs, sems):
            cp.start()

    def finish(ins, bufs, sems):
        x, y, c = _place()
        chips = _three_chips(x, y)
        passed = []
        for ti in range(n):
            for k, chip in enumerate(chips):
                copy(bufs, sems, ti, k, chip, c, (x, y, c)).wait_recv()
                passed.append(copy(bufs, sems, ti, 3 + k, chip, c, (x, y, 1 - c)))
                passed[-1].start()
        for ti in range(n):
            for k, chip in enumerate(chips):
                copy(bufs, sems, ti, 3 + k, chip, 1 - c, (x, y, c)).wait_recv()
        for cp in sends(bufs, sems) + passed:
            cp.wait_send()

    return dict(ins=list(placed), out_shape=[jax.ShapeDtypeStruct(w.shape, w.dtype) for w in placed],
                aliases={i: i for i in range(n)}, start=start, finish=finish,
                scratch=[pltpu.SemaphoreType.DMA((n, 6)), pltpu.SemaphoreType.DMA((n, 6))])


def scatter_exchange(p16):
    n = len(p16)

    def copies(ins, got, sems):
        x, y, c = _place()
        return [pltpu.make_async_remote_copy(src_ref=ins[ti].at[2 * chip[0] + chip[1]], dst_ref=got[3 * ti + k],
                                             send_sem=sems[0].at[ti, k], recv_sem=sems[1].at[ti, k], device_id=(*chip, c),
                                             device_id_type=MESH)
                for ti in range(n) for k, chip in enumerate(_three_chips(x, y))]

    def start(ins, got, sems):
        for cp in copies(ins, got, sems):
            cp.start()

    def finish(ins, got, sems):
        for cp in copies(ins, got, sems):
            cp.wait()

    return dict(ins=list(p16), out_shape=[jax.ShapeDtypeStruct(a.shape[1:], BF16) for a in p16 for _ in range(3)], aliases={},
                start=start, finish=finish, scratch=[pltpu.SemaphoreType.DMA((n, 3)), pltpu.SemaphoreType.DMA((n, 3))])


def run_exchange(ex, name):
    ci, co = len(ex["ins"]), len(ex["out_shape"])

    def body(*refs):
        ins, outs, sems = refs[:ci], refs[ci:ci + co], refs[ci + co:]
        ex["start"](ins, outs, sems)
        ex["finish"](ins, outs, sems)

    return pl.pallas_call(body, name=name, in_specs=_any(ci), out_specs=_any(co), out_shape=ex["out_shape"],
                          input_output_aliases=ex["aliases"], scratch_shapes=ex["scratch"])(*ex["ins"])


def _grid_call(body, name, grid, in_specs, out_specs, out_shape, args, sem, ex=None):
    grid = (grid,) if isinstance(grid, int) else tuple(grid)
    sems_of = (sem,) * len(grid) if isinstance(sem, str) else tuple(sem)
    n_in, n_out = len(in_specs), len(out_specs)
    if ex is None:
        return pl.pallas_call(body, name=name, grid=grid, in_specs=in_specs, out_specs=out_specs, out_shape=out_shape,
                              compiler_params=_params(*sems_of))(*args), []
    ci, co = len(ex["ins"]), len(ex["out_shape"])

    def at(step):
        where = [pl.program_id(ax) == (0 if step == "first" else grid[ax] - 1) for ax in range(len(grid))]
        return functools.reduce(jnp.logical_and, where)

    def carrying(*refs):
        c_in, c_out = refs[n_in:n_in + ci], refs[n_in + ci + n_out:n_in + ci + n_out + co]
        sems = refs[n_in + ci + n_out + co:]

        @pl.when(at("first"))
        def _():
            ex["start"](c_in, c_out, sems)

        body(*refs[:n_in], *refs[n_in + ci:n_in + ci + n_out])

        @pl.when(at("last"))
        def _():
            ex["finish"](c_in, c_out, sems)

    outs = pl.pallas_call(
        carrying, name=name, grid=grid, in_specs=list(in_specs) + _any(ci), out_specs=list(out_specs) + _any(co),
        out_shape=list(out_shape) + ex["out_shape"], scratch_shapes=ex["scratch"],
        input_output_aliases={n_in + i: n_out + j for i, j in ex["aliases"].items()},
        compiler_params=_params(*["arbitrary"] * len(grid)),
    )(*args, *ex["ins"])
    return outs[:n_out], outs[n_out:]


def both(*exchanges):
    exchanges = [ex for ex in exchanges if ex is not None]
    if len(exchanges) < 2:
        return exchanges[0] if exchanges else None
    n_ins = [len(ex["ins"]) for ex in exchanges]
    n_outs = [len(ex["out_shape"]) for ex in exchanges]
    n_sems = [len(ex["scratch"]) for ex in exchanges]

    def parts(seq, counts, k):
        first = sum(counts[:k])
        return seq[first:first + counts[k]]

    def run(phase):
        def go(ins, outs, sems):
            for k, ex in enumerate(exchanges):
                ex[phase](parts(ins, n_ins, k), parts(outs, n_outs, k), parts(sems, n_sems, k))
        return go

    aliases = {sum(n_ins[:k]) + i: sum(n_outs[:k]) + j for k, ex in enumerate(exchanges) for i, j in ex["aliases"].items()}
    return dict(ins=[a for ex in exchanges for a in ex["ins"]], out_shape=[o for ex in exchanges for o in ex["out_shape"]],
                aliases=aliases, start=run("start"), finish=run("finish"), scratch=[s for ex in exchanges for s in ex["scratch"]])


def split_outputs(got, *exchanges):
    got, out = list(got), []
    for ex in exchanges:
        n = len(ex["out_shape"]) if ex is not None else 0
        out.append(got[:n])
        got = got[n:]
    return out


def pair_exchange(g16):
    n = len(g16)

    def copies(a16, got, sems):
        x, y, c = _place()
        out = []
        for ti in range(n):
            rh = a16[ti].shape[1] // 2
            out.append(pltpu.make_async_remote_copy(
                src_ref=a16[ti].at[:, pl.ds((1 - c) * rh, rh), :], dst_ref=got[ti], send_sem=sems[0].at[ti],
                recv_sem=sems[1].at[ti], device_id=(x, y, 1 - c), device_id_type=MESH))
        return out

    def start(a16, got, sems):
        for cp in copies(a16, got, sems):
            cp.start()

    def finish(a16, got, sems):
        for cp in copies(a16, got, sems):
            cp.wait()

    return dict(ins=list(g16), out_shape=[jax.ShapeDtypeStruct((a.shape[0], a.shape[1] // 2, a.shape[2]), BF16) for a in g16],
                aliases={}, start=start, finish=finish, scratch=[pltpu.SemaphoreType.DMA((n,)), pltpu.SemaphoreType.DMA((n,))])


def pair_gather(halves, name):
    n = len(halves)

    def body(*refs):
        bufs = refs[n:2 * n]
        send_sems, recv_sems = refs[2 * n:]
        x, y, c = _place()
        copies = []
        for ti in range(n):
            rh = bufs[ti].shape[0] // 2
            rows = bufs[ti].at[pl.ds(c * rh, rh), :]
            copies.append(pltpu.make_async_remote_copy(src_ref=rows, dst_ref=rows, send_sem=send_sems.at[ti],
                                                       recv_sem=recv_sems.at[ti], device_id=(x, y, 1 - c), device_id_type=MESH))
        for cp in copies:
            cp.start()
        for ti, cp in enumerate(copies):
            cp.wait_send()
            rh = bufs[ti].shape[0] // 2
            theirs = bufs[ti].at[pl.ds((1 - c) * rh, rh), :]
            pltpu.make_async_remote_copy(src_ref=theirs, dst_ref=theirs, send_sem=send_sems.at[ti], recv_sem=recv_sems.at[ti],
                                         device_id=(x, y, 1 - c), device_id_type=MESH).wait_recv()

    return pl.pallas_call(
        body, name=name, in_specs=_any(n), out_specs=_any(n), input_output_aliases={i: i for i in range(n)},
        out_shape=[jax.ShapeDtypeStruct(a.shape, a.dtype) for a in halves],
        scratch_shapes=[pltpu.SemaphoreType.DMA((n,)), pltpu.SemaphoreType.DMA((n,))],
    )(*halves)


def reduce_small(dm_f1, dm_mix, dm_gate, dm_f2, loss_blk, name):
    def body(f1_ref, mix_ref, gate_ref, f2_ref, l_ref, tot_ref, rows_ref, fin_ref):
        rows_ref[...] = jnp.zeros_like(rows_ref)
        tot_ref[...] = jnp.zeros_like(tot_ref)
        mod_src = [(f1_ref, 0), (f1_ref, 1), (f1_ref, 2), (mix_ref, 0), (mix_ref, 1), (gate_ref, 2),
                   (f2_ref, 0), (f2_ref, 1), (f2_ref, 2)]
        norm_src = [(f1_ref, 3), (mix_ref, 3), (f2_ref, 3)]
        for l in range(2):
            for k, (ref, r) in enumerate(mod_src + norm_src):
                lat = ref[0, l, 0, r:r + 1, :]
                ctx = ref[0, l, 1, r:r + 1, :]
                for dev in range(N_DEV):
                    if dev:
                        lat = lat + ref[dev, l, 0, r:r + 1, :]
                        ctx = ctx + ref[dev, l, 1, r:r + 1, :]
                    if k < N_MOD:
                        rows_ref[l, dev, k:k + 1, :] = ref[dev, l, 0, r:r + 1, :]
                if k < N_MOD:
                    rows_ref[l, N_DEV, k:k + 1, :] = ctx
                tot_ref[l, k:k + 1, :] = lat + ctx
        acc = l_ref[0]
        for dev in range(1, N_DEV):
            acc = acc + l_ref[dev]
        loss = (0.5 / D) * jnp.sum(acc[1:2, :], axis=1, keepdims=True)
        row = lax.broadcasted_iota(jnp.int32, (8, D), 0)
        fin_ref[...] = jnp.where(row == 0, acc[0:1, :], loss)

    return pl.pallas_call(
        body, name=name,
        out_shape=[jax.ShapeDtypeStruct((2, 16, D), F32), jax.ShapeDtypeStruct((2, 16, 16, D), F32),
                   jax.ShapeDtypeStruct((8, D), F32)],
        compiler_params=_params(),
    )(dm_f1, dm_mix, dm_gate, dm_f2, loss_blk)


def rope_tables(t, s):
    rows = t // GRID_W
    row = jnp.repeat(jnp.arange(rows), GRID_W).astype(F32)
    col = jnp.tile(jnp.arange(GRID_W), rows).astype(F32)
    inv = ROPE_BASE ** (-jnp.arange(0, HEAD // 2, 2, dtype=F32) / (HEAD // 2))
    ang = jnp.concatenate([row[:, None] * inv, col[:, None] * inv], axis=-1)
    cos, sin = jnp.cos(ang), jnp.sin(ang)
    cos = jnp.concatenate([jnp.tile(cos, (1, 4)), jnp.ones((s - t, BLK), F32)], axis=0)
    sin = jnp.concatenate([jnp.tile(jnp.concatenate([-sin, sin], axis=1), (1, 2)), jnp.zeros((s - t, BLK), F32)], axis=0)
    return cos, sin


BIG = ("ffn1_in", "ffn1_out", "w_in", "w_out", "ffn2_in", "ffn2_out")
GROUPS = dict(ffn1=("ffn1_in", "ffn1_out"), mix=("w_in", "w_out"), ffn2=("ffn2_in", "ffn2_out"))
GATHER_BEHIND = {("ffn1", 0): [("w_in", 0), ("w_out", 0), ("ffn2_out", 0), ("ffn1_out", 1)], ("mix", 0): [("ffn2_in", 0)],
                 ("ffn2", 0): [("ffn1_in", 1), ("w_in", 1), ("w_out", 1)], ("ffn1", 1): [("ffn2_in", 1)],
                 ("mix", 1): [("ffn2_out", 1)]}


def _slot_major(name, g):
    if name == "w_in":
        return jnp.stack(jnp.split(g, N_SLOT, axis=1), axis=0)
    if name in ("ffn1_in", "ffn2_in"):
        return g
    return g.reshape(N_SLOT, g.shape[0] // N_SLOT, g.shape[1])


def _whole_weight(name, buf):
    if name == "w_in":
        return buf.transpose(1, 0, 2).reshape(D, PROJ_W)
    if name in ("ffn1_in", "ffn2_in"):
        return buf
    return buf.reshape(-1, buf.shape[2])


def local_step(x1, ctx1, target, mods, norms, nfinal, placed, w_pool, pool_scale, sink, place, small_blocks):
    t, s = x1.shape[0], x1.shape[0] + ctx1.shape[0]
    n_lat = t // TM
    cos, sin = rope_tables(t, s)
    tables = mix_tables(t, s)
    wts ={name: list(pair) for name, pair in placed.items()}

    def gather(tensors):
        return gather_exchange([wts[name][l] for name, l in tensors])

    def gathered(tensors, arrays):
        for (name, l), whole in zip(tensors, arrays):
            wts[name][l] = whole

    def weight(name, l):
        return _whole_weight(name, wts[name][l])

    def fwd_ex(grp, l):
        groups = GATHER_BEHIND.get((grp, l))
        return (groups, gather(groups)) if groups else (None, None)

    first = [("ffn1_in", 0), ("ffn1_out", 0)]
    gathered(first, run_exchange(gather(first), "gather_first"))
    h = jnp.concatenate([x1, ctx1], axis=0)
    saved = []
    for l in range(2):
        h0 = h
        groups, ex = fwd_ex("ffn1", l)
        (h1, ab1, f1), got = ffn_fwd(h0, mods, norms[0], weight("ffn1_in", l), weight("ffn1_out", l), l, 0, n_lat, f"ffn1_fwd_{l}", ex)
        gathered(groups or [], got)
        u, q, k, v = proj_fwd(h1, mods, norms[1], weight("w_in", l), cos, sin, l, n_lat, f"proj_fwd_{l}")
        groups, ex = fwd_ex("mix", l)
        (h2, cat, lse, mo), got = mix_fwd(h1, q, k, v, u, w_pool, pool_scale, sink, weight("w_out", l), mods, tables, l, t,
                                          f"mix_fwd_{l}", ex)
        gathered(groups or [], got)
        groups, ex = fwd_ex("ffn2", l)
        (h, ab2, f2), got = ffn_fwd(h2, mods, norms[2], weight("ffn2_in", l), weight("ffn2_out", l), l, 6, n_lat, f"ffn2_fwd_{l}", ex)
        gathered(groups or [], got)
        saved.append((h0, ab1, f1, h1, u, q, k, v, cat, lse, mo, h2, ab2, f2))
    dh, loss_blk = loss_head(h, target, nfinal, t, "loss_head")

    halves = {name: [None, None] for name in BIG}
    pending = []

    def summed_in_pair(grp, l, name_a, g_a, name_b, wgrad_b):
        g_b, got_a = wgrad_b(pair_exchange([_slot_major(name_a, g_a[1])]))
        got_b = run_exchange(pair_exchange([_slot_major(name_b, g_b[1])]), f"pair_exchange_{name_b}_{l}")
        by = {name_a: (g_a[0], got_a[0]), name_b: (g_b[0], got_b[0])}
        pending.append((grp, l, [pair_sum(_slot_major(n, by[n][0]), by[n][1], place, f"pair_sum_{n}_{l}") for n in GROUPS[grp]]))

    def scatter():
        return scatter_exchange([p16 for _, p16 in pending[0][2]]) if pending else None

    def scattered(got):
        if pending:
            grp, l, pairs = pending.pop(0)
            for i, name in enumerate(GROUPS[grp]):
                halves[name][l] = chip_sum(pairs[i][0], got[3 * i:3 * i + 3], place, f"chip_sum_{name}_{l}")

    small = [None, None]
    for l in (1, 0):
        h0, ab1, f1, h1, u, q, k, v, cat, lse, mo, h2, ab2, f2 = saved[l]
        (dh, dab, df, n, act, dm_f2), got = ffn_bwd(h2, ab2, f2, dh, mods, norms[2], weight("ffn2_in", l), weight("ffn2_out", l),
                                                    l, 6, n_lat, f"ffn2_bwd_{l}", scatter())
        scattered(got)
        g_in, _ = wgrad(n, dab, D, FF_COLS, FF_COLS, f"ffn2_in_wgrad_{l}")
        summed_in_pair("ffn2", l, "ffn2_in", g_in, "ffn2_out",
                       lambda ex, a=act, b=df: wgrad(a, b, D_FF // 2, D, None, f"ffn2_out_wgrad_{l}", ex))
        (dq, dk, dv, du, dmo, dwp, dps, dsink, dm_gate), got = mix_bwd(
            dh, mo, q, k, v, u, lse, w_pool, pool_scale, sink, weight("w_out", l), mods, tables, l, t, f"mix_bwd_{l}", scatter())
        scattered(got)
        g_wo, _ = wgrad(cat, dmo, POOL_W + ATTN_W, D, None, f"w_out_wgrad_{l}")
        dh, dp, n, dm_mix = proj_bwd(h1, du, dq, dk, dv, dh, mods, norms[1], weight("w_in", l), cos, sin, l, n_lat, f"proj_bwd_{l}")
        summed_in_pair("mix", l, "w_out", g_wo, "w_in",
                       lambda ex, a=n, b=dp: wgrad(a, b, D, PROJ_W // 2, None, f"w_in_wgrad_{l}", ex))
        (dh, dab, df, n, act, dm_f1), got = ffn_bwd(h0, ab1, f1, dh, mods, norms[0], weight("ffn1_in", l), weight("ffn1_out", l),
                                                    l, 0, n_lat, f"ffn1_bwd_{l}", scatter())
        scattered(got)
        small[l] = dict(dm_f1=dm_f1, dm_mix=dm_mix, dm_gate=dm_gate, dm_f2=dm_f2, dwp=dwp, dps=dps, dsink=dsink)
        if l:
            g_in, _ = wgrad(n, dab, D, FF_COLS, FF_COLS, f"ffn1_in_wgrad_{l}")
            summed_in_pair("ffn1", l, "ffn1_in", g_in, "ffn1_out",
                           lambda ex, a=act, b=df: wgrad(a, b, D_FF // 2, D, None, f"ffn1_out_wgrad_{l}", ex))
    g_out, small_all = wgrad(act, df, D_FF // 2, D, None, "ffn1_out_wgrad_0", gather8_exchange(small_blocks(small, loss_blk)))
    got = run_exchange(pair_exchange([_slot_major("ffn1_out", g_out[1])]), "pair_exchange_ffn1_out_0")
    p32, p16 = pair_sum(_slot_major("ffn1_out", g_out[0]), got[0], place, "pair_sum_ffn1_out_0")
    g_in, got = wgrad(n, dab, D, FF_COLS, FF_COLS, "ffn1_in_wgrad_0", scatter_exchange([p16]))
    halves["ffn1_out"][0] = chip_sum(p32, got, place, "chip_sum_ffn1_out_0")
    got = run_exchange(pair_exchange([_slot_major("ffn1_in", g_in[1])]), "pair_exchange_ffn1_in_0")
    return dh[:t], halves, pair_sum(_slot_major("ffn1_in", g_in[0]), got[0], place, "pair_sum_ffn1_in_0"), small_all


def _silu_grad(z):
    sg = jax.nn.sigmoid(z)
    return sg * (1 + z * (1 - sg))


def kernel(x, c, ctx, c_ctx, w_mod, b_mod, norm_ffn1, w_ffn1_in, w_ffn1_out, norm_mix, w_in, w_pool, pool_scale, sink, w_out, norm_ffn2, w_ffn2_in, w_ffn2_out, norm_final, loss_target, m_c_ctx, m_w_mod, m_b_mod, m_norm_ffn1, m_w_ffn1_in, m_w_ffn1_out, m_norm_mix, m_w_in, m_w_pool, m_pool_scale, m_sink, m_w_out, m_norm_ffn2, m_w_ffn2_in, m_w_ffn2_out, m_norm_final, v_c_ctx, v_w_mod, v_b_mod, v_norm_ffn1, v_w_ffn1_in, v_w_ffn1_out, v_norm_mix, v_w_in, v_w_pool, v_pool_scale, v_sink, v_w_out, v_norm_ffn2, v_w_ffn2_in, v_w_ffn2_out, v_norm_final):
    px, py, pc = _place()
    slot, me = 2 * px + py, 4 * px + 2 * py + pc
    n_grp = len(POOL_WINDOWS)

    (c_rows,) = all_gather([c.reshape(8, D // 8)], "gather_c")
    c_all = jnp.concatenate([c_rows.reshape(N_DEV, D), c_ctx.reshape(1, D), jnp.zeros((16 - N_DEV - 1, D), F32)], axis=0)
    b_cols = lax.dynamic_slice(b_mod, (0, slot * MOD_COLS), (2, MOD_COLS)).reshape(2, 1, MOD_COLS)
    (mod_parts,) = all_gather([mod_rows(c_all, w_mod, b_cols, "mod_rows")], "gather_mods")
    mods_all = mod_parts[0::2].transpose(1, 2, 0, 3).reshape(2, 16, N_MOD * D)
    mx = lax.dynamic_slice(mods_all, (0, me, 0), (2, 1, N_MOD * D)).reshape(2, N_MOD, D)
    mc = mods_all[:, N_DEV].reshape(2, N_MOD, D)
    pad = jnp.zeros((2, 16 - N_MOD, D), F32)
    mods = jnp.stack([jnp.concatenate([mx, pad], axis=1), jnp.concatenate([mc, pad], axis=1)], axis=1)

    place = jnp.stack([pc, slot]).astype(jnp.int32)
    shards = dict(ffn1_in=w_ffn1_in, ffn1_out=w_ffn1_out, w_in=w_in, w_out=w_out, ffn2_in=w_ffn2_in, ffn2_out=w_ffn2_out)
    placed = {name: [cast_place(shards[name], l, place, f"cast_{name}_{l}") for l in range(2)] for name in BIG}
    norms = [g.reshape(2, 1, D) for g in (norm_ffn1, norm_mix, norm_ffn2)]
    row_sums = ("dm_f1", "dm_mix", "dm_gate", "dm_f2")

    def small_blocks(small, loss_blk):
        stacked = {k: jnp.stack([small[0][k], small[1][k]]) for k in row_sums + ("dwp", "dps", "dsink")}
        return ([stacked[k].reshape(32, D) for k in row_sums]
                + [stacked["dwp"].reshape(2 * n_grp * GROUP, GROUP), stacked["dps"].reshape(16, POOL_W),
                   stacked["dsink"].reshape(16, BLK), loss_blk])

    dx, halves, last_pair, small_all = local_step(x[0], ctx[0], loss_target[0], mods, norms, norm_final.reshape(1, D), placed,
                                                   w_pool.astype(BF16), pool_scale.reshape(2, 1, POOL_W), sink, place, small_blocks)
    grads = {}

    *g_dm, g_dwp, g_dps, g_dsink, g_loss = small_all
    tot, rows, fin = reduce_small(*[g.reshape(N_DEV, 2, 2, 8, D) for g in g_dm], g_loss, "reduce_small")
    s_dwp, s_dps, s_dsink = sum8([g_dwp, g_dps, g_dsink], "sum_pool_sink")
    grads.update(
        w_pool=s_dwp.reshape(2, n_grp, GROUP, GROUP), pool_scale=s_dps.reshape(2, 8, POOL_W)[:, 0],
        sink=s_dsink.reshape(2, 8, BLK)[:, 0, :N_HEADS], b_mod=tot[:, :N_MOD].reshape(2, N_MOD * D),
        norm_ffn1=tot[:, N_MOD], norm_mix=tot[:, N_MOD + 1], norm_ffn2=tot[:, N_MOD + 2], norm_final=fin[0])
    loss = fin[1, 0]

    dmod_cols = lax.dynamic_slice(rows[:, :, :N_MOD, :].reshape(2, 16, N_MOD * D), (0, 0, slot * MOD_COLS), (2, 16, MOD_COLS))
    grads["w_mod"], dc = mod_grads(c_all, dmod_cols, w_mod, "mod_grads")
    (g_dc,) = all_gather([dc], "gather_dc")
    (s_dc,) = sum8([g_dc], "sum_dc")
    (d_c_ctx,) = elementwise(lambda d, z: (0.5 * d * _silu_grad(z),), [s_dc[N_DEV:N_DEV + 1], c_ctx.reshape(1, D)], [F32], "c_ctx_grad")
    grads["c_ctx"] = d_c_ctx.reshape(D)

    given = dict(c_ctx=(c_ctx, m_c_ctx, v_c_ctx), w_mod=(w_mod, m_w_mod, v_w_mod), b_mod=(b_mod, m_b_mod, v_b_mod),
                 norm_ffn1=(norm_ffn1, m_norm_ffn1, v_norm_ffn1), w_ffn1_in=(w_ffn1_in, m_w_ffn1_in, v_w_ffn1_in),
                 w_ffn1_out=(w_ffn1_out, m_w_ffn1_out, v_w_ffn1_out), norm_mix=(norm_mix, m_norm_mix, v_norm_mix),
                 w_in=(w_in, m_w_in, v_w_in), w_pool=(w_pool, m_w_pool, v_w_pool),
                 pool_scale=(pool_scale, m_pool_scale, v_pool_scale), sink=(sink, m_sink, v_sink), w_out=(w_out, m_w_out, v_w_out),
                 norm_ffn2=(norm_ffn2, m_norm_ffn2, v_norm_ffn2), w_ffn2_in=(w_ffn2_in, m_w_ffn2_in, v_w_ffn2_in),
                 w_ffn2_out=(w_ffn2_out, m_w_ffn2_out, v_w_ffn2_out), norm_final=(norm_final, m_norm_final, v_norm_final))
    got = run_exchange(scatter_exchange([last_pair[1]]), "scatter_last")
    halves["ffn1_in"][0] = chip_sum(last_pair[0], got, place, "chip_sum_ffn1_in_0")
    order = [(name, l) for name in BIG for l in range(2)]
    shard = dict(zip(order, pair_gather([halves[name][l] for name, l in order], "grad_pair_gather")))

    g_out, d_out, m_out, v_out = [], [], [], []
    for name, (w, m, v) in given.items():
        if name in BIG or name[2:] in BIG:
            key = name if name in BIG else name[2:]
            grad, delta, new_m, new_v = adamw_layers(w, shard[key, 0], shard[key, 1], m, v, f"adamw_{name}")
        else:
            grad = grads[name]
            delta, new_m, new_v = adamw(w, grad, m, v, f"adamw_{name}")
        g_out.append(grad)
        d_out.append(delta)
        m_out.append(new_m)
        v_out.append(new_v)
    return (loss, dx[None], *g_out, *d_out, *m_out, *v_out)
```

```python
import functools

import jax
import jax.numpy as jnp
from jax import lax
from jax.experimental import pallas as pl
from jax.experimental.pallas import tpu as pltpu

F32, BF16 = jnp.float32, jnp.bfloat16
D = 1024
D_FF = 2816
N_SLOT = 4
FF_COLS = 2 * D_FF // N_SLOT
N_MOD = 9
MOD_COLS = N_MOD * D // N_SLOT
POOL_W, ATTN_W, KV_W = 512, 512, 128
PROJ_W = POOL_W + ATTN_W + 2 * KV_W
N_HEADS, Q_GROUP, HEAD = 8, 4, 64
GROUP = 128
POOL_WINDOWS = (2, 4, 8, 16)
BLK = 128
QB = 256
WIN = QB + 2 * BLK
GRID_W = 64
ROPE_BASE = 10000.0
EPS = 1e-6
NEG_INF = -1e30
TM = 256
N_DEV = 8
VMEM_LIMIT_BYTES = 56 * 1024 * 1024
ADAM_LR, ADAM_B1, ADAM_B2, ADAM_EPS, ADAM_WD, ADAM_STEP = 0.001, 0.9, 0.999, 1e-08, 0.01, 10
MESH = pl.DeviceIdType.MESH
NT = (((1,), (1,)), ((), ()))
TN = (((0,), (0,)), ((), ()))


def _params(*sem):
    return pltpu.CompilerParams(dimension_semantics=sem, vmem_limit_bytes=VMEM_LIMIT_BYTES)


def _whole(shape, lead=()):
    idx = tuple(lead) + (0,) * len(shape)
    return pl.BlockSpec((None,) * len(lead) + tuple(shape), lambda *_: idx, pipeline_mode=pl.Buffered(1))


def _rows(cols, tm=TM):
    return pl.BlockSpec((tm, cols), lambda i: (i, 0))


def _mods_spec(layer, n_lat):
    return pl.BlockSpec((None, None, 16, D), lambda i: (layer, (i >= n_lat).astype(jnp.int32), 0, 0))


def _acc_spec(n_lat):
    return pl.BlockSpec((None, 8, D), lambda i: ((i >= n_lat).astype(jnp.int32), 0, 0))


def _dot(a, b):
    return jnp.dot(a, b, preferred_element_type=F32)


def _dotg(a, b, dims):
    return lax.dot_general(a, b, dims, preferred_element_type=F32)


def _sum0(v):
    return jnp.sum(v, axis=0, keepdims=True)


def _norm_mod(h, g, shift, scale):
    r = lax.rsqrt(jnp.mean(h * h, axis=-1, keepdims=True) + EPS)
    xhat = h * r
    y = xhat * g
    return y * (1 + scale) + shift, xhat, r, y


def _norm_mod_bwd(dn, xhat, r, y, g, scale):
    dy = dn * (1 + scale)
    dx = dy * g
    dh = r * (dx - xhat * jnp.mean(dx * xhat, axis=-1, keepdims=True))
    return _sum0(dn), _sum0(dn * y), _sum0(dy * xhat), dh


def _swap_halves(v):
    w = v.shape[1]
    lane = lax.broadcasted_iota(jnp.int32, v.shape, 1)
    return jnp.where(lane % HEAD < HEAD // 2, pltpu.roll(v, w - HEAD // 2, axis=1), pltpu.roll(v, HEAD // 2, axis=1))


def _tile_lanes(t, width):
    return t if width == t.shape[1] else jnp.concatenate([t] * (width // t.shape[1]), axis=1)


def _rope(v, cos, sin):
    return v * _tile_lanes(cos, v.shape[1]) + _swap_halves(v) * _tile_lanes(sin, v.shape[1])


def _unrope(g, cos, sin):
    return g * _tile_lanes(cos, g.shape[1]) + _swap_halves(g * _tile_lanes(sin, g.shape[1]))


def ffn_fwd(h, mods, g, w4, wo, layer, k0, n_lat, name, ex=None):
    s = h.shape[0]

    def body(h_ref, m_ref, g_ref, w_ref, wo_ref, ho_ref, ab_ref, f_ref):
        hh = h_ref[...]
        n, _, _, _ = _norm_mod(hh, g_ref[...], m_ref[k0:k0 + 1, :], m_ref[k0 + 1:k0 + 2, :])
        nb = n.astype(BF16)
        acc = jnp.zeros((TM, D), F32)
        for j in range(2):
            a = _dot(nb, w_ref[j])
            b = _dot(nb, w_ref[2 + j])
            ab_ref[:, j * FF_COLS:(j + 1) * FF_COLS] = a.astype(BF16)
            ab_ref[:, (2 + j) * FF_COLS:(3 + j) * FF_COLS] = b.astype(BF16)
            act = (a * jax.nn.sigmoid(a) * b).astype(BF16)
            acc = acc + _dot(act, wo_ref[j * FF_COLS:(j + 1) * FF_COLS, :])
        f_ref[...] = acc
        ho_ref[...] = hh + 0.5 * m_ref[k0 + 2:k0 + 3, :] * acc

    return _grid_call(
        body, name, s // TM,
        [_rows(D), _mods_spec(layer, n_lat), _whole((1, D), (layer,)), _whole((N_SLOT, D, FF_COLS)), _whole((D_FF, D))],
        [_rows(D), _rows(2 * D_FF), _rows(D)],
        [jax.ShapeDtypeStruct((s, D), F32), jax.ShapeDtypeStruct((s, 2 * D_FF), BF16), jax.ShapeDtypeStruct((s, D), F32)],
        (h, mods, g, w4, wo), "parallel", ex)


def ffn_bwd(h, ab, f, dh, mods, g, w4, wo, layer, k0, n_lat, name, ex=None):
    s = h.shape[0]

    def body(h_ref, ab_ref, f_ref, dh_ref, m_ref, g_ref, w_ref, wo_ref, dhi_ref, dab_ref, df_ref, n_ref, act_ref, dm_ref):
        i = pl.program_id(0)

        @pl.when((i == 0) | (i == n_lat))
        def _():
            dm_ref[...] = jnp.zeros_like(dm_ref)

        hh, dho, gg = h_ref[...], dh_ref[...], g_ref[...]
        scale, gate = m_ref[k0 + 1:k0 + 2, :], m_ref[k0 + 2:k0 + 3, :]
        n, xhat, r, y = _norm_mod(hh, gg, m_ref[k0:k0 + 1, :], scale)
        n_ref[...] = n.astype(BF16)
        dgate = _sum0(dho * (0.5 * f_ref[...]))
        dfb = ((0.5 * gate) * dho).astype(BF16)
        df_ref[...] = dfb
        dn = jnp.zeros((TM, D), F32)
        for j in range(2):
            a = ab_ref[:, j * FF_COLS:(j + 1) * FF_COLS].astype(F32)
            b = ab_ref[:, (2 + j) * FF_COLS:(3 + j) * FF_COLS].astype(F32)
            sg = jax.nn.sigmoid(a)
            sa = a * sg
            act_ref[:, j * FF_COLS:(j + 1) * FF_COLS] = (sa * b).astype(BF16)
            dact = _dotg(dfb, wo_ref[j * FF_COLS:(j + 1) * FF_COLS, :], NT)
            da = (dact * b * (sg * (1 + a * (1 - sg)))).astype(BF16)
            db = (dact * sa).astype(BF16)
            dab_ref[:, j * FF_COLS:(j + 1) * FF_COLS] = da
            dab_ref[:, (2 + j) * FF_COLS:(3 + j) * FF_COLS] = db
            dn = dn + _dotg(da, w_ref[j], NT) + _dotg(db, w_ref[2 + j], NT)
        dsh, dsc, dg, dhn = _norm_mod_bwd(dn, xhat, r, y, gg, scale)
        dhi_ref[...] = dho + dhn
        dm_ref[0:1, :] += dsh
        dm_ref[1:2, :] += dsc
        dm_ref[2:3, :] += dgate
        dm_ref[3:4, :] += dg

    return _grid_call(
        body, name, s // TM,
        [_rows(D), _rows(2 * D_FF), _rows(D), _rows(D), _mods_spec(layer, n_lat), _whole((1, D), (layer,)),
         _whole((N_SLOT, D, FF_COLS)), _whole((D_FF, D))],
        [_rows(D), _rows(2 * D_FF), _rows(D), _rows(D), _rows(D_FF), _acc_spec(n_lat)],
        [jax.ShapeDtypeStruct((s, D), F32), jax.ShapeDtypeStruct((s, 2 * D_FF), BF16), jax.ShapeDtypeStruct((s, D), BF16),
         jax.ShapeDtypeStruct((s, D), BF16), jax.ShapeDtypeStruct((s, D_FF), BF16), jax.ShapeDtypeStruct((2, 8, D), F32)],
        (h, ab, f, dh, mods, g, w4, wo), "arbitrary", ex)


def _token_tile(s, limit=2176):
    return max(ts for ts in range(16, limit + 1, 16) if s % ts == 0)


def wgrad(a, b, tk, tn, slot_cols, name, ex=None):
    s, k = a.shape
    n = b.shape[1]
    ts = _token_tile(s)
    steps = s // ts

    def body(a_ref, b_ref, o_ref, o16_ref):
        r = _dotg(a_ref[...], b_ref[...], TN)
        si = pl.program_id(2)

        @pl.when(si == 0)
        def _():
            o_ref[...] = r

        @pl.when(si > 0)
        def _():
            o_ref[...] += r

        @pl.when(si == steps - 1)
        def _():
            o16_ref[...] = o_ref[...].astype(BF16)

    if slot_cols is None:
        shape, spec = (k, n), pl.BlockSpec((tk, tn), lambda i, j, si: (i, j))
    else:
        per = slot_cols // tn
        shape, spec = (n // slot_cols, k, slot_cols), pl.BlockSpec((None, tk, tn), lambda i, j, si: (lax.div(j, per), i, lax.rem(j, per)))
    return _grid_call(
        body, name, (k // tk, n // tn, steps),
        [pl.BlockSpec((ts, tk), lambda i, j, si: (si, i)), pl.BlockSpec((ts, tn), lambda i, j, si: (si, j))], [spec, spec],
        [jax.ShapeDtypeStruct(shape, F32), jax.ShapeDtypeStruct(shape, BF16)], (a, b), ("parallel", "parallel", "arbitrary"), ex)


def proj_fwd(h, mods, g, w_in, cos, sin, layer, n_lat, name):
    s = h.shape[0]

    def body(h_ref, m_ref, g_ref, w_ref, cos_ref, sin_ref, u_ref, q_ref, k_ref, v_ref):
        n, _, _, _ = _norm_mod(h_ref[...], g_ref[...], m_ref[3:4, :], m_ref[4:5, :])
        p = _dot(n.astype(BF16), w_ref[...])
        cs, sn = cos_ref[...], sin_ref[...]
        u_ref[...] = p[:, :POOL_W]
        q_ref[...] = (_rope(p[:, POOL_W:POOL_W + ATTN_W], cs, sn) * HEAD ** -0.5).astype(BF16)
        k_ref[...] = _rope(p[:, POOL_W + ATTN_W:POOL_W + ATTN_W + KV_W], cs, sn).astype(BF16)
        v_ref[...] = p[:, POOL_W + ATTN_W + KV_W:].astype(BF16)

    return pl.pallas_call(
        body, name=name, grid=(s // TM,),
        in_specs=[_rows(D), _mods_spec(layer, n_lat), _whole((1, D), (layer,)), _whole((D, PROJ_W)),
                  _rows(BLK), _rows(BLK)],
        out_specs=[_rows(POOL_W), _rows(ATTN_W), _rows(KV_W), _rows(KV_W)],
        out_shape=[jax.ShapeDtypeStruct((s, POOL_W), F32), jax.ShapeDtypeStruct((s, ATTN_W), BF16),
                   jax.ShapeDtypeStruct((s, KV_W), BF16), jax.ShapeDtypeStruct((s, KV_W), BF16)],
        compiler_params=_params("parallel"),
    )(h, mods, g, w_in, cos, sin)


def proj_bwd(h, du, dq, dk, dv, dh, mods, g, w_in, cos, sin, layer, n_lat, name):
    s = h.shape[0]

    def body(h_ref, du_ref, dq_ref, dk_ref, dv_ref, dh_ref, m_ref, g_ref, w_ref, cos_ref, sin_ref,
             dhi_ref, dp_ref, n_ref, dm_ref):
        i = pl.program_id(0)

        @pl.when((i == 0) | (i == n_lat))
        def _():
            dm_ref[...] = jnp.zeros_like(dm_ref)

        gg, scale = g_ref[...], m_ref[4:5, :]
        n, xhat, r, y = _norm_mod(h_ref[...], gg, m_ref[3:4, :], scale)
        n_ref[...] = n.astype(BF16)
        cs, sn = cos_ref[...], sin_ref[...]
        dp = jnp.concatenate([du_ref[...], _unrope(dq_ref[...], cs, sn) * HEAD ** -0.5, _unrope(dk_ref[...], cs, sn),
                              dv_ref[...]], axis=1).astype(BF16)
        dp_ref[...] = dp
        dsh, dsc, dg, dhn = _norm_mod_bwd(_dotg(dp, w_ref[...], NT), xhat, r, y, gg, scale)
        dhi_ref[...] = dh_ref[...] + dhn
        dm_ref[0:1, :] += dsh
        dm_ref[1:2, :] += dsc
        dm_ref[3:4, :] += dg

    return pl.pallas_call(
        body, name=name, grid=(s // TM,),
        in_specs=[_rows(D), _rows(POOL_W), _rows(ATTN_W), _rows(KV_W), _rows(KV_W), _rows(D), _mods_spec(layer, n_lat),
                  _whole((1, D), (layer,)), _whole((D, PROJ_W)), _rows(BLK), _rows(BLK)],
        out_specs=[_rows(D), _rows(PROJ_W), _rows(D), _acc_spec(n_lat)],
        out_shape=[jax.ShapeDtypeStruct((s, D), F32), jax.ShapeDtypeStruct((s, PROJ_W), BF16),
                   jax.ShapeDtypeStruct((s, D), BF16), jax.ShapeDtypeStruct((2, 8, D), F32)],
        compiler_params=_params("arbitrary"),
    )(h, du, dq, dk, dv, dh, mods, g, w_in, cos, sin)


def _window(i, s):
    return pl.multiple_of(jnp.clip(i * QB - BLK, 0, s - WIN), BLK)


def mix_tables(t, s):
    n_lat = t // QB
    blocks = jnp.array([0, 1, n_lat - 1] + list(range(n_lat, s // QB)))[:, None, None]
    ws = jnp.clip(blocks * QB - BLK, 0, s - WIN)
    q = blocks * QB + jnp.arange(QB)[None, :, None]
    k = ws + jnp.arange(WIN)[None, None, :]
    is_lat = blocks < n_lat
    local = jnp.where(is_lat & (k < t) & (jnp.abs(k - q) <= BLK), 0.0, NEG_INF).astype(F32)
    bias = jnp.concatenate([local, jnp.zeros(local.shape[:2] + (s - t,), F32)], axis=2)
    seq_lo, seq_hi = jnp.where(is_lat, 0, t), jnp.where(is_lat, t, s)
    bands, counts = [], []
    for w in POOL_WINDOWS:
        lo, hi = jnp.maximum(q - w // 2, seq_lo), jnp.minimum(q + w - w // 2, seq_hi)
        bands.append((k >= lo) & (k < hi))
        counts.append((hi - lo).astype(F32))
    band = jnp.stack(bands, axis=1).astype(BF16)
    count = jnp.concatenate(counts + [jnp.ones(counts[0].shape[:2] + (BLK - len(counts),), F32)], axis=2)
    return dict(bias=bias, band=band, band_t=band.transpose(0, 1, 3, 2), count=count)


def _case_spec(table, n_lat_blk):
    def kind(i):
        return jnp.where(i < n_lat_blk - 1, jnp.minimum(i, 1), i - n_lat_blk + 3)

    shape = table.shape[1:]
    return pl.BlockSpec((None,) + shape, lambda i: (kind(i),) + (0,) * len(shape))


def _split_dot(band, v):
    return _dot(band, v.astype(BF16))


def _pooled(u_ref, band_ref, cnt_ref, i, ws, gi):
    cols = slice(gi * GROUP, (gi + 1) * GROUP)
    mean = _split_dot(band_ref[gi], u_ref[pl.ds(ws, WIN), cols]) / cnt_ref[:, gi:gi + 1]
    return mean - u_ref[pl.ds(pl.multiple_of(i * QB, QB), QB), cols]


def _head_cols(hd):
    return slice(hd * HEAD, (hd + 1) * HEAD)


def _stack_heads(x, hk, first=0):
    return jnp.concatenate([x[:, first + (Q_GROUP * hk + g) * HEAD:first + (Q_GROUP * hk + g + 1) * HEAD]
                            for g in range(Q_GROUP)], axis=0)


def _biased(scores, bias):
    return (scores.reshape(Q_GROUP, QB, -1) + bias).reshape(Q_GROUP * QB, -1)


def _group_column(vals):
    row = lax.broadcasted_iota(jnp.int32, (Q_GROUP * QB, 1), 0)
    out = jnp.full((Q_GROUP * QB, 1), vals[Q_GROUP - 1], F32)
    for g in range(Q_GROUP - 2, -1, -1):
        out = jnp.where(row < (g + 1) * QB, vals[g], out)
    return out


def _lane_place(cols, width=BLK):
    lane = lax.broadcasted_iota(jnp.int32, (cols[0].shape[0], width), 1)
    out = jnp.zeros((cols[0].shape[0], width), F32)
    for hd, c in enumerate(cols):
        out = jnp.where(lane == hd, c, out)
    return out


def mix_fwd(h, q, k, v, u, w_pool, pool_scale, sink, w_out, mods, tables, layer, t, name, ex=None):
    s = h.shape[0]
    n_lat_blk = t // QB

    def body(h_ref, q_ref, k_ref, v_ref, u_ref, wp_ref, ps_ref, sink_ref, wo_ref, m_ref, bias_ref, band_ref, cnt_ref,
             ho_ref, cat_ref, lse_ref, mo_ref):
        i = pl.program_id(0)
        ws = _window(i, s)
        for gi in range(len(POOL_WINDOWS)):
            mixed = _dot(_pooled(u_ref, band_ref, cnt_ref, i, ws, gi).astype(BF16), wp_ref[gi])
            cat_ref[:, gi * GROUP:(gi + 1) * GROUP] = (mixed * ps_ref[:, gi * GROUP:(gi + 1) * GROUP]).astype(BF16)
        bias = bias_ref[...]
        k_all = jnp.concatenate([k_ref[pl.ds(ws, WIN), :], k_ref[t:s, :]], axis=0)
        v_all = jnp.concatenate([v_ref[pl.ds(ws, WIN), :], v_ref[t:s, :]], axis=0)
        lses = []
        for hk in range(N_HEADS // Q_GROUP):
            kv = _head_cols(hk)
            sc = _biased(_dotg(_stack_heads(q_ref[...], hk), k_all[:, kv], NT), bias)
            sk = _group_column([sink_ref[layer, Q_GROUP * hk + g] for g in range(Q_GROUP)])
            m = jnp.maximum(jnp.max(sc, axis=1, keepdims=True), sk)
            e = jnp.exp(sc - m)
            l = jnp.sum(e, axis=1, keepdims=True) + jnp.exp(sk - m)
            o = _dot(e.astype(BF16), v_all[:, kv]) * (1.0 / l)
            lse = m + jnp.log(l)
            for g in range(Q_GROUP):
                hd = Q_GROUP * hk + g
                cat_ref[:, POOL_W + hd * HEAD:POOL_W + (hd + 1) * HEAD] = o[g * QB:(g + 1) * QB].astype(BF16)
                lses.append(lse[g * QB:(g + 1) * QB])
        lse_ref[...] = _lane_place(lses)
        mo = _dot(cat_ref[...], wo_ref[...])
        mo_ref[...] = mo
        ho_ref[...] = h_ref[...] + m_ref[5:6, :] * mo

    blk = lambda cols: _rows(cols, QB)
    return _grid_call(
        body, name, s // QB,
        [blk(D), blk(ATTN_W), _whole((s, KV_W)), _whole((s, KV_W)), _whole((s, POOL_W)),
         _whole((len(POOL_WINDOWS), GROUP, GROUP), (layer,)), _whole((1, POOL_W), (layer,)),
         pl.BlockSpec(memory_space=pltpu.SMEM), _whole((POOL_W + ATTN_W, D)), _mods_spec(layer, n_lat_blk),
         _case_spec(tables["bias"], n_lat_blk), _case_spec(tables["band"], n_lat_blk), _case_spec(tables["count"], n_lat_blk)],
        [blk(D), blk(POOL_W + ATTN_W), blk(BLK), blk(D)],
        [jax.ShapeDtypeStruct((s, D), F32), jax.ShapeDtypeStruct((s, POOL_W + ATTN_W), BF16), jax.ShapeDtypeStruct((s, BLK), F32),
         jax.ShapeDtypeStruct((s, D), F32)],
        (h, q, k, v, u, w_pool, pool_scale, sink, w_out, mods, tables["bias"], tables["band"], tables["count"]), "parallel", ex)


def mix_bwd(dh, mo, q, k, v, u, lse, w_pool, pool_scale, sink, w_out, mods, tables, layer, t, name, ex=None):
    s = dh.shape[0]
    n_lat_blk = t // QB
    n_grp = len(POOL_WINDOWS)

    def body(dh_ref, mo_ref, q_ref, k_ref, v_ref, u_ref, lse_ref, wp_ref, ps_ref, sink_ref, wo_ref, m_ref,
             bias_ref, band_ref, band_t_ref, cnt_ref,
             dq_ref, dk_ref, dv_ref, du_ref, dmo_ref, dwp_ref, dps_ref, dsink_ref, dm_ref):
        i = pl.program_id(0)

        @pl.when(i == 0)
        def _():
            for ref in (dk_ref, dv_ref, du_ref, dwp_ref, dps_ref, dsink_ref):
                ref[...] = jnp.zeros_like(ref)

        @pl.when((i == 0) | (i == n_lat_blk))
        def _():
            dm_ref[...] = jnp.zeros_like(dm_ref)

        ws = _window(i, s)
        here = pl.ds(pl.multiple_of(i * QB, QB), QB)
        dho = dh_ref[...]
        dm_ref[2:3, :] += _sum0(dho * mo_ref[...])
        dmo = (m_ref[5:6, :] * dho).astype(BF16)
        dmo_ref[...] = dmo
        dcat = _dotg(dmo, wo_ref[...], NT)

        for gi in range(n_grp):
            cols = slice(gi * GROUP, (gi + 1) * GROUP)
            pooled = _pooled(u_ref, band_ref, cnt_ref, i, ws, gi).astype(BF16)
            dpo = dcat[:, cols]
            dps_ref[0:1, cols] += _sum0(dpo * _dot(pooled, wp_ref[gi]))
            dmixed = (dpo * ps_ref[:, cols]).astype(BF16)
            dwp_ref[gi] += _dotg(pooled, dmixed, TN)
            dpooled = _dotg(dmixed, wp_ref[gi], NT)
            du_ref[pl.ds(ws, WIN), cols] += _split_dot(band_t_ref[gi], dpooled / cnt_ref[:, gi:gi + 1])
            du_ref[here, cols] -= dpooled

        bias = bias_ref[...]
        k_all = jnp.concatenate([k_ref[pl.ds(ws, WIN), :], k_ref[t:s, :]], axis=0)
        v_all = jnp.concatenate([v_ref[pl.ds(ws, WIN), :], v_ref[t:s, :]], axis=0)
        qq, lse_all = q_ref[...], lse_ref[...]
        dqs, dsinks, dks, dvs = [], [], [], []
        for hk in range(N_HEADS // Q_GROUP):
            kv = _head_cols(hk)
            q4 = _stack_heads(qq, hk)
            lse = jnp.concatenate([lse_all[:, Q_GROUP * hk + g:Q_GROUP * hk + g + 1] for g in range(Q_GROUP)], axis=0)
            p = jnp.exp(_biased(_dotg(q4, k_all[:, kv], NT), bias) - lse)
            do = _stack_heads(dcat, hk, POOL_W).astype(BF16)
            dp = _dotg(do, v_all[:, kv], NT)
            delta = jnp.sum(p * dp, axis=1, keepdims=True)
            ds = (p * (dp - delta)).astype(BF16)
            sk = _group_column([sink_ref[layer, Q_GROUP * hk + g] for g in range(Q_GROUP)])
            dsk = -jnp.exp(sk - lse) * delta
            dq = _dot(ds, k_all[:, kv])
            for g in range(Q_GROUP):
                dqs.append(dq[g * QB:(g + 1) * QB])
                dsinks.append(_sum0(dsk[g * QB:(g + 1) * QB]))
            dks.append(_dotg(ds, q4, TN))
            dvs.append(_dotg(p.astype(BF16), do, TN))
        dq_ref[...] = jnp.concatenate(dqs, axis=1)
        dk, dv = jnp.concatenate(dks, axis=1), jnp.concatenate(dvs, axis=1)
        dk_ref[pl.ds(ws, WIN), :] += dk[:WIN]
        dv_ref[pl.ds(ws, WIN), :] += dv[:WIN]
        dk_ref[t:s, :] += dk[WIN:]
        dv_ref[t:s, :] += dv[WIN:]
        dsink_ref[0:1, :] += _lane_place(dsinks)

    blk = lambda cols: _rows(cols, QB)
    full = lambda shape: pl.BlockSpec(shape, lambda i: (0,) * len(shape))
    return _grid_call(
        body, name, s // QB,
        [blk(D), blk(D), blk(ATTN_W), _whole((s, KV_W)), _whole((s, KV_W)), _whole((s, POOL_W)),
         blk(BLK), _whole((n_grp, GROUP, GROUP), (layer,)), _whole((1, POOL_W), (layer,)),
         pl.BlockSpec(memory_space=pltpu.SMEM), _whole((POOL_W + ATTN_W, D)), _mods_spec(layer, n_lat_blk)]
        + [_case_spec(tables[key], n_lat_blk) for key in ("bias", "band", "band_t", "count")],
        [blk(ATTN_W), full((s, KV_W)), full((s, KV_W)), full((s, POOL_W)), blk(D),
         full((n_grp, GROUP, GROUP)), full((8, POOL_W)), full((8, BLK)), _acc_spec(n_lat_blk)],
        [jax.ShapeDtypeStruct((s, ATTN_W), F32), jax.ShapeDtypeStruct((s, KV_W), F32),
         jax.ShapeDtypeStruct((s, KV_W), F32), jax.ShapeDtypeStruct((s, POOL_W), F32),
         jax.ShapeDtypeStruct((s, D), BF16), jax.ShapeDtypeStruct((n_grp, GROUP, GROUP), F32),
         jax.ShapeDtypeStruct((8, POOL_W), F32), jax.ShapeDtypeStruct((8, BLK), F32), jax.ShapeDtypeStruct((2, 8, D), F32)],
        (dh, mo, q, k, v, u, lse, w_pool, pool_scale, sink, w_out, mods, tables["bias"], tables["band"], tables["band_t"],
         tables["count"]), "arbitrary", ex)


def loss_head(h, target, g, t, name):
    s = h.shape[0]
    n_lat = t // TM

    def body(h_ref, t_ref, g_ref, dh_ref, acc_ref):
        i = pl.program_id(0)

        @pl.when(i == 0)
        def _():
            acc_ref[...] = jnp.zeros_like(acc_ref)

        @pl.when(i < n_lat)
        def _():
            hh, gg = h_ref[...], g_ref[...]
            r = lax.rsqrt(jnp.mean(hh * hh, axis=-1, keepdims=True) + EPS)
            xhat = hh * r
            err = xhat * gg - t_ref[...]
            dy = err * (1.0 / D)
            dx = dy * gg
            dh_ref[...] = r * (dx - xhat * jnp.mean(dx * xhat, axis=-1, keepdims=True))
            acc_ref[0:1, :] += _sum0(dy * xhat)
            acc_ref[1:2, :] += _sum0(err * err)

        @pl.when(i >= n_lat)
        def _():
            dh_ref[...] = jnp.zeros_like(dh_ref)

    return pl.pallas_call(
        body, name=name, grid=(s // TM,),
        in_specs=[_rows(D), pl.BlockSpec((TM, D), lambda i: (jnp.minimum(i, n_lat - 1), 0)), _whole((1, D))],
        out_specs=[_rows(D), pl.BlockSpec((8, D), lambda i: (0, 0))],
        out_shape=[jax.ShapeDtypeStruct((s, D), F32), jax.ShapeDtypeStruct((8, D), F32)],
        compiler_params=_params("arbitrary"),
    )(h, target, g)


def mod_rows(c_all, w_mod, b_cols, name):
    def body(c_ref, w_ref, b_ref, o_ref):
        cc = c_ref[...]
        o_ref[...] = _dot((cc * jax.nn.sigmoid(cc)).astype(BF16), w_ref[...].astype(BF16)) + b_ref[...]

    return pl.pallas_call(
        body, name=name, grid=(2,),
        in_specs=[pl.BlockSpec((16, D), lambda l: (0, 0)), pl.BlockSpec((None, D, MOD_COLS), lambda l: (l, 0, 0)),
                  pl.BlockSpec((None, 1, MOD_COLS), lambda l: (l, 0, 0))],
        out_specs=pl.BlockSpec((None, 16, MOD_COLS), lambda l: (l, 0, 0)),
        out_shape=jax.ShapeDtypeStruct((2, 16, MOD_COLS), F32),
        compiler_params=_params("parallel"),
    )(c_all, w_mod, b_cols)


def mod_grads(c_all, dmod_cols, w_mod, name):
    def body(c_ref, d_ref, w_ref, dw_ref, dc_ref):
        @pl.when(pl.program_id(0) == 0)
        def _():
            dc_ref[...] = jnp.zeros_like(dc_ref)

        cc = c_ref[...]
        dd = d_ref[...].astype(BF16)
        dw_ref[...] = _dotg((cc * jax.nn.sigmoid(cc)).astype(BF16), dd, TN)
        dc_ref[...] += _dotg(dd, w_ref[...].astype(BF16), NT)

    return pl.pallas_call(
        body, name=name, grid=(2,),
        in_specs=[pl.BlockSpec((16, D), lambda l: (0, 0)), pl.BlockSpec((None, 16, MOD_COLS), lambda l: (l, 0, 0)),
                  pl.BlockSpec((None, D, MOD_COLS), lambda l: (l, 0, 0))],
        out_specs=[pl.BlockSpec((None, D, MOD_COLS), lambda l: (l, 0, 0)), pl.BlockSpec((16, D), lambda l: (0, 0))],
        out_shape=[jax.ShapeDtypeStruct((2, D, MOD_COLS), F32), jax.ShapeDtypeStruct((16, D), F32)],
        compiler_params=_params("arbitrary"),
    )(c_all, dmod_cols, w_mod)


def _row_tile(rows, cols, n_arrays):
    budget = VMEM_LIMIT_BYTES // 4 // (2 * 4 * n_arrays * cols)
    best = None
    for tr in range(16, rows + 1, 16):
        if rows % tr == 0 and tr <= budget:
            best = tr
    return best if best is not None else rows


def elementwise(fn, ins, out_dtypes, name, ex=None):
    rows, cols = ins[0].shape
    tr = _row_tile(rows, cols, len(ins) + len(out_dtypes))

    def body(*refs):
        outs = fn(*[r[...] for r in refs[:len(ins)]])
        for o_ref, o in zip(refs[len(ins):], outs):
            o_ref[...] = o.astype(o_ref.dtype)

    spec = pl.BlockSpec((tr, cols), lambda i: (i, 0))
    outs, got = _grid_call(body, name, rows // tr, [spec] * len(ins), [spec] * len(out_dtypes),
                           [jax.ShapeDtypeStruct((rows, cols), dt) for dt in out_dtypes], ins, "parallel", ex)
    return outs if ex is None else (outs, got)


def _adamw_tile(w, g, m, v):
    m = ADAM_B1 * m + (1.0 - ADAM_B1) * g
    v = ADAM_B2 * v + (1.0 - ADAM_B2) * (g * g)
    m_hat = m / (1.0 - ADAM_B1 ** ADAM_STEP)
    v_hat = v / (1.0 - ADAM_B2 ** ADAM_STEP)
    return -ADAM_LR * (m_hat / (jnp.sqrt(v_hat) + ADAM_EPS) + ADAM_WD * w), m, v


def adamw(w, g, m, v, name, ex=None):
    shape = w.shape
    two_d = (-1, shape[-1]) if w.ndim > 1 else (1, -1)
    outs = elementwise(_adamw_tile, [a.reshape(two_d) for a in (w, g, m, v)], [F32] * 3, name, ex)
    outs, got = outs if ex is not None else (outs, None)
    outs = [o.reshape(shape) for o in outs]
    return outs if ex is None else (outs, got)


def _prefetch_call(body, name, grid, in_specs, out_specs, out_shape, place, args, aliases=None):
    spec = pltpu.PrefetchScalarGridSpec(num_scalar_prefetch=1, grid=grid, in_specs=in_specs, out_specs=out_specs)
    return pl.pallas_call(body, name=name, grid_spec=spec, out_shape=out_shape, input_output_aliases=aliases or {},
                          compiler_params=_params(*["parallel"] * len(grid)))(place, *args)


def cast_place(w, layer, place, name):
    _, r, c = w.shape
    tr = _row_tile(r, c, 2)

    def body(p_ref, w_ref, o_ref):
        o_ref[...] = w_ref[...].astype(BF16)

    return _prefetch_call(
        body, name, (r // tr,), [pl.BlockSpec((None, tr, c), lambda i, p: (layer, i, 0))],
        pl.BlockSpec((None, tr, c), lambda i, p: (p[1], i, 0)), jax.ShapeDtypeStruct((N_SLOT, r, c), BF16), place, [w])


def pair_sum(g32, got, place, name):
    n_slot, rh, c = got.shape
    tr = _row_tile(rh, c, 4)
    per = rh // tr

    def body(p_ref, a_ref, b_ref, o_ref, o16_ref):
        r = a_ref[...] + b_ref[...].astype(F32)
        o_ref[...] = r
        o16_ref[...] = r.astype(BF16)

    half = pl.BlockSpec((None, tr, c), lambda s, i, p: (s, i, 0))
    return _prefetch_call(
        body, name, (n_slot, per), [pl.BlockSpec((None, tr, c), lambda s, i, p: (s, p[0] * per + i, 0)), half], [half, half],
        [jax.ShapeDtypeStruct(got.shape, F32), jax.ShapeDtypeStruct(got.shape, BF16)], place, [g32, got])


def chip_sum(p32, got, place, name):
    _, rh, c = p32.shape
    tr = _row_tile(rh, c, 5)
    per = rh // tr

    def body(p_ref, m_ref, r0_ref, r1_ref, r2_ref, o_ref):
        o_ref[...] = m_ref[...] + r0_ref[...].astype(F32) + r1_ref[...].astype(F32) + r2_ref[...].astype(F32)

    part = pl.BlockSpec((tr, c), lambda i, p: (i, 0))
    return _prefetch_call(
        body, name, (per,), [pl.BlockSpec((None, tr, c), lambda i, p: (p[1], i, 0)), part, part, part],
        pl.BlockSpec((tr, c), lambda i, p: (p[0] * per + i, 0)), jax.ShapeDtypeStruct((2 * rh, c), F32), place, [p32, *got])


def adamw_layers(w, g0, g1, m, v, name, ex=None):
    _, r, c = w.shape
    tr = _row_tile(r, c, 10)

    def body(w_ref, g0_ref, g1_ref, m_ref, v_ref, g_ref, d_ref, mo_ref, vo_ref):
        g = jnp.where(pl.program_id(0) == 0, g0_ref[...], g1_ref[...])
        g_ref[...] = g
        d_ref[...], mo_ref[...], vo_ref[...] = _adamw_tile(w_ref[...], g, m_ref[...], v_ref[...])

    stacked = pl.BlockSpec((None, tr, c), lambda l, i: (l, i, 0))
    layer = pl.BlockSpec((tr, c), lambda l, i: (i, 0))
    outs, got = _grid_call(body, name, (2, r // tr), [stacked, layer, layer, stacked, stacked], [stacked] * 4,
                           [jax.ShapeDtypeStruct(w.shape, F32)] * 4, (w, g0, g1, m, v), "parallel", ex)
    return outs if ex is None else (outs, got)


def sum8(gathered, name):
    def body(*refs):
        n = len(refs) // 2
        for g_ref, o_ref in zip(refs[:n], refs[n:]):
            acc = g_ref[0]
            for dev in range(1, N_DEV):
                acc = acc + g_ref[dev]
            o_ref[...] = acc

    return pl.pallas_call(
        body, name=name,
        out_shape=[jax.ShapeDtypeStruct(a.shape[1:], F32) for a in gathered],
        compiler_params=_params(),
    )(*gathered)


PHASES = ("start", "late", "finish")


def _place():
    return lax.axis_index("x"), lax.axis_index("y"), lax.axis_index("c")


def _any(n):
    return [pl.BlockSpec(memory_space=pl.ANY)] * n


def gather8_exchange(blocks):
    n = len(blocks)

    def copy(outs, sems, ti, k, block, to, src=None):
        dst = outs[ti].at[4 * block[0] + 2 * block[1] + block[2]]
        return pltpu.make_async_remote_copy(src_ref=dst if src is None else src, dst_ref=dst, send_sem=sems[0].at[ti, k],
                                            recv_sem=sems[1].at[ti, k], device_id=to, device_id_type=MESH)

    def first(ins, outs, sems):
        x, y, c = _place()
        local, sent = [], []
        for ti in range(n):
            local.append(pltpu.make_async_copy(ins[ti], outs[ti].at[4 * x + 2 * y + c], sems[2].at[ti]))
            sent.append(copy(outs, sems, ti, 0, (x, y, c), (x, y, 1 - c), src=ins[ti]))
            sent += [copy(outs, sems, ti, 1 + j, (x, y, c), (*chip, c), src=ins[ti]) for j, chip in enumerate(_three_chips(x, y))]
        return local, sent

    def start(ins, outs, sems):
        local, sent = first(ins, outs, sems)
        for cp in local + sent:
            cp.start()

    def passed_on(outs, sems):
        x, y, c = _place()
        return [copy(outs, sems, ti, 4 + j, (*chip, c), (x, y, 1 - c)) for ti in range(n) for j, chip in enumerate(_three_chips(x, y))]

    def late(ins, outs, sems):
        x, y, c = _place()
        on = passed_on(outs, sems)
        for ti in range(n):
            for j, chip in enumerate(_three_chips(x, y)):
                copy(outs, sems, ti, 1 + j, (*chip, c), (x, y, c)).wait_recv()
                on[3 * ti + j].start()

    def finish(ins, outs, sems):
        x, y, c = _place()
        me, sibling = (x, y, c), (x, y, 1 - c)
        local, sent = first(ins, outs, sems)
        for ti in range(n):
            copy(outs, sems, ti, 0, sibling, me).wait_recv()
            for j, chip in enumerate(_three_chips(x, y)):
                copy(outs, sems, ti, 4 + j, (*chip, 1 - c), me).wait_recv()
        for cp in sent + passed_on(outs, sems):
            cp.wait_send()
        for cp in local:
            cp.wait()

    return dict(ins=list(blocks), out_shape=[jax.ShapeDtypeStruct((N_DEV,) + b.shape, b.dtype) for b in blocks], aliases={},
                start=start, late=late, finish=finish,
                scratch=[pltpu.SemaphoreType.DMA((n, 7)), pltpu.SemaphoreType.DMA((n, 7)), pltpu.SemaphoreType.DMA((n,))])


def all_gather(blocks, name):
    return run_exchange(gather8_exchange(blocks), name)


def _three_chips(x, y):
    return [(1 - x, y), (x, 1 - y), (1 - x, 1 - y)]


def gather_exchange(placed):
    n = len(placed)

    def copy(bufs, sems, ti, k, chip, core, to):
        rh = bufs[ti].shape[1] // 2
        half = bufs[ti].at[2 * chip[0] + chip[1], pl.ds(core * rh, rh), :]
        return pltpu.make_async_remote_copy(src_ref=half, dst_ref=half, send_sem=sems[0].at[ti, k], recv_sem=sems[1].at[ti, k],
                                            device_id=to, device_id_type=MESH)

    def sends(bufs, sems):
        x, y, c = _place()
        return [copy(bufs, sems, ti, k, (x, y), c, (*chip, c)) for ti in range(n) for k, chip in enumerate(_three_chips(x, y))]

    def passed_on(bufs, sems):
        x, y, c = _place()
        return [copy(bufs, sems, ti, 3 + k, chip, c, (x, y, 1 - c)) for ti in range(n) for k, chip in enumerate(_three_chips(x, y))]

    def start(ins, bufs, sems):
        for cp in sends(bufs, sems):
            cp.start()

    def late(ins, bufs, sems):
        x, y, c = _place()
        on = passed_on(bufs, sems)
        for ti in range(n):
            for k, chip in enumerate(_three_chips(x, y)):
                copy(bufs, sems, ti, k, chip, c, (x, y, c)).wait_recv()
                on[3 * ti + k].start()

    def finish(ins, bufs, sems):
        x, y, c = _place()
        for ti in range(n):
            for k, chip in enumerate(_three_chips(x, y)):
                copy(bufs, sems, ti, 3 + k, chip, 1 - c, (x, y, c)).wait_recv()
        for cp in sends(bufs, sems) + passed_on(bufs, sems):
            cp.wait_send()

    return dict(ins=list(placed), out_shape=[jax.ShapeDtypeStruct(w.shape, w.dtype) for w in placed],
                aliases={i: i for i in range(n)}, start=start, late=late, finish=finish,
                scratch=[pltpu.SemaphoreType.DMA((n, 6)), pltpu.SemaphoreType.DMA((n, 6))])


def scatter_exchange(p16):
    n = len(p16)

    def copies(ins, got, sems):
        x, y, c = _place()
        return [pltpu.make_async_remote_copy(src_ref=ins[ti].at[2 * chip[0] + chip[1]], dst_ref=got[3 * ti + k],
                                             send_sem=sems[0].at[ti, k], recv_sem=sems[1].at[ti, k], device_id=(*chip, c),
                                             device_id_type=MESH)
                for ti in range(n) for k, chip in enumerate(_three_chips(x, y))]

    def start(ins, got, sems):
        for cp in copies(ins, got, sems):
            cp.start()

    def finish(ins, got, sems):
        for cp in copies(ins, got, sems):
            cp.wait()

    return dict(ins=list(p16), out_shape=[jax.ShapeDtypeStruct(a.shape[1:], BF16) for a in p16 for _ in range(3)], aliases={},
                start=start, finish=finish, scratch=[pltpu.SemaphoreType.DMA((n, 3)), pltpu.SemaphoreType.DMA((n, 3))])


def run_exchange(ex, name):
    ci, co = len(ex["ins"]), len(ex["out_shape"])

    def body(*refs):
        ins, outs, sems = refs[:ci], refs[ci:ci + co], refs[ci + co:]
        for phase in PHASES:
            if phase in ex:
                ex[phase](ins, outs, sems)

    return pl.pallas_call(body, name=name, in_specs=_any(ci), out_specs=_any(co), out_shape=ex["out_shape"],
                          input_output_aliases=ex["aliases"], scratch_shapes=ex["scratch"])(*ex["ins"])


def _grid_call(body, name, grid, in_specs, out_specs, out_shape, args, sem, ex=None):
    grid = (grid,) if isinstance(grid, int) else tuple(grid)
    sems_of = (sem,) * len(grid) if isinstance(sem, str) else tuple(sem)
    n_in, n_out = len(in_specs), len(out_specs)
    if ex is None:
        return pl.pallas_call(body, name=name, grid=grid, in_specs=in_specs, out_specs=out_specs, out_shape=out_shape,
                              compiler_params=_params(*sems_of))(*args), []
    ci, co = len(ex["ins"]), len(ex["out_shape"])

    first, last = (0,) * len(grid), tuple(g - 1 for g in grid)
    late = (max(grid[0] - 2, 0),) + last[1:]
    steps = dict(start=first, late=last if late == first else late, finish=last)

    def at(ids):
        return functools.reduce(jnp.logical_and, [pl.program_id(ax) == v for ax, v in enumerate(ids)])

    def carrying(*refs):
        c_in, c_out = refs[n_in:n_in + ci], refs[n_in + ci + n_out:n_in + ci + n_out + co]
        sems = refs[n_in + ci + n_out + co:]
        for phase in PHASES:
            if phase == "finish":
                body(*refs[:n_in], *refs[n_in + ci:n_in + ci + n_out])
            if phase in ex:
                pl.when(at(steps[phase]))(functools.partial(ex[phase], c_in, c_out, sems))

    outs = pl.pallas_call(
        carrying, name=name, grid=grid, in_specs=list(in_specs) + _any(ci), out_specs=list(out_specs) + _any(co),
        out_shape=list(out_shape) + ex["out_shape"], scratch_shapes=ex["scratch"],
        input_output_aliases={n_in + i: n_out + j for i, j in ex["aliases"].items()},
        compiler_params=_params(*["arbitrary"] * len(grid)),
    )(*args, *ex["ins"])
    return outs[:n_out], outs[n_out:]


def both(*exchanges):
    exchanges = [ex for ex in exchanges if ex is not None]
    if len(exchanges) < 2:
        return exchanges[0] if exchanges else None
    n_ins = [len(ex["ins"]) for ex in exchanges]
    n_outs = [len(ex["out_shape"]) for ex in exchanges]
    n_sems = [len(ex["scratch"]) for ex in exchanges]

    def parts(seq, counts, k):
        first = sum(counts[:k])
        return seq[first:first + counts[k]]

    def run(phase):
        def go(ins, outs, sems):
            for k, ex in enumerate(exchanges):
                if phase in ex:
                    ex[phase](parts(ins, n_ins, k), parts(outs, n_outs, k), parts(sems, n_sems, k))
        return go

    aliases = {sum(n_ins[:k]) + i: sum(n_outs[:k]) + j for k, ex in enumerate(exchanges) for i, j in ex["aliases"].items()}
    return dict(ins=[a for ex in exchanges for a in ex["ins"]], out_shape=[o for ex in exchanges for o in ex["out_shape"]],
                aliases=aliases, scratch=[s for ex in exchanges for s in ex["scratch"]], **{ph: run(ph) for ph in PHASES})


def split_outputs(got, *exchanges):
    got, out = list(got), []
    for ex in exchanges:
        n = len(ex["out_shape"]) if ex is not None else 0
        out.append(got[:n])
        got = got[n:]
    return out


def pair_exchange(g16):
    n = len(g16)

    def copies(a16, got, sems):
        x, y, c = _place()
        out = []
        for ti in range(n):
            rh = a16[ti].shape[1] // 2
            out.append(pltpu.make_async_remote_copy(
                src_ref=a16[ti].at[:, pl.ds((1 - c) * rh, rh), :], dst_ref=got[ti], send_sem=sems[0].at[ti],
                recv_sem=sems[1].at[ti], device_id=(x, y, 1 - c), device_id_type=MESH))
        return out

    def start(a16, got, sems):
        for cp in copies(a16, got, sems):
            cp.start()

    def finish(a16, got, sems):
        for cp in copies(a16, got, sems):
            cp.wait()

    return dict(ins=list(g16), out_shape=[jax.ShapeDtypeStruct((a.shape[0], a.shape[1] // 2, a.shape[2]), BF16) for a in g16],
                aliases={}, start=start, finish=finish, scratch=[pltpu.SemaphoreType.DMA((n,)), pltpu.SemaphoreType.DMA((n,))])


def pair_gather(halves, name):
    n = len(halves)

    def body(*refs):
        bufs = refs[n:2 * n]
        send_sems, recv_sems = refs[2 * n:]
        x, y, c = _place()
        copies = []
        for ti in range(n):
            rh = bufs[ti].shape[0] // 2
            rows = bufs[ti].at[pl.ds(c * rh, rh), :]
            copies.append(pltpu.make_async_remote_copy(src_ref=rows, dst_ref=rows, send_sem=send_sems.at[ti],
                                                       recv_sem=recv_sems.at[ti], device_id=(x, y, 1 - c), device_id_type=MESH))
        for cp in copies:
            cp.start()
        for ti, cp in enumerate(copies):
            cp.wait_send()
            rh = bufs[ti].shape[0] // 2
            theirs = bufs[ti].at[pl.ds((1 - c) * rh, rh), :]
            pltpu.make_async_remote_copy(src_ref=theirs, dst_ref=theirs, send_sem=send_sems.at[ti], recv_sem=recv_sems.at[ti],
                                         device_id=(x, y, 1 - c), device_id_type=MESH).wait_recv()

    return pl.pallas_call(
        body, name=name, in_specs=_any(n), out_specs=_any(n), input_output_aliases={i: i for i in range(n)},
        out_shape=[jax.ShapeDtypeStruct(a.shape, a.dtype) for a in halves],
        scratch_shapes=[pltpu.SemaphoreType.DMA((n,)), pltpu.SemaphoreType.DMA((n,))],
    )(*halves)


def reduce_small(dm_f1, dm_mix, dm_gate, dm_f2, loss_blk, name):
    def body(f1_ref, mix_ref, gate_ref, f2_ref, l_ref, tot_ref, rows_ref, fin_ref):
        rows_ref[...] = jnp.zeros_like(rows_ref)
        tot_ref[...] = jnp.zeros_like(tot_ref)
        mod_src = [(f1_ref, 0), (f1_ref, 1), (f1_ref, 2), (mix_ref, 0), (mix_ref, 1), (gate_ref, 2),
                   (f2_ref, 0), (f2_ref, 1), (f2_ref, 2)]
        norm_src = [(f1_ref, 3), (mix_ref, 3), (f2_ref, 3)]
        for l in range(2):
            for k, (ref, r) in enumerate(mod_src + norm_src):
                lat = ref[0, l, 0, r:r + 1, :]
                ctx = ref[0, l, 1, r:r + 1, :]
                for dev in range(N_DEV):
                    if dev:
                        lat = lat + ref[dev, l, 0, r:r + 1, :]
                        ctx = ctx + ref[dev, l, 1, r:r + 1, :]
                    if k < N_MOD:
                        rows_ref[l, dev, k:k + 1, :] = ref[dev, l, 0, r:r + 1, :]
                if k < N_MOD:
                    rows_ref[l, N_DEV, k:k + 1, :] = ctx
                tot_ref[l, k:k + 1, :] = lat + ctx
        acc = l_ref[0]
        for dev in range(1, N_DEV):
            acc = acc + l_ref[dev]
        loss = (0.5 / D) * jnp.sum(acc[1:2, :], axis=1, keepdims=True)
        row = lax.broadcasted_iota(jnp.int32, (8, D), 0)
        fin_ref[...] = jnp.where(row == 0, acc[0:1, :], loss)

    return pl.pallas_call(
        body, name=name,
        out_shape=[jax.ShapeDtypeStruct((2, 16, D), F32), jax.ShapeDtypeStruct((2, 16, 16, D), F32),
                   jax.ShapeDtypeStruct((8, D), F32)],
        compiler_params=_params(),
    )(dm_f1, dm_mix, dm_gate, dm_f2, loss_blk)


def rope_tables(t, s):
    rows = t // GRID_W
    row = jnp.repeat(jnp.arange(rows), GRID_W).astype(F32)
    col = jnp.tile(jnp.arange(GRID_W), rows).astype(F32)
    inv = ROPE_BASE ** (-jnp.arange(0, HEAD // 2, 2, dtype=F32) / (HEAD // 2))
    ang = jnp.concatenate([row[:, None] * inv, col[:, None] * inv], axis=-1)
    cos, sin = jnp.cos(ang), jnp.sin(ang)
    cos = jnp.concatenate([jnp.tile(cos, (1, 4)), jnp.ones((s - t, BLK), F32)], axis=0)
    sin = jnp.concatenate([jnp.tile(jnp.concatenate([-sin, sin], axis=1), (1, 2)), jnp.zeros((s - t, BLK), F32)], axis=0)
    return cos, sin


BIG = ("ffn1_in", "ffn1_out", "w_in", "w_out", "ffn2_in", "ffn2_out")
GROUPS = dict(ffn1=("ffn1_in", "ffn1_out"), mix=("w_in", "w_out"), ffn2=("ffn2_in", "ffn2_out"))
GATHER_BEHIND = {("ffn1", 0): [("w_in", 0), ("w_out", 0), ("ffn2_out", 0), ("ffn1_out", 1)], ("mix", 0): [("ffn2_in", 0)],
                 ("ffn2", 0): [("ffn1_in", 1), ("w_in", 1), ("w_out", 1)], ("ffn1", 1): [("ffn2_in", 1)],
                 ("mix", 1): [("ffn2_out", 1)]}


def _slot_major(name, g):
    if name == "w_in":
        return jnp.stack(jnp.split(g, N_SLOT, axis=1), axis=0)
    if name in ("ffn1_in", "ffn2_in"):
        return g
    return g.reshape(N_SLOT, g.shape[0] // N_SLOT, g.shape[1])


def _whole_weight(name, buf):
    if name == "w_in":
        return buf.transpose(1, 0, 2).reshape(D, PROJ_W)
    if name in ("ffn1_in", "ffn2_in"):
        return buf
    return buf.reshape(-1, buf.shape[2])


def local_step(x1, ctx1, target, mods, norms, nfinal, placed, w_pool, pool_scale, sink, place, small_blocks):
    t, s = x1.shape[0], x1.shape[0] + ctx1.shape[0]
    n_lat = t // TM
    cos, sin = rope_tables(t, s)
    tables = mix_tables(t, s)
    wts ={name: list(pair) for name, pair in placed.items()}

    def gather(tensors):
        return gather_exchange([wts[name][l] for name, l in tensors])

    def gathered(tensors, arrays):
        for (name, l), whole in zip(tensors, arrays):
            wts[name][l] = whole

    def weight(name, l):
        return _whole_weight(name, wts[name][l])

    def fwd_ex(grp, l):
        groups = GATHER_BEHIND.get((grp, l))
        return (groups, gather(groups)) if groups else (None, None)

    first = [("ffn1_in", 0), ("ffn1_out", 0)]
    gathered(first, run_exchange(gather(first), "gather_first"))
    h = jnp.concatenate([x1, ctx1], axis=0)
    saved = []
    for l in range(2):
        h0 = h
        groups, ex = fwd_ex("ffn1", l)
        (h1, ab1, f1), got = ffn_fwd(h0, mods, norms[0], weight("ffn1_in", l), weight("ffn1_out", l), l, 0, n_lat, f"ffn1_fwd_{l}", ex)
        gathered(groups or [], got)
        u, q, k, v = proj_fwd(h1, mods, norms[1], weight("w_in", l), cos, sin, l, n_lat, f"proj_fwd_{l}")
        groups, ex = fwd_ex("mix", l)
        (h2, cat, lse, mo), got = mix_fwd(h1, q, k, v, u, w_pool, pool_scale, sink, weight("w_out", l), mods, tables, l, t,
                                          f"mix_fwd_{l}", ex)
        gathered(groups or [], got)
        groups, ex = fwd_ex("ffn2", l)
        (h, ab2, f2), got = ffn_fwd(h2, mods, norms[2], weight("ffn2_in", l), weight("ffn2_out", l), l, 6, n_lat, f"ffn2_fwd_{l}", ex)
        gathered(groups or [], got)
        saved.append((h0, ab1, f1, h1, u, q, k, v, cat, lse, mo, h2, ab2, f2))
    dh, loss_blk = loss_head(h, target, nfinal, t, "loss_head")

    halves = {name: [None, None] for name in BIG}
    pending = []

    def summed_in_pair(grp, l, name_a, g_a, name_b, wgrad_b):
        g_b, got_a = wgrad_b(pair_exchange([_slot_major(name_a, g_a[1])]))
        got_b = run_exchange(pair_exchange([_slot_major(name_b, g_b[1])]), f"pair_exchange_{name_b}_{l}")
        by = {name_a: (g_a[0], got_a[0]), name_b: (g_b[0], got_b[0])}
        pending.append((grp, l, [pair_sum(_slot_major(n, by[n][0]), by[n][1], place, f"pair_sum_{n}_{l}") for n in GROUPS[grp]]))

    def scatter():
        return scatter_exchange([p16 for _, p16 in pending[0][2]]) if pending else None

    def scattered(got):
        if pending:
            grp, l, pairs = pending.pop(0)
            for i, name in enumerate(GROUPS[grp]):
                halves[name][l] = chip_sum(pairs[i][0], got[3 * i:3 * i + 3], place, f"chip_sum_{name}_{l}")

    small = [None, None]
    for l in (1, 0):
        h0, ab1, f1, h1, u, q, k, v, cat, lse, mo, h2, ab2, f2 = saved[l]
        (dh, dab, df, n, act, dm_f2), got = ffn_bwd(h2, ab2, f2, dh, mods, norms[2], weight("ffn2_in", l), weight("ffn2_out", l),
                                                    l, 6, n_lat, f"ffn2_bwd_{l}", scatter())
        scattered(got)
        g_in, _ = wgrad(n, dab, D, FF_COLS, FF_COLS, f"ffn2_in_wgrad_{l}")
        summed_in_pair("ffn2", l, "ffn2_in", g_in, "ffn2_out",
                       lambda ex, a=act, b=df: wgrad(a, b, D_FF // 2, D, None, f"ffn2_out_wgrad_{l}", ex))
        (dq, dk, dv, du, dmo, dwp, dps, dsink, dm_gate), got = mix_bwd(
            dh, mo, q, k, v, u, lse, w_pool, pool_scale, sink, weight("w_out", l), mods, tables, l, t, f"mix_bwd_{l}", scatter())
        scattered(got)
        g_wo, _ = wgrad(cat, dmo, POOL_W + ATTN_W, D, None, f"w_out_wgrad_{l}")
        dh, dp, n, dm_mix = proj_bwd(h1, du, dq, dk, dv, dh, mods, norms[1], weight("w_in", l), cos, sin, l, n_lat, f"proj_bwd_{l}")
        summed_in_pair("mix", l, "w_out", g_wo, "w_in",
                       lambda ex, a=n, b=dp: wgrad(a, b, D, PROJ_W // 2, None, f"w_in_wgrad_{l}", ex))
        (dh, dab, df, n, act, dm_f1), got = ffn_bwd(h0, ab1, f1, dh, mods, norms[0], weight("ffn1_in", l), weight("ffn1_out", l),
                                                    l, 0, n_lat, f"ffn1_bwd_{l}", scatter())
        scattered(got)
        small[l] = dict(dm_f1=dm_f1, dm_mix=dm_mix, dm_gate=dm_gate, dm_f2=dm_f2, dwp=dwp, dps=dps, dsink=dsink)
        if l:
            g_in, _ = wgrad(n, dab, D, FF_COLS, FF_COLS, f"ffn1_in_wgrad_{l}")
            summed_in_pair("ffn1", l, "ffn1_in", g_in, "ffn1_out",
                           lambda ex, a=act, b=df: wgrad(a, b, D_FF // 2, D, None, f"ffn1_out_wgrad_{l}", ex))
    g_out, small_all = wgrad(act, df, D_FF // 2, D, None, "ffn1_out_wgrad_0", gather8_exchange(small_blocks(small, loss_blk)))
    got = run_exchange(pair_exchange([_slot_major("ffn1_out", g_out[1])]), "pair_exchange_ffn1_out_0")
    p32, p16 = pair_sum(_slot_major("ffn1_out", g_out[0]), got[0], place, "pair_sum_ffn1_out_0")
    g_in, got = wgrad(n, dab, D, FF_COLS, FF_COLS, "ffn1_in_wgrad_0", scatter_exchange([p16]))
    halves["ffn1_out"][0] = chip_sum(p32, got, place, "chip_sum_ffn1_out_0")
    got = run_exchange(pair_exchange([_slot_major("ffn1_in", g_in[1])]), "pair_exchange_ffn1_in_0")
    return dh[:t], halves, pair_sum(_slot_major("ffn1_in", g_in[0]), got[0], place, "pair_sum_ffn1_in_0"), small_all


def _silu_grad(z):
    sg = jax.nn.sigmoid(z)
    return sg * (1 + z * (1 - sg))


def kernel(x, c, ctx, c_ctx, w_mod, b_mod, norm_ffn1, w_ffn1_in, w_ffn1_out, norm_mix, w_in, w_pool, pool_scale, sink, w_out, norm_ffn2, w_ffn2_in, w_ffn2_out, norm_final, loss_target, m_c_ctx, m_w_mod, m_b_mod, m_norm_ffn1, m_w_ffn1_in, m_w_ffn1_out, m_norm_mix, m_w_in, m_w_pool, m_pool_scale, m_sink, m_w_out, m_norm_ffn2, m_w_ffn2_in, m_w_ffn2_out, m_norm_final, v_c_ctx, v_w_mod, v_b_mod, v_norm_ffn1, v_w_ffn1_in, v_w_ffn1_out, v_norm_mix, v_w_in, v_w_pool, v_pool_scale, v_sink, v_w_out, v_norm_ffn2, v_w_ffn2_in, v_w_ffn2_out, v_norm_final):
    px, py, pc = _place()
    slot, me = 2 * px + py, 4 * px + 2 * py + pc
    n_grp = len(POOL_WINDOWS)

    (c_rows,) = all_gather([c.reshape(8, D // 8)], "gather_c")
    c_all = jnp.concatenate([c_rows.reshape(N_DEV, D), c_ctx.reshape(1, D), jnp.zeros((16 - N_DEV - 1, D), F32)], axis=0)
    b_cols = lax.dynamic_slice(b_mod, (0, slot * MOD_COLS), (2, MOD_COLS)).reshape(2, 1, MOD_COLS)
    (mod_parts,) = all_gather([mod_rows(c_all, w_mod, b_cols, "mod_rows")], "gather_mods")
    mods_all = mod_parts[0::2].transpose(1, 2, 0, 3).reshape(2, 16, N_MOD * D)
    mx = lax.dynamic_slice(mods_all, (0, me, 0), (2, 1, N_MOD * D)).reshape(2, N_MOD, D)
    mc = mods_all[:, N_DEV].reshape(2, N_MOD, D)
    pad = jnp.zeros((2, 16 - N_MOD, D), F32)
    mods = jnp.stack([jnp.concatenate([mx, pad], axis=1), jnp.concatenate([mc, pad], axis=1)], axis=1)

    place = jnp.stack([pc, slot]).astype(jnp.int32)
    shards = dict(ffn1_in=w_ffn1_in, ffn1_out=w_ffn1_out, w_in=w_in, w_out=w_out, ffn2_in=w_ffn2_in, ffn2_out=w_ffn2_out)
    placed = {name: [cast_place(shards[name], l, place, f"cast_{name}_{l}") for l in range(2)] for name in BIG}
    norms = [g.reshape(2, 1, D) for g in (norm_ffn1, norm_mix, norm_ffn2)]
    row_sums = ("dm_f1", "dm_mix", "dm_gate", "dm_f2")

    def small_blocks(small, loss_blk):
        stacked = {k: jnp.stack([small[0][k], small[1][k]]) for k in row_sums + ("dwp", "dps", "dsink")}
        return ([stacked[k].reshape(32, D) for k in row_sums]
                + [stacked["dwp"].reshape(2 * n_grp * GROUP, GROUP), stacked["dps"].reshape(16, POOL_W),
                   stacked["dsink"].reshape(16, BLK), loss_blk])

    dx, halves, last_pair, small_all = local_step(x[0], ctx[0], loss_target[0], mods, norms, norm_final.reshape(1, D), placed,
                                                   w_pool.astype(BF16), pool_scale.reshape(2, 1, POOL_W), sink, place, small_blocks)
    grads = {}

    *g_dm, g_dwp, g_dps, g_dsink, g_loss = small_all
    tot, rows, fin = reduce_small(*[g.reshape(N_DEV, 2, 2, 8, D) for g in g_dm], g_loss, "reduce_small")
    s_dwp, s_dps, s_dsink = sum8([g_dwp, g_dps, g_dsink], "sum_pool_sink")
    grads.update(
        w_pool=s_dwp.reshape(2, n_grp, GROUP, GROUP), pool_scale=s_dps.reshape(2, 8, POOL_W)[:, 0],
        sink=s_dsink.reshape(2, 8, BLK)[:, 0, :N_HEADS], b_mod=tot[:, :N_MOD].reshape(2, N_MOD * D),
        norm_ffn1=tot[:, N_MOD], norm_mix=tot[:, N_MOD + 1], norm_ffn2=tot[:, N_MOD + 2], norm_final=fin[0])
    loss = fin[1, 0]

    dmod_cols = lax.dynamic_slice(rows[:, :, :N_MOD, :].reshape(2, 16, N_MOD * D), (0, 0, slot * MOD_COLS), (2, 16, MOD_COLS))
    grads["w_mod"], dc = mod_grads(c_all, dmod_cols, w_mod, "mod_grads")
    (g_dc,) = all_gather([dc], "gather_dc")
    (s_dc,) = sum8([g_dc], "sum_dc")
    (d_c_ctx,) = elementwise(lambda d, z: (0.5 * d * _silu_grad(z),), [s_dc[N_DEV:N_DEV + 1], c_ctx.reshape(1, D)], [F32], "c_ctx_grad")
    grads["c_ctx"] = d_c_ctx.reshape(D)

    given = dict(c_ctx=(c_ctx, m_c_ctx, v_c_ctx), w_mod=(w_mod, m_w_mod, v_w_mod), b_mod=(b_mod, m_b_mod, v_b_mod),
                 norm_ffn1=(norm_ffn1, m_norm_ffn1, v_norm_ffn1), w_ffn1_in=(w_ffn1_in, m_w_ffn1_in, v_w_ffn1_in),
                 w_ffn1_out=(w_ffn1_out, m_w_ffn1_out, v_w_ffn1_out), norm_mix=(norm_mix, m_norm_mix, v_norm_mix),
                 w_in=(w_in, m_w_in, v_w_in), w_pool=(w_pool, m_w_pool, v_w_pool),
                 pool_scale=(pool_scale, m_pool_scale, v_pool_scale), sink=(sink, m_sink, v_sink), w_out=(w_out, m_w_out, v_w_out),
                 norm_ffn2=(norm_ffn2, m_norm_ffn2, v_norm_ffn2), w_ffn2_in=(w_ffn2_in, m_w_ffn2_in, v_w_ffn2_in),
                 w_ffn2_out=(w_ffn2_out, m_w_ffn2_out, v_w_ffn2_out), norm_final=(norm_final, m_norm_final, v_norm_final))
    got = run_exchange(scatter_exchange([last_pair[1]]), "scatter_last")
    halves["ffn1_in"][0] = chip_sum(last_pair[0], got, place, "chip_sum_ffn1_in_0")
    order = [(name, l) for name in BIG for l in range(2)]
    shard = dict(zip(order, pair_gather([halves[name][l] for name, l in order], "grad_pair_gather")))

    g_out, d_out, m_out, v_out = [], [], [], []
    for name, (w, m, v) in given.items():
        if name in BIG or name[2:] in BIG:
            key = name if name in BIG else name[2:]
            grad, delta, new_m, new_v = adamw_layers(w, shard[key, 0], shard[key, 1], m, v, f"adamw_{name}")
        else:
            grad = grads[name]
            delta, new_m, new_v = adamw(w, grad, m, v, f"adamw_{name}")
        g_out.append(grad)
        d_out.append(delta)
        m_out.append(new_m)
        v_out.append(new_v)
    return (loss, dx[None], *g_out, *d_out, *m_out, *v_out)
```

```python
import functools

import jax
import jax.numpy as jnp
from jax import lax
from jax.experimental import pallas as pl
from jax.experimental.pallas import tpu as pltpu

F32, BF16 = jnp.float32, jnp.bfloat16
D = 1024
D_FF = 2816
N_SLOT = 4
FF_COLS = 2 * D_FF // N_SLOT
N_MOD = 9
MOD_COLS = N_MOD * D // N_SLOT
POOL_W, ATTN_W, KV_W = 512, 512, 128
PROJ_W = POOL_W + ATTN_W + 2 * KV_W
N_HEADS, Q_GROUP, HEAD = 8, 4, 64
GROUP = 128
POOL_WINDOWS = (2, 4, 8, 16)
BLK = 128
QB = 256
WIN = QB + 2 * BLK
GRID_W = 64
ROPE_BASE = 10000.0
EPS = 1e-6
NEG_INF = -1e30
TM = 256
N_DEV = 8
VMEM_LIMIT_BYTES = 56 * 1024 * 1024
ADAM_LR, ADAM_B1, ADAM_B2, ADAM_EPS, ADAM_WD, ADAM_STEP = 0.001, 0.9, 0.999, 1e-08, 0.01, 10
MESH = pl.DeviceIdType.MESH
NT = (((1,), (1,)), ((), ()))
TN = (((0,), (0,)), ((), ()))


def _params(*sem):
    return pltpu.CompilerParams(dimension_semantics=sem, vmem_limit_bytes=VMEM_LIMIT_BYTES)


def _whole(shape, lead=()):
    idx = tuple(lead) + (0,) * len(shape)
    return pl.BlockSpec((None,) * len(lead) + tuple(shape), lambda *_: idx, pipeline_mode=pl.Buffered(1))


def _rows(cols, tm=TM):
    return pl.BlockSpec((tm, cols), lambda i: (i, 0))


def _mods_spec(layer, n_lat):
    return pl.BlockSpec((None, None, 16, D), lambda i: (layer, (i >= n_lat).astype(jnp.int32), 0, 0))


def _acc_spec(n_lat):
    return pl.BlockSpec((None, 8, D), lambda i: ((i >= n_lat).astype(jnp.int32), 0, 0))


def _dot(a, b):
    return jnp.dot(a, b, preferred_element_type=F32)


def _dotg(a, b, dims):
    return lax.dot_general(a, b, dims, preferred_element_type=F32)


def _sum0(v):
    return jnp.sum(v, axis=0, keepdims=True)


def _norm_mod(h, g, shift, scale):
    r = lax.rsqrt(jnp.mean(h * h, axis=-1, keepdims=True) + EPS)
    xhat = h * r
    y = xhat * g
    return y * (1 + scale) + shift, xhat, r, y


def _norm_mod_bwd(dn, xhat, r, y, g, scale):
    dy = dn * (1 + scale)
    dx = dy * g
    dh = r * (dx - xhat * jnp.mean(dx * xhat, axis=-1, keepdims=True))
    return _sum0(dn), _sum0(dn * y), _sum0(dy * xhat), dh


def _swap_halves(v):
    w = v.shape[1]
    lane = lax.broadcasted_iota(jnp.int32, v.shape, 1)
    return jnp.where(lane % HEAD < HEAD // 2, pltpu.roll(v, w - HEAD // 2, axis=1), pltpu.roll(v, HEAD // 2, axis=1))


def _tile_lanes(t, width):
    return t if width == t.shape[1] else jnp.concatenate([t] * (width // t.shape[1]), axis=1)


def _rope(v, cos, sin):
    return v * _tile_lanes(cos, v.shape[1]) + _swap_halves(v) * _tile_lanes(sin, v.shape[1])


def _unrope(g, cos, sin):
    return g * _tile_lanes(cos, g.shape[1]) + _swap_halves(g * _tile_lanes(sin, g.shape[1]))


def ffn_fwd(h, mods, g, w4, wo, layer, k0, n_lat, name, ex=None):
    s = h.shape[0]

    def body(h_ref, m_ref, g_ref, w_ref, wo_ref, ho_ref, ab_ref, f_ref):
        hh = h_ref[...]
        n, _, _, _ = _norm_mod(hh, g_ref[...], m_ref[k0:k0 + 1, :], m_ref[k0 + 1:k0 + 2, :])
        nb = n.astype(BF16)
        acc = jnp.zeros((TM, D), F32)
        for j in range(2):
            a = _dot(nb, w_ref[j])
            b = _dot(nb, w_ref[2 + j])
            ab_ref[:, j * FF_COLS:(j + 1) * FF_COLS] = a.astype(BF16)
            ab_ref[:, (2 + j) * FF_COLS:(3 + j) * FF_COLS] = b.astype(BF16)
            act = (a * jax.nn.sigmoid(a) * b).astype(BF16)
            acc = acc + _dot(act, wo_ref[j * FF_COLS:(j + 1) * FF_COLS, :])
        f_ref[...] = acc
        ho_ref[...] = hh + 0.5 * m_ref[k0 + 2:k0 + 3, :] * acc

    return _grid_call(
        body, name, s // TM,
        [_rows(D), _mods_spec(layer, n_lat), _whole((1, D), (layer,)), _whole((N_SLOT, D, FF_COLS)), _whole((D_FF, D))],
        [_rows(D), _rows(2 * D_FF), _rows(D)],
        [jax.ShapeDtypeStruct((s, D), F32), jax.ShapeDtypeStruct((s, 2 * D_FF), BF16), jax.ShapeDtypeStruct((s, D), F32)],
        (h, mods, g, w4, wo), "parallel", ex)


def ffn_bwd(h, ab, f, dh, mods, g, w4, wo, layer, k0, n_lat, name, ex=None):
    s = h.shape[0]

    def body(h_ref, ab_ref, f_ref, dh_ref, m_ref, g_ref, w_ref, wo_ref, dhi_ref, dab_ref, df_ref, n_ref, act_ref, dm_ref):
        i = pl.program_id(0)

        @pl.when((i == 0) | (i == n_lat))
        def _():
            dm_ref[...] = jnp.zeros_like(dm_ref)

        hh, dho, gg = h_ref[...], dh_ref[...], g_ref[...]
        scale, gate = m_ref[k0 + 1:k0 + 2, :], m_ref[k0 + 2:k0 + 3, :]
        n, xhat, r, y = _norm_mod(hh, gg, m_ref[k0:k0 + 1, :], scale)
        n_ref[...] = n.astype(BF16)
        dgate = _sum0(dho * (0.5 * f_ref[...]))
        dfb = ((0.5 * gate) * dho).astype(BF16)
        df_ref[...] = dfb
        dn = jnp.zeros((TM, D), F32)
        for j in range(2):
            a = ab_ref[:, j * FF_COLS:(j + 1) * FF_COLS].astype(F32)
            b = ab_ref[:, (2 + j) * FF_COLS:(3 + j) * FF_COLS].astype(F32)
            sg = jax.nn.sigmoid(a)
            sa = a * sg
            act_ref[:, j * FF_COLS:(j + 1) * FF_COLS] = (sa * b).astype(BF16)
            dact = _dotg(dfb, wo_ref[j * FF_COLS:(j + 1) * FF_COLS, :], NT)
            da = (dact * b * (sg * (1 + a * (1 - sg)))).astype(BF16)
            db = (dact * sa).astype(BF16)
            dab_ref[:, j * FF_COLS:(j + 1) * FF_COLS] = da
            dab_ref[:, (2 + j) * FF_COLS:(3 + j) * FF_COLS] = db
            dn = dn + _dotg(da, w_ref[j], NT) + _dotg(db, w_ref[2 + j], NT)
        dsh, dsc, dg, dhn = _norm_mod_bwd(dn, xhat, r, y, gg, scale)
        dhi_ref[...] = dho + dhn
        dm_ref[0:1, :] += dsh
        dm_ref[1:2, :] += dsc
        dm_ref[2:3, :] += dgate
        dm_ref[3:4, :] += dg

    return _grid_call(
        body, name, s // TM,
        [_rows(D), _rows(2 * D_FF), _rows(D), _rows(D), _mods_spec(layer, n_lat), _whole((1, D), (layer,)),
         _whole((N_SLOT, D, FF_COLS)), _whole((D_FF, D))],
        [_rows(D), _rows(2 * D_FF), _rows(D), _rows(D), _rows(D_FF), _acc_spec(n_lat)],
        [jax.ShapeDtypeStruct((s, D), F32), jax.ShapeDtypeStruct((s, 2 * D_FF), BF16), jax.ShapeDtypeStruct((s, D), BF16),
         jax.ShapeDtypeStruct((s, D), BF16), jax.ShapeDtypeStruct((s, D_FF), BF16), jax.ShapeDtypeStruct((2, 8, D), F32)],
        (h, ab, f, dh, mods, g, w4, wo), "arbitrary", ex)


def _token_tile(s, limit=2176):
    return max(ts for ts in range(16, limit + 1, 16) if s % ts == 0)


def wgrad(a, b, tk, tn, slot_cols, name, ex=None):
    s, k = a.shape
    n = b.shape[1]
    ts = _token_tile(s)
    steps = s // ts

    def body(a_ref, b_ref, o_ref, o16_ref):
        r = _dotg(a_ref[...], b_ref[...], TN)
        si = pl.program_id(2)

        @pl.when(si == 0)
        def _():
            o_ref[...] = r

        @pl.when(si > 0)
        def _():
            o_ref[...] += r

        @pl.when(si == steps - 1)
        def _():
            o16_ref[...] = o_ref[...].astype(BF16)

    if slot_cols is None:
        shape, spec = (k, n), pl.BlockSpec((tk, tn), lambda i, j, si: (i, j))
    else:
        per = slot_cols // tn
        shape, spec = (n // slot_cols, k, slot_cols), pl.BlockSpec((None, tk, tn), lambda i, j, si: (lax.div(j, per), i, lax.rem(j, per)))
    return _grid_call(
        body, name, (k // tk, n // tn, steps),
        [pl.BlockSpec((ts, tk), lambda i, j, si: (si, i)), pl.BlockSpec((ts, tn), lambda i, j, si: (si, j))], [spec, spec],
        [jax.ShapeDtypeStruct(shape, F32), jax.ShapeDtypeStruct(shape, BF16)], (a, b), ("parallel", "parallel", "arbitrary"), ex)


def proj_fwd(h, mods, g, w_in, cos, sin, layer, n_lat, name, ex=None):
    s = h.shape[0]

    def body(h_ref, m_ref, g_ref, w_ref, cos_ref, sin_ref, u_ref, q_ref, k_ref, v_ref):
        n, _, _, _ = _norm_mod(h_ref[...], g_ref[...], m_ref[3:4, :], m_ref[4:5, :])
        p = _dot(n.astype(BF16), w_ref[...])
        cs, sn = cos_ref[...], sin_ref[...]
        u_ref[...] = p[:, :POOL_W]
        q_ref[...] = (_rope(p[:, POOL_W:POOL_W + ATTN_W], cs, sn) * HEAD ** -0.5).astype(BF16)
        k_ref[...] = _rope(p[:, POOL_W + ATTN_W:POOL_W + ATTN_W + KV_W], cs, sn).astype(BF16)
        v_ref[...] = p[:, POOL_W + ATTN_W + KV_W:].astype(BF16)

    return _grid_call(
        body, name, s // TM,
        [_rows(D), _mods_spec(layer, n_lat), _whole((1, D), (layer,)), _whole((D, PROJ_W)), _rows(BLK), _rows(BLK)],
        [_rows(POOL_W), _rows(ATTN_W), _rows(KV_W), _rows(KV_W)],
        [jax.ShapeDtypeStruct((s, POOL_W), F32), jax.ShapeDtypeStruct((s, ATTN_W), BF16),
         jax.ShapeDtypeStruct((s, KV_W), BF16), jax.ShapeDtypeStruct((s, KV_W), BF16)],
        (h, mods, g, w_in, cos, sin), "parallel", ex)


def proj_bwd(h, du, dq, dk, dv, dh, mods, g, w_in, cos, sin, layer, n_lat, name):
    s = h.shape[0]

    def body(h_ref, du_ref, dq_ref, dk_ref, dv_ref, dh_ref, m_ref, g_ref, w_ref, cos_ref, sin_ref,
             dhi_ref, dp_ref, n_ref, dm_ref):
        i = pl.program_id(0)

        @pl.when((i == 0) | (i == n_lat))
        def _():
            dm_ref[...] = jnp.zeros_like(dm_ref)

        gg, scale = g_ref[...], m_ref[4:5, :]
        n, xhat, r, y = _norm_mod(h_ref[...], gg, m_ref[3:4, :], scale)
        n_ref[...] = n.astype(BF16)
        cs, sn = cos_ref[...], sin_ref[...]
        dp = jnp.concatenate([du_ref[...], _unrope(dq_ref[...], cs, sn) * HEAD ** -0.5, _unrope(dk_ref[...], cs, sn),
                              dv_ref[...]], axis=1).astype(BF16)
        dp_ref[...] = dp
        dsh, dsc, dg, dhn = _norm_mod_bwd(_dotg(dp, w_ref[...], NT), xhat, r, y, gg, scale)
        dhi_ref[...] = dh_ref[...] + dhn
        dm_ref[0:1, :] += dsh
        dm_ref[1:2, :] += dsc
        dm_ref[3:4, :] += dg

    return pl.pallas_call(
        body, name=name, grid=(s // TM,),
        in_specs=[_rows(D), _rows(POOL_W), _rows(ATTN_W), _rows(KV_W), _rows(KV_W), _rows(D), _mods_spec(layer, n_lat),
                  _whole((1, D), (layer,)), _whole((D, PROJ_W)), _rows(BLK), _rows(BLK)],
        out_specs=[_rows(D), _rows(PROJ_W), _rows(D), _acc_spec(n_lat)],
        out_shape=[jax.ShapeDtypeStruct((s, D), F32), jax.ShapeDtypeStruct((s, PROJ_W), BF16),
                   jax.ShapeDtypeStruct((s, D), BF16), jax.ShapeDtypeStruct((2, 8, D), F32)],
        compiler_params=_params("arbitrary"),
    )(h, du, dq, dk, dv, dh, mods, g, w_in, cos, sin)


def _window(i, s):
    return pl.multiple_of(jnp.clip(i * QB - BLK, 0, s - WIN), BLK)


def mix_tables(t, s):
    n_lat = t // QB
    blocks = jnp.array([0, 1, n_lat - 1] + list(range(n_lat, s // QB)))[:, None, None]
    ws = jnp.clip(blocks * QB - BLK, 0, s - WIN)
    q = blocks * QB + jnp.arange(QB)[None, :, None]
    k = ws + jnp.arange(WIN)[None, None, :]
    is_lat = blocks < n_lat
    local = jnp.where(is_lat & (k < t) & (jnp.abs(k - q) <= BLK), 0.0, NEG_INF).astype(F32)
    bias = jnp.concatenate([local, jnp.zeros(local.shape[:2] + (s - t,), F32)], axis=2)
    seq_lo, seq_hi = jnp.where(is_lat, 0, t), jnp.where(is_lat, t, s)
    bands, counts = [], []
    for w in POOL_WINDOWS:
        lo, hi = jnp.maximum(q - w // 2, seq_lo), jnp.minimum(q + w - w // 2, seq_hi)
        bands.append((k >= lo) & (k < hi))
        counts.append((hi - lo).astype(F32))
    band = jnp.stack(bands, axis=1).astype(BF16)
    count = jnp.concatenate(counts + [jnp.ones(counts[0].shape[:2] + (BLK - len(counts),), F32)], axis=2)
    return dict(bias=bias, band=band, band_t=band.transpose(0, 1, 3, 2), count=count)


def _case_spec(table, n_lat_blk):
    def kind(i):
        return jnp.where(i < n_lat_blk - 1, jnp.minimum(i, 1), i - n_lat_blk + 3)

    shape = table.shape[1:]
    return pl.BlockSpec((None,) + shape, lambda i: (kind(i),) + (0,) * len(shape))


def _split_dot(band, v):
    return _dot(band, v.astype(BF16))


def _pooled(u_ref, band_ref, cnt_ref, i, ws, gi):
    cols = slice(gi * GROUP, (gi + 1) * GROUP)
    mean = _split_dot(band_ref[gi], u_ref[pl.ds(ws, WIN), cols]) / cnt_ref[:, gi:gi + 1]
    return mean - u_ref[pl.ds(pl.multiple_of(i * QB, QB), QB), cols]


def _head_cols(hd):
    return slice(hd * HEAD, (hd + 1) * HEAD)


def _stack_heads(x, hk, first=0):
    return jnp.concatenate([x[:, first + (Q_GROUP * hk + g) * HEAD:first + (Q_GROUP * hk + g + 1) * HEAD]
                            for g in range(Q_GROUP)], axis=0)


def _biased(scores, bias):
    return (scores.reshape(Q_GROUP, QB, -1) + bias).reshape(Q_GROUP * QB, -1)


def _group_column(vals):
    row = lax.broadcasted_iota(jnp.int32, (Q_GROUP * QB, 1), 0)
    out = jnp.full((Q_GROUP * QB, 1), vals[Q_GROUP - 1], F32)
    for g in range(Q_GROUP - 2, -1, -1):
        out = jnp.where(row < (g + 1) * QB, vals[g], out)
    return out


def _lane_place(cols, width=BLK):
    lane = lax.broadcasted_iota(jnp.int32, (cols[0].shape[0], width), 1)
    out = jnp.zeros((cols[0].shape[0], width), F32)
    for hd, c in enumerate(cols):
        out = jnp.where(lane == hd, c, out)
    return out


def mix_fwd(h, q, k, v, u, w_pool, pool_scale, sink, w_out, mods, tables, layer, t, name, ex=None):
    s = h.shape[0]
    n_lat_blk = t // QB

    def body(h_ref, q_ref, k_ref, v_ref, u_ref, wp_ref, ps_ref, sink_ref, wo_ref, m_ref, bias_ref, band_ref, cnt_ref,
             ho_ref, cat_ref, lse_ref, mo_ref):
        i = pl.program_id(0)
        ws = _window(i, s)
        for gi in range(len(POOL_WINDOWS)):
            mixed = _dot(_pooled(u_ref, band_ref, cnt_ref, i, ws, gi).astype(BF16), wp_ref[gi])
            cat_ref[:, gi * GROUP:(gi + 1) * GROUP] = (mixed * ps_ref[:, gi * GROUP:(gi + 1) * GROUP]).astype(BF16)
        bias = bias_ref[...]
        k_all = jnp.concatenate([k_ref[pl.ds(ws, WIN), :], k_ref[t:s, :]], axis=0)
        v_all = jnp.concatenate([v_ref[pl.ds(ws, WIN), :], v_ref[t:s, :]], axis=0)
        lses = []
        for hk in range(N_HEADS // Q_GROUP):
            kv = _head_cols(hk)
            sc = _biased(_dotg(_stack_heads(q_ref[...], hk), k_all[:, kv], NT), bias)
            sk = _group_column([sink_ref[layer, Q_GROUP * hk + g] for g in range(Q_GROUP)])
            m = jnp.maximum(jnp.max(sc, axis=1, keepdims=True), sk)
            e = jnp.exp(sc - m)
            l = jnp.sum(e, axis=1, keepdims=True) + jnp.exp(sk - m)
            o = _dot(e.astype(BF16), v_all[:, kv]) * (1.0 / l)
            lse = m + jnp.log(l)
            for g in range(Q_GROUP):
                hd = Q_GROUP * hk + g
                cat_ref[:, POOL_W + hd * HEAD:POOL_W + (hd + 1) * HEAD] = o[g * QB:(g + 1) * QB].astype(BF16)
                lses.append(lse[g * QB:(g + 1) * QB])
        lse_ref[...] = _lane_place(lses)
        mo = _dot(cat_ref[...], wo_ref[...])
        mo_ref[...] = mo
        ho_ref[...] = h_ref[...] + m_ref[5:6, :] * mo

    blk = lambda cols: _rows(cols, QB)
    return _grid_call(
        body, name, s // QB,
        [blk(D), blk(ATTN_W), _whole((s, KV_W)), _whole((s, KV_W)), _whole((s, POOL_W)),
         _whole((len(POOL_WINDOWS), GROUP, GROUP), (layer,)), _whole((1, POOL_W), (layer,)),
         pl.BlockSpec(memory_space=pltpu.SMEM), _whole((POOL_W + ATTN_W, D)), _mods_spec(layer, n_lat_blk),
         _case_spec(tables["bias"], n_lat_blk), _case_spec(tables["band"], n_lat_blk), _case_spec(tables["count"], n_lat_blk)],
        [blk(D), blk(POOL_W + ATTN_W), blk(BLK), blk(D)],
        [jax.ShapeDtypeStruct((s, D), F32), jax.ShapeDtypeStruct((s, POOL_W + ATTN_W), BF16), jax.ShapeDtypeStruct((s, BLK), F32),
         jax.ShapeDtypeStruct((s, D), F32)],
        (h, q, k, v, u, w_pool, pool_scale, sink, w_out, mods, tables["bias"], tables["band"], tables["count"]), "parallel", ex)


def mix_bwd(dh, mo, q, k, v, u, lse, w_pool, pool_scale, sink, w_out, mods, tables, layer, t, name, ex=None):
    s = dh.shape[0]
    n_lat_blk = t // QB
    n_grp = len(POOL_WINDOWS)

    def body(dh_ref, mo_ref, q_ref, k_ref, v_ref, u_ref, lse_ref, wp_ref, ps_ref, sink_ref, wo_ref, m_ref,
             bias_ref, band_ref, band_t_ref, cnt_ref,
             dq_ref, dk_ref, dv_ref, du_ref, dmo_ref, dwp_ref, dps_ref, dsink_ref, dm_ref):
        i = pl.program_id(0)

        @pl.when(i == 0)
        def _():
            for ref in (dk_ref, dv_ref, du_ref, dwp_ref, dps_ref, dsink_ref):
                ref[...] = jnp.zeros_like(ref)

        @pl.when((i == 0) | (i == n_lat_blk))
        def _():
            dm_ref[...] = jnp.zeros_like(dm_ref)

        ws = _window(i, s)
        here = pl.ds(pl.multiple_of(i * QB, QB), QB)
        dho = dh_ref[...]
        dm_ref[2:3, :] += _sum0(dho * mo_ref[...])
        dmo = (m_ref[5:6, :] * dho).astype(BF16)
        dmo_ref[...] = dmo
        dcat = _dotg(dmo, wo_ref[...], NT)

        for gi in range(n_grp):
            cols = slice(gi * GROUP, (gi + 1) * GROUP)
            pooled = _pooled(u_ref, band_ref, cnt_ref, i, ws, gi).astype(BF16)
            dpo = dcat[:, cols]
            dps_ref[0:1, cols] += _sum0(dpo * _dot(pooled, wp_ref[gi]))
            dmixed = (dpo * ps_ref[:, cols]).astype(BF16)
            dwp_ref[gi] += _dotg(pooled, dmixed, TN)
            dpooled = _dotg(dmixed, wp_ref[gi], NT)
            du_ref[pl.ds(ws, WIN), cols] += _split_dot(band_t_ref[gi], dpooled / cnt_ref[:, gi:gi + 1])
            du_ref[here, cols] -= dpooled

        bias = bias_ref[...]
        k_all = jnp.concatenate([k_ref[pl.ds(ws, WIN), :], k_ref[t:s, :]], axis=0)
        v_all = jnp.concatenate([v_ref[pl.ds(ws, WIN), :], v_ref[t:s, :]], axis=0)
        qq, lse_all = q_ref[...], lse_ref[...]
        dqs, dsinks, dks, dvs = [], [], [], []
        for hk in range(N_HEADS // Q_GROUP):
            kv = _head_cols(hk)
            q4 = _stack_heads(qq, hk)
            lse = jnp.concatenate([lse_all[:, Q_GROUP * hk + g:Q_GROUP * hk + g + 1] for g in range(Q_GROUP)], axis=0)
            p = jnp.exp(_biased(_dotg(q4, k_all[:, kv], NT), bias) - lse)
            do = _stack_heads(dcat, hk, POOL_W).astype(BF16)
            dp = _dotg(do, v_all[:, kv], NT)
            delta = jnp.sum(p * dp, axis=1, keepdims=True)
            ds = (p * (dp - delta)).astype(BF16)
            sk = _group_column([sink_ref[layer, Q_GROUP * hk + g] for g in range(Q_GROUP)])
            dsk = -jnp.exp(sk - lse) * delta
            dq = _dot(ds, k_all[:, kv])
            for g in range(Q_GROUP):
                dqs.append(dq[g * QB:(g + 1) * QB])
                dsinks.append(_sum0(dsk[g * QB:(g + 1) * QB]))
            dks.append(_dotg(ds, q4, TN))
            dvs.append(_dotg(p.astype(BF16), do, TN))
        dq_ref[...] = jnp.concatenate(dqs, axis=1)
        dk, dv = jnp.concatenate(dks, axis=1), jnp.concatenate(dvs, axis=1)
        dk_ref[pl.ds(ws, WIN), :] += dk[:WIN]
        dv_ref[pl.ds(ws, WIN), :] += dv[:WIN]
        dk_ref[t:s, :] += dk[WIN:]
        dv_ref[t:s, :] += dv[WIN:]
        dsink_ref[0:1, :] += _lane_place(dsinks)

    blk = lambda cols: _rows(cols, QB)
    full = lambda shape: pl.BlockSpec(shape, lambda i: (0,) * len(shape))
    return _grid_call(
        body, name, s // QB,
        [blk(D), blk(D), blk(ATTN_W), _whole((s, KV_W)), _whole((s, KV_W)), _whole((s, POOL_W)),
         blk(BLK), _whole((n_grp, GROUP, GROUP), (layer,)), _whole((1, POOL_W), (layer,)),
         pl.BlockSpec(memory_space=pltpu.SMEM), _whole((POOL_W + ATTN_W, D)), _mods_spec(layer, n_lat_blk)]
        + [_case_spec(tables[key], n_lat_blk) for key in ("bias", "band", "band_t", "count")],
        [blk(ATTN_W), full((s, KV_W)), full((s, KV_W)), full((s, POOL_W)), blk(D),
         full((n_grp, GROUP, GROUP)), full((8, POOL_W)), full((8, BLK)), _acc_spec(n_lat_blk)],
        [jax.ShapeDtypeStruct((s, ATTN_W), F32), jax.ShapeDtypeStruct((s, KV_W), F32),
         jax.ShapeDtypeStruct((s, KV_W), F32), jax.ShapeDtypeStruct((s, POOL_W), F32),
         jax.ShapeDtypeStruct((s, D), BF16), jax.ShapeDtypeStruct((n_grp, GROUP, GROUP), F32),
         jax.ShapeDtypeStruct((8, POOL_W), F32), jax.ShapeDtypeStruct((8, BLK), F32), jax.ShapeDtypeStruct((2, 8, D), F32)],
        (dh, mo, q, k, v, u, lse, w_pool, pool_scale, sink, w_out, mods, tables["bias"], tables["band"], tables["band_t"],
         tables["count"]), "arbitrary", ex)


def loss_head(h, target, g, t, name):
    s = h.shape[0]
    n_lat = t // TM

    def body(h_ref, t_ref, g_ref, dh_ref, acc_ref):
        i = pl.program_id(0)

        @pl.when(i == 0)
        def _():
            acc_ref[...] = jnp.zeros_like(acc_ref)

        @pl.when(i < n_lat)
        def _():
            hh, gg = h_ref[...], g_ref[...]
            r = lax.rsqrt(jnp.mean(hh * hh, axis=-1, keepdims=True) + EPS)
            xhat = hh * r
            err = xhat * gg - t_ref[...]
            dy = err * (1.0 / D)
            dx = dy * gg
            dh_ref[...] = r * (dx - xhat * jnp.mean(dx * xhat, axis=-1, keepdims=True))
            acc_ref[0:1, :] += _sum0(dy * xhat)
            acc_ref[1:2, :] += _sum0(err * err)

        @pl.when(i >= n_lat)
        def _():
            dh_ref[...] = jnp.zeros_like(dh_ref)

    return pl.pallas_call(
        body, name=name, grid=(s // TM,),
        in_specs=[_rows(D), pl.BlockSpec((TM, D), lambda i: (jnp.minimum(i, n_lat - 1), 0)), _whole((1, D))],
        out_specs=[_rows(D), pl.BlockSpec((8, D), lambda i: (0, 0))],
        out_shape=[jax.ShapeDtypeStruct((s, D), F32), jax.ShapeDtypeStruct((8, D), F32)],
        compiler_params=_params("arbitrary"),
    )(h, target, g)


def mod_rows(c_all, w_mod, b_cols, name):
    def body(c_ref, w_ref, b_ref, o_ref):
        cc = c_ref[...]
        o_ref[...] = _dot((cc * jax.nn.sigmoid(cc)).astype(BF16), w_ref[...].astype(BF16)) + b_ref[...]

    return pl.pallas_call(
        body, name=name, grid=(2,),
        in_specs=[pl.BlockSpec((16, D), lambda l: (0, 0)), pl.BlockSpec((None, D, MOD_COLS), lambda l: (l, 0, 0)),
                  pl.BlockSpec((None, 1, MOD_COLS), lambda l: (l, 0, 0))],
        out_specs=pl.BlockSpec((None, 16, MOD_COLS), lambda l: (l, 0, 0)),
        out_shape=jax.ShapeDtypeStruct((2, 16, MOD_COLS), F32),
        compiler_params=_params("parallel"),
    )(c_all, w_mod, b_cols)


def mod_grads(c_all, dmod_cols, w_mod, name):
    def body(c_ref, d_ref, w_ref, dw_ref, dc_ref):
        @pl.when(pl.program_id(0) == 0)
        def _():
            dc_ref[...] = jnp.zeros_like(dc_ref)

        cc = c_ref[...]
        dd = d_ref[...].astype(BF16)
        dw_ref[...] = _dotg((cc * jax.nn.sigmoid(cc)).astype(BF16), dd, TN)
        dc_ref[...] += _dotg(dd, w_ref[...].astype(BF16), NT)

    return pl.pallas_call(
        body, name=name, grid=(2,),
        in_specs=[pl.BlockSpec((16, D), lambda l: (0, 0)), pl.BlockSpec((None, 16, MOD_COLS), lambda l: (l, 0, 0)),
                  pl.BlockSpec((None, D, MOD_COLS), lambda l: (l, 0, 0))],
        out_specs=[pl.BlockSpec((None, D, MOD_COLS), lambda l: (l, 0, 0)), pl.BlockSpec((16, D), lambda l: (0, 0))],
        out_shape=[jax.ShapeDtypeStruct((2, D, MOD_COLS), F32), jax.ShapeDtypeStruct((16, D), F32)],
        compiler_params=_params("arbitrary"),
    )(c_all, dmod_cols, w_mod)


def _row_tile(rows, cols, n_arrays):
    budget = VMEM_LIMIT_BYTES // 4 // (2 * 4 * n_arrays * cols)
    best = None
    for tr in range(16, rows + 1, 16):
        if rows % tr == 0 and tr <= budget:
            best = tr
    return best if best is not None else rows


def elementwise(fn, ins, out_dtypes, name, ex=None):
    rows, cols = ins[0].shape
    tr = _row_tile(rows, cols, len(ins) + len(out_dtypes))

    def body(*refs):
        outs = fn(*[r[...] for r in refs[:len(ins)]])
        for o_ref, o in zip(refs[len(ins):], outs):
            o_ref[...] = o.astype(o_ref.dtype)

    spec = pl.BlockSpec((tr, cols), lambda i: (i, 0))
    outs, got = _grid_call(body, name, rows // tr, [spec] * len(ins), [spec] * len(out_dtypes),
                           [jax.ShapeDtypeStruct((rows, cols), dt) for dt in out_dtypes], ins, "parallel", ex)
    return outs if ex is None else (outs, got)


def _adamw_tile(w, g, m, v):
    m = ADAM_B1 * m + (1.0 - ADAM_B1) * g
    v = ADAM_B2 * v + (1.0 - ADAM_B2) * (g * g)
    m_hat = m / (1.0 - ADAM_B1 ** ADAM_STEP)
    v_hat = v / (1.0 - ADAM_B2 ** ADAM_STEP)
    return -ADAM_LR * (m_hat / (jnp.sqrt(v_hat) + ADAM_EPS) + ADAM_WD * w), m, v


def adamw(w, g, m, v, name, ex=None):
    shape = w.shape
    two_d = (-1, shape[-1]) if w.ndim > 1 else (1, -1)
    outs = elementwise(_adamw_tile, [a.reshape(two_d) for a in (w, g, m, v)], [F32] * 3, name, ex)
    outs, got = outs if ex is not None else (outs, None)
    outs = [o.reshape(shape) for o in outs]
    return outs if ex is None else (outs, got)


def _prefetch_call(body, name, grid, in_specs, out_specs, out_shape, place, args, ex=None):
    if ex is None:
        spec = pltpu.PrefetchScalarGridSpec(num_scalar_prefetch=1, grid=grid, in_specs=in_specs, out_specs=out_specs)
        return pl.pallas_call(body, name=name, grid_spec=spec, out_shape=out_shape,
                              compiler_params=_params(*["parallel"] * len(grid)))(place, *args)
    n_in, n_out, ci, co = len(in_specs), len(out_specs), len(ex["ins"]), len(ex["out_shape"])
    spec = pltpu.PrefetchScalarGridSpec(num_scalar_prefetch=1, grid=grid, in_specs=list(in_specs) + _any(ci),
                                        out_specs=list(out_specs) + _any(co), scratch_shapes=ex["scratch"])
    outs = pl.pallas_call(
        _carrying(body, grid, n_in, n_out, ex, lead=1), name=name, grid_spec=spec, out_shape=list(out_shape) + ex["out_shape"],
        input_output_aliases={1 + n_in + i: n_out + j for i, j in ex["aliases"].items()},
        compiler_params=_params(*["arbitrary"] * len(grid)))(place, *args, *ex["ins"])
    return outs[:n_out], outs[n_out:]


def cast_place(w, layer, place, name):
    _, r, c = w.shape
    tr = _row_tile(r, c, 2)

    def body(p_ref, w_ref, o_ref):
        o_ref[...] = w_ref[...].astype(BF16)

    return _prefetch_call(
        body, name, (r // tr,), [pl.BlockSpec((None, tr, c), lambda i, p: (layer, i, 0))],
        pl.BlockSpec((None, tr, c), lambda i, p: (p[1], i, 0)), jax.ShapeDtypeStruct((N_SLOT, r, c), BF16), place, [w])


def pair_sum(g32, got, place, name, ex=None):
    n_slot, rh, c = got.shape
    tr = _row_tile(rh, c, 4)
    per = rh // tr

    def body(p_ref, a_ref, b_ref, o_ref, o16_ref):
        r = a_ref[...] + b_ref[...].astype(F32)
        o_ref[...] = r
        o16_ref[...] = r.astype(BF16)

    half = pl.BlockSpec((None, tr, c), lambda s, i, p: (s, i, 0))
    return _prefetch_call(
        body, name, (n_slot, per), [pl.BlockSpec((None, tr, c), lambda s, i, p: (s, p[0] * per + i, 0)), half], [half, half],
        [jax.ShapeDtypeStruct(got.shape, F32), jax.ShapeDtypeStruct(got.shape, BF16)], place, [g32, got], ex)


def chip_sum(p32, got, place, name):
    _, rh, c = p32.shape
    tr = _row_tile(rh, c, 5)
    per = rh // tr

    def body(p_ref, m_ref, r0_ref, r1_ref, r2_ref, o_ref):
        o_ref[...] = m_ref[...] + r0_ref[...].astype(F32) + r1_ref[...].astype(F32) + r2_ref[...].astype(F32)

    part = pl.BlockSpec((tr, c), lambda i, p: (i, 0))
    return _prefetch_call(
        body, name, (per,), [pl.BlockSpec((None, tr, c), lambda i, p: (p[1], i, 0)), part, part, part],
        pl.BlockSpec((tr, c), lambda i, p: (p[0] * per + i, 0)), jax.ShapeDtypeStruct((2 * rh, c), F32), place, [p32, *got])


def adamw_layers(w, g0, g1, m, v, name, ex=None):
    _, r, c = w.shape
    tr = _row_tile(r, c, 10)

    def body(w_ref, g0_ref, g1_ref, m_ref, v_ref, g_ref, d_ref, mo_ref, vo_ref):
        g = jnp.where(pl.program_id(0) == 0, g0_ref[...], g1_ref[...])
        g_ref[...] = g
        d_ref[...], mo_ref[...], vo_ref[...] = _adamw_tile(w_ref[...], g, m_ref[...], v_ref[...])

    stacked = pl.BlockSpec((None, tr, c), lambda l, i: (l, i, 0))
    layer = pl.BlockSpec((tr, c), lambda l, i: (i, 0))
    outs, got = _grid_call(body, name, (2, r // tr), [stacked, layer, layer, stacked, stacked], [stacked] * 4,
                           [jax.ShapeDtypeStruct(w.shape, F32)] * 4, (w, g0, g1, m, v), "parallel", ex)
    return outs if ex is None else (outs, got)


def sum8(gathered, name):
    def body(*refs):
        n = len(refs) // 2
        for g_ref, o_ref in zip(refs[:n], refs[n:]):
            acc = g_ref[0]
            for dev in range(1, N_DEV):
                acc = acc + g_ref[dev]
            o_ref[...] = acc

    return pl.pallas_call(
        body, name=name,
        out_shape=[jax.ShapeDtypeStruct(a.shape[1:], F32) for a in gathered],
        compiler_params=_params(),
    )(*gathered)


PHASES = ("start", "late", "finish")


def _place():
    return lax.axis_index("x"), lax.axis_index("y"), lax.axis_index("c")


def _any(n):
    return [pl.BlockSpec(memory_space=pl.ANY)] * n


def gather8_exchange(blocks):
    n = len(blocks)

    def copy(outs, sems, ti, k, block, to, src=None):
        dst = outs[ti].at[4 * block[0] + 2 * block[1] + block[2]]
        return pltpu.make_async_remote_copy(src_ref=dst if src is None else src, dst_ref=dst, send_sem=sems[0].at[ti, k],
                                            recv_sem=sems[1].at[ti, k], device_id=to, device_id_type=MESH)

    def first(ins, outs, sems):
        x, y, c = _place()
        local, sent = [], []
        for ti in range(n):
            local.append(pltpu.make_async_copy(ins[ti], outs[ti].at[4 * x + 2 * y + c], sems[2].at[ti]))
            sent.append(copy(outs, sems, ti, 0, (x, y, c), (x, y, 1 - c), src=ins[ti]))
            sent += [copy(outs, sems, ti, 1 + j, (x, y, c), (*chip, c), src=ins[ti]) for j, chip in enumerate(_three_chips(x, y))]
        return local, sent

    def start(ins, outs, sems):
        local, sent = first(ins, outs, sems)
        for cp in local + sent:
            cp.start()

    def passed_on(outs, sems):
        x, y, c = _place()
        return [copy(outs, sems, ti, 4 + j, (*chip, c), (x, y, 1 - c)) for ti in range(n) for j, chip in enumerate(_three_chips(x, y))]

    def late(ins, outs, sems):
        x, y, c = _place()
        on = passed_on(outs, sems)
        for ti in range(n):
            for j, chip in enumerate(_three_chips(x, y)):
                copy(outs, sems, ti, 1 + j, (*chip, c), (x, y, c)).wait_recv()
                on[3 * ti + j].start()

    def finish(ins, outs, sems):
        x, y, c = _place()
        me, sibling = (x, y, c), (x, y, 1 - c)
        local, sent = first(ins, outs, sems)
        for ti in range(n):
            copy(outs, sems, ti, 0, sibling, me).wait_recv()
            for j, chip in enumerate(_three_chips(x, y)):
                copy(outs, sems, ti, 4 + j, (*chip, 1 - c), me).wait_recv()
        for cp in sent + passed_on(outs, sems):
            cp.wait_send()
        for cp in local:
            cp.wait()

    return dict(ins=list(blocks), out_shape=[jax.ShapeDtypeStruct((N_DEV,) + b.shape, b.dtype) for b in blocks], aliases={},
                start=start, late=late, finish=finish,
                scratch=[pltpu.SemaphoreType.DMA((n, 7)), pltpu.SemaphoreType.DMA((n, 7)), pltpu.SemaphoreType.DMA((n,))])


def all_gather(blocks, name):
    return run_exchange(gather8_exchange(blocks), name)


def _three_chips(x, y):
    return [(1 - x, y), (x, 1 - y), (1 - x, 1 - y)]


def gather_exchange(placed):
    n = len(placed)

    def copy(bufs, sems, ti, k, chip, core, to):
        rh = bufs[ti].shape[1] // 2
        half = bufs[ti].at[2 * chip[0] + chip[1], pl.ds(core * rh, rh), :]
        return pltpu.make_async_remote_copy(src_ref=half, dst_ref=half, send_sem=sems[0].at[ti, k], recv_sem=sems[1].at[ti, k],
                                            device_id=to, device_id_type=MESH)

    def sends(bufs, sems):
        x, y, c = _place()
        return [copy(bufs, sems, ti, k, (x, y), c, (*chip, c)) for ti in range(n) for k, chip in enumerate(_three_chips(x, y))]

    def passed_on(bufs, sems):
        x, y, c = _place()
        return [copy(bufs, sems, ti, 3 + k, chip, c, (x, y, 1 - c)) for ti in range(n) for k, chip in enumerate(_three_chips(x, y))]

    def start(ins, bufs, sems):
        for cp in sends(bufs, sems):
            cp.start()

    def late(ins, bufs, sems):
        x, y, c = _place()
        on = passed_on(bufs, sems)
        for ti in range(n):
            for k, chip in enumerate(_three_chips(x, y)):
                copy(bufs, sems, ti, k, chip, c, (x, y, c)).wait_recv()
                on[3 * ti + k].start()

    def finish(ins, bufs, sems):
        x, y, c = _place()
        for ti in range(n):
            for k, chip in enumerate(_three_chips(x, y)):
                copy(bufs, sems, ti, 3 + k, chip, 1 - c, (x, y, c)).wait_recv()
        for cp in sends(bufs, sems) + passed_on(bufs, sems):
            cp.wait_send()

    return dict(ins=list(placed), out_shape=[jax.ShapeDtypeStruct(w.shape, w.dtype) for w in placed],
                aliases={i: i for i in range(n)}, start=start, late=late, finish=finish,
                scratch=[pltpu.SemaphoreType.DMA((n, 6)), pltpu.SemaphoreType.DMA((n, 6))])


def scatter_exchange(p16):
    n = len(p16)

    def copies(ins, got, sems):
        x, y, c = _place()
        return [pltpu.make_async_remote_copy(src_ref=ins[ti].at[2 * chip[0] + chip[1]], dst_ref=got[3 * ti + k],
                                             send_sem=sems[0].at[ti, k], recv_sem=sems[1].at[ti, k], device_id=(*chip, c),
                                             device_id_type=MESH)
                for ti in range(n) for k, chip in enumerate(_three_chips(x, y))]

    def start(ins, got, sems):
        for cp in copies(ins, got, sems):
            cp.start()

    def finish(ins, got, sems):
        for cp in copies(ins, got, sems):
            cp.wait()

    return dict(ins=list(p16), out_shape=[jax.ShapeDtypeStruct(a.shape[1:], BF16) for a in p16 for _ in range(3)], aliases={},
                start=start, finish=finish, scratch=[pltpu.SemaphoreType.DMA((n, 3)), pltpu.SemaphoreType.DMA((n, 3))])


def run_exchange(ex, name):
    ci, co = len(ex["ins"]), len(ex["out_shape"])

    def body(*refs):
        ins, outs, sems = refs[:ci], refs[ci:ci + co], refs[ci + co:]
        for phase in PHASES:
            if phase in ex:
                ex[phase](ins, outs, sems)

    return pl.pallas_call(body, name=name, in_specs=_any(ci), out_specs=_any(co), out_shape=ex["out_shape"],
                          input_output_aliases=ex["aliases"], scratch_shapes=ex["scratch"])(*ex["ins"])


def _carrying(body, grid, n_in, n_out, ex, lead=0):
    ci, co = len(ex["ins"]), len(ex["out_shape"])
    first, last = (0,) * len(grid), tuple(g - 1 for g in grid)
    steps = dict(start=first, late=(grid[0] - 2,) if len(grid) == 1 and grid[0] > 2 else last, finish=last)

    def at(ids):
        return functools.reduce(jnp.logical_and, [pl.program_id(ax) == v for ax, v in enumerate(ids)])

    def carrying(*refs):
        head, refs = refs[:lead], refs[lead:]
        c_in, c_out = refs[n_in:n_in + ci], refs[n_in + ci + n_out:n_in + ci + n_out + co]
        sems = refs[n_in + ci + n_out + co:]
        for phase in PHASES:
            if phase == "finish":
                body(*head, *refs[:n_in], *refs[n_in + ci:n_in + ci + n_out])
            if phase in ex:
                pl.when(at(steps[phase]))(functools.partial(ex[phase], c_in, c_out, sems))

    return carrying


def _grid_call(body, name, grid, in_specs, out_specs, out_shape, args, sem, ex=None):
    grid = (grid,) if isinstance(grid, int) else tuple(grid)
    sems_of = (sem,) * len(grid) if isinstance(sem, str) else tuple(sem)
    n_in, n_out = len(in_specs), len(out_specs)
    if ex is None:
        return pl.pallas_call(body, name=name, grid=grid, in_specs=in_specs, out_specs=out_specs, out_shape=out_shape,
                              compiler_params=_params(*sems_of))(*args), []
    ci, co = len(ex["ins"]), len(ex["out_shape"])
    outs = pl.pallas_call(
        _carrying(body, grid, n_in, n_out, ex), name=name, grid=grid, in_specs=list(in_specs) + _any(ci),
        out_specs=list(out_specs) + _any(co), out_shape=list(out_shape) + ex["out_shape"], scratch_shapes=ex["scratch"],
        input_output_aliases={n_in + i: n_out + j for i, j in ex["aliases"].items()},
        compiler_params=_params(*["arbitrary"] * len(grid)),
    )(*args, *ex["ins"])
    return outs[:n_out], outs[n_out:]


def both(*exchanges):
    exchanges = [ex for ex in exchanges if ex is not None]
    if len(exchanges) < 2:
        return exchanges[0] if exchanges else None
    n_ins = [len(ex["ins"]) for ex in exchanges]
    n_outs = [len(ex["out_shape"]) for ex in exchanges]
    n_sems = [len(ex["scratch"]) for ex in exchanges]

    def parts(seq, counts, k):
        first = sum(counts[:k])
        return seq[first:first + counts[k]]

    def run(phase):
        def go(ins, outs, sems):
            for k, ex in enumerate(exchanges):
                if phase in ex:
                    ex[phase](parts(ins, n_ins, k), parts(outs, n_outs, k), parts(sems, n_sems, k))
        return go

    aliases = {sum(n_ins[:k]) + i: sum(n_outs[:k]) + j for k, ex in enumerate(exchanges) for i, j in ex["aliases"].items()}
    return dict(ins=[a for ex in exchanges for a in ex["ins"]], out_shape=[o for ex in exchanges for o in ex["out_shape"]],
                aliases=aliases, scratch=[s for ex in exchanges for s in ex["scratch"]], **{ph: run(ph) for ph in PHASES})


def split_outputs(got, *exchanges):
    got, out = list(got), []
    for ex in exchanges:
        n = len(ex["out_shape"]) if ex is not None else 0
        out.append(got[:n])
        got = got[n:]
    return out


def pair_exchange(g16):
    n = len(g16)

    def copies(a16, got, sems):
        x, y, c = _place()
        out = []
        for ti in range(n):
            rh = a16[ti].shape[1] // 2
            out.append(pltpu.make_async_remote_copy(
                src_ref=a16[ti].at[:, pl.ds((1 - c) * rh, rh), :], dst_ref=got[ti], send_sem=sems[0].at[ti],
                recv_sem=sems[1].at[ti], device_id=(x, y, 1 - c), device_id_type=MESH))
        return out

    def start(a16, got, sems):
        for cp in copies(a16, got, sems):
            cp.start()

    def finish(a16, got, sems):
        for cp in copies(a16, got, sems):
            cp.wait()

    return dict(ins=list(g16), out_shape=[jax.ShapeDtypeStruct((a.shape[0], a.shape[1] // 2, a.shape[2]), BF16) for a in g16],
                aliases={}, start=start, finish=finish, scratch=[pltpu.SemaphoreType.DMA((n,)), pltpu.SemaphoreType.DMA((n,))])


def pair_gather(halves, name):
    n = len(halves)

    def body(*refs):
        bufs = refs[n:2 * n]
        send_sems, recv_sems = refs[2 * n:]
        x, y, c = _place()
        copies = []
        for ti in range(n):
            rh = bufs[ti].shape[0] // 2
            rows = bufs[ti].at[pl.ds(c * rh, rh), :]
            copies.append(pltpu.make_async_remote_copy(src_ref=rows, dst_ref=rows, send_sem=send_sems.at[ti],
                                                       recv_sem=recv_sems.at[ti], device_id=(x, y, 1 - c), device_id_type=MESH))
        for cp in copies:
            cp.start()
        for ti, cp in enumerate(copies):
            cp.wait_send()
            rh = bufs[ti].shape[0] // 2
            theirs = bufs[ti].at[pl.ds((1 - c) * rh, rh), :]
            pltpu.make_async_remote_copy(src_ref=theirs, dst_ref=theirs, send_sem=send_sems.at[ti], recv_sem=recv_sems.at[ti],
                                         device_id=(x, y, 1 - c), device_id_type=MESH).wait_recv()

    return pl.pallas_call(
        body, name=name, in_specs=_any(n), out_specs=_any(n), input_output_aliases={i: i for i in range(n)},
        out_shape=[jax.ShapeDtypeStruct(a.shape, a.dtype) for a in halves],
        scratch_shapes=[pltpu.SemaphoreType.DMA((n,)), pltpu.SemaphoreType.DMA((n,))],
    )(*halves)


def reduce_small(dm_f1, dm_mix, dm_gate, dm_f2, loss_blk, name):
    def body(f1_ref, mix_ref, gate_ref, f2_ref, l_ref, tot_ref, rows_ref, fin_ref):
        rows_ref[...] = jnp.zeros_like(rows_ref)
        tot_ref[...] = jnp.zeros_like(tot_ref)
        mod_src = [(f1_ref, 0), (f1_ref, 1), (f1_ref, 2), (mix_ref, 0), (mix_ref, 1), (gate_ref, 2),
                   (f2_ref, 0), (f2_ref, 1), (f2_ref, 2)]
        norm_src = [(f1_ref, 3), (mix_ref, 3), (f2_ref, 3)]
        for l in range(2):
            for k, (ref, r) in enumerate(mod_src + norm_src):
                lat = ref[0, l, 0, r:r + 1, :]
                ctx = ref[0, l, 1, r:r + 1, :]
                for dev in range(N_DEV):
                    if dev:
                        lat = lat + ref[dev, l, 0, r:r + 1, :]
                        ctx = ctx + ref[dev, l, 1, r:r + 1, :]
                    if k < N_MOD:
                        rows_ref[l, dev, k:k + 1, :] = ref[dev, l, 0, r:r + 1, :]
                if k < N_MOD:
                    rows_ref[l, N_DEV, k:k + 1, :] = ctx
                tot_ref[l, k:k + 1, :] = lat + ctx
        acc = l_ref[0]
        for dev in range(1, N_DEV):
            acc = acc + l_ref[dev]
        loss = (0.5 / D) * jnp.sum(acc[1:2, :], axis=1, keepdims=True)
        row = lax.broadcasted_iota(jnp.int32, (8, D), 0)
        fin_ref[...] = jnp.where(row == 0, acc[0:1, :], loss)

    return pl.pallas_call(
        body, name=name,
        out_shape=[jax.ShapeDtypeStruct((2, 16, D), F32), jax.ShapeDtypeStruct((2, 16, 16, D), F32),
                   jax.ShapeDtypeStruct((8, D), F32)],
        compiler_params=_params(),
    )(dm_f1, dm_mix, dm_gate, dm_f2, loss_blk)


def rope_tables(t, s):
    rows = t // GRID_W
    row = jnp.repeat(jnp.arange(rows), GRID_W).astype(F32)
    col = jnp.tile(jnp.arange(GRID_W), rows).astype(F32)
    inv = ROPE_BASE ** (-jnp.arange(0, HEAD // 2, 2, dtype=F32) / (HEAD // 2))
    ang = jnp.concatenate([row[:, None] * inv, col[:, None] * inv], axis=-1)
    cos, sin = jnp.cos(ang), jnp.sin(ang)
    cos = jnp.concatenate([jnp.tile(cos, (1, 4)), jnp.ones((s - t, BLK), F32)], axis=0)
    sin = jnp.concatenate([jnp.tile(jnp.concatenate([-sin, sin], axis=1), (1, 2)), jnp.zeros((s - t, BLK), F32)], axis=0)
    return cos, sin


BIG = ("ffn1_in", "ffn1_out", "w_in", "w_out", "ffn2_in", "ffn2_out")
GROUPS = dict(ffn1=("ffn1_in", "ffn1_out"), mix=("w_in", "w_out"), ffn2=("ffn2_in", "ffn2_out"))
GATHER_BEHIND = {("ffn1", 0): [("w_in", 0), ("ffn2_out", 0), ("ffn1_out", 1)], ("proj", 0): [("w_out", 0)],
                 ("mix", 0): [("ffn2_in", 0)], ("ffn2", 0): [("ffn1_in", 1), ("w_in", 1)],
                 ("ffn1", 1): [("ffn2_in", 1), ("w_out", 1)], ("mix", 1): [("ffn2_out", 1)]}


def _slot_major(name, g):
    if name == "w_in":
        return jnp.stack(jnp.split(g, N_SLOT, axis=1), axis=0)
    if name in ("ffn1_in", "ffn2_in"):
        return g
    return g.reshape(N_SLOT, g.shape[0] // N_SLOT, g.shape[1])


def _whole_weight(name, buf):
    if name == "w_in":
        return buf.transpose(1, 0, 2).reshape(D, PROJ_W)
    if name in ("ffn1_in", "ffn2_in"):
        return buf
    return buf.reshape(-1, buf.shape[2])


def local_step(x1, ctx1, target, mods, norms, nfinal, placed, w_pool, pool_scale, sink, place, small_blocks):
    t, s = x1.shape[0], x1.shape[0] + ctx1.shape[0]
    n_lat = t // TM
    cos, sin = rope_tables(t, s)
    tables = mix_tables(t, s)
    wts ={name: list(pair) for name, pair in placed.items()}

    def gather(tensors):
        return gather_exchange([wts[name][l] for name, l in tensors])

    def gathered(tensors, arrays):
        for (name, l), whole in zip(tensors, arrays):
            wts[name][l] = whole

    def weight(name, l):
        return _whole_weight(name, wts[name][l])

    def fwd_ex(grp, l):
        groups = GATHER_BEHIND.get((grp, l))
        return (groups, gather(groups)) if groups else (None, None)

    first = [("ffn1_in", 0), ("ffn1_out", 0)]
    gathered(first, run_exchange(gather(first), "gather_first"))
    h = jnp.concatenate([x1, ctx1], axis=0)
    saved = []
    for l in range(2):
        h0 = h
        groups, ex = fwd_ex("ffn1", l)
        (h1, ab1, f1), got = ffn_fwd(h0, mods, norms[0], weight("ffn1_in", l), weight("ffn1_out", l), l, 0, n_lat, f"ffn1_fwd_{l}", ex)
        gathered(groups or [], got)
        groups, ex = fwd_ex("proj", l)
        (u, q, k, v), got = proj_fwd(h1, mods, norms[1], weight("w_in", l), cos, sin, l, n_lat, f"proj_fwd_{l}", ex)
        gathered(groups or [], got)
        groups, ex = fwd_ex("mix", l)
        (h2, cat, lse, mo), got = mix_fwd(h1, q, k, v, u, w_pool, pool_scale, sink, weight("w_out", l), mods, tables, l, t,
                                          f"mix_fwd_{l}", ex)
        gathered(groups or [], got)
        groups, ex = fwd_ex("ffn2", l)
        (h, ab2, f2), got = ffn_fwd(h2, mods, norms[2], weight("ffn2_in", l), weight("ffn2_out", l), l, 6, n_lat, f"ffn2_fwd_{l}", ex)
        gathered(groups or [], got)
        saved.append((h0, ab1, f1, h1, u, q, k, v, cat, lse, mo, h2, ab2, f2))
    dh, loss_blk = loss_head(h, target, nfinal, t, "loss_head")

    halves = {name: [None, None] for name in BIG}
    pending = []

    def summed_in_pair(grp, l, name_a, g_a, name_b, wgrad_b):
        g_b, got_a = wgrad_b(pair_exchange([_slot_major(name_a, g_a[1])]))
        sum_a, got_b = pair_sum(_slot_major(name_a, g_a[0]), got_a[0], place, f"pair_sum_{name_a}_{l}",
                                pair_exchange([_slot_major(name_b, g_b[1])]))
        sums = {name_a: sum_a, name_b: pair_sum(_slot_major(name_b, g_b[0]), got_b[0], place, f"pair_sum_{name_b}_{l}")}
        pending.append((grp, l, [sums[n] for n in GROUPS[grp]]))

    def scatter():
        return scatter_exchange([p16 for _, p16 in pending[0][2]]) if pending else None

    def scattered(got):
        if pending:
            grp, l, pairs = pending.pop(0)
            for i, name in enumerate(GROUPS[grp]):
                halves[name][l] = chip_sum(pairs[i][0], got[3 * i:3 * i + 3], place, f"chip_sum_{name}_{l}")

    small = [None, None]
    for l in (1, 0):
        h0, ab1, f1, h1, u, q, k, v, cat, lse, mo, h2, ab2, f2 = saved[l]
        (dh, dab, df, n, act, dm_f2), got = ffn_bwd(h2, ab2, f2, dh, mods, norms[2], weight("ffn2_in", l), weight("ffn2_out", l),
                                                    l, 6, n_lat, f"ffn2_bwd_{l}", scatter())
        scattered(got)
        g_in, _ = wgrad(n, dab, D, FF_COLS, FF_COLS, f"ffn2_in_wgrad_{l}")
        summed_in_pair("ffn2", l, "ffn2_in", g_in, "ffn2_out",
                       lambda ex, a=act, b=df: wgrad(a, b, D_FF // 2, D, None, f"ffn2_out_wgrad_{l}", ex))
        (dq, dk, dv, du, dmo, dwp, dps, dsink, dm_gate), got = mix_bwd(
            dh, mo, q, k, v, u, lse, w_pool, pool_scale, sink, weight("w_out", l), mods, tables, l, t, f"mix_bwd_{l}", scatter())
        scattered(got)
        g_wo, _ = wgrad(cat, dmo, POOL_W + ATTN_W, D, None, f"w_out_wgrad_{l}")
        dh, dp, n, dm_mix = proj_bwd(h1, du, dq, dk, dv, dh, mods, norms[1], weight("w_in", l), cos, sin, l, n_lat, f"proj_bwd_{l}")
        summed_in_pair("mix", l, "w_out", g_wo, "w_in",
                       lambda ex, a=n, b=dp: wgrad(a, b, D, PROJ_W // 2, None, f"w_in_wgrad_{l}", ex))
        (dh, dab, df, n, act, dm_f1), got = ffn_bwd(h0, ab1, f1, dh, mods, norms[0], weight("ffn1_in", l), weight("ffn1_out", l),
                                                    l, 0, n_lat, f"ffn1_bwd_{l}", scatter())
        scattered(got)
        small[l] = dict(dm_f1=dm_f1, dm_mix=dm_mix, dm_gate=dm_gate, dm_f2=dm_f2, dwp=dwp, dps=dps, dsink=dsink)
        if l:
            g_in, _ = wgrad(n, dab, D, FF_COLS, FF_COLS, f"ffn1_in_wgrad_{l}")
            summed_in_pair("ffn1", l, "ffn1_in", g_in, "ffn1_out",
                           lambda ex, a=act, b=df: wgrad(a, b, D_FF // 2, D, None, f"ffn1_out_wgrad_{l}", ex))
    g_out, small_all = wgrad(act, df, D_FF // 2, D, None, "ffn1_out_wgrad_0", gather8_exchange(small_blocks(small, loss_blk)))
    got = run_exchange(pair_exchange([_slot_major("ffn1_out", g_out[1])]), "pair_exchange_ffn1_out_0")
    p32, p16 = pair_sum(_slot_major("ffn1_out", g_out[0]), got[0], place, "pair_sum_ffn1_out_0")
    g_in, got = wgrad(n, dab, D, FF_COLS, FF_COLS, "ffn1_in_wgrad_0", scatter_exchange([p16]))
    halves["ffn1_out"][0] = chip_sum(p32, got, place, "chip_sum_ffn1_out_0")
    got = run_exchange(pair_exchange([_slot_major("ffn1_in", g_in[1])]), "pair_exchange_ffn1_in_0")
    return dh[:t], halves, pair_sum(_slot_major("ffn1_in", g_in[0]), got[0], place, "pair_sum_ffn1_in_0"), small_all


def _silu_grad(z):
    sg = jax.nn.sigmoid(z)
    return sg * (1 + z * (1 - sg))


def kernel(x, c, ctx, c_ctx, w_mod, b_mod, norm_ffn1, w_ffn1_in, w_ffn1_out, norm_mix, w_in, w_pool, pool_scale, sink, w_out, norm_ffn2, w_ffn2_in, w_ffn2_out, norm_final, loss_target, m_c_ctx, m_w_mod, m_b_mod, m_norm_ffn1, m_w_ffn1_in, m_w_ffn1_out, m_norm_mix, m_w_in, m_w_pool, m_pool_scale, m_sink, m_w_out, m_norm_ffn2, m_w_ffn2_in, m_w_ffn2_out, m_norm_final, v_c_ctx, v_w_mod, v_b_mod, v_norm_ffn1, v_w_ffn1_in, v_w_ffn1_out, v_norm_mix, v_w_in, v_w_pool, v_pool_scale, v_sink, v_w_out, v_norm_ffn2, v_w_ffn2_in, v_w_ffn2_out, v_norm_final):
    px, py, pc = _place()
    slot, me = 2 * px + py, 4 * px + 2 * py + pc
    n_grp = len(POOL_WINDOWS)

    (c_rows,) = all_gather([c.reshape(8, D // 8)], "gather_c")
    c_all = jnp.concatenate([c_rows.reshape(N_DEV, D), c_ctx.reshape(1, D), jnp.zeros((16 - N_DEV - 1, D), F32)], axis=0)
    b_cols = lax.dynamic_slice(b_mod, (0, slot * MOD_COLS), (2, MOD_COLS)).reshape(2, 1, MOD_COLS)
    (mod_parts,) = all_gather([mod_rows(c_all, w_mod, b_cols, "mod_rows")], "gather_mods")
    mods_all = mod_parts[0::2].transpose(1, 2, 0, 3).reshape(2, 16, N_MOD * D)
    mx = lax.dynamic_slice(mods_all, (0, me, 0), (2, 1, N_MOD * D)).reshape(2, N_MOD, D)
    mc = mods_all[:, N_DEV].reshape(2, N_MOD, D)
    pad = jnp.zeros((2, 16 - N_MOD, D), F32)
    mods = jnp.stack([jnp.concatenate([mx, pad], axis=1), jnp.concatenate([mc, pad], axis=1)], axis=1)

    place = jnp.stack([pc, slot]).astype(jnp.int32)
    shards = dict(ffn1_in=w_ffn1_in, ffn1_out=w_ffn1_out, w_in=w_in, w_out=w_out, ffn2_in=w_ffn2_in, ffn2_out=w_ffn2_out)
    placed = {name: [cast_place(shards[name], l, place, f"cast_{name}_{l}") for l in range(2)] for name in BIG}
    norms = [g.reshape(2, 1, D) for g in (norm_ffn1, norm_mix, norm_ffn2)]
    row_sums = ("dm_f1", "dm_mix", "dm_gate", "dm_f2")

    def small_blocks(small, loss_blk):
        stacked = {k: jnp.stack([small[0][k], small[1][k]]) for k in row_sums + ("dwp", "dps", "dsink")}
        return ([stacked[k].reshape(32, D) for k in row_sums]
                + [stacked["dwp"].reshape(2 * n_grp * GROUP, GROUP), stacked["dps"].reshape(16, POOL_W),
                   stacked["dsink"].reshape(16, BLK), loss_blk])

    dx, halves, last_pair, small_all = local_step(x[0], ctx[0], loss_target[0], mods, norms, norm_final.reshape(1, D), placed,
                                                   w_pool.astype(BF16), pool_scale.reshape(2, 1, POOL_W), sink, place, small_blocks)
    grads = {}

    *g_dm, g_dwp, g_dps, g_dsink, g_loss = small_all
    tot, rows, fin = reduce_small(*[g.reshape(N_DEV, 2, 2, 8, D) for g in g_dm], g_loss, "reduce_small")
    s_dwp, s_dps, s_dsink = sum8([g_dwp, g_dps, g_dsink], "sum_pool_sink")
    grads.update(
        w_pool=s_dwp.reshape(2, n_grp, GROUP, GROUP), pool_scale=s_dps.reshape(2, 8, POOL_W)[:, 0],
        sink=s_dsink.reshape(2, 8, BLK)[:, 0, :N_HEADS], b_mod=tot[:, :N_MOD].reshape(2, N_MOD * D),
        norm_ffn1=tot[:, N_MOD], norm_mix=tot[:, N_MOD + 1], norm_ffn2=tot[:, N_MOD + 2], norm_final=fin[0])
    loss = fin[1, 0]

    dmod_cols = lax.dynamic_slice(rows[:, :, :N_MOD, :].reshape(2, 16, N_MOD * D), (0, 0, slot * MOD_COLS), (2, 16, MOD_COLS))
    grads["w_mod"], dc = mod_grads(c_all, dmod_cols, w_mod, "mod_grads")
    (g_dc,) = all_gather([dc], "gather_dc")
    (s_dc,) = sum8([g_dc], "sum_dc")
    (d_c_ctx,) = elementwise(lambda d, z: (0.5 * d * _silu_grad(z),), [s_dc[N_DEV:N_DEV + 1], c_ctx.reshape(1, D)], [F32], "c_ctx_grad")
    grads["c_ctx"] = d_c_ctx.reshape(D)

    given = dict(c_ctx=(c_ctx, m_c_ctx, v_c_ctx), w_mod=(w_mod, m_w_mod, v_w_mod), b_mod=(b_mod, m_b_mod, v_b_mod),
                 norm_ffn1=(norm_ffn1, m_norm_ffn1, v_norm_ffn1), w_ffn1_in=(w_ffn1_in, m_w_ffn1_in, v_w_ffn1_in),
                 w_ffn1_out=(w_ffn1_out, m_w_ffn1_out, v_w_ffn1_out), norm_mix=(norm_mix, m_norm_mix, v_norm_mix),
                 w_in=(w_in, m_w_in, v_w_in), w_pool=(w_pool, m_w_pool, v_w_pool),
                 pool_scale=(pool_scale, m_pool_scale, v_pool_scale), sink=(sink, m_sink, v_sink), w_out=(w_out, m_w_out, v_w_out),
                 norm_ffn2=(norm_ffn2, m_norm_ffn2, v_norm_ffn2), w_ffn2_in=(w_ffn2_in, m_w_ffn2_in, v_w_ffn2_in),
                 w_ffn2_out=(w_ffn2_out, m_w_ffn2_out, v_w_ffn2_out), norm_final=(norm_final, m_norm_final, v_norm_final))
    got = run_exchange(scatter_exchange([last_pair[1]]), "scatter_last")
    halves["ffn1_in"][0] = chip_sum(last_pair[0], got, place, "chip_sum_ffn1_in_0")
    order = [(name, l) for name in BIG for l in range(2)]
    shard = dict(zip(order, pair_gather([halves[name][l] for name, l in order], "grad_pair_gather")))

    g_out, d_out, m_out, v_out = [], [], [], []
    for name, (w, m, v) in given.items():
        if name in BIG or name[2:] in BIG:
            key = name if name in BIG else name[2:]
            grad, delta, new_m, new_v = adamw_layers(w, shard[key, 0], shard[key, 1], m, v, f"adamw_{name}")
        else:
            grad = grads[name]
            delta, new_m, new_v = adamw(w, grad, m, v, f"adamw_{name}")
        g_out.append(grad)
        d_out.append(delta)
        m_out.append(new_m)
        v_out.append(new_v)
    return (loss, dx[None], *g_out, *d_out, *m_out, *v_out)
```

```python
import functools

import jax
import jax.numpy as jnp
from jax import lax
from jax.experimental import pallas as pl
from jax.experimental.pallas import tpu as pltpu

F32, BF16 = jnp.float32, jnp.bfloat16
D = 1024
D_FF = 2816
N_SLOT = 4
FF_COLS = 2 * D_FF // N_SLOT
N_MOD = 9
MOD_COLS = N_MOD * D // N_SLOT
POOL_W, ATTN_W, KV_W = 512, 512, 128
PROJ_W = POOL_W + ATTN_W + 2 * KV_W
N_HEADS, Q_GROUP, HEAD = 8, 4, 64
GROUP = 128
POOL_WINDOWS = (2, 4, 8, 16)
BLK = 128
QB = 256
WIN = QB + 2 * BLK
GRID_W = 64
ROPE_BASE = 10000.0
EPS = 1e-6
NEG_INF = -1e30
TM = 256
N_DEV = 8
VMEM_LIMIT_BYTES = 56 * 1024 * 1024
ADAM_LR, ADAM_B1, ADAM_B2, ADAM_EPS, ADAM_WD, ADAM_STEP = 0.001, 0.9, 0.999, 1e-08, 0.01, 10
MESH = pl.DeviceIdType.MESH
NT = (((1,), (1,)), ((), ()))
TN = (((0,), (0,)), ((), ()))


def _params(*sem):
    return pltpu.CompilerParams(dimension_semantics=sem, vmem_limit_bytes=VMEM_LIMIT_BYTES)


def _whole(shape, lead=()):
    idx = tuple(lead) + (0,) * len(shape)
    return pl.BlockSpec((None,) * len(lead) + tuple(shape), lambda *_: idx, pipeline_mode=pl.Buffered(1))


def _rows(cols, tm=TM):
    return pl.BlockSpec((tm, cols), lambda i: (i, 0))


def _mods_spec(layer, n_lat):
    return pl.BlockSpec((None, None, 16, D), lambda i: (layer, (i >= n_lat).astype(jnp.int32), 0, 0))


def _acc_spec(n_lat):
    return pl.BlockSpec((None, 8, D), lambda i: ((i >= n_lat).astype(jnp.int32), 0, 0))


def _dot(a, b):
    return jnp.dot(a, b, preferred_element_type=F32)


def _dotg(a, b, dims):
    return lax.dot_general(a, b, dims, preferred_element_type=F32)


def _sum0(v):
    return jnp.sum(v, axis=0, keepdims=True)


def _norm_mod(h, g, shift, scale):
    r = lax.rsqrt(jnp.mean(h * h, axis=-1, keepdims=True) + EPS)
    xhat = h * r
    y = xhat * g
    return y * (1 + scale) + shift, xhat, r, y


def _norm_mod_bwd(dn, xhat, r, y, g, scale):
    dy = dn * (1 + scale)
    dx = dy * g
    dh = r * (dx - xhat * jnp.mean(dx * xhat, axis=-1, keepdims=True))
    return _sum0(dn), _sum0(dn * y), _sum0(dy * xhat), dh


def _swap_halves(v):
    w = v.shape[1]
    lane = lax.broadcasted_iota(jnp.int32, v.shape, 1)
    return jnp.where(lane % HEAD < HEAD // 2, pltpu.roll(v, w - HEAD // 2, axis=1), pltpu.roll(v, HEAD // 2, axis=1))


def _tile_lanes(t, width):
    return t if width == t.shape[1] else jnp.concatenate([t] * (width // t.shape[1]), axis=1)


def _rope(v, cos, sin):
    return v * _tile_lanes(cos, v.shape[1]) + _swap_halves(v) * _tile_lanes(sin, v.shape[1])


def _unrope(g, cos, sin):
    return g * _tile_lanes(cos, g.shape[1]) + _swap_halves(g * _tile_lanes(sin, g.shape[1]))


def ffn_fwd(h, mods, g, w4, wo, layer, k0, n_lat, name, ex=None):
    s = h.shape[0]

    def body(h_ref, m_ref, g_ref, w_ref, wo_ref, ho_ref, ab_ref, f_ref):
        hh = h_ref[...]
        n, _, _, _ = _norm_mod(hh, g_ref[...], m_ref[k0:k0 + 1, :], m_ref[k0 + 1:k0 + 2, :])
        nb = n.astype(BF16)
        acc = jnp.zeros((TM, D), F32)
        for j in range(2):
            a = _dot(nb, w_ref[j])
            b = _dot(nb, w_ref[2 + j])
            ab_ref[:, j * FF_COLS:(j + 1) * FF_COLS] = a.astype(BF16)
            ab_ref[:, (2 + j) * FF_COLS:(3 + j) * FF_COLS] = b.astype(BF16)
            act = (a * jax.nn.sigmoid(a) * b).astype(BF16)
            acc = acc + _dot(act, wo_ref[j * FF_COLS:(j + 1) * FF_COLS, :])
        f_ref[...] = acc
        ho_ref[...] = hh + 0.5 * m_ref[k0 + 2:k0 + 3, :] * acc

    return _grid_call(
        body, name, s // TM,
        [_rows(D), _mods_spec(layer, n_lat), _whole((1, D), (layer,)), _whole((N_SLOT, D, FF_COLS)), _whole((D_FF, D))],
        [_rows(D), _rows(2 * D_FF), _rows(D)],
        [jax.ShapeDtypeStruct((s, D), F32), jax.ShapeDtypeStruct((s, 2 * D_FF), BF16), jax.ShapeDtypeStruct((s, D), F32)],
        (h, mods, g, w4, wo), "parallel", ex)


def ffn_bwd(h, ab, f, dh, mods, g, w4, wo, layer, k0, n_lat, name, ex=None):
    s = h.shape[0]

    def body(h_ref, ab_ref, f_ref, dh_ref, m_ref, g_ref, w_ref, wo_ref, dhi_ref, dab_ref, df_ref, n_ref, act_ref, dm_ref):
        i = pl.program_id(0)

        @pl.when((i == 0) | (i == n_lat))
        def _():
            dm_ref[...] = jnp.zeros_like(dm_ref)

        hh, dho, gg = h_ref[...], dh_ref[...], g_ref[...]
        scale, gate = m_ref[k0 + 1:k0 + 2, :], m_ref[k0 + 2:k0 + 3, :]
        n, xhat, r, y = _norm_mod(hh, gg, m_ref[k0:k0 + 1, :], scale)
        n_ref[...] = n.astype(BF16)
        dgate = _sum0(dho * (0.5 * f_ref[...]))
        dfb = ((0.5 * gate) * dho).astype(BF16)
        df_ref[...] = dfb
        dn = jnp.zeros((TM, D), F32)
        for j in range(2):
            a = ab_ref[:, j * FF_COLS:(j + 1) * FF_COLS].astype(F32)
            b = ab_ref[:, (2 + j) * FF_COLS:(3 + j) * FF_COLS].astype(F32)
            sg = jax.nn.sigmoid(a)
            sa = a * sg
            act_ref[:, j * FF_COLS:(j + 1) * FF_COLS] = (sa * b).astype(BF16)
            dact = _dotg(dfb, wo_ref[j * FF_COLS:(j + 1) * FF_COLS, :], NT)
            da = (dact * b * (sg * (1 + a * (1 - sg)))).astype(BF16)
            db = (dact * sa).astype(BF16)
            dab_ref[:, j * FF_COLS:(j + 1) * FF_COLS] = da
            dab_ref[:, (2 + j) * FF_COLS:(3 + j) * FF_COLS] = db
            dn = dn + _dotg(da, w_ref[j], NT) + _dotg(db, w_ref[2 + j], NT)
        dsh, dsc, dg, dhn = _norm_mod_bwd(dn, xhat, r, y, gg, scale)
        dhi_ref[...] = dho + dhn
        dm_ref[0:1, :] += dsh
        dm_ref[1:2, :] += dsc
        dm_ref[2:3, :] += dgate
        dm_ref[3:4, :] += dg

    return _grid_call(
        body, name, s // TM,
        [_rows(D), _rows(2 * D_FF), _rows(D), _rows(D), _mods_spec(layer, n_lat), _whole((1, D), (layer,)),
         _whole((N_SLOT, D, FF_COLS)), _whole((D_FF, D))],
        [_rows(D), _rows(2 * D_FF), _rows(D), _rows(D), _rows(D_FF), _acc_spec(n_lat)],
        [jax.ShapeDtypeStruct((s, D), F32), jax.ShapeDtypeStruct((s, 2 * D_FF), BF16), jax.ShapeDtypeStruct((s, D), BF16),
         jax.ShapeDtypeStruct((s, D), BF16), jax.ShapeDtypeStruct((s, D_FF), BF16), jax.ShapeDtypeStruct((2, 8, D), F32)],
        (h, ab, f, dh, mods, g, w4, wo), "arbitrary", ex)


def _token_tile(s, limit=2176):
    return max(ts for ts in range(16, limit + 1, 16) if s % ts == 0)


def wgrad(a, b, tk, tn, slot_cols, name, ex=None):
    s, k = a.shape
    n = b.shape[1]
    ts = _token_tile(s)
    steps = s // ts

    def body(a_ref, b_ref, o_ref, o16_ref):
        r = _dotg(a_ref[...], b_ref[...], TN)
        si = pl.program_id(2)

        @pl.when(si == 0)
        def _():
            o_ref[...] = r

        @pl.when(si > 0)
        def _():
            o_ref[...] += r

        @pl.when(si == steps - 1)
        def _():
            o16_ref[...] = o_ref[...].astype(BF16)

    if slot_cols is None:
        shape, spec = (k, n), pl.BlockSpec((tk, tn), lambda i, j, si: (i, j))
    else:
        per = slot_cols // tn
        shape, spec = (n // slot_cols, k, slot_cols), pl.BlockSpec((None, tk, tn), lambda i, j, si: (lax.div(j, per), i, lax.rem(j, per)))
    return _grid_call(
        body, name, (k // tk, n // tn, steps),
        [pl.BlockSpec((ts, tk), lambda i, j, si: (si, i)), pl.BlockSpec((ts, tn), lambda i, j, si: (si, j))], [spec, spec],
        [jax.ShapeDtypeStruct(shape, F32), jax.ShapeDtypeStruct(shape, BF16)], (a, b), ("parallel", "parallel", "arbitrary"), ex)


def proj_fwd(h, mods, g, w_in, cos, sin, layer, n_lat, name, ex=None):
    s = h.shape[0]

    def body(h_ref, m_ref, g_ref, w_ref, cos_ref, sin_ref, u_ref, q_ref, k_ref, v_ref):
        n, _, _, _ = _norm_mod(h_ref[...], g_ref[...], m_ref[3:4, :], m_ref[4:5, :])
        p = _dot(n.astype(BF16), w_ref[...])
        cs, sn = cos_ref[...], sin_ref[...]
        u_ref[...] = p[:, :POOL_W]
        q_ref[...] = (_rope(p[:, POOL_W:POOL_W + ATTN_W], cs, sn) * HEAD ** -0.5).astype(BF16)
        k_ref[...] = _rope(p[:, POOL_W + ATTN_W:POOL_W + ATTN_W + KV_W], cs, sn).astype(BF16)
        v_ref[...] = p[:, POOL_W + ATTN_W + KV_W:].astype(BF16)

    return _grid_call(
        body, name, s // TM,
        [_rows(D), _mods_spec(layer, n_lat), _whole((1, D), (layer,)), _whole((D, PROJ_W)), _rows(BLK), _rows(BLK)],
        [_rows(POOL_W), _rows(ATTN_W), _rows(KV_W), _rows(KV_W)],
        [jax.ShapeDtypeStruct((s, POOL_W), F32), jax.ShapeDtypeStruct((s, ATTN_W), BF16),
         jax.ShapeDtypeStruct((s, KV_W), BF16), jax.ShapeDtypeStruct((s, KV_W), BF16)],
        (h, mods, g, w_in, cos, sin), "parallel", ex)


def proj_bwd(h, du, dq, dk, dv, dh, mods, g, w_in, cos, sin, layer, n_lat, name):
    s = h.shape[0]

    def body(h_ref, du_ref, dq_ref, dk_ref, dv_ref, dh_ref, m_ref, g_ref, w_ref, cos_ref, sin_ref,
             dhi_ref, dp_ref, n_ref, dm_ref):
        i = pl.program_id(0)

        @pl.when((i == 0) | (i == n_lat))
        def _():
            dm_ref[...] = jnp.zeros_like(dm_ref)

        gg, scale = g_ref[...], m_ref[4:5, :]
        n, xhat, r, y = _norm_mod(h_ref[...], gg, m_ref[3:4, :], scale)
        n_ref[...] = n.astype(BF16)
        cs, sn = cos_ref[...], sin_ref[...]
        dp = jnp.concatenate([du_ref[...], _unrope(dq_ref[...], cs, sn) * HEAD ** -0.5, _unrope(dk_ref[...], cs, sn),
                              dv_ref[...]], axis=1).astype(BF16)
        dp_ref[...] = dp
        dsh, dsc, dg, dhn = _norm_mod_bwd(_dotg(dp, w_ref[...], NT), xhat, r, y, gg, scale)
        dhi_ref[...] = dh_ref[...] + dhn
        dm_ref[0:1, :] += dsh
        dm_ref[1:2, :] += dsc
        dm_ref[3:4, :] += dg

    return pl.pallas_call(
        body, name=name, grid=(s // TM,),
        in_specs=[_rows(D), _rows(POOL_W), _rows(ATTN_W), _rows(KV_W), _rows(KV_W), _rows(D), _mods_spec(layer, n_lat),
                  _whole((1, D), (layer,)), _whole((D, PROJ_W)), _rows(BLK), _rows(BLK)],
        out_specs=[_rows(D), _rows(PROJ_W), _rows(D), _acc_spec(n_lat)],
        out_shape=[jax.ShapeDtypeStruct((s, D), F32), jax.ShapeDtypeStruct((s, PROJ_W), BF16),
                   jax.ShapeDtypeStruct((s, D), BF16), jax.ShapeDtypeStruct((2, 8, D), F32)],
        compiler_params=_params("arbitrary"),
    )(h, du, dq, dk, dv, dh, mods, g, w_in, cos, sin)


def _window(i, s):
    return pl.multiple_of(jnp.clip(i * QB - BLK, 0, s - WIN), BLK)


def mix_tables(t, s):
    n_lat = t // QB
    blocks = jnp.array([0, 1, n_lat - 1] + list(range(n_lat, s // QB)))[:, None, None]
    ws = jnp.clip(blocks * QB - BLK, 0, s - WIN)
    q = blocks * QB + jnp.arange(QB)[None, :, None]
    k = ws + jnp.arange(WIN)[None, None, :]
    is_lat = blocks < n_lat
    local = jnp.where(is_lat & (k < t) & (jnp.abs(k - q) <= BLK), 0.0, NEG_INF).astype(F32)
    bias = jnp.concatenate([local, jnp.zeros(local.shape[:2] + (s - t,), F32)], axis=2)
    seq_lo, seq_hi = jnp.where(is_lat, 0, t), jnp.where(is_lat, t, s)
    bands, counts = [], []
    for w in POOL_WINDOWS:
        lo, hi = jnp.maximum(q - w // 2, seq_lo), jnp.minimum(q + w - w // 2, seq_hi)
        bands.append((k >= lo) & (k < hi))
        counts.append((hi - lo).astype(F32))
    band = jnp.stack(bands, axis=1).astype(BF16)
    count = jnp.concatenate(counts + [jnp.ones(counts[0].shape[:2] + (BLK - len(counts),), F32)], axis=2)
    return dict(bias=bias, band=band, band_t=band.transpose(0, 1, 3, 2), count=count)


def _case_spec(table, n_lat_blk):
    def kind(i):
        return jnp.where(i < n_lat_blk - 1, jnp.minimum(i, 1), i - n_lat_blk + 3)

    shape = table.shape[1:]
    return pl.BlockSpec((None,) + shape, lambda i: (kind(i),) + (0,) * len(shape))


def _split_dot(band, v):
    return _dot(band, v.astype(BF16))


def _pooled(u_ref, band_ref, cnt_ref, i, ws, gi):
    cols = slice(gi * GROUP, (gi + 1) * GROUP)
    mean = _split_dot(band_ref[gi], u_ref[pl.ds(ws, WIN), cols]) / cnt_ref[:, gi:gi + 1]
    return mean - u_ref[pl.ds(pl.multiple_of(i * QB, QB), QB), cols]


def _head_cols(hd):
    return slice(hd * HEAD, (hd + 1) * HEAD)


def _stack_heads(x, hk, first=0):
    return jnp.concatenate([x[:, first + (Q_GROUP * hk + g) * HEAD:first + (Q_GROUP * hk + g + 1) * HEAD]
                            for g in range(Q_GROUP)], axis=0)


def _biased(scores, bias):
    return (scores.reshape(Q_GROUP, QB, -1) + bias).reshape(Q_GROUP * QB, -1)


def _group_column(vals):
    row = lax.broadcasted_iota(jnp.int32, (Q_GROUP * QB, 1), 0)
    out = jnp.full((Q_GROUP * QB, 1), vals[Q_GROUP - 1], F32)
    for g in range(Q_GROUP - 2, -1, -1):
        out = jnp.where(row < (g + 1) * QB, vals[g], out)
    return out


def _lane_place(cols, width=BLK):
    lane = lax.broadcasted_iota(jnp.int32, (cols[0].shape[0], width), 1)
    out = jnp.zeros((cols[0].shape[0], width), F32)
    for hd, c in enumerate(cols):
        out = jnp.where(lane == hd, c, out)
    return out


def mix_fwd(h, q, k, v, u, w_pool, pool_scale, sink, w_out, mods, tables, layer, t, name, ex=None):
    s = h.shape[0]
    n_lat_blk = t // QB

    def body(h_ref, q_ref, k_ref, v_ref, u_ref, wp_ref, ps_ref, sink_ref, wo_ref, m_ref, bias_ref, band_ref, cnt_ref,
             ho_ref, cat_ref, lse_ref, mo_ref):
        i = pl.program_id(0)
        ws = _window(i, s)
        for gi in range(len(POOL_WINDOWS)):
            mixed = _dot(_pooled(u_ref, band_ref, cnt_ref, i, ws, gi).astype(BF16), wp_ref[gi])
            cat_ref[:, gi * GROUP:(gi + 1) * GROUP] = (mixed * ps_ref[:, gi * GROUP:(gi + 1) * GROUP]).astype(BF16)
        bias = bias_ref[...]
        k_all = jnp.concatenate([k_ref[pl.ds(ws, WIN), :], k_ref[t:s, :]], axis=0)
        v_all = jnp.concatenate([v_ref[pl.ds(ws, WIN), :], v_ref[t:s, :]], axis=0)
        lses = []
        for hk in range(N_HEADS // Q_GROUP):
            kv = _head_cols(hk)
            sc = _biased(_dotg(_stack_heads(q_ref[...], hk), k_all[:, kv], NT), bias)
            sk = _group_column([sink_ref[layer, Q_GROUP * hk + g] for g in range(Q_GROUP)])
            m = jnp.maximum(jnp.max(sc, axis=1, keepdims=True), sk)
            e = jnp.exp(sc - m)
            l = jnp.sum(e, axis=1, keepdims=True) + jnp.exp(sk - m)
            o = _dot(e.astype(BF16), v_all[:, kv]) * (1.0 / l)
            lse = m + jnp.log(l)
            for g in range(Q_GROUP):
                hd = Q_GROUP * hk + g
                cat_ref[:, POOL_W + hd * HEAD:POOL_W + (hd + 1) * HEAD] = o[g * QB:(g + 1) * QB].astype(BF16)
                lses.append(lse[g * QB:(g + 1) * QB])
        lse_ref[...] = _lane_place(lses)
        mo = _dot(cat_ref[...], wo_ref[...])
        mo_ref[...] = mo
        ho_ref[...] = h_ref[...] + m_ref[5:6, :] * mo

    blk = lambda cols: _rows(cols, QB)
    return _grid_call(
        body, name, s // QB,
        [blk(D), blk(ATTN_W), _whole((s, KV_W)), _whole((s, KV_W)), _whole((s, POOL_W)),
         _whole((len(POOL_WINDOWS), GROUP, GROUP), (layer,)), _whole((1, POOL_W), (layer,)),
         pl.BlockSpec(memory_space=pltpu.SMEM), _whole((POOL_W + ATTN_W, D)), _mods_spec(layer, n_lat_blk),
         _case_spec(tables["bias"], n_lat_blk), _case_spec(tables["band"], n_lat_blk), _case_spec(tables["count"], n_lat_blk)],
        [blk(D), blk(POOL_W + ATTN_W), blk(BLK), blk(D)],
        [jax.ShapeDtypeStruct((s, D), F32), jax.ShapeDtypeStruct((s, POOL_W + ATTN_W), BF16), jax.ShapeDtypeStruct((s, BLK), F32),
         jax.ShapeDtypeStruct((s, D), F32)],
        (h, q, k, v, u, w_pool, pool_scale, sink, w_out, mods, tables["bias"], tables["band"], tables["count"]), "parallel", ex)


def mix_bwd(dh, mo, q, k, v, u, lse, w_pool, pool_scale, sink, w_out, mods, tables, layer, t, name, ex=None):
    s = dh.shape[0]
    n_lat_blk = t // QB
    n_grp = len(POOL_WINDOWS)

    def body(dh_ref, mo_ref, q_ref, k_ref, v_ref, u_ref, lse_ref, wp_ref, ps_ref, sink_ref, wo_ref, m_ref,
             bias_ref, band_ref, band_t_ref, cnt_ref,
             dq_ref, dk_ref, dv_ref, du_ref, dmo_ref, dwp_ref, dps_ref, dsink_ref, dm_ref):
        i = pl.program_id(0)

        @pl.when(i == 0)
        def _():
            for ref in (dk_ref, dv_ref, du_ref, dwp_ref, dps_ref, dsink_ref):
                ref[...] = jnp.zeros_like(ref)

        @pl.when((i == 0) | (i == n_lat_blk))
        def _():
            dm_ref[...] = jnp.zeros_like(dm_ref)

        ws = _window(i, s)
        here = pl.ds(pl.multiple_of(i * QB, QB), QB)
        dho = dh_ref[...]
        dm_ref[2:3, :] += _sum0(dho * mo_ref[...])
        dmo = (m_ref[5:6, :] * dho).astype(BF16)
        dmo_ref[...] = dmo
        dcat = _dotg(dmo, wo_ref[...], NT)

        for gi in range(n_grp):
            cols = slice(gi * GROUP, (gi + 1) * GROUP)
            pooled = _pooled(u_ref, band_ref, cnt_ref, i, ws, gi).astype(BF16)
            dpo = dcat[:, cols]
            dps_ref[0:1, cols] += _sum0(dpo * _dot(pooled, wp_ref[gi]))
            dmixed = (dpo * ps_ref[:, cols]).astype(BF16)
            dwp_ref[gi] += _dotg(pooled, dmixed, TN)
            dpooled = _dotg(dmixed, wp_ref[gi], NT)
            du_ref[pl.ds(ws, WIN), cols] += _split_dot(band_t_ref[gi], dpooled / cnt_ref[:, gi:gi + 1])
            du_ref[here, cols] -= dpooled

        bias = bias_ref[...]
        k_all = jnp.concatenate([k_ref[pl.ds(ws, WIN), :], k_ref[t:s, :]], axis=0)
        v_all = jnp.concatenate([v_ref[pl.ds(ws, WIN), :], v_ref[t:s, :]], axis=0)
        qq, lse_all = q_ref[...], lse_ref[...]
        dqs, dsinks, dks, dvs = [], [], [], []
        for hk in range(N_HEADS // Q_GROUP):
            kv = _head_cols(hk)
            q4 = _stack_heads(qq, hk)
            lse = jnp.concatenate([lse_all[:, Q_GROUP * hk + g:Q_GROUP * hk + g + 1] for g in range(Q_GROUP)], axis=0)
            p = jnp.exp(_biased(_dotg(q4, k_all[:, kv], NT), bias) - lse)
            do = _stack_heads(dcat, hk, POOL_W).astype(BF16)
            dp = _dotg(do, v_all[:, kv], NT)
            delta = jnp.sum(p * dp, axis=1, keepdims=True)
            ds = (p * (dp - delta)).astype(BF16)
            sk = _group_column([sink_ref[layer, Q_GROUP * hk + g] for g in range(Q_GROUP)])
            dsk = -jnp.exp(sk - lse) * delta
            dq = _dot(ds, k_all[:, kv])
            for g in range(Q_GROUP):
                dqs.append(dq[g * QB:(g + 1) * QB])
                dsinks.append(_sum0(dsk[g * QB:(g + 1) * QB]))
            dks.append(_dotg(ds, q4, TN))
            dvs.append(_dotg(p.astype(BF16), do, TN))
        dq_ref[...] = jnp.concatenate(dqs, axis=1)
        dk, dv = jnp.concatenate(dks, axis=1), jnp.concatenate(dvs, axis=1)
        dk_ref[pl.ds(ws, WIN), :] += dk[:WIN]
        dv_ref[pl.ds(ws, WIN), :] += dv[:WIN]
        dk_ref[t:s, :] += dk[WIN:]
        dv_ref[t:s, :] += dv[WIN:]
        dsink_ref[0:1, :] += _lane_place(dsinks)

    blk = lambda cols: _rows(cols, QB)
    full = lambda shape: pl.BlockSpec(shape, lambda i: (0,) * len(shape))
    return _grid_call(
        body, name, s // QB,
        [blk(D), blk(D), blk(ATTN_W), _whole((s, KV_W)), _whole((s, KV_W)), _whole((s, POOL_W)),
         blk(BLK), _whole((n_grp, GROUP, GROUP), (layer,)), _whole((1, POOL_W), (layer,)),
         pl.BlockSpec(memory_space=pltpu.SMEM), _whole((POOL_W + ATTN_W, D)), _mods_spec(layer, n_lat_blk)]
        + [_case_spec(tables[key], n_lat_blk) for key in ("bias", "band", "band_t", "count")],
        [blk(ATTN_W), full((s, KV_W)), full((s, KV_W)), full((s, POOL_W)), blk(D),
         full((n_grp, GROUP, GROUP)), full((8, POOL_W)), full((8, BLK)), _acc_spec(n_lat_blk)],
        [jax.ShapeDtypeStruct((s, ATTN_W), F32), jax.ShapeDtypeStruct((s, KV_W), F32),
         jax.ShapeDtypeStruct((s, KV_W), F32), jax.ShapeDtypeStruct((s, POOL_W), F32),
         jax.ShapeDtypeStruct((s, D), BF16), jax.ShapeDtypeStruct((n_grp, GROUP, GROUP), F32),
         jax.ShapeDtypeStruct((8, POOL_W), F32), jax.ShapeDtypeStruct((8, BLK), F32), jax.ShapeDtypeStruct((2, 8, D), F32)],
        (dh, mo, q, k, v, u, lse, w_pool, pool_scale, sink, w_out, mods, tables["bias"], tables["band"], tables["band_t"],
         tables["count"]), "arbitrary", ex)


def loss_head(h, target, g, t, name):
    s = h.shape[0]
    n_lat = t // TM

    def body(h_ref, t_ref, g_ref, dh_ref, acc_ref):
        i = pl.program_id(0)

        @pl.when(i == 0)
        def _():
            acc_ref[...] = jnp.zeros_like(acc_ref)

        @pl.when(i < n_lat)
        def _():
            hh, gg = h_ref[...], g_ref[...]
            r = lax.rsqrt(jnp.mean(hh * hh, axis=-1, keepdims=True) + EPS)
            xhat = hh * r
            err = xhat * gg - t_ref[...]
            dy = err * (1.0 / D)
            dx = dy * gg
            dh_ref[...] = r * (dx - xhat * jnp.mean(dx * xhat, axis=-1, keepdims=True))
            acc_ref[0:1, :] += _sum0(dy * xhat)
            acc_ref[1:2, :] += _sum0(err * err)

        @pl.when(i >= n_lat)
        def _():
            dh_ref[...] = jnp.zeros_like(dh_ref)

    return pl.pallas_call(
        body, name=name, grid=(s // TM,),
        in_specs=[_rows(D), pl.BlockSpec((TM, D), lambda i: (jnp.minimum(i, n_lat - 1), 0)), _whole((1, D))],
        out_specs=[_rows(D), pl.BlockSpec((8, D), lambda i: (0, 0))],
        out_shape=[jax.ShapeDtypeStruct((s, D), F32), jax.ShapeDtypeStruct((8, D), F32)],
        compiler_params=_params("arbitrary"),
    )(h, target, g)


def mod_rows(c_all, w_mod, b_cols, name):
    def body(c_ref, w_ref, b_ref, o_ref):
        cc = c_ref[...]
        o_ref[...] = _dot((cc * jax.nn.sigmoid(cc)).astype(BF16), w_ref[...].astype(BF16)) + b_ref[...]

    return pl.pallas_call(
        body, name=name, grid=(2,),
        in_specs=[pl.BlockSpec((16, D), lambda l: (0, 0)), pl.BlockSpec((None, D, MOD_COLS), lambda l: (l, 0, 0)),
                  pl.BlockSpec((None, 1, MOD_COLS), lambda l: (l, 0, 0))],
        out_specs=pl.BlockSpec((None, 16, MOD_COLS), lambda l: (l, 0, 0)),
        out_shape=jax.ShapeDtypeStruct((2, 16, MOD_COLS), F32),
        compiler_params=_params("parallel"),
    )(c_all, w_mod, b_cols)


def mod_grads(c_all, dmod_cols, w_mod, name):
    def body(c_ref, d_ref, w_ref, dw_ref, dc_ref):
        @pl.when(pl.program_id(0) == 0)
        def _():
            dc_ref[...] = jnp.zeros_like(dc_ref)

        cc = c_ref[...]
        dd = d_ref[...].astype(BF16)
        dw_ref[...] = _dotg((cc * jax.nn.sigmoid(cc)).astype(BF16), dd, TN)
        dc_ref[...] += _dotg(dd, w_ref[...].astype(BF16), NT)

    return pl.pallas_call(
        body, name=name, grid=(2,),
        in_specs=[pl.BlockSpec((16, D), lambda l: (0, 0)), pl.BlockSpec((None, 16, MOD_COLS), lambda l: (l, 0, 0)),
                  pl.BlockSpec((None, D, MOD_COLS), lambda l: (l, 0, 0))],
        out_specs=[pl.BlockSpec((None, D, MOD_COLS), lambda l: (l, 0, 0)), pl.BlockSpec((16, D), lambda l: (0, 0))],
        out_shape=[jax.ShapeDtypeStruct((2, D, MOD_COLS), F32), jax.ShapeDtypeStruct((16, D), F32)],
        compiler_params=_params("arbitrary"),
    )(c_all, dmod_cols, w_mod)


def _row_tile(rows, cols, n_arrays):
    budget = VMEM_LIMIT_BYTES // 4 // (2 * 4 * n_arrays * cols)
    best = None
    for tr in range(16, rows + 1, 16):
        if rows % tr == 0 and tr <= budget:
            best = tr
    return best if best is not None else rows


def elementwise(fn, ins, out_dtypes, name, ex=None):
    rows, cols = ins[0].shape
    tr = _row_tile(rows, cols, len(ins) + len(out_dtypes))

    def body(*refs):
        outs = fn(*[r[...] for r in refs[:len(ins)]])
        for o_ref, o in zip(refs[len(ins):], outs):
            o_ref[...] = o.astype(o_ref.dtype)

    spec = pl.BlockSpec((tr, cols), lambda i: (i, 0))
    outs, got = _grid_call(body, name, rows // tr, [spec] * len(ins), [spec] * len(out_dtypes),
                           [jax.ShapeDtypeStruct((rows, cols), dt) for dt in out_dtypes], ins, "parallel", ex)
    return outs if ex is None else (outs, got)


def _adamw_tile(w, g, m, v):
    m = ADAM_B1 * m + (1.0 - ADAM_B1) * g
    v = ADAM_B2 * v + (1.0 - ADAM_B2) * (g * g)
    m_hat = m / (1.0 - ADAM_B1 ** ADAM_STEP)
    v_hat = v / (1.0 - ADAM_B2 ** ADAM_STEP)
    return -ADAM_LR * (m_hat / (jnp.sqrt(v_hat) + ADAM_EPS) + ADAM_WD * w), m, v


def adamw(w, g, m, v, name, ex=None):
    shape = w.shape
    two_d = (-1, shape[-1]) if w.ndim > 1 else (1, -1)
    outs = elementwise(_adamw_tile, [a.reshape(two_d) for a in (w, g, m, v)], [F32] * 3, name, ex)
    outs, got = outs if ex is not None else (outs, None)
    outs = [o.reshape(shape) for o in outs]
    return outs if ex is None else (outs, got)


def _prefetch_call(body, name, grid, in_specs, out_specs, out_shape, place, args, ex=None):
    if ex is None:
        spec = pltpu.PrefetchScalarGridSpec(num_scalar_prefetch=1, grid=grid, in_specs=in_specs, out_specs=out_specs)
        return pl.pallas_call(body, name=name, grid_spec=spec, out_shape=out_shape,
                              compiler_params=_params(*["parallel"] * len(grid)))(place, *args)
    n_in, n_out, ci, co = len(in_specs), len(out_specs), len(ex["ins"]), len(ex["out_shape"])
    spec = pltpu.PrefetchScalarGridSpec(num_scalar_prefetch=1, grid=grid, in_specs=list(in_specs) + _any(ci),
                                        out_specs=list(out_specs) + _any(co), scratch_shapes=ex["scratch"])
    outs = pl.pallas_call(
        _carrying(body, grid, n_in, n_out, ex, lead=1), name=name, grid_spec=spec, out_shape=list(out_shape) + ex["out_shape"],
        input_output_aliases={1 + n_in + i: n_out + j for i, j in ex["aliases"].items()},
        compiler_params=_params(*["arbitrary"] * len(grid)))(place, *args, *ex["ins"])
    return outs[:n_out], outs[n_out:]


def cast_place(w, layer, place, name):
    _, r, c = w.shape
    tr = _row_tile(r, c, 2)

    def body(p_ref, w_ref, o_ref):
        o_ref[...] = w_ref[...].astype(BF16)

    return _prefetch_call(
        body, name, (r // tr,), [pl.BlockSpec((None, tr, c), lambda i, p: (layer, i, 0))],
        pl.BlockSpec((None, tr, c), lambda i, p: (p[1], i, 0)), jax.ShapeDtypeStruct((N_SLOT, r, c), BF16), place, [w])


def pair_sum(g32, got, place, name, ex=None):
    n_slot, rh, c = got.shape
    tr = _row_tile(rh, c, 4)
    per = rh // tr

    def body(p_ref, a_ref, b_ref, o_ref, o16_ref):
        r = a_ref[...] + b_ref[...].astype(F32)
        o_ref[...] = r
        o16_ref[...] = r.astype(BF16)

    half = pl.BlockSpec((None, tr, c), lambda s, i, p: (s, i, 0))
    return _prefetch_call(
        body, name, (n_slot, per), [pl.BlockSpec((None, tr, c), lambda s, i, p: (s, p[0] * per + i, 0)), half], [half, half],
        [jax.ShapeDtypeStruct(got.shape, F32), jax.ShapeDtypeStruct(got.shape, BF16)], place, [g32, got], ex)


def chip_sum(p32, got, place, name):
    _, rh, c = p32.shape
    tr = _row_tile(rh, c, 5)
    per = rh // tr

    def body(p_ref, m_ref, r0_ref, r1_ref, r2_ref, o_ref):
        o_ref[...] = m_ref[...] + r0_ref[...].astype(F32) + r1_ref[...].astype(F32) + r2_ref[...].astype(F32)

    part = pl.BlockSpec((tr, c), lambda i, p: (i, 0))
    return _prefetch_call(
        body, name, (per,), [pl.BlockSpec((None, tr, c), lambda i, p: (p[1], i, 0)), part, part, part],
        pl.BlockSpec((tr, c), lambda i, p: (p[0] * per + i, 0)), jax.ShapeDtypeStruct((2 * rh, c), F32), place, [p32, *got])


def adamw_layers(w, g0, g1, m, v, name, ex=None):
    _, r, c = w.shape
    tr = _row_tile(r, c, 10)

    def body(w_ref, g0_ref, g1_ref, m_ref, v_ref, g_ref, d_ref, mo_ref, vo_ref):
        g = jnp.where(pl.program_id(0) == 0, g0_ref[...], g1_ref[...])
        g_ref[...] = g
        d_ref[...], mo_ref[...], vo_ref[...] = _adamw_tile(w_ref[...], g, m_ref[...], v_ref[...])

    stacked = pl.BlockSpec((None, tr, c), lambda l, i: (l, i, 0))
    layer = pl.BlockSpec((tr, c), lambda l, i: (i, 0))
    outs, got = _grid_call(body, name, (2, r // tr), [stacked, layer, layer, stacked, stacked], [stacked] * 4,
                           [jax.ShapeDtypeStruct(w.shape, F32)] * 4, (w, g0, g1, m, v), "parallel", ex)
    return outs if ex is None else (outs, got)


def sum8(gathered, name):
    def body(*refs):
        n = len(refs) // 2
        for g_ref, o_ref in zip(refs[:n], refs[n:]):
            acc = g_ref[0]
            for dev in range(1, N_DEV):
                acc = acc + g_ref[dev]
            o_ref[...] = acc

    return pl.pallas_call(
        body, name=name,
        out_shape=[jax.ShapeDtypeStruct(a.shape[1:], F32) for a in gathered],
        compiler_params=_params(),
    )(*gathered)


PHASES = ("start", "late", "finish")


def _place():
    return lax.axis_index("x"), lax.axis_index("y"), lax.axis_index("c")


def _any(n):
    return [pl.BlockSpec(memory_space=pl.ANY)] * n


def gather8_exchange(blocks):
    n = len(blocks)

    def copy(outs, sems, ti, k, block, to, src=None):
        dst = outs[ti].at[4 * block[0] + 2 * block[1] + block[2]]
        return pltpu.make_async_remote_copy(src_ref=dst if src is None else src, dst_ref=dst, send_sem=sems[0].at[ti, k],
                                            recv_sem=sems[1].at[ti, k], device_id=to, device_id_type=MESH)

    def first(ins, outs, sems):
        x, y, c = _place()
        local, sent = [], []
        for ti in range(n):
            local.append(pltpu.make_async_copy(ins[ti], outs[ti].at[4 * x + 2 * y + c], sems[2].at[ti]))
            sent.append(copy(outs, sems, ti, 0, (x, y, c), (x, y, 1 - c), src=ins[ti]))
            sent += [copy(outs, sems, ti, 1 + j, (x, y, c), (*chip, c), src=ins[ti]) for j, chip in enumerate(_three_chips(x, y))]
        return local, sent

    def start(ins, outs, sems):
        local, sent = first(ins, outs, sems)
        for cp in local + sent:
            cp.start()

    def passed_on(outs, sems):
        x, y, c = _place()
        return [copy(outs, sems, ti, 4 + j, (*chip, c), (x, y, 1 - c)) for ti in range(n) for j, chip in enumerate(_three_chips(x, y))]

    def late(ins, outs, sems):
        x, y, c = _place()
        on = passed_on(outs, sems)
        for ti in range(n):
            for j, chip in enumerate(_three_chips(x, y)):
                copy(outs, sems, ti, 1 + j, (*chip, c), (x, y, c)).wait_recv()
                on[3 * ti + j].start()

    def finish(ins, outs, sems):
        x, y, c = _place()
        me, sibling = (x, y, c), (x, y, 1 - c)
        local, sent = first(ins, outs, sems)
        for ti in range(n):
            copy(outs, sems, ti, 0, sibling, me).wait_recv()
            for j, chip in enumerate(_three_chips(x, y)):
                copy(outs, sems, ti, 4 + j, (*chip, 1 - c), me).wait_recv()
        for cp in sent + passed_on(outs, sems):
            cp.wait_send()
        for cp in local:
            cp.wait()

    return dict(ins=list(blocks), out_shape=[jax.ShapeDtypeStruct((N_DEV,) + b.shape, b.dtype) for b in blocks], aliases={},
                start=start, late=late, finish=finish,
                scratch=[pltpu.SemaphoreType.DMA((n, 7)), pltpu.SemaphoreType.DMA((n, 7)), pltpu.SemaphoreType.DMA((n,))])


def all_gather(blocks, name):
    return run_exchange(gather8_exchange(blocks), name)


def _three_chips(x, y):
    return [(1 - x, y), (x, 1 - y), (1 - x, 1 - y)]


def gather_exchange(placed):
    n = len(placed)

    def copy(bufs, sems, ti, k, chip, core, to):
        rh = bufs[ti].shape[1] // 2
        half = bufs[ti].at[2 * chip[0] + chip[1], pl.ds(core * rh, rh), :]
        return pltpu.make_async_remote_copy(src_ref=half, dst_ref=half, send_sem=sems[0].at[ti, k], recv_sem=sems[1].at[ti, k],
                                            device_id=to, device_id_type=MESH)

    def sends(bufs, sems):
        x, y, c = _place()
        return [copy(bufs, sems, ti, k, (x, y), c, (*chip, c)) for ti in range(n) for k, chip in enumerate(_three_chips(x, y))]

    def passed_on(bufs, sems):
        x, y, c = _place()
        return [copy(bufs, sems, ti, 3 + k, chip, c, (x, y, 1 - c)) for ti in range(n) for k, chip in enumerate(_three_chips(x, y))]

    def start(ins, bufs, sems):
        for cp in sends(bufs, sems):
            cp.start()

    def late(ins, bufs, sems):
        x, y, c = _place()
        on = passed_on(bufs, sems)
        for ti in range(n):
            for k, chip in enumerate(_three_chips(x, y)):
                copy(bufs, sems, ti, k, chip, c, (x, y, c)).wait_recv()
                on[3 * ti + k].start()

    def finish(ins, bufs, sems):
        x, y, c = _place()
        for ti in range(n):
            for k, chip in enumerate(_three_chips(x, y)):
                copy(bufs, sems, ti, 3 + k, chip, 1 - c, (x, y, c)).wait_recv()
        for cp in sends(bufs, sems) + passed_on(bufs, sems):
            cp.wait_send()

    return dict(ins=list(placed), out_shape=[jax.ShapeDtypeStruct(w.shape, w.dtype) for w in placed],
                aliases={i: i for i in range(n)}, start=start, late=late, finish=finish,
                scratch=[pltpu.SemaphoreType.DMA((n, 6)), pltpu.SemaphoreType.DMA((n, 6))])


def scatter_exchange(p16):
    n = len(p16)

    def copies(ins, got, sems):
        x, y, c = _place()
        return [pltpu.make_async_remote_copy(src_ref=ins[ti].at[2 * chip[0] + chip[1]], dst_ref=got[3 * ti + k],
                                             send_sem=sems[0].at[ti, k], recv_sem=sems[1].at[ti, k], device_id=(*chip, c),
                                             device_id_type=MESH)
                for ti in range(n) for k, chip in enumerate(_three_chips(x, y))]

    def start(ins, got, sems):
        for cp in copies(ins, got, sems):
            cp.start()

    def finish(ins, got, sems):
        for cp in copies(ins, got, sems):
            cp.wait()

    return dict(ins=list(p16), out_shape=[jax.ShapeDtypeStruct(a.shape[1:], BF16) for a in p16 for _ in range(3)], aliases={},
                start=start, finish=finish, scratch=[pltpu.SemaphoreType.DMA((n, 3)), pltpu.SemaphoreType.DMA((n, 3))])


def run_exchange(ex, name):
    ci, co = len(ex["ins"]), len(ex["out_shape"])

    def body(*refs):
        ins, outs, sems = refs[:ci], refs[ci:ci + co], refs[ci + co:]
        for phase in PHASES:
            if phase in ex:
                ex[phase](ins, outs, sems)

    return pl.pallas_call(body, name=name, in_specs=_any(ci), out_specs=_any(co), out_shape=ex["out_shape"],
                          input_output_aliases=ex["aliases"], scratch_shapes=ex["scratch"])(*ex["ins"])


def _carrying(body, grid, n_in, n_out, ex, lead=0):
    ci, co = len(ex["ins"]), len(ex["out_shape"])
    first, last = (0,) * len(grid), tuple(g - 1 for g in grid)
    steps = dict(start=first, late=(grid[0] - 2,) if len(grid) == 1 and grid[0] > 2 else last, finish=last)

    def at(ids):
        return functools.reduce(jnp.logical_and, [pl.program_id(ax) == v for ax, v in enumerate(ids)])

    def carrying(*refs):
        head, refs = refs[:lead], refs[lead:]
        c_in, c_out = refs[n_in:n_in + ci], refs[n_in + ci + n_out:n_in + ci + n_out + co]
        sems = refs[n_in + ci + n_out + co:]
        for phase in PHASES:
            if phase == "finish":
                body(*head, *refs[:n_in], *refs[n_in + ci:n_in + ci + n_out])
            if phase in ex:
                pl.when(at(steps[phase]))(functools.partial(ex[phase], c_in, c_out, sems))

    return carrying


def _grid_call(body, name, grid, in_specs, out_specs, out_shape, args, sem, ex=None):
    grid = (grid,) if isinstance(grid, int) else tuple(grid)
    sems_of = (sem,) * len(grid) if isinstance(sem, str) else tuple(sem)
    n_in, n_out = len(in_specs), len(out_specs)
    if ex is None:
        return pl.pallas_call(body, name=name, grid=grid, in_specs=in_specs, out_specs=out_specs, out_shape=out_shape,
                              compiler_params=_params(*sems_of))(*args), []
    ci, co = len(ex["ins"]), len(ex["out_shape"])
    outs = pl.pallas_call(
        _carrying(body, grid, n_in, n_out, ex), name=name, grid=grid, in_specs=list(in_specs) + _any(ci),
        out_specs=list(out_specs) + _any(co), out_shape=list(out_shape) + ex["out_shape"], scratch_shapes=ex["scratch"],
        input_output_aliases={n_in + i: n_out + j for i, j in ex["aliases"].items()},
        compiler_params=_params(*["arbitrary"] * len(grid)),
    )(*args, *ex["ins"])
    return outs[:n_out], outs[n_out:]


def both(*exchanges):
    exchanges = [ex for ex in exchanges if ex is not None]
    if len(exchanges) < 2:
        return exchanges[0] if exchanges else None
    n_ins = [len(ex["ins"]) for ex in exchanges]
    n_outs = [len(ex["out_shape"]) for ex in exchanges]
    n_sems = [len(ex["scratch"]) for ex in exchanges]

    def parts(seq, counts, k):
        first = sum(counts[:k])
        return seq[first:first + counts[k]]

    def run(phase):
        def go(ins, outs, sems):
            for k, ex in enumerate(exchanges):
                if phase in ex:
                    ex[phase](parts(ins, n_ins, k), parts(outs, n_outs, k), parts(sems, n_sems, k))
        return go

    aliases = {sum(n_ins[:k]) + i: sum(n_outs[:k]) + j for k, ex in enumerate(exchanges) for i, j in ex["aliases"].items()}
    return dict(ins=[a for ex in exchanges for a in ex["ins"]], out_shape=[o for ex in exchanges for o in ex["out_shape"]],
                aliases=aliases, scratch=[s for ex in exchanges for s in ex["scratch"]], **{ph: run(ph) for ph in PHASES})


def split_outputs(got, *exchanges):
    got, out = list(got), []
    for ex in exchanges:
        n = len(ex["out_shape"]) if ex is not None else 0
        out.append(got[:n])
        got = got[n:]
    return out


def pair_exchange(g16):
    n = len(g16)

    def copies(a16, got, sems):
        x, y, c = _place()
        out = []
        for ti in range(n):
            rh = a16[ti].shape[1] // 2
            out.append(pltpu.make_async_remote_copy(
                src_ref=a16[ti].at[:, pl.ds((1 - c) * rh, rh), :], dst_ref=got[ti], send_sem=sems[0].at[ti],
                recv_sem=sems[1].at[ti], device_id=(x, y, 1 - c), device_id_type=MESH))
        return out

    def start(a16, got, sems):
        for cp in copies(a16, got, sems):
            cp.start()

    def finish(a16, got, sems):
        for cp in copies(a16, got, sems):
            cp.wait()

    return dict(ins=list(g16), out_shape=[jax.ShapeDtypeStruct((a.shape[0], a.shape[1] // 2, a.shape[2]), BF16) for a in g16],
                aliases={}, start=start, finish=finish, scratch=[pltpu.SemaphoreType.DMA((n,)), pltpu.SemaphoreType.DMA((n,))])


def _scatter_copies(src_ref, lands, send_sems, recv_sems):
    x, y, c = _place()
    return [pltpu.make_async_remote_copy(src_ref=src_ref.at[2 * chip[0] + chip[1]], dst_ref=lands[k], send_sem=send_sems.at[k],
                                         recv_sem=recv_sems.at[k], device_id=(*chip, c), device_id_type=MESH)
            for k, chip in enumerate(_three_chips(x, y))]


def scatter_start(p16, name):
    hbm, sem = pl.BlockSpec(memory_space=pltpu.HBM), pl.BlockSpec(memory_space=pltpu.SEMAPHORE)

    def body(src_ref, l0_ref, l1_ref, l2_ref, send_sems, recv_sems, src_thru, o0_ref, o1_ref, o2_ref, token_ref):
        for cp in _scatter_copies(src_ref, (l0_ref, l1_ref, l2_ref), send_sems, recv_sems):
            cp.start()
        token_ref[...] = jnp.zeros_like(token_ref)

    land = [pltpu.with_memory_space_constraint(lax.empty(p16.shape[1:], BF16), pltpu.HBM) for _ in range(3)]
    return pl.pallas_call(
        body, name=name,
        out_shape=(pltpu.SemaphoreType.DMA((3,)), pltpu.SemaphoreType.DMA((3,)), pltpu.HBM(p16.shape, BF16),
                   *[pltpu.HBM(p16.shape[1:], BF16)] * 3, jax.ShapeDtypeStruct((8, BLK), F32)),
        in_specs=(hbm,) * 4, out_specs=(sem, sem, hbm, hbm, hbm, hbm, pl.BlockSpec(memory_space=pltpu.VMEM)),
        input_output_aliases={0: 2, 1: 3, 2: 4, 3: 5},
        compiler_params=pltpu.CompilerParams(has_side_effects=pltpu.SideEffectType.DATAFLOW_SIDE_EFFECTING),
    )(pltpu.with_memory_space_constraint(p16, pltpu.HBM), *land)


def scatter_wait(send_sems, recv_sems, src_thru, lands, after, name):
    hbm, sem = pl.BlockSpec(memory_space=pltpu.HBM), pl.BlockSpec(memory_space=pltpu.SEMAPHORE)

    def body(src_ref, l0_ref, l1_ref, l2_ref, send_sems, recv_sems, after_ref, src_dead, g0_ref, g1_ref, g2_ref):
        for cp in _scatter_copies(src_ref, (l0_ref, l1_ref, l2_ref), send_sems, recv_sems):
            cp.wait_send()
            cp.wait_recv()

    return pl.pallas_call(
        body, name=name, out_shape=(pltpu.HBM(src_thru.shape, BF16), *[pltpu.HBM(lands[0].shape, BF16)] * 3),
        in_specs=(hbm, hbm, hbm, hbm, sem, sem, pl.BlockSpec(memory_space=pl.ANY)), out_specs=(hbm,) * 4,
        input_output_aliases={0: 0, 1: 1, 2: 2, 3: 3},
        compiler_params=pltpu.CompilerParams(has_side_effects=pltpu.SideEffectType.DATAFLOW_SIDE_EFFECTING),
    )(src_thru, *lands, send_sems, recv_sems, after)[1:]


def pair_gather(halves, name):
    n = len(halves)

    def body(*refs):
        bufs = refs[n:2 * n]
        send_sems, recv_sems = refs[2 * n:]
        x, y, c = _place()
        copies = []
        for ti in range(n):
            rh = bufs[ti].shape[0] // 2
            rows = bufs[ti].at[pl.ds(c * rh, rh), :]
            copies.append(pltpu.make_async_remote_copy(src_ref=rows, dst_ref=rows, send_sem=send_sems.at[ti],
                                                       recv_sem=recv_sems.at[ti], device_id=(x, y, 1 - c), device_id_type=MESH))
        for cp in copies:
            cp.start()
        for ti, cp in enumerate(copies):
            cp.wait_send()
            rh = bufs[ti].shape[0] // 2
            theirs = bufs[ti].at[pl.ds((1 - c) * rh, rh), :]
            pltpu.make_async_remote_copy(src_ref=theirs, dst_ref=theirs, send_sem=send_sems.at[ti], recv_sem=recv_sems.at[ti],
                                         device_id=(x, y, 1 - c), device_id_type=MESH).wait_recv()

    return pl.pallas_call(
        body, name=name, in_specs=_any(n), out_specs=_any(n), input_output_aliases={i: i for i in range(n)},
        out_shape=[jax.ShapeDtypeStruct(a.shape, a.dtype) for a in halves],
        scratch_shapes=[pltpu.SemaphoreType.DMA((n,)), pltpu.SemaphoreType.DMA((n,))],
    )(*halves)


def reduce_small(dm_f1, dm_mix, dm_gate, dm_f2, loss_blk, name):
    def body(f1_ref, mix_ref, gate_ref, f2_ref, l_ref, tot_ref, rows_ref, fin_ref):
        rows_ref[...] = jnp.zeros_like(rows_ref)
        tot_ref[...] = jnp.zeros_like(tot_ref)
        mod_src = [(f1_ref, 0), (f1_ref, 1), (f1_ref, 2), (mix_ref, 0), (mix_ref, 1), (gate_ref, 2),
                   (f2_ref, 0), (f2_ref, 1), (f2_ref, 2)]
        norm_src = [(f1_ref, 3), (mix_ref, 3), (f2_ref, 3)]
        for l in range(2):
            for k, (ref, r) in enumerate(mod_src + norm_src):
                lat = ref[0, l, 0, r:r + 1, :]
                ctx = ref[0, l, 1, r:r + 1, :]
                for dev in range(N_DEV):
                    if dev:
                        lat = lat + ref[dev, l, 0, r:r + 1, :]
                        ctx = ctx + ref[dev, l, 1, r:r + 1, :]
                    if k < N_MOD:
                        rows_ref[l, dev, k:k + 1, :] = ref[dev, l, 0, r:r + 1, :]
                if k < N_MOD:
                    rows_ref[l, N_DEV, k:k + 1, :] = ctx
                tot_ref[l, k:k + 1, :] = lat + ctx
        acc = l_ref[0]
        for dev in range(1, N_DEV):
            acc = acc + l_ref[dev]
        loss = (0.5 / D) * jnp.sum(acc[1:2, :], axis=1, keepdims=True)
        row = lax.broadcasted_iota(jnp.int32, (8, D), 0)
        fin_ref[...] = jnp.where(row == 0, acc[0:1, :], loss)

    return pl.pallas_call(
        body, name=name,
        out_shape=[jax.ShapeDtypeStruct((2, 16, D), F32), jax.ShapeDtypeStruct((2, 16, 16, D), F32),
                   jax.ShapeDtypeStruct((8, D), F32)],
        compiler_params=_params(),
    )(dm_f1, dm_mix, dm_gate, dm_f2, loss_blk)


def rope_tables(t, s):
    rows = t // GRID_W
    row = jnp.repeat(jnp.arange(rows), GRID_W).astype(F32)
    col = jnp.tile(jnp.arange(GRID_W), rows).astype(F32)
    inv = ROPE_BASE ** (-jnp.arange(0, HEAD // 2, 2, dtype=F32) / (HEAD // 2))
    ang = jnp.concatenate([row[:, None] * inv, col[:, None] * inv], axis=-1)
    cos, sin = jnp.cos(ang), jnp.sin(ang)
    cos = jnp.concatenate([jnp.tile(cos, (1, 4)), jnp.ones((s - t, BLK), F32)], axis=0)
    sin = jnp.concatenate([jnp.tile(jnp.concatenate([-sin, sin], axis=1), (1, 2)), jnp.zeros((s - t, BLK), F32)], axis=0)
    return cos, sin


BIG = ("ffn1_in", "ffn1_out", "w_in", "w_out", "ffn2_in", "ffn2_out")
GROUPS = dict(ffn1=("ffn1_in", "ffn1_out"), mix=("w_in", "w_out"), ffn2=("ffn2_in", "ffn2_out"))
GATHER_BEHIND = {("ffn1", 0): [("w_in", 0), ("ffn2_out", 0), ("ffn1_out", 1)], ("proj", 0): [("w_out", 0)],
                 ("mix", 0): [("ffn2_in", 0)], ("ffn2", 0): [("ffn1_in", 1), ("w_in", 1)],
                 ("ffn1", 1): [("ffn2_in", 1), ("w_out", 1)], ("mix", 1): [("ffn2_out", 1)]}


def _slot_major(name, g):
    if name == "w_in":
        return jnp.stack(jnp.split(g, N_SLOT, axis=1), axis=0)
    if name in ("ffn1_in", "ffn2_in"):
        return g
    return g.reshape(N_SLOT, g.shape[0] // N_SLOT, g.shape[1])


def _whole_weight(name, buf):
    if name == "w_in":
        return buf.transpose(1, 0, 2).reshape(D, PROJ_W)
    if name in ("ffn1_in", "ffn2_in"):
        return buf
    return buf.reshape(-1, buf.shape[2])


def local_step(x1, ctx1, target, mods, norms, nfinal, placed, w_pool, pool_scale, sink, place, small_blocks):
    t, s = x1.shape[0], x1.shape[0] + ctx1.shape[0]
    n_lat = t // TM
    cos, sin = rope_tables(t, s)
    tables = mix_tables(t, s)
    wts ={name: list(pair) for name, pair in placed.items()}

    def gather(tensors):
        return gather_exchange([wts[name][l] for name, l in tensors])

    def gathered(tensors, arrays):
        for (name, l), whole in zip(tensors, arrays):
            wts[name][l] = whole

    def weight(name, l):
        return _whole_weight(name, wts[name][l])

    def fwd_ex(grp, l):
        groups = GATHER_BEHIND.get((grp, l))
        return (groups, gather(groups)) if groups else (None, None)

    first = [("ffn1_in", 0), ("ffn1_out", 0)]
    gathered(first, run_exchange(gather(first), "gather_first"))
    h = jnp.concatenate([x1, ctx1], axis=0)
    saved = []
    for l in range(2):
        h0 = h
        groups, ex = fwd_ex("ffn1", l)
        (h1, ab1, f1), got = ffn_fwd(h0, mods, norms[0], weight("ffn1_in", l), weight("ffn1_out", l), l, 0, n_lat, f"ffn1_fwd_{l}", ex)
        gathered(groups or [], got)
        groups, ex = fwd_ex("proj", l)
        (u, q, k, v), got = proj_fwd(h1, mods, norms[1], weight("w_in", l), cos, sin, l, n_lat, f"proj_fwd_{l}", ex)
        gathered(groups or [], got)
        groups, ex = fwd_ex("mix", l)
        (h2, cat, lse, mo), got = mix_fwd(h1, q, k, v, u, w_pool, pool_scale, sink, weight("w_out", l), mods, tables, l, t,
                                          f"mix_fwd_{l}", ex)
        gathered(groups or [], got)
        groups, ex = fwd_ex("ffn2", l)
        (h, ab2, f2), got = ffn_fwd(h2, mods, norms[2], weight("ffn2_in", l), weight("ffn2_out", l), l, 6, n_lat, f"ffn2_fwd_{l}", ex)
        gathered(groups or [], got)
        saved.append((h0, ab1, f1, h1, u, q, k, v, cat, lse, mo, h2, ab2, f2))
    dh, loss_blk = loss_head(h, target, nfinal, t, "loss_head")

    halves = {name: [None, None] for name in BIG}
    pending = []

    def summed_in_pair(grp, l, name_a, g_a, name_b, wgrad_b):
        g_b, got_a = wgrad_b(pair_exchange([_slot_major(name_a, g_a[1])]))
        sum_a, got_b = pair_sum(_slot_major(name_a, g_a[0]), got_a[0], place, f"pair_sum_{name_a}_{l}",
                                pair_exchange([_slot_major(name_b, g_b[1])]))
        sums = {name_a: sum_a, name_b: pair_sum(_slot_major(name_b, g_b[0]), got_b[0], place, f"pair_sum_{name_b}_{l}")}
        pending.append((grp, l, [sums[n] for n in GROUPS[grp]]))

    def scatter():
        return scatter_exchange([p16 for _, p16 in pending[0][2]]) if pending else None

    def scattered(got):
        if pending:
            grp, l, pairs = pending.pop(0)
            for i, name in enumerate(GROUPS[grp]):
                halves[name][l] = chip_sum(pairs[i][0], got[3 * i:3 * i + 3], place, f"chip_sum_{name}_{l}")

    small = [None, None]
    for l in (1, 0):
        h0, ab1, f1, h1, u, q, k, v, cat, lse, mo, h2, ab2, f2 = saved[l]
        (dh, dab, df, n, act, dm_f2), got = ffn_bwd(h2, ab2, f2, dh, mods, norms[2], weight("ffn2_in", l), weight("ffn2_out", l),
                                                    l, 6, n_lat, f"ffn2_bwd_{l}", scatter())
        scattered(got)
        g_in, _ = wgrad(n, dab, D, FF_COLS, FF_COLS, f"ffn2_in_wgrad_{l}")
        summed_in_pair("ffn2", l, "ffn2_in", g_in, "ffn2_out",
                       lambda ex, a=act, b=df: wgrad(a, b, D_FF // 2, D, None, f"ffn2_out_wgrad_{l}", ex))
        (dq, dk, dv, du, dmo, dwp, dps, dsink, dm_gate), got = mix_bwd(
            dh, mo, q, k, v, u, lse, w_pool, pool_scale, sink, weight("w_out", l), mods, tables, l, t, f"mix_bwd_{l}", scatter())
        scattered(got)
        g_wo, _ = wgrad(cat, dmo, POOL_W + ATTN_W, D, None, f"w_out_wgrad_{l}")
        dh, dp, n, dm_mix = proj_bwd(h1, du, dq, dk, dv, dh, mods, norms[1], weight("w_in", l), cos, sin, l, n_lat, f"proj_bwd_{l}")
        summed_in_pair("mix", l, "w_out", g_wo, "w_in",
                       lambda ex, a=n, b=dp: wgrad(a, b, D, PROJ_W // 2, None, f"w_in_wgrad_{l}", ex))
        (dh, dab, df, n, act, dm_f1), got = ffn_bwd(h0, ab1, f1, dh, mods, norms[0], weight("ffn1_in", l), weight("ffn1_out", l),
                                                    l, 0, n_lat, f"ffn1_bwd_{l}", scatter())
        scattered(got)
        small[l] = dict(dm_f1=dm_f1, dm_mix=dm_mix, dm_gate=dm_gate, dm_f2=dm_f2, dwp=dwp, dps=dps, dsink=dsink)
        if l:
            g_in, _ = wgrad(n, dab, D, FF_COLS, FF_COLS, f"ffn1_in_wgrad_{l}")
            summed_in_pair("ffn1", l, "ffn1_in", g_in, "ffn1_out",
                           lambda ex, a=act, b=df: wgrad(a, b, D_FF // 2, D, None, f"ffn1_out_wgrad_{l}", ex))
    g_out, small_all = wgrad(act, df, D_FF // 2, D, None, "ffn1_out_wgrad_0", gather8_exchange(small_blocks(small, loss_blk)))
    got = run_exchange(pair_exchange([_slot_major("ffn1_out", g_out[1])]), "pair_exchange_ffn1_out_0")
    p32, p16 = pair_sum(_slot_major("ffn1_out", g_out[0]), got[0], place, "pair_sum_ffn1_out_0")
    g_in, got = wgrad(n, dab, D, FF_COLS, FF_COLS, "ffn1_in_wgrad_0", scatter_exchange([p16]))
    halves["ffn1_out"][0] = chip_sum(p32, got, place, "chip_sum_ffn1_out_0")
    got = run_exchange(pair_exchange([_slot_major("ffn1_in", g_in[1])]), "pair_exchange_ffn1_in_0")
    return dh[:t], halves, pair_sum(_slot_major("ffn1_in", g_in[0]), got[0], place, "pair_sum_ffn1_in_0"), small_all


def _silu_grad(z):
    sg = jax.nn.sigmoid(z)
    return sg * (1 + z * (1 - sg))


def kernel(x, c, ctx, c_ctx, w_mod, b_mod, norm_ffn1, w_ffn1_in, w_ffn1_out, norm_mix, w_in, w_pool, pool_scale, sink, w_out, norm_ffn2, w_ffn2_in, w_ffn2_out, norm_final, loss_target, m_c_ctx, m_w_mod, m_b_mod, m_norm_ffn1, m_w_ffn1_in, m_w_ffn1_out, m_norm_mix, m_w_in, m_w_pool, m_pool_scale, m_sink, m_w_out, m_norm_ffn2, m_w_ffn2_in, m_w_ffn2_out, m_norm_final, v_c_ctx, v_w_mod, v_b_mod, v_norm_ffn1, v_w_ffn1_in, v_w_ffn1_out, v_norm_mix, v_w_in, v_w_pool, v_pool_scale, v_sink, v_w_out, v_norm_ffn2, v_w_ffn2_in, v_w_ffn2_out, v_norm_final):
    px, py, pc = _place()
    slot, me = 2 * px + py, 4 * px + 2 * py + pc
    n_grp = len(POOL_WINDOWS)

    (c_rows,) = all_gather([c.reshape(8, D // 8)], "gather_c")
    c_all = jnp.concatenate([c_rows.reshape(N_DEV, D), c_ctx.reshape(1, D), jnp.zeros((16 - N_DEV - 1, D), F32)], axis=0)
    b_cols = lax.dynamic_slice(b_mod, (0, slot * MOD_COLS), (2, MOD_COLS)).reshape(2, 1, MOD_COLS)
    (mod_parts,) = all_gather([mod_rows(c_all, w_mod, b_cols, "mod_rows")], "gather_mods")
    mods_all = mod_parts[0::2].transpose(1, 2, 0, 3).reshape(2, 16, N_MOD * D)
    mx = lax.dynamic_slice(mods_all, (0, me, 0), (2, 1, N_MOD * D)).reshape(2, N_MOD, D)
    mc = mods_all[:, N_DEV].reshape(2, N_MOD, D)
    pad = jnp.zeros((2, 16 - N_MOD, D), F32)
    mods = jnp.stack([jnp.concatenate([mx, pad], axis=1), jnp.concatenate([mc, pad], axis=1)], axis=1)

    place = jnp.stack([pc, slot]).astype(jnp.int32)
    shards = dict(ffn1_in=w_ffn1_in, ffn1_out=w_ffn1_out, w_in=w_in, w_out=w_out, ffn2_in=w_ffn2_in, ffn2_out=w_ffn2_out)
    placed = {name: [cast_place(shards[name], l, place, f"cast_{name}_{l}") for l in range(2)] for name in BIG}
    norms = [g.reshape(2, 1, D) for g in (norm_ffn1, norm_mix, norm_ffn2)]
    row_sums = ("dm_f1", "dm_mix", "dm_gate", "dm_f2")

    def small_blocks(small, loss_blk):
        stacked = {k: jnp.stack([small[0][k], small[1][k]]) for k in row_sums + ("dwp", "dps", "dsink")}
        return ([stacked[k].reshape(32, D) for k in row_sums]
                + [stacked["dwp"].reshape(2 * n_grp * GROUP, GROUP), stacked["dps"].reshape(16, POOL_W),
                   stacked["dsink"].reshape(16, BLK), loss_blk])

    dx, halves, last_pair, small_all = local_step(x[0], ctx[0], loss_target[0], mods, norms, norm_final.reshape(1, D), placed,
                                                   w_pool.astype(BF16), pool_scale.reshape(2, 1, POOL_W), sink, place, small_blocks)
    grads = {}
    send_sems, recv_sems, src_thru, *lands, token = scatter_start(last_pair[1], "scatter_last_start")

    *g_dm, g_dwp, g_dps, g_dsink, g_loss = small_all
    tot, rows, fin = reduce_small(*[g.reshape(N_DEV, 2, 2, 8, D) for g in g_dm], g_loss, "reduce_small")
    s_dwp, s_dps, s_dsink = sum8([g_dwp, g_dps, g_dsink], "sum_pool_sink")
    grads.update(
        w_pool=s_dwp.reshape(2, n_grp, GROUP, GROUP), pool_scale=s_dps.reshape(2, 8, POOL_W)[:, 0],
        sink=s_dsink.reshape(2, 8, BLK)[:, 0, :N_HEADS], b_mod=tot[:, :N_MOD].reshape(2, N_MOD * D),
        norm_ffn1=tot[:, N_MOD], norm_mix=tot[:, N_MOD + 1], norm_ffn2=tot[:, N_MOD + 2], norm_final=fin[0])
    loss = fin[1, 0]

    dmod_cols = lax.dynamic_slice(rows[:, :, :N_MOD, :].reshape(2, 16, N_MOD * D), (0, 0, slot * MOD_COLS), (2, 16, MOD_COLS))
    grads["w_mod"], dc = mod_grads(c_all, dmod_cols + token[0, 0], w_mod, "mod_grads")
    (g_dc,) = all_gather([dc], "gather_dc")
    (s_dc,) = sum8([g_dc], "sum_dc")
    (d_c_ctx,) = elementwise(lambda d, z: (0.5 * d * _silu_grad(z),), [s_dc[N_DEV:N_DEV + 1], c_ctx.reshape(1, D)], [F32], "c_ctx_grad")
    grads["c_ctx"] = d_c_ctx.reshape(D)

    given = dict(c_ctx=(c_ctx, m_c_ctx, v_c_ctx), w_mod=(w_mod, m_w_mod, v_w_mod), b_mod=(b_mod, m_b_mod, v_b_mod),
                 norm_ffn1=(norm_ffn1, m_norm_ffn1, v_norm_ffn1), w_ffn1_in=(w_ffn1_in, m_w_ffn1_in, v_w_ffn1_in),
                 w_ffn1_out=(w_ffn1_out, m_w_ffn1_out, v_w_ffn1_out), norm_mix=(norm_mix, m_norm_mix, v_norm_mix),
                 w_in=(w_in, m_w_in, v_w_in), w_pool=(w_pool, m_w_pool, v_w_pool),
                 pool_scale=(pool_scale, m_pool_scale, v_pool_scale), sink=(sink, m_sink, v_sink), w_out=(w_out, m_w_out, v_w_out),
                 norm_ffn2=(norm_ffn2, m_norm_ffn2, v_norm_ffn2), w_ffn2_in=(w_ffn2_in, m_w_ffn2_in, v_w_ffn2_in),
                 w_ffn2_out=(w_ffn2_out, m_w_ffn2_out, v_w_ffn2_out), norm_final=(norm_final, m_norm_final, v_norm_final))
    ready = [(name, l) for name in BIG for l in range(2) if halves[name][l] is not None]
    shard = dict(zip(ready, pair_gather([halves[name][l] for name, l in ready], "grad_pair_gather")))

    def update(name):
        w, m, v = given[name]
        if name in BIG or name[2:] in BIG:
            key = name if name in BIG else name[2:]
            return adamw_layers(w, shard[key, 0], shard[key, 1], m, v, f"adamw_{name}")
        return [grads[name], *adamw(w, grads[name], m, v, f"adamw_{name}")]

    done = {name: update(name) for name in given if name != "w_ffn1_in"}
    got = scatter_wait(send_sems, recv_sems, src_thru, lands, done["w_ffn2_out"][3], "scatter_last_wait")
    (shard["ffn1_in", 0],) = pair_gather([chip_sum(last_pair[0], got, place, "chip_sum_ffn1_in_0")], "grad_pair_gather_last")
    done["w_ffn1_in"] = update("w_ffn1_in")
    return (loss, dx[None], *[done[name][i] for i in range(4) for name in given])
```

```python
import functools

import jax
import jax.numpy as jnp
from jax import lax
from jax.experimental import pallas as pl
from jax.experimental.pallas import tpu as pltpu

F32, BF16 = jnp.float32, jnp.bfloat16
D = 1024
D_FF = 2816
N_SLOT = 4
FF_COLS = 2 * D_FF // N_SLOT
N_MOD = 9
MOD_COLS = N_MOD * D // N_SLOT
POOL_W, ATTN_W, KV_W = 512, 512, 128
PROJ_W = POOL_W + ATTN_W + 2 * KV_W
N_HEADS, Q_GROUP, HEAD = 8, 4, 64
GROUP = 128
POOL_WINDOWS = (2, 4, 8, 16)
BLK = 128
QB = 256
WIN = QB + 2 * BLK
GRID_W = 64
ROPE_BASE = 10000.0
EPS = 1e-6
NEG_INF = -1e30
TM = 256
N_DEV = 8
VMEM_LIMIT_BYTES = 56 * 1024 * 1024
ADAM_LR, ADAM_B1, ADAM_B2, ADAM_EPS, ADAM_WD, ADAM_STEP = 0.001, 0.9, 0.999, 1e-08, 0.01, 10
MESH = pl.DeviceIdType.MESH
NT = (((1,), (1,)), ((), ()))
TN = (((0,), (0,)), ((), ()))


def _params(*sem):
    return pltpu.CompilerParams(dimension_semantics=sem, vmem_limit_bytes=VMEM_LIMIT_BYTES)


def _whole(shape, lead=()):
    idx = tuple(lead) + (0,) * len(shape)
    return pl.BlockSpec((None,) * len(lead) + tuple(shape), lambda *_: idx, pipeline_mode=pl.Buffered(1))


def _rows(cols, tm=TM):
    return pl.BlockSpec((tm, cols), lambda i: (i, 0))


def _mods_spec(layer, n_lat):
    return pl.BlockSpec((None, None, 16, D), lambda i: (layer, (i >= n_lat).astype(jnp.int32), 0, 0))


def _acc_spec(n_lat):
    return pl.BlockSpec((None, 8, D), lambda i: ((i >= n_lat).astype(jnp.int32), 0, 0))


def _dot(a, b):
    return jnp.dot(a, b, preferred_element_type=F32)


def _dotg(a, b, dims):
    return lax.dot_general(a, b, dims, preferred_element_type=F32)


def _sum0(v):
    return jnp.sum(v, axis=0, keepdims=True)


def _norm_mod(h, g, shift, scale):
    r = lax.rsqrt(jnp.mean(h * h, axis=-1, keepdims=True) + EPS)
    xhat = h * r
    y = xhat * g
    return y * (1 + scale) + shift, xhat, r, y


def _norm_mod_bwd(dn, xhat, r, y, g, scale):
    dy = dn * (1 + scale)
    dx = dy * g
    dh = r * (dx - xhat * jnp.mean(dx * xhat, axis=-1, keepdims=True))
    return _sum0(dn), _sum0(dn * y), _sum0(dy * xhat), dh


def _swap_halves(v):
    w = v.shape[1]
    lane = lax.broadcasted_iota(jnp.int32, v.shape, 1)
    return jnp.where(lane % HEAD < HEAD // 2, pltpu.roll(v, w - HEAD // 2, axis=1), pltpu.roll(v, HEAD // 2, axis=1))


def _tile_lanes(t, width):
    return t if width == t.shape[1] else jnp.concatenate([t] * (width // t.shape[1]), axis=1)


def _rope(v, cos, sin):
    return v * _tile_lanes(cos, v.shape[1]) + _swap_halves(v) * _tile_lanes(sin, v.shape[1])


def _unrope(g, cos, sin):
    return g * _tile_lanes(cos, g.shape[1]) + _swap_halves(g * _tile_lanes(sin, g.shape[1]))


def ffn_fwd(h, mods, g, w4, wo, layer, k0, n_lat, name, ex=None):
    s = h.shape[0]

    def body(h_ref, m_ref, g_ref, w_ref, wo_ref, ho_ref, ab_ref, f_ref):
        hh = h_ref[...]
        n, _, _, _ = _norm_mod(hh, g_ref[...], m_ref[k0:k0 + 1, :], m_ref[k0 + 1:k0 + 2, :])
        nb = n.astype(BF16)
        acc = jnp.zeros((TM, D), F32)
        for j in range(2):
            a = _dot(nb, w_ref[j])
            b = _dot(nb, w_ref[2 + j])
            ab_ref[:, j * FF_COLS:(j + 1) * FF_COLS] = a.astype(BF16)
            ab_ref[:, (2 + j) * FF_COLS:(3 + j) * FF_COLS] = b.astype(BF16)
            act = (a * jax.nn.sigmoid(a) * b).astype(BF16)
            acc = acc + _dot(act, wo_ref[j * FF_COLS:(j + 1) * FF_COLS, :])
        f_ref[...] = acc
        ho_ref[...] = hh + 0.5 * m_ref[k0 + 2:k0 + 3, :] * acc

    return _grid_call(
        body, name, s // TM,
        [_rows(D), _mods_spec(layer, n_lat), _whole((1, D), (layer,)), _whole((N_SLOT, D, FF_COLS)), _whole((D_FF, D))],
        [_rows(D), _rows(2 * D_FF), _rows(D)],
        [jax.ShapeDtypeStruct((s, D), F32), jax.ShapeDtypeStruct((s, 2 * D_FF), BF16), jax.ShapeDtypeStruct((s, D), F32)],
        (h, mods, g, w4, wo), "parallel", ex)


def ffn_bwd(h, ab, f, dh, mods, g, w4, wo, layer, k0, n_lat, name, ex=None):
    s = h.shape[0]

    def body(h_ref, ab_ref, f_ref, dh_ref, m_ref, g_ref, w_ref, wo_ref, dhi_ref, dab_ref, df_ref, n_ref, act_ref, dm_ref):
        i = pl.program_id(0)

        @pl.when((i == 0) | (i == n_lat))
        def _():
            dm_ref[...] = jnp.zeros_like(dm_ref)

        hh, dho, gg = h_ref[...], dh_ref[...], g_ref[...]
        scale, gate = m_ref[k0 + 1:k0 + 2, :], m_ref[k0 + 2:k0 + 3, :]
        n, xhat, r, y = _norm_mod(hh, gg, m_ref[k0:k0 + 1, :], scale)
        n_ref[...] = n.astype(BF16)
        dgate = _sum0(dho * (0.5 * f_ref[...]))
        dfb = ((0.5 * gate) * dho).astype(BF16)
        df_ref[...] = dfb
        dn = jnp.zeros((TM, D), F32)
        for j in range(2):
            a = ab_ref[:, j * FF_COLS:(j + 1) * FF_COLS].astype(F32)
            b = ab_ref[:, (2 + j) * FF_COLS:(3 + j) * FF_COLS].astype(F32)
            sg = jax.nn.sigmoid(a)
            sa = a * sg
            act_ref[:, j * FF_COLS:(j + 1) * FF_COLS] = (sa * b).astype(BF16)
            dact = _dotg(dfb, wo_ref[j * FF_COLS:(j + 1) * FF_COLS, :], NT)
            da = (dact * b * (sg * (1 + a * (1 - sg)))).astype(BF16)
            db = (dact * sa).astype(BF16)
            dab_ref[:, j * FF_COLS:(j + 1) * FF_COLS] = da
            dab_ref[:, (2 + j) * FF_COLS:(3 + j) * FF_COLS] = db
            dn = dn + _dotg(da, w_ref[j], NT) + _dotg(db, w_ref[2 + j], NT)
        dsh, dsc, dg, dhn = _norm_mod_bwd(dn, xhat, r, y, gg, scale)
        dhi_ref[...] = dho + dhn
        dm_ref[0:1, :] += dsh
        dm_ref[1:2, :] += dsc
        dm_ref[2:3, :] += dgate
        dm_ref[3:4, :] += dg

    return _grid_call(
        body, name, s // TM,
        [_rows(D), _rows(2 * D_FF), _rows(D), _rows(D), _mods_spec(layer, n_lat), _whole((1, D), (layer,)),
         _whole((N_SLOT, D, FF_COLS)), _whole((D_FF, D))],
        [_rows(D), _rows(2 * D_FF), _rows(D), _rows(D), _rows(D_FF), _acc_spec(n_lat)],
        [jax.ShapeDtypeStruct((s, D), F32), jax.ShapeDtypeStruct((s, 2 * D_FF), BF16), jax.ShapeDtypeStruct((s, D), BF16),
         jax.ShapeDtypeStruct((s, D), BF16), jax.ShapeDtypeStruct((s, D_FF), BF16), jax.ShapeDtypeStruct((2, 8, D), F32)],
        (h, ab, f, dh, mods, g, w4, wo), "arbitrary", ex)


def _token_tile(s, limit=2176):
    return max(ts for ts in range(16, limit + 1, 16) if s % ts == 0)


def wgrad(a, b, tk, tn, slot_cols, name, ex=None):
    s, k = a.shape
    n = b.shape[1]
    ts = _token_tile(s)
    steps = s // ts

    def body(a_ref, b_ref, o_ref, o16_ref):
        r = _dotg(a_ref[...], b_ref[...], TN)
        si = pl.program_id(2)

        @pl.when(si == 0)
        def _():
            o_ref[...] = r

        @pl.when(si > 0)
        def _():
            o_ref[...] += r

        @pl.when(si == steps - 1)
        def _():
            o16_ref[...] = o_ref[...].astype(BF16)

    if slot_cols is None:
        shape, spec = (k, n), pl.BlockSpec((tk, tn), lambda i, j, si: (i, j))
    else:
        per = slot_cols // tn
        shape, spec = (n // slot_cols, k, slot_cols), pl.BlockSpec((None, tk, tn), lambda i, j, si: (lax.div(j, per), i, lax.rem(j, per)))
    return _grid_call(
        body, name, (k // tk, n // tn, steps),
        [pl.BlockSpec((ts, tk), lambda i, j, si: (si, i)), pl.BlockSpec((ts, tn), lambda i, j, si: (si, j))], [spec, spec],
        [jax.ShapeDtypeStruct(shape, F32), jax.ShapeDtypeStruct(shape, BF16)], (a, b), ("parallel", "parallel", "arbitrary"), ex)


def proj_fwd(h, mods, g, w_in, cos, sin, layer, n_lat, name, ex=None):
    s = h.shape[0]

    def body(h_ref, m_ref, g_ref, w_ref, cos_ref, sin_ref, u_ref, q_ref, k_ref, v_ref):
        n, _, _, _ = _norm_mod(h_ref[...], g_ref[...], m_ref[3:4, :], m_ref[4:5, :])
        p = _dot(n.astype(BF16), w_ref[...])
        cs, sn = cos_ref[...], sin_ref[...]
        u_ref[...] = p[:, :POOL_W]
        q_ref[...] = (_rope(p[:, POOL_W:POOL_W + ATTN_W], cs, sn) * HEAD ** -0.5).astype(BF16)
        k_ref[...] = _rope(p[:, POOL_W + ATTN_W:POOL_W + ATTN_W + KV_W], cs, sn).astype(BF16)
        v_ref[...] = p[:, POOL_W + ATTN_W + KV_W:].astype(BF16)

    return _grid_call(
        body, name, s // TM,
        [_rows(D), _mods_spec(layer, n_lat), _whole((1, D), (layer,)), _whole((D, PROJ_W)), _rows(BLK), _rows(BLK)],
        [_rows(POOL_W), _rows(ATTN_W), _rows(KV_W), _rows(KV_W)],
        [jax.ShapeDtypeStruct((s, POOL_W), F32), jax.ShapeDtypeStruct((s, ATTN_W), BF16),
         jax.ShapeDtypeStruct((s, KV_W), BF16), jax.ShapeDtypeStruct((s, KV_W), BF16)],
        (h, mods, g, w_in, cos, sin), "parallel", ex)


def proj_bwd(h, du, dq, dk, dv, dh, mods, g, w_in, cos, sin, layer, n_lat, name):
    s = h.shape[0]

    def body(h_ref, du_ref, dq_ref, dk_ref, dv_ref, dh_ref, m_ref, g_ref, w_ref, cos_ref, sin_ref,
             dhi_ref, dp_ref, n_ref, dm_ref):
        i = pl.program_id(0)

        @pl.when((i == 0) | (i == n_lat))
        def _():
            dm_ref[...] = jnp.zeros_like(dm_ref)

        gg, scale = g_ref[...], m_ref[4:5, :]
        n, xhat, r, y = _norm_mod(h_ref[...], gg, m_ref[3:4, :], scale)
        n_ref[...] = n.astype(BF16)
        cs, sn = cos_ref[...], sin_ref[...]
        dp = jnp.concatenate([du_ref[...], _unrope(dq_ref[...], cs, sn) * HEAD ** -0.5, _unrope(dk_ref[...], cs, sn),
                              dv_ref[...]], axis=1).astype(BF16)
        dp_ref[...] = dp
        dsh, dsc, dg, dhn = _norm_mod_bwd(_dotg(dp, w_ref[...], NT), xhat, r, y, gg, scale)
        dhi_ref[...] = dh_ref[...] + dhn
        dm_ref[0:1, :] += dsh
        dm_ref[1:2, :] += dsc
        dm_ref[3:4, :] += dg

    return pl.pallas_call(
        body, name=name, grid=(s // TM,),
        in_specs=[_rows(D), _rows(POOL_W), _rows(ATTN_W), _rows(KV_W), _rows(KV_W), _rows(D), _mods_spec(layer, n_lat),
                  _whole((1, D), (layer,)), _whole((D, PROJ_W)), _rows(BLK), _rows(BLK)],
        out_specs=[_rows(D), _rows(PROJ_W), _rows(D), _acc_spec(n_lat)],
        out_shape=[jax.ShapeDtypeStruct((s, D), F32), jax.ShapeDtypeStruct((s, PROJ_W), BF16),
                   jax.ShapeDtypeStruct((s, D), BF16), jax.ShapeDtypeStruct((2, 8, D), F32)],
        compiler_params=_params("arbitrary"),
    )(h, du, dq, dk, dv, dh, mods, g, w_in, cos, sin)


def _window(i, s):
    return pl.multiple_of(jnp.clip(i * QB - BLK, 0, s - WIN), BLK)


def mix_tables(t, s):
    n_lat = t // QB
    blocks = jnp.array([0, 1, n_lat - 1] + list(range(n_lat, s // QB)))[:, None, None]
    ws = jnp.clip(blocks * QB - BLK, 0, s - WIN)
    q = blocks * QB + jnp.arange(QB)[None, :, None]
    k = ws + jnp.arange(WIN)[None, None, :]
    is_lat = blocks < n_lat
    local = jnp.where(is_lat & (k < t) & (jnp.abs(k - q) <= BLK), 0.0, NEG_INF).astype(F32)
    bias = jnp.concatenate([local, jnp.zeros(local.shape[:2] + (s - t,), F32)], axis=2)
    seq_lo, seq_hi = jnp.where(is_lat, 0, t), jnp.where(is_lat, t, s)
    bands, counts = [], []
    for w in POOL_WINDOWS:
        lo, hi = jnp.maximum(q - w // 2, seq_lo), jnp.minimum(q + w - w // 2, seq_hi)
        bands.append((k >= lo) & (k < hi))
        counts.append((hi - lo).astype(F32))
    band = jnp.stack(bands, axis=1).astype(BF16)
    count = jnp.concatenate(counts + [jnp.ones(counts[0].shape[:2] + (BLK - len(counts),), F32)], axis=2)
    return dict(bias=bias, band=band, band_t=band.transpose(0, 1, 3, 2), count=count)


def _case_spec(table, n_lat_blk):
    def kind(i):
        return jnp.where(i < n_lat_blk - 1, jnp.minimum(i, 1), i - n_lat_blk + 3)

    shape = table.shape[1:]
    return pl.BlockSpec((None,) + shape, lambda i: (kind(i),) + (0,) * len(shape))


def _split_dot(band, v):
    return _dot(band, v.astype(BF16))


def _pooled(u_ref, band_ref, cnt_ref, i, ws, gi):
    cols = slice(gi * GROUP, (gi + 1) * GROUP)
    mean = _split_dot(band_ref[gi], u_ref[pl.ds(ws, WIN), cols]) / cnt_ref[:, gi:gi + 1]
    return mean - u_ref[pl.ds(pl.multiple_of(i * QB, QB), QB), cols]


def _head_cols(hd):
    return slice(hd * HEAD, (hd + 1) * HEAD)


def _stack_heads(x, hk, first=0):
    return jnp.concatenate([x[:, first + (Q_GROUP * hk + g) * HEAD:first + (Q_GROUP * hk + g + 1) * HEAD]
                            for g in range(Q_GROUP)], axis=0)


def _biased(scores, bias):
    return (scores.reshape(Q_GROUP, QB, -1) + bias).reshape(Q_GROUP * QB, -1)


def _group_column(vals):
    row = lax.broadcasted_iota(jnp.int32, (Q_GROUP * QB, 1), 0)
    out = jnp.full((Q_GROUP * QB, 1), vals[Q_GROUP - 1], F32)
    for g in range(Q_GROUP - 2, -1, -1):
        out = jnp.where(row < (g + 1) * QB, vals[g], out)
    return out


def _lane_place(cols, width=BLK):
    lane = lax.broadcasted_iota(jnp.int32, (cols[0].shape[0], width), 1)
    out = jnp.zeros((cols[0].shape[0], width), F32)
    for hd, c in enumerate(cols):
        out = jnp.where(lane == hd, c, out)
    return out


def mix_fwd(h, q, k, v, u, w_pool, pool_scale, sink, w_out, mods, tables, layer, t, name, ex=None):
    s = h.shape[0]
    n_lat_blk = t // QB

    def body(h_ref, q_ref, k_ref, v_ref, u_ref, wp_ref, ps_ref, sink_ref, wo_ref, m_ref, bias_ref, band_ref, cnt_ref,
             ho_ref, cat_ref, lse_ref, mo_ref):
        i = pl.program_id(0)
        ws = _window(i, s)
        for gi in range(len(POOL_WINDOWS)):
            mixed = _dot(_pooled(u_ref, band_ref, cnt_ref, i, ws, gi).astype(BF16), wp_ref[gi])
            cat_ref[:, gi * GROUP:(gi + 1) * GROUP] = (mixed * ps_ref[:, gi * GROUP:(gi + 1) * GROUP]).astype(BF16)
        bias = bias_ref[...]
        k_all = jnp.concatenate([k_ref[pl.ds(ws, WIN), :], k_ref[t:s, :]], axis=0)
        v_all = jnp.concatenate([v_ref[pl.ds(ws, WIN), :], v_ref[t:s, :]], axis=0)
        lses = []
        for hk in range(N_HEADS // Q_GROUP):
            kv = _head_cols(hk)
            sc = _biased(_dotg(_stack_heads(q_ref[...], hk), k_all[:, kv], NT), bias)
            sk = _group_column([sink_ref[layer, Q_GROUP * hk + g] for g in range(Q_GROUP)])
            m = jnp.maximum(jnp.max(sc, axis=1, keepdims=True), sk)
            e = jnp.exp(sc - m)
            l = jnp.sum(e, axis=1, keepdims=True) + jnp.exp(sk - m)
            o = _dot(e.astype(BF16), v_all[:, kv]) * (1.0 / l)
            lse = m + jnp.log(l)
            for g in range(Q_GROUP):
                hd = Q_GROUP * hk + g
                cat_ref[:, POOL_W + hd * HEAD:POOL_W + (hd + 1) * HEAD] = o[g * QB:(g + 1) * QB].astype(BF16)
                lses.append(lse[g * QB:(g + 1) * QB])
        lse_ref[...] = _lane_place(lses)
        mo = _dot(cat_ref[...], wo_ref[...])
        mo_ref[...] = mo
        ho_ref[...] = h_ref[...] + m_ref[5:6, :] * mo

    blk = lambda cols: _rows(cols, QB)
    return _grid_call(
        body, name, s // QB,
        [blk(D), blk(ATTN_W), _whole((s, KV_W)), _whole((s, KV_W)), _whole((s, POOL_W)),
         _whole((len(POOL_WINDOWS), GROUP, GROUP), (layer,)), _whole((1, POOL_W), (layer,)),
         pl.BlockSpec(memory_space=pltpu.SMEM), _whole((POOL_W + ATTN_W, D)), _mods_spec(layer, n_lat_blk),
         _case_spec(tables["bias"], n_lat_blk), _case_spec(tables["band"], n_lat_blk), _case_spec(tables["count"], n_lat_blk)],
        [blk(D), blk(POOL_W + ATTN_W), blk(BLK), blk(D)],
        [jax.ShapeDtypeStruct((s, D), F32), jax.ShapeDtypeStruct((s, POOL_W + ATTN_W), BF16), jax.ShapeDtypeStruct((s, BLK), F32),
         jax.ShapeDtypeStruct((s, D), F32)],
        (h, q, k, v, u, w_pool, pool_scale, sink, w_out, mods, tables["bias"], tables["band"], tables["count"]), "parallel", ex)


def mix_bwd(dh, mo, q, k, v, u, lse, w_pool, pool_scale, sink, w_out, mods, tables, layer, t, name, ex=None):
    s = dh.shape[0]
    n_lat_blk = t // QB
    n_grp = len(POOL_WINDOWS)

    def body(dh_ref, mo_ref, q_ref, k_ref, v_ref, u_ref, lse_ref, wp_ref, ps_ref, sink_ref, wo_ref, m_ref,
             bias_ref, band_ref, band_t_ref, cnt_ref,
             dq_ref, dk_ref, dv_ref, du_ref, dmo_ref, dwp_ref, dps_ref, dsink_ref, dm_ref):
        i = pl.program_id(0)

        @pl.when(i == 0)
        def _():
            for ref in (dk_ref, dv_ref, du_ref, dwp_ref, dps_ref, dsink_ref):
                ref[...] = jnp.zeros_like(ref)

        @pl.when((i == 0) | (i == n_lat_blk))
        def _():
            dm_ref[...] = jnp.zeros_like(dm_ref)

        ws = _window(i, s)
        here = pl.ds(pl.multiple_of(i * QB, QB), QB)
        dho = dh_ref[...]
        dm_ref[2:3, :] += _sum0(dho * mo_ref[...])
        dmo = (m_ref[5:6, :] * dho).astype(BF16)
        dmo_ref[...] = dmo
        dcat = _dotg(dmo, wo_ref[...], NT)

        for gi in range(n_grp):
            cols = slice(gi * GROUP, (gi + 1) * GROUP)
            pooled = _pooled(u_ref, band_ref, cnt_ref, i, ws, gi).astype(BF16)
            dpo = dcat[:, cols]
            dps_ref[0:1, cols] += _sum0(dpo * _dot(pooled, wp_ref[gi]))
            dmixed = (dpo * ps_ref[:, cols]).astype(BF16)
            dwp_ref[gi] += _dotg(pooled, dmixed, TN)
            dpooled = _dotg(dmixed, wp_ref[gi], NT)
            du_ref[pl.ds(ws, WIN), cols] += _split_dot(band_t_ref[gi], dpooled / cnt_ref[:, gi:gi + 1])
            du_ref[here, cols] -= dpooled

        bias = bias_ref[...]
        k_all = jnp.concatenate([k_ref[pl.ds(ws, WIN), :], k_ref[t:s, :]], axis=0)
        v_all = jnp.concatenate([v_ref[pl.ds(ws, WIN), :], v_ref[t:s, :]], axis=0)
        qq, lse_all = q_ref[...], lse_ref[...]
        dqs, dsinks, dks, dvs = [], [], [], []
        for hk in range(N_HEADS // Q_GROUP):
            kv = _head_cols(hk)
            q4 = _stack_heads(qq, hk)
            lse = jnp.concatenate([lse_all[:, Q_GROUP * hk + g:Q_GROUP * hk + g + 1] for g in range(Q_GROUP)], axis=0)
            p = jnp.exp(_biased(_dotg(q4, k_all[:, kv], NT), bias) - lse)
            do = _stack_heads(dcat, hk, POOL_W).astype(BF16)
            dp = _dotg(do, v_all[:, kv], NT)
            delta = jnp.sum(p * dp, axis=1, keepdims=True)
            ds = (p * (dp - delta)).astype(BF16)
            sk = _group_column([sink_ref[layer, Q_GROUP * hk + g] for g in range(Q_GROUP)])
            dsk = -jnp.exp(sk - lse) * delta
            dq = _dot(ds, k_all[:, kv])
            for g in range(Q_GROUP):
                dqs.append(dq[g * QB:(g + 1) * QB])
                dsinks.append(_sum0(dsk[g * QB:(g + 1) * QB]))
            dks.append(_dotg(ds, q4, TN))
            dvs.append(_dotg(p.astype(BF16), do, TN))
        dq_ref[...] = jnp.concatenate(dqs, axis=1)
        dk, dv = jnp.concatenate(dks, axis=1), jnp.concatenate(dvs, axis=1)
        dk_ref[pl.ds(ws, WIN), :] += dk[:WIN]
        dv_ref[pl.ds(ws, WIN), :] += dv[:WIN]
        dk_ref[t:s, :] += dk[WIN:]
        dv_ref[t:s, :] += dv[WIN:]
        dsink_ref[0:1, :] += _lane_place(dsinks)

    blk = lambda cols: _rows(cols, QB)
    full = lambda shape: pl.BlockSpec(shape, lambda i: (0,) * len(shape))
    return _grid_call(
        body, name, s // QB,
        [blk(D), blk(D), blk(ATTN_W), _whole((s, KV_W)), _whole((s, KV_W)), _whole((s, POOL_W)),
         blk(BLK), _whole((n_grp, GROUP, GROUP), (layer,)), _whole((1, POOL_W), (layer,)),
         pl.BlockSpec(memory_space=pltpu.SMEM), _whole((POOL_W + ATTN_W, D)), _mods_spec(layer, n_lat_blk)]
        + [_case_spec(tables[key], n_lat_blk) for key in ("bias", "band", "band_t", "count")],
        [blk(ATTN_W), full((s, KV_W)), full((s, KV_W)), full((s, POOL_W)), blk(D),
         full((n_grp, GROUP, GROUP)), full((8, POOL_W)), full((8, BLK)), _acc_spec(n_lat_blk)],
        [jax.ShapeDtypeStruct((s, ATTN_W), F32), jax.ShapeDtypeStruct((s, KV_W), F32),
         jax.ShapeDtypeStruct((s, KV_W), F32), jax.ShapeDtypeStruct((s, POOL_W), F32),
         jax.ShapeDtypeStruct((s, D), BF16), jax.ShapeDtypeStruct((n_grp, GROUP, GROUP), F32),
         jax.ShapeDtypeStruct((8, POOL_W), F32), jax.ShapeDtypeStruct((8, BLK), F32), jax.ShapeDtypeStruct((2, 8, D), F32)],
        (dh, mo, q, k, v, u, lse, w_pool, pool_scale, sink, w_out, mods, tables["bias"], tables["band"], tables["band_t"],
         tables["count"]), "arbitrary", ex)


def loss_head(h, target, g, t, name):
    s = h.shape[0]
    n_lat = t // TM

    def body(h_ref, t_ref, g_ref, dh_ref, acc_ref):
        i = pl.program_id(0)

        @pl.when(i == 0)
        def _():
            acc_ref[...] = jnp.zeros_like(acc_ref)

        @pl.when(i < n_lat)
        def _():
            hh, gg = h_ref[...], g_ref[...]
            r = lax.rsqrt(jnp.mean(hh * hh, axis=-1, keepdims=True) + EPS)
            xhat = hh * r
            err = xhat * gg - t_ref[...]
            dy = err * (1.0 / D)
            dx = dy * gg
            dh_ref[...] = r * (dx - xhat * jnp.mean(dx * xhat, axis=-1, keepdims=True))
            acc_ref[0:1, :] += _sum0(dy * xhat)
            acc_ref[1:2, :] += _sum0(err * err)

        @pl.when(i >= n_lat)
        def _():
            dh_ref[...] = jnp.zeros_like(dh_ref)

    return pl.pallas_call(
        body, name=name, grid=(s // TM,),
        in_specs=[_rows(D), pl.BlockSpec((TM, D), lambda i: (jnp.minimum(i, n_lat - 1), 0)), _whole((1, D))],
        out_specs=[_rows(D), pl.BlockSpec((8, D), lambda i: (0, 0))],
        out_shape=[jax.ShapeDtypeStruct((s, D), F32), jax.ShapeDtypeStruct((8, D), F32)],
        compiler_params=_params("arbitrary"),
    )(h, target, g)


def mod_rows(c_all, w_mod, b_cols, name):
    def body(c_ref, w_ref, b_ref, o_ref):
        cc = c_ref[...]
        o_ref[...] = _dot((cc * jax.nn.sigmoid(cc)).astype(BF16), w_ref[...].astype(BF16)) + b_ref[...]

    return pl.pallas_call(
        body, name=name, grid=(2,),
        in_specs=[pl.BlockSpec((16, D), lambda l: (0, 0)), pl.BlockSpec((None, D, MOD_COLS), lambda l: (l, 0, 0)),
                  pl.BlockSpec((None, 1, MOD_COLS), lambda l: (l, 0, 0))],
        out_specs=pl.BlockSpec((None, 16, MOD_COLS), lambda l: (l, 0, 0)),
        out_shape=jax.ShapeDtypeStruct((2, 16, MOD_COLS), F32),
        compiler_params=_params("parallel"),
    )(c_all, w_mod, b_cols)


def mod_grads(c_all, dmod_cols, w_mod, name):
    def body(c_ref, d_ref, w_ref, dw_ref, dc_ref):
        @pl.when(pl.program_id(0) == 0)
        def _():
            dc_ref[...] = jnp.zeros_like(dc_ref)

        cc = c_ref[...]
        dd = d_ref[...].astype(BF16)
        dw_ref[...] = _dotg((cc * jax.nn.sigmoid(cc)).astype(BF16), dd, TN)
        dc_ref[...] += _dotg(dd, w_ref[...].astype(BF16), NT)

    return pl.pallas_call(
        body, name=name, grid=(2,),
        in_specs=[pl.BlockSpec((16, D), lambda l: (0, 0)), pl.BlockSpec((None, 16, MOD_COLS), lambda l: (l, 0, 0)),
                  pl.BlockSpec((None, D, MOD_COLS), lambda l: (l, 0, 0))],
        out_specs=[pl.BlockSpec((None, D, MOD_COLS), lambda l: (l, 0, 0)), pl.BlockSpec((16, D), lambda l: (0, 0))],
        out_shape=[jax.ShapeDtypeStruct((2, D, MOD_COLS), F32), jax.ShapeDtypeStruct((16, D), F32)],
        compiler_params=_params("arbitrary"),
    )(c_all, dmod_cols, w_mod)


def _row_tile(rows, cols, n_arrays):
    budget = VMEM_LIMIT_BYTES // 4 // (2 * 4 * n_arrays * cols)
    best = None
    for tr in range(16, rows + 1, 16):
        if rows % tr == 0 and tr <= budget:
            best = tr
    return best if best is not None else rows


def elementwise(fn, ins, out_dtypes, name, ex=None):
    rows, cols = ins[0].shape
    tr = _row_tile(rows, cols, len(ins) + len(out_dtypes))

    def body(*refs):
        outs = fn(*[r[...] for r in refs[:len(ins)]])
        for o_ref, o in zip(refs[len(ins):], outs):
            o_ref[...] = o.astype(o_ref.dtype)

    spec = pl.BlockSpec((tr, cols), lambda i: (i, 0))
    outs, got = _grid_call(body, name, rows // tr, [spec] * len(ins), [spec] * len(out_dtypes),
                           [jax.ShapeDtypeStruct((rows, cols), dt) for dt in out_dtypes], ins, "parallel", ex)
    return outs if ex is None else (outs, got)


def _adamw_tile(w, g, m, v):
    m = ADAM_B1 * m + (1.0 - ADAM_B1) * g
    v = ADAM_B2 * v + (1.0 - ADAM_B2) * (g * g)
    m_hat = m / (1.0 - ADAM_B1 ** ADAM_STEP)
    v_hat = v / (1.0 - ADAM_B2 ** ADAM_STEP)
    return -ADAM_LR * (m_hat / (jnp.sqrt(v_hat) + ADAM_EPS) + ADAM_WD * w), m, v


def adamw(w, g, m, v, name, ex=None):
    shape = w.shape
    two_d = (-1, shape[-1]) if w.ndim > 1 else (1, -1)
    outs = elementwise(_adamw_tile, [a.reshape(two_d) for a in (w, g, m, v)], [F32] * 3, name, ex)
    outs, got = outs if ex is not None else (outs, None)
    outs = [o.reshape(shape) for o in outs]
    return outs if ex is None else (outs, got)


def _prefetch_call(body, name, grid, in_specs, out_specs, out_shape, place, args, ex=None):
    if ex is None:
        spec = pltpu.PrefetchScalarGridSpec(num_scalar_prefetch=1, grid=grid, in_specs=in_specs, out_specs=out_specs)
        return pl.pallas_call(body, name=name, grid_spec=spec, out_shape=out_shape,
                              compiler_params=_params(*["parallel"] * len(grid)))(place, *args)
    n_in, n_out, ci, co = len(in_specs), len(out_specs), len(ex["ins"]), len(ex["out_shape"])
    spec = pltpu.PrefetchScalarGridSpec(num_scalar_prefetch=1, grid=grid, in_specs=list(in_specs) + _any(ci),
                                        out_specs=list(out_specs) + _any(co), scratch_shapes=ex["scratch"])
    outs = pl.pallas_call(
        _carrying(body, grid, n_in, n_out, ex, lead=1), name=name, grid_spec=spec, out_shape=list(out_shape) + ex["out_shape"],
        input_output_aliases={1 + n_in + i: n_out + j for i, j in ex["aliases"].items()},
        compiler_params=_params(*["arbitrary"] * len(grid)))(place, *args, *ex["ins"])
    return outs[:n_out], outs[n_out:]


def cast_place(w, layer, place, name):
    _, r, c = w.shape
    tr = _row_tile(r, c, 2)

    def body(p_ref, w_ref, o_ref):
        o_ref[...] = w_ref[...].astype(BF16)

    return _prefetch_call(
        body, name, (r // tr,), [pl.BlockSpec((None, tr, c), lambda i, p: (layer, i, 0))],
        pl.BlockSpec((None, tr, c), lambda i, p: (p[1], i, 0)), jax.ShapeDtypeStruct((N_SLOT, r, c), BF16), place, [w])


def pair_sum(g32, got, place, name, ex=None):
    n_slot, rh, c = got.shape
    tr = _row_tile(rh, c, 4)
    per = rh // tr

    def body(p_ref, a_ref, b_ref, o_ref, o16_ref):
        r = a_ref[...] + b_ref[...].astype(F32)
        o_ref[...] = r
        o16_ref[...] = r.astype(BF16)

    half = pl.BlockSpec((None, tr, c), lambda s, i, p: (s, i, 0))
    return _prefetch_call(
        body, name, (n_slot, per), [pl.BlockSpec((None, tr, c), lambda s, i, p: (s, p[0] * per + i, 0)), half], [half, half],
        [jax.ShapeDtypeStruct(got.shape, F32), jax.ShapeDtypeStruct(got.shape, BF16)], place, [g32, got], ex)


def chip_sum(p32, got, place, name):
    _, rh, c = p32.shape
    tr = _row_tile(rh, c, 5)
    per = rh // tr

    def body(p_ref, m_ref, r0_ref, r1_ref, r2_ref, o_ref):
        o_ref[...] = m_ref[...] + r0_ref[...].astype(F32) + r1_ref[...].astype(F32) + r2_ref[...].astype(F32)

    part = pl.BlockSpec((tr, c), lambda i, p: (i, 0))
    return _prefetch_call(
        body, name, (per,), [pl.BlockSpec((None, tr, c), lambda i, p: (p[1], i, 0)), part, part, part],
        pl.BlockSpec((tr, c), lambda i, p: (p[0] * per + i, 0)), jax.ShapeDtypeStruct((2 * rh, c), F32), place, [p32, *got])


def adamw_layers(w, g0, g1, m, v, name, ex=None):
    _, r, c = w.shape
    tr = _row_tile(r, c, 10)

    def body(w_ref, g0_ref, g1_ref, m_ref, v_ref, g_ref, d_ref, mo_ref, vo_ref):
        g = jnp.where(pl.program_id(0) == 0, g0_ref[...], g1_ref[...])
        g_ref[...] = g
        d_ref[...], mo_ref[...], vo_ref[...] = _adamw_tile(w_ref[...], g, m_ref[...], v_ref[...])

    stacked = pl.BlockSpec((None, tr, c), lambda l, i: (l, i, 0))
    layer = pl.BlockSpec((tr, c), lambda l, i: (i, 0))
    outs, got = _grid_call(body, name, (2, r // tr), [stacked, layer, layer, stacked, stacked], [stacked] * 4,
                           [jax.ShapeDtypeStruct(w.shape, F32)] * 4, (w, g0, g1, m, v), "parallel", ex)
    return outs if ex is None else (outs, got)


def sum8(gathered, name):
    def body(*refs):
        n = len(refs) // 2
        for g_ref, o_ref in zip(refs[:n], refs[n:]):
            acc = g_ref[0]
            for dev in range(1, N_DEV):
                acc = acc + g_ref[dev]
            o_ref[...] = acc

    return pl.pallas_call(
        body, name=name,
        out_shape=[jax.ShapeDtypeStruct(a.shape[1:], F32) for a in gathered],
        compiler_params=_params(),
    )(*gathered)


PHASES = ("start", "late", "finish")


def _place():
    return lax.axis_index("x"), lax.axis_index("y"), lax.axis_index("c")


def _any(n):
    return [pl.BlockSpec(memory_space=pl.ANY)] * n


def gather8_exchange(blocks):
    n = len(blocks)

    def copy(outs, sems, ti, k, block, to, src=None):
        dst = outs[ti].at[4 * block[0] + 2 * block[1] + block[2]]
        return pltpu.make_async_remote_copy(src_ref=dst if src is None else src, dst_ref=dst, send_sem=sems[0].at[ti, k],
                                            recv_sem=sems[1].at[ti, k], device_id=to, device_id_type=MESH)

    def first(ins, outs, sems):
        x, y, c = _place()
        local, sent = [], []
        for ti in range(n):
            local.append(pltpu.make_async_copy(ins[ti], outs[ti].at[4 * x + 2 * y + c], sems[2].at[ti]))
            sent.append(copy(outs, sems, ti, 0, (x, y, c), (x, y, 1 - c), src=ins[ti]))
            sent += [copy(outs, sems, ti, 1 + j, (x, y, c), (*chip, c), src=ins[ti]) for j, chip in enumerate(_three_chips(x, y))]
        return local, sent

    def start(ins, outs, sems):
        local, sent = first(ins, outs, sems)
        for cp in local + sent:
            cp.start()

    def passed_on(outs, sems):
        x, y, c = _place()
        return [copy(outs, sems, ti, 4 + j, (*chip, c), (x, y, 1 - c)) for ti in range(n) for j, chip in enumerate(_three_chips(x, y))]

    def late(ins, outs, sems):
        x, y, c = _place()
        on = passed_on(outs, sems)
        for ti in range(n):
            for j, chip in enumerate(_three_chips(x, y)):
                copy(outs, sems, ti, 1 + j, (*chip, c), (x, y, c)).wait_recv()
                on[3 * ti + j].start()

    def finish(ins, outs, sems):
        x, y, c = _place()
        me, sibling = (x, y, c), (x, y, 1 - c)
        local, sent = first(ins, outs, sems)
        for ti in range(n):
            copy(outs, sems, ti, 0, sibling, me).wait_recv()
            for j, chip in enumerate(_three_chips(x, y)):
                copy(outs, sems, ti, 4 + j, (*chip, 1 - c), me).wait_recv()
        for cp in sent + passed_on(outs, sems):
            cp.wait_send()
        for cp in local:
            cp.wait()

    return dict(ins=list(blocks), out_shape=[jax.ShapeDtypeStruct((N_DEV,) + b.shape, b.dtype) for b in blocks], aliases={},
                start=start, late=late, finish=finish,
                scratch=[pltpu.SemaphoreType.DMA((n, 7)), pltpu.SemaphoreType.DMA((n, 7)), pltpu.SemaphoreType.DMA((n,))])


def all_gather(blocks, name):
    return run_exchange(gather8_exchange(blocks), name)


def _three_chips(x, y):
    return [(1 - x, y), (x, 1 - y), (1 - x, 1 - y)]


def gather_exchange(placed):
    n = len(placed)

    def copy(bufs, sems, ti, k, chip, core, to):
        rh = bufs[ti].shape[1] // 2
        half = bufs[ti].at[2 * chip[0] + chip[1], pl.ds(core * rh, rh), :]
        return pltpu.make_async_remote_copy(src_ref=half, dst_ref=half, send_sem=sems[0].at[ti, k], recv_sem=sems[1].at[ti, k],
                                            device_id=to, device_id_type=MESH)

    def sends(bufs, sems):
        x, y, c = _place()
        return [copy(bufs, sems, ti, k, (x, y), c, (*chip, c)) for ti in range(n) for k, chip in enumerate(_three_chips(x, y))]

    def passed_on(bufs, sems):
        x, y, c = _place()
        return [copy(bufs, sems, ti, 3 + k, chip, c, (x, y, 1 - c)) for ti in range(n) for k, chip in enumerate(_three_chips(x, y))]

    def start(ins, bufs, sems):
        for cp in sends(bufs, sems):
            cp.start()

    def late(ins, bufs, sems):
        x, y, c = _place()
        on = passed_on(bufs, sems)
        for ti in range(n):
            for k, chip in enumerate(_three_chips(x, y)):
                copy(bufs, sems, ti, k, chip, c, (x, y, c)).wait_recv()
                on[3 * ti + k].start()

    def finish(ins, bufs, sems):
        x, y, c = _place()
        for ti in range(n):
            for k, chip in enumerate(_three_chips(x, y)):
                copy(bufs, sems, ti, 3 + k, chip, 1 - c, (x, y, c)).wait_recv()
        for cp in sends(bufs, sems) + passed_on(bufs, sems):
            cp.wait_send()

    return dict(ins=list(placed), out_shape=[jax.ShapeDtypeStruct(w.shape, w.dtype) for w in placed],
                aliases={i: i for i in range(n)}, start=start, late=late, finish=finish,
                scratch=[pltpu.SemaphoreType.DMA((n, 6)), pltpu.SemaphoreType.DMA((n, 6))])


def scatter_exchange(p16):
    n = len(p16)

    def copies(ins, got, sems):
        x, y, c = _place()
        return [pltpu.make_async_remote_copy(src_ref=ins[ti].at[2 * chip[0] + chip[1]], dst_ref=got[3 * ti + k],
                                             send_sem=sems[0].at[ti, k], recv_sem=sems[1].at[ti, k], device_id=(*chip, c),
                                             device_id_type=MESH)
                for ti in range(n) for k, chip in enumerate(_three_chips(x, y))]

    def start(ins, got, sems):
        for cp in copies(ins, got, sems):
            cp.start()

    def finish(ins, got, sems):
        for cp in copies(ins, got, sems):
            cp.wait()

    return dict(ins=list(p16), out_shape=[jax.ShapeDtypeStruct(a.shape[1:], BF16) for a in p16 for _ in range(3)], aliases={},
                start=start, finish=finish, scratch=[pltpu.SemaphoreType.DMA((n, 3)), pltpu.SemaphoreType.DMA((n, 3))])


def run_exchange(ex, name):
    ci, co = len(ex["ins"]), len(ex["out_shape"])

    def body(*refs):
        ins, outs, sems = refs[:ci], refs[ci:ci + co], refs[ci + co:]
        for phase in PHASES:
            if phase in ex:
                ex[phase](ins, outs, sems)

    return pl.pallas_call(body, name=name, in_specs=_any(ci), out_specs=_any(co), out_shape=ex["out_shape"],
                          input_output_aliases=ex["aliases"], scratch_shapes=ex["scratch"])(*ex["ins"])


def _carrying(body, grid, n_in, n_out, ex, lead=0):
    ci, co = len(ex["ins"]), len(ex["out_shape"])
    first, last = (0,) * len(grid), tuple(g - 1 for g in grid)
    steps = dict(start=first, late=(grid[0] - 2,) if len(grid) == 1 and grid[0] > 2 else last, finish=last)

    def at(ids):
        return functools.reduce(jnp.logical_and, [pl.program_id(ax) == v for ax, v in enumerate(ids)])

    def carrying(*refs):
        head, refs = refs[:lead], refs[lead:]
        c_in, c_out = refs[n_in:n_in + ci], refs[n_in + ci + n_out:n_in + ci + n_out + co]
        sems = refs[n_in + ci + n_out + co:]
        for phase in PHASES:
            if phase == "finish":
                body(*head, *refs[:n_in], *refs[n_in + ci:n_in + ci + n_out])
            if phase in ex:
                pl.when(at(steps[phase]))(functools.partial(ex[phase], c_in, c_out, sems))

    return carrying


def _grid_call(body, name, grid, in_specs, out_specs, out_shape, args, sem, ex=None):
    grid = (grid,) if isinstance(grid, int) else tuple(grid)
    sems_of = (sem,) * len(grid) if isinstance(sem, str) else tuple(sem)
    n_in, n_out = len(in_specs), len(out_specs)
    if ex is None:
        return pl.pallas_call(body, name=name, grid=grid, in_specs=in_specs, out_specs=out_specs, out_shape=out_shape,
                              compiler_params=_params(*sems_of))(*args), []
    ci, co = len(ex["ins"]), len(ex["out_shape"])
    outs = pl.pallas_call(
        _carrying(body, grid, n_in, n_out, ex), name=name, grid=grid, in_specs=list(in_specs) + _any(ci),
        out_specs=list(out_specs) + _any(co), out_shape=list(out_shape) + ex["out_shape"], scratch_shapes=ex["scratch"],
        input_output_aliases={n_in + i: n_out + j for i, j in ex["aliases"].items()},
        compiler_params=_params(*["arbitrary"] * len(grid)),
    )(*args, *ex["ins"])
    return outs[:n_out], outs[n_out:]


def both(*exchanges):
    exchanges = [ex for ex in exchanges if ex is not None]
    if len(exchanges) < 2:
        return exchanges[0] if exchanges else None
    n_ins = [len(ex["ins"]) for ex in exchanges]
    n_outs = [len(ex["out_shape"]) for ex in exchanges]
    n_sems = [len(ex["scratch"]) for ex in exchanges]

    def parts(seq, counts, k):
        first = sum(counts[:k])
        return seq[first:first + counts[k]]

    def run(phase):
        def go(ins, outs, sems):
            for k, ex in enumerate(exchanges):
                if phase in ex:
                    ex[phase](parts(ins, n_ins, k), parts(outs, n_outs, k), parts(sems, n_sems, k))
        return go

    aliases = {sum(n_ins[:k]) + i: sum(n_outs[:k]) + j for k, ex in enumerate(exchanges) for i, j in ex["aliases"].items()}
    return dict(ins=[a for ex in exchanges for a in ex["ins"]], out_shape=[o for ex in exchanges for o in ex["out_shape"]],
                aliases=aliases, scratch=[s for ex in exchanges for s in ex["scratch"]], **{ph: run(ph) for ph in PHASES})


def split_outputs(got, *exchanges):
    got, out = list(got), []
    for ex in exchanges:
        n = len(ex["out_shape"]) if ex is not None else 0
        out.append(got[:n])
        got = got[n:]
    return out


def pair_exchange(g16):
    n = len(g16)

    def copies(a16, got, sems):
        x, y, c = _place()
        out = []
        for ti in range(n):
            rh = a16[ti].shape[1] // 2
            out.append(pltpu.make_async_remote_copy(
                src_ref=a16[ti].at[:, pl.ds((1 - c) * rh, rh), :], dst_ref=got[ti], send_sem=sems[0].at[ti],
                recv_sem=sems[1].at[ti], device_id=(x, y, 1 - c), device_id_type=MESH))
        return out

    def start(a16, got, sems):
        for cp in copies(a16, got, sems):
            cp.start()

    def finish(a16, got, sems):
        for cp in copies(a16, got, sems):
            cp.wait()

    return dict(ins=list(g16), out_shape=[jax.ShapeDtypeStruct((a.shape[0], a.shape[1] // 2, a.shape[2]), BF16) for a in g16],
                aliases={}, start=start, finish=finish, scratch=[pltpu.SemaphoreType.DMA((n,)), pltpu.SemaphoreType.DMA((n,))])


def _scatter_copies(src_ref, lands, send_sems, recv_sems):
    x, y, c = _place()
    return [pltpu.make_async_remote_copy(src_ref=src_ref.at[2 * chip[0] + chip[1]], dst_ref=lands[k], send_sem=send_sems.at[k],
                                         recv_sem=recv_sems.at[k], device_id=(*chip, c), device_id_type=MESH)
            for k, chip in enumerate(_three_chips(x, y))]


def scatter_start(p16, name):
    hbm, sem = pl.BlockSpec(memory_space=pltpu.HBM), pl.BlockSpec(memory_space=pltpu.SEMAPHORE)

    def body(src_ref, l0_ref, l1_ref, l2_ref, send_sems, recv_sems, src_thru, o0_ref, o1_ref, o2_ref, token_ref):
        for cp in _scatter_copies(src_ref, (l0_ref, l1_ref, l2_ref), send_sems, recv_sems):
            cp.start()
        token_ref[...] = jnp.zeros_like(token_ref)

    land = [pltpu.with_memory_space_constraint(lax.empty(p16.shape[1:], BF16), pltpu.HBM) for _ in range(3)]
    return pl.pallas_call(
        body, name=name,
        out_shape=(pltpu.SemaphoreType.DMA((3,)), pltpu.SemaphoreType.DMA((3,)), pltpu.HBM(p16.shape, BF16),
                   *[pltpu.HBM(p16.shape[1:], BF16)] * 3, jax.ShapeDtypeStruct((8, BLK), F32)),
        in_specs=(hbm,) * 4, out_specs=(sem, sem, hbm, hbm, hbm, hbm, pl.BlockSpec(memory_space=pltpu.VMEM)),
        input_output_aliases={0: 2, 1: 3, 2: 4, 3: 5},
        compiler_params=pltpu.CompilerParams(has_side_effects=pltpu.SideEffectType.DATAFLOW_SIDE_EFFECTING),
    )(pltpu.with_memory_space_constraint(p16, pltpu.HBM), *land)


def scatter_wait(send_sems, recv_sems, src_thru, lands, after, name):
    hbm, sem = pl.BlockSpec(memory_space=pltpu.HBM), pl.BlockSpec(memory_space=pltpu.SEMAPHORE)

    def body(src_ref, l0_ref, l1_ref, l2_ref, send_sems, recv_sems, after_ref, src_dead, g0_ref, g1_ref, g2_ref):
        for cp in _scatter_copies(src_ref, (l0_ref, l1_ref, l2_ref), send_sems, recv_sems):
            cp.wait_send()
            cp.wait_recv()

    return pl.pallas_call(
        body, name=name, out_shape=(pltpu.HBM(src_thru.shape, BF16), *[pltpu.HBM(lands[0].shape, BF16)] * 3),
        in_specs=(hbm, hbm, hbm, hbm, sem, sem, pl.BlockSpec(memory_space=pl.ANY)), out_specs=(hbm,) * 4,
        input_output_aliases={0: 0, 1: 1, 2: 2, 3: 3},
        compiler_params=pltpu.CompilerParams(has_side_effects=pltpu.SideEffectType.DATAFLOW_SIDE_EFFECTING),
    )(src_thru, *lands, send_sems, recv_sems, after)[1:]


def pair_gather(halves, name):
    n = len(halves)

    def body(*refs):
        bufs = refs[n:2 * n]
        send_sems, recv_sems = refs[2 * n:]
        x, y, c = _place()
        copies = []
        for ti in range(n):
            rh = bufs[ti].shape[0] // 2
            rows = bufs[ti].at[pl.ds(c * rh, rh), :]
            copies.append(pltpu.make_async_remote_copy(src_ref=rows, dst_ref=rows, send_sem=send_sems.at[ti],
                                                       recv_sem=recv_sems.at[ti], device_id=(x, y, 1 - c), device_id_type=MESH))
        for cp in copies:
            cp.start()
        for ti, cp in enumerate(copies):
            cp.wait_send()
            rh = bufs[ti].shape[0] // 2
            theirs = bufs[ti].at[pl.ds((1 - c) * rh, rh), :]
            pltpu.make_async_remote_copy(src_ref=theirs, dst_ref=theirs, send_sem=send_sems.at[ti], recv_sem=recv_sems.at[ti],
                                         device_id=(x, y, 1 - c), device_id_type=MESH).wait_recv()

    return pl.pallas_call(
        body, name=name, in_specs=_any(n), out_specs=_any(n), input_output_aliases={i: i for i in range(n)},
        out_shape=[jax.ShapeDtypeStruct(a.shape, a.dtype) for a in halves],
        scratch_shapes=[pltpu.SemaphoreType.DMA((n,)), pltpu.SemaphoreType.DMA((n,))],
    )(*halves)


def reduce_small(dm_f1, dm_mix, dm_gate, dm_f2, loss_blk, name):
    def body(f1_ref, mix_ref, gate_ref, f2_ref, l_ref, tot_ref, rows_ref, fin_ref):
        rows_ref[...] = jnp.zeros_like(rows_ref)
        tot_ref[...] = jnp.zeros_like(tot_ref)
        mod_src = [(f1_ref, 0), (f1_ref, 1), (f1_ref, 2), (mix_ref, 0), (mix_ref, 1), (gate_ref, 2),
                   (f2_ref, 0), (f2_ref, 1), (f2_ref, 2)]
        norm_src = [(f1_ref, 3), (mix_ref, 3), (f2_ref, 3)]
        for l in range(2):
            for k, (ref, r) in enumerate(mod_src + norm_src):
                lat = ref[0, l, 0, r:r + 1, :]
                ctx = ref[0, l, 1, r:r + 1, :]
                for dev in range(N_DEV):
                    if dev:
                        lat = lat + ref[dev, l, 0, r:r + 1, :]
                        ctx = ctx + ref[dev, l, 1, r:r + 1, :]
                    if k < N_MOD:
                        rows_ref[l, dev, k:k + 1, :] = ref[dev, l, 0, r:r + 1, :]
                if k < N_MOD:
                    rows_ref[l, N_DEV, k:k + 1, :] = ctx
                tot_ref[l, k:k + 1, :] = lat + ctx
        acc = l_ref[0]
        for dev in range(1, N_DEV):
            acc = acc + l_ref[dev]
        loss = (0.5 / D) * jnp.sum(acc[1:2, :], axis=1, keepdims=True)
        row = lax.broadcasted_iota(jnp.int32, (8, D), 0)
        fin_ref[...] = jnp.where(row == 0, acc[0:1, :], loss)

    return pl.pallas_call(
        body, name=name,
        out_shape=[jax.ShapeDtypeStruct((2, 16, D), F32), jax.ShapeDtypeStruct((2, 16, 16, D), F32),
                   jax.ShapeDtypeStruct((8, D), F32)],
        compiler_params=_params(),
    )(dm_f1, dm_mix, dm_gate, dm_f2, loss_blk)


def rope_tables(t, s):
    rows = t // GRID_W
    row = jnp.repeat(jnp.arange(rows), GRID_W).astype(F32)
    col = jnp.tile(jnp.arange(GRID_W), rows).astype(F32)
    inv = ROPE_BASE ** (-jnp.arange(0, HEAD // 2, 2, dtype=F32) / (HEAD // 2))
    ang = jnp.concatenate([row[:, None] * inv, col[:, None] * inv], axis=-1)
    cos, sin = jnp.cos(ang), jnp.sin(ang)
    cos = jnp.concatenate([jnp.tile(cos, (1, 4)), jnp.ones((s - t, BLK), F32)], axis=0)
    sin = jnp.concatenate([jnp.tile(jnp.concatenate([-sin, sin], axis=1), (1, 2)), jnp.zeros((s - t, BLK), F32)], axis=0)
    return cos, sin


BIG = ("ffn1_in", "ffn1_out", "w_in", "w_out", "ffn2_in", "ffn2_out")
GROUPS = dict(ffn1=("ffn1_in", "ffn1_out"), mix=("w_in", "w_out"), ffn2=("ffn2_in", "ffn2_out"))
GATHER_BEHIND = {("ffn1", 0): [("w_in", 0), ("ffn2_out", 0), ("ffn1_out", 1)], ("proj", 0): [("w_out", 0)],
                 ("mix", 0): [("ffn2_in", 0)], ("ffn2", 0): [("ffn1_in", 1), ("w_in", 1)],
                 ("ffn1", 1): [("ffn2_in", 1), ("w_out", 1)], ("mix", 1): [("ffn2_out", 1)]}


def _slot_major(name, g):
    if name == "w_in":
        return jnp.stack(jnp.split(g, N_SLOT, axis=1), axis=0)
    if name in ("ffn1_in", "ffn2_in"):
        return g
    return g.reshape(N_SLOT, g.shape[0] // N_SLOT, g.shape[1])


def _whole_weight(name, buf):
    if name == "w_in":
        return buf.transpose(1, 0, 2).reshape(D, PROJ_W)
    if name in ("ffn1_in", "ffn2_in"):
        return buf
    return buf.reshape(-1, buf.shape[2])


def local_step(x1, ctx1, target, mods, norms, nfinal, placed, w_pool, pool_scale, sink, place, small_blocks):
    t, s = x1.shape[0], x1.shape[0] + ctx1.shape[0]
    n_lat = t // TM
    cos, sin = rope_tables(t, s)
    tables = mix_tables(t, s)
    wts ={name: list(pair) for name, pair in placed.items()}

    def gather(tensors):
        return gather_exchange([wts[name][l] for name, l in tensors])

    def gathered(tensors, arrays):
        for (name, l), whole in zip(tensors, arrays):
            wts[name][l] = whole

    def weight(name, l):
        return _whole_weight(name, wts[name][l])

    def fwd_ex(grp, l):
        groups = GATHER_BEHIND.get((grp, l))
        return (groups, gather(groups)) if groups else (None, None)

    first = [("ffn1_in", 0), ("ffn1_out", 0)]
    gathered(first, run_exchange(gather(first), "gather_first"))
    h = jnp.concatenate([x1, ctx1], axis=0)
    saved = []
    for l in range(2):
        h0 = h
        groups, ex = fwd_ex("ffn1", l)
        (h1, ab1, f1), got = ffn_fwd(h0, mods, norms[0], weight("ffn1_in", l), weight("ffn1_out", l), l, 0, n_lat, f"ffn1_fwd_{l}", ex)
        gathered(groups or [], got)
        groups, ex = fwd_ex("proj", l)
        (u, q, k, v), got = proj_fwd(h1, mods, norms[1], weight("w_in", l), cos, sin, l, n_lat, f"proj_fwd_{l}", ex)
        gathered(groups or [], got)
        groups, ex = fwd_ex("mix", l)
        (h2, cat, lse, mo), got = mix_fwd(h1, q, k, v, u, w_pool, pool_scale, sink, weight("w_out", l), mods, tables, l, t,
                                          f"mix_fwd_{l}", ex)
        gathered(groups or [], got)
        groups, ex = fwd_ex("ffn2", l)
        (h, ab2, f2), got = ffn_fwd(h2, mods, norms[2], weight("ffn2_in", l), weight("ffn2_out", l), l, 6, n_lat, f"ffn2_fwd_{l}", ex)
        gathered(groups or [], got)
        saved.append((h0, ab1, f1, h1, u, q, k, v, cat, lse, mo, h2, ab2, f2))
    dh, loss_blk = loss_head(h, target, nfinal, t, "loss_head")

    halves = {name: [None, None] for name in BIG}
    pending = []

    def summed_in_pair(grp, l, name_a, g_a, name_b, wgrad_b):
        g_b, got_a = wgrad_b(pair_exchange([_slot_major(name_a, g_a[1])]))
        sum_a, got_b = pair_sum(_slot_major(name_a, g_a[0]), got_a[0], place, f"pair_sum_{name_a}_{l}",
                                pair_exchange([_slot_major(name_b, g_b[1])]))
        sums = {name_a: sum_a, name_b: pair_sum(_slot_major(name_b, g_b[0]), got_b[0], place, f"pair_sum_{name_b}_{l}")}
        pending.append((grp, l, [sums[n] for n in GROUPS[grp]]))

    def scatter():
        return scatter_exchange([p16 for _, p16 in pending[0][2]]) if pending else None

    def scattered(got):
        if pending:
            grp, l, pairs = pending.pop(0)
            for i, name in enumerate(GROUPS[grp]):
                halves[name][l] = chip_sum(pairs[i][0], got[3 * i:3 * i + 3], place, f"chip_sum_{name}_{l}")

    small = [None, None]
    for l in (1, 0):
        h0, ab1, f1, h1, u, q, k, v, cat, lse, mo, h2, ab2, f2 = saved[l]
        (dh, dab, df, n, act, dm_f2), got = ffn_bwd(h2, ab2, f2, dh, mods, norms[2], weight("ffn2_in", l), weight("ffn2_out", l),
                                                    l, 6, n_lat, f"ffn2_bwd_{l}", scatter())
        scattered(got)
        g_in, _ = wgrad(n, dab, D, FF_COLS, FF_COLS, f"ffn2_in_wgrad_{l}")
        summed_in_pair("ffn2", l, "ffn2_in", g_in, "ffn2_out",
                       lambda ex, a=act, b=df: wgrad(a, b, D_FF // 2, D, None, f"ffn2_out_wgrad_{l}", ex))
        (dq, dk, dv, du, dmo, dwp, dps, dsink, dm_gate), got = mix_bwd(
            dh, mo, q, k, v, u, lse, w_pool, pool_scale, sink, weight("w_out", l), mods, tables, l, t, f"mix_bwd_{l}", scatter())
        scattered(got)
        g_wo, _ = wgrad(cat, dmo, POOL_W + ATTN_W, D, None, f"w_out_wgrad_{l}")
        dh, dp, n, dm_mix = proj_bwd(h1, du, dq, dk, dv, dh, mods, norms[1], weight("w_in", l), cos, sin, l, n_lat, f"proj_bwd_{l}")
        summed_in_pair("mix", l, "w_out", g_wo, "w_in",
                       lambda ex, a=n, b=dp: wgrad(a, b, D, PROJ_W // 2, None, f"w_in_wgrad_{l}", ex))
        (dh, dab, df, n, act, dm_f1), got = ffn_bwd(h0, ab1, f1, dh, mods, norms[0], weight("ffn1_in", l), weight("ffn1_out", l),
                                                    l, 0, n_lat, f"ffn1_bwd_{l}", scatter())
        scattered(got)
        small[l] = dict(dm_f1=dm_f1, dm_mix=dm_mix, dm_gate=dm_gate, dm_f2=dm_f2, dwp=dwp, dps=dps, dsink=dsink)
        if l:
            g_in, _ = wgrad(n, dab, D, FF_COLS, FF_COLS, f"ffn1_in_wgrad_{l}")
            summed_in_pair("ffn1", l, "ffn1_in", g_in, "ffn1_out",
                           lambda ex, a=act, b=df: wgrad(a, b, D_FF // 2, D, None, f"ffn1_out_wgrad_{l}", ex))
    g_out, small_all = wgrad(act, df, D_FF // 2, D, None, "ffn1_out_wgrad_0", gather8_exchange(small_blocks(small, loss_blk)))
    got = run_exchange(pair_exchange([_slot_major("ffn1_out", g_out[1])]), "pair_exchange_ffn1_out_0")
    p32, p16 = pair_sum(_slot_major("ffn1_out", g_out[0]), got[0], place, "pair_sum_ffn1_out_0")
    g_in, got = wgrad(n, dab, D, FF_COLS, FF_COLS, "ffn1_in_wgrad_0", scatter_exchange([p16]))
    halves["ffn1_out"][0] = chip_sum(p32, got, place, "chip_sum_ffn1_out_0")
    got = run_exchange(pair_exchange([_slot_major("ffn1_in", g_in[1])]), "pair_exchange_ffn1_in_0")
    return dh[:t], halves, pair_sum(_slot_major("ffn1_in", g_in[0]), got[0], place, "pair_sum_ffn1_in_0"), small_all


def _silu_grad(z):
    sg = jax.nn.sigmoid(z)
    return sg * (1 + z * (1 - sg))


def kernel(x, c, ctx, c_ctx, w_mod, b_mod, norm_ffn1, w_ffn1_in, w_ffn1_out, norm_mix, w_in, w_pool, pool_scale, sink, w_out, norm_ffn2, w_ffn2_in, w_ffn2_out, norm_final, loss_target, m_c_ctx, m_w_mod, m_b_mod, m_norm_ffn1, m_w_ffn1_in, m_w_ffn1_out, m_norm_mix, m_w_in, m_w_pool, m_pool_scale, m_sink, m_w_out, m_norm_ffn2, m_w_ffn2_in, m_w_ffn2_out, m_norm_final, v_c_ctx, v_w_mod, v_b_mod, v_norm_ffn1, v_w_ffn1_in, v_w_ffn1_out, v_norm_mix, v_w_in, v_w_pool, v_pool_scale, v_sink, v_w_out, v_norm_ffn2, v_w_ffn2_in, v_w_ffn2_out, v_norm_final):
    px, py, pc = _place()
    slot, me = 2 * px + py, 4 * px + 2 * py + pc
    n_grp = len(POOL_WINDOWS)

    (c_rows,) = all_gather([c.reshape(8, D // 8)], "gather_c")
    c_all = jnp.concatenate([c_rows.reshape(N_DEV, D), c_ctx.reshape(1, D), jnp.zeros((16 - N_DEV - 1, D), F32)], axis=0)
    b_cols = lax.dynamic_slice(b_mod, (0, slot * MOD_COLS), (2, MOD_COLS)).reshape(2, 1, MOD_COLS)
    (mod_parts,) = all_gather([mod_rows(c_all, w_mod, b_cols, "mod_rows")], "gather_mods")
    mods_all = mod_parts[0::2].transpose(1, 2, 0, 3).reshape(2, 16, N_MOD * D)
    mx = lax.dynamic_slice(mods_all, (0, me, 0), (2, 1, N_MOD * D)).reshape(2, N_MOD, D)
    mc = mods_all[:, N_DEV].reshape(2, N_MOD, D)
    pad = jnp.zeros((2, 16 - N_MOD, D), F32)
    mods = jnp.stack([jnp.concatenate([mx, pad], axis=1), jnp.concatenate([mc, pad], axis=1)], axis=1)

    place = jnp.stack([pc, slot]).astype(jnp.int32)
    shards = dict(ffn1_in=w_ffn1_in, ffn1_out=w_ffn1_out, w_in=w_in, w_out=w_out, ffn2_in=w_ffn2_in, ffn2_out=w_ffn2_out)
    placed = {name: [cast_place(shards[name], l, place, f"cast_{name}_{l}") for l in range(2)] for name in BIG}
    norms = [g.reshape(2, 1, D) for g in (norm_ffn1, norm_mix, norm_ffn2)]
    row_sums = ("dm_f1", "dm_mix", "dm_gate", "dm_f2")

    def small_blocks(small, loss_blk):
        stacked = {k: jnp.stack([small[0][k], small[1][k]]) for k in row_sums + ("dwp", "dps", "dsink")}
        return ([stacked[k].reshape(32, D) for k in row_sums]
                + [stacked["dwp"].reshape(2 * n_grp * GROUP, GROUP), stacked["dps"].reshape(16, POOL_W),
                   stacked["dsink"].reshape(16, BLK), loss_blk])

    dx, halves, last_pair, small_all = local_step(x[0], ctx[0], loss_target[0], mods, norms, norm_final.reshape(1, D), placed,
                                                   w_pool.astype(BF16), pool_scale.reshape(2, 1, POOL_W), sink, place, small_blocks)
    grads = {}

    *g_dm, g_dwp, g_dps, g_dsink, g_loss = small_all
    tot, rows, fin = reduce_small(*[g.reshape(N_DEV, 2, 2, 8, D) for g in g_dm], g_loss, "reduce_small")
    s_dwp, s_dps, s_dsink = sum8([g_dwp, g_dps, g_dsink], "sum_pool_sink")
    grads.update(
        w_pool=s_dwp.reshape(2, n_grp, GROUP, GROUP), pool_scale=s_dps.reshape(2, 8, POOL_W)[:, 0],
        sink=s_dsink.reshape(2, 8, BLK)[:, 0, :N_HEADS], b_mod=tot[:, :N_MOD].reshape(2, N_MOD * D),
        norm_ffn1=tot[:, N_MOD], norm_mix=tot[:, N_MOD + 1], norm_ffn2=tot[:, N_MOD + 2], norm_final=fin[0])
    loss = fin[1, 0]

    dmod_cols = lax.dynamic_slice(rows[:, :, :N_MOD, :].reshape(2, 16, N_MOD * D), (0, 0, slot * MOD_COLS), (2, 16, MOD_COLS))
    grads["w_mod"], dc = mod_grads(c_all, dmod_cols, w_mod, "mod_grads")
    (g_dc,) = all_gather([dc], "gather_dc")
    (s_dc,) = sum8([g_dc], "sum_dc")
    (d_c_ctx,) = elementwise(lambda d, z: (0.5 * d * _silu_grad(z),), [s_dc[N_DEV:N_DEV + 1], c_ctx.reshape(1, D)], [F32], "c_ctx_grad")
    send_sems, recv_sems, src_thru, *lands, token = scatter_start(last_pair[1], "scatter_last_start")
    grads["c_ctx"] = d_c_ctx.reshape(D) + token[0, :1]

    given = dict(c_ctx=(c_ctx, m_c_ctx, v_c_ctx), w_mod=(w_mod, m_w_mod, v_w_mod), b_mod=(b_mod, m_b_mod, v_b_mod),
                 norm_ffn1=(norm_ffn1, m_norm_ffn1, v_norm_ffn1), w_ffn1_in=(w_ffn1_in, m_w_ffn1_in, v_w_ffn1_in),
                 w_ffn1_out=(w_ffn1_out, m_w_ffn1_out, v_w_ffn1_out), norm_mix=(norm_mix, m_norm_mix, v_norm_mix),
                 w_in=(w_in, m_w_in, v_w_in), w_pool=(w_pool, m_w_pool, v_w_pool),
                 pool_scale=(pool_scale, m_pool_scale, v_pool_scale), sink=(sink, m_sink, v_sink), w_out=(w_out, m_w_out, v_w_out),
                 norm_ffn2=(norm_ffn2, m_norm_ffn2, v_norm_ffn2), w_ffn2_in=(w_ffn2_in, m_w_ffn2_in, v_w_ffn2_in),
                 w_ffn2_out=(w_ffn2_out, m_w_ffn2_out, v_w_ffn2_out), norm_final=(norm_final, m_norm_final, v_norm_final))
    ready = [(name, l) for name in BIG for l in range(2) if halves[name][l] is not None]
    shard = dict(zip(ready, pair_gather([halves[name][l] for name, l in ready], "grad_pair_gather")))

    def update(name):
        w, m, v = given[name]
        if name in BIG or name[2:] in BIG:
            key = name if name in BIG else name[2:]
            return adamw_layers(w, shard[key, 0], shard[key, 1], m, v, f"adamw_{name}")
        return [grads[name], *adamw(w, grads[name], m, v, f"adamw_{name}")]

    done = {name: update(name) for name in given if name != "w_ffn1_in"}
    got = scatter_wait(send_sems, recv_sems, src_thru, lands, done["w_ffn2_out"][3], "scatter_last_wait")
    (shard["ffn1_in", 0],) = pair_gather([chip_sum(last_pair[0], got, place, "chip_sum_ffn1_in_0")], "grad_pair_gather_last")
    done["w_ffn1_in"] = update("w_ffn1_in")
    return (loss, dx[None], *[done[name][i] for i in range(4) for name in given])
```

```python
import functools

import jax
import jax.numpy as jnp
from jax import lax
from jax.experimental import pallas as pl
from jax.experimental.pallas import tpu as pltpu

F32, BF16 = jnp.float32, jnp.bfloat16
D = 1024
D_FF = 2816
N_SLOT = 4
FF_COLS = 2 * D_FF // N_SLOT
N_MOD = 9
MOD_COLS = N_MOD * D // N_SLOT
POOL_W, ATTN_W, KV_W = 512, 512, 128
PROJ_W = POOL_W + ATTN_W + 2 * KV_W
N_HEADS, Q_GROUP, HEAD = 8, 4, 64
GROUP = 128
POOL_WINDOWS = (2, 4, 8, 16)
BLK = 128
QB = 256
WIN = QB + 2 * BLK
GRID_W = 64
ROPE_BASE = 10000.0
EPS = 1e-6
NEG_INF = -1e30
TM = 256
N_DEV = 8
VMEM_LIMIT_BYTES = 56 * 1024 * 1024
ADAM_LR, ADAM_B1, ADAM_B2, ADAM_EPS, ADAM_WD, ADAM_STEP = 0.001, 0.9, 0.999, 1e-08, 0.01, 10
MESH = pl.DeviceIdType.MESH
NT = (((1,), (1,)), ((), ()))
TN = (((0,), (0,)), ((), ()))


def _params(*sem):
    return pltpu.CompilerParams(dimension_semantics=sem, vmem_limit_bytes=VMEM_LIMIT_BYTES)


def _whole(shape, lead=()):
    idx = tuple(lead) + (0,) * len(shape)
    return pl.BlockSpec((None,) * len(lead) + tuple(shape), lambda *_: idx, pipeline_mode=pl.Buffered(1))


def _rows(cols, tm=TM):
    return pl.BlockSpec((tm, cols), lambda i: (i, 0))


def _mods_spec(layer, n_lat):
    return pl.BlockSpec((None, None, 16, D), lambda i: (layer, (i >= n_lat).astype(jnp.int32), 0, 0))


def _acc_spec(n_lat):
    return pl.BlockSpec((None, 8, D), lambda i: ((i >= n_lat).astype(jnp.int32), 0, 0))


def _dot(a, b):
    return jnp.dot(a, b, preferred_element_type=F32)


def _dotg(a, b, dims):
    return lax.dot_general(a, b, dims, preferred_element_type=F32)


def _sum0(v):
    return jnp.sum(v, axis=0, keepdims=True)


def _norm_mod(h, g, shift, scale):
    r = lax.rsqrt(jnp.mean(h * h, axis=-1, keepdims=True) + EPS)
    xhat = h * r
    y = xhat * g
    return y * (1 + scale) + shift, xhat, r, y


def _norm_mod_bwd(dn, xhat, r, y, g, scale):
    dy = dn * (1 + scale)
    dx = dy * g
    dh = r * (dx - xhat * jnp.mean(dx * xhat, axis=-1, keepdims=True))
    return _sum0(dn), _sum0(dn * y), _sum0(dy * xhat), dh


def _swap_halves(v):
    w = v.shape[1]
    lane = lax.broadcasted_iota(jnp.int32, v.shape, 1)
    return jnp.where(lane % HEAD < HEAD // 2, pltpu.roll(v, w - HEAD // 2, axis=1), pltpu.roll(v, HEAD // 2, axis=1))


def _tile_lanes(t, width):
    return t if width == t.shape[1] else jnp.concatenate([t] * (width // t.shape[1]), axis=1)


def _rope(v, cos, sin):
    return v * _tile_lanes(cos, v.shape[1]) + _swap_halves(v) * _tile_lanes(sin, v.shape[1])


def _unrope(g, cos, sin):
    return g * _tile_lanes(cos, g.shape[1]) + _swap_halves(g * _tile_lanes(sin, g.shape[1]))


def ffn_fwd(h, mods, g, w4, wo, layer, k0, n_lat, name, ex=None):
    s = h.shape[0]

    def body(h_ref, m_ref, g_ref, w_ref, wo_ref, ho_ref, ab_ref, f_ref):
        hh = h_ref[...]
        n, _, _, _ = _norm_mod(hh, g_ref[...], m_ref[k0:k0 + 1, :], m_ref[k0 + 1:k0 + 2, :])
        nb = n.astype(BF16)
        acc = jnp.zeros((TM, D), F32)
        for j in range(2):
            a = _dot(nb, w_ref[j])
            b = _dot(nb, w_ref[2 + j])
            ab_ref[:, j * FF_COLS:(j + 1) * FF_COLS] = a.astype(BF16)
            ab_ref[:, (2 + j) * FF_COLS:(3 + j) * FF_COLS] = b.astype(BF16)
            act = (a * jax.nn.sigmoid(a) * b).astype(BF16)
            acc = acc + _dot(act, wo_ref[j * FF_COLS:(j + 1) * FF_COLS, :])
        f_ref[...] = acc
        ho_ref[...] = hh + 0.5 * m_ref[k0 + 2:k0 + 3, :] * acc

    return _grid_call(
        body, name, s // TM,
        [_rows(D), _mods_spec(layer, n_lat), _whole((1, D), (layer,)), _whole((N_SLOT, D, FF_COLS)), _whole((D_FF, D))],
        [_rows(D), _rows(2 * D_FF), _rows(D)],
        [jax.ShapeDtypeStruct((s, D), F32), jax.ShapeDtypeStruct((s, 2 * D_FF), BF16), jax.ShapeDtypeStruct((s, D), F32)],
        (h, mods, g, w4, wo), "parallel", ex)


def ffn_bwd(h, ab, f, dh, mods, g, w4, wo, layer, k0, n_lat, name, ex=None):
    s = h.shape[0]

    def body(h_ref, ab_ref, f_ref, dh_ref, m_ref, g_ref, w_ref, wo_ref, dhi_ref, dab_ref, df_ref, n_ref, act_ref, dm_ref):
        i = pl.program_id(0)

        @pl.when((i == 0) | (i == n_lat))
        def _():
            dm_ref[...] = jnp.zeros_like(dm_ref)

        hh, dho, gg = h_ref[...], dh_ref[...], g_ref[...]
        scale, gate = m_ref[k0 + 1:k0 + 2, :], m_ref[k0 + 2:k0 + 3, :]
        n, xhat, r, y = _norm_mod(hh, gg, m_ref[k0:k0 + 1, :], scale)
        n_ref[...] = n.astype(BF16)
        dgate = _sum0(dho * (0.5 * f_ref[...]))
        dfb = ((0.5 * gate) * dho).astype(BF16)
        df_ref[...] = dfb
        dn = jnp.zeros((TM, D), F32)
        for j in range(2):
            a = ab_ref[:, j * FF_COLS:(j + 1) * FF_COLS].astype(F32)
            b = ab_ref[:, (2 + j) * FF_COLS:(3 + j) * FF_COLS].astype(F32)
            sg = jax.nn.sigmoid(a)
            sa = a * sg
            act_ref[:, j * FF_COLS:(j + 1) * FF_COLS] = (sa * b).astype(BF16)
            dact = _dotg(dfb, wo_ref[j * FF_COLS:(j + 1) * FF_COLS, :], NT)
            da = (dact * b * (sg * (1 + a * (1 - sg)))).astype(BF16)
            db = (dact * sa).astype(BF16)
            dab_ref[:, j * FF_COLS:(j + 1) * FF_COLS] = da
            dab_ref[:, (2 + j) * FF_COLS:(3 + j) * FF_COLS] = db
            dn = dn + _dotg(da, w_ref[j], NT) + _dotg(db, w_ref[2 + j], NT)
        dsh, dsc, dg, dhn = _norm_mod_bwd(dn, xhat, r, y, gg, scale)
        dhi_ref[...] = dho + dhn
        dm_ref[0:1, :] += dsh
        dm_ref[1:2, :] += dsc
        dm_ref[2:3, :] += dgate
        dm_ref[3:4, :] += dg

    return _grid_call(
        body, name, s // TM,
        [_rows(D), _rows(2 * D_FF), _rows(D), _rows(D), _mods_spec(layer, n_lat), _whole((1, D), (layer,)),
         _whole((N_SLOT, D, FF_COLS)), _whole((D_FF, D))],
        [_rows(D), _rows(2 * D_FF), _rows(D), _rows(D), _rows(D_FF), _acc_spec(n_lat)],
        [jax.ShapeDtypeStruct((s, D), F32), jax.ShapeDtypeStruct((s, 2 * D_FF), BF16), jax.ShapeDtypeStruct((s, D), BF16),
         jax.ShapeDtypeStruct((s, D), BF16), jax.ShapeDtypeStruct((s, D_FF), BF16), jax.ShapeDtypeStruct((2, 8, D), F32)],
        (h, ab, f, dh, mods, g, w4, wo), "arbitrary", ex)


def _token_tile(s, limit=2176):
    return max(ts for ts in range(16, limit + 1, 16) if s % ts == 0)


def wgrad(a, b, tk, tn, slot_cols, name, ex=None):
    s, k = a.shape
    n = b.shape[1]
    ts = _token_tile(s)
    steps = s // ts

    def body(a_ref, b_ref, o_ref, o16_ref):
        r = _dotg(a_ref[...], b_ref[...], TN)
        si = pl.program_id(2)

        @pl.when(si == 0)
        def _():
            o_ref[...] = r

        @pl.when(si > 0)
        def _():
            o_ref[...] += r

        @pl.when(si == steps - 1)
        def _():
            o16_ref[...] = o_ref[...].astype(BF16)

    if slot_cols is None:
        shape, spec = (k, n), pl.BlockSpec((tk, tn), lambda i, j, si: (i, j))
    else:
        per = slot_cols // tn
        shape, spec = (n // slot_cols, k, slot_cols), pl.BlockSpec((None, tk, tn), lambda i, j, si: (lax.div(j, per), i, lax.rem(j, per)))
    return _grid_call(
        body, name, (k // tk, n // tn, steps),
        [pl.BlockSpec((ts, tk), lambda i, j, si: (si, i)), pl.BlockSpec((ts, tn), lambda i, j, si: (si, j))], [spec, spec],
        [jax.ShapeDtypeStruct(shape, F32), jax.ShapeDtypeStruct(shape, BF16)], (a, b), ("parallel", "parallel", "arbitrary"), ex)


def proj_fwd(h, mods, g, w_in, cos, sin, layer, n_lat, name, ex=None):
    s = h.shape[0]

    def body(h_ref, m_ref, g_ref, w_ref, cos_ref, sin_ref, u_ref, q_ref, k_ref, v_ref):
        n, _, _, _ = _norm_mod(h_ref[...], g_ref[...], m_ref[3:4, :], m_ref[4:5, :])
        p = _dot(n.astype(BF16), w_ref[...])
        cs, sn = cos_ref[...], sin_ref[...]
        u_ref[...] = p[:, :POOL_W]
        q_ref[...] = (_rope(p[:, POOL_W:POOL_W + ATTN_W], cs, sn) * HEAD ** -0.5).astype(BF16)
        k_ref[...] = _rope(p[:, POOL_W + ATTN_W:POOL_W + ATTN_W + KV_W], cs, sn).astype(BF16)
        v_ref[...] = p[:, POOL_W + ATTN_W + KV_W:].astype(BF16)

    return _grid_call(
        body, name, s // TM,
        [_rows(D), _mods_spec(layer, n_lat), _whole((1, D), (layer,)), _whole((D, PROJ_W)), _rows(BLK), _rows(BLK)],
        [_rows(POOL_W), _rows(ATTN_W), _rows(KV_W), _rows(KV_W)],
        [jax.ShapeDtypeStruct((s, POOL_W), F32), jax.ShapeDtypeStruct((s, ATTN_W), BF16),
         jax.ShapeDtypeStruct((s, KV_W), BF16), jax.ShapeDtypeStruct((s, KV_W), BF16)],
        (h, mods, g, w_in, cos, sin), "parallel", ex)


def proj_bwd(h, du, dq, dk, dv, dh, mods, g, w_in, cos, sin, layer, n_lat, name):
    s = h.shape[0]

    def body(h_ref, du_ref, dq_ref, dk_ref, dv_ref, dh_ref, m_ref, g_ref, w_ref, cos_ref, sin_ref,
             dhi_ref, dp_ref, n_ref, dm_ref):
        i = pl.program_id(0)

        @pl.when((i == 0) | (i == n_lat))
        def _():
            dm_ref[...] = jnp.zeros_like(dm_ref)

        gg, scale = g_ref[...], m_ref[4:5, :]
        n, xhat, r, y = _norm_mod(h_ref[...], gg, m_ref[3:4, :], scale)
        n_ref[...] = n.astype(BF16)
        cs, sn = cos_ref[...], sin_ref[...]
        dp = jnp.concatenate([du_ref[...], _unrope(dq_ref[...], cs, sn) * HEAD ** -0.5, _unrope(dk_ref[...], cs, sn),
                              dv_ref[...]], axis=1).astype(BF16)
        dp_ref[...] = dp
        dsh, dsc, dg, dhn = _norm_mod_bwd(_dotg(dp, w_ref[...], NT), xhat, r, y, gg, scale)
        dhi_ref[...] = dh_ref[...] + dhn
        dm_ref[0:1, :] += dsh
        dm_ref[1:2, :] += dsc
        dm_ref[3:4, :] += dg

    return pl.pallas_call(
        body, name=name, grid=(s // TM,),
        in_specs=[_rows(D), _rows(POOL_W), _rows(ATTN_W), _rows(KV_W), _rows(KV_W), _rows(D), _mods_spec(layer, n_lat),
                  _whole((1, D), (layer,)), _whole((D, PROJ_W)), _rows(BLK), _rows(BLK)],
        out_specs=[_rows(D), _rows(PROJ_W), _rows(D), _acc_spec(n_lat)],
        out_shape=[jax.ShapeDtypeStruct((s, D), F32), jax.ShapeDtypeStruct((s, PROJ_W), BF16),
                   jax.ShapeDtypeStruct((s, D), BF16), jax.ShapeDtypeStruct((2, 8, D), F32)],
        compiler_params=_params("arbitrary"),
    )(h, du, dq, dk, dv, dh, mods, g, w_in, cos, sin)


def _window(i, s):
    return pl.multiple_of(jnp.clip(i * QB - BLK, 0, s - WIN), BLK)


def mix_tables(t, s):
    n_lat = t // QB
    blocks = jnp.array([0, 1, n_lat - 1] + list(range(n_lat, s // QB)))[:, None, None]
    ws = jnp.clip(blocks * QB - BLK, 0, s - WIN)
    q = blocks * QB + jnp.arange(QB)[None, :, None]
    k = ws + jnp.arange(WIN)[None, None, :]
    is_lat = blocks < n_lat
    local = jnp.where(is_lat & (k < t) & (jnp.abs(k - q) <= BLK), 0.0, NEG_INF).astype(F32)
    bias = jnp.concatenate([local, jnp.zeros(local.shape[:2] + (s - t,), F32)], axis=2)
    seq_lo, seq_hi = jnp.where(is_lat, 0, t), jnp.where(is_lat, t, s)
    bands, counts = [], []
    for w in POOL_WINDOWS:
        lo, hi = jnp.maximum(q - w // 2, seq_lo), jnp.minimum(q + w - w // 2, seq_hi)
        bands.append((k >= lo) & (k < hi))
        counts.append((hi - lo).astype(F32))
    band = jnp.stack(bands, axis=1).astype(BF16)
    count = jnp.concatenate(counts + [jnp.ones(counts[0].shape[:2] + (BLK - len(counts),), F32)], axis=2)
    return dict(bias=bias, band=band, band_t=band.transpose(0, 1, 3, 2), count=count)


def _case_spec(table, n_lat_blk):
    def kind(i):
        return jnp.where(i < n_lat_blk - 1, jnp.minimum(i, 1), i - n_lat_blk + 3)

    shape = table.shape[1:]
    return pl.BlockSpec((None,) + shape, lambda i: (kind(i),) + (0,) * len(shape))


def _split_dot(band, v):
    return _dot(band, v.astype(BF16))


def _pooled(u_ref, band_ref, cnt_ref, i, ws, gi):
    cols = slice(gi * GROUP, (gi + 1) * GROUP)
    mean = _split_dot(band_ref[gi], u_ref[pl.ds(ws, WIN), cols]) / cnt_ref[:, gi:gi + 1]
    return mean - u_ref[pl.ds(pl.multiple_of(i * QB, QB), QB), cols]


def _head_cols(hd):
    return slice(hd * HEAD, (hd + 1) * HEAD)


def _stack_heads(x, hk, first=0):
    return jnp.concatenate([x[:, first + (Q_GROUP * hk + g) * HEAD:first + (Q_GROUP * hk + g + 1) * HEAD]
                            for g in range(Q_GROUP)], axis=0)


def _biased(scores, bias):
    return (scores.reshape(Q_GROUP, QB, -1) + bias).reshape(Q_GROUP * QB, -1)


def _group_column(vals):
    row = lax.broadcasted_iota(jnp.int32, (Q_GROUP * QB, 1), 0)
    out = jnp.full((Q_GROUP * QB, 1), vals[Q_GROUP - 1], F32)
    for g in range(Q_GROUP - 2, -1, -1):
        out = jnp.where(row < (g + 1) * QB, vals[g], out)
    return out


def _lane_place(cols, width=BLK):
    lane = lax.broadcasted_iota(jnp.int32, (cols[0].shape[0], width), 1)
    out = jnp.zeros((cols[0].shape[0], width), F32)
    for hd, c in enumerate(cols):
        out = jnp.where(lane == hd, c, out)
    return out


def mix_fwd(h, q, k, v, u, w_pool, pool_scale, sink, w_out, mods, tables, layer, t, name, ex=None):
    s = h.shape[0]
    n_lat_blk = t // QB

    def body(h_ref, q_ref, k_ref, v_ref, u_ref, wp_ref, ps_ref, sink_ref, wo_ref, m_ref, bias_ref, band_ref, cnt_ref,
             ho_ref, cat_ref, lse_ref, mo_ref):
        i = pl.program_id(0)
        ws = _window(i, s)
        for gi in range(len(POOL_WINDOWS)):
            mixed = _dot(_pooled(u_ref, band_ref, cnt_ref, i, ws, gi).astype(BF16), wp_ref[gi])
            cat_ref[:, gi * GROUP:(gi + 1) * GROUP] = (mixed * ps_ref[:, gi * GROUP:(gi + 1) * GROUP]).astype(BF16)
        bias = bias_ref[...]
        k_all = jnp.concatenate([k_ref[pl.ds(ws, WIN), :], k_ref[t:s, :]], axis=0)
        v_all = jnp.concatenate([v_ref[pl.ds(ws, WIN), :], v_ref[t:s, :]], axis=0)
        lses = []
        for hk in range(N_HEADS // Q_GROUP):
            kv = _head_cols(hk)
            sc = _biased(_dotg(_stack_heads(q_ref[...], hk), k_all[:, kv], NT), bias)
            sk = _group_column([sink_ref[layer, Q_GROUP * hk + g] for g in range(Q_GROUP)])
            m = jnp.maximum(jnp.max(sc, axis=1, keepdims=True), sk)
            e = jnp.exp(sc - m)
            l = jnp.sum(e, axis=1, keepdims=True) + jnp.exp(sk - m)
            o = _dot(e.astype(BF16), v_all[:, kv]) * (1.0 / l)
            lse = m + jnp.log(l)
            for g in range(Q_GROUP):
                hd = Q_GROUP * hk + g
                cat_ref[:, POOL_W + hd * HEAD:POOL_W + (hd + 1) * HEAD] = o[g * QB:(g + 1) * QB].astype(BF16)
                lses.append(lse[g * QB:(g + 1) * QB])
        lse_ref[...] = _lane_place(lses)
        mo = _dot(cat_ref[...], wo_ref[...])
        mo_ref[...] = mo
        ho_ref[...] = h_ref[...] + m_ref[5:6, :] * mo

    blk = lambda cols: _rows(cols, QB)
    return _grid_call(
        body, name, s // QB,
        [blk(D), blk(ATTN_W), _whole((s, KV_W)), _whole((s, KV_W)), _whole((s, POOL_W)),
         _whole((len(POOL_WINDOWS), GROUP, GROUP), (layer,)), _whole((1, POOL_W), (layer,)),
         pl.BlockSpec(memory_space=pltpu.SMEM), _whole((POOL_W + ATTN_W, D)), _mods_spec(layer, n_lat_blk),
         _case_spec(tables["bias"], n_lat_blk), _case_spec(tables["band"], n_lat_blk), _case_spec(tables["count"], n_lat_blk)],
        [blk(D), blk(POOL_W + ATTN_W), blk(BLK), blk(D)],
        [jax.ShapeDtypeStruct((s, D), F32), jax.ShapeDtypeStruct((s, POOL_W + ATTN_W), BF16), jax.ShapeDtypeStruct((s, BLK), F32),
         jax.ShapeDtypeStruct((s, D), F32)],
        (h, q, k, v, u, w_pool, pool_scale, sink, w_out, mods, tables["bias"], tables["band"], tables["count"]), "parallel", ex)


def mix_bwd(dh, mo, q, k, v, u, lse, w_pool, pool_scale, sink, w_out, mods, tables, layer, t, name, ex=None):
    s = dh.shape[0]
    n_lat_blk = t // QB
    n_grp = len(POOL_WINDOWS)

    def body(dh_ref, mo_ref, q_ref, k_ref, v_ref, u_ref, lse_ref, wp_ref, ps_ref, sink_ref, wo_ref, m_ref,
             bias_ref, band_ref, band_t_ref, cnt_ref,
             dq_ref, dk_ref, dv_ref, du_ref, dmo_ref, dwp_ref, dps_ref, dsink_ref, dm_ref):
        i = pl.program_id(0)

        @pl.when(i == 0)
        def _():
            for ref in (dk_ref, dv_ref, du_ref, dwp_ref, dps_ref, dsink_ref):
                ref[...] = jnp.zeros_like(ref)

        @pl.when((i == 0) | (i == n_lat_blk))
        def _():
            dm_ref[...] = jnp.zeros_like(dm_ref)

        ws = _window(i, s)
        here = pl.ds(pl.multiple_of(i * QB, QB), QB)
        dho = dh_ref[...]
        dm_ref[2:3, :] += _sum0(dho * mo_ref[...])
        dmo = (m_ref[5:6, :] * dho).astype(BF16)
        dmo_ref[...] = dmo
        dcat = _dotg(dmo, wo_ref[...], NT)

        for gi in range(n_grp):
            cols = slice(gi * GROUP, (gi + 1) * GROUP)
            pooled = _pooled(u_ref, band_ref, cnt_ref, i, ws, gi).astype(BF16)
            dpo = dcat[:, cols]
            dps_ref[0:1, cols] += _sum0(dpo * _dot(pooled, wp_ref[gi]))
            dmixed = (dpo * ps_ref[:, cols]).astype(BF16)
            dwp_ref[gi] += _dotg(pooled, dmixed, TN)
            dpooled = _dotg(dmixed, wp_ref[gi], NT)
            du_ref[pl.ds(ws, WIN), cols] += _split_dot(band_t_ref[gi], dpooled / cnt_ref[:, gi:gi + 1])
            du_ref[here, cols] -= dpooled

        bias = bias_ref[...]
        k_all = jnp.concatenate([k_ref[pl.ds(ws, WIN), :], k_ref[t:s, :]], axis=0)
        v_all = jnp.concatenate([v_ref[pl.ds(ws, WIN), :], v_ref[t:s, :]], axis=0)
        qq, lse_all = q_ref[...], lse_ref[...]
        dqs, dsinks, dks, dvs = [], [], [], []
        for hk in range(N_HEADS // Q_GROUP):
            kv = _head_cols(hk)
            q4 = _stack_heads(qq, hk)
            lse = jnp.concatenate([lse_all[:, Q_GROUP * hk + g:Q_GROUP * hk + g + 1] for g in range(Q_GROUP)], axis=0)
            p = jnp.exp(_biased(_dotg(q4, k_all[:, kv], NT), bias) - lse)
            do = _stack_heads(dcat, hk, POOL_W).astype(BF16)
            dp = _dotg(do, v_all[:, kv], NT)
            delta = jnp.sum(p * dp, axis=1, keepdims=True)
            ds = (p * (dp - delta)).astype(BF16)
            sk = _group_column([sink_ref[layer, Q_GROUP * hk + g] for g in range(Q_GROUP)])
            dsk = -jnp.exp(sk - lse) * delta
            dq = _dot(ds, k_all[:, kv])
            for g in range(Q_GROUP):
                dqs.append(dq[g * QB:(g + 1) * QB])
                dsinks.append(_sum0(dsk[g * QB:(g + 1) * QB]))
            dks.append(_dotg(ds, q4, TN))
            dvs.append(_dotg(p.astype(BF16), do, TN))
        dq_ref[...] = jnp.concatenate(dqs, axis=1)
        dk, dv = jnp.concatenate(dks, axis=1), jnp.concatenate(dvs, axis=1)
        dk_ref[pl.ds(ws, WIN), :] += dk[:WIN]
        dv_ref[pl.ds(ws, WIN), :] += dv[:WIN]
        dk_ref[t:s, :] += dk[WIN:]
        dv_ref[t:s, :] += dv[WIN:]
        dsink_ref[0:1, :] += _lane_place(dsinks)

    blk = lambda cols: _rows(cols, QB)
    full = lambda shape: pl.BlockSpec(shape, lambda i: (0,) * len(shape))
    return _grid_call(
        body, name, s // QB,
        [blk(D), blk(D), blk(ATTN_W), _whole((s, KV_W)), _whole((s, KV_W)), _whole((s, POOL_W)),
         blk(BLK), _whole((n_grp, GROUP, GROUP), (layer,)), _whole((1, POOL_W), (layer,)),
         pl.BlockSpec(memory_space=pltpu.SMEM), _whole((POOL_W + ATTN_W, D)), _mods_spec(layer, n_lat_blk)]
        + [_case_spec(tables[key], n_lat_blk) for key in ("bias", "band", "band_t", "count")],
        [blk(ATTN_W), full((s, KV_W)), full((s, KV_W)), full((s, POOL_W)), blk(D),
         full((n_grp, GROUP, GROUP)), full((8, POOL_W)), full((8, BLK)), _acc_spec(n_lat_blk)],
        [jax.ShapeDtypeStruct((s, ATTN_W), F32), jax.ShapeDtypeStruct((s, KV_W), F32),
         jax.ShapeDtypeStruct((s, KV_W), F32), jax.ShapeDtypeStruct((s, POOL_W), F32),
         jax.ShapeDtypeStruct((s, D), BF16), jax.ShapeDtypeStruct((n_grp, GROUP, GROUP), F32),
         jax.ShapeDtypeStruct((8, POOL_W), F32), jax.ShapeDtypeStruct((8, BLK), F32), jax.ShapeDtypeStruct((2, 8, D), F32)],
        (dh, mo, q, k, v, u, lse, w_pool, pool_scale, sink, w_out, mods, tables["bias"], tables["band"], tables["band_t"],
         tables["count"]), "arbitrary", ex)


def loss_head(h, target, g, t, name):
    s = h.shape[0]
    n_lat = t // TM

    def body(h_ref, t_ref, g_ref, dh_ref, acc_ref):
        i = pl.program_id(0)

        @pl.when(i == 0)
        def _():
            acc_ref[...] = jnp.zeros_like(acc_ref)

        @pl.when(i < n_lat)
        def _():
            hh, gg = h_ref[...], g_ref[...]
            r = lax.rsqrt(jnp.mean(hh * hh, axis=-1, keepdims=True) + EPS)
            xhat = hh * r
            err = xhat * gg - t_ref[...]
            dy = err * (1.0 / D)
            dx = dy * gg
            dh_ref[...] = r * (dx - xhat * jnp.mean(dx * xhat, axis=-1, keepdims=True))
            acc_ref[0:1, :] += _sum0(dy * xhat)
            acc_ref[1:2, :] += _sum0(err * err)

        @pl.when(i >= n_lat)
        def _():
            dh_ref[...] = jnp.zeros_like(dh_ref)

    return pl.pallas_call(
        body, name=name, grid=(s // TM,),
        in_specs=[_rows(D), pl.BlockSpec((TM, D), lambda i: (jnp.minimum(i, n_lat - 1), 0)), _whole((1, D))],
        out_specs=[_rows(D), pl.BlockSpec((8, D), lambda i: (0, 0))],
        out_shape=[jax.ShapeDtypeStruct((s, D), F32), jax.ShapeDtypeStruct((8, D), F32)],
        compiler_params=_params("arbitrary"),
    )(h, target, g)


def mod_rows(c_all, w_mod, b_cols, name):
    def body(c_ref, w_ref, b_ref, o_ref):
        cc = c_ref[...]
        o_ref[...] = _dot((cc * jax.nn.sigmoid(cc)).astype(BF16), w_ref[...].astype(BF16)) + b_ref[...]

    return pl.pallas_call(
        body, name=name, grid=(2,),
        in_specs=[pl.BlockSpec((16, D), lambda l: (0, 0)), pl.BlockSpec((None, D, MOD_COLS), lambda l: (l, 0, 0)),
                  pl.BlockSpec((None, 1, MOD_COLS), lambda l: (l, 0, 0))],
        out_specs=pl.BlockSpec((None, 16, MOD_COLS), lambda l: (l, 0, 0)),
        out_shape=jax.ShapeDtypeStruct((2, 16, MOD_COLS), F32),
        compiler_params=_params("parallel"),
    )(c_all, w_mod, b_cols)


def mod_grads(c_all, dmod_cols, w_mod, name):
    def body(c_ref, d_ref, w_ref, dw_ref, dc_ref):
        @pl.when(pl.program_id(0) == 0)
        def _():
            dc_ref[...] = jnp.zeros_like(dc_ref)

        cc = c_ref[...]
        dd = d_ref[...].astype(BF16)
        dw_ref[...] = _dotg((cc * jax.nn.sigmoid(cc)).astype(BF16), dd, TN)
        dc_ref[...] += _dotg(dd, w_ref[...].astype(BF16), NT)

    return pl.pallas_call(
        body, name=name, grid=(2,),
        in_specs=[pl.BlockSpec((16, D), lambda l: (0, 0)), pl.BlockSpec((None, 16, MOD_COLS), lambda l: (l, 0, 0)),
                  pl.BlockSpec((None, D, MOD_COLS), lambda l: (l, 0, 0))],
        out_specs=[pl.BlockSpec((None, D, MOD_COLS), lambda l: (l, 0, 0)), pl.BlockSpec((16, D), lambda l: (0, 0))],
        out_shape=[jax.ShapeDtypeStruct((2, D, MOD_COLS), F32), jax.ShapeDtypeStruct((16, D), F32)],
        compiler_params=_params("arbitrary"),
    )(c_all, dmod_cols, w_mod)


def _row_tile(rows, cols, n_arrays):
    budget = VMEM_LIMIT_BYTES // 4 // (2 * 4 * n_arrays * cols)
    best = None
    for tr in range(16, rows + 1, 16):
        if rows % tr == 0 and tr <= budget:
            best = tr
    return best if best is not None else rows


def elementwise(fn, ins, out_dtypes, name, ex=None):
    rows, cols = ins[0].shape
    tr = _row_tile(rows, cols, len(ins) + len(out_dtypes))

    def body(*refs):
        outs = fn(*[r[...] for r in refs[:len(ins)]])
        for o_ref, o in zip(refs[len(ins):], outs):
            o_ref[...] = o.astype(o_ref.dtype)

    spec = pl.BlockSpec((tr, cols), lambda i: (i, 0))
    outs, got = _grid_call(body, name, rows // tr, [spec] * len(ins), [spec] * len(out_dtypes),
                           [jax.ShapeDtypeStruct((rows, cols), dt) for dt in out_dtypes], ins, "parallel", ex)
    return outs if ex is None else (outs, got)


def _adamw_tile(w, g, m, v):
    m = ADAM_B1 * m + (1.0 - ADAM_B1) * g
    v = ADAM_B2 * v + (1.0 - ADAM_B2) * (g * g)
    m_hat = m / (1.0 - ADAM_B1 ** ADAM_STEP)
    v_hat = v / (1.0 - ADAM_B2 ** ADAM_STEP)
    return -ADAM_LR * (m_hat / (jnp.sqrt(v_hat) + ADAM_EPS) + ADAM_WD * w), m, v


def adamw(w, g, m, v, name, ex=None):
    shape = w.shape
    two_d = (-1, shape[-1]) if w.ndim > 1 else (1, -1)
    outs = elementwise(_adamw_tile, [a.reshape(two_d) for a in (w, g, m, v)], [F32] * 3, name, ex)
    outs, got = outs if ex is not None else (outs, None)
    outs = [o.reshape(shape) for o in outs]
    return outs if ex is None else (outs, got)


def _prefetch_call(body, name, grid, in_specs, out_specs, out_shape, place, args, ex=None):
    if ex is None:
        spec = pltpu.PrefetchScalarGridSpec(num_scalar_prefetch=1, grid=grid, in_specs=in_specs, out_specs=out_specs)
        return pl.pallas_call(body, name=name, grid_spec=spec, out_shape=out_shape,
                              compiler_params=_params(*["parallel"] * len(grid)))(place, *args)
    n_in, n_out, ci, co = len(in_specs), len(out_specs), len(ex["ins"]), len(ex["out_shape"])
    spec = pltpu.PrefetchScalarGridSpec(num_scalar_prefetch=1, grid=grid, in_specs=list(in_specs) + _any(ci),
                                        out_specs=list(out_specs) + _any(co), scratch_shapes=ex["scratch"])
    outs = pl.pallas_call(
        _carrying(body, grid, n_in, n_out, ex, lead=1), name=name, grid_spec=spec, out_shape=list(out_shape) + ex["out_shape"],
        input_output_aliases={1 + n_in + i: n_out + j for i, j in ex["aliases"].items()},
        compiler_params=_params(*["arbitrary"] * len(grid)))(place, *args, *ex["ins"])
    return outs[:n_out], outs[n_out:]


def cast_place(w, layer, place, name):
    _, r, c = w.shape
    tr = _row_tile(r, c, 2)

    def body(p_ref, w_ref, o_ref):
        o_ref[...] = w_ref[...].astype(BF16)

    return _prefetch_call(
        body, name, (r // tr,), [pl.BlockSpec((None, tr, c), lambda i, p: (layer, i, 0))],
        pl.BlockSpec((None, tr, c), lambda i, p: (p[1], i, 0)), jax.ShapeDtypeStruct((N_SLOT, r, c), BF16), place, [w])


def pair_sum(g32, got, place, name, ex=None):
    n_slot, rh, c = got.shape
    tr = _row_tile(rh, c, 4)
    per = rh // tr

    def body(p_ref, a_ref, b_ref, o_ref, o16_ref):
        r = a_ref[...] + b_ref[...].astype(F32)
        o_ref[...] = r
        o16_ref[...] = r.astype(BF16)

    half = pl.BlockSpec((None, tr, c), lambda s, i, p: (s, i, 0))
    return _prefetch_call(
        body, name, (n_slot, per), [pl.BlockSpec((None, tr, c), lambda s, i, p: (s, p[0] * per + i, 0)), half], [half, half],
        [jax.ShapeDtypeStruct(got.shape, F32), jax.ShapeDtypeStruct(got.shape, BF16)], place, [g32, got], ex)


def chip_sum(p32, got, place, name):
    _, rh, c = p32.shape
    tr = _row_tile(rh, c, 5)
    per = rh // tr

    def body(p_ref, m_ref, r0_ref, r1_ref, r2_ref, o_ref):
        o_ref[...] = m_ref[...] + r0_ref[...].astype(F32) + r1_ref[...].astype(F32) + r2_ref[...].astype(F32)

    part = pl.BlockSpec((tr, c), lambda i, p: (i, 0))
    return _prefetch_call(
        body, name, (per,), [pl.BlockSpec((None, tr, c), lambda i, p: (p[1], i, 0)), part, part, part],
        pl.BlockSpec((tr, c), lambda i, p: (p[0] * per + i, 0)), jax.ShapeDtypeStruct((2 * rh, c), F32), place, [p32, *got])


def adamw_layers(w, g0, g1, m, v, name, ex=None):
    _, r, c = w.shape
    tr = _row_tile(r, c, 10)

    def body(w_ref, g0_ref, g1_ref, m_ref, v_ref, g_ref, d_ref, mo_ref, vo_ref):
        g = jnp.where(pl.program_id(0) == 0, g0_ref[...], g1_ref[...])
        g_ref[...] = g
        d_ref[...], mo_ref[...], vo_ref[...] = _adamw_tile(w_ref[...], g, m_ref[...], v_ref[...])

    steps = r // tr
    stacked = pl.BlockSpec((None, tr, c), lambda l, i: (l, i, 0))
    layer0 = pl.BlockSpec((tr, c), lambda l, i: (jnp.where(l == 0, i, steps - 1), 0))
    layer1 = pl.BlockSpec((tr, c), lambda l, i: (jnp.where(l == 0, 0, i), 0))
    outs, got = _grid_call(body, name, (2, steps), [stacked, layer0, layer1, stacked, stacked], [stacked] * 4,
                           [jax.ShapeDtypeStruct(w.shape, F32)] * 4, (w, g0, g1, m, v), "parallel", ex)
    return outs if ex is None else (outs, got)


def sum8(gathered, name):
    def body(*refs):
        n = len(refs) // 2
        for g_ref, o_ref in zip(refs[:n], refs[n:]):
            acc = g_ref[0]
            for dev in range(1, N_DEV):
                acc = acc + g_ref[dev]
            o_ref[...] = acc

    return pl.pallas_call(
        body, name=name,
        out_shape=[jax.ShapeDtypeStruct(a.shape[1:], F32) for a in gathered],
        compiler_params=_params(),
    )(*gathered)


PHASES = ("start", "late", "finish")


def _place():
    return lax.axis_index("x"), lax.axis_index("y"), lax.axis_index("c")


def _any(n):
    return [pl.BlockSpec(memory_space=pl.ANY)] * n


def gather8_exchange(blocks):
    n = len(blocks)

    def copy(outs, sems, ti, k, block, to, src=None):
        dst = outs[ti].at[4 * block[0] + 2 * block[1] + block[2]]
        return pltpu.make_async_remote_copy(src_ref=dst if src is None else src, dst_ref=dst, send_sem=sems[0].at[ti, k],
                                            recv_sem=sems[1].at[ti, k], device_id=to, device_id_type=MESH)

    def first(ins, outs, sems):
        x, y, c = _place()
        local, sent = [], []
        for ti in range(n):
            local.append(pltpu.make_async_copy(ins[ti], outs[ti].at[4 * x + 2 * y + c], sems[2].at[ti]))
            sent.append(copy(outs, sems, ti, 0, (x, y, c), (x, y, 1 - c), src=ins[ti]))
            sent += [copy(outs, sems, ti, 1 + j, (x, y, c), (*chip, c), src=ins[ti]) for j, chip in enumerate(_three_chips(x, y))]
        return local, sent

    def start(ins, outs, sems):
        local, sent = first(ins, outs, sems)
        for cp in local + sent:
            cp.start()

    def passed_on(outs, sems):
        x, y, c = _place()
        return [copy(outs, sems, ti, 4 + j, (*chip, c), (x, y, 1 - c)) for ti in range(n) for j, chip in enumerate(_three_chips(x, y))]

    def late(ins, outs, sems):
        x, y, c = _place()
        on = passed_on(outs, sems)
        for ti in range(n):
            for j, chip in enumerate(_three_chips(x, y)):
                copy(outs, sems, ti, 1 + j, (*chip, c), (x, y, c)).wait_recv()
                on[3 * ti + j].start()

    def finish(ins, outs, sems):
        x, y, c = _place()
        me, sibling = (x, y, c), (x, y, 1 - c)
        local, sent = first(ins, outs, sems)
        for ti in range(n):
            copy(outs, sems, ti, 0, sibling, me).wait_recv()
            for j, chip in enumerate(_three_chips(x, y)):
                copy(outs, sems, ti, 4 + j, (*chip, 1 - c), me).wait_recv()
        for cp in sent + passed_on(outs, sems):
            cp.wait_send()
        for cp in local:
            cp.wait()

    return dict(ins=list(blocks), out_shape=[jax.ShapeDtypeStruct((N_DEV,) + b.shape, b.dtype) for b in blocks], aliases={},
                start=start, late=late, finish=finish,
                scratch=[pltpu.SemaphoreType.DMA((n, 7)), pltpu.SemaphoreType.DMA((n, 7)), pltpu.SemaphoreType.DMA((n,))])


def all_gather(blocks, name):
    return run_exchange(gather8_exchange(blocks), name)


def _three_chips(x, y):
    return [(1 - x, y), (x, 1 - y), (1 - x, 1 - y)]


def gather_exchange(placed):
    n = len(placed)

    def copy(bufs, sems, ti, k, chip, core, to):
        rh = bufs[ti].shape[1] // 2
        half = bufs[ti].at[2 * chip[0] + chip[1], pl.ds(core * rh, rh), :]
        return pltpu.make_async_remote_copy(src_ref=half, dst_ref=half, send_sem=sems[0].at[ti, k], recv_sem=sems[1].at[ti, k],
                                            device_id=to, device_id_type=MESH)

    def sends(bufs, sems):
        x, y, c = _place()
        return [copy(bufs, sems, ti, k, (x, y), c, (*chip, c)) for ti in range(n) for k, chip in enumerate(_three_chips(x, y))]

    def passed_on(bufs, sems):
        x, y, c = _place()
        return [copy(bufs, sems, ti, 3 + k, chip, c, (x, y, 1 - c)) for ti in range(n) for k, chip in enumerate(_three_chips(x, y))]

    def start(ins, bufs, sems):
        for cp in sends(bufs, sems):
            cp.start()

    def late(ins, bufs, sems):
        x, y, c = _place()
        on = passed_on(bufs, sems)
        for ti in range(n):
            for k, chip in enumerate(_three_chips(x, y)):
                copy(bufs, sems, ti, k, chip, c, (x, y, c)).wait_recv()
                on[3 * ti + k].start()

    def finish(ins, bufs, sems):
        x, y, c = _place()
        for ti in range(n):
            for k, chip in enumerate(_three_chips(x, y)):
                copy(bufs, sems, ti, 3 + k, chip, 1 - c, (x, y, c)).wait_recv()
        for cp in sends(bufs, sems) + passed_on(bufs, sems):
            cp.wait_send()

    return dict(ins=list(placed), out_shape=[jax.ShapeDtypeStruct(w.shape, w.dtype) for w in placed],
                aliases={i: i for i in range(n)}, start=start, late=late, finish=finish,
                scratch=[pltpu.SemaphoreType.DMA((n, 6)), pltpu.SemaphoreType.DMA((n, 6))])


def scatter_exchange(p16):
    n = len(p16)

    def copies(ins, got, sems):
        x, y, c = _place()
        return [pltpu.make_async_remote_copy(src_ref=ins[ti].at[2 * chip[0] + chip[1]], dst_ref=got[3 * ti + k],
                                             send_sem=sems[0].at[ti, k], recv_sem=sems[1].at[ti, k], device_id=(*chip, c),
                                             device_id_type=MESH)
                for ti in range(n) for k, chip in enumerate(_three_chips(x, y))]

    def start(ins, got, sems):
        for cp in copies(ins, got, sems):
            cp.start()

    def finish(ins, got, sems):
        for cp in copies(ins, got, sems):
            cp.wait()

    return dict(ins=list(p16), out_shape=[jax.ShapeDtypeStruct(a.shape[1:], BF16) for a in p16 for _ in range(3)], aliases={},
                start=start, finish=finish, scratch=[pltpu.SemaphoreType.DMA((n, 3)), pltpu.SemaphoreType.DMA((n, 3))])


def run_exchange(ex, name):
    ci, co = len(ex["ins"]), len(ex["out_shape"])

    def body(*refs):
        ins, outs, sems = refs[:ci], refs[ci:ci + co], refs[ci + co:]
        for phase in PHASES:
            if phase in ex:
                ex[phase](ins, outs, sems)

    return pl.pallas_call(body, name=name, in_specs=_any(ci), out_specs=_any(co), out_shape=ex["out_shape"],
                          input_output_aliases=ex["aliases"], scratch_shapes=ex["scratch"])(*ex["ins"])


def _carrying(body, grid, n_in, n_out, ex, lead=0):
    ci, co = len(ex["ins"]), len(ex["out_shape"])
    first, last = (0,) * len(grid), tuple(g - 1 for g in grid)
    steps = dict(start=first, late=(grid[0] - 2,) if len(grid) == 1 and grid[0] > 2 else last, finish=last)

    def at(ids):
        return functools.reduce(jnp.logical_and, [pl.program_id(ax) == v for ax, v in enumerate(ids)])

    def carrying(*refs):
        head, refs = refs[:lead], refs[lead:]
        c_in, c_out = refs[n_in:n_in + ci], refs[n_in + ci + n_out:n_in + ci + n_out + co]
        sems = refs[n_in + ci + n_out + co:]
        for phase in PHASES:
            if phase == "finish":
                body(*head, *refs[:n_in], *refs[n_in + ci:n_in + ci + n_out])
            if phase in ex:
                pl.when(at(steps[phase]))(functools.partial(ex[phase], c_in, c_out, sems))

    return carrying


def _grid_call(body, name, grid, in_specs, out_specs, out_shape, args, sem, ex=None):
    grid = (grid,) if isinstance(grid, int) else tuple(grid)
    sems_of = (sem,) * len(grid) if isinstance(sem, str) else tuple(sem)
    n_in, n_out = len(in_specs), len(out_specs)
    if ex is None:
        return pl.pallas_call(body, name=name, grid=grid, in_specs=in_specs, out_specs=out_specs, out_shape=out_shape,
                              compiler_params=_params(*sems_of))(*args), []
    ci, co = len(ex["ins"]), len(ex["out_shape"])
    outs = pl.pallas_call(
        _carrying(body, grid, n_in, n_out, ex), name=name, grid=grid, in_specs=list(in_specs) + _any(ci),
        out_specs=list(out_specs) + _any(co), out_shape=list(out_shape) + ex["out_shape"], scratch_shapes=ex["scratch"],
        input_output_aliases={n_in + i: n_out + j for i, j in ex["aliases"].items()},
        compiler_params=_params(*["arbitrary"] * len(grid)),
    )(*args, *ex["ins"])
    return outs[:n_out], outs[n_out:]


def both(*exchanges):
    exchanges = [ex for ex in exchanges if ex is not None]
    if len(exchanges) < 2:
        return exchanges[0] if exchanges else None
    n_ins = [len(ex["ins"]) for ex in exchanges]
    n_outs = [len(ex["out_shape"]) for ex in exchanges]
    n_sems = [len(ex["scratch"]) for ex in exchanges]

    def parts(seq, counts, k):
        first = sum(counts[:k])
        return seq[first:first + counts[k]]

    def run(phase):
        def go(ins, outs, sems):
            for k, ex in enumerate(exchanges):
                if phase in ex:
                    ex[phase](parts(ins, n_ins, k), parts(outs, n_outs, k), parts(sems, n_sems, k))
        return go

    aliases = {sum(n_ins[:k]) + i: sum(n_outs[:k]) + j for k, ex in enumerate(exchanges) for i, j in ex["aliases"].items()}
    return dict(ins=[a for ex in exchanges for a in ex["ins"]], out_shape=[o for ex in exchanges for o in ex["out_shape"]],
                aliases=aliases, scratch=[s for ex in exchanges for s in ex["scratch"]], **{ph: run(ph) for ph in PHASES})


def split_outputs(got, *exchanges):
    got, out = list(got), []
    for ex in exchanges:
        n = len(ex["out_shape"]) if ex is not None else 0
        out.append(got[:n])
        got = got[n:]
    return out


def pair_exchange(g16):
    n = len(g16)

    def copies(a16, got, sems):
        x, y, c = _place()
        out = []
        for ti in range(n):
            rh = a16[ti].shape[1] // 2
            out.append(pltpu.make_async_remote_copy(
                src_ref=a16[ti].at[:, pl.ds((1 - c) * rh, rh), :], dst_ref=got[ti], send_sem=sems[0].at[ti],
                recv_sem=sems[1].at[ti], device_id=(x, y, 1 - c), device_id_type=MESH))
        return out

    def start(a16, got, sems):
        for cp in copies(a16, got, sems):
            cp.start()

    def finish(a16, got, sems):
        for cp in copies(a16, got, sems):
            cp.wait()

    return dict(ins=list(g16), out_shape=[jax.ShapeDtypeStruct((a.shape[0], a.shape[1] // 2, a.shape[2]), BF16) for a in g16],
                aliases={}, start=start, finish=finish, scratch=[pltpu.SemaphoreType.DMA((n,)), pltpu.SemaphoreType.DMA((n,))])


def _scatter_copies(src_ref, lands, send_sems, recv_sems):
    x, y, c = _place()
    return [pltpu.make_async_remote_copy(src_ref=src_ref.at[2 * chip[0] + chip[1]], dst_ref=lands[k], send_sem=send_sems.at[k],
                                         recv_sem=recv_sems.at[k], device_id=(*chip, c), device_id_type=MESH)
            for k, chip in enumerate(_three_chips(x, y))]


def scatter_start(p16, name):
    hbm, sem = pl.BlockSpec(memory_space=pltpu.HBM), pl.BlockSpec(memory_space=pltpu.SEMAPHORE)

    def body(src_ref, l0_ref, l1_ref, l2_ref, send_sems, recv_sems, src_thru, o0_ref, o1_ref, o2_ref, token_ref):
        for cp in _scatter_copies(src_ref, (l0_ref, l1_ref, l2_ref), send_sems, recv_sems):
            cp.start()
        token_ref[...] = jnp.zeros_like(token_ref)

    land = [pltpu.with_memory_space_constraint(lax.empty(p16.shape[1:], BF16), pltpu.HBM) for _ in range(3)]
    return pl.pallas_call(
        body, name=name,
        out_shape=(pltpu.SemaphoreType.DMA((3,)), pltpu.SemaphoreType.DMA((3,)), pltpu.HBM(p16.shape, BF16),
                   *[pltpu.HBM(p16.shape[1:], BF16)] * 3, jax.ShapeDtypeStruct((8, BLK), F32)),
        in_specs=(hbm,) * 4, out_specs=(sem, sem, hbm, hbm, hbm, hbm, pl.BlockSpec(memory_space=pltpu.VMEM)),
        input_output_aliases={0: 2, 1: 3, 2: 4, 3: 5},
        compiler_params=pltpu.CompilerParams(has_side_effects=pltpu.SideEffectType.DATAFLOW_SIDE_EFFECTING),
    )(pltpu.with_memory_space_constraint(p16, pltpu.HBM), *land)


def scatter_wait(send_sems, recv_sems, src_thru, lands, after, name):
    hbm, sem = pl.BlockSpec(memory_space=pltpu.HBM), pl.BlockSpec(memory_space=pltpu.SEMAPHORE)

    def body(src_ref, l0_ref, l1_ref, l2_ref, send_sems, recv_sems, after_ref, src_dead, g0_ref, g1_ref, g2_ref):
        for cp in _scatter_copies(src_ref, (l0_ref, l1_ref, l2_ref), send_sems, recv_sems):
            cp.wait_send()
            cp.wait_recv()

    return pl.pallas_call(
        body, name=name, out_shape=(pltpu.HBM(src_thru.shape, BF16), *[pltpu.HBM(lands[0].shape, BF16)] * 3),
        in_specs=(hbm, hbm, hbm, hbm, sem, sem, pl.BlockSpec(memory_space=pl.ANY)), out_specs=(hbm,) * 4,
        input_output_aliases={0: 0, 1: 1, 2: 2, 3: 3},
        compiler_params=pltpu.CompilerParams(has_side_effects=pltpu.SideEffectType.DATAFLOW_SIDE_EFFECTING),
    )(src_thru, *lands, send_sems, recv_sems, after)[1:]


def pair_fill_exchange(halves):
    n = len(halves)

    def copies(bufs, sems, core):
        x, y, c = _place()
        out = []
        for ti in range(n):
            rh = bufs[ti].shape[0] // 2
            rows = bufs[ti].at[pl.ds((c if core == "mine" else 1 - c) * rh, rh), :]
            out.append(pltpu.make_async_remote_copy(src_ref=rows, dst_ref=rows, send_sem=sems[0].at[ti], recv_sem=sems[1].at[ti],
                                                    device_id=(x, y, 1 - c), device_id_type=MESH))
        return out

    def start(ins, bufs, sems):
        for cp in copies(bufs, sems, "mine"):
            cp.start()

    def finish(ins, bufs, sems):
        for cp in copies(bufs, sems, "mine"):
            cp.wait_send()
        for cp in copies(bufs, sems, "sibling's"):
            cp.wait_recv()

    return dict(ins=list(halves), out_shape=[jax.ShapeDtypeStruct(a.shape, a.dtype) for a in halves],
                aliases={i: i for i in range(n)}, start=start, finish=finish,
                scratch=[pltpu.SemaphoreType.DMA((n,)), pltpu.SemaphoreType.DMA((n,))])


def pair_gather(halves, name):
    return run_exchange(pair_fill_exchange(halves), name)


def reduce_small(dm_f1, dm_mix, dm_gate, dm_f2, loss_blk, name):
    def body(f1_ref, mix_ref, gate_ref, f2_ref, l_ref, tot_ref, rows_ref, fin_ref):
        rows_ref[...] = jnp.zeros_like(rows_ref)
        tot_ref[...] = jnp.zeros_like(tot_ref)
        mod_src = [(f1_ref, 0), (f1_ref, 1), (f1_ref, 2), (mix_ref, 0), (mix_ref, 1), (gate_ref, 2),
                   (f2_ref, 0), (f2_ref, 1), (f2_ref, 2)]
        norm_src = [(f1_ref, 3), (mix_ref, 3), (f2_ref, 3)]
        for l in range(2):
            for k, (ref, r) in enumerate(mod_src + norm_src):
                lat = ref[0, l, 0, r:r + 1, :]
                ctx = ref[0, l, 1, r:r + 1, :]
                for dev in range(N_DEV):
                    if dev:
                        lat = lat + ref[dev, l, 0, r:r + 1, :]
                        ctx = ctx + ref[dev, l, 1, r:r + 1, :]
                    if k < N_MOD:
                        rows_ref[l, dev, k:k + 1, :] = ref[dev, l, 0, r:r + 1, :]
                if k < N_MOD:
                    rows_ref[l, N_DEV, k:k + 1, :] = ctx
                tot_ref[l, k:k + 1, :] = lat + ctx
        acc = l_ref[0]
        for dev in range(1, N_DEV):
            acc = acc + l_ref[dev]
        loss = (0.5 / D) * jnp.sum(acc[1:2, :], axis=1, keepdims=True)
        row = lax.broadcasted_iota(jnp.int32, (8, D), 0)
        fin_ref[...] = jnp.where(row == 0, acc[0:1, :], loss)

    return pl.pallas_call(
        body, name=name,
        out_shape=[jax.ShapeDtypeStruct((2, 16, D), F32), jax.ShapeDtypeStruct((2, 16, 16, D), F32),
                   jax.ShapeDtypeStruct((8, D), F32)],
        compiler_params=_params(),
    )(dm_f1, dm_mix, dm_gate, dm_f2, loss_blk)


def rope_tables(t, s):
    rows = t // GRID_W
    row = jnp.repeat(jnp.arange(rows), GRID_W).astype(F32)
    col = jnp.tile(jnp.arange(GRID_W), rows).astype(F32)
    inv = ROPE_BASE ** (-jnp.arange(0, HEAD // 2, 2, dtype=F32) / (HEAD // 2))
    ang = jnp.concatenate([row[:, None] * inv, col[:, None] * inv], axis=-1)
    cos, sin = jnp.cos(ang), jnp.sin(ang)
    cos = jnp.concatenate([jnp.tile(cos, (1, 4)), jnp.ones((s - t, BLK), F32)], axis=0)
    sin = jnp.concatenate([jnp.tile(jnp.concatenate([-sin, sin], axis=1), (1, 2)), jnp.zeros((s - t, BLK), F32)], axis=0)
    return cos, sin


BIG = ("ffn1_in", "ffn1_out", "w_in", "w_out", "ffn2_in", "ffn2_out")
GROUPS = dict(ffn1=("ffn1_in", "ffn1_out"), mix=("w_in", "w_out"), ffn2=("ffn2_in", "ffn2_out"))
GATHER_BEHIND = {("ffn1", 0): [("w_in", 0), ("ffn2_out", 0), ("ffn1_out", 1)], ("proj", 0): [("w_out", 0)],
                 ("mix", 0): [("ffn2_in", 0)], ("ffn2", 0): [("ffn1_in", 1), ("w_in", 1)],
                 ("ffn1", 1): [("ffn2_in", 1), ("w_out", 1)], ("mix", 1): [("ffn2_out", 1)]}


def _slot_major(name, g):
    if name == "w_in":
        return jnp.stack(jnp.split(g, N_SLOT, axis=1), axis=0)
    if name in ("ffn1_in", "ffn2_in"):
        return g
    return g.reshape(N_SLOT, g.shape[0] // N_SLOT, g.shape[1])


def _whole_weight(name, buf):
    if name == "w_in":
        return buf.transpose(1, 0, 2).reshape(D, PROJ_W)
    if name in ("ffn1_in", "ffn2_in"):
        return buf
    return buf.reshape(-1, buf.shape[2])


def local_step(x1, ctx1, target, mods, norms, nfinal, placed, w_pool, pool_scale, sink, place, small_blocks):
    t, s = x1.shape[0], x1.shape[0] + ctx1.shape[0]
    n_lat = t // TM
    cos, sin = rope_tables(t, s)
    tables = mix_tables(t, s)
    wts ={name: list(pair) for name, pair in placed.items()}

    def gather(tensors):
        return gather_exchange([wts[name][l] for name, l in tensors])

    def gathered(tensors, arrays):
        for (name, l), whole in zip(tensors, arrays):
            wts[name][l] = whole

    def weight(name, l):
        return _whole_weight(name, wts[name][l])

    def fwd_ex(grp, l):
        groups = GATHER_BEHIND.get((grp, l))
        return (groups, gather(groups)) if groups else (None, None)

    first = [("ffn1_in", 0), ("ffn1_out", 0)]
    gathered(first, run_exchange(gather(first), "gather_first"))
    h = jnp.concatenate([x1, ctx1], axis=0)
    saved = []
    for l in range(2):
        h0 = h
        groups, ex = fwd_ex("ffn1", l)
        (h1, ab1, f1), got = ffn_fwd(h0, mods, norms[0], weight("ffn1_in", l), weight("ffn1_out", l), l, 0, n_lat, f"ffn1_fwd_{l}", ex)
        gathered(groups or [], got)
        groups, ex = fwd_ex("proj", l)
        (u, q, k, v), got = proj_fwd(h1, mods, norms[1], weight("w_in", l), cos, sin, l, n_lat, f"proj_fwd_{l}", ex)
        gathered(groups or [], got)
        groups, ex = fwd_ex("mix", l)
        (h2, cat, lse, mo), got = mix_fwd(h1, q, k, v, u, w_pool, pool_scale, sink, weight("w_out", l), mods, tables, l, t,
                                          f"mix_fwd_{l}", ex)
        gathered(groups or [], got)
        groups, ex = fwd_ex("ffn2", l)
        (h, ab2, f2), got = ffn_fwd(h2, mods, norms[2], weight("ffn2_in", l), weight("ffn2_out", l), l, 6, n_lat, f"ffn2_fwd_{l}", ex)
        gathered(groups or [], got)
        saved.append((h0, ab1, f1, h1, u, q, k, v, cat, lse, mo, h2, ab2, f2))
    dh, loss_blk = loss_head(h, target, nfinal, t, "loss_head")

    halves = {name: [None, None] for name in BIG}
    pending = []

    def summed_in_pair(grp, l, name_a, g_a, name_b, wgrad_b):
        g_b, got_a = wgrad_b(pair_exchange([_slot_major(name_a, g_a[1])]))
        sum_a, got_b = pair_sum(_slot_major(name_a, g_a[0]), got_a[0], place, f"pair_sum_{name_a}_{l}",
                                pair_exchange([_slot_major(name_b, g_b[1])]))
        sums = {name_a: sum_a, name_b: pair_sum(_slot_major(name_b, g_b[0]), got_b[0], place, f"pair_sum_{name_b}_{l}")}
        pending.append((grp, l, [sums[n] for n in GROUPS[grp]]))

    lacking = []

    def riders():
        return (scatter_exchange([p16 for _, p16 in pending[0][2]]) if pending else None,
                pair_fill_exchange([halves[name][l] for name, l in lacking]) if lacking else None)

    def carried(got, exs):
        got, filled = split_outputs(got, *exs)
        for (name, l), whole in zip(list(lacking), filled):
            halves[name][l] = whole
            lacking.remove((name, l))
        if pending:
            grp, l, pairs = pending.pop(0)
            for i, name in enumerate(GROUPS[grp]):
                halves[name][l] = chip_sum(pairs[i][0], got[3 * i:3 * i + 3], place, f"chip_sum_{name}_{l}")
                lacking.append((name, l))

    small = [None, None]
    for l in (1, 0):
        h0, ab1, f1, h1, u, q, k, v, cat, lse, mo, h2, ab2, f2 = saved[l]
        exs = riders()
        (dh, dab, df, n, act, dm_f2), got = ffn_bwd(h2, ab2, f2, dh, mods, norms[2], weight("ffn2_in", l), weight("ffn2_out", l),
                                                    l, 6, n_lat, f"ffn2_bwd_{l}", both(*exs))
        carried(got, exs)
        g_in, _ = wgrad(n, dab, D, FF_COLS, FF_COLS, f"ffn2_in_wgrad_{l}")
        summed_in_pair("ffn2", l, "ffn2_in", g_in, "ffn2_out",
                       lambda ex, a=act, b=df: wgrad(a, b, D_FF // 2, D, None, f"ffn2_out_wgrad_{l}", ex))
        exs = riders()
        (dq, dk, dv, du, dmo, dwp, dps, dsink, dm_gate), got = mix_bwd(
            dh, mo, q, k, v, u, lse, w_pool, pool_scale, sink, weight("w_out", l), mods, tables, l, t, f"mix_bwd_{l}", both(*exs))
        carried(got, exs)
        g_wo, _ = wgrad(cat, dmo, POOL_W + ATTN_W, D, None, f"w_out_wgrad_{l}")
        dh, dp, n, dm_mix = proj_bwd(h1, du, dq, dk, dv, dh, mods, norms[1], weight("w_in", l), cos, sin, l, n_lat, f"proj_bwd_{l}")
        summed_in_pair("mix", l, "w_out", g_wo, "w_in",
                       lambda ex, a=n, b=dp: wgrad(a, b, D, PROJ_W // 2, None, f"w_in_wgrad_{l}", ex))
        exs = riders()
        (dh, dab, df, n, act, dm_f1), got = ffn_bwd(h0, ab1, f1, dh, mods, norms[0], weight("ffn1_in", l), weight("ffn1_out", l),
                                                    l, 0, n_lat, f"ffn1_bwd_{l}", both(*exs))
        carried(got, exs)
        small[l] = dict(dm_f1=dm_f1, dm_mix=dm_mix, dm_gate=dm_gate, dm_f2=dm_f2, dwp=dwp, dps=dps, dsink=dsink)
        if l:
            g_in, _ = wgrad(n, dab, D, FF_COLS, FF_COLS, f"ffn1_in_wgrad_{l}")
            summed_in_pair("ffn1", l, "ffn1_in", g_in, "ffn1_out",
                           lambda ex, a=act, b=df: wgrad(a, b, D_FF // 2, D, None, f"ffn1_out_wgrad_{l}", ex))
    g_out, _ = wgrad(act, df, D_FF // 2, D, None, "ffn1_out_wgrad_0")
    got = run_exchange(pair_exchange([_slot_major("ffn1_out", g_out[1])]), "pair_exchange_ffn1_out_0")
    p32, p16 = pair_sum(_slot_major("ffn1_out", g_out[0]), got[0], place, "pair_sum_ffn1_out_0")
    riding = (scatter_exchange([p16]), gather8_exchange(small_blocks(small, loss_blk)),
              pair_fill_exchange([halves[name][l] for name, l in lacking]))
    g_in, got = wgrad(n, dab, D, FF_COLS, FF_COLS, "ffn1_in_wgrad_0", both(*riding))
    got, small_all, filled = split_outputs(got, *riding)
    for (name, l), whole in zip(lacking, filled):
        halves[name][l] = whole
    (halves["ffn1_out"][0],) = pair_gather([chip_sum(p32, got, place, "chip_sum_ffn1_out_0")], "pair_gather_ffn1_out_0")
    got = run_exchange(pair_exchange([_slot_major("ffn1_in", g_in[1])]), "pair_exchange_ffn1_in_0")
    return dh[:t], halves, pair_sum(_slot_major("ffn1_in", g_in[0]), got[0], place, "pair_sum_ffn1_in_0"), small_all


def _silu_grad(z):
    sg = jax.nn.sigmoid(z)
    return sg * (1 + z * (1 - sg))


def kernel(x, c, ctx, c_ctx, w_mod, b_mod, norm_ffn1, w_ffn1_in, w_ffn1_out, norm_mix, w_in, w_pool, pool_scale, sink, w_out, norm_ffn2, w_ffn2_in, w_ffn2_out, norm_final, loss_target, m_c_ctx, m_w_mod, m_b_mod, m_norm_ffn1, m_w_ffn1_in, m_w_ffn1_out, m_norm_mix, m_w_in, m_w_pool, m_pool_scale, m_sink, m_w_out, m_norm_ffn2, m_w_ffn2_in, m_w_ffn2_out, m_norm_final, v_c_ctx, v_w_mod, v_b_mod, v_norm_ffn1, v_w_ffn1_in, v_w_ffn1_out, v_norm_mix, v_w_in, v_w_pool, v_pool_scale, v_sink, v_w_out, v_norm_ffn2, v_w_ffn2_in, v_w_ffn2_out, v_norm_final):
    px, py, pc = _place()
    slot, me = 2 * px + py, 4 * px + 2 * py + pc
    n_grp = len(POOL_WINDOWS)

    (c_rows,) = all_gather([c.reshape(8, D // 8)], "gather_c")
    c_all = jnp.concatenate([c_rows.reshape(N_DEV, D), c_ctx.reshape(1, D), jnp.zeros((16 - N_DEV - 1, D), F32)], axis=0)
    b_cols = lax.dynamic_slice(b_mod, (0, slot * MOD_COLS), (2, MOD_COLS)).reshape(2, 1, MOD_COLS)
    (mod_parts,) = all_gather([mod_rows(c_all, w_mod, b_cols, "mod_rows")], "gather_mods")
    mods_all = mod_parts[0::2].transpose(1, 2, 0, 3).reshape(2, 16, N_MOD * D)
    mx = lax.dynamic_slice(mods_all, (0, me, 0), (2, 1, N_MOD * D)).reshape(2, N_MOD, D)
    mc = mods_all[:, N_DEV].reshape(2, N_MOD, D)
    pad = jnp.zeros((2, 16 - N_MOD, D), F32)
    mods = jnp.stack([jnp.concatenate([mx, pad], axis=1), jnp.concatenate([mc, pad], axis=1)], axis=1)

    place = jnp.stack([pc, slot]).astype(jnp.int32)
    shards = dict(ffn1_in=w_ffn1_in, ffn1_out=w_ffn1_out, w_in=w_in, w_out=w_out, ffn2_in=w_ffn2_in, ffn2_out=w_ffn2_out)
    placed = {name: [cast_place(shards[name], l, place, f"cast_{name}_{l}") for l in range(2)] for name in BIG}
    norms = [g.reshape(2, 1, D) for g in (norm_ffn1, norm_mix, norm_ffn2)]
    row_sums = ("dm_f1", "dm_mix", "dm_gate", "dm_f2")

    def small_blocks(small, loss_blk):
        stacked = {k: jnp.stack([small[0][k], small[1][k]]) for k in row_sums + ("dwp", "dps", "dsink")}
        return ([stacked[k].reshape(32, D) for k in row_sums]
                + [stacked["dwp"].reshape(2 * n_grp * GROUP, GROUP), stacked["dps"].reshape(16, POOL_W),
                   stacked["dsink"].reshape(16, BLK), loss_blk])

    dx, halves, last_pair, small_all = local_step(x[0], ctx[0], loss_target[0], mods, norms, norm_final.reshape(1, D), placed,
                                                   w_pool.astype(BF16), pool_scale.reshape(2, 1, POOL_W), sink, place, small_blocks)
    grads = {}

    *g_dm, g_dwp, g_dps, g_dsink, g_loss = small_all
    tot, rows, fin = reduce_small(*[g.reshape(N_DEV, 2, 2, 8, D) for g in g_dm], g_loss, "reduce_small")
    s_dwp, s_dps, s_dsink = sum8([g_dwp, g_dps, g_dsink], "sum_pool_sink")
    grads.update(
        w_pool=s_dwp.reshape(2, n_grp, GROUP, GROUP), pool_scale=s_dps.reshape(2, 8, POOL_W)[:, 0],
        sink=s_dsink.reshape(2, 8, BLK)[:, 0, :N_HEADS], b_mod=tot[:, :N_MOD].reshape(2, N_MOD * D),
        norm_ffn1=tot[:, N_MOD], norm_mix=tot[:, N_MOD + 1], norm_ffn2=tot[:, N_MOD + 2], norm_final=fin[0])
    loss = fin[1, 0]

    dmod_cols = lax.dynamic_slice(rows[:, :, :N_MOD, :].reshape(2, 16, N_MOD * D), (0, 0, slot * MOD_COLS), (2, 16, MOD_COLS))
    grads["w_mod"], dc = mod_grads(c_all, dmod_cols, w_mod, "mod_grads")
    (g_dc,) = all_gather([dc], "gather_dc")
    (s_dc,) = sum8([g_dc], "sum_dc")
    (d_c_ctx,) = elementwise(lambda d, z: (0.5 * d * _silu_grad(z),), [s_dc[N_DEV:N_DEV + 1], c_ctx.reshape(1, D)], [F32], "c_ctx_grad")
    send_sems, recv_sems, src_thru, *lands, token = scatter_start(last_pair[1], "scatter_last_start")
    grads["c_ctx"] = d_c_ctx.reshape(D) + token[0, :1]

    given = dict(c_ctx=(c_ctx, m_c_ctx, v_c_ctx), w_mod=(w_mod, m_w_mod, v_w_mod), b_mod=(b_mod, m_b_mod, v_b_mod),
                 norm_ffn1=(norm_ffn1, m_norm_ffn1, v_norm_ffn1), w_ffn1_in=(w_ffn1_in, m_w_ffn1_in, v_w_ffn1_in),
                 w_ffn1_out=(w_ffn1_out, m_w_ffn1_out, v_w_ffn1_out), norm_mix=(norm_mix, m_norm_mix, v_norm_mix),
                 w_in=(w_in, m_w_in, v_w_in), w_pool=(w_pool, m_w_pool, v_w_pool),
                 pool_scale=(pool_scale, m_pool_scale, v_pool_scale), sink=(sink, m_sink, v_sink), w_out=(w_out, m_w_out, v_w_out),
                 norm_ffn2=(norm_ffn2, m_norm_ffn2, v_norm_ffn2), w_ffn2_in=(w_ffn2_in, m_w_ffn2_in, v_w_ffn2_in),
                 w_ffn2_out=(w_ffn2_out, m_w_ffn2_out, v_w_ffn2_out), norm_final=(norm_final, m_norm_final, v_norm_final))
    shard = {(name, l): halves[name][l] for name in BIG for l in range(2)}

    def update(name):
        w, m, v = given[name]
        if name in BIG or name[2:] in BIG:
            key = name if name in BIG else name[2:]
            return adamw_layers(w, shard[key, 0], shard[key, 1], m, v, f"adamw_{name}")
        return [grads[name], *adamw(w, grads[name], m, v, f"adamw_{name}")]

    done = {name: update(name) for name in given if name != "w_ffn1_in"}
    got = scatter_wait(send_sems, recv_sems, src_thru, lands, done["w_ffn2_out"][3], "scatter_last_wait")
    (shard["ffn1_in", 0],) = pair_gather([chip_sum(last_pair[0], got, place, "chip_sum_ffn1_in_0")], "grad_pair_gather_last")
    done["w_ffn1_in"] = update("w_ffn1_in")
    return (loss, dx[None], *[done[name][i] for i in range(4) for name in given])
```

```python
import functools

import jax
import jax.numpy as jnp
from jax import lax
from jax.experimental import pallas as pl
from jax.experimental.pallas import tpu as pltpu

F32, BF16 = jnp.float32, jnp.bfloat16
D = 1024
D_FF = 2816
N_SLOT = 4
FF_COLS = 2 * D_FF // N_SLOT
N_MOD = 9
MOD_COLS = N_MOD * D // N_SLOT
POOL_W, ATTN_W, KV_W = 512, 512, 128
PROJ_W = POOL_W + ATTN_W + 2 * KV_W
N_HEADS, Q_GROUP, HEAD = 8, 4, 64
GROUP = 128
POOL_WINDOWS = (2, 4, 8, 16)
BLK = 128
QB = 256
WIN = QB + 2 * BLK
GRID_W = 64
ROPE_BASE = 10000.0
EPS = 1e-6
NEG_INF = -1e30
TM = 256
N_DEV = 8
VMEM_LIMIT_BYTES = 56 * 1024 * 1024
ADAM_LR, ADAM_B1, ADAM_B2, ADAM_EPS, ADAM_WD, ADAM_STEP = 0.001, 0.9, 0.999, 1e-08, 0.01, 10
MESH = pl.DeviceIdType.MESH
NT = (((1,), (1,)), ((), ()))
TN = (((0,), (0,)), ((), ()))


def _params(*sem):
    return pltpu.CompilerParams(dimension_semantics=sem, vmem_limit_bytes=VMEM_LIMIT_BYTES)


def _whole(shape, lead=()):
    idx = tuple(lead) + (0,) * len(shape)
    return pl.BlockSpec((None,) * len(lead) + tuple(shape), lambda *_: idx, pipeline_mode=pl.Buffered(1))


def _rows(cols, tm=TM):
    return pl.BlockSpec((tm, cols), lambda i: (i, 0))


def _mods_spec(layer, n_lat):
    return pl.BlockSpec((None, None, 16, D), lambda i: (layer, (i >= n_lat).astype(jnp.int32), 0, 0))


def _acc_spec(n_lat):
    return pl.BlockSpec((None, 8, D), lambda i: ((i >= n_lat).astype(jnp.int32), 0, 0))


def _dot(a, b):
    return jnp.dot(a, b, preferred_element_type=F32)


def _dotg(a, b, dims):
    return lax.dot_general(a, b, dims, preferred_element_type=F32)


def _sum0(v):
    return jnp.sum(v, axis=0, keepdims=True)


def _norm_mod(h, g, shift, scale):
    r = lax.rsqrt(jnp.mean(h * h, axis=-1, keepdims=True) + EPS)
    xhat = h * r
    y = xhat * g
    return y * (1 + scale) + shift, xhat, r, y


def _norm_mod_bwd(dn, xhat, r, y, g, scale):
    dy = dn * (1 + scale)
    dx = dy * g
    dh = r * (dx - xhat * jnp.mean(dx * xhat, axis=-1, keepdims=True))
    return _sum0(dn), _sum0(dn * y), _sum0(dy * xhat), dh


def _swap_halves(v):
    w = v.shape[1]
    lane = lax.broadcasted_iota(jnp.int32, v.shape, 1)
    return jnp.where(lane % HEAD < HEAD // 2, pltpu.roll(v, w - HEAD // 2, axis=1), pltpu.roll(v, HEAD // 2, axis=1))


def _tile_lanes(t, width):
    return t if width == t.shape[1] else jnp.concatenate([t] * (width // t.shape[1]), axis=1)


def _rope(v, cos, sin):
    return v * _tile_lanes(cos, v.shape[1]) + _swap_halves(v) * _tile_lanes(sin, v.shape[1])


def _unrope(g, cos, sin):
    return g * _tile_lanes(cos, g.shape[1]) + _swap_halves(g * _tile_lanes(sin, g.shape[1]))


def ffn_fwd(h, mods, g, w4, wo, layer, k0, n_lat, name, ex=None):
    s = h.shape[0]

    def body(h_ref, m_ref, g_ref, w_ref, wo_ref, ho_ref, ab_ref, f_ref):
        hh = h_ref[...]
        n, _, _, _ = _norm_mod(hh, g_ref[...], m_ref[k0:k0 + 1, :], m_ref[k0 + 1:k0 + 2, :])
        nb = n.astype(BF16)
        acc = jnp.zeros((TM, D), F32)
        for j in range(2):
            a = _dot(nb, w_ref[j])
            b = _dot(nb, w_ref[2 + j])
            ab_ref[:, j * FF_COLS:(j + 1) * FF_COLS] = a.astype(BF16)
            ab_ref[:, (2 + j) * FF_COLS:(3 + j) * FF_COLS] = b.astype(BF16)
            act = (a * jax.nn.sigmoid(a) * b).astype(BF16)
            acc = acc + _dot(act, wo_ref[j * FF_COLS:(j + 1) * FF_COLS, :])
        f_ref[...] = acc
        ho_ref[...] = hh + 0.5 * m_ref[k0 + 2:k0 + 3, :] * acc

    return _grid_call(
        body, name, s // TM,
        [_rows(D), _mods_spec(layer, n_lat), _whole((1, D), (layer,)), _whole((N_SLOT, D, FF_COLS)), _whole((D_FF, D))],
        [_rows(D), _rows(2 * D_FF), _rows(D)],
        [jax.ShapeDtypeStruct((s, D), F32), jax.ShapeDtypeStruct((s, 2 * D_FF), BF16), jax.ShapeDtypeStruct((s, D), F32)],
        (h, mods, g, w4, wo), "parallel", ex)


def ffn_bwd(h, ab, f, dh, mods, g, w4, wo, layer, k0, n_lat, name, ex=None):
    s = h.shape[0]

    def body(h_ref, ab_ref, f_ref, dh_ref, m_ref, g_ref, w_ref, wo_ref, dhi_ref, dab_ref, df_ref, n_ref, act_ref, dm_ref):
        i = pl.program_id(0)

        @pl.when((i == 0) | (i == n_lat))
        def _():
            dm_ref[...] = jnp.zeros_like(dm_ref)

        hh, dho, gg = h_ref[...], dh_ref[...], g_ref[...]
        scale, gate = m_ref[k0 + 1:k0 + 2, :], m_ref[k0 + 2:k0 + 3, :]
        n, xhat, r, y = _norm_mod(hh, gg, m_ref[k0:k0 + 1, :], scale)
        n_ref[...] = n.astype(BF16)
        dgate = _sum0(dho * (0.5 * f_ref[...]))
        dfb = ((0.5 * gate) * dho).astype(BF16)
        df_ref[...] = dfb
        dn = jnp.zeros((TM, D), F32)
        for j in range(2):
            a = ab_ref[:, j * FF_COLS:(j + 1) * FF_COLS].astype(F32)
            b = ab_ref[:, (2 + j) * FF_COLS:(3 + j) * FF_COLS].astype(F32)
            sg = jax.nn.sigmoid(a)
            sa = a * sg
            act_ref[:, j * FF_COLS:(j + 1) * FF_COLS] = (sa * b).astype(BF16)
            dact = _dotg(dfb, wo_ref[j * FF_COLS:(j + 1) * FF_COLS, :], NT)
            da = (dact * b * (sg * (1 + a * (1 - sg)))).astype(BF16)
            db = (dact * sa).astype(BF16)
            dab_ref[:, j * FF_COLS:(j + 1) * FF_COLS] = da
            dab_ref[:, (2 + j) * FF_COLS:(3 + j) * FF_COLS] = db
            dn = dn + _dotg(da, w_ref[j], NT) + _dotg(db, w_ref[2 + j], NT)
        dsh, dsc, dg, dhn = _norm_mod_bwd(dn, xhat, r, y, gg, scale)
        dhi_ref[...] = dho + dhn
        dm_ref[0:1, :] += dsh
        dm_ref[1:2, :] += dsc
        dm_ref[2:3, :] += dgate
        dm_ref[3:4, :] += dg

    return _grid_call(
        body, name, s // TM,
        [_rows(D), _rows(2 * D_FF), _rows(D), _rows(D), _mods_spec(layer, n_lat), _whole((1, D), (layer,)),
         _whole((N_SLOT, D, FF_COLS)), _whole((D_FF, D))],
        [_rows(D), _rows(2 * D_FF), _rows(D), _rows(D), _rows(D_FF), _acc_spec(n_lat)],
        [jax.ShapeDtypeStruct((s, D), F32), jax.ShapeDtypeStruct((s, 2 * D_FF), BF16), jax.ShapeDtypeStruct((s, D), BF16),
         jax.ShapeDtypeStruct((s, D), BF16), jax.ShapeDtypeStruct((s, D_FF), BF16), jax.ShapeDtypeStruct((2, 8, D), F32)],
        (h, ab, f, dh, mods, g, w4, wo), "arbitrary", ex)


def _token_tile(s, limit=2176):
    return max(ts for ts in range(16, limit + 1, 16) if s % ts == 0)


def wgrad(a, b, tk, tn, slot_cols, name, ex=None):
    s, k = a.shape
    n = b.shape[1]
    ts = _token_tile(s)
    steps = s // ts

    def body(a_ref, b_ref, o_ref, o16_ref):
        r = _dotg(a_ref[...], b_ref[...], TN)
        si = pl.program_id(2)

        @pl.when(si == 0)
        def _():
            o_ref[...] = r

        @pl.when(si > 0)
        def _():
            o_ref[...] += r

        @pl.when(si == steps - 1)
        def _():
            o16_ref[...] = o_ref[...].astype(BF16)

    if slot_cols is None:
        shape, spec = (k, n), pl.BlockSpec((tk, tn), lambda i, j, si: (i, j))
    else:
        per = slot_cols // tn
        shape, spec = (n // slot_cols, k, slot_cols), pl.BlockSpec((None, tk, tn), lambda i, j, si: (lax.div(j, per), i, lax.rem(j, per)))
    return _grid_call(
        body, name, (k // tk, n // tn, steps),
        [pl.BlockSpec((ts, tk), lambda i, j, si: (si, i)), pl.BlockSpec((ts, tn), lambda i, j, si: (si, j))], [spec, spec],
        [jax.ShapeDtypeStruct(shape, F32), jax.ShapeDtypeStruct(shape, BF16)], (a, b), ("parallel", "parallel", "arbitrary"), ex)


def proj_fwd(h, mods, g, w_in, cos, sin, layer, n_lat, name, ex=None):
    s = h.shape[0]

    def body(h_ref, m_ref, g_ref, w_ref, cos_ref, sin_ref, u_ref, q_ref, k_ref, v_ref):
        n, _, _, _ = _norm_mod(h_ref[...], g_ref[...], m_ref[3:4, :], m_ref[4:5, :])
        p = _dot(n.astype(BF16), w_ref[...])
        cs, sn = cos_ref[...], sin_ref[...]
        u_ref[...] = p[:, :POOL_W]
        q_ref[...] = (_rope(p[:, POOL_W:POOL_W + ATTN_W], cs, sn) * HEAD ** -0.5).astype(BF16)
        k_ref[...] = _rope(p[:, POOL_W + ATTN_W:POOL_W + ATTN_W + KV_W], cs, sn).astype(BF16)
        v_ref[...] = p[:, POOL_W + ATTN_W + KV_W:].astype(BF16)

    return _grid_call(
        body, name, s // TM,
        [_rows(D), _mods_spec(layer, n_lat), _whole((1, D), (layer,)), _whole((D, PROJ_W)), _rows(BLK), _rows(BLK)],
        [_rows(POOL_W), _rows(ATTN_W), _rows(KV_W), _rows(KV_W)],
        [jax.ShapeDtypeStruct((s, POOL_W), F32), jax.ShapeDtypeStruct((s, ATTN_W), BF16),
         jax.ShapeDtypeStruct((s, KV_W), BF16), jax.ShapeDtypeStruct((s, KV_W), BF16)],
        (h, mods, g, w_in, cos, sin), "parallel", ex)


def proj_bwd(h, du, dq, dk, dv, dh, mods, g, w_in, cos, sin, layer, n_lat, name):
    s = h.shape[0]

    def body(h_ref, du_ref, dq_ref, dk_ref, dv_ref, dh_ref, m_ref, g_ref, w_ref, cos_ref, sin_ref,
             dhi_ref, dp_ref, n_ref, dm_ref):
        i = pl.program_id(0)

        @pl.when((i == 0) | (i == n_lat))
        def _():
            dm_ref[...] = jnp.zeros_like(dm_ref)

        gg, scale = g_ref[...], m_ref[4:5, :]
        n, xhat, r, y = _norm_mod(h_ref[...], gg, m_ref[3:4, :], scale)
        n_ref[...] = n.astype(BF16)
        cs, sn = cos_ref[...], sin_ref[...]
        dp = jnp.concatenate([du_ref[...], _unrope(dq_ref[...], cs, sn) * HEAD ** -0.5, _unrope(dk_ref[...], cs, sn),
                              dv_ref[...]], axis=1).astype(BF16)
        dp_ref[...] = dp
        dsh, dsc, dg, dhn = _norm_mod_bwd(_dotg(dp, w_ref[...], NT), xhat, r, y, gg, scale)
        dhi_ref[...] = dh_ref[...] + dhn
        dm_ref[0:1, :] += dsh
        dm_ref[1:2, :] += dsc
        dm_ref[3:4, :] += dg

    return pl.pallas_call(
        body, name=name, grid=(s // TM,),
        in_specs=[_rows(D), _rows(POOL_W), _rows(ATTN_W), _rows(KV_W), _rows(KV_W), _rows(D), _mods_spec(layer, n_lat),
                  _whole((1, D), (layer,)), _whole((D, PROJ_W)), _rows(BLK), _rows(BLK)],
        out_specs=[_rows(D), _rows(PROJ_W), _rows(D), _acc_spec(n_lat)],
        out_shape=[jax.ShapeDtypeStruct((s, D), F32), jax.ShapeDtypeStruct((s, PROJ_W), BF16),
                   jax.ShapeDtypeStruct((s, D), BF16), jax.ShapeDtypeStruct((2, 8, D), F32)],
        compiler_params=_params("arbitrary"),
    )(h, du, dq, dk, dv, dh, mods, g, w_in, cos, sin)


def _window(i, s):
    return pl.multiple_of(jnp.clip(i * QB - BLK, 0, s - WIN), BLK)


def mix_tables(t, s):
    n_lat = t // QB
    blocks = jnp.array([0, 1, n_lat - 1] + list(range(n_lat, s // QB)))[:, None, None]
    ws = jnp.clip(blocks * QB - BLK, 0, s - WIN)
    q = blocks * QB + jnp.arange(QB)[None, :, None]
    k = ws + jnp.arange(WIN)[None, None, :]
    is_lat = blocks < n_lat
    local = jnp.where(is_lat & (k < t) & (jnp.abs(k - q) <= BLK), 0.0, NEG_INF).astype(F32)
    bias = jnp.concatenate([local, jnp.zeros(local.shape[:2] + (s - t,), F32)], axis=2)
    seq_lo, seq_hi = jnp.where(is_lat, 0, t), jnp.where(is_lat, t, s)
    bands, counts = [], []
    for w in POOL_WINDOWS:
        lo, hi = jnp.maximum(q - w // 2, seq_lo), jnp.minimum(q + w - w // 2, seq_hi)
        bands.append((k >= lo) & (k < hi))
        counts.append((hi - lo).astype(F32))
    band = jnp.stack(bands, axis=1).astype(BF16)
    count = jnp.concatenate(counts + [jnp.ones(counts[0].shape[:2] + (BLK - len(counts),), F32)], axis=2)
    return dict(bias=bias, band=band, band_t=band.transpose(0, 1, 3, 2), count=count)


def _case_spec(table, n_lat_blk):
    def kind(i):
        return jnp.where(i < n_lat_blk - 1, jnp.minimum(i, 1), i - n_lat_blk + 3)

    shape = table.shape[1:]
    return pl.BlockSpec((None,) + shape, lambda i: (kind(i),) + (0,) * len(shape))


def _split_dot(band, v):
    return _dot(band, v.astype(BF16))


def _pooled(u_ref, band_ref, cnt_ref, i, ws, gi):
    cols = slice(gi * GROUP, (gi + 1) * GROUP)
    mean = _split_dot(band_ref[gi], u_ref[pl.ds(ws, WIN), cols]) / cnt_ref[:, gi:gi + 1]
    return mean - u_ref[pl.ds(pl.multiple_of(i * QB, QB), QB), cols]


def _head_cols(hd):
    return slice(hd * HEAD, (hd + 1) * HEAD)


def _stack_heads(x, hk, first=0):
    return jnp.concatenate([x[:, first + (Q_GROUP * hk + g) * HEAD:first + (Q_GROUP * hk + g + 1) * HEAD]
                            for g in range(Q_GROUP)], axis=0)


def _biased(scores, bias):
    return (scores.reshape(Q_GROUP, QB, -1) + bias).reshape(Q_GROUP * QB, -1)


def _group_column(vals):
    row = lax.broadcasted_iota(jnp.int32, (Q_GROUP * QB, 1), 0)
    out = jnp.full((Q_GROUP * QB, 1), vals[Q_GROUP - 1], F32)
    for g in range(Q_GROUP - 2, -1, -1):
        out = jnp.where(row < (g + 1) * QB, vals[g], out)
    return out


def _lane_place(cols, width=BLK):
    lane = lax.broadcasted_iota(jnp.int32, (cols[0].shape[0], width), 1)
    out = jnp.zeros((cols[0].shape[0], width), F32)
    for hd, c in enumerate(cols):
        out = jnp.where(lane == hd, c, out)
    return out


def mix_fwd(h, q, k, v, u, w_pool, pool_scale, sink, w_out, mods, tables, layer, t, name, ex=None):
    s = h.shape[0]
    n_lat_blk = t // QB

    def body(h_ref, q_ref, k_ref, v_ref, u_ref, wp_ref, ps_ref, sink_ref, wo_ref, m_ref, bias_ref, band_ref, cnt_ref,
             ho_ref, cat_ref, lse_ref, mo_ref):
        i = pl.program_id(0)
        ws = _window(i, s)
        for gi in range(len(POOL_WINDOWS)):
            mixed = _dot(_pooled(u_ref, band_ref, cnt_ref, i, ws, gi).astype(BF16), wp_ref[gi])
            cat_ref[:, gi * GROUP:(gi + 1) * GROUP] = (mixed * ps_ref[:, gi * GROUP:(gi + 1) * GROUP]).astype(BF16)
        bias = bias_ref[...]
        k_all = jnp.concatenate([k_ref[pl.ds(ws, WIN), :], k_ref[t:s, :]], axis=0)
        v_all = jnp.concatenate([v_ref[pl.ds(ws, WIN), :], v_ref[t:s, :]], axis=0)
        lses = []
        for hk in range(N_HEADS // Q_GROUP):
            kv = _head_cols(hk)
            sc = _biased(_dotg(_stack_heads(q_ref[...], hk), k_all[:, kv], NT), bias)
            sk = _group_column([sink_ref[layer, Q_GROUP * hk + g] for g in range(Q_GROUP)])
            m = jnp.maximum(jnp.max(sc, axis=1, keepdims=True), sk)
            e = jnp.exp(sc - m)
            l = jnp.sum(e, axis=1, keepdims=True) + jnp.exp(sk - m)
            o = _dot(e.astype(BF16), v_all[:, kv]) * (1.0 / l)
            lse = m + jnp.log(l)
            for g in range(Q_GROUP):
                hd = Q_GROUP * hk + g
                cat_ref[:, POOL_W + hd * HEAD:POOL_W + (hd + 1) * HEAD] = o[g * QB:(g + 1) * QB].astype(BF16)
                lses.append(lse[g * QB:(g + 1) * QB])
        lse_ref[...] = _lane_place(lses)
        mo = _dot(cat_ref[...], wo_ref[...])
        mo_ref[...] = mo
        ho_ref[...] = h_ref[...] + m_ref[5:6, :] * mo

    blk = lambda cols: _rows(cols, QB)
    return _grid_call(
        body, name, s // QB,
        [blk(D), blk(ATTN_W), _whole((s, KV_W)), _whole((s, KV_W)), _whole((s, POOL_W)),
         _whole((len(POOL_WINDOWS), GROUP, GROUP), (layer,)), _whole((1, POOL_W), (layer,)),
         pl.BlockSpec(memory_space=pltpu.SMEM), _whole((POOL_W + ATTN_W, D)), _mods_spec(layer, n_lat_blk),
         _case_spec(tables["bias"], n_lat_blk), _case_spec(tables["band"], n_lat_blk), _case_spec(tables["count"], n_lat_blk)],
        [blk(D), blk(POOL_W + ATTN_W), blk(BLK), blk(D)],
        [jax.ShapeDtypeStruct((s, D), F32), jax.ShapeDtypeStruct((s, POOL_W + ATTN_W), BF16), jax.ShapeDtypeStruct((s, BLK), F32),
         jax.ShapeDtypeStruct((s, D), F32)],
        (h, q, k, v, u, w_pool, pool_scale, sink, w_out, mods, tables["bias"], tables["band"], tables["count"]), "parallel", ex)


def mix_bwd(dh, mo, q, k, v, u, lse, w_pool, pool_scale, sink, w_out, mods, tables, layer, t, name, ex=None):
    s = dh.shape[0]
    n_lat_blk = t // QB
    n_grp = len(POOL_WINDOWS)

    def body(dh_ref, mo_ref, q_ref, k_ref, v_ref, u_ref, lse_ref, wp_ref, ps_ref, sink_ref, wo_ref, m_ref,
             bias_ref, band_ref, band_t_ref, cnt_ref,
             dq_ref, dk_ref, dv_ref, du_ref, dmo_ref, dwp_ref, dps_ref, dsink_ref, dm_ref):
        i = pl.program_id(0)

        @pl.when(i == 0)
        def _():
            for ref in (dk_ref, dv_ref, du_ref, dwp_ref, dps_ref, dsink_ref):
                ref[...] = jnp.zeros_like(ref)

        @pl.when((i == 0) | (i == n_lat_blk))
        def _():
            dm_ref[...] = jnp.zeros_like(dm_ref)

        ws = _window(i, s)
        here = pl.ds(pl.multiple_of(i * QB, QB), QB)
        dho = dh_ref[...]
        dm_ref[2:3, :] += _sum0(dho * mo_ref[...])
        dmo = (m_ref[5:6, :] * dho).astype(BF16)
        dmo_ref[...] = dmo
        dcat = _dotg(dmo, wo_ref[...], NT)

        for gi in range(n_grp):
            cols = slice(gi * GROUP, (gi + 1) * GROUP)
            pooled = _pooled(u_ref, band_ref, cnt_ref, i, ws, gi).astype(BF16)
            dpo = dcat[:, cols]
            dps_ref[0:1, cols] += _sum0(dpo * _dot(pooled, wp_ref[gi]))
            dmixed = (dpo * ps_ref[:, cols]).astype(BF16)
            dwp_ref[gi] += _dotg(pooled, dmixed, TN)
            dpooled = _dotg(dmixed, wp_ref[gi], NT)
            du_ref[pl.ds(ws, WIN), cols] += _split_dot(band_t_ref[gi], dpooled / cnt_ref[:, gi:gi + 1])
            du_ref[here, cols] -= dpooled

        bias = bias_ref[...]
        k_all = jnp.concatenate([k_ref[pl.ds(ws, WIN), :], k_ref[t:s, :]], axis=0)
        v_all = jnp.concatenate([v_ref[pl.ds(ws, WIN), :], v_ref[t:s, :]], axis=0)
        qq, lse_all = q_ref[...], lse_ref[...]
        dqs, dsinks, dks, dvs = [], [], [], []
        for hk in range(N_HEADS // Q_GROUP):
            kv = _head_cols(hk)
            q4 = _stack_heads(qq, hk)
            lse = jnp.concatenate([lse_all[:, Q_GROUP * hk + g:Q_GROUP * hk + g + 1] for g in range(Q_GROUP)], axis=0)
            p = jnp.exp(_biased(_dotg(q4, k_all[:, kv], NT), bias) - lse)
            do = _stack_heads(dcat, hk, POOL_W).astype(BF16)
            dp = _dotg(do, v_all[:, kv], NT)
            delta = jnp.sum(p * dp, axis=1, keepdims=True)
            ds = (p * (dp - delta)).astype(BF16)
            sk = _group_column([sink_ref[layer, Q_GROUP * hk + g] for g in range(Q_GROUP)])
            dsk = -jnp.exp(sk - lse) * delta
            dq = _dot(ds, k_all[:, kv])
            for g in range(Q_GROUP):
                dqs.append(dq[g * QB:(g + 1) * QB])
                dsinks.append(_sum0(dsk[g * QB:(g + 1) * QB]))
            dks.append(_dotg(ds, q4, TN))
            dvs.append(_dotg(p.astype(BF16), do, TN))
        dq_ref[...] = jnp.concatenate(dqs, axis=1)
        dk, dv = jnp.concatenate(dks, axis=1), jnp.concatenate(dvs, axis=1)
        dk_ref[pl.ds(ws, WIN), :] += dk[:WIN]
        dv_ref[pl.ds(ws, WIN), :] += dv[:WIN]
        dk_ref[t:s, :] += dk[WIN:]
        dv_ref[t:s, :] += dv[WIN:]
        dsink_ref[0:1, :] += _lane_place(dsinks)

    blk = lambda cols: _rows(cols, QB)
    full = lambda shape: pl.BlockSpec(shape, lambda i: (0,) * len(shape))
    return _grid_call(
        body, name, s // QB,
        [blk(D), blk(D), blk(ATTN_W), _whole((s, KV_W)), _whole((s, KV_W)), _whole((s, POOL_W)),
         blk(BLK), _whole((n_grp, GROUP, GROUP), (layer,)), _whole((1, POOL_W), (layer,)),
         pl.BlockSpec(memory_space=pltpu.SMEM), _whole((POOL_W + ATTN_W, D)), _mods_spec(layer, n_lat_blk)]
        + [_case_spec(tables[key], n_lat_blk) for key in ("bias", "band", "band_t", "count")],
        [blk(ATTN_W), full((s, KV_W)), full((s, KV_W)), full((s, POOL_W)), blk(D),
         full((n_grp, GROUP, GROUP)), full((8, POOL_W)), full((8, BLK)), _acc_spec(n_lat_blk)],
        [jax.ShapeDtypeStruct((s, ATTN_W), F32), jax.ShapeDtypeStruct((s, KV_W), F32),
         jax.ShapeDtypeStruct((s, KV_W), F32), jax.ShapeDtypeStruct((s, POOL_W), F32),
         jax.ShapeDtypeStruct((s, D), BF16), jax.ShapeDtypeStruct((n_grp, GROUP, GROUP), F32),
         jax.ShapeDtypeStruct((8, POOL_W), F32), jax.ShapeDtypeStruct((8, BLK), F32), jax.ShapeDtypeStruct((2, 8, D), F32)],
        (dh, mo, q, k, v, u, lse, w_pool, pool_scale, sink, w_out, mods, tables["bias"], tables["band"], tables["band_t"],
         tables["count"]), "arbitrary", ex)


def loss_head(h, target, g, t, name):
    s = h.shape[0]
    n_lat = t // TM

    def body(h_ref, t_ref, g_ref, dh_ref, acc_ref):
        i = pl.program_id(0)

        @pl.when(i == 0)
        def _():
            acc_ref[...] = jnp.zeros_like(acc_ref)

        @pl.when(i < n_lat)
        def _():
            hh, gg = h_ref[...], g_ref[...]
            r = lax.rsqrt(jnp.mean(hh * hh, axis=-1, keepdims=True) + EPS)
            xhat = hh * r
            err = xhat * gg - t_ref[...]
            dy = err * (1.0 / D)
            dx = dy * gg
            dh_ref[...] = r * (dx - xhat * jnp.mean(dx * xhat, axis=-1, keepdims=True))
            acc_ref[0:1, :] += _sum0(dy * xhat)
            acc_ref[1:2, :] += _sum0(err * err)

        @pl.when(i >= n_lat)
        def _():
            dh_ref[...] = jnp.zeros_like(dh_ref)

    return pl.pallas_call(
        body, name=name, grid=(s // TM,),
        in_specs=[_rows(D), pl.BlockSpec((TM, D), lambda i: (jnp.minimum(i, n_lat - 1), 0)), _whole((1, D))],
        out_specs=[_rows(D), pl.BlockSpec((8, D), lambda i: (0, 0))],
        out_shape=[jax.ShapeDtypeStruct((s, D), F32), jax.ShapeDtypeStruct((8, D), F32)],
        compiler_params=_params("arbitrary"),
    )(h, target, g)


def mod_rows(c_all, w_mod, b_cols, name):
    def body(c_ref, w_ref, b_ref, o_ref):
        cc = c_ref[...]
        o_ref[...] = _dot((cc * jax.nn.sigmoid(cc)).astype(BF16), w_ref[...].astype(BF16)) + b_ref[...]

    return pl.pallas_call(
        body, name=name, grid=(2,),
        in_specs=[pl.BlockSpec((16, D), lambda l: (0, 0)), pl.BlockSpec((None, D, MOD_COLS), lambda l: (l, 0, 0)),
                  pl.BlockSpec((None, 1, MOD_COLS), lambda l: (l, 0, 0))],
        out_specs=pl.BlockSpec((None, 16, MOD_COLS), lambda l: (l, 0, 0)),
        out_shape=jax.ShapeDtypeStruct((2, 16, MOD_COLS), F32),
        compiler_params=_params("parallel"),
    )(c_all, w_mod, b_cols)


def mod_grads(c_all, dmod_cols, w_mod, name):
    def body(c_ref, d_ref, w_ref, dw_ref, dc_ref):
        @pl.when(pl.program_id(0) == 0)
        def _():
            dc_ref[...] = jnp.zeros_like(dc_ref)

        cc = c_ref[...]
        dd = d_ref[...].astype(BF16)
        dw_ref[...] = _dotg((cc * jax.nn.sigmoid(cc)).astype(BF16), dd, TN)
        dc_ref[...] += _dotg(dd, w_ref[...].astype(BF16), NT)

    return pl.pallas_call(
        body, name=name, grid=(2,),
        in_specs=[pl.BlockSpec((16, D), lambda l: (0, 0)), pl.BlockSpec((None, 16, MOD_COLS), lambda l: (l, 0, 0)),
                  pl.BlockSpec((None, D, MOD_COLS), lambda l: (l, 0, 0))],
        out_specs=[pl.BlockSpec((None, D, MOD_COLS), lambda l: (l, 0, 0)), pl.BlockSpec((16, D), lambda l: (0, 0))],
        out_shape=[jax.ShapeDtypeStruct((2, D, MOD_COLS), F32), jax.ShapeDtypeStruct((16, D), F32)],
        compiler_params=_params("arbitrary"),
    )(c_all, dmod_cols, w_mod)


def _row_tile(rows, cols, n_arrays):
    budget = VMEM_LIMIT_BYTES // 4 // (2 * 4 * n_arrays * cols)
    best = None
    for tr in range(16, rows + 1, 16):
        if rows % tr == 0 and tr <= budget:
            best = tr
    return best if best is not None else rows


def elementwise(fn, ins, out_dtypes, name, ex=None):
    rows, cols = ins[0].shape
    tr = _row_tile(rows, cols, len(ins) + len(out_dtypes))

    def body(*refs):
        outs = fn(*[r[...] for r in refs[:len(ins)]])
        for o_ref, o in zip(refs[len(ins):], outs):
            o_ref[...] = o.astype(o_ref.dtype)

    spec = pl.BlockSpec((tr, cols), lambda i: (i, 0))
    outs, got = _grid_call(body, name, rows // tr, [spec] * len(ins), [spec] * len(out_dtypes),
                           [jax.ShapeDtypeStruct((rows, cols), dt) for dt in out_dtypes], ins, "parallel", ex)
    return outs if ex is None else (outs, got)


def _adamw_tile(w, g, m, v):
    m = ADAM_B1 * m + (1.0 - ADAM_B1) * g
    v = ADAM_B2 * v + (1.0 - ADAM_B2) * (g * g)
    m_hat = m / (1.0 - ADAM_B1 ** ADAM_STEP)
    v_hat = v / (1.0 - ADAM_B2 ** ADAM_STEP)
    return -ADAM_LR * (m_hat / (jnp.sqrt(v_hat) + ADAM_EPS) + ADAM_WD * w), m, v


def adamw(w, g, m, v, name, ex=None):
    shape = w.shape
    two_d = (-1, shape[-1]) if w.ndim > 1 else (1, -1)
    outs = elementwise(_adamw_tile, [a.reshape(two_d) for a in (w, g, m, v)], [F32] * 3, name, ex)
    outs, got = outs if ex is not None else (outs, None)
    outs = [o.reshape(shape) for o in outs]
    return outs if ex is None else (outs, got)


def _prefetch_call(body, name, grid, in_specs, out_specs, out_shape, place, args, ex=None):
    if ex is None:
        spec = pltpu.PrefetchScalarGridSpec(num_scalar_prefetch=1, grid=grid, in_specs=in_specs, out_specs=out_specs)
        return pl.pallas_call(body, name=name, grid_spec=spec, out_shape=out_shape,
                              compiler_params=_params(*["parallel"] * len(grid)))(place, *args)
    n_in, n_out, ci, co = len(in_specs), len(out_specs), len(ex["ins"]), len(ex["out_shape"])
    spec = pltpu.PrefetchScalarGridSpec(num_scalar_prefetch=1, grid=grid, in_specs=list(in_specs) + _any(ci),
                                        out_specs=list(out_specs) + _any(co), scratch_shapes=ex["scratch"])
    outs = pl.pallas_call(
        _carrying(body, grid, n_in, n_out, ex, lead=1), name=name, grid_spec=spec, out_shape=list(out_shape) + ex["out_shape"],
        input_output_aliases={1 + n_in + i: n_out + j for i, j in ex["aliases"].items()},
        compiler_params=_params(*["arbitrary"] * len(grid)))(place, *args, *ex["ins"])
    return outs[:n_out], outs[n_out:]


def cast_place(w, layer, place, name):
    _, r, c = w.shape
    tr = _row_tile(r, c, 2)

    def body(p_ref, w_ref, o_ref):
        o_ref[...] = w_ref[...].astype(BF16)

    return _prefetch_call(
        body, name, (r // tr,), [pl.BlockSpec((None, tr, c), lambda i, p: (layer, i, 0))],
        pl.BlockSpec((None, tr, c), lambda i, p: (p[1], i, 0)), jax.ShapeDtypeStruct((N_SLOT, r, c), BF16), place, [w])


def pair_sum(g32, got, place, name, ex=None):
    n_slot, rh, c = got.shape
    tr = _row_tile(rh, c, 4)
    per = rh // tr

    def body(p_ref, a_ref, b_ref, o_ref, o16_ref):
        r = a_ref[...] + b_ref[...].astype(F32)
        o_ref[...] = r
        o16_ref[...] = r.astype(BF16)

    half = pl.BlockSpec((None, tr, c), lambda s, i, p: (s, i, 0))
    return _prefetch_call(
        body, name, (n_slot, per), [pl.BlockSpec((None, tr, c), lambda s, i, p: (s, p[0] * per + i, 0)), half], [half, half],
        [jax.ShapeDtypeStruct(got.shape, F32), jax.ShapeDtypeStruct(got.shape, BF16)], place, [g32, got], ex)


def chip_sum(p32, got, place, name):
    _, rh, c = p32.shape
    tr = _row_tile(rh, c, 5)
    per = rh // tr

    def body(p_ref, m_ref, r0_ref, r1_ref, r2_ref, o_ref):
        o_ref[...] = m_ref[...] + r0_ref[...].astype(F32) + r1_ref[...].astype(F32) + r2_ref[...].astype(F32)

    part = pl.BlockSpec((tr, c), lambda i, p: (i, 0))
    return _prefetch_call(
        body, name, (per,), [pl.BlockSpec((None, tr, c), lambda i, p: (p[1], i, 0)), part, part, part],
        pl.BlockSpec((tr, c), lambda i, p: (p[0] * per + i, 0)), jax.ShapeDtypeStruct((2 * rh, c), F32), place, [p32, *got])


def adamw_layers(w, g0, g1, m, v, name, ex=None):
    _, r, c = w.shape
    tr = _row_tile(r, c, 10)

    def body(w_ref, g0_ref, g1_ref, m_ref, v_ref, g_ref, d_ref, mo_ref, vo_ref):
        g = jnp.where(pl.program_id(0) == 0, g0_ref[...], g1_ref[...])
        g_ref[...] = g
        d_ref[...], mo_ref[...], vo_ref[...] = _adamw_tile(w_ref[...], g, m_ref[...], v_ref[...])

    steps = r // tr
    stacked = pl.BlockSpec((None, tr, c), lambda l, i: (l, i, 0))
    layer0 = pl.BlockSpec((tr, c), lambda l, i: (jnp.where(l == 0, i, steps - 1), 0))
    layer1 = pl.BlockSpec((tr, c), lambda l, i: (jnp.where(l == 0, 0, i), 0))
    outs, got = _grid_call(body, name, (2, steps), [stacked, layer0, layer1, stacked, stacked], [stacked] * 4,
                           [jax.ShapeDtypeStruct(w.shape, F32)] * 4, (w, g0, g1, m, v), "parallel", ex)
    return outs if ex is None else (outs, got)


def sum8(gathered, name):
    def body(*refs):
        n = len(refs) // 2
        for g_ref, o_ref in zip(refs[:n], refs[n:]):
            acc = g_ref[0]
            for dev in range(1, N_DEV):
                acc = acc + g_ref[dev]
            o_ref[...] = acc

    return pl.pallas_call(
        body, name=name,
        out_shape=[jax.ShapeDtypeStruct(a.shape[1:], F32) for a in gathered],
        compiler_params=_params(),
    )(*gathered)


PHASES = ("start", "late", "finish")


def _place():
    return lax.axis_index("x"), lax.axis_index("y"), lax.axis_index("c")


def _any(n):
    return [pl.BlockSpec(memory_space=pl.ANY)] * n


def gather8_exchange(blocks):
    n = len(blocks)

    def copy(outs, sems, ti, k, block, to, src=None):
        dst = outs[ti].at[4 * block[0] + 2 * block[1] + block[2]]
        return pltpu.make_async_remote_copy(src_ref=dst if src is None else src, dst_ref=dst, send_sem=sems[0].at[ti, k],
                                            recv_sem=sems[1].at[ti, k], device_id=to, device_id_type=MESH)

    def first(ins, outs, sems):
        x, y, c = _place()
        local, sent = [], []
        for ti in range(n):
            local.append(pltpu.make_async_copy(ins[ti], outs[ti].at[4 * x + 2 * y + c], sems[2].at[ti]))
            sent.append(copy(outs, sems, ti, 0, (x, y, c), (x, y, 1 - c), src=ins[ti]))
            sent += [copy(outs, sems, ti, 1 + j, (x, y, c), (*chip, c), src=ins[ti]) for j, chip in enumerate(_three_chips(x, y))]
        return local, sent

    def start(ins, outs, sems):
        local, sent = first(ins, outs, sems)
        for cp in local + sent:
            cp.start()

    def passed_on(outs, sems):
        x, y, c = _place()
        return [copy(outs, sems, ti, 4 + j, (*chip, c), (x, y, 1 - c)) for ti in range(n) for j, chip in enumerate(_three_chips(x, y))]

    def late(ins, outs, sems):
        x, y, c = _place()
        on = passed_on(outs, sems)
        for ti in range(n):
            for j, chip in enumerate(_three_chips(x, y)):
                copy(outs, sems, ti, 1 + j, (*chip, c), (x, y, c)).wait_recv()
                on[3 * ti + j].start()

    def finish(ins, outs, sems):
        x, y, c = _place()
        me, sibling = (x, y, c), (x, y, 1 - c)
        local, sent = first(ins, outs, sems)
        for ti in range(n):
            copy(outs, sems, ti, 0, sibling, me).wait_recv()
            for j, chip in enumerate(_three_chips(x, y)):
                copy(outs, sems, ti, 4 + j, (*chip, 1 - c), me).wait_recv()
        for cp in sent + passed_on(outs, sems):
            cp.wait_send()
        for cp in local:
            cp.wait()

    return dict(ins=list(blocks), out_shape=[jax.ShapeDtypeStruct((N_DEV,) + b.shape, b.dtype) for b in blocks], aliases={},
                start=start, late=late, finish=finish,
                scratch=[pltpu.SemaphoreType.DMA((n, 7)), pltpu.SemaphoreType.DMA((n, 7)), pltpu.SemaphoreType.DMA((n,))])


def all_gather(blocks, name):
    return run_exchange(gather8_exchange(blocks), name)


def _three_chips(x, y):
    return [(1 - x, y), (x, 1 - y), (1 - x, 1 - y)]


def gather_exchange(placed):
    n = len(placed)

    def copy(bufs, sems, ti, k, chip, core, to):
        rh = bufs[ti].shape[1] // 2
        half = bufs[ti].at[2 * chip[0] + chip[1], pl.ds(core * rh, rh), :]
        return pltpu.make_async_remote_copy(src_ref=half, dst_ref=half, send_sem=sems[0].at[ti, k], recv_sem=sems[1].at[ti, k],
                                            device_id=to, device_id_type=MESH)

    def sends(bufs, sems):
        x, y, c = _place()
        return [copy(bufs, sems, ti, k, (x, y), c, (*chip, c)) for ti in range(n) for k, chip in enumerate(_three_chips(x, y))]

    def passed_on(bufs, sems):
        x, y, c = _place()
        return [copy(bufs, sems, ti, 3 + k, chip, c, (x, y, 1 - c)) for ti in range(n) for k, chip in enumerate(_three_chips(x, y))]

    def start(ins, bufs, sems):
        for cp in sends(bufs, sems):
            cp.start()

    def late(ins, bufs, sems):
        x, y, c = _place()
        on = passed_on(bufs, sems)
        for ti in range(n):
            for k, chip in enumerate(_three_chips(x, y)):
                copy(bufs, sems, ti, k, chip, c, (x, y, c)).wait_recv()
                on[3 * ti + k].start()

    def finish(ins, bufs, sems):
        x, y, c = _place()
        for ti in range(n):
            for k, chip in enumerate(_three_chips(x, y)):
                copy(bufs, sems, ti, 3 + k, chip, 1 - c, (x, y, c)).wait_recv()
        for cp in sends(bufs, sems) + passed_on(bufs, sems):
            cp.wait_send()

    return dict(ins=list(placed), out_shape=[jax.ShapeDtypeStruct(w.shape, w.dtype) for w in placed],
                aliases={i: i for i in range(n)}, start=start, late=late, finish=finish,
                scratch=[pltpu.SemaphoreType.DMA((n, 6)), pltpu.SemaphoreType.DMA((n, 6))])


def scatter_exchange(p16):
    n = len(p16)

    def copies(ins, got, sems):
        x, y, c = _place()
        return [pltpu.make_async_remote_copy(src_ref=ins[ti].at[2 * chip[0] + chip[1]], dst_ref=got[3 * ti + k],
                                             send_sem=sems[0].at[ti, k], recv_sem=sems[1].at[ti, k], device_id=(*chip, c),
                                             device_id_type=MESH)
                for ti in range(n) for k, chip in enumerate(_three_chips(x, y))]

    def start(ins, got, sems):
        for cp in copies(ins, got, sems):
            cp.start()

    def finish(ins, got, sems):
        for cp in copies(ins, got, sems):
            cp.wait()

    return dict(ins=list(p16), out_shape=[jax.ShapeDtypeStruct(a.shape[1:], BF16) for a in p16 for _ in range(3)], aliases={},
                start=start, finish=finish, scratch=[pltpu.SemaphoreType.DMA((n, 3)), pltpu.SemaphoreType.DMA((n, 3))])


def run_exchange(ex, name):
    ci, co = len(ex["ins"]), len(ex["out_shape"])

    def body(*refs):
        ins, outs, sems = refs[:ci], refs[ci:ci + co], refs[ci + co:]
        for phase in PHASES:
            if phase in ex:
                ex[phase](ins, outs, sems)

    return pl.pallas_call(body, name=name, in_specs=_any(ci), out_specs=_any(co), out_shape=ex["out_shape"],
                          input_output_aliases=ex["aliases"], scratch_shapes=ex["scratch"])(*ex["ins"])


def _carrying(body, grid, n_in, n_out, ex, lead=0):
    ci, co = len(ex["ins"]), len(ex["out_shape"])
    first, last = (0,) * len(grid), tuple(g - 1 for g in grid)
    steps = dict(start=first, late=(grid[0] - 2,) if len(grid) == 1 and grid[0] > 2 else last, finish=last)

    def at(ids):
        return functools.reduce(jnp.logical_and, [pl.program_id(ax) == v for ax, v in enumerate(ids)])

    def carrying(*refs):
        head, refs = refs[:lead], refs[lead:]
        c_in, c_out = refs[n_in:n_in + ci], refs[n_in + ci + n_out:n_in + ci + n_out + co]
        sems = refs[n_in + ci + n_out + co:]
        for phase in PHASES:
            if phase == "finish":
                body(*head, *refs[:n_in], *refs[n_in + ci:n_in + ci + n_out])
            if phase in ex:
                pl.when(at(steps[phase]))(functools.partial(ex[phase], c_in, c_out, sems))

    return carrying


def _grid_call(body, name, grid, in_specs, out_specs, out_shape, args, sem, ex=None):
    grid = (grid,) if isinstance(grid, int) else tuple(grid)
    sems_of = (sem,) * len(grid) if isinstance(sem, str) else tuple(sem)
    n_in, n_out = len(in_specs), len(out_specs)
    if ex is None:
        return pl.pallas_call(body, name=name, grid=grid, in_specs=in_specs, out_specs=out_specs, out_shape=out_shape,
                              compiler_params=_params(*sems_of))(*args), []
    ci, co = len(ex["ins"]), len(ex["out_shape"])
    outs = pl.pallas_call(
        _carrying(body, grid, n_in, n_out, ex), name=name, grid=grid, in_specs=list(in_specs) + _any(ci),
        out_specs=list(out_specs) + _any(co), out_shape=list(out_shape) + ex["out_shape"], scratch_shapes=ex["scratch"],
        input_output_aliases={n_in + i: n_out + j for i, j in ex["aliases"].items()},
        compiler_params=_params(*["arbitrary"] * len(grid)),
    )(*args, *ex["ins"])
    return outs[:n_out], outs[n_out:]


def both(*exchanges):
    exchanges = [ex for ex in exchanges if ex is not None]
    if len(exchanges) < 2:
        return exchanges[0] if exchanges else None
    n_ins = [len(ex["ins"]) for ex in exchanges]
    n_outs = [len(ex["out_shape"]) for ex in exchanges]
    n_sems = [len(ex["scratch"]) for ex in exchanges]

    def parts(seq, counts, k):
        first = sum(counts[:k])
        return seq[first:first + counts[k]]

    def run(phase):
        def go(ins, outs, sems):
            for k, ex in enumerate(exchanges):
                if phase in ex:
                    ex[phase](parts(ins, n_ins, k), parts(outs, n_outs, k), parts(sems, n_sems, k))
        return go

    aliases = {sum(n_ins[:k]) + i: sum(n_outs[:k]) + j for k, ex in enumerate(exchanges) for i, j in ex["aliases"].items()}
    return dict(ins=[a for ex in exchanges for a in ex["ins"]], out_shape=[o for ex in exchanges for o in ex["out_shape"]],
                aliases=aliases, scratch=[s for ex in exchanges for s in ex["scratch"]], **{ph: run(ph) for ph in PHASES})


def split_outputs(got, *exchanges):
    got, out = list(got), []
    for ex in exchanges:
        n = len(ex["out_shape"]) if ex is not None else 0
        out.append(got[:n])
        got = got[n:]
    return out


def pair_exchange(g16):
    n = len(g16)

    def copies(a16, got, sems):
        x, y, c = _place()
        out = []
        for ti in range(n):
            rh = a16[ti].shape[1] // 2
            out.append(pltpu.make_async_remote_copy(
                src_ref=a16[ti].at[:, pl.ds((1 - c) * rh, rh), :], dst_ref=got[ti], send_sem=sems[0].at[ti],
                recv_sem=sems[1].at[ti], device_id=(x, y, 1 - c), device_id_type=MESH))
        return out

    def start(a16, got, sems):
        for cp in copies(a16, got, sems):
            cp.start()

    def finish(a16, got, sems):
        for cp in copies(a16, got, sems):
            cp.wait()

    return dict(ins=list(g16), out_shape=[jax.ShapeDtypeStruct((a.shape[0], a.shape[1] // 2, a.shape[2]), BF16) for a in g16],
                aliases={}, start=start, finish=finish, scratch=[pltpu.SemaphoreType.DMA((n,)), pltpu.SemaphoreType.DMA((n,))])


def _scatter_copies(src_ref, lands, send_sems, recv_sems):
    x, y, c = _place()
    return [pltpu.make_async_remote_copy(src_ref=src_ref.at[2 * chip[0] + chip[1]], dst_ref=lands[k], send_sem=send_sems.at[k],
                                         recv_sem=recv_sems.at[k], device_id=(*chip, c), device_id_type=MESH)
            for k, chip in enumerate(_three_chips(x, y))]


def scatter_start(p16, name):
    hbm, sem = pl.BlockSpec(memory_space=pltpu.HBM), pl.BlockSpec(memory_space=pltpu.SEMAPHORE)

    def body(src_ref, l0_ref, l1_ref, l2_ref, send_sems, recv_sems, src_thru, o0_ref, o1_ref, o2_ref, token_ref):
        for cp in _scatter_copies(src_ref, (l0_ref, l1_ref, l2_ref), send_sems, recv_sems):
            cp.start()
        token_ref[...] = jnp.zeros_like(token_ref)

    land = [pltpu.with_memory_space_constraint(lax.empty(p16.shape[1:], BF16), pltpu.HBM) for _ in range(3)]
    return pl.pallas_call(
        body, name=name,
        out_shape=(pltpu.SemaphoreType.DMA((3,)), pltpu.SemaphoreType.DMA((3,)), pltpu.HBM(p16.shape, BF16),
                   *[pltpu.HBM(p16.shape[1:], BF16)] * 3, jax.ShapeDtypeStruct((8, BLK), F32)),
        in_specs=(hbm,) * 4, out_specs=(sem, sem, hbm, hbm, hbm, hbm, pl.BlockSpec(memory_space=pltpu.VMEM)),
        input_output_aliases={0: 2, 1: 3, 2: 4, 3: 5},
        compiler_params=pltpu.CompilerParams(has_side_effects=pltpu.SideEffectType.DATAFLOW_SIDE_EFFECTING),
    )(pltpu.with_memory_space_constraint(p16, pltpu.HBM), *land)


def scatter_wait(send_sems, recv_sems, src_thru, lands, after, name):
    hbm, sem = pl.BlockSpec(memory_space=pltpu.HBM), pl.BlockSpec(memory_space=pltpu.SEMAPHORE)

    def body(src_ref, l0_ref, l1_ref, l2_ref, send_sems, recv_sems, *rest):
        for cp in _scatter_copies(src_ref, (l0_ref, l1_ref, l2_ref), send_sems, recv_sems):
            cp.wait_send()
            cp.wait_recv()

    return pl.pallas_call(
        body, name=name, out_shape=(pltpu.HBM(src_thru.shape, BF16), *[pltpu.HBM(lands[0].shape, BF16)] * 3),
        in_specs=(hbm, hbm, hbm, hbm, sem, sem, *_any(len(after))), out_specs=(hbm,) * 4,
        input_output_aliases={0: 0, 1: 1, 2: 2, 3: 3},
        compiler_params=pltpu.CompilerParams(has_side_effects=pltpu.SideEffectType.DATAFLOW_SIDE_EFFECTING),
    )(src_thru, *lands, send_sems, recv_sems, *after)[1:]


def pair_fill_exchange(halves):
    n = len(halves)

    def copies(bufs, sems, core):
        x, y, c = _place()
        out = []
        for ti in range(n):
            rh = bufs[ti].shape[0] // 2
            rows = bufs[ti].at[pl.ds((c if core == "mine" else 1 - c) * rh, rh), :]
            out.append(pltpu.make_async_remote_copy(src_ref=rows, dst_ref=rows, send_sem=sems[0].at[ti], recv_sem=sems[1].at[ti],
                                                    device_id=(x, y, 1 - c), device_id_type=MESH))
        return out

    def start(ins, bufs, sems):
        for cp in copies(bufs, sems, "mine"):
            cp.start()

    def finish(ins, bufs, sems):
        for cp in copies(bufs, sems, "mine"):
            cp.wait_send()
        for cp in copies(bufs, sems, "sibling's"):
            cp.wait_recv()

    return dict(ins=list(halves), out_shape=[jax.ShapeDtypeStruct(a.shape, a.dtype) for a in halves],
                aliases={i: i for i in range(n)}, start=start, finish=finish,
                scratch=[pltpu.SemaphoreType.DMA((n,)), pltpu.SemaphoreType.DMA((n,))])


def pair_gather(halves, name):
    return run_exchange(pair_fill_exchange(halves), name)


def reduce_small(dm_f1, dm_mix, dm_gate, dm_f2, loss_blk, name):
    def body(f1_ref, mix_ref, gate_ref, f2_ref, l_ref, tot_ref, rows_ref, fin_ref):
        rows_ref[...] = jnp.zeros_like(rows_ref)
        tot_ref[...] = jnp.zeros_like(tot_ref)
        mod_src = [(f1_ref, 0), (f1_ref, 1), (f1_ref, 2), (mix_ref, 0), (mix_ref, 1), (gate_ref, 2),
                   (f2_ref, 0), (f2_ref, 1), (f2_ref, 2)]
        norm_src = [(f1_ref, 3), (mix_ref, 3), (f2_ref, 3)]
        for l in range(2):
            for k, (ref, r) in enumerate(mod_src + norm_src):
                lat = ref[0, l, 0, r:r + 1, :]
                ctx = ref[0, l, 1, r:r + 1, :]
                for dev in range(N_DEV):
                    if dev:
                        lat = lat + ref[dev, l, 0, r:r + 1, :]
                        ctx = ctx + ref[dev, l, 1, r:r + 1, :]
                    if k < N_MOD:
                        rows_ref[l, dev, k:k + 1, :] = ref[dev, l, 0, r:r + 1, :]
                if k < N_MOD:
                    rows_ref[l, N_DEV, k:k + 1, :] = ctx
                tot_ref[l, k:k + 1, :] = lat + ctx
        acc = l_ref[0]
        for dev in range(1, N_DEV):
            acc = acc + l_ref[dev]
        loss = (0.5 / D) * jnp.sum(acc[1:2, :], axis=1, keepdims=True)
        row = lax.broadcasted_iota(jnp.int32, (8, D), 0)
        fin_ref[...] = jnp.where(row == 0, acc[0:1, :], loss)

    return pl.pallas_call(
        body, name=name,
        out_shape=[jax.ShapeDtypeStruct((2, 16, D), F32), jax.ShapeDtypeStruct((2, 16, 16, D), F32),
                   jax.ShapeDtypeStruct((8, D), F32)],
        compiler_params=_params(),
    )(dm_f1, dm_mix, dm_gate, dm_f2, loss_blk)


def rope_tables(t, s):
    rows = t // GRID_W
    row = jnp.repeat(jnp.arange(rows), GRID_W).astype(F32)
    col = jnp.tile(jnp.arange(GRID_W), rows).astype(F32)
    inv = ROPE_BASE ** (-jnp.arange(0, HEAD // 2, 2, dtype=F32) / (HEAD // 2))
    ang = jnp.concatenate([row[:, None] * inv, col[:, None] * inv], axis=-1)
    cos, sin = jnp.cos(ang), jnp.sin(ang)
    cos = jnp.concatenate([jnp.tile(cos, (1, 4)), jnp.ones((s - t, BLK), F32)], axis=0)
    sin = jnp.concatenate([jnp.tile(jnp.concatenate([-sin, sin], axis=1), (1, 2)), jnp.zeros((s - t, BLK), F32)], axis=0)
    return cos, sin


BIG = ("ffn1_in", "ffn1_out", "w_in", "w_out", "ffn2_in", "ffn2_out")
GROUPS = dict(ffn1=("ffn1_in", "ffn1_out"), mix=("w_in", "w_out"), ffn2=("ffn2_in", "ffn2_out"))
GATHER_BEHIND = {("ffn1", 0): [("w_in", 0), ("ffn2_out", 0), ("ffn1_out", 1)], ("proj", 0): [("w_out", 0)],
                 ("mix", 0): [("ffn2_in", 0)], ("ffn2", 0): [("ffn1_in", 1), ("w_in", 1)],
                 ("ffn1", 1): [("ffn2_in", 1), ("w_out", 1)], ("mix", 1): [("ffn2_out", 1)]}


def _slot_major(name, g):
    if name == "w_in":
        return jnp.stack(jnp.split(g, N_SLOT, axis=1), axis=0)
    if name in ("ffn1_in", "ffn2_in"):
        return g
    return g.reshape(N_SLOT, g.shape[0] // N_SLOT, g.shape[1])


def _whole_weight(name, buf):
    if name == "w_in":
        return buf.transpose(1, 0, 2).reshape(D, PROJ_W)
    if name in ("ffn1_in", "ffn2_in"):
        return buf
    return buf.reshape(-1, buf.shape[2])


def local_step(x1, ctx1, target, mods, norms, nfinal, placed, w_pool, pool_scale, sink, place, small_blocks):
    t, s = x1.shape[0], x1.shape[0] + ctx1.shape[0]
    n_lat = t // TM
    cos, sin = rope_tables(t, s)
    tables = mix_tables(t, s)
    wts ={name: list(pair) for name, pair in placed.items()}

    def gather(tensors):
        return gather_exchange([wts[name][l] for name, l in tensors])

    def gathered(tensors, arrays):
        for (name, l), whole in zip(tensors, arrays):
            wts[name][l] = whole

    def weight(name, l):
        return _whole_weight(name, wts[name][l])

    def fwd_ex(grp, l):
        groups = GATHER_BEHIND.get((grp, l))
        return (groups, gather(groups)) if groups else (None, None)

    first = [("ffn1_in", 0), ("ffn1_out", 0)]
    gathered(first, run_exchange(gather(first), "gather_first"))
    h = jnp.concatenate([x1, ctx1], axis=0)
    saved = []
    for l in range(2):
        h0 = h
        groups, ex = fwd_ex("ffn1", l)
        (h1, ab1, f1), got = ffn_fwd(h0, mods, norms[0], weight("ffn1_in", l), weight("ffn1_out", l), l, 0, n_lat, f"ffn1_fwd_{l}", ex)
        gathered(groups or [], got)
        groups, ex = fwd_ex("proj", l)
        (u, q, k, v), got = proj_fwd(h1, mods, norms[1], weight("w_in", l), cos, sin, l, n_lat, f"proj_fwd_{l}", ex)
        gathered(groups or [], got)
        groups, ex = fwd_ex("mix", l)
        (h2, cat, lse, mo), got = mix_fwd(h1, q, k, v, u, w_pool, pool_scale, sink, weight("w_out", l), mods, tables, l, t,
                                          f"mix_fwd_{l}", ex)
        gathered(groups or [], got)
        groups, ex = fwd_ex("ffn2", l)
        (h, ab2, f2), got = ffn_fwd(h2, mods, norms[2], weight("ffn2_in", l), weight("ffn2_out", l), l, 6, n_lat, f"ffn2_fwd_{l}", ex)
        gathered(groups or [], got)
        saved.append((h0, ab1, f1, h1, u, q, k, v, cat, lse, mo, h2, ab2, f2))
    dh, loss_blk = loss_head(h, target, nfinal, t, "loss_head")

    halves = {name: [None, None] for name in BIG}
    pending = []

    def summed_in_pair(grp, l, name_a, g_a, name_b, wgrad_b):
        g_b, got_a = wgrad_b(pair_exchange([_slot_major(name_a, g_a[1])]))
        sum_a, got_b = pair_sum(_slot_major(name_a, g_a[0]), got_a[0], place, f"pair_sum_{name_a}_{l}",
                                pair_exchange([_slot_major(name_b, g_b[1])]))
        sums = {name_a: sum_a, name_b: pair_sum(_slot_major(name_b, g_b[0]), got_b[0], place, f"pair_sum_{name_b}_{l}")}
        pending.append((grp, l, [sums[n] for n in GROUPS[grp]]))

    lacking = []

    def riders():
        return (scatter_exchange([p16 for _, p16 in pending[0][2]]) if pending else None,
                pair_fill_exchange([halves[name][l] for name, l in lacking]) if lacking else None)

    def carried(got, exs):
        got, filled = split_outputs(got, *exs)
        for (name, l), whole in zip(list(lacking), filled):
            halves[name][l] = whole
            lacking.remove((name, l))
        if pending:
            grp, l, pairs = pending.pop(0)
            for i, name in enumerate(GROUPS[grp]):
                halves[name][l] = chip_sum(pairs[i][0], got[3 * i:3 * i + 3], place, f"chip_sum_{name}_{l}")
                lacking.append((name, l))

    small = [None, None]
    for l in (1, 0):
        h0, ab1, f1, h1, u, q, k, v, cat, lse, mo, h2, ab2, f2 = saved[l]
        exs = riders()
        (dh, dab, df, n, act, dm_f2), got = ffn_bwd(h2, ab2, f2, dh, mods, norms[2], weight("ffn2_in", l), weight("ffn2_out", l),
                                                    l, 6, n_lat, f"ffn2_bwd_{l}", both(*exs))
        carried(got, exs)
        g_in, _ = wgrad(n, dab, D, FF_COLS, FF_COLS, f"ffn2_in_wgrad_{l}")
        summed_in_pair("ffn2", l, "ffn2_in", g_in, "ffn2_out",
                       lambda ex, a=act, b=df: wgrad(a, b, D_FF // 2, D, None, f"ffn2_out_wgrad_{l}", ex))
        exs = riders()
        (dq, dk, dv, du, dmo, dwp, dps, dsink, dm_gate), got = mix_bwd(
            dh, mo, q, k, v, u, lse, w_pool, pool_scale, sink, weight("w_out", l), mods, tables, l, t, f"mix_bwd_{l}", both(*exs))
        carried(got, exs)
        g_wo, _ = wgrad(cat, dmo, POOL_W + ATTN_W, D, None, f"w_out_wgrad_{l}")
        dh, dp, n, dm_mix = proj_bwd(h1, du, dq, dk, dv, dh, mods, norms[1], weight("w_in", l), cos, sin, l, n_lat, f"proj_bwd_{l}")
        summed_in_pair("mix", l, "w_out", g_wo, "w_in",
                       lambda ex, a=n, b=dp: wgrad(a, b, D, PROJ_W // 2, None, f"w_in_wgrad_{l}", ex))
        exs = riders()
        (dh, dab, df, n, act, dm_f1), got = ffn_bwd(h0, ab1, f1, dh, mods, norms[0], weight("ffn1_in", l), weight("ffn1_out", l),
                                                    l, 0, n_lat, f"ffn1_bwd_{l}", both(*exs))
        carried(got, exs)
        small[l] = dict(dm_f1=dm_f1, dm_mix=dm_mix, dm_gate=dm_gate, dm_f2=dm_f2, dwp=dwp, dps=dps, dsink=dsink)
        if l:
            g_in, _ = wgrad(n, dab, D, FF_COLS, FF_COLS, f"ffn1_in_wgrad_{l}")
            summed_in_pair("ffn1", l, "ffn1_in", g_in, "ffn1_out",
                           lambda ex, a=act, b=df: wgrad(a, b, D_FF // 2, D, None, f"ffn1_out_wgrad_{l}", ex))
    g_out, _ = wgrad(act, df, D_FF // 2, D, None, "ffn1_out_wgrad_0")
    got = run_exchange(pair_exchange([_slot_major("ffn1_out", g_out[1])]), "pair_exchange_ffn1_out_0")
    p32, p16 = pair_sum(_slot_major("ffn1_out", g_out[0]), got[0], place, "pair_sum_ffn1_out_0")
    riding = (scatter_exchange([p16]), gather8_exchange(small_blocks(small, loss_blk)),
              pair_fill_exchange([halves[name][l] for name, l in lacking]))
    g_in, got = wgrad(n, dab, D, FF_COLS, FF_COLS, "ffn1_in_wgrad_0", both(*riding))
    got, small_all, filled = split_outputs(got, *riding)
    for (name, l), whole in zip(lacking, filled):
        halves[name][l] = whole
    (halves["ffn1_out"][0],) = pair_gather([chip_sum(p32, got, place, "chip_sum_ffn1_out_0")], "pair_gather_ffn1_out_0")
    got = run_exchange(pair_exchange([_slot_major("ffn1_in", g_in[1])]), "pair_exchange_ffn1_in_0")
    return dh[:t], halves, pair_sum(_slot_major("ffn1_in", g_in[0]), got[0], place, "pair_sum_ffn1_in_0"), small_all


def _silu_grad(z):
    sg = jax.nn.sigmoid(z)
    return sg * (1 + z * (1 - sg))


def kernel(x, c, ctx, c_ctx, w_mod, b_mod, norm_ffn1, w_ffn1_in, w_ffn1_out, norm_mix, w_in, w_pool, pool_scale, sink, w_out, norm_ffn2, w_ffn2_in, w_ffn2_out, norm_final, loss_target, m_c_ctx, m_w_mod, m_b_mod, m_norm_ffn1, m_w_ffn1_in, m_w_ffn1_out, m_norm_mix, m_w_in, m_w_pool, m_pool_scale, m_sink, m_w_out, m_norm_ffn2, m_w_ffn2_in, m_w_ffn2_out, m_norm_final, v_c_ctx, v_w_mod, v_b_mod, v_norm_ffn1, v_w_ffn1_in, v_w_ffn1_out, v_norm_mix, v_w_in, v_w_pool, v_pool_scale, v_sink, v_w_out, v_norm_ffn2, v_w_ffn2_in, v_w_ffn2_out, v_norm_final):
    px, py, pc = _place()
    slot, me = 2 * px + py, 4 * px + 2 * py + pc
    n_grp = len(POOL_WINDOWS)

    (c_rows,) = all_gather([c.reshape(8, D // 8)], "gather_c")
    c_all = jnp.concatenate([c_rows.reshape(N_DEV, D), c_ctx.reshape(1, D), jnp.zeros((16 - N_DEV - 1, D), F32)], axis=0)
    b_cols = lax.dynamic_slice(b_mod, (0, slot * MOD_COLS), (2, MOD_COLS)).reshape(2, 1, MOD_COLS)
    (mod_parts,) = all_gather([mod_rows(c_all, w_mod, b_cols, "mod_rows")], "gather_mods")
    mods_all = mod_parts[0::2].transpose(1, 2, 0, 3).reshape(2, 16, N_MOD * D)
    mx = lax.dynamic_slice(mods_all, (0, me, 0), (2, 1, N_MOD * D)).reshape(2, N_MOD, D)
    mc = mods_all[:, N_DEV].reshape(2, N_MOD, D)
    pad = jnp.zeros((2, 16 - N_MOD, D), F32)
    mods = jnp.stack([jnp.concatenate([mx, pad], axis=1), jnp.concatenate([mc, pad], axis=1)], axis=1)

    place = jnp.stack([pc, slot]).astype(jnp.int32)
    shards = dict(ffn1_in=w_ffn1_in, ffn1_out=w_ffn1_out, w_in=w_in, w_out=w_out, ffn2_in=w_ffn2_in, ffn2_out=w_ffn2_out)
    placed = {name: [cast_place(shards[name], l, place, f"cast_{name}_{l}") for l in range(2)] for name in BIG}
    norms = [g.reshape(2, 1, D) for g in (norm_ffn1, norm_mix, norm_ffn2)]
    row_sums = ("dm_f1", "dm_mix", "dm_gate", "dm_f2")

    def small_blocks(small, loss_blk):
        stacked = {k: jnp.stack([small[0][k], small[1][k]]) for k in row_sums + ("dwp", "dps", "dsink")}
        return ([stacked[k].reshape(32, D) for k in row_sums]
                + [stacked["dwp"].reshape(2 * n_grp * GROUP, GROUP), stacked["dps"].reshape(16, POOL_W),
                   stacked["dsink"].reshape(16, BLK), loss_blk])

    dx, halves, last_pair, small_all = local_step(x[0], ctx[0], loss_target[0], mods, norms, norm_final.reshape(1, D), placed,
                                                   w_pool.astype(BF16), pool_scale.reshape(2, 1, POOL_W), sink, place, small_blocks)
    grads = {}

    *g_dm, g_dwp, g_dps, g_dsink, g_loss = small_all
    tot, rows, fin = reduce_small(*[g.reshape(N_DEV, 2, 2, 8, D) for g in g_dm], g_loss, "reduce_small")
    s_dwp, s_dps, s_dsink = sum8([g_dwp, g_dps, g_dsink], "sum_pool_sink")
    grads.update(
        w_pool=s_dwp.reshape(2, n_grp, GROUP, GROUP), pool_scale=s_dps.reshape(2, 8, POOL_W)[:, 0],
        sink=s_dsink.reshape(2, 8, BLK)[:, 0, :N_HEADS], b_mod=tot[:, :N_MOD].reshape(2, N_MOD * D),
        norm_ffn1=tot[:, N_MOD], norm_mix=tot[:, N_MOD + 1], norm_ffn2=tot[:, N_MOD + 2], norm_final=fin[0])
    loss = fin[1, 0]

    dmod_cols = lax.dynamic_slice(rows[:, :, :N_MOD, :].reshape(2, 16, N_MOD * D), (0, 0, slot * MOD_COLS), (2, 16, MOD_COLS))
    grads["w_mod"], dc = mod_grads(c_all, dmod_cols, w_mod, "mod_grads")
    (g_dc,) = all_gather([dc], "gather_dc")
    (s_dc,) = sum8([g_dc], "sum_dc")
    (d_c_ctx,) = elementwise(lambda d, z: (0.5 * d * _silu_grad(z),), [s_dc[N_DEV:N_DEV + 1], c_ctx.reshape(1, D)], [F32], "c_ctx_grad")
    send_sems, recv_sems, src_thru, *lands, token = scatter_start(last_pair[1], "scatter_last_start")
    grads["c_ctx"] = d_c_ctx.reshape(D) + token[0, :1]

    given = dict(c_ctx=(c_ctx, m_c_ctx, v_c_ctx), w_mod=(w_mod, m_w_mod, v_w_mod), b_mod=(b_mod, m_b_mod, v_b_mod),
                 norm_ffn1=(norm_ffn1, m_norm_ffn1, v_norm_ffn1), w_ffn1_in=(w_ffn1_in, m_w_ffn1_in, v_w_ffn1_in),
                 w_ffn1_out=(w_ffn1_out, m_w_ffn1_out, v_w_ffn1_out), norm_mix=(norm_mix, m_norm_mix, v_norm_mix),
                 w_in=(w_in, m_w_in, v_w_in), w_pool=(w_pool, m_w_pool, v_w_pool),
                 pool_scale=(pool_scale, m_pool_scale, v_pool_scale), sink=(sink, m_sink, v_sink), w_out=(w_out, m_w_out, v_w_out),
                 norm_ffn2=(norm_ffn2, m_norm_ffn2, v_norm_ffn2), w_ffn2_in=(w_ffn2_in, m_w_ffn2_in, v_w_ffn2_in),
                 w_ffn2_out=(w_ffn2_out, m_w_ffn2_out, v_w_ffn2_out), norm_final=(norm_final, m_norm_final, v_norm_final))
    shard = {(name, l): halves[name][l] for name in BIG for l in range(2)}

    def update(name):
        w, m, v = given[name]
        if name in BIG or name[2:] in BIG:
            key = name if name in BIG else name[2:]
            return adamw_layers(w, shard[key, 0], shard[key, 1], m, v, f"adamw_{name}")
        return [grads[name], *adamw(w, grads[name], m, v, f"adamw_{name}")]

    done = {name: update(name) for name in given if name != "w_ffn1_in"}
    got = scatter_wait(send_sems, recv_sems, src_thru, lands, [done[name][3] for name in done if name[2:] in BIG or name in BIG]
                       + [done["w_mod"][3]], "scatter_last_wait")
    (shard["ffn1_in", 0],) = pair_gather([chip_sum(last_pair[0], got, place, "chip_sum_ffn1_in_0")], "grad_pair_gather_last")
    done["w_ffn1_in"] = update("w_ffn1_in")
    return (loss, dx[None], *[done[name][i] for i in range(4) for name in given])
```

```python
import functools

import jax
import jax.numpy as jnp
from jax import lax
from jax.experimental import pallas as pl
from jax.experimental.pallas import tpu as pltpu

F32, BF16 = jnp.float32, jnp.bfloat16
D = 1024
D_FF = 2816
N_SLOT = 4
FF_COLS = 2 * D_FF // N_SLOT
N_MOD = 9
MOD_COLS = N_MOD * D // N_SLOT
POOL_W, ATTN_W, KV_W = 512, 512, 128
PROJ_W = POOL_W + ATTN_W + 2 * KV_W
N_HEADS, Q_GROUP, HEAD = 8, 4, 64
GROUP = 128
POOL_WINDOWS = (2, 4, 8, 16)
BLK = 128
QB = 256
WIN = QB + 2 * BLK
GRID_W = 64
ROPE_BASE = 10000.0
EPS = 1e-6
NEG_INF = -1e30
TM = 256
N_DEV = 8
VMEM_LIMIT_BYTES = 56 * 1024 * 1024
ADAM_LR, ADAM_B1, ADAM_B2, ADAM_EPS, ADAM_WD, ADAM_STEP = 0.001, 0.9, 0.999, 1e-08, 0.01, 10
MESH = pl.DeviceIdType.MESH
NT = (((1,), (1,)), ((), ()))
TN = (((0,), (0,)), ((), ()))


def _params(*sem):
    return pltpu.CompilerParams(dimension_semantics=sem, vmem_limit_bytes=VMEM_LIMIT_BYTES)


def _whole(shape, lead=()):
    idx = tuple(lead) + (0,) * len(shape)
    return pl.BlockSpec((None,) * len(lead) + tuple(shape), lambda *_: idx, pipeline_mode=pl.Buffered(1))


def _rows(cols, tm=TM):
    return pl.BlockSpec((tm, cols), lambda i: (i, 0))


def _mods_spec(layer, n_lat):
    return pl.BlockSpec((None, None, 16, D), lambda i: (layer, (i >= n_lat).astype(jnp.int32), 0, 0))


def _acc_spec(n_lat):
    return pl.BlockSpec((None, 8, D), lambda i: ((i >= n_lat).astype(jnp.int32), 0, 0))


def _dot(a, b):
    return jnp.dot(a, b, preferred_element_type=F32)


def _dotg(a, b, dims):
    return lax.dot_general(a, b, dims, preferred_element_type=F32)


def _sum0(v):
    return jnp.sum(v, axis=0, keepdims=True)


def _norm_mod(h, g, shift, scale):
    r = lax.rsqrt(jnp.mean(h * h, axis=-1, keepdims=True) + EPS)
    xhat = h * r
    y = xhat * g
    return y * (1 + scale) + shift, xhat, r, y


def _norm_mod_bwd(dn, xhat, r, y, g, scale):
    dy = dn * (1 + scale)
    dx = dy * g
    dh = r * (dx - xhat * jnp.mean(dx * xhat, axis=-1, keepdims=True))
    return _sum0(dn), _sum0(dn * y), _sum0(dy * xhat), dh


def _swap_halves(v):
    w = v.shape[1]
    lane = lax.broadcasted_iota(jnp.int32, v.shape, 1)
    return jnp.where(lane % HEAD < HEAD // 2, pltpu.roll(v, w - HEAD // 2, axis=1), pltpu.roll(v, HEAD // 2, axis=1))


def _tile_lanes(t, width):
    return t if width == t.shape[1] else jnp.concatenate([t] * (width // t.shape[1]), axis=1)


def _rope(v, cos, sin):
    return v * _tile_lanes(cos, v.shape[1]) + _swap_halves(v) * _tile_lanes(sin, v.shape[1])


def _unrope(g, cos, sin):
    return g * _tile_lanes(cos, g.shape[1]) + _swap_halves(g * _tile_lanes(sin, g.shape[1]))


def ffn_fwd(h, mods, g, w4, wo, layer, k0, n_lat, name, ex=None):
    s = h.shape[0]

    def body(h_ref, m_ref, g_ref, w_ref, wo_ref, ho_ref, ab_ref, f_ref):
        hh = h_ref[...]
        n, _, _, _ = _norm_mod(hh, g_ref[...], m_ref[k0:k0 + 1, :], m_ref[k0 + 1:k0 + 2, :])
        nb = n.astype(BF16)
        acc = jnp.zeros((TM, D), F32)
        for j in range(2):
            a = _dot(nb, w_ref[j])
            b = _dot(nb, w_ref[2 + j])
            ab_ref[:, j * FF_COLS:(j + 1) * FF_COLS] = a.astype(BF16)
            ab_ref[:, (2 + j) * FF_COLS:(3 + j) * FF_COLS] = b.astype(BF16)
            act = (a * jax.nn.sigmoid(a) * b).astype(BF16)
            acc = acc + _dot(act, wo_ref[j * FF_COLS:(j + 1) * FF_COLS, :])
        f_ref[...] = acc
        ho_ref[...] = hh + 0.5 * m_ref[k0 + 2:k0 + 3, :] * acc

    return _grid_call(
        body, name, s // TM,
        [_rows(D), _mods_spec(layer, n_lat), _whole((1, D), (layer,)), _whole((N_SLOT, D, FF_COLS)), _whole((D_FF, D))],
        [_rows(D), _rows(2 * D_FF), _rows(D)],
        [jax.ShapeDtypeStruct((s, D), F32), jax.ShapeDtypeStruct((s, 2 * D_FF), BF16), jax.ShapeDtypeStruct((s, D), F32)],
        (h, mods, g, w4, wo), "parallel", ex)


def ffn_bwd(h, ab, f, dh, mods, g, w4, wo, layer, k0, n_lat, name, ex=None):
    s = h.shape[0]

    def body(h_ref, ab_ref, f_ref, dh_ref, m_ref, g_ref, w_ref, wo_ref, dhi_ref, dab_ref, df_ref, n_ref, act_ref, dm_ref):
        i = pl.program_id(0)

        @pl.when((i == 0) | (i == n_lat))
        def _():
            dm_ref[...] = jnp.zeros_like(dm_ref)

        hh, dho, gg = h_ref[...], dh_ref[...], g_ref[...]
        scale, gate = m_ref[k0 + 1:k0 + 2, :], m_ref[k0 + 2:k0 + 3, :]
        n, xhat, r, y = _norm_mod(hh, gg, m_ref[k0:k0 + 1, :], scale)
        n_ref[...] = n.astype(BF16)
        dgate = _sum0(dho * (0.5 * f_ref[...]))
        dfb = ((0.5 * gate) * dho).astype(BF16)
        df_ref[...] = dfb
        dn = jnp.zeros((TM, D), F32)
        for j in range(2):
            a = ab_ref[:, j * FF_COLS:(j + 1) * FF_COLS].astype(F32)
            b = ab_ref[:, (2 + j) * FF_COLS:(3 + j) * FF_COLS].astype(F32)
            sg = jax.nn.sigmoid(a)
            sa = a * sg
            act_ref[:, j * FF_COLS:(j + 1) * FF_COLS] = (sa * b).astype(BF16)
            dact = _dotg(dfb, wo_ref[j * FF_COLS:(j + 1) * FF_COLS, :], NT)
            da = (dact * b * (sg * (1 + a * (1 - sg)))).astype(BF16)
            db = (dact * sa).astype(BF16)
            dab_ref[:, j * FF_COLS:(j + 1) * FF_COLS] = da
            dab_ref[:, (2 + j) * FF_COLS:(3 + j) * FF_COLS] = db
            dn = dn + _dotg(da, w_ref[j], NT) + _dotg(db, w_ref[2 + j], NT)
        dsh, dsc, dg, dhn = _norm_mod_bwd(dn, xhat, r, y, gg, scale)
        dhi_ref[...] = dho + dhn
        dm_ref[0:1, :] += dsh
        dm_ref[1:2, :] += dsc
        dm_ref[2:3, :] += dgate
        dm_ref[3:4, :] += dg

    return _grid_call(
        body, name, s // TM,
        [_rows(D), _rows(2 * D_FF), _rows(D), _rows(D), _mods_spec(layer, n_lat), _whole((1, D), (layer,)),
         _whole((N_SLOT, D, FF_COLS)), _whole((D_FF, D))],
        [_rows(D), _rows(2 * D_FF), _rows(D), _rows(D), _rows(D_FF), _acc_spec(n_lat)],
        [jax.ShapeDtypeStruct((s, D), F32), jax.ShapeDtypeStruct((s, 2 * D_FF), BF16), jax.ShapeDtypeStruct((s, D), BF16),
         jax.ShapeDtypeStruct((s, D), BF16), jax.ShapeDtypeStruct((s, D_FF), BF16), jax.ShapeDtypeStruct((2, 8, D), F32)],
        (h, ab, f, dh, mods, g, w4, wo), "arbitrary", ex)


def _token_tile(s, limit=2176):
    return max(ts for ts in range(16, limit + 1, 16) if s % ts == 0)


def wgrad(a, b, tk, tn, slot_cols, name, ex=None):
    s, k = a.shape
    n = b.shape[1]
    ts = _token_tile(s)
    steps = s // ts

    def body(a_ref, b_ref, o_ref, o16_ref):
        r = _dotg(a_ref[...], b_ref[...], TN)
        si = pl.program_id(2)

        @pl.when(si == 0)
        def _():
            o_ref[...] = r

        @pl.when(si > 0)
        def _():
            o_ref[...] += r

        @pl.when(si == steps - 1)
        def _():
            o16_ref[...] = o_ref[...].astype(BF16)

    if slot_cols is None:
        shape, spec = (k, n), pl.BlockSpec((tk, tn), lambda i, j, si: (i, j))
    else:
        per = slot_cols // tn
        shape, spec = (n // slot_cols, k, slot_cols), pl.BlockSpec((None, tk, tn), lambda i, j, si: (lax.div(j, per), i, lax.rem(j, per)))
    return _grid_call(
        body, name, (k // tk, n // tn, steps),
        [pl.BlockSpec((ts, tk), lambda i, j, si: (si, i)), pl.BlockSpec((ts, tn), lambda i, j, si: (si, j))], [spec, spec],
        [jax.ShapeDtypeStruct(shape, F32), jax.ShapeDtypeStruct(shape, BF16)], (a, b), ("parallel", "parallel", "arbitrary"), ex)


def proj_fwd(h, mods, g, w_in, cos, sin, layer, n_lat, name, ex=None):
    s = h.shape[0]

    def body(h_ref, m_ref, g_ref, w_ref, cos_ref, sin_ref, u_ref, q_ref, k_ref, v_ref):
        n, _, _, _ = _norm_mod(h_ref[...], g_ref[...], m_ref[3:4, :], m_ref[4:5, :])
        p = _dot(n.astype(BF16), w_ref[...])
        cs, sn = cos_ref[...], sin_ref[...]
        u_ref[...] = p[:, :POOL_W]
        q_ref[...] = (_rope(p[:, POOL_W:POOL_W + ATTN_W], cs, sn) * HEAD ** -0.5).astype(BF16)
        k_ref[...] = _rope(p[:, POOL_W + ATTN_W:POOL_W + ATTN_W + KV_W], cs, sn).astype(BF16)
        v_ref[...] = p[:, POOL_W + ATTN_W + KV_W:].astype(BF16)

    return _grid_call(
        body, name, s // TM,
        [_rows(D), _mods_spec(layer, n_lat), _whole((1, D), (layer,)), _whole((D, PROJ_W)), _rows(BLK), _rows(BLK)],
        [_rows(POOL_W), _rows(ATTN_W), _rows(KV_W), _rows(KV_W)],
        [jax.ShapeDtypeStruct((s, POOL_W), F32), jax.ShapeDtypeStruct((s, ATTN_W), BF16),
         jax.ShapeDtypeStruct((s, KV_W), BF16), jax.ShapeDtypeStruct((s, KV_W), BF16)],
        (h, mods, g, w_in, cos, sin), "parallel", ex)


def proj_bwd(h, du, dq, dk, dv, dh, mods, g, w_in, cos, sin, layer, n_lat, name):
    s = h.shape[0]

    def body(h_ref, du_ref, dq_ref, dk_ref, dv_ref, dh_ref, m_ref, g_ref, w_ref, cos_ref, sin_ref,
             dhi_ref, dp_ref, n_ref, dm_ref):
        i = pl.program_id(0)

        @pl.when((i == 0) | (i == n_lat))
        def _():
            dm_ref[...] = jnp.zeros_like(dm_ref)

        gg, scale = g_ref[...], m_ref[4:5, :]
        n, xhat, r, y = _norm_mod(h_ref[...], gg, m_ref[3:4, :], scale)
        n_ref[...] = n.astype(BF16)
        cs, sn = cos_ref[...], sin_ref[...]
        dp = jnp.concatenate([du_ref[...], _unrope(dq_ref[...], cs, sn) * HEAD ** -0.5, _unrope(dk_ref[...], cs, sn),
                              dv_ref[...]], axis=1).astype(BF16)
        dp_ref[...] = dp
        dsh, dsc, dg, dhn = _norm_mod_bwd(_dotg(dp, w_ref[...], NT), xhat, r, y, gg, scale)
        dhi_ref[...] = dh_ref[...] + dhn
        dm_ref[0:1, :] += dsh
        dm_ref[1:2, :] += dsc
        dm_ref[3:4, :] += dg

    return pl.pallas_call(
        body, name=name, grid=(s // TM,),
        in_specs=[_rows(D), _rows(POOL_W), _rows(ATTN_W), _rows(KV_W), _rows(KV_W), _rows(D), _mods_spec(layer, n_lat),
                  _whole((1, D), (layer,)), _whole((D, PROJ_W)), _rows(BLK), _rows(BLK)],
        out_specs=[_rows(D), _rows(PROJ_W), _rows(D), _acc_spec(n_lat)],
        out_shape=[jax.ShapeDtypeStruct((s, D), F32), jax.ShapeDtypeStruct((s, PROJ_W), BF16),
                   jax.ShapeDtypeStruct((s, D), BF16), jax.ShapeDtypeStruct((2, 8, D), F32)],
        compiler_params=_params("arbitrary"),
    )(h, du, dq, dk, dv, dh, mods, g, w_in, cos, sin)


def _window(i, s):
    return pl.multiple_of(jnp.clip(i * QB - BLK, 0, s - WIN), BLK)


def mix_tables(t, s):
    n_lat = t // QB
    blocks = jnp.array([0, 1, n_lat - 1] + list(range(n_lat, s // QB)))[:, None, None]
    ws = jnp.clip(blocks * QB - BLK, 0, s - WIN)
    q = blocks * QB + jnp.arange(QB)[None, :, None]
    k = ws + jnp.arange(WIN)[None, None, :]
    is_lat = blocks < n_lat
    local = jnp.where(is_lat & (k < t) & (jnp.abs(k - q) <= BLK), 0.0, NEG_INF).astype(F32)
    bias = jnp.concatenate([local, jnp.zeros(local.shape[:2] + (s - t,), F32)], axis=2)
    seq_lo, seq_hi = jnp.where(is_lat, 0, t), jnp.where(is_lat, t, s)
    bands, counts = [], []
    for w in POOL_WINDOWS:
        lo, hi = jnp.maximum(q - w // 2, seq_lo), jnp.minimum(q + w - w // 2, seq_hi)
        bands.append((k >= lo) & (k < hi))
        counts.append((hi - lo).astype(F32))
    band = jnp.stack(bands, axis=1).astype(BF16)
    count = jnp.concatenate(counts + [jnp.ones(counts[0].shape[:2] + (BLK - len(counts),), F32)], axis=2)
    return dict(bias=bias, band=band, band_t=band.transpose(0, 1, 3, 2), count=count)


def _case_spec(table, n_lat_blk):
    def kind(i):
        return jnp.where(i < n_lat_blk - 1, jnp.minimum(i, 1), i - n_lat_blk + 3)

    shape = table.shape[1:]
    return pl.BlockSpec((None,) + shape, lambda i: (kind(i),) + (0,) * len(shape))


def _split_dot(band, v):
    return _dot(band, v.astype(BF16))


def _pooled(u_ref, band_ref, cnt_ref, i, ws, gi):
    cols = slice(gi * GROUP, (gi + 1) * GROUP)
    mean = _split_dot(band_ref[gi], u_ref[pl.ds(ws, WIN), cols]) / cnt_ref[:, gi:gi + 1]
    return mean - u_ref[pl.ds(pl.multiple_of(i * QB, QB), QB), cols]


def _head_cols(hd):
    return slice(hd * HEAD, (hd + 1) * HEAD)


def _stack_heads(x, hk, first=0):
    return jnp.concatenate([x[:, first + (Q_GROUP * hk + g) * HEAD:first + (Q_GROUP * hk + g + 1) * HEAD]
                            for g in range(Q_GROUP)], axis=0)


def _biased(scores, bias):
    return (scores.reshape(Q_GROUP, QB, -1) + bias).reshape(Q_GROUP * QB, -1)


def _group_column(vals):
    row = lax.broadcasted_iota(jnp.int32, (Q_GROUP * QB, 1), 0)
    out = jnp.full((Q_GROUP * QB, 1), vals[Q_GROUP - 1], F32)
    for g in range(Q_GROUP - 2, -1, -1):
        out = jnp.where(row < (g + 1) * QB, vals[g], out)
    return out


def _lane_place(cols, width=BLK):
    lane = lax.broadcasted_iota(jnp.int32, (cols[0].shape[0], width), 1)
    out = jnp.zeros((cols[0].shape[0], width), F32)
    for hd, c in enumerate(cols):
        out = jnp.where(lane == hd, c, out)
    return out


def mix_fwd(h, q, k, v, u, w_pool, pool_scale, sink, w_out, mods, tables, layer, t, name, ex=None):
    s = h.shape[0]
    n_lat_blk = t // QB

    def body(h_ref, q_ref, k_ref, v_ref, u_ref, wp_ref, ps_ref, sink_ref, wo_ref, m_ref, bias_ref, band_ref, cnt_ref,
             ho_ref, cat_ref, lse_ref, mo_ref):
        i = pl.program_id(0)
        ws = _window(i, s)
        for gi in range(len(POOL_WINDOWS)):
            mixed = _dot(_pooled(u_ref, band_ref, cnt_ref, i, ws, gi).astype(BF16), wp_ref[gi])
            cat_ref[:, gi * GROUP:(gi + 1) * GROUP] = (mixed * ps_ref[:, gi * GROUP:(gi + 1) * GROUP]).astype(BF16)
        bias = bias_ref[...]
        k_all = jnp.concatenate([k_ref[pl.ds(ws, WIN), :], k_ref[t:s, :]], axis=0)
        v_all = jnp.concatenate([v_ref[pl.ds(ws, WIN), :], v_ref[t:s, :]], axis=0)
        lses = []
        for hk in range(N_HEADS // Q_GROUP):
            kv = _head_cols(hk)
            sc = _biased(_dotg(_stack_heads(q_ref[...], hk), k_all[:, kv], NT), bias)
            sk = _group_column([sink_ref[layer, Q_GROUP * hk + g] for g in range(Q_GROUP)])
            m = jnp.maximum(jnp.max(sc, axis=1, keepdims=True), sk)
            e = jnp.exp(sc - m)
            l = jnp.sum(e, axis=1, keepdims=True) + jnp.exp(sk - m)
            o = _dot(e.astype(BF16), v_all[:, kv]) * (1.0 / l)
            lse = m + jnp.log(l)
            for g in range(Q_GROUP):
                hd = Q_GROUP * hk + g
                cat_ref[:, POOL_W + hd * HEAD:POOL_W + (hd + 1) * HEAD] = o[g * QB:(g + 1) * QB].astype(BF16)
                lses.append(lse[g * QB:(g + 1) * QB])
        lse_ref[...] = _lane_place(lses)
        mo = _dot(cat_ref[...], wo_ref[...])
        mo_ref[...] = mo
        ho_ref[...] = h_ref[...] + m_ref[5:6, :] * mo

    blk = lambda cols: _rows(cols, QB)
    return _grid_call(
        body, name, s // QB,
        [blk(D), blk(ATTN_W), _whole((s, KV_W)), _whole((s, KV_W)), _whole((s, POOL_W)),
         _whole((len(POOL_WINDOWS), GROUP, GROUP), (layer,)), _whole((1, POOL_W), (layer,)),
         pl.BlockSpec(memory_space=pltpu.SMEM), _whole((POOL_W + ATTN_W, D)), _mods_spec(layer, n_lat_blk),
         _case_spec(tables["bias"], n_lat_blk), _case_spec(tables["band"], n_lat_blk), _case_spec(tables["count"], n_lat_blk)],
        [blk(D), blk(POOL_W + ATTN_W), blk(BLK), blk(D)],
        [jax.ShapeDtypeStruct((s, D), F32), jax.ShapeDtypeStruct((s, POOL_W + ATTN_W), BF16), jax.ShapeDtypeStruct((s, BLK), F32),
         jax.ShapeDtypeStruct((s, D), F32)],
        (h, q, k, v, u, w_pool, pool_scale, sink, w_out, mods, tables["bias"], tables["band"], tables["count"]), "parallel", ex)


def mix_bwd(dh, mo, q, k, v, u, lse, w_pool, pool_scale, sink, w_out, mods, tables, layer, t, name, ex=None):
    s = dh.shape[0]
    n_lat_blk = t // QB
    n_grp = len(POOL_WINDOWS)

    def body(dh_ref, mo_ref, q_ref, k_ref, v_ref, u_ref, lse_ref, wp_ref, ps_ref, sink_ref, wo_ref, m_ref,
             bias_ref, band_ref, band_t_ref, cnt_ref,
             dq_ref, dk_ref, dv_ref, du_ref, dmo_ref, dwp_ref, dps_ref, dsink_ref, dm_ref):
        i = pl.program_id(0)

        @pl.when(i == 0)
        def _():
            for ref in (dk_ref, dv_ref, du_ref, dwp_ref, dps_ref, dsink_ref):
                ref[...] = jnp.zeros_like(ref)

        @pl.when((i == 0) | (i == n_lat_blk))
        def _():
            dm_ref[...] = jnp.zeros_like(dm_ref)

        ws = _window(i, s)
        here = pl.ds(pl.multiple_of(i * QB, QB), QB)
        dho = dh_ref[...]
        dm_ref[2:3, :] += _sum0(dho * mo_ref[...])
        dmo = (m_ref[5:6, :] * dho).astype(BF16)
        dmo_ref[...] = dmo
        dcat = _dotg(dmo, wo_ref[...], NT)

        for gi in range(n_grp):
            cols = slice(gi * GROUP, (gi + 1) * GROUP)
            pooled = _pooled(u_ref, band_ref, cnt_ref, i, ws, gi).astype(BF16)
            dpo = dcat[:, cols]
            dps_ref[0:1, cols] += _sum0(dpo * _dot(pooled, wp_ref[gi]))
            dmixed = (dpo * ps_ref[:, cols]).astype(BF16)
            dwp_ref[gi] += _dotg(pooled, dmixed, TN)
            dpooled = _dotg(dmixed, wp_ref[gi], NT)
            du_ref[pl.ds(ws, WIN), cols] += _split_dot(band_t_ref[gi], dpooled / cnt_ref[:, gi:gi + 1])
            du_ref[here, cols] -= dpooled

        bias = bias_ref[...]
        k_all = jnp.concatenate([k_ref[pl.ds(ws, WIN), :], k_ref[t:s, :]], axis=0)
        v_all = jnp.concatenate([v_ref[pl.ds(ws, WIN), :], v_ref[t:s, :]], axis=0)
        qq, lse_all = q_ref[...], lse_ref[...]
        dqs, dsinks, dks, dvs = [], [], [], []
        for hk in range(N_HEADS // Q_GROUP):
            kv = _head_cols(hk)
            q4 = _stack_heads(qq, hk)
            lse = jnp.concatenate([lse_all[:, Q_GROUP * hk + g:Q_GROUP * hk + g + 1] for g in range(Q_GROUP)], axis=0)
            p = jnp.exp(_biased(_dotg(q4, k_all[:, kv], NT), bias) - lse)
            do = _stack_heads(dcat, hk, POOL_W).astype(BF16)
            dp = _dotg(do, v_all[:, kv], NT)
            delta = jnp.sum(p * dp, axis=1, keepdims=True)
            ds = (p * (dp - delta)).astype(BF16)
            sk = _group_column([sink_ref[layer, Q_GROUP * hk + g] for g in range(Q_GROUP)])
            dsk = -jnp.exp(sk - lse) * delta
            dq = _dot(ds, k_all[:, kv])
            for g in range(Q_GROUP):
                dqs.append(dq[g * QB:(g + 1) * QB])
                dsinks.append(_sum0(dsk[g * QB:(g + 1) * QB]))
            dks.append(_dotg(ds, q4, TN))
            dvs.append(_dotg(p.astype(BF16), do, TN))
        dq_ref[...] = jnp.concatenate(dqs, axis=1)
        dk, dv = jnp.concatenate(dks, axis=1), jnp.concatenate(dvs, axis=1)
        dk_ref[pl.ds(ws, WIN), :] += dk[:WIN]
        dv_ref[pl.ds(ws, WIN), :] += dv[:WIN]
        dk_ref[t:s, :] += dk[WIN:]
        dv_ref[t:s, :] += dv[WIN:]
        dsink_ref[0:1, :] += _lane_place(dsinks)

    blk = lambda cols: _rows(cols, QB)
    full = lambda shape: pl.BlockSpec(shape, lambda i: (0,) * len(shape))
    return _grid_call(
        body, name, s // QB,
        [blk(D), blk(D), blk(ATTN_W), _whole((s, KV_W)), _whole((s, KV_W)), _whole((s, POOL_W)),
         blk(BLK), _whole((n_grp, GROUP, GROUP), (layer,)), _whole((1, POOL_W), (layer,)),
         pl.BlockSpec(memory_space=pltpu.SMEM), _whole((POOL_W + ATTN_W, D)), _mods_spec(layer, n_lat_blk)]
        + [_case_spec(tables[key], n_lat_blk) for key in ("bias", "band", "band_t", "count")],
        [blk(ATTN_W), full((s, KV_W)), full((s, KV_W)), full((s, POOL_W)), blk(D),
         full((n_grp, GROUP, GROUP)), full((8, POOL_W)), full((8, BLK)), _acc_spec(n_lat_blk)],
        [jax.ShapeDtypeStruct((s, ATTN_W), F32), jax.ShapeDtypeStruct((s, KV_W), F32),
         jax.ShapeDtypeStruct((s, KV_W), F32), jax.ShapeDtypeStruct((s, POOL_W), F32),
         jax.ShapeDtypeStruct((s, D), BF16), jax.ShapeDtypeStruct((n_grp, GROUP, GROUP), F32),
         jax.ShapeDtypeStruct((8, POOL_W), F32), jax.ShapeDtypeStruct((8, BLK), F32), jax.ShapeDtypeStruct((2, 8, D), F32)],
        (dh, mo, q, k, v, u, lse, w_pool, pool_scale, sink, w_out, mods, tables["bias"], tables["band"], tables["band_t"],
         tables["count"]), "arbitrary", ex)


def loss_head(h, target, g, t, name):
    s = h.shape[0]
    n_lat = t // TM

    def body(h_ref, t_ref, g_ref, dh_ref, acc_ref):
        i = pl.program_id(0)

        @pl.when(i == 0)
        def _():
            acc_ref[...] = jnp.zeros_like(acc_ref)

        @pl.when(i < n_lat)
        def _():
            hh, gg = h_ref[...], g_ref[...]
            r = lax.rsqrt(jnp.mean(hh * hh, axis=-1, keepdims=True) + EPS)
            xhat = hh * r
            err = xhat * gg - t_ref[...]
            dy = err * (1.0 / D)
            dx = dy * gg
            dh_ref[...] = r * (dx - xhat * jnp.mean(dx * xhat, axis=-1, keepdims=True))
            acc_ref[0:1, :] += _sum0(dy * xhat)
            acc_ref[1:2, :] += _sum0(err * err)

        @pl.when(i >= n_lat)
        def _():
            dh_ref[...] = jnp.zeros_like(dh_ref)

    return pl.pallas_call(
        body, name=name, grid=(s // TM,),
        in_specs=[_rows(D), pl.BlockSpec((TM, D), lambda i: (jnp.minimum(i, n_lat - 1), 0)), _whole((1, D))],
        out_specs=[_rows(D), pl.BlockSpec((8, D), lambda i: (0, 0))],
        out_shape=[jax.ShapeDtypeStruct((s, D), F32), jax.ShapeDtypeStruct((8, D), F32)],
        compiler_params=_params("arbitrary"),
    )(h, target, g)


def mod_rows(c_all, w_mod, b_cols, name):
    def body(c_ref, w_ref, b_ref, o_ref):
        cc = c_ref[...]
        o_ref[...] = _dot((cc * jax.nn.sigmoid(cc)).astype(BF16), w_ref[...].astype(BF16)) + b_ref[...]

    return pl.pallas_call(
        body, name=name, grid=(2,),
        in_specs=[pl.BlockSpec((16, D), lambda l: (0, 0)), pl.BlockSpec((None, D, MOD_COLS), lambda l: (l, 0, 0)),
                  pl.BlockSpec((None, 1, MOD_COLS), lambda l: (l, 0, 0))],
        out_specs=pl.BlockSpec((None, 16, MOD_COLS), lambda l: (l, 0, 0)),
        out_shape=jax.ShapeDtypeStruct((2, 16, MOD_COLS), F32),
        compiler_params=_params("parallel"),
    )(c_all, w_mod, b_cols)


def mod_grads(c_all, dmod_cols, w_mod, name):
    def body(c_ref, d_ref, w_ref, dw_ref, dc_ref):
        @pl.when(pl.program_id(0) == 0)
        def _():
            dc_ref[...] = jnp.zeros_like(dc_ref)

        cc = c_ref[...]
        dd = d_ref[...].astype(BF16)
        dw_ref[...] = _dotg((cc * jax.nn.sigmoid(cc)).astype(BF16), dd, TN)
        dc_ref[...] += _dotg(dd, w_ref[...].astype(BF16), NT)

    return pl.pallas_call(
        body, name=name, grid=(2,),
        in_specs=[pl.BlockSpec((16, D), lambda l: (0, 0)), pl.BlockSpec((None, 16, MOD_COLS), lambda l: (l, 0, 0)),
                  pl.BlockSpec((None, D, MOD_COLS), lambda l: (l, 0, 0))],
        out_specs=[pl.BlockSpec((None, D, MOD_COLS), lambda l: (l, 0, 0)), pl.BlockSpec((16, D), lambda l: (0, 0))],
        out_shape=[jax.ShapeDtypeStruct((2, D, MOD_COLS), F32), jax.ShapeDtypeStruct((16, D), F32)],
        compiler_params=_params("arbitrary"),
    )(c_all, dmod_cols, w_mod)


def _row_tile(rows, cols, n_arrays):
    budget = VMEM_LIMIT_BYTES // 4 // (2 * 4 * n_arrays * cols)
    best = None
    for tr in range(16, rows + 1, 16):
        if rows % tr == 0 and tr <= budget:
            best = tr
    return best if best is not None else rows


def elementwise(fn, ins, out_dtypes, name, ex=None):
    rows, cols = ins[0].shape
    tr = _row_tile(rows, cols, len(ins) + len(out_dtypes))

    def body(*refs):
        outs = fn(*[r[...] for r in refs[:len(ins)]])
        for o_ref, o in zip(refs[len(ins):], outs):
            o_ref[...] = o.astype(o_ref.dtype)

    spec = pl.BlockSpec((tr, cols), lambda i: (i, 0))
    outs, got = _grid_call(body, name, rows // tr, [spec] * len(ins), [spec] * len(out_dtypes),
                           [jax.ShapeDtypeStruct((rows, cols), dt) for dt in out_dtypes], ins, "parallel", ex)
    return outs if ex is None else (outs, got)


def _adamw_tile(w, g, m, v):
    m = ADAM_B1 * m + (1.0 - ADAM_B1) * g
    v = ADAM_B2 * v + (1.0 - ADAM_B2) * (g * g)
    m_hat = m / (1.0 - ADAM_B1 ** ADAM_STEP)
    v_hat = v / (1.0 - ADAM_B2 ** ADAM_STEP)
    return -ADAM_LR * (m_hat / (jnp.sqrt(v_hat) + ADAM_EPS) + ADAM_WD * w), m, v


def adamw(w, g, m, v, name, ex=None):
    shape = w.shape
    two_d = (-1, shape[-1]) if w.ndim > 1 else (1, -1)
    outs = elementwise(_adamw_tile, [a.reshape(two_d) for a in (w, g, m, v)], [F32] * 3, name, ex)
    outs, got = outs if ex is not None else (outs, None)
    outs = [o.reshape(shape) for o in outs]
    return outs if ex is None else (outs, got)


def _prefetch_call(body, name, grid, in_specs, out_specs, out_shape, place, args, ex=None):
    if ex is None:
        spec = pltpu.PrefetchScalarGridSpec(num_scalar_prefetch=1, grid=grid, in_specs=in_specs, out_specs=out_specs)
        return pl.pallas_call(body, name=name, grid_spec=spec, out_shape=out_shape,
                              compiler_params=_params(*["parallel"] * len(grid)))(place, *args)
    n_in, n_out, ci, co = len(in_specs), len(out_specs), len(ex["ins"]), len(ex["out_shape"])
    spec = pltpu.PrefetchScalarGridSpec(num_scalar_prefetch=1, grid=grid, in_specs=list(in_specs) + _any(ci),
                                        out_specs=list(out_specs) + _any(co), scratch_shapes=ex["scratch"])
    outs = pl.pallas_call(
        _carrying(body, grid, n_in, n_out, ex, lead=1), name=name, grid_spec=spec, out_shape=list(out_shape) + ex["out_shape"],
        input_output_aliases={1 + n_in + i: n_out + j for i, j in ex["aliases"].items()},
        compiler_params=_params(*["arbitrary"] * len(grid)))(place, *args, *ex["ins"])
    return outs[:n_out], outs[n_out:]


def cast_place(w, layer, place, name):
    _, r, c = w.shape
    tr = _row_tile(r, c, 2)

    def body(p_ref, w_ref, o_ref):
        o_ref[...] = w_ref[...].astype(BF16)

    return _prefetch_call(
        body, name, (r // tr,), [pl.BlockSpec((None, tr, c), lambda i, p: (layer, i, 0))],
        pl.BlockSpec((None, tr, c), lambda i, p: (p[1], i, 0)), jax.ShapeDtypeStruct((N_SLOT, r, c), BF16), place, [w])


def pair_sum(g32, got, place, name, ex=None):
    n_slot, rh, c = got.shape
    tr = _row_tile(rh, c, 4)
    per = rh // tr

    def body(p_ref, a_ref, b_ref, o_ref, o16_ref):
        r = a_ref[...] + b_ref[...].astype(F32)
        o_ref[...] = r
        o16_ref[...] = r.astype(BF16)

    half = pl.BlockSpec((None, tr, c), lambda s, i, p: (s, i, 0))
    return _prefetch_call(
        body, name, (n_slot, per), [pl.BlockSpec((None, tr, c), lambda s, i, p: (s, p[0] * per + i, 0)), half], [half, half],
        [jax.ShapeDtypeStruct(got.shape, F32), jax.ShapeDtypeStruct(got.shape, BF16)], place, [g32, got], ex)


def chip_sum(p32, got, place, name):
    _, rh, c = p32.shape
    tr = _row_tile(rh, c, 5)
    per = rh // tr

    def body(p_ref, m_ref, r0_ref, r1_ref, r2_ref, o_ref):
        o_ref[...] = m_ref[...] + r0_ref[...].astype(F32) + r1_ref[...].astype(F32) + r2_ref[...].astype(F32)

    part = pl.BlockSpec((tr, c), lambda i, p: (i, 0))
    return _prefetch_call(
        body, name, (per,), [pl.BlockSpec((None, tr, c), lambda i, p: (p[1], i, 0)), part, part, part],
        pl.BlockSpec((tr, c), lambda i, p: (p[0] * per + i, 0)), jax.ShapeDtypeStruct((2 * rh, c), F32), place, [p32, *got])


def adamw_layers(w, g0, g1, m, v, name, ex=None):
    _, r, c = w.shape
    tr = _row_tile(r, c, 10)

    def body(w_ref, g0_ref, g1_ref, m_ref, v_ref, g_ref, d_ref, mo_ref, vo_ref):
        g = jnp.where(pl.program_id(0) == 0, g0_ref[...], g1_ref[...])
        g_ref[...] = g
        d_ref[...], mo_ref[...], vo_ref[...] = _adamw_tile(w_ref[...], g, m_ref[...], v_ref[...])

    steps = r // tr
    stacked = pl.BlockSpec((None, tr, c), lambda l, i: (l, i, 0))
    layer0 = pl.BlockSpec((tr, c), lambda l, i: (jnp.where(l == 0, i, steps - 1), 0))
    layer1 = pl.BlockSpec((tr, c), lambda l, i: (jnp.where(l == 0, 0, i), 0))
    outs, got = _grid_call(body, name, (2, steps), [stacked, layer0, layer1, stacked, stacked], [stacked] * 4,
                           [jax.ShapeDtypeStruct(w.shape, F32)] * 4, (w, g0, g1, m, v), "parallel", ex)
    return outs if ex is None else (outs, got)


def sum8(gathered, name):
    def body(*refs):
        n = len(refs) // 2
        for g_ref, o_ref in zip(refs[:n], refs[n:]):
            acc = g_ref[0]
            for dev in range(1, N_DEV):
                acc = acc + g_ref[dev]
            o_ref[...] = acc

    return pl.pallas_call(
        body, name=name,
        out_shape=[jax.ShapeDtypeStruct(a.shape[1:], F32) for a in gathered],
        compiler_params=_params(),
    )(*gathered)


PHASES = ("start", "late", "finish")


def _place():
    return lax.axis_index("x"), lax.axis_index("y"), lax.axis_index("c")


def _any(n):
    return [pl.BlockSpec(memory_space=pl.ANY)] * n


def gather8_exchange(blocks):
    n = len(blocks)

    def copy(outs, sems, ti, k, block, to, src=None):
        dst = outs[ti].at[4 * block[0] + 2 * block[1] + block[2]]
        return pltpu.make_async_remote_copy(src_ref=dst if src is None else src, dst_ref=dst, send_sem=sems[0].at[ti, k],
                                            recv_sem=sems[1].at[ti, k], device_id=to, device_id_type=MESH)

    def first(ins, outs, sems):
        x, y, c = _place()
        local, sent = [], []
        for ti in range(n):
            local.append(pltpu.make_async_copy(ins[ti], outs[ti].at[4 * x + 2 * y + c], sems[2].at[ti]))
            sent.append(copy(outs, sems, ti, 0, (x, y, c), (x, y, 1 - c), src=ins[ti]))
            sent += [copy(outs, sems, ti, 1 + j, (x, y, c), (*chip, c), src=ins[ti]) for j, chip in enumerate(_three_chips(x, y))]
        return local, sent

    def start(ins, outs, sems):
        local, sent = first(ins, outs, sems)
        for cp in local + sent:
            cp.start()

    def passed_on(outs, sems):
        x, y, c = _place()
        return [copy(outs, sems, ti, 4 + j, (*chip, c), (x, y, 1 - c)) for ti in range(n) for j, chip in enumerate(_three_chips(x, y))]

    def late(ins, outs, sems):
        x, y, c = _place()
        on = passed_on(outs, sems)
        for ti in range(n):
            for j, chip in enumerate(_three_chips(x, y)):
                copy(outs, sems, ti, 1 + j, (*chip, c), (x, y, c)).wait_recv()
                on[3 * ti + j].start()

    def finish(ins, outs, sems):
        x, y, c = _place()
        me, sibling = (x, y, c), (x, y, 1 - c)
        local, sent = first(ins, outs, sems)
        for ti in range(n):
            copy(outs, sems, ti, 0, sibling, me).wait_recv()
            for j, chip in enumerate(_three_chips(x, y)):
                copy(outs, sems, ti, 4 + j, (*chip, 1 - c), me).wait_recv()
        for cp in sent + passed_on(outs, sems):
            cp.wait_send()
        for cp in local:
            cp.wait()

    return dict(ins=list(blocks), out_shape=[jax.ShapeDtypeStruct((N_DEV,) + b.shape, b.dtype) for b in blocks], aliases={},
                start=start, late=late, finish=finish,
                scratch=[pltpu.SemaphoreType.DMA((n, 7)), pltpu.SemaphoreType.DMA((n, 7)), pltpu.SemaphoreType.DMA((n,))])


def all_gather(blocks, name):
    return run_exchange(gather8_exchange(blocks), name)


def _three_chips(x, y):
    return [(1 - x, y), (x, 1 - y), (1 - x, 1 - y)]


def gather_exchange(placed):
    n = len(placed)

    def copy(bufs, sems, ti, k, chip, core, to):
        rh = bufs[ti].shape[1] // 2
        half = bufs[ti].at[2 * chip[0] + chip[1], pl.ds(core * rh, rh), :]
        return pltpu.make_async_remote_copy(src_ref=half, dst_ref=half, send_sem=sems[0].at[ti, k], recv_sem=sems[1].at[ti, k],
                                            device_id=to, device_id_type=MESH)

    def sends(bufs, sems):
        x, y, c = _place()
        return [copy(bufs, sems, ti, k, (x, y), c, (*chip, c)) for ti in range(n) for k, chip in enumerate(_three_chips(x, y))]

    def passed_on(bufs, sems):
        x, y, c = _place()
        return [copy(bufs, sems, ti, 3 + k, chip, c, (x, y, 1 - c)) for ti in range(n) for k, chip in enumerate(_three_chips(x, y))]

    def start(ins, bufs, sems):
        for cp in sends(bufs, sems):
            cp.start()

    def late(ins, bufs, sems):
        x, y, c = _place()
        on = passed_on(bufs, sems)
        for ti in range(n):
            for k, chip in enumerate(_three_chips(x, y)):
                copy(bufs, sems, ti, k, chip, c, (x, y, c)).wait_recv()
                on[3 * ti + k].start()

    def finish(ins, bufs, sems):
        x, y, c = _place()
        for ti in range(n):
            for k, chip in enumerate(_three_chips(x, y)):
                copy(bufs, sems, ti, 3 + k, chip, 1 - c, (x, y, c)).wait_recv()
        for cp in sends(bufs, sems) + passed_on(bufs, sems):
            cp.wait_send()

    return dict(ins=list(placed), out_shape=[jax.ShapeDtypeStruct(w.shape, w.dtype) for w in placed],
                aliases={i: i for i in range(n)}, start=start, late=late, finish=finish,
                scratch=[pltpu.SemaphoreType.DMA((n, 6)), pltpu.SemaphoreType.DMA((n, 6))])


def scatter_exchange(p16):
    n = len(p16)

    def copies(ins, got, sems):
        x, y, c = _place()
        return [pltpu.make_async_remote_copy(src_ref=ins[ti].at[2 * chip[0] + chip[1]], dst_ref=got[3 * ti + k],
                                             send_sem=sems[0].at[ti, k], recv_sem=sems[1].at[ti, k], device_id=(*chip, c),
                                             device_id_type=MESH)
                for ti in range(n) for k, chip in enumerate(_three_chips(x, y))]

    def start(ins, got, sems):
        for cp in copies(ins, got, sems):
            cp.start()

    def finish(ins, got, sems):
        for cp in copies(ins, got, sems):
            cp.wait()

    return dict(ins=list(p16), out_shape=[jax.ShapeDtypeStruct(a.shape[1:], BF16) for a in p16 for _ in range(3)], aliases={},
                start=start, finish=finish, scratch=[pltpu.SemaphoreType.DMA((n, 3)), pltpu.SemaphoreType.DMA((n, 3))])


def run_exchange(ex, name):
    ci, co = len(ex["ins"]), len(ex["out_shape"])

    def body(*refs):
        ins, outs, sems = refs[:ci], refs[ci:ci + co], refs[ci + co:]
        for phase in PHASES:
            if phase in ex:
                ex[phase](ins, outs, sems)

    return pl.pallas_call(body, name=name, in_specs=_any(ci), out_specs=_any(co), out_shape=ex["out_shape"],
                          input_output_aliases=ex["aliases"], scratch_shapes=ex["scratch"])(*ex["ins"])


def _carrying(body, grid, n_in, n_out, ex, lead=0):
    ci, co = len(ex["ins"]), len(ex["out_shape"])
    first, last = (0,) * len(grid), tuple(g - 1 for g in grid)
    total = functools.reduce(lambda a, b: a * b, grid)
    at_late, late = (total - 2 if len(grid) == 1 else total // 2), []
    for g in reversed(grid):
        late.insert(0, at_late % g)
        at_late //= g
    steps = dict(start=first, late=tuple(late) if tuple(late) not in (first, last) and total > 2 else last, finish=last)

    def at(ids):
        return functools.reduce(jnp.logical_and, [pl.program_id(ax) == v for ax, v in enumerate(ids)])

    def carrying(*refs):
        head, refs = refs[:lead], refs[lead:]
        c_in, c_out = refs[n_in:n_in + ci], refs[n_in + ci + n_out:n_in + ci + n_out + co]
        sems = refs[n_in + ci + n_out + co:]
        for phase in PHASES:
            if phase == "finish":
                body(*head, *refs[:n_in], *refs[n_in + ci:n_in + ci + n_out])
            if phase in ex:
                pl.when(at(steps[phase]))(functools.partial(ex[phase], c_in, c_out, sems))

    return carrying


def _grid_call(body, name, grid, in_specs, out_specs, out_shape, args, sem, ex=None):
    grid = (grid,) if isinstance(grid, int) else tuple(grid)
    sems_of = (sem,) * len(grid) if isinstance(sem, str) else tuple(sem)
    n_in, n_out = len(in_specs), len(out_specs)
    if ex is None:
        return pl.pallas_call(body, name=name, grid=grid, in_specs=in_specs, out_specs=out_specs, out_shape=out_shape,
                              compiler_params=_params(*sems_of))(*args), []
    ci, co = len(ex["ins"]), len(ex["out_shape"])
    outs = pl.pallas_call(
        _carrying(body, grid, n_in, n_out, ex), name=name, grid=grid, in_specs=list(in_specs) + _any(ci),
        out_specs=list(out_specs) + _any(co), out_shape=list(out_shape) + ex["out_shape"], scratch_shapes=ex["scratch"],
        input_output_aliases={n_in + i: n_out + j for i, j in ex["aliases"].items()},
        compiler_params=_params(*["arbitrary"] * len(grid)),
    )(*args, *ex["ins"])
    return outs[:n_out], outs[n_out:]


def both(*exchanges):
    exchanges = [ex for ex in exchanges if ex is not None]
    if len(exchanges) < 2:
        return exchanges[0] if exchanges else None
    n_ins = [len(ex["ins"]) for ex in exchanges]
    n_outs = [len(ex["out_shape"]) for ex in exchanges]
    n_sems = [len(ex["scratch"]) for ex in exchanges]

    def parts(seq, counts, k):
        first = sum(counts[:k])
        return seq[first:first + counts[k]]

    def run(phase):
        def go(ins, outs, sems):
            for k, ex in enumerate(exchanges):
                if phase in ex:
                    ex[phase](parts(ins, n_ins, k), parts(outs, n_outs, k), parts(sems, n_sems, k))
        return go

    aliases = {sum(n_ins[:k]) + i: sum(n_outs[:k]) + j for k, ex in enumerate(exchanges) for i, j in ex["aliases"].items()}
    return dict(ins=[a for ex in exchanges for a in ex["ins"]], out_shape=[o for ex in exchanges for o in ex["out_shape"]],
                aliases=aliases, scratch=[s for ex in exchanges for s in ex["scratch"]], **{ph: run(ph) for ph in PHASES})


def split_outputs(got, *exchanges):
    got, out = list(got), []
    for ex in exchanges:
        n = len(ex["out_shape"]) if ex is not None else 0
        out.append(got[:n])
        got = got[n:]
    return out


def pair_exchange(g16):
    n = len(g16)

    def copies(a16, got, sems):
        x, y, c = _place()
        out = []
        for ti in range(n):
            rh = a16[ti].shape[1] // 2
            out.append(pltpu.make_async_remote_copy(
                src_ref=a16[ti].at[:, pl.ds((1 - c) * rh, rh), :], dst_ref=got[ti], send_sem=sems[0].at[ti],
                recv_sem=sems[1].at[ti], device_id=(x, y, 1 - c), device_id_type=MESH))
        return out

    def start(a16, got, sems):
        for cp in copies(a16, got, sems):
            cp.start()

    def finish(a16, got, sems):
        for cp in copies(a16, got, sems):
            cp.wait()

    return dict(ins=list(g16), out_shape=[jax.ShapeDtypeStruct((a.shape[0], a.shape[1] // 2, a.shape[2]), BF16) for a in g16],
                aliases={}, start=start, finish=finish, scratch=[pltpu.SemaphoreType.DMA((n,)), pltpu.SemaphoreType.DMA((n,))])


def _gather_half(buf, chip, core):
    rh = buf.shape[1] // 2
    return buf.at[2 * chip[0] + chip[1], pl.ds(core * rh, rh), :]


def gather_start(placed, name):
    n = len(placed)
    hbm, sem = pl.BlockSpec(memory_space=pltpu.HBM), pl.BlockSpec(memory_space=pltpu.SEMAPHORE)

    def body(*refs):
        bufs, send_sems, recv_sems, token_ref = refs[:n], refs[n], refs[n + 1], refs[-1]
        x, y, c = _place()
        for ti in range(n):
            for k, chip in enumerate(_three_chips(x, y)):
                half = _gather_half(bufs[ti], (x, y), c)
                pltpu.make_async_remote_copy(src_ref=half, dst_ref=half, send_sem=send_sems.at[3 * ti + k],
                                             recv_sem=recv_sems.at[3 * ti + k], device_id=(*chip, c), device_id_type=MESH).start()
        token_ref[...] = jnp.zeros_like(token_ref)

    return pl.pallas_call(
        body, name=name,
        out_shape=(pltpu.SemaphoreType.DMA((3 * n,)), pltpu.SemaphoreType.DMA((3 * n,)), *[pltpu.HBM(w.shape, w.dtype) for w in placed],
                   jax.ShapeDtypeStruct((8, BLK), F32)),
        in_specs=(hbm,) * n, out_specs=(sem, sem, *(hbm,) * n, pl.BlockSpec(memory_space=pltpu.VMEM)),
        input_output_aliases={i: 2 + i for i in range(n)},
        compiler_params=pltpu.CompilerParams(has_side_effects=pltpu.SideEffectType.DATAFLOW_SIDE_EFFECTING),
    )(*[pltpu.with_memory_space_constraint(w, pltpu.HBM) for w in placed])


def gather_wait(send_sems, recv_sems, bufs, after, name):
    n = len(bufs)
    hbm, sem = pl.BlockSpec(memory_space=pltpu.HBM), pl.BlockSpec(memory_space=pltpu.SEMAPHORE)

    def body(*refs):
        bufs, send_sems, recv_sems = refs[:n], refs[n], refs[n + 1]
        x, y, c = _place()
        for ti in range(n):
            for k, chip in enumerate(_three_chips(x, y)):
                mine, theirs = _gather_half(bufs[ti], (x, y), c), _gather_half(bufs[ti], chip, c)
                cp = pltpu.make_async_remote_copy(src_ref=mine, dst_ref=theirs, send_sem=send_sems.at[3 * ti + k],
                                                  recv_sem=recv_sems.at[3 * ti + k], device_id=(*chip, c), device_id_type=MESH)
                cp.wait_send()
                cp.wait_recv()

    return pl.pallas_call(
        body, name=name, out_shape=tuple(pltpu.HBM(w.shape, w.dtype) for w in bufs),
        in_specs=(*(hbm,) * n, sem, sem, *_any(len(after))), out_specs=(hbm,) * n,
        input_output_aliases={i: i for i in range(n)},
        compiler_params=pltpu.CompilerParams(has_side_effects=pltpu.SideEffectType.DATAFLOW_SIDE_EFFECTING),
    )(*bufs, send_sems, recv_sems, *after)


def pass_on_exchange(bufs):
    n = len(bufs)

    def copies(refs, sems, core):
        x, y, c = _place()
        return [pltpu.make_async_remote_copy(src_ref=_gather_half(refs[ti], chip, c if core == "mine" else 1 - c),
                                             dst_ref=_gather_half(refs[ti], chip, c if core == "mine" else 1 - c),
                                             send_sem=sems[0].at[ti, k], recv_sem=sems[1].at[ti, k], device_id=(x, y, 1 - c),
                                             device_id_type=MESH)
                for ti in range(n) for k, chip in enumerate(_three_chips(x, y))]

    def start(ins, refs, sems):
        for cp in copies(refs, sems, "mine"):
            cp.start()

    def finish(ins, refs, sems):
        for cp in copies(refs, sems, "mine"):
            cp.wait_send()
        for cp in copies(refs, sems, "sibling's"):
            cp.wait_recv()

    return dict(ins=list(bufs), out_shape=[jax.ShapeDtypeStruct(w.shape, w.dtype) for w in bufs], aliases={i: i for i in range(n)},
                start=start, finish=finish, scratch=[pltpu.SemaphoreType.DMA((n, 3)), pltpu.SemaphoreType.DMA((n, 3))])


def _scatter_copies(src_ref, lands, send_sems, recv_sems):
    x, y, c = _place()
    return [pltpu.make_async_remote_copy(src_ref=src_ref.at[2 * chip[0] + chip[1]], dst_ref=lands[k], send_sem=send_sems.at[k],
                                         recv_sem=recv_sems.at[k], device_id=(*chip, c), device_id_type=MESH)
            for k, chip in enumerate(_three_chips(x, y))]


def scatter_start(p16, name):
    hbm, sem = pl.BlockSpec(memory_space=pltpu.HBM), pl.BlockSpec(memory_space=pltpu.SEMAPHORE)

    def body(src_ref, l0_ref, l1_ref, l2_ref, send_sems, recv_sems, src_thru, o0_ref, o1_ref, o2_ref, token_ref):
        for cp in _scatter_copies(src_ref, (l0_ref, l1_ref, l2_ref), send_sems, recv_sems):
            cp.start()
        token_ref[...] = jnp.zeros_like(token_ref)

    land = [pltpu.with_memory_space_constraint(lax.empty(p16.shape[1:], BF16), pltpu.HBM) for _ in range(3)]
    return pl.pallas_call(
        body, name=name,
        out_shape=(pltpu.SemaphoreType.DMA((3,)), pltpu.SemaphoreType.DMA((3,)), pltpu.HBM(p16.shape, BF16),
                   *[pltpu.HBM(p16.shape[1:], BF16)] * 3, jax.ShapeDtypeStruct((8, BLK), F32)),
        in_specs=(hbm,) * 4, out_specs=(sem, sem, hbm, hbm, hbm, hbm, pl.BlockSpec(memory_space=pltpu.VMEM)),
        input_output_aliases={0: 2, 1: 3, 2: 4, 3: 5},
        compiler_params=pltpu.CompilerParams(has_side_effects=pltpu.SideEffectType.DATAFLOW_SIDE_EFFECTING),
    )(pltpu.with_memory_space_constraint(p16, pltpu.HBM), *land)


def scatter_wait(send_sems, recv_sems, src_thru, lands, after, name):
    hbm, sem = pl.BlockSpec(memory_space=pltpu.HBM), pl.BlockSpec(memory_space=pltpu.SEMAPHORE)

    def body(src_ref, l0_ref, l1_ref, l2_ref, send_sems, recv_sems, *rest):
        for cp in _scatter_copies(src_ref, (l0_ref, l1_ref, l2_ref), send_sems, recv_sems):
            cp.wait_send()
            cp.wait_recv()

    return pl.pallas_call(
        body, name=name, out_shape=(pltpu.HBM(src_thru.shape, BF16), *[pltpu.HBM(lands[0].shape, BF16)] * 3),
        in_specs=(hbm, hbm, hbm, hbm, sem, sem, *_any(len(after))), out_specs=(hbm,) * 4,
        input_output_aliases={0: 0, 1: 1, 2: 2, 3: 3},
        compiler_params=pltpu.CompilerParams(has_side_effects=pltpu.SideEffectType.DATAFLOW_SIDE_EFFECTING),
    )(src_thru, *lands, send_sems, recv_sems, *after)[1:]


def pair_fill_exchange(halves):
    n = len(halves)

    def copies(bufs, sems, core):
        x, y, c = _place()
        out = []
        for ti in range(n):
            rh = bufs[ti].shape[0] // 2
            rows = bufs[ti].at[pl.ds((c if core == "mine" else 1 - c) * rh, rh), :]
            out.append(pltpu.make_async_remote_copy(src_ref=rows, dst_ref=rows, send_sem=sems[0].at[ti], recv_sem=sems[1].at[ti],
                                                    device_id=(x, y, 1 - c), device_id_type=MESH))
        return out

    def start(ins, bufs, sems):
        for cp in copies(bufs, sems, "mine"):
            cp.start()

    def finish(ins, bufs, sems):
        for cp in copies(bufs, sems, "mine"):
            cp.wait_send()
        for cp in copies(bufs, sems, "sibling's"):
            cp.wait_recv()

    return dict(ins=list(halves), out_shape=[jax.ShapeDtypeStruct(a.shape, a.dtype) for a in halves],
                aliases={i: i for i in range(n)}, start=start, finish=finish,
                scratch=[pltpu.SemaphoreType.DMA((n,)), pltpu.SemaphoreType.DMA((n,))])


def pair_gather(halves, name):
    return run_exchange(pair_fill_exchange(halves), name)


def reduce_small(dm_f1, dm_mix, dm_gate, dm_f2, loss_blk, name):
    def body(f1_ref, mix_ref, gate_ref, f2_ref, l_ref, tot_ref, rows_ref, fin_ref):
        rows_ref[...] = jnp.zeros_like(rows_ref)
        tot_ref[...] = jnp.zeros_like(tot_ref)
        mod_src = [(f1_ref, 0), (f1_ref, 1), (f1_ref, 2), (mix_ref, 0), (mix_ref, 1), (gate_ref, 2),
                   (f2_ref, 0), (f2_ref, 1), (f2_ref, 2)]
        norm_src = [(f1_ref, 3), (mix_ref, 3), (f2_ref, 3)]
        for l in range(2):
            for k, (ref, r) in enumerate(mod_src + norm_src):
                lat = ref[0, l, 0, r:r + 1, :]
                ctx = ref[0, l, 1, r:r + 1, :]
                for dev in range(N_DEV):
                    if dev:
                        lat = lat + ref[dev, l, 0, r:r + 1, :]
                        ctx = ctx + ref[dev, l, 1, r:r + 1, :]
                    if k < N_MOD:
                        rows_ref[l, dev, k:k + 1, :] = ref[dev, l, 0, r:r + 1, :]
                if k < N_MOD:
                    rows_ref[l, N_DEV, k:k + 1, :] = ctx
                tot_ref[l, k:k + 1, :] = lat + ctx
        acc = l_ref[0]
        for dev in range(1, N_DEV):
            acc = acc + l_ref[dev]
        loss = (0.5 / D) * jnp.sum(acc[1:2, :], axis=1, keepdims=True)
        row = lax.broadcasted_iota(jnp.int32, (8, D), 0)
        fin_ref[...] = jnp.where(row == 0, acc[0:1, :], loss)

    return pl.pallas_call(
        body, name=name,
        out_shape=[jax.ShapeDtypeStruct((2, 16, D), F32), jax.ShapeDtypeStruct((2, 16, 16, D), F32),
                   jax.ShapeDtypeStruct((8, D), F32)],
        compiler_params=_params(),
    )(dm_f1, dm_mix, dm_gate, dm_f2, loss_blk)


def rope_tables(t, s):
    rows = t // GRID_W
    row = jnp.repeat(jnp.arange(rows), GRID_W).astype(F32)
    col = jnp.tile(jnp.arange(GRID_W), rows).astype(F32)
    inv = ROPE_BASE ** (-jnp.arange(0, HEAD // 2, 2, dtype=F32) / (HEAD // 2))
    ang = jnp.concatenate([row[:, None] * inv, col[:, None] * inv], axis=-1)
    cos, sin = jnp.cos(ang), jnp.sin(ang)
    cos = jnp.concatenate([jnp.tile(cos, (1, 4)), jnp.ones((s - t, BLK), F32)], axis=0)
    sin = jnp.concatenate([jnp.tile(jnp.concatenate([-sin, sin], axis=1), (1, 2)), jnp.zeros((s - t, BLK), F32)], axis=0)
    return cos, sin


BIG = ("ffn1_in", "ffn1_out", "w_in", "w_out", "ffn2_in", "ffn2_out")
GROUPS = dict(ffn1=("ffn1_in", "ffn1_out"), mix=("w_in", "w_out"), ffn2=("ffn2_in", "ffn2_out"))
GATHER_BEHIND = {("ffn1", 0): [("w_in", 0), ("ffn2_out", 0), ("ffn1_out", 1)], ("proj", 0): [("w_out", 0)],
                 ("mix", 0): [("ffn2_in", 0)], ("ffn2", 0): [("ffn1_in", 1), ("w_in", 1)],
                 ("ffn1", 1): [("ffn2_in", 1), ("w_out", 1)], ("mix", 1): [("ffn2_out", 1)]}


def _slot_major(name, g):
    if name == "w_in":
        return jnp.stack(jnp.split(g, N_SLOT, axis=1), axis=0)
    if name in ("ffn1_in", "ffn2_in"):
        return g
    return g.reshape(N_SLOT, g.shape[0] // N_SLOT, g.shape[1])


def _whole_weight(name, buf):
    if name == "w_in":
        return buf.transpose(1, 0, 2).reshape(D, PROJ_W)
    if name in ("ffn1_in", "ffn2_in"):
        return buf
    return buf.reshape(-1, buf.shape[2])


def local_step(x1, ctx1, target, mods, norms, nfinal, placed, w_pool, pool_scale, sink, place, small_blocks):
    t, s = x1.shape[0], x1.shape[0] + ctx1.shape[0]
    n_lat = t // TM
    cos, sin = rope_tables(t, s)
    tables = mix_tables(t, s)
    wts ={name: list(pair) for name, pair in placed.items()}

    def gather(tensors):
        return gather_exchange([wts[name][l] for name, l in tensors])

    def gathered(tensors, arrays):
        for (name, l), whole in zip(tensors, arrays):
            wts[name][l] = whole

    def weight(name, l):
        return _whole_weight(name, wts[name][l])

    def fwd_ex(grp, l):
        groups = GATHER_BEHIND.get((grp, l))
        return (groups, gather(groups)) if groups else (None, None)

    h = jnp.concatenate([x1, ctx1], axis=0)
    saved = []
    for l in range(2):
        h0 = h
        groups, ex = fwd_ex("ffn1", l)
        (h1, ab1, f1), got = ffn_fwd(h0, mods, norms[0], weight("ffn1_in", l), weight("ffn1_out", l), l, 0, n_lat, f"ffn1_fwd_{l}", ex)
        gathered(groups or [], got)
        groups, ex = fwd_ex("proj", l)
        (u, q, k, v), got = proj_fwd(h1, mods, norms[1], weight("w_in", l), cos, sin, l, n_lat, f"proj_fwd_{l}", ex)
        gathered(groups or [], got)
        groups, ex = fwd_ex("mix", l)
        (h2, cat, lse, mo), got = mix_fwd(h1, q, k, v, u, w_pool, pool_scale, sink, weight("w_out", l), mods, tables, l, t,
                                          f"mix_fwd_{l}", ex)
        gathered(groups or [], got)
        groups, ex = fwd_ex("ffn2", l)
        (h, ab2, f2), got = ffn_fwd(h2, mods, norms[2], weight("ffn2_in", l), weight("ffn2_out", l), l, 6, n_lat, f"ffn2_fwd_{l}", ex)
        gathered(groups or [], got)
        saved.append((h0, ab1, f1, h1, u, q, k, v, cat, lse, mo, h2, ab2, f2))
    dh, loss_blk = loss_head(h, target, nfinal, t, "loss_head")

    halves = {name: [None, None] for name in BIG}
    pending = []

    def summed_in_pair(grp, l, name_a, g_a, name_b, wgrad_b):
        g_b, got_a = wgrad_b(pair_exchange([_slot_major(name_a, g_a[1])]))
        sum_a, got_b = pair_sum(_slot_major(name_a, g_a[0]), got_a[0], place, f"pair_sum_{name_a}_{l}",
                                pair_exchange([_slot_major(name_b, g_b[1])]))
        sums = {name_a: sum_a, name_b: pair_sum(_slot_major(name_b, g_b[0]), got_b[0], place, f"pair_sum_{name_b}_{l}")}
        pending.append((grp, l, [sums[n] for n in GROUPS[grp]]))

    lacking = []

    def riders():
        return (scatter_exchange([p16 for _, p16 in pending[0][2]]) if pending else None,
                pair_fill_exchange([halves[name][l] for name, l in lacking]) if lacking else None)

    def carried(got, exs):
        got, filled = split_outputs(got, *exs)
        for (name, l), whole in zip(list(lacking), filled):
            halves[name][l] = whole
            lacking.remove((name, l))
        if pending:
            grp, l, pairs = pending.pop(0)
            for i, name in enumerate(GROUPS[grp]):
                halves[name][l] = chip_sum(pairs[i][0], got[3 * i:3 * i + 3], place, f"chip_sum_{name}_{l}")
                lacking.append((name, l))

    small = [None, None]
    for l in (1, 0):
        h0, ab1, f1, h1, u, q, k, v, cat, lse, mo, h2, ab2, f2 = saved[l]
        exs = riders()
        (dh, dab, df, n, act, dm_f2), got = ffn_bwd(h2, ab2, f2, dh, mods, norms[2], weight("ffn2_in", l), weight("ffn2_out", l),
                                                    l, 6, n_lat, f"ffn2_bwd_{l}", both(*exs))
        carried(got, exs)
        g_in, _ = wgrad(n, dab, D, FF_COLS, FF_COLS, f"ffn2_in_wgrad_{l}")
        summed_in_pair("ffn2", l, "ffn2_in", g_in, "ffn2_out",
                       lambda ex, a=act, b=df: wgrad(a, b, D_FF // 2, D, None, f"ffn2_out_wgrad_{l}", ex))
        exs = riders()
        (dq, dk, dv, du, dmo, dwp, dps, dsink, dm_gate), got = mix_bwd(
            dh, mo, q, k, v, u, lse, w_pool, pool_scale, sink, weight("w_out", l), mods, tables, l, t, f"mix_bwd_{l}", both(*exs))
        carried(got, exs)
        g_wo, _ = wgrad(cat, dmo, POOL_W + ATTN_W, D, None, f"w_out_wgrad_{l}")
        dh, dp, n, dm_mix = proj_bwd(h1, du, dq, dk, dv, dh, mods, norms[1], weight("w_in", l), cos, sin, l, n_lat, f"proj_bwd_{l}")
        summed_in_pair("mix", l, "w_out", g_wo, "w_in",
                       lambda ex, a=n, b=dp: wgrad(a, b, D, PROJ_W // 2, None, f"w_in_wgrad_{l}", ex))
        exs = riders()
        (dh, dab, df, n, act, dm_f1), got = ffn_bwd(h0, ab1, f1, dh, mods, norms[0], weight("ffn1_in", l), weight("ffn1_out", l),
                                                    l, 0, n_lat, f"ffn1_bwd_{l}", both(*exs))
        carried(got, exs)
        small[l] = dict(dm_f1=dm_f1, dm_mix=dm_mix, dm_gate=dm_gate, dm_f2=dm_f2, dwp=dwp, dps=dps, dsink=dsink)
        if l:
            g_in, _ = wgrad(n, dab, D, FF_COLS, FF_COLS, f"ffn1_in_wgrad_{l}")
            summed_in_pair("ffn1", l, "ffn1_in", g_in, "ffn1_out",
                           lambda ex, a=act, b=df: wgrad(a, b, D_FF // 2, D, None, f"ffn1_out_wgrad_{l}", ex))
    g_out, _ = wgrad(act, df, D_FF // 2, D, None, "ffn1_out_wgrad_0")
    got = run_exchange(pair_exchange([_slot_major("ffn1_out", g_out[1])]), "pair_exchange_ffn1_out_0")
    p32, p16 = pair_sum(_slot_major("ffn1_out", g_out[0]), got[0], place, "pair_sum_ffn1_out_0")
    riding = (scatter_exchange([p16]), gather8_exchange(small_blocks(small, loss_blk)),
              pair_fill_exchange([halves[name][l] for name, l in lacking]))
    g_in, got = wgrad(n, dab, D, FF_COLS, FF_COLS, "ffn1_in_wgrad_0", both(*riding))
    got, small_all, filled = split_outputs(got, *riding)
    for (name, l), whole in zip(lacking, filled):
        halves[name][l] = whole
    (halves["ffn1_out"][0],) = pair_gather([chip_sum(p32, got, place, "chip_sum_ffn1_out_0")], "pair_gather_ffn1_out_0")
    got = run_exchange(pair_exchange([_slot_major("ffn1_in", g_in[1])]), "pair_exchange_ffn1_in_0")
    return dh[:t], halves, pair_sum(_slot_major("ffn1_in", g_in[0]), got[0], place, "pair_sum_ffn1_in_0"), small_all


def _silu_grad(z):
    sg = jax.nn.sigmoid(z)
    return sg * (1 + z * (1 - sg))


def kernel(x, c, ctx, c_ctx, w_mod, b_mod, norm_ffn1, w_ffn1_in, w_ffn1_out, norm_mix, w_in, w_pool, pool_scale, sink, w_out, norm_ffn2, w_ffn2_in, w_ffn2_out, norm_final, loss_target, m_c_ctx, m_w_mod, m_b_mod, m_norm_ffn1, m_w_ffn1_in, m_w_ffn1_out, m_norm_mix, m_w_in, m_w_pool, m_pool_scale, m_sink, m_w_out, m_norm_ffn2, m_w_ffn2_in, m_w_ffn2_out, m_norm_final, v_c_ctx, v_w_mod, v_b_mod, v_norm_ffn1, v_w_ffn1_in, v_w_ffn1_out, v_norm_mix, v_w_in, v_w_pool, v_pool_scale, v_sink, v_w_out, v_norm_ffn2, v_w_ffn2_in, v_w_ffn2_out, v_norm_final):
    px, py, pc = _place()
    slot, me = 2 * px + py, 4 * px + 2 * py + pc
    n_grp = len(POOL_WINDOWS)

    (c_rows,) = all_gather([c.reshape(8, D // 8)], "gather_c")
    c_all = jnp.concatenate([c_rows.reshape(N_DEV, D), c_ctx.reshape(1, D), jnp.zeros((16 - N_DEV - 1, D), F32)], axis=0)
    b_cols = lax.dynamic_slice(b_mod, (0, slot * MOD_COLS), (2, MOD_COLS)).reshape(2, 1, MOD_COLS)
    (mod_parts,) = all_gather([mod_rows(c_all, w_mod, b_cols, "mod_rows")], "gather_mods")
    mods_all = mod_parts[0::2].transpose(1, 2, 0, 3).reshape(2, 16, N_MOD * D)
    mx = lax.dynamic_slice(mods_all, (0, me, 0), (2, 1, N_MOD * D)).reshape(2, N_MOD, D)
    mc = mods_all[:, N_DEV].reshape(2, N_MOD, D)
    pad = jnp.zeros((2, 16 - N_MOD, D), F32)
    mods = jnp.stack([jnp.concatenate([mx, pad], axis=1), jnp.concatenate([mc, pad], axis=1)], axis=1)

    place = jnp.stack([pc, slot]).astype(jnp.int32)
    shards = dict(ffn1_in=w_ffn1_in, ffn1_out=w_ffn1_out, w_in=w_in, w_out=w_out, ffn2_in=w_ffn2_in, ffn2_out=w_ffn2_out)
    first = [("ffn1_in", 0), ("ffn1_out", 0)]
    placed = {name: [None, None] for name in BIG}
    for name, l in first:
        placed[name][l] = cast_place(shards[name], l, place, f"cast_{name}_{l}")
    send_sems, recv_sems, *bufs, token = gather_start([placed[name][l] for name, l in first], "gather_first_start")
    others = [(name, l) for name in BIG for l in range(2) if (name, l) not in first]
    for name, l in others:
        placed[name][l] = cast_place(shards[name], l, place, f"cast_{name}_{l}")
    bufs = gather_wait(send_sems, recv_sems, bufs, [placed[name][l] for name, l in others], "gather_first_wait")
    for (name, l), whole in zip(first, run_exchange(pass_on_exchange(bufs), "gather_first_pass_on")):
        placed[name][l] = whole
    norms = [g.reshape(2, 1, D) for g in (norm_ffn1, norm_mix, norm_ffn2)]
    row_sums = ("dm_f1", "dm_mix", "dm_gate", "dm_f2")

    def small_blocks(small, loss_blk):
        stacked = {k: jnp.stack([small[0][k], small[1][k]]) for k in row_sums + ("dwp", "dps", "dsink")}
        return ([stacked[k].reshape(32, D) for k in row_sums]
                + [stacked["dwp"].reshape(2 * n_grp * GROUP, GROUP), stacked["dps"].reshape(16, POOL_W),
                   stacked["dsink"].reshape(16, BLK), loss_blk])

    dx, halves, last_pair, small_all = local_step(x[0], ctx[0], loss_target[0], mods, norms, norm_final.reshape(1, D), placed,
                                                   w_pool.astype(BF16), pool_scale.reshape(2, 1, POOL_W), sink, place, small_blocks)
    grads = {}

    *g_dm, g_dwp, g_dps, g_dsink, g_loss = small_all
    tot, rows, fin = reduce_small(*[g.reshape(N_DEV, 2, 2, 8, D) for g in g_dm], g_loss, "reduce_small")
    s_dwp, s_dps, s_dsink = sum8([g_dwp, g_dps, g_dsink], "sum_pool_sink")
    grads.update(
        w_pool=s_dwp.reshape(2, n_grp, GROUP, GROUP), pool_scale=s_dps.reshape(2, 8, POOL_W)[:, 0],
        sink=s_dsink.reshape(2, 8, BLK)[:, 0, :N_HEADS], b_mod=tot[:, :N_MOD].reshape(2, N_MOD * D),
        norm_ffn1=tot[:, N_MOD], norm_mix=tot[:, N_MOD + 1], norm_ffn2=tot[:, N_MOD + 2], norm_final=fin[0])
    loss = fin[1, 0]

    dmod_cols = lax.dynamic_slice(rows[:, :, :N_MOD, :].reshape(2, 16, N_MOD * D), (0, 0, slot * MOD_COLS), (2, 16, MOD_COLS))
    grads["w_mod"], dc = mod_grads(c_all, dmod_cols, w_mod, "mod_grads")
    (g_dc,) = all_gather([dc], "gather_dc")
    (s_dc,) = sum8([g_dc], "sum_dc")
    (d_c_ctx,) = elementwise(lambda d, z: (0.5 * d * _silu_grad(z),), [s_dc[N_DEV:N_DEV + 1], c_ctx.reshape(1, D)], [F32], "c_ctx_grad")
    send_sems, recv_sems, src_thru, *lands, token = scatter_start(last_pair[1], "scatter_last_start")
    grads["c_ctx"] = d_c_ctx.reshape(D) + token[0, :1]

    given = dict(c_ctx=(c_ctx, m_c_ctx, v_c_ctx), w_mod=(w_mod, m_w_mod, v_w_mod), b_mod=(b_mod, m_b_mod, v_b_mod),
                 norm_ffn1=(norm_ffn1, m_norm_ffn1, v_norm_ffn1), w_ffn1_in=(w_ffn1_in, m_w_ffn1_in, v_w_ffn1_in),
                 w_ffn1_out=(w_ffn1_out, m_w_ffn1_out, v_w_ffn1_out), norm_mix=(norm_mix, m_norm_mix, v_norm_mix),
                 w_in=(w_in, m_w_in, v_w_in), w_pool=(w_pool, m_w_pool, v_w_pool),
                 pool_scale=(pool_scale, m_pool_scale, v_pool_scale), sink=(sink, m_sink, v_sink), w_out=(w_out, m_w_out, v_w_out),
                 norm_ffn2=(norm_ffn2, m_norm_ffn2, v_norm_ffn2), w_ffn2_in=(w_ffn2_in, m_w_ffn2_in, v_w_ffn2_in),
                 w_ffn2_out=(w_ffn2_out, m_w_ffn2_out, v_w_ffn2_out), norm_final=(norm_final, m_norm_final, v_norm_final))
    shard = {(name, l): halves[name][l] for name in BIG for l in range(2)}

    def update(name):
        w, m, v = given[name]
        if name in BIG or name[2:] in BIG:
            key = name if name in BIG else name[2:]
            return adamw_layers(w, shard[key, 0], shard[key, 1], m, v, f"adamw_{name}")
        return [grads[name], *adamw(w, grads[name], m, v, f"adamw_{name}")]

    done = {name: update(name) for name in given if name != "w_ffn1_in"}
    got = scatter_wait(send_sems, recv_sems, src_thru, lands, [done[name][3] for name in done if name[2:] in BIG or name in BIG]
                       + [done["w_mod"][3]], "scatter_last_wait")
    (shard["ffn1_in", 0],) = pair_gather([chip_sum(last_pair[0], got, place, "chip_sum_ffn1_in_0")], "grad_pair_gather_last")
    done["w_ffn1_in"] = update("w_ffn1_in")
    return (loss, dx[None], *[done[name][i] for i in range(4) for name in given])
```

```python
import functools

import jax
import jax.numpy as jnp
from jax import lax
from jax.experimental import pallas as pl
from jax.experimental.pallas import tpu as pltpu

F32, BF16 = jnp.float32, jnp.bfloat16
D = 1024
D_FF = 2816
N_SLOT = 4
FF_COLS = 2 * D_FF // N_SLOT
N_MOD = 9
MOD_COLS = N_MOD * D // N_SLOT
POOL_W, ATTN_W, KV_W = 512, 512, 128
PROJ_W = POOL_W + ATTN_W + 2 * KV_W
N_HEADS, Q_GROUP, HEAD = 8, 4, 64
GROUP = 128
POOL_WINDOWS = (2, 4, 8, 16)
BLK = 128
QB = 256
WIN = QB + 2 * BLK
GRID_W = 64
ROPE_BASE = 10000.0
EPS = 1e-6
NEG_INF = -1e30
TM = 256
N_DEV = 8
VMEM_LIMIT_BYTES = 56 * 1024 * 1024
ADAM_LR, ADAM_B1, ADAM_B2, ADAM_EPS, ADAM_WD, ADAM_STEP = 0.001, 0.9, 0.999, 1e-08, 0.01, 10
MESH = pl.DeviceIdType.MESH
NT = (((1,), (1,)), ((), ()))
TN = (((0,), (0,)), ((), ()))


def _params(*sem):
    return pltpu.CompilerParams(dimension_semantics=sem, vmem_limit_bytes=VMEM_LIMIT_BYTES)


def _whole(shape, lead=()):
    idx = tuple(lead) + (0,) * len(shape)
    return pl.BlockSpec((None,) * len(lead) + tuple(shape), lambda *_: idx, pipeline_mode=pl.Buffered(1))


def _rows(cols, tm=TM):
    return pl.BlockSpec((tm, cols), lambda i: (i, 0))


def _mods_spec(layer, n_lat):
    return pl.BlockSpec((None, None, 16, D), lambda i: (layer, (i >= n_lat).astype(jnp.int32), 0, 0))


def _acc_spec(n_lat):
    return pl.BlockSpec((None, 8, D), lambda i: ((i >= n_lat).astype(jnp.int32), 0, 0))


def _dot(a, b):
    return jnp.dot(a, b, preferred_element_type=F32)


def _dotg(a, b, dims):
    return lax.dot_general(a, b, dims, preferred_element_type=F32)


def _sum0(v):
    return jnp.sum(v, axis=0, keepdims=True)


def _norm_mod(h, g, shift, scale):
    r = lax.rsqrt(jnp.mean(h * h, axis=-1, keepdims=True) + EPS)
    xhat = h * r
    y = xhat * g
    return y * (1 + scale) + shift, xhat, r, y


def _norm_mod_bwd(dn, xhat, r, y, g, scale):
    dy = dn * (1 + scale)
    dx = dy * g
    dh = r * (dx - xhat * jnp.mean(dx * xhat, axis=-1, keepdims=True))
    return _sum0(dn), _sum0(dn * y), _sum0(dy * xhat), dh


def _swap_halves(v):
    w = v.shape[1]
    lane = lax.broadcasted_iota(jnp.int32, v.shape, 1)
    return jnp.where(lane % HEAD < HEAD // 2, pltpu.roll(v, w - HEAD // 2, axis=1), pltpu.roll(v, HEAD // 2, axis=1))


def _tile_lanes(t, width):
    return t if width == t.shape[1] else jnp.concatenate([t] * (width // t.shape[1]), axis=1)


def _rope(v, cos, sin):
    return v * _tile_lanes(cos, v.shape[1]) + _swap_halves(v) * _tile_lanes(sin, v.shape[1])


def _unrope(g, cos, sin):
    return g * _tile_lanes(cos, g.shape[1]) + _swap_halves(g * _tile_lanes(sin, g.shape[1]))


def ffn_fwd(h, mods, g, w4, wo, layer, k0, n_lat, name, ex=None):
    s = h.shape[0]

    def body(h_ref, m_ref, g_ref, w_ref, wo_ref, ho_ref, ab_ref, f_ref):
        hh = h_ref[...]
        n, _, _, _ = _norm_mod(hh, g_ref[...], m_ref[k0:k0 + 1, :], m_ref[k0 + 1:k0 + 2, :])
        nb = n.astype(BF16)
        acc = jnp.zeros((TM, D), F32)
        for j in range(2):
            a = _dot(nb, w_ref[j])
            b = _dot(nb, w_ref[2 + j])
            ab_ref[:, j * FF_COLS:(j + 1) * FF_COLS] = a.astype(BF16)
            ab_ref[:, (2 + j) * FF_COLS:(3 + j) * FF_COLS] = b.astype(BF16)
            act = (a * jax.nn.sigmoid(a) * b).astype(BF16)
            acc = acc + _dot(act, wo_ref[j * FF_COLS:(j + 1) * FF_COLS, :])
        f_ref[...] = acc
        ho_ref[...] = hh + 0.5 * m_ref[k0 + 2:k0 + 3, :] * acc

    return _grid_call(
        body, name, s // TM,
        [_rows(D), _mods_spec(layer, n_lat), _whole((1, D), (layer,)), _whole((N_SLOT, D, FF_COLS)), _whole((D_FF, D))],
        [_rows(D), _rows(2 * D_FF), _rows(D)],
        [jax.ShapeDtypeStruct((s, D), F32), jax.ShapeDtypeStruct((s, 2 * D_FF), BF16), jax.ShapeDtypeStruct((s, D), F32)],
        (h, mods, g, w4, wo), "parallel", ex)


def ffn_bwd(h, ab, f, dh, mods, g, w4, wo, layer, k0, n_lat, name, ex=None):
    s = h.shape[0]

    def body(h_ref, ab_ref, f_ref, dh_ref, m_ref, g_ref, w_ref, wo_ref, dhi_ref, dab_ref, df_ref, n_ref, act_ref, dm_ref):
        i = pl.program_id(0)

        @pl.when((i == 0) | (i == n_lat))
        def _():
            dm_ref[...] = jnp.zeros_like(dm_ref)

        hh, dho, gg = h_ref[...], dh_ref[...], g_ref[...]
        scale, gate = m_ref[k0 + 1:k0 + 2, :], m_ref[k0 + 2:k0 + 3, :]
        n, xhat, r, y = _norm_mod(hh, gg, m_ref[k0:k0 + 1, :], scale)
        n_ref[...] = n.astype(BF16)
        dgate = _sum0(dho * (0.5 * f_ref[...]))
        dfb = ((0.5 * gate) * dho).astype(BF16)
        df_ref[...] = dfb
        dn = jnp.zeros((TM, D), F32)
        for j in range(2):
            a = ab_ref[:, j * FF_COLS:(j + 1) * FF_COLS].astype(F32)
            b = ab_ref[:, (2 + j) * FF_COLS:(3 + j) * FF_COLS].astype(F32)
            sg = jax.nn.sigmoid(a)
            sa = a * sg
            act_ref[:, j * FF_COLS:(j + 1) * FF_COLS] = (sa * b).astype(BF16)
            dact = _dotg(dfb, wo_ref[j * FF_COLS:(j + 1) * FF_COLS, :], NT)
            da = (dact * b * (sg * (1 + a * (1 - sg)))).astype(BF16)
            db = (dact * sa).astype(BF16)
            dab_ref[:, j * FF_COLS:(j + 1) * FF_COLS] = da
            dab_ref[:, (2 + j) * FF_COLS:(3 + j) * FF_COLS] = db
            dn = dn + _dotg(da, w_ref[j], NT) + _dotg(db, w_ref[2 + j], NT)
        dsh, dsc, dg, dhn = _norm_mod_bwd(dn, xhat, r, y, gg, scale)
        dhi_ref[...] = dho + dhn
        dm_ref[0:1, :] += dsh
        dm_ref[1:2, :] += dsc
        dm_ref[2:3, :] += dgate
        dm_ref[3:4, :] += dg

    return _grid_call(
        body, name, s // TM,
        [_rows(D), _rows(2 * D_FF), _rows(D), _rows(D), _mods_spec(layer, n_lat), _whole((1, D), (layer,)),
         _whole((N_SLOT, D, FF_COLS)), _whole((D_FF, D))],
        [_rows(D), _rows(2 * D_FF), _rows(D), _rows(D), _rows(D_FF), _acc_spec(n_lat)],
        [jax.ShapeDtypeStruct((s, D), F32), jax.ShapeDtypeStruct((s, 2 * D_FF), BF16), jax.ShapeDtypeStruct((s, D), BF16),
         jax.ShapeDtypeStruct((s, D), BF16), jax.ShapeDtypeStruct((s, D_FF), BF16), jax.ShapeDtypeStruct((2, 8, D), F32)],
        (h, ab, f, dh, mods, g, w4, wo), "arbitrary", ex)


def _token_tile(s, limit=2176):
    return max(ts for ts in range(16, limit + 1, 16) if s % ts == 0)


def wgrad(a, b, tk, tn, slot_cols, name, ex=None):
    s, k = a.shape
    n = b.shape[1]
    ts = _token_tile(s)
    steps = s // ts

    def body(a_ref, b_ref, o_ref, o16_ref):
        r = _dotg(a_ref[...], b_ref[...], TN)
        si = pl.program_id(2)

        @pl.when(si == 0)
        def _():
            o_ref[...] = r

        @pl.when(si > 0)
        def _():
            o_ref[...] += r

        @pl.when(si == steps - 1)
        def _():
            o16_ref[...] = o_ref[...].astype(BF16)

    if slot_cols is None:
        shape, spec = (k, n), pl.BlockSpec((tk, tn), lambda i, j, si: (i, j))
    else:
        per = slot_cols // tn
        shape, spec = (n // slot_cols, k, slot_cols), pl.BlockSpec((None, tk, tn), lambda i, j, si: (lax.div(j, per), i, lax.rem(j, per)))
    return _grid_call(
        body, name, (k // tk, n // tn, steps),
        [pl.BlockSpec((ts, tk), lambda i, j, si: (si, i)), pl.BlockSpec((ts, tn), lambda i, j, si: (si, j))], [spec, spec],
        [jax.ShapeDtypeStruct(shape, F32), jax.ShapeDtypeStruct(shape, BF16)], (a, b), ("parallel", "parallel", "arbitrary"), ex)


def proj_fwd(h, mods, g, w_in, cos, sin, layer, n_lat, name, ex=None):
    s = h.shape[0]

    def body(h_ref, m_ref, g_ref, w_ref, cos_ref, sin_ref, u_ref, q_ref, k_ref, v_ref):
        n, _, _, _ = _norm_mod(h_ref[...], g_ref[...], m_ref[3:4, :], m_ref[4:5, :])
        p = _dot(n.astype(BF16), w_ref[...])
        cs, sn = cos_ref[...], sin_ref[...]
        u_ref[...] = p[:, :POOL_W]
        q_ref[...] = (_rope(p[:, POOL_W:POOL_W + ATTN_W], cs, sn) * HEAD ** -0.5).astype(BF16)
        k_ref[...] = _rope(p[:, POOL_W + ATTN_W:POOL_W + ATTN_W + KV_W], cs, sn).astype(BF16)
        v_ref[...] = p[:, POOL_W + ATTN_W + KV_W:].astype(BF16)

    return _grid_call(
        body, name, s // TM,
        [_rows(D), _mods_spec(layer, n_lat), _whole((1, D), (layer,)), _whole((D, PROJ_W)), _rows(BLK), _rows(BLK)],
        [_rows(POOL_W), _rows(ATTN_W), _rows(KV_W), _rows(KV_W)],
        [jax.ShapeDtypeStruct((s, POOL_W), F32), jax.ShapeDtypeStruct((s, ATTN_W), BF16),
         jax.ShapeDtypeStruct((s, KV_W), BF16), jax.ShapeDtypeStruct((s, KV_W), BF16)],
        (h, mods, g, w_in, cos, sin), "parallel", ex)


def proj_bwd(h, du, dq, dk, dv, dh, mods, g, w_in, cos, sin, layer, n_lat, name):
    s = h.shape[0]

    def body(h_ref, du_ref, dq_ref, dk_ref, dv_ref, dh_ref, m_ref, g_ref, w_ref, cos_ref, sin_ref,
             dhi_ref, dp_ref, n_ref, dm_ref):
        i = pl.program_id(0)

        @pl.when((i == 0) | (i == n_lat))
        def _():
            dm_ref[...] = jnp.zeros_like(dm_ref)

        gg, scale = g_ref[...], m_ref[4:5, :]
        n, xhat, r, y = _norm_mod(h_ref[...], gg, m_ref[3:4, :], scale)
        n_ref[...] = n.astype(BF16)
        cs, sn = cos_ref[...], sin_ref[...]
        dp = jnp.concatenate([du_ref[...], _unrope(dq_ref[...], cs, sn) * HEAD ** -0.5, _unrope(dk_ref[...], cs, sn),
                              dv_ref[...]], axis=1).astype(BF16)
        dp_ref[...] = dp
        dsh, dsc, dg, dhn = _norm_mod_bwd(_dotg(dp, w_ref[...], NT), xhat, r, y, gg, scale)
        dhi_ref[...] = dh_ref[...] + dhn
        dm_ref[0:1, :] += dsh
        dm_ref[1:2, :] += dsc
        dm_ref[3:4, :] += dg

    return pl.pallas_call(
        body, name=name, grid=(s // TM,),
        in_specs=[_rows(D), _rows(POOL_W), _rows(ATTN_W), _rows(KV_W), _rows(KV_W), _rows(D), _mods_spec(layer, n_lat),
                  _whole((1, D), (layer,)), _whole((D, PROJ_W)), _rows(BLK), _rows(BLK)],
        out_specs=[_rows(D), _rows(PROJ_W), _rows(D), _acc_spec(n_lat)],
        out_shape=[jax.ShapeDtypeStruct((s, D), F32), jax.ShapeDtypeStruct((s, PROJ_W), BF16),
                   jax.ShapeDtypeStruct((s, D), BF16), jax.ShapeDtypeStruct((2, 8, D), F32)],
        compiler_params=_params("arbitrary"),
    )(h, du, dq, dk, dv, dh, mods, g, w_in, cos, sin)


def _window(i, s):
    return pl.multiple_of(jnp.clip(i * QB - BLK, 0, s - WIN), BLK)


def mix_tables(t, s):
    n_lat = t // QB
    blocks = jnp.array([0, 1, n_lat - 1] + list(range(n_lat, s // QB)))[:, None, None]
    ws = jnp.clip(blocks * QB - BLK, 0, s - WIN)
    q = blocks * QB + jnp.arange(QB)[None, :, None]
    k = ws + jnp.arange(WIN)[None, None, :]
    is_lat = blocks < n_lat
    local = jnp.where(is_lat & (k < t) & (jnp.abs(k - q) <= BLK), 0.0, NEG_INF).astype(F32)
    bias = jnp.concatenate([local, jnp.zeros(local.shape[:2] + (s - t,), F32)], axis=2)
    seq_lo, seq_hi = jnp.where(is_lat, 0, t), jnp.where(is_lat, t, s)
    bands, counts = [], []
    for w in POOL_WINDOWS:
        lo, hi = jnp.maximum(q - w // 2, seq_lo), jnp.minimum(q + w - w // 2, seq_hi)
        bands.append((k >= lo) & (k < hi))
        counts.append((hi - lo).astype(F32))
    band = jnp.stack(bands, axis=1).astype(BF16)
    count = jnp.concatenate(counts + [jnp.ones(counts[0].shape[:2] + (BLK - len(counts),), F32)], axis=2)
    return dict(bias=bias, band=band, band_t=band.transpose(0, 1, 3, 2), count=count)


def _case_spec(table, n_lat_blk):
    def kind(i):
        return jnp.where(i < n_lat_blk - 1, jnp.minimum(i, 1), i - n_lat_blk + 3)

    shape = table.shape[1:]
    return pl.BlockSpec((None,) + shape, lambda i: (kind(i),) + (0,) * len(shape))


def _split_dot(band, v):
    return _dot(band, v.astype(BF16))


def _pooled(u_ref, band_ref, cnt_ref, i, ws, gi):
    cols = slice(gi * GROUP, (gi + 1) * GROUP)
    mean = _split_dot(band_ref[gi], u_ref[pl.ds(ws, WIN), cols]) / cnt_ref[:, gi:gi + 1]
    return mean - u_ref[pl.ds(pl.multiple_of(i * QB, QB), QB), cols]


def _head_cols(hd):
    return slice(hd * HEAD, (hd + 1) * HEAD)


def _stack_heads(x, hk, first=0):
    return jnp.concatenate([x[:, first + (Q_GROUP * hk + g) * HEAD:first + (Q_GROUP * hk + g + 1) * HEAD]
                            for g in range(Q_GROUP)], axis=0)


def _biased(scores, bias):
    return (scores.reshape(Q_GROUP, QB, -1) + bias).reshape(Q_GROUP * QB, -1)


def _group_column(vals):
    row = lax.broadcasted_iota(jnp.int32, (Q_GROUP * QB, 1), 0)
    out = jnp.full((Q_GROUP * QB, 1), vals[Q_GROUP - 1], F32)
    for g in range(Q_GROUP - 2, -1, -1):
        out = jnp.where(row < (g + 1) * QB, vals[g], out)
    return out


def _lane_place(cols, width=BLK):
    lane = lax.broadcasted_iota(jnp.int32, (cols[0].shape[0], width), 1)
    out = jnp.zeros((cols[0].shape[0], width), F32)
    for hd, c in enumerate(cols):
        out = jnp.where(lane == hd, c, out)
    return out


def mix_fwd(h, q, k, v, u, w_pool, pool_scale, sink, w_out, mods, tables, layer, t, name, ex=None):
    s = h.shape[0]
    n_lat_blk = t // QB

    def body(h_ref, q_ref, k_ref, v_ref, u_ref, wp_ref, ps_ref, sink_ref, wo_ref, m_ref, bias_ref, band_ref, cnt_ref,
             ho_ref, cat_ref, lse_ref, mo_ref):
        i = pl.program_id(0)
        ws = _window(i, s)
        for gi in range(len(POOL_WINDOWS)):
            mixed = _dot(_pooled(u_ref, band_ref, cnt_ref, i, ws, gi).astype(BF16), wp_ref[gi])
            cat_ref[:, gi * GROUP:(gi + 1) * GROUP] = (mixed * ps_ref[:, gi * GROUP:(gi + 1) * GROUP]).astype(BF16)
        bias = bias_ref[...]
        k_all = jnp.concatenate([k_ref[pl.ds(ws, WIN), :], k_ref[t:s, :]], axis=0)
        v_all = jnp.concatenate([v_ref[pl.ds(ws, WIN), :], v_ref[t:s, :]], axis=0)
        lses = []
        for hk in range(N_HEADS // Q_GROUP):
            kv = _head_cols(hk)
            sc = _biased(_dotg(_stack_heads(q_ref[...], hk), k_all[:, kv], NT), bias)
            sk = _group_column([sink_ref[layer, Q_GROUP * hk + g] for g in range(Q_GROUP)])
            m = jnp.maximum(jnp.max(sc, axis=1, keepdims=True), sk)
            e = jnp.exp(sc - m)
            l = jnp.sum(e, axis=1, keepdims=True) + jnp.exp(sk - m)
            o = _dot(e.astype(BF16), v_all[:, kv]) * (1.0 / l)
            lse = m + jnp.log(l)
            for g in range(Q_GROUP):
                hd = Q_GROUP * hk + g
                cat_ref[:, POOL_W + hd * HEAD:POOL_W + (hd + 1) * HEAD] = o[g * QB:(g + 1) * QB].astype(BF16)
                lses.append(lse[g * QB:(g + 1) * QB])
        lse_ref[...] = _lane_place(lses)
        mo = _dot(cat_ref[...], wo_ref[...])
        mo_ref[...] = mo
        ho_ref[...] = h_ref[...] + m_ref[5:6, :] * mo

    blk = lambda cols: _rows(cols, QB)
    return _grid_call(
        body, name, s // QB,
        [blk(D), blk(ATTN_W), _whole((s, KV_W)), _whole((s, KV_W)), _whole((s, POOL_W)),
         _whole((len(POOL_WINDOWS), GROUP, GROUP), (layer,)), _whole((1, POOL_W), (layer,)),
         pl.BlockSpec(memory_space=pltpu.SMEM), _whole((POOL_W + ATTN_W, D)), _mods_spec(layer, n_lat_blk),
         _case_spec(tables["bias"], n_lat_blk), _case_spec(tables["band"], n_lat_blk), _case_spec(tables["count"], n_lat_blk)],
        [blk(D), blk(POOL_W + ATTN_W), blk(BLK), blk(D)],
        [jax.ShapeDtypeStruct((s, D), F32), jax.ShapeDtypeStruct((s, POOL_W + ATTN_W), BF16), jax.ShapeDtypeStruct((s, BLK), F32),
         jax.ShapeDtypeStruct((s, D), F32)],
        (h, q, k, v, u, w_pool, pool_scale, sink, w_out, mods, tables["bias"], tables["band"], tables["count"]), "parallel", ex)


def mix_bwd(dh, mo, q, k, v, u, lse, w_pool, pool_scale, sink, w_out, mods, tables, layer, t, name, ex=None):
    s = dh.shape[0]
    n_lat_blk = t // QB
    n_grp = len(POOL_WINDOWS)

    def body(dh_ref, mo_ref, q_ref, k_ref, v_ref, u_ref, lse_ref, wp_ref, ps_ref, sink_ref, wo_ref, m_ref,
             bias_ref, band_ref, band_t_ref, cnt_ref,
             dq_ref, dk_ref, dv_ref, du_ref, dmo_ref, dwp_ref, dps_ref, dsink_ref, dm_ref):
        i = pl.program_id(0)

        @pl.when(i == 0)
        def _():
            for ref in (dk_ref, dv_ref, du_ref, dwp_ref, dps_ref, dsink_ref):
                ref[...] = jnp.zeros_like(ref)

        @pl.when((i == 0) | (i == n_lat_blk))
        def _():
            dm_ref[...] = jnp.zeros_like(dm_ref)

        ws = _window(i, s)
        here = pl.ds(pl.multiple_of(i * QB, QB), QB)
        dho = dh_ref[...]
        dm_ref[2:3, :] += _sum0(dho * mo_ref[...])
        dmo = (m_ref[5:6, :] * dho).astype(BF16)
        dmo_ref[...] = dmo
        dcat = _dotg(dmo, wo_ref[...], NT)

        for gi in range(n_grp):
            cols = slice(gi * GROUP, (gi + 1) * GROUP)
            pooled = _pooled(u_ref, band_ref, cnt_ref, i, ws, gi).astype(BF16)
            dpo = dcat[:, cols]
            dps_ref[0:1, cols] += _sum0(dpo * _dot(pooled, wp_ref[gi]))
            dmixed = (dpo * ps_ref[:, cols]).astype(BF16)
            dwp_ref[gi] += _dotg(pooled, dmixed, TN)
            dpooled = _dotg(dmixed, wp_ref[gi], NT)
            du_ref[pl.ds(ws, WIN), cols] += _split_dot(band_t_ref[gi], dpooled / cnt_ref[:, gi:gi + 1])
            du_ref[here, cols] -= dpooled

        bias = bias_ref[...]
        k_all = jnp.concatenate([k_ref[pl.ds(ws, WIN), :], k_ref[t:s, :]], axis=0)
        v_all = jnp.concatenate([v_ref[pl.ds(ws, WIN), :], v_ref[t:s, :]], axis=0)
        qq, lse_all = q_ref[...], lse_ref[...]
        dqs, dsinks, dks, dvs = [], [], [], []
        for hk in range(N_HEADS // Q_GROUP):
            kv = _head_cols(hk)
            q4 = _stack_heads(qq, hk)
            lse = jnp.concatenate([lse_all[:, Q_GROUP * hk + g:Q_GROUP * hk + g + 1] for g in range(Q_GROUP)], axis=0)
            p = jnp.exp(_biased(_dotg(q4, k_all[:, kv], NT), bias) - lse)
            do = _stack_heads(dcat, hk, POOL_W).astype(BF16)
            dp = _dotg(do, v_all[:, kv], NT)
            delta = jnp.sum(p * dp, axis=1, keepdims=True)
            ds = (p * (dp - delta)).astype(BF16)
            sk = _group_column([sink_ref[layer, Q_GROUP * hk + g] for g in range(Q_GROUP)])
            dsk = -jnp.exp(sk - lse) * delta
            dq = _dot(ds, k_all[:, kv])
            for g in range(Q_GROUP):
                dqs.append(dq[g * QB:(g + 1) * QB])
                dsinks.append(_sum0(dsk[g * QB:(g + 1) * QB]))
            dks.append(_dotg(ds, q4, TN))
            dvs.append(_dotg(p.astype(BF16), do, TN))
        dq_ref[...] = jnp.concatenate(dqs, axis=1)
        dk, dv = jnp.concatenate(dks, axis=1), jnp.concatenate(dvs, axis=1)
        dk_ref[pl.ds(ws, WIN), :] += dk[:WIN]
        dv_ref[pl.ds(ws, WIN), :] += dv[:WIN]
        dk_ref[t:s, :] += dk[WIN:]
        dv_ref[t:s, :] += dv[WIN:]
        dsink_ref[0:1, :] += _lane_place(dsinks)

    blk = lambda cols: _rows(cols, QB)
    full = lambda shape: pl.BlockSpec(shape, lambda i: (0,) * len(shape))
    return _grid_call(
        body, name, s // QB,
        [blk(D), blk(D), blk(ATTN_W), _whole((s, KV_W)), _whole((s, KV_W)), _whole((s, POOL_W)),
         blk(BLK), _whole((n_grp, GROUP, GROUP), (layer,)), _whole((1, POOL_W), (layer,)),
         pl.BlockSpec(memory_space=pltpu.SMEM), _whole((POOL_W + ATTN_W, D)), _mods_spec(layer, n_lat_blk)]
        + [_case_spec(tables[key], n_lat_blk) for key in ("bias", "band", "band_t", "count")],
        [blk(ATTN_W), full((s, KV_W)), full((s, KV_W)), full((s, POOL_W)), blk(D),
         full((n_grp, GROUP, GROUP)), full((8, POOL_W)), full((8, BLK)), _acc_spec(n_lat_blk)],
        [jax.ShapeDtypeStruct((s, ATTN_W), F32), jax.ShapeDtypeStruct((s, KV_W), F32),
         jax.ShapeDtypeStruct((s, KV_W), F32), jax.ShapeDtypeStruct((s, POOL_W), F32),
         jax.ShapeDtypeStruct((s, D), BF16), jax.ShapeDtypeStruct((n_grp, GROUP, GROUP), F32),
         jax.ShapeDtypeStruct((8, POOL_W), F32), jax.ShapeDtypeStruct((8, BLK), F32), jax.ShapeDtypeStruct((2, 8, D), F32)],
        (dh, mo, q, k, v, u, lse, w_pool, pool_scale, sink, w_out, mods, tables["bias"], tables["band"], tables["band_t"],
         tables["count"]), "arbitrary", ex)


def loss_head(h, target, g, t, name):
    s = h.shape[0]
    n_lat = t // TM

    def body(h_ref, t_ref, g_ref, dh_ref, acc_ref):
        i = pl.program_id(0)

        @pl.when(i == 0)
        def _():
            acc_ref[...] = jnp.zeros_like(acc_ref)

        @pl.when(i < n_lat)
        def _():
            hh, gg = h_ref[...], g_ref[...]
            r = lax.rsqrt(jnp.mean(hh * hh, axis=-1, keepdims=True) + EPS)
            xhat = hh * r
            err = xhat * gg - t_ref[...]
            dy = err * (1.0 / D)
            dx = dy * gg
            dh_ref[...] = r * (dx - xhat * jnp.mean(dx * xhat, axis=-1, keepdims=True))
            acc_ref[0:1, :] += _sum0(dy * xhat)
            acc_ref[1:2, :] += _sum0(err * err)

        @pl.when(i >= n_lat)
        def _():
            dh_ref[...] = jnp.zeros_like(dh_ref)

    return pl.pallas_call(
        body, name=name, grid=(s // TM,),
        in_specs=[_rows(D), pl.BlockSpec((TM, D), lambda i: (jnp.minimum(i, n_lat - 1), 0)), _whole((1, D))],
        out_specs=[_rows(D), pl.BlockSpec((8, D), lambda i: (0, 0))],
        out_shape=[jax.ShapeDtypeStruct((s, D), F32), jax.ShapeDtypeStruct((8, D), F32)],
        compiler_params=_params("arbitrary"),
    )(h, target, g)


def mod_rows(c_all, w_mod, b_cols, name):
    def body(c_ref, w_ref, b_ref, o_ref):
        cc = c_ref[...]
        o_ref[...] = _dot((cc * jax.nn.sigmoid(cc)).astype(BF16), w_ref[...].astype(BF16)) + b_ref[...]

    return pl.pallas_call(
        body, name=name, grid=(2,),
        in_specs=[pl.BlockSpec((16, D), lambda l: (0, 0)), pl.BlockSpec((None, D, MOD_COLS), lambda l: (l, 0, 0)),
                  pl.BlockSpec((None, 1, MOD_COLS), lambda l: (l, 0, 0))],
        out_specs=pl.BlockSpec((None, 16, MOD_COLS), lambda l: (l, 0, 0)),
        out_shape=jax.ShapeDtypeStruct((2, 16, MOD_COLS), F32),
        compiler_params=_params("parallel"),
    )(c_all, w_mod, b_cols)


def mod_grads(c_all, dmod_cols, w_mod, name):
    def body(c_ref, d_ref, w_ref, dw_ref, dc_ref):
        @pl.when(pl.program_id(0) == 0)
        def _():
            dc_ref[...] = jnp.zeros_like(dc_ref)

        cc = c_ref[...]
        dd = d_ref[...].astype(BF16)
        dw_ref[...] = _dotg((cc * jax.nn.sigmoid(cc)).astype(BF16), dd, TN)
        dc_ref[...] += _dotg(dd, w_ref[...].astype(BF16), NT)

    return pl.pallas_call(
        body, name=name, grid=(2,),
        in_specs=[pl.BlockSpec((16, D), lambda l: (0, 0)), pl.BlockSpec((None, 16, MOD_COLS), lambda l: (l, 0, 0)),
                  pl.BlockSpec((None, D, MOD_COLS), lambda l: (l, 0, 0))],
        out_specs=[pl.BlockSpec((None, D, MOD_COLS), lambda l: (l, 0, 0)), pl.BlockSpec((16, D), lambda l: (0, 0))],
        out_shape=[jax.ShapeDtypeStruct((2, D, MOD_COLS), F32), jax.ShapeDtypeStruct((16, D), F32)],
        compiler_params=_params("arbitrary"),
    )(c_all, dmod_cols, w_mod)


def _row_tile(rows, cols, n_arrays):
    budget = VMEM_LIMIT_BYTES // 4 // (2 * 4 * n_arrays * cols)
    best = None
    for tr in range(16, rows + 1, 16):
        if rows % tr == 0 and tr <= budget:
            best = tr
    return best if best is not None else rows


def elementwise(fn, ins, out_dtypes, name, ex=None):
    rows, cols = ins[0].shape
    tr = _row_tile(rows, cols, len(ins) + len(out_dtypes))

    def body(*refs):
        outs = fn(*[r[...] for r in refs[:len(ins)]])
        for o_ref, o in zip(refs[len(ins):], outs):
            o_ref[...] = o.astype(o_ref.dtype)

    spec = pl.BlockSpec((tr, cols), lambda i: (i, 0))
    outs, got = _grid_call(body, name, rows // tr, [spec] * len(ins), [spec] * len(out_dtypes),
                           [jax.ShapeDtypeStruct((rows, cols), dt) for dt in out_dtypes], ins, "parallel", ex)
    return outs if ex is None else (outs, got)


def _adamw_tile(w, g, m, v):
    m = ADAM_B1 * m + (1.0 - ADAM_B1) * g
    v = ADAM_B2 * v + (1.0 - ADAM_B2) * (g * g)
    m_hat = m / (1.0 - ADAM_B1 ** ADAM_STEP)
    v_hat = v / (1.0 - ADAM_B2 ** ADAM_STEP)
    return -ADAM_LR * (m_hat / (jnp.sqrt(v_hat) + ADAM_EPS) + ADAM_WD * w), m, v


def adamw(w, g, m, v, name, ex=None):
    shape = w.shape
    two_d = (-1, shape[-1]) if w.ndim > 1 else (1, -1)
    outs = elementwise(_adamw_tile, [a.reshape(two_d) for a in (w, g, m, v)], [F32] * 3, name, ex)
    outs, got = outs if ex is not None else (outs, None)
    outs = [o.reshape(shape) for o in outs]
    return outs if ex is None else (outs, got)


def _prefetch_call(body, name, grid, in_specs, out_specs, out_shape, place, args, ex=None):
    if ex is None:
        spec = pltpu.PrefetchScalarGridSpec(num_scalar_prefetch=1, grid=grid, in_specs=in_specs, out_specs=out_specs)
        return pl.pallas_call(body, name=name, grid_spec=spec, out_shape=out_shape,
                              compiler_params=_params(*["parallel"] * len(grid)))(place, *args)
    n_in, n_out, ci, co = len(in_specs), len(out_specs), len(ex["ins"]), len(ex["out_shape"])
    spec = pltpu.PrefetchScalarGridSpec(num_scalar_prefetch=1, grid=grid, in_specs=list(in_specs) + _any(ci),
                                        out_specs=list(out_specs) + _any(co), scratch_shapes=ex["scratch"])
    outs = pl.pallas_call(
        _carrying(body, grid, n_in, n_out, ex, lead=1), name=name, grid_spec=spec, out_shape=list(out_shape) + ex["out_shape"],
        input_output_aliases={1 + n_in + i: n_out + j for i, j in ex["aliases"].items()},
        compiler_params=_params(*["arbitrary"] * len(grid)))(place, *args, *ex["ins"])
    return outs[:n_out], outs[n_out:]


def cast_place(w, layer, place, name):
    _, r, c = w.shape
    tr = _row_tile(r, c, 2)

    def body(p_ref, w_ref, o_ref):
        o_ref[...] = w_ref[...].astype(BF16)

    return _prefetch_call(
        body, name, (r // tr,), [pl.BlockSpec((None, tr, c), lambda i, p: (layer, i, 0))],
        pl.BlockSpec((None, tr, c), lambda i, p: (p[1], i, 0)), jax.ShapeDtypeStruct((N_SLOT, r, c), BF16), place, [w])


def pair_sum(g32, got, place, name, ex=None):
    n_slot, rh, c = got.shape
    tr = _row_tile(rh, c, 4)
    per = rh // tr

    def body(p_ref, a_ref, b_ref, o_ref, o16_ref):
        r = a_ref[...] + b_ref[...].astype(F32)
        o_ref[...] = r
        o16_ref[...] = r.astype(BF16)

    half = pl.BlockSpec((None, tr, c), lambda s, i, p: (s, i, 0))
    return _prefetch_call(
        body, name, (n_slot, per), [pl.BlockSpec((None, tr, c), lambda s, i, p: (s, p[0] * per + i, 0)), half], [half, half],
        [jax.ShapeDtypeStruct(got.shape, F32), jax.ShapeDtypeStruct(got.shape, BF16)], place, [g32, got], ex)


def chip_sum(p32, got, place, name):
    _, rh, c = p32.shape
    tr = _row_tile(rh, c, 5)
    per = rh // tr

    def body(p_ref, m_ref, r0_ref, r1_ref, r2_ref, o_ref):
        o_ref[...] = m_ref[...] + r0_ref[...].astype(F32) + r1_ref[...].astype(F32) + r2_ref[...].astype(F32)

    part = pl.BlockSpec((tr, c), lambda i, p: (i, 0))
    return _prefetch_call(
        body, name, (per,), [pl.BlockSpec((None, tr, c), lambda i, p: (p[1], i, 0)), part, part, part],
        pl.BlockSpec((tr, c), lambda i, p: (p[0] * per + i, 0)), jax.ShapeDtypeStruct((2 * rh, c), F32), place, [p32, *got])


def adamw_layers(w, g0, g1, m, v, name, ex=None):
    _, r, c = w.shape
    tr = _row_tile(r, c, 10)

    def body(w_ref, g0_ref, g1_ref, m_ref, v_ref, g_ref, d_ref, mo_ref, vo_ref):
        g = jnp.where(pl.program_id(0) == 0, g0_ref[...], g1_ref[...])
        g_ref[...] = g
        d_ref[...], mo_ref[...], vo_ref[...] = _adamw_tile(w_ref[...], g, m_ref[...], v_ref[...])

    steps = r // tr
    stacked = pl.BlockSpec((None, tr, c), lambda l, i: (l, i, 0))
    layer0 = pl.BlockSpec((tr, c), lambda l, i: (jnp.where(l == 0, i, steps - 1), 0))
    layer1 = pl.BlockSpec((tr, c), lambda l, i: (jnp.where(l == 0, 0, i), 0))
    outs, got = _grid_call(body, name, (2, steps), [stacked, layer0, layer1, stacked, stacked], [stacked] * 4,
                           [jax.ShapeDtypeStruct(w.shape, F32)] * 4, (w, g0, g1, m, v), "parallel", ex)
    return outs if ex is None else (outs, got)


def sum8(gathered, name):
    def body(*refs):
        n = len(refs) // 2
        for g_ref, o_ref in zip(refs[:n], refs[n:]):
            acc = g_ref[0]
            for dev in range(1, N_DEV):
                acc = acc + g_ref[dev]
            o_ref[...] = acc

    return pl.pallas_call(
        body, name=name,
        out_shape=[jax.ShapeDtypeStruct(a.shape[1:], F32) for a in gathered],
        compiler_params=_params(),
    )(*gathered)


PHASES = ("start", "late", "finish")


def _place():
    return lax.axis_index("x"), lax.axis_index("y"), lax.axis_index("c")


def _any(n):
    return [pl.BlockSpec(memory_space=pl.ANY)] * n


def gather8_exchange(blocks):
    n = len(blocks)

    def copy(outs, sems, ti, k, block, to, src=None):
        dst = outs[ti].at[4 * block[0] + 2 * block[1] + block[2]]
        return pltpu.make_async_remote_copy(src_ref=dst if src is None else src, dst_ref=dst, send_sem=sems[0].at[ti, k],
                                            recv_sem=sems[1].at[ti, k], device_id=to, device_id_type=MESH)

    def first(ins, outs, sems):
        x, y, c = _place()
        local, sent = [], []
        for ti in range(n):
            local.append(pltpu.make_async_copy(ins[ti], outs[ti].at[4 * x + 2 * y + c], sems[2].at[ti]))
            sent.append(copy(outs, sems, ti, 0, (x, y, c), (x, y, 1 - c), src=ins[ti]))
            sent += [copy(outs, sems, ti, 1 + j, (x, y, c), (*chip, c), src=ins[ti]) for j, chip in enumerate(_three_chips(x, y))]
        return local, sent

    def start(ins, outs, sems):
        local, sent = first(ins, outs, sems)
        for cp in local + sent:
            cp.start()

    def passed_on(outs, sems):
        x, y, c = _place()
        return [copy(outs, sems, ti, 4 + j, (*chip, c), (x, y, 1 - c)) for ti in range(n) for j, chip in enumerate(_three_chips(x, y))]

    def late(ins, outs, sems):
        x, y, c = _place()
        on = passed_on(outs, sems)
        for ti in range(n):
            for j, chip in enumerate(_three_chips(x, y)):
                copy(outs, sems, ti, 1 + j, (*chip, c), (x, y, c)).wait_recv()
                on[3 * ti + j].start()

    def finish(ins, outs, sems):
        x, y, c = _place()
        me, sibling = (x, y, c), (x, y, 1 - c)
        local, sent = first(ins, outs, sems)
        for ti in range(n):
            copy(outs, sems, ti, 0, sibling, me).wait_recv()
            for j, chip in enumerate(_three_chips(x, y)):
                copy(outs, sems, ti, 4 + j, (*chip, 1 - c), me).wait_recv()
        for cp in sent + passed_on(outs, sems):
            cp.wait_send()
        for cp in local:
            cp.wait()

    return dict(ins=list(blocks), out_shape=[jax.ShapeDtypeStruct((N_DEV,) + b.shape, b.dtype) for b in blocks], aliases={},
                start=start, late=late, finish=finish,
                scratch=[pltpu.SemaphoreType.DMA((n, 7)), pltpu.SemaphoreType.DMA((n, 7)), pltpu.SemaphoreType.DMA((n,))])


def all_gather(blocks, name):
    return run_exchange(gather8_exchange(blocks), name)


def _three_chips(x, y):
    return [(1 - x, y), (x, 1 - y), (1 - x, 1 - y)]


def gather_exchange(placed):
    n = len(placed)

    def copy(bufs, sems, ti, k, chip, core, to):
        rh = bufs[ti].shape[1] // 2
        half = bufs[ti].at[2 * chip[0] + chip[1], pl.ds(core * rh, rh), :]
        return pltpu.make_async_remote_copy(src_ref=half, dst_ref=half, send_sem=sems[0].at[ti, k], recv_sem=sems[1].at[ti, k],
                                            device_id=to, device_id_type=MESH)

    def sends(bufs, sems):
        x, y, c = _place()
        return [copy(bufs, sems, ti, k, (x, y), c, (*chip, c)) for ti in range(n) for k, chip in enumerate(_three_chips(x, y))]

    def passed_on(bufs, sems):
        x, y, c = _place()
        return [copy(bufs, sems, ti, 3 + k, chip, c, (x, y, 1 - c)) for ti in range(n) for k, chip in enumerate(_three_chips(x, y))]

    def start(ins, bufs, sems):
        for cp in sends(bufs, sems):
            cp.start()

    def late(ins, bufs, sems):
        x, y, c = _place()
        on = passed_on(bufs, sems)
        for ti in range(n):
            for k, chip in enumerate(_three_chips(x, y)):
                copy(bufs, sems, ti, k, chip, c, (x, y, c)).wait_recv()
                on[3 * ti + k].start()

    def finish(ins, bufs, sems):
        x, y, c = _place()
        for ti in range(n):
            for k, chip in enumerate(_three_chips(x, y)):
                copy(bufs, sems, ti, 3 + k, chip, 1 - c, (x, y, c)).wait_recv()
        for cp in sends(bufs, sems) + passed_on(bufs, sems):
            cp.wait_send()

    return dict(ins=list(placed), out_shape=[jax.ShapeDtypeStruct(w.shape, w.dtype) for w in placed],
                aliases={i: i for i in range(n)}, start=start, late=late, finish=finish,
                scratch=[pltpu.SemaphoreType.DMA((n, 6)), pltpu.SemaphoreType.DMA((n, 6))])


def scatter_exchange(p16):
    n = len(p16)

    def copies(ins, got, sems):
        x, y, c = _place()
        return [pltpu.make_async_remote_copy(src_ref=ins[ti].at[2 * chip[0] + chip[1]], dst_ref=got[3 * ti + k],
                                             send_sem=sems[0].at[ti, k], recv_sem=sems[1].at[ti, k], device_id=(*chip, c),
                                             device_id_type=MESH)
                for ti in range(n) for k, chip in enumerate(_three_chips(x, y))]

    def start(ins, got, sems):
        for cp in copies(ins, got, sems):
            cp.start()

    def finish(ins, got, sems):
        for cp in copies(ins, got, sems):
            cp.wait()

    return dict(ins=list(p16), out_shape=[jax.ShapeDtypeStruct(a.shape[1:], BF16) for a in p16 for _ in range(3)], aliases={},
                start=start, finish=finish, scratch=[pltpu.SemaphoreType.DMA((n, 3)), pltpu.SemaphoreType.DMA((n, 3))])


def run_exchange(ex, name):
    ci, co = len(ex["ins"]), len(ex["out_shape"])

    def body(*refs):
        ins, outs, sems = refs[:ci], refs[ci:ci + co], refs[ci + co:]
        for phase in PHASES:
            if phase in ex:
                ex[phase](ins, outs, sems)

    return pl.pallas_call(body, name=name, in_specs=_any(ci), out_specs=_any(co), out_shape=ex["out_shape"],
                          input_output_aliases=ex["aliases"], scratch_shapes=ex["scratch"])(*ex["ins"])


def _carrying(body, grid, n_in, n_out, ex, lead=0):
    ci, co = len(ex["ins"]), len(ex["out_shape"])
    first, last = (0,) * len(grid), tuple(g - 1 for g in grid)
    steps = dict(start=first, late=(grid[0] - 2,) if len(grid) == 1 and grid[0] > 2 else last, finish=last)

    def at(ids):
        return functools.reduce(jnp.logical_and, [pl.program_id(ax) == v for ax, v in enumerate(ids)])

    def carrying(*refs):
        head, refs = refs[:lead], refs[lead:]
        c_in, c_out = refs[n_in:n_in + ci], refs[n_in + ci + n_out:n_in + ci + n_out + co]
        sems = refs[n_in + ci + n_out + co:]
        for phase in PHASES:
            if phase == "finish":
                body(*head, *refs[:n_in], *refs[n_in + ci:n_in + ci + n_out])
            if phase in ex:
                pl.when(at(steps[phase]))(functools.partial(ex[phase], c_in, c_out, sems))

    return carrying


def _grid_call(body, name, grid, in_specs, out_specs, out_shape, args, sem, ex=None):
    grid = (grid,) if isinstance(grid, int) else tuple(grid)
    sems_of = (sem,) * len(grid) if isinstance(sem, str) else tuple(sem)
    n_in, n_out = len(in_specs), len(out_specs)
    if ex is None:
        return pl.pallas_call(body, name=name, grid=grid, in_specs=in_specs, out_specs=out_specs, out_shape=out_shape,
                              compiler_params=_params(*sems_of))(*args), []
    ci, co = len(ex["ins"]), len(ex["out_shape"])
    outs = pl.pallas_call(
        _carrying(body, grid, n_in, n_out, ex), name=name, grid=grid, in_specs=list(in_specs) + _any(ci),
        out_specs=list(out_specs) + _any(co), out_shape=list(out_shape) + ex["out_shape"], scratch_shapes=ex["scratch"],
        input_output_aliases={n_in + i: n_out + j for i, j in ex["aliases"].items()},
        compiler_params=_params(*["arbitrary"] * len(grid)),
    )(*args, *ex["ins"])
    return outs[:n_out], outs[n_out:]


def both(*exchanges):
    exchanges = [ex for ex in exchanges if ex is not None]
    if len(exchanges) < 2:
        return exchanges[0] if exchanges else None
    n_ins = [len(ex["ins"]) for ex in exchanges]
    n_outs = [len(ex["out_shape"]) for ex in exchanges]
    n_sems = [len(ex["scratch"]) for ex in exchanges]

    def parts(seq, counts, k):
        first = sum(counts[:k])
        return seq[first:first + counts[k]]

    def run(phase):
        def go(ins, outs, sems):
            for k, ex in enumerate(exchanges):
                if phase in ex:
                    ex[phase](parts(ins, n_ins, k), parts(outs, n_outs, k), parts(sems, n_sems, k))
        return go

    aliases = {sum(n_ins[:k]) + i: sum(n_outs[:k]) + j for k, ex in enumerate(exchanges) for i, j in ex["aliases"].items()}
    return dict(ins=[a for ex in exchanges for a in ex["ins"]], out_shape=[o for ex in exchanges for o in ex["out_shape"]],
                aliases=aliases, scratch=[s for ex in exchanges for s in ex["scratch"]], **{ph: run(ph) for ph in PHASES})


def split_outputs(got, *exchanges):
    got, out = list(got), []
    for ex in exchanges:
        n = len(ex["out_shape"]) if ex is not None else 0
        out.append(got[:n])
        got = got[n:]
    return out


def pair_exchange(g16):
    n = len(g16)

    def copies(a16, got, sems):
        x, y, c = _place()
        out = []
        for ti in range(n):
            rh = a16[ti].shape[1] // 2
            out.append(pltpu.make_async_remote_copy(
                src_ref=a16[ti].at[:, pl.ds((1 - c) * rh, rh), :], dst_ref=got[ti], send_sem=sems[0].at[ti],
                recv_sem=sems[1].at[ti], device_id=(x, y, 1 - c), device_id_type=MESH))
        return out

    def start(a16, got, sems):
        for cp in copies(a16, got, sems):
            cp.start()

    def finish(a16, got, sems):
        for cp in copies(a16, got, sems):
            cp.wait()

    return dict(ins=list(g16), out_shape=[jax.ShapeDtypeStruct((a.shape[0], a.shape[1] // 2, a.shape[2]), BF16) for a in g16],
                aliases={}, start=start, finish=finish, scratch=[pltpu.SemaphoreType.DMA((n,)), pltpu.SemaphoreType.DMA((n,))])


def _gather_half(buf, chip, core):
    rh = buf.shape[1] // 2
    return buf.at[2 * chip[0] + chip[1], pl.ds(core * rh, rh), :]


def gather_start(placed, name):
    n = len(placed)
    hbm, sem = pl.BlockSpec(memory_space=pltpu.HBM), pl.BlockSpec(memory_space=pltpu.SEMAPHORE)

    def body(*refs):
        bufs, send_sems, recv_sems, token_ref = refs[:n], refs[n], refs[n + 1], refs[-1]
        x, y, c = _place()
        for ti in range(n):
            for k, chip in enumerate(_three_chips(x, y)):
                half = _gather_half(bufs[ti], (x, y), c)
                pltpu.make_async_remote_copy(src_ref=half, dst_ref=half, send_sem=send_sems.at[3 * ti + k],
                                             recv_sem=recv_sems.at[3 * ti + k], device_id=(*chip, c), device_id_type=MESH).start()
        token_ref[...] = jnp.zeros_like(token_ref)

    return pl.pallas_call(
        body, name=name,
        out_shape=(pltpu.SemaphoreType.DMA((3 * n,)), pltpu.SemaphoreType.DMA((3 * n,)), *[pltpu.HBM(w.shape, w.dtype) for w in placed],
                   jax.ShapeDtypeStruct((8, BLK), F32)),
        in_specs=(hbm,) * n, out_specs=(sem, sem, *(hbm,) * n, pl.BlockSpec(memory_space=pltpu.VMEM)),
        input_output_aliases={i: 2 + i for i in range(n)},
        compiler_params=pltpu.CompilerParams(has_side_effects=pltpu.SideEffectType.DATAFLOW_SIDE_EFFECTING),
    )(*[pltpu.with_memory_space_constraint(w, pltpu.HBM) for w in placed])


def gather_wait(send_sems, recv_sems, bufs, after, name):
    n = len(bufs)
    hbm, sem = pl.BlockSpec(memory_space=pltpu.HBM), pl.BlockSpec(memory_space=pltpu.SEMAPHORE)

    def body(*refs):
        bufs, send_sems, recv_sems = refs[:n], refs[n], refs[n + 1]
        x, y, c = _place()
        for ti in range(n):
            for k, chip in enumerate(_three_chips(x, y)):
                mine, theirs = _gather_half(bufs[ti], (x, y), c), _gather_half(bufs[ti], chip, c)
                cp = pltpu.make_async_remote_copy(src_ref=mine, dst_ref=theirs, send_sem=send_sems.at[3 * ti + k],
                                                  recv_sem=recv_sems.at[3 * ti + k], device_id=(*chip, c), device_id_type=MESH)
                cp.wait_send()
                cp.wait_recv()

    return pl.pallas_call(
        body, name=name, out_shape=tuple(pltpu.HBM(w.shape, w.dtype) for w in bufs),
        in_specs=(*(hbm,) * n, sem, sem, *_any(len(after))), out_specs=(hbm,) * n,
        input_output_aliases={i: i for i in range(n)},
        compiler_params=pltpu.CompilerParams(has_side_effects=pltpu.SideEffectType.DATAFLOW_SIDE_EFFECTING),
    )(*bufs, send_sems, recv_sems, *after)


def pass_on_exchange(bufs):
    n = len(bufs)

    def copies(refs, sems, core):
        x, y, c = _place()
        return [pltpu.make_async_remote_copy(src_ref=_gather_half(refs[ti], chip, c if core == "mine" else 1 - c),
                                             dst_ref=_gather_half(refs[ti], chip, c if core == "mine" else 1 - c),
                                             send_sem=sems[0].at[ti, k], recv_sem=sems[1].at[ti, k], device_id=(x, y, 1 - c),
                                             device_id_type=MESH)
                for ti in range(n) for k, chip in enumerate(_three_chips(x, y))]

    def start(ins, refs, sems):
        for cp in copies(refs, sems, "mine"):
            cp.start()

    def finish(ins, refs, sems):
        for cp in copies(refs, sems, "mine"):
            cp.wait_send()
        for cp in copies(refs, sems, "sibling's"):
            cp.wait_recv()

    return dict(ins=list(bufs), out_shape=[jax.ShapeDtypeStruct(w.shape, w.dtype) for w in bufs], aliases={i: i for i in range(n)},
                start=start, finish=finish, scratch=[pltpu.SemaphoreType.DMA((n, 3)), pltpu.SemaphoreType.DMA((n, 3))])


def _scatter_copies(src_ref, lands, send_sems, recv_sems):
    x, y, c = _place()
    return [pltpu.make_async_remote_copy(src_ref=src_ref.at[2 * chip[0] + chip[1]], dst_ref=lands[k], send_sem=send_sems.at[k],
                                         recv_sem=recv_sems.at[k], device_id=(*chip, c), device_id_type=MESH)
            for k, chip in enumerate(_three_chips(x, y))]


def scatter_start(p16, name):
    hbm, sem = pl.BlockSpec(memory_space=pltpu.HBM), pl.BlockSpec(memory_space=pltpu.SEMAPHORE)

    def body(src_ref, l0_ref, l1_ref, l2_ref, send_sems, recv_sems, src_thru, o0_ref, o1_ref, o2_ref, token_ref):
        for cp in _scatter_copies(src_ref, (l0_ref, l1_ref, l2_ref), send_sems, recv_sems):
            cp.start()
        token_ref[...] = jnp.zeros_like(token_ref)

    land = [pltpu.with_memory_space_constraint(lax.empty(p16.shape[1:], BF16), pltpu.HBM) for _ in range(3)]
    return pl.pallas_call(
        body, name=name,
        out_shape=(pltpu.SemaphoreType.DMA((3,)), pltpu.SemaphoreType.DMA((3,)), pltpu.HBM(p16.shape, BF16),
                   *[pltpu.HBM(p16.shape[1:], BF16)] * 3, jax.ShapeDtypeStruct((8, BLK), F32)),
        in_specs=(hbm,) * 4, out_specs=(sem, sem, hbm, hbm, hbm, hbm, pl.BlockSpec(memory_space=pltpu.VMEM)),
        input_output_aliases={0: 2, 1: 3, 2: 4, 3: 5},
        compiler_params=pltpu.CompilerParams(has_side_effects=pltpu.SideEffectType.DATAFLOW_SIDE_EFFECTING),
    )(pltpu.with_memory_space_constraint(p16, pltpu.HBM), *land)


def scatter_wait(send_sems, recv_sems, src_thru, lands, after, name):
    hbm, sem = pl.BlockSpec(memory_space=pltpu.HBM), pl.BlockSpec(memory_space=pltpu.SEMAPHORE)

    def body(src_ref, l0_ref, l1_ref, l2_ref, send_sems, recv_sems, *rest):
        for cp in _scatter_copies(src_ref, (l0_ref, l1_ref, l2_ref), send_sems, recv_sems):
            cp.wait_send()
            cp.wait_recv()

    return pl.pallas_call(
        body, name=name, out_shape=(pltpu.HBM(src_thru.shape, BF16), *[pltpu.HBM(lands[0].shape, BF16)] * 3),
        in_specs=(hbm, hbm, hbm, hbm, sem, sem, *_any(len(after))), out_specs=(hbm,) * 4,
        input_output_aliases={0: 0, 1: 1, 2: 2, 3: 3},
        compiler_params=pltpu.CompilerParams(has_side_effects=pltpu.SideEffectType.DATAFLOW_SIDE_EFFECTING),
    )(src_thru, *lands, send_sems, recv_sems, *after)[1:]


def pair_fill_exchange(halves):
    n = len(halves)

    def copies(bufs, sems, core):
        x, y, c = _place()
        out = []
        for ti in range(n):
            rh = bufs[ti].shape[0] // 2
            rows = bufs[ti].at[pl.ds((c if core == "mine" else 1 - c) * rh, rh), :]
            out.append(pltpu.make_async_remote_copy(src_ref=rows, dst_ref=rows, send_sem=sems[0].at[ti], recv_sem=sems[1].at[ti],
                                                    device_id=(x, y, 1 - c), device_id_type=MESH))
        return out

    def start(ins, bufs, sems):
        for cp in copies(bufs, sems, "mine"):
            cp.start()

    def finish(ins, bufs, sems):
        for cp in copies(bufs, sems, "mine"):
            cp.wait_send()
        for cp in copies(bufs, sems, "sibling's"):
            cp.wait_recv()

    return dict(ins=list(halves), out_shape=[jax.ShapeDtypeStruct(a.shape, a.dtype) for a in halves],
                aliases={i: i for i in range(n)}, start=start, finish=finish,
                scratch=[pltpu.SemaphoreType.DMA((n,)), pltpu.SemaphoreType.DMA((n,))])


def pair_gather(halves, name):
    return run_exchange(pair_fill_exchange(halves), name)


def reduce_small(dm_f1, dm_mix, dm_gate, dm_f2, loss_blk, name):
    def body(f1_ref, mix_ref, gate_ref, f2_ref, l_ref, tot_ref, rows_ref, fin_ref):
        rows_ref[...] = jnp.zeros_like(rows_ref)
        tot_ref[...] = jnp.zeros_like(tot_ref)
        mod_src = [(f1_ref, 0), (f1_ref, 1), (f1_ref, 2), (mix_ref, 0), (mix_ref, 1), (gate_ref, 2),
                   (f2_ref, 0), (f2_ref, 1), (f2_ref, 2)]
        norm_src = [(f1_ref, 3), (mix_ref, 3), (f2_ref, 3)]
        for l in range(2):
            for k, (ref, r) in enumerate(mod_src + norm_src):
                lat = ref[0, l, 0, r:r + 1, :]
                ctx = ref[0, l, 1, r:r + 1, :]
                for dev in range(N_DEV):
                    if dev:
                        lat = lat + ref[dev, l, 0, r:r + 1, :]
                        ctx = ctx + ref[dev, l, 1, r:r + 1, :]
                    if k < N_MOD:
                        rows_ref[l, dev, k:k + 1, :] = ref[dev, l, 0, r:r + 1, :]
                if k < N_MOD:
                    rows_ref[l, N_DEV, k:k + 1, :] = ctx
                tot_ref[l, k:k + 1, :] = lat + ctx
        acc = l_ref[0]
        for dev in range(1, N_DEV):
            acc = acc + l_ref[dev]
        loss = (0.5 / D) * jnp.sum(acc[1:2, :], axis=1, keepdims=True)
        row = lax.broadcasted_iota(jnp.int32, (8, D), 0)
        fin_ref[...] = jnp.where(row == 0, acc[0:1, :], loss)

    return pl.pallas_call(
        body, name=name,
        out_shape=[jax.ShapeDtypeStruct((2, 16, D), F32), jax.ShapeDtypeStruct((2, 16, 16, D), F32),
                   jax.ShapeDtypeStruct((8, D), F32)],
        compiler_params=_params(),
    )(dm_f1, dm_mix, dm_gate, dm_f2, loss_blk)


def rope_tables(t, s):
    rows = t // GRID_W
    row = jnp.repeat(jnp.arange(rows), GRID_W).astype(F32)
    col = jnp.tile(jnp.arange(GRID_W), rows).astype(F32)
    inv = ROPE_BASE ** (-jnp.arange(0, HEAD // 2, 2, dtype=F32) / (HEAD // 2))
    ang = jnp.concatenate([row[:, None] * inv, col[:, None] * inv], axis=-1)
    cos, sin = jnp.cos(ang), jnp.sin(ang)
    cos = jnp.concatenate([jnp.tile(cos, (1, 4)), jnp.ones((s - t, BLK), F32)], axis=0)
    sin = jnp.concatenate([jnp.tile(jnp.concatenate([-sin, sin], axis=1), (1, 2)), jnp.zeros((s - t, BLK), F32)], axis=0)
    return cos, sin


BIG = ("ffn1_in", "ffn1_out", "w_in", "w_out", "ffn2_in", "ffn2_out")
GROUPS = dict(ffn1=("ffn1_in", "ffn1_out"), mix=("w_in", "w_out"), ffn2=("ffn2_in", "ffn2_out"))
GATHER_BEHIND = {("ffn1", 0): [("w_in", 0), ("ffn2_out", 0), ("ffn1_out", 1)], ("proj", 0): [("w_out", 0)],
                 ("mix", 0): [("ffn2_in", 0)], ("ffn2", 0): [("ffn1_in", 1), ("w_in", 1)],
                 ("ffn1", 1): [("ffn2_in", 1), ("w_out", 1)], ("mix", 1): [("ffn2_out", 1)]}


def _slot_major(name, g):
    if name == "w_in":
        return jnp.stack(jnp.split(g, N_SLOT, axis=1), axis=0)
    if name in ("ffn1_in", "ffn2_in"):
        return g
    return g.reshape(N_SLOT, g.shape[0] // N_SLOT, g.shape[1])


def _whole_weight(name, buf):
    if name == "w_in":
        return buf.transpose(1, 0, 2).reshape(D, PROJ_W)
    if name in ("ffn1_in", "ffn2_in"):
        return buf
    return buf.reshape(-1, buf.shape[2])


def local_step(x1, ctx1, target, mods, norms, nfinal, placed, w_pool, pool_scale, sink, place, small_blocks):
    t, s = x1.shape[0], x1.shape[0] + ctx1.shape[0]
    n_lat = t // TM
    cos, sin = rope_tables(t, s)
    tables = mix_tables(t, s)
    wts ={name: list(pair) for name, pair in placed.items()}

    def gather(tensors):
        return gather_exchange([wts[name][l] for name, l in tensors])

    def gathered(tensors, arrays):
        for (name, l), whole in zip(tensors, arrays):
            wts[name][l] = whole

    def weight(name, l):
        return _whole_weight(name, wts[name][l])

    def fwd_ex(grp, l):
        groups = GATHER_BEHIND.get((grp, l))
        return (groups, gather(groups)) if groups else (None, None)

    h = jnp.concatenate([x1, ctx1], axis=0)
    saved = []
    for l in range(2):
        h0 = h
        groups, ex = fwd_ex("ffn1", l)
        (h1, ab1, f1), got = ffn_fwd(h0, mods, norms[0], weight("ffn1_in", l), weight("ffn1_out", l), l, 0, n_lat, f"ffn1_fwd_{l}", ex)
        gathered(groups or [], got)
        groups, ex = fwd_ex("proj", l)
        (u, q, k, v), got = proj_fwd(h1, mods, norms[1], weight("w_in", l), cos, sin, l, n_lat, f"proj_fwd_{l}", ex)
        gathered(groups or [], got)
        groups, ex = fwd_ex("mix", l)
        (h2, cat, lse, mo), got = mix_fwd(h1, q, k, v, u, w_pool, pool_scale, sink, weight("w_out", l), mods, tables, l, t,
                                          f"mix_fwd_{l}", ex)
        gathered(groups or [], got)
        groups, ex = fwd_ex("ffn2", l)
        (h, ab2, f2), got = ffn_fwd(h2, mods, norms[2], weight("ffn2_in", l), weight("ffn2_out", l), l, 6, n_lat, f"ffn2_fwd_{l}", ex)
        gathered(groups or [], got)
        saved.append((h0, ab1, f1, h1, u, q, k, v, cat, lse, mo, h2, ab2, f2))
    dh, loss_blk = loss_head(h, target, nfinal, t, "loss_head")

    halves = {name: [None, None] for name in BIG}
    pending = []

    def summed_in_pair(grp, l, name_a, g_a, name_b, wgrad_b):
        g_b, got_a = wgrad_b(pair_exchange([_slot_major(name_a, g_a[1])]))
        sum_a, got_b = pair_sum(_slot_major(name_a, g_a[0]), got_a[0], place, f"pair_sum_{name_a}_{l}",
                                pair_exchange([_slot_major(name_b, g_b[1])]))
        sums = {name_a: sum_a, name_b: pair_sum(_slot_major(name_b, g_b[0]), got_b[0], place, f"pair_sum_{name_b}_{l}")}
        pending.append((grp, l, [sums[n] for n in GROUPS[grp]]))

    lacking = []

    def riders():
        return (scatter_exchange([p16 for _, p16 in pending[0][2]]) if pending else None,
                pair_fill_exchange([halves[name][l] for name, l in lacking]) if lacking else None)

    def carried(got, exs):
        got, filled = split_outputs(got, *exs)
        for (name, l), whole in zip(list(lacking), filled):
            halves[name][l] = whole
            lacking.remove((name, l))
        if pending:
            grp, l, pairs = pending.pop(0)
            for i, name in enumerate(GROUPS[grp]):
                halves[name][l] = chip_sum(pairs[i][0], got[3 * i:3 * i + 3], place, f"chip_sum_{name}_{l}")
                lacking.append((name, l))

    small = [None, None]
    for l in (1, 0):
        h0, ab1, f1, h1, u, q, k, v, cat, lse, mo, h2, ab2, f2 = saved[l]
        exs = riders()
        (dh, dab, df, n, act, dm_f2), got = ffn_bwd(h2, ab2, f2, dh, mods, norms[2], weight("ffn2_in", l), weight("ffn2_out", l),
                                                    l, 6, n_lat, f"ffn2_bwd_{l}", both(*exs))
        carried(got, exs)
        g_in, _ = wgrad(n, dab, D, FF_COLS, FF_COLS, f"ffn2_in_wgrad_{l}")
        summed_in_pair("ffn2", l, "ffn2_in", g_in, "ffn2_out",
                       lambda ex, a=act, b=df: wgrad(a, b, D_FF // 2, D, None, f"ffn2_out_wgrad_{l}", ex))
        exs = riders()
        (dq, dk, dv, du, dmo, dwp, dps, dsink, dm_gate), got = mix_bwd(
            dh, mo, q, k, v, u, lse, w_pool, pool_scale, sink, weight("w_out", l), mods, tables, l, t, f"mix_bwd_{l}", both(*exs))
        carried(got, exs)
        g_wo, _ = wgrad(cat, dmo, POOL_W + ATTN_W, D, None, f"w_out_wgrad_{l}")
        dh, dp, n, dm_mix = proj_bwd(h1, du, dq, dk, dv, dh, mods, norms[1], weight("w_in", l), cos, sin, l, n_lat, f"proj_bwd_{l}")
        summed_in_pair("mix", l, "w_out", g_wo, "w_in",
                       lambda ex, a=n, b=dp: wgrad(a, b, D, PROJ_W // 2, None, f"w_in_wgrad_{l}", ex))
        exs = riders()
        (dh, dab, df, n, act, dm_f1), got = ffn_bwd(h0, ab1, f1, dh, mods, norms[0], weight("ffn1_in", l), weight("ffn1_out", l),
                                                    l, 0, n_lat, f"ffn1_bwd_{l}", both(*exs))
        carried(got, exs)
        small[l] = dict(dm_f1=dm_f1, dm_mix=dm_mix, dm_gate=dm_gate, dm_f2=dm_f2, dwp=dwp, dps=dps, dsink=dsink)
        if l:
            g_in, _ = wgrad(n, dab, D, FF_COLS, FF_COLS, f"ffn1_in_wgrad_{l}")
            summed_in_pair("ffn1", l, "ffn1_in", g_in, "ffn1_out",
                           lambda ex, a=act, b=df: wgrad(a, b, D_FF // 2, D, None, f"ffn1_out_wgrad_{l}", ex))
    g_out, _ = wgrad(act, df, D_FF // 2, D, None, "ffn1_out_wgrad_0")
    got = run_exchange(pair_exchange([_slot_major("ffn1_out", g_out[1])]), "pair_exchange_ffn1_out_0")
    p32, p16 = pair_sum(_slot_major("ffn1_out", g_out[0]), got[0], place, "pair_sum_ffn1_out_0")
    riding = (gather8_exchange(small_blocks(small, loss_blk)), scatter_exchange([p16]),
              pair_fill_exchange([halves[name][l] for name, l in lacking]))
    g_in, got = wgrad(n, dab, D, FF_COLS, FF_COLS, "ffn1_in_wgrad_0", both(*riding))
    small_all, got, filled = split_outputs(got, *riding)
    for (name, l), whole in zip(lacking, filled):
        halves[name][l] = whole
    (halves["ffn1_out"][0],) = pair_gather([chip_sum(p32, got, place, "chip_sum_ffn1_out_0")], "pair_gather_ffn1_out_0")
    got = run_exchange(pair_exchange([_slot_major("ffn1_in", g_in[1])]), "pair_exchange_ffn1_in_0")
    return dh[:t], halves, pair_sum(_slot_major("ffn1_in", g_in[0]), got[0], place, "pair_sum_ffn1_in_0"), small_all


def _silu_grad(z):
    sg = jax.nn.sigmoid(z)
    return sg * (1 + z * (1 - sg))


def kernel(x, c, ctx, c_ctx, w_mod, b_mod, norm_ffn1, w_ffn1_in, w_ffn1_out, norm_mix, w_in, w_pool, pool_scale, sink, w_out, norm_ffn2, w_ffn2_in, w_ffn2_out, norm_final, loss_target, m_c_ctx, m_w_mod, m_b_mod, m_norm_ffn1, m_w_ffn1_in, m_w_ffn1_out, m_norm_mix, m_w_in, m_w_pool, m_pool_scale, m_sink, m_w_out, m_norm_ffn2, m_w_ffn2_in, m_w_ffn2_out, m_norm_final, v_c_ctx, v_w_mod, v_b_mod, v_norm_ffn1, v_w_ffn1_in, v_w_ffn1_out, v_norm_mix, v_w_in, v_w_pool, v_pool_scale, v_sink, v_w_out, v_norm_ffn2, v_w_ffn2_in, v_w_ffn2_out, v_norm_final):
    px, py, pc = _place()
    slot, me = 2 * px + py, 4 * px + 2 * py + pc
    n_grp = len(POOL_WINDOWS)

    (c_rows,) = all_gather([c.reshape(8, D // 8)], "gather_c")
    c_all = jnp.concatenate([c_rows.reshape(N_DEV, D), c_ctx.reshape(1, D), jnp.zeros((16 - N_DEV - 1, D), F32)], axis=0)
    b_cols = lax.dynamic_slice(b_mod, (0, slot * MOD_COLS), (2, MOD_COLS)).reshape(2, 1, MOD_COLS)
    (mod_parts,) = all_gather([mod_rows(c_all, w_mod, b_cols, "mod_rows")], "gather_mods")
    mods_all = mod_parts[0::2].transpose(1, 2, 0, 3).reshape(2, 16, N_MOD * D)
    mx = lax.dynamic_slice(mods_all, (0, me, 0), (2, 1, N_MOD * D)).reshape(2, N_MOD, D)
    mc = mods_all[:, N_DEV].reshape(2, N_MOD, D)
    pad = jnp.zeros((2, 16 - N_MOD, D), F32)
    mods = jnp.stack([jnp.concatenate([mx, pad], axis=1), jnp.concatenate([mc, pad], axis=1)], axis=1)

    place = jnp.stack([pc, slot]).astype(jnp.int32)
    shards = dict(ffn1_in=w_ffn1_in, ffn1_out=w_ffn1_out, w_in=w_in, w_out=w_out, ffn2_in=w_ffn2_in, ffn2_out=w_ffn2_out)
    first = [("ffn1_in", 0), ("ffn1_out", 0)]
    placed = {name: [None, None] for name in BIG}
    for name, l in first:
        placed[name][l] = cast_place(shards[name], l, place, f"cast_{name}_{l}")
    send_sems, recv_sems, *bufs, token = gather_start([placed[name][l] for name, l in first], "gather_first_start")
    others = [(name, l) for name in BIG for l in range(2) if (name, l) not in first]
    for name, l in others:
        placed[name][l] = cast_place(shards[name], l, place, f"cast_{name}_{l}")
    bufs = gather_wait(send_sems, recv_sems, bufs, [placed[name][l] for name, l in others], "gather_first_wait")
    for (name, l), whole in zip(first, run_exchange(pass_on_exchange(bufs), "gather_first_pass_on")):
        placed[name][l] = whole
    norms = [g.reshape(2, 1, D) for g in (norm_ffn1, norm_mix, norm_ffn2)]
    row_sums = ("dm_f1", "dm_mix", "dm_gate", "dm_f2")

    def small_blocks(small, loss_blk):
        stacked = {k: jnp.stack([small[0][k], small[1][k]]) for k in row_sums + ("dwp", "dps", "dsink")}
        return ([stacked[k].reshape(32, D) for k in row_sums]
                + [stacked["dwp"].reshape(2 * n_grp * GROUP, GROUP), stacked["dps"].reshape(16, POOL_W),
                   stacked["dsink"].reshape(16, BLK), loss_blk])

    dx, halves, last_pair, small_all = local_step(x[0], ctx[0], loss_target[0], mods, norms, norm_final.reshape(1, D), placed,
                                                   w_pool.astype(BF16), pool_scale.reshape(2, 1, POOL_W), sink, place, small_blocks)
    grads = {}

    *g_dm, g_dwp, g_dps, g_dsink, g_loss = small_all
    tot, rows, fin = reduce_small(*[g.reshape(N_DEV, 2, 2, 8, D) for g in g_dm], g_loss, "reduce_small")
    s_dwp, s_dps, s_dsink = sum8([g_dwp, g_dps, g_dsink], "sum_pool_sink")
    grads.update(
        w_pool=s_dwp.reshape(2, n_grp, GROUP, GROUP), pool_scale=s_dps.reshape(2, 8, POOL_W)[:, 0],
        sink=s_dsink.reshape(2, 8, BLK)[:, 0, :N_HEADS], b_mod=tot[:, :N_MOD].reshape(2, N_MOD * D),
        norm_ffn1=tot[:, N_MOD], norm_mix=tot[:, N_MOD + 1], norm_ffn2=tot[:, N_MOD + 2], norm_final=fin[0])
    loss = fin[1, 0]

    dmod_cols = lax.dynamic_slice(rows[:, :, :N_MOD, :].reshape(2, 16, N_MOD * D), (0, 0, slot * MOD_COLS), (2, 16, MOD_COLS))
    grads["w_mod"], dc = mod_grads(c_all, dmod_cols, w_mod, "mod_grads")
    (g_dc,) = all_gather([dc], "gather_dc")
    (s_dc,) = sum8([g_dc], "sum_dc")
    (d_c_ctx,) = elementwise(lambda d, z: (0.5 * d * _silu_grad(z),), [s_dc[N_DEV:N_DEV + 1], c_ctx.reshape(1, D)], [F32], "c_ctx_grad")
    send_sems, recv_sems, src_thru, *lands, token = scatter_start(last_pair[1], "scatter_last_start")
    grads["c_ctx"] = d_c_ctx.reshape(D) + token[0, :1]

    given = dict(c_ctx=(c_ctx, m_c_ctx, v_c_ctx), w_mod=(w_mod, m_w_mod, v_w_mod), b_mod=(b_mod, m_b_mod, v_b_mod),
                 norm_ffn1=(norm_ffn1, m_norm_ffn1, v_norm_ffn1), w_ffn1_in=(w_ffn1_in, m_w_ffn1_in, v_w_ffn1_in),
                 w_ffn1_out=(w_ffn1_out, m_w_ffn1_out, v_w_ffn1_out), norm_mix=(norm_mix, m_norm_mix, v_norm_mix),
                 w_in=(w_in, m_w_in, v_w_in), w_pool=(w_pool, m_w_pool, v_w_pool),
                 pool_scale=(pool_scale, m_pool_scale, v_pool_scale), sink=(sink, m_sink, v_sink), w_out=(w_out, m_w_out, v_w_out),
                 norm_ffn2=(norm_ffn2, m_norm_ffn2, v_norm_ffn2), w_ffn2_in=(w_ffn2_in, m_w_ffn2_in, v_w_ffn2_in),
                 w_ffn2_out=(w_ffn2_out, m_w_ffn2_out, v_w_ffn2_out), norm_final=(norm_final, m_norm_final, v_norm_final))
    shard = {(name, l): halves[name][l] for name in BIG for l in range(2)}

    def update(name):
        w, m, v = given[name]
        if name in BIG or name[2:] in BIG:
            key = name if name in BIG else name[2:]
            return adamw_layers(w, shard[key, 0], shard[key, 1], m, v, f"adamw_{name}")
        return [grads[name], *adamw(w, grads[name], m, v, f"adamw_{name}")]

    done = {name: update(name) for name in given if name != "w_ffn1_in"}
    got = scatter_wait(send_sems, recv_sems, src_thru, lands, [done[name][3] for name in done if name[2:] in BIG or name in BIG]
                       + [done["w_mod"][3]], "scatter_last_wait")
    (shard["ffn1_in", 0],) = pair_gather([chip_sum(last_pair[0], got, place, "chip_sum_ffn1_in_0")], "grad_pair_gather_last")
    done["w_ffn1_in"] = update("w_ffn1_in")
    return (loss, dx[None], *[done[name][i] for i in range(4) for name in given])
```

```python
import functools

import jax
import jax.numpy as jnp
from jax import lax
from jax.experimental import pallas as pl
from jax.experimental.pallas import tpu as pltpu

F32, BF16 = jnp.float32, jnp.bfloat16
D = 1024
D_FF = 2816
N_SLOT = 4
FF_COLS = 2 * D_FF // N_SLOT
N_MOD = 9
MOD_COLS = N_MOD * D // N_SLOT
POOL_W, ATTN_W, KV_W = 512, 512, 128
PROJ_W = POOL_W + ATTN_W + 2 * KV_W
N_HEADS, Q_GROUP, HEAD = 8, 4, 64
GROUP = 128
POOL_WINDOWS = (2, 4, 8, 16)
BLK = 128
QB = 256
WIN = QB + 2 * BLK
GRID_W = 64
ROPE_BASE = 10000.0
EPS = 1e-6
NEG_INF = -1e30
TM = 256
N_DEV = 8
VMEM_LIMIT_BYTES = 56 * 1024 * 1024
WGRAD_VMEM_BYTES = 44 * 1024 * 1024
ADAM_LR, ADAM_B1, ADAM_B2, ADAM_EPS, ADAM_WD, ADAM_STEP = 0.001, 0.9, 0.999, 1e-08, 0.01, 10
MESH = pl.DeviceIdType.MESH
NT = (((1,), (1,)), ((), ()))
TN = (((0,), (0,)), ((), ()))


def _params(*sem):
    return pltpu.CompilerParams(dimension_semantics=sem, vmem_limit_bytes=VMEM_LIMIT_BYTES)


def _whole(shape, lead=()):
    idx = tuple(lead) + (0,) * len(shape)
    return pl.BlockSpec((None,) * len(lead) + tuple(shape), lambda *_: idx, pipeline_mode=pl.Buffered(1))


def _rows(cols, tm=TM):
    return pl.BlockSpec((tm, cols), lambda i: (i, 0))


def _mods_spec(layer, n_lat):
    return pl.BlockSpec((None, None, 16, D), lambda i: (layer, (i >= n_lat).astype(jnp.int32), 0, 0))


def _acc_spec(n_lat):
    return pl.BlockSpec((None, 8, D), lambda i: ((i >= n_lat).astype(jnp.int32), 0, 0))


def _dot(a, b):
    return jnp.dot(a, b, preferred_element_type=F32)


def _dotg(a, b, dims):
    return lax.dot_general(a, b, dims, preferred_element_type=F32)


def _sum0(v):
    return jnp.sum(v, axis=0, keepdims=True)


def _norm_mod(h, g, shift, scale):
    r = lax.rsqrt(jnp.mean(h * h, axis=-1, keepdims=True) + EPS)
    xhat = h * r
    y = xhat * g
    return y * (1 + scale) + shift, xhat, r, y


def _norm_mod_bwd(dn, xhat, r, y, g, scale):
    dy = dn * (1 + scale)
    dx = dy * g
    dh = r * (dx - xhat * jnp.mean(dx * xhat, axis=-1, keepdims=True))
    return _sum0(dn), _sum0(dn * y), _sum0(dy * xhat), dh


def _swap_halves(v):
    w = v.shape[1]
    lane = lax.broadcasted_iota(jnp.int32, v.shape, 1)
    return jnp.where(lane % HEAD < HEAD // 2, pltpu.roll(v, w - HEAD // 2, axis=1), pltpu.roll(v, HEAD // 2, axis=1))


def _tile_lanes(t, width):
    return t if width == t.shape[1] else jnp.concatenate([t] * (width // t.shape[1]), axis=1)


def _rope(v, cos, sin):
    return v * _tile_lanes(cos, v.shape[1]) + _swap_halves(v) * _tile_lanes(sin, v.shape[1])


def _unrope(g, cos, sin):
    return g * _tile_lanes(cos, g.shape[1]) + _swap_halves(g * _tile_lanes(sin, g.shape[1]))


def ffn_fwd(h, mods, g, w4, wo, layer, k0, n_lat, name, ex=None):
    s = h.shape[0]

    def body(h_ref, m_ref, g_ref, w_ref, wo_ref, ho_ref, ab_ref, f_ref):
        hh = h_ref[...]
        n, _, _, _ = _norm_mod(hh, g_ref[...], m_ref[k0:k0 + 1, :], m_ref[k0 + 1:k0 + 2, :])
        nb = n.astype(BF16)
        acc = jnp.zeros((TM, D), F32)
        for j in range(2):
            a = _dot(nb, w_ref[j])
            b = _dot(nb, w_ref[2 + j])
            ab_ref[:, j * FF_COLS:(j + 1) * FF_COLS] = a.astype(BF16)
            ab_ref[:, (2 + j) * FF_COLS:(3 + j) * FF_COLS] = b.astype(BF16)
            act = (a * jax.nn.sigmoid(a) * b).astype(BF16)
            acc = acc + _dot(act, wo_ref[j * FF_COLS:(j + 1) * FF_COLS, :])
        f_ref[...] = acc
        ho_ref[...] = hh + 0.5 * m_ref[k0 + 2:k0 + 3, :] * acc

    return _grid_call(
        body, name, s // TM,
        [_rows(D), _mods_spec(layer, n_lat), _whole((1, D), (layer,)), _whole((N_SLOT, D, FF_COLS)), _whole((D_FF, D))],
        [_rows(D), _rows(2 * D_FF), _rows(D)],
        [jax.ShapeDtypeStruct((s, D), F32), jax.ShapeDtypeStruct((s, 2 * D_FF), BF16), jax.ShapeDtypeStruct((s, D), F32)],
        (h, mods, g, w4, wo), "parallel", ex)


def ffn_bwd(h, ab, f, dh, mods, g, w4, wo, layer, k0, n_lat, name, ex=None):
    s = h.shape[0]

    def body(h_ref, ab_ref, f_ref, dh_ref, m_ref, g_ref, w_ref, wo_ref, dhi_ref, dab_ref, df_ref, n_ref, act_ref, dm_ref):
        i = pl.program_id(0)

        @pl.when((i == 0) | (i == n_lat))
        def _():
            dm_ref[...] = jnp.zeros_like(dm_ref)

        hh, dho, gg = h_ref[...], dh_ref[...], g_ref[...]
        scale, gate = m_ref[k0 + 1:k0 + 2, :], m_ref[k0 + 2:k0 + 3, :]
        n, xhat, r, y = _norm_mod(hh, gg, m_ref[k0:k0 + 1, :], scale)
        n_ref[...] = n.astype(BF16)
        dgate = _sum0(dho * (0.5 * f_ref[...]))
        dfb = ((0.5 * gate) * dho).astype(BF16)
        df_ref[...] = dfb
        dn = jnp.zeros((TM, D), F32)
        for j in range(2):
            a = ab_ref[:, j * FF_COLS:(j + 1) * FF_COLS].astype(F32)
            b = ab_ref[:, (2 + j) * FF_COLS:(3 + j) * FF_COLS].astype(F32)
            sg = jax.nn.sigmoid(a)
            sa = a * sg
            act_ref[:, j * FF_COLS:(j + 1) * FF_COLS] = (sa * b).astype(BF16)
            dact = _dotg(dfb, wo_ref[j * FF_COLS:(j + 1) * FF_COLS, :], NT)
            da = (dact * b * (sg * (1 + a * (1 - sg)))).astype(BF16)
            db = (dact * sa).astype(BF16)
            dab_ref[:, j * FF_COLS:(j + 1) * FF_COLS] = da
            dab_ref[:, (2 + j) * FF_COLS:(3 + j) * FF_COLS] = db
            dn = dn + _dotg(da, w_ref[j], NT) + _dotg(db, w_ref[2 + j], NT)
        dsh, dsc, dg, dhn = _norm_mod_bwd(dn, xhat, r, y, gg, scale)
        dhi_ref[...] = dho + dhn
        dm_ref[0:1, :] += dsh
        dm_ref[1:2, :] += dsc
        dm_ref[2:3, :] += dgate
        dm_ref[3:4, :] += dg

    return _grid_call(
        body, name, s // TM,
        [_rows(D), _rows(2 * D_FF), _rows(D), _rows(D), _mods_spec(layer, n_lat), _whole((1, D), (layer,)),
         _whole((N_SLOT, D, FF_COLS)), _whole((D_FF, D))],
        [_rows(D), _rows(2 * D_FF), _rows(D), _rows(D), _rows(D_FF), _acc_spec(n_lat)],
        [jax.ShapeDtypeStruct((s, D), F32), jax.ShapeDtypeStruct((s, 2 * D_FF), BF16), jax.ShapeDtypeStruct((s, D), BF16),
         jax.ShapeDtypeStruct((s, D), BF16), jax.ShapeDtypeStruct((s, D_FF), BF16), jax.ShapeDtypeStruct((2, 8, D), F32)],
        (h, ab, f, dh, mods, g, w4, wo), "arbitrary", ex)


def _token_tile(s, limit=2176):
    return max(ts for ts in range(16, limit + 1, 16) if s % ts == 0)


def wgrad(a, b, tk, tn, slot_cols, name, ex=None):
    s, k = a.shape
    n = b.shape[1]
    a_bufs, b_bufs = (1 if k == tk else 2), (1 if n == tn else 2)
    whole = 2 * s * (a_bufs * tk + b_bufs * tn) + 2 * 6 * tk * tn
    ts = s if whole <= WGRAD_VMEM_BYTES else _token_tile(s)
    steps = s // ts
    once = dict(pipeline_mode=pl.Buffered(1))

    def body(a_ref, b_ref, o_ref, o16_ref):
        r = _dotg(a_ref[...], b_ref[...], TN)
        si = pl.program_id(2)

        @pl.when(si == 0)
        def _():
            o_ref[...] = r

        @pl.when(si > 0)
        def _():
            o_ref[...] += r

        @pl.when(si == steps - 1)
        def _():
            o16_ref[...] = o_ref[...].astype(BF16)

    if slot_cols is None:
        shape, spec = (k, n), pl.BlockSpec((tk, tn), lambda i, j, si: (i, j))
    else:
        per = slot_cols // tn
        shape, spec = (n // slot_cols, k, slot_cols), pl.BlockSpec((None, tk, tn), lambda i, j, si: (lax.div(j, per), i, lax.rem(j, per)))
    return _grid_call(
        body, name, (k // tk, n // tn, steps),
        [pl.BlockSpec((ts, tk), lambda i, j, si: (si, i), **(once if a_bufs == 1 and steps == 1 else {})),
         pl.BlockSpec((ts, tn), lambda i, j, si: (si, j), **(once if b_bufs == 1 and steps == 1 else {}))], [spec, spec],
        [jax.ShapeDtypeStruct(shape, F32), jax.ShapeDtypeStruct(shape, BF16)], (a, b), ("parallel", "parallel", "arbitrary"), ex)


def proj_fwd(h, mods, g, w_in, cos, sin, layer, n_lat, name, ex=None):
    s = h.shape[0]

    def body(h_ref, m_ref, g_ref, w_ref, cos_ref, sin_ref, u_ref, q_ref, k_ref, v_ref):
        n, _, _, _ = _norm_mod(h_ref[...], g_ref[...], m_ref[3:4, :], m_ref[4:5, :])
        p = _dot(n.astype(BF16), w_ref[...])
        cs, sn = cos_ref[...], sin_ref[...]
        u_ref[...] = p[:, :POOL_W]
        q_ref[...] = (_rope(p[:, POOL_W:POOL_W + ATTN_W], cs, sn) * HEAD ** -0.5).astype(BF16)
        k_ref[...] = _rope(p[:, POOL_W + ATTN_W:POOL_W + ATTN_W + KV_W], cs, sn).astype(BF16)
        v_ref[...] = p[:, POOL_W + ATTN_W + KV_W:].astype(BF16)

    return _grid_call(
        body, name, s // TM,
        [_rows(D), _mods_spec(layer, n_lat), _whole((1, D), (layer,)), _whole((D, PROJ_W)), _rows(BLK), _rows(BLK)],
        [_rows(POOL_W), _rows(ATTN_W), _rows(KV_W), _rows(KV_W)],
        [jax.ShapeDtypeStruct((s, POOL_W), F32), jax.ShapeDtypeStruct((s, ATTN_W), BF16),
         jax.ShapeDtypeStruct((s, KV_W), BF16), jax.ShapeDtypeStruct((s, KV_W), BF16)],
        (h, mods, g, w_in, cos, sin), "parallel", ex)


def proj_bwd(h, du, dq, dk, dv, dh, mods, g, w_in, cos, sin, layer, n_lat, name):
    s = h.shape[0]

    def body(h_ref, du_ref, dq_ref, dk_ref, dv_ref, dh_ref, m_ref, g_ref, w_ref, cos_ref, sin_ref,
             dhi_ref, dp_ref, n_ref, dm_ref):
        i = pl.program_id(0)

        @pl.when((i == 0) | (i == n_lat))
        def _():
            dm_ref[...] = jnp.zeros_like(dm_ref)

        gg, scale = g_ref[...], m_ref[4:5, :]
        n, xhat, r, y = _norm_mod(h_ref[...], gg, m_ref[3:4, :], scale)
        n_ref[...] = n.astype(BF16)
        cs, sn = cos_ref[...], sin_ref[...]
        dp = jnp.concatenate([du_ref[...], _unrope(dq_ref[...], cs, sn) * HEAD ** -0.5, _unrope(dk_ref[...], cs, sn),
                              dv_ref[...]], axis=1).astype(BF16)
        dp_ref[...] = dp
        dsh, dsc, dg, dhn = _norm_mod_bwd(_dotg(dp, w_ref[...], NT), xhat, r, y, gg, scale)
        dhi_ref[...] = dh_ref[...] + dhn
        dm_ref[0:1, :] += dsh
        dm_ref[1:2, :] += dsc
        dm_ref[3:4, :] += dg

    return pl.pallas_call(
        body, name=name, grid=(s // TM,),
        in_specs=[_rows(D), _rows(POOL_W), _rows(ATTN_W), _rows(KV_W), _rows(KV_W), _rows(D), _mods_spec(layer, n_lat),
                  _whole((1, D), (layer,)), _whole((D, PROJ_W)), _rows(BLK), _rows(BLK)],
        out_specs=[_rows(D), _rows(PROJ_W), _rows(D), _acc_spec(n_lat)],
        out_shape=[jax.ShapeDtypeStruct((s, D), F32), jax.ShapeDtypeStruct((s, PROJ_W), BF16),
                   jax.ShapeDtypeStruct((s, D), BF16), jax.ShapeDtypeStruct((2, 8, D), F32)],
        compiler_params=_params("arbitrary"),
    )(h, du, dq, dk, dv, dh, mods, g, w_in, cos, sin)


def _window(i, s):
    return pl.multiple_of(jnp.clip(i * QB - BLK, 0, s - WIN), BLK)


def mix_tables(t, s):
    n_lat = t // QB
    blocks = jnp.array([0, 1, n_lat - 1] + list(range(n_lat, s // QB)))[:, None, None]
    ws = jnp.clip(blocks * QB - BLK, 0, s - WIN)
    q = blocks * QB + jnp.arange(QB)[None, :, None]
    k = ws + jnp.arange(WIN)[None, None, :]
    is_lat = blocks < n_lat
    local = jnp.where(is_lat & (k < t) & (jnp.abs(k - q) <= BLK), 0.0, NEG_INF).astype(F32)
    bias = jnp.concatenate([local, jnp.zeros(local.shape[:2] + (s - t,), F32)], axis=2)
    seq_lo, seq_hi = jnp.where(is_lat, 0, t), jnp.where(is_lat, t, s)
    bands, counts = [], []
    for w in POOL_WINDOWS:
        lo, hi = jnp.maximum(q - w // 2, seq_lo), jnp.minimum(q + w - w // 2, seq_hi)
        bands.append((k >= lo) & (k < hi))
        counts.append((hi - lo).astype(F32))
    band = jnp.stack(bands, axis=1).astype(BF16)
    count = jnp.concatenate(counts + [jnp.ones(counts[0].shape[:2] + (BLK - len(counts),), F32)], axis=2)
    return dict(bias=bias, band=band, band_t=band.transpose(0, 1, 3, 2), count=count)


def _case_spec(table, n_lat_blk):
    def kind(i):
        return jnp.where(i < n_lat_blk - 1, jnp.minimum(i, 1), i - n_lat_blk + 3)

    shape = table.shape[1:]
    return pl.BlockSpec((None,) + shape, lambda i: (kind(i),) + (0,) * len(shape))


def _split_dot(band, v):
    return _dot(band, v.astype(BF16))


def _pooled(u_ref, band_ref, cnt_ref, i, ws, gi):
    cols = slice(gi * GROUP, (gi + 1) * GROUP)
    mean = _split_dot(band_ref[gi], u_ref[pl.ds(ws, WIN), cols]) / cnt_ref[:, gi:gi + 1]
    return mean - u_ref[pl.ds(pl.multiple_of(i * QB, QB), QB), cols]


def _head_cols(hd):
    return slice(hd * HEAD, (hd + 1) * HEAD)


def _stack_heads(x, hk, first=0):
    return jnp.concatenate([x[:, first + (Q_GROUP * hk + g) * HEAD:first + (Q_GROUP * hk + g + 1) * HEAD]
                            for g in range(Q_GROUP)], axis=0)


def _biased(scores, bias):
    return (scores.reshape(Q_GROUP, QB, -1) + bias).reshape(Q_GROUP * QB, -1)


def _group_column(vals):
    row = lax.broadcasted_iota(jnp.int32, (Q_GROUP * QB, 1), 0)
    out = jnp.full((Q_GROUP * QB, 1), vals[Q_GROUP - 1], F32)
    for g in range(Q_GROUP - 2, -1, -1):
        out = jnp.where(row < (g + 1) * QB, vals[g], out)
    return out


def _lane_place(cols, width=BLK):
    lane = lax.broadcasted_iota(jnp.int32, (cols[0].shape[0], width), 1)
    out = jnp.zeros((cols[0].shape[0], width), F32)
    for hd, c in enumerate(cols):
        out = jnp.where(lane == hd, c, out)
    return out


def mix_fwd(h, q, k, v, u, w_pool, pool_scale, sink, w_out, mods, tables, layer, t, name, ex=None):
    s = h.shape[0]
    n_lat_blk = t // QB

    def body(h_ref, q_ref, k_ref, v_ref, u_ref, wp_ref, ps_ref, sink_ref, wo_ref, m_ref, bias_ref, band_ref, cnt_ref,
             ho_ref, cat_ref, lse_ref, mo_ref):
        i = pl.program_id(0)
        ws = _window(i, s)
        for gi in range(len(POOL_WINDOWS)):
            mixed = _dot(_pooled(u_ref, band_ref, cnt_ref, i, ws, gi).astype(BF16), wp_ref[gi])
            cat_ref[:, gi * GROUP:(gi + 1) * GROUP] = (mixed * ps_ref[:, gi * GROUP:(gi + 1) * GROUP]).astype(BF16)
        bias = bias_ref[...]
        k_all = jnp.concatenate([k_ref[pl.ds(ws, WIN), :], k_ref[t:s, :]], axis=0)
        v_all = jnp.concatenate([v_ref[pl.ds(ws, WIN), :], v_ref[t:s, :]], axis=0)
        lses = []
        for hk in range(N_HEADS // Q_GROUP):
            kv = _head_cols(hk)
            sc = _biased(_dotg(_stack_heads(q_ref[...], hk), k_all[:, kv], NT), bias)
            sk = _group_column([sink_ref[layer, Q_GROUP * hk + g] for g in range(Q_GROUP)])
            m = jnp.maximum(jnp.max(sc, axis=1, keepdims=True), sk)
            e = jnp.exp(sc - m)
            l = jnp.sum(e, axis=1, keepdims=True) + jnp.exp(sk - m)
            o = _dot(e.astype(BF16), v_all[:, kv]) * (1.0 / l)
            lse = m + jnp.log(l)
            for g in range(Q_GROUP):
                hd = Q_GROUP * hk + g
                cat_ref[:, POOL_W + hd * HEAD:POOL_W + (hd + 1) * HEAD] = o[g * QB:(g + 1) * QB].astype(BF16)
                lses.append(lse[g * QB:(g + 1) * QB])
        lse_ref[...] = _lane_place(lses)
        mo = _dot(cat_ref[...], wo_ref[...])
        mo_ref[...] = mo
        ho_ref[...] = h_ref[...] + m_ref[5:6, :] * mo

    blk = lambda cols: _rows(cols, QB)
    return _grid_call(
        body, name, s // QB,
        [blk(D), blk(ATTN_W), _whole((s, KV_W)), _whole((s, KV_W)), _whole((s, POOL_W)),
         _whole((len(POOL_WINDOWS), GROUP, GROUP), (layer,)), _whole((1, POOL_W), (layer,)),
         pl.BlockSpec(memory_space=pltpu.SMEM), _whole((POOL_W + ATTN_W, D)), _mods_spec(layer, n_lat_blk),
         _case_spec(tables["bias"], n_lat_blk), _case_spec(tables["band"], n_lat_blk), _case_spec(tables["count"], n_lat_blk)],
        [blk(D), blk(POOL_W + ATTN_W), blk(BLK), blk(D)],
        [jax.ShapeDtypeStruct((s, D), F32), jax.ShapeDtypeStruct((s, POOL_W + ATTN_W), BF16), jax.ShapeDtypeStruct((s, BLK), F32),
         jax.ShapeDtypeStruct((s, D), F32)],
        (h, q, k, v, u, w_pool, pool_scale, sink, w_out, mods, tables["bias"], tables["band"], tables["count"]), "parallel", ex)


def mix_bwd(dh, mo, q, k, v, u, lse, w_pool, pool_scale, sink, w_out, mods, tables, layer, t, name, ex=None):
    s = dh.shape[0]
    n_lat_blk = t // QB
    n_grp = len(POOL_WINDOWS)

    def body(dh_ref, mo_ref, q_ref, k_ref, v_ref, u_ref, lse_ref, wp_ref, ps_ref, sink_ref, wo_ref, m_ref,
             bias_ref, band_ref, band_t_ref, cnt_ref,
             dq_ref, dk_ref, dv_ref, du_ref, dmo_ref, dwp_ref, dps_ref, dsink_ref, dm_ref):
        i = pl.program_id(0)

        @pl.when(i == 0)
        def _():
            for ref in (dk_ref, dv_ref, du_ref, dwp_ref, dps_ref, dsink_ref):
                ref[...] = jnp.zeros_like(ref)

        @pl.when((i == 0) | (i == n_lat_blk))
        def _():
            dm_ref[...] = jnp.zeros_like(dm_ref)

        ws = _window(i, s)
        here = pl.ds(pl.multiple_of(i * QB, QB), QB)
        dho = dh_ref[...]
        dm_ref[2:3, :] += _sum0(dho * mo_ref[...])
        dmo = (m_ref[5:6, :] * dho).astype(BF16)
        dmo_ref[...] = dmo
        dcat = _dotg(dmo, wo_ref[...], NT)

        for gi in range(n_grp):
            cols = slice(gi * GROUP, (gi + 1) * GROUP)
            pooled = _pooled(u_ref, band_ref, cnt_ref, i, ws, gi).astype(BF16)
            dpo = dcat[:, cols]
            dps_ref[0:1, cols] += _sum0(dpo * _dot(pooled, wp_ref[gi]))
            dmixed = (dpo * ps_ref[:, cols]).astype(BF16)
            dwp_ref[gi] += _dotg(pooled, dmixed, TN)
            dpooled = _dotg(dmixed, wp_ref[gi], NT)
            du_ref[pl.ds(ws, WIN), cols] += _split_dot(band_t_ref[gi], dpooled / cnt_ref[:, gi:gi + 1])
            du_ref[here, cols] -= dpooled

        bias = bias_ref[...]
        k_all = jnp.concatenate([k_ref[pl.ds(ws, WIN), :], k_ref[t:s, :]], axis=0)
        v_all = jnp.concatenate([v_ref[pl.ds(ws, WIN), :], v_ref[t:s, :]], axis=0)
        qq, lse_all = q_ref[...], lse_ref[...]
        dqs, dsinks, dks, dvs = [], [], [], []
        for hk in range(N_HEADS // Q_GROUP):
            kv = _head_cols(hk)
            q4 = _stack_heads(qq, hk)
            lse = jnp.concatenate([lse_all[:, Q_GROUP * hk + g:Q_GROUP * hk + g + 1] for g in range(Q_GROUP)], axis=0)
            p = jnp.exp(_biased(_dotg(q4, k_all[:, kv], NT), bias) - lse)
            do = _stack_heads(dcat, hk, POOL_W).astype(BF16)
            dp = _dotg(do, v_all[:, kv], NT)
            delta = jnp.sum(p * dp, axis=1, keepdims=True)
            ds = (p * (dp - delta)).astype(BF16)
            sk = _group_column([sink_ref[layer, Q_GROUP * hk + g] for g in range(Q_GROUP)])
            dsk = -jnp.exp(sk - lse) * delta
            dq = _dot(ds, k_all[:, kv])
            for g in range(Q_GROUP):
                dqs.append(dq[g * QB:(g + 1) * QB])
                dsinks.append(_sum0(dsk[g * QB:(g + 1) * QB]))
            dks.append(_dotg(ds, q4, TN))
            dvs.append(_dotg(p.astype(BF16), do, TN))
        dq_ref[...] = jnp.concatenate(dqs, axis=1)
        dk, dv = jnp.concatenate(dks, axis=1), jnp.concatenate(dvs, axis=1)
        dk_ref[pl.ds(ws, WIN), :] += dk[:WIN]
        dv_ref[pl.ds(ws, WIN), :] += dv[:WIN]
        dk_ref[t:s, :] += dk[WIN:]
        dv_ref[t:s, :] += dv[WIN:]
        dsink_ref[0:1, :] += _lane_place(dsinks)

    blk = lambda cols: _rows(cols, QB)
    full = lambda shape: pl.BlockSpec(shape, lambda i: (0,) * len(shape))
    return _grid_call(
        body, name, s // QB,
        [blk(D), blk(D), blk(ATTN_W), _whole((s, KV_W)), _whole((s, KV_W)), _whole((s, POOL_W)),
         blk(BLK), _whole((n_grp, GROUP, GROUP), (layer,)), _whole((1, POOL_W), (layer,)),
         pl.BlockSpec(memory_space=pltpu.SMEM), _whole((POOL_W + ATTN_W, D)), _mods_spec(layer, n_lat_blk)]
        + [_case_spec(tables[key], n_lat_blk) for key in ("bias", "band", "band_t", "count")],
        [blk(ATTN_W), full((s, KV_W)), full((s, KV_W)), full((s, POOL_W)), blk(D),
         full((n_grp, GROUP, GROUP)), full((8, POOL_W)), full((8, BLK)), _acc_spec(n_lat_blk)],
        [jax.ShapeDtypeStruct((s, ATTN_W), F32), jax.ShapeDtypeStruct((s, KV_W), F32),
         jax.ShapeDtypeStruct((s, KV_W), F32), jax.ShapeDtypeStruct((s, POOL_W), F32),
         jax.ShapeDtypeStruct((s, D), BF16), jax.ShapeDtypeStruct((n_grp, GROUP, GROUP), F32),
         jax.ShapeDtypeStruct((8, POOL_W), F32), jax.ShapeDtypeStruct((8, BLK), F32), jax.ShapeDtypeStruct((2, 8, D), F32)],
        (dh, mo, q, k, v, u, lse, w_pool, pool_scale, sink, w_out, mods, tables["bias"], tables["band"], tables["band_t"],
         tables["count"]), "arbitrary", ex)


def loss_head(h, target, g, t, name):
    s = h.shape[0]
    n_lat = t // TM

    def body(h_ref, t_ref, g_ref, dh_ref, acc_ref):
        i = pl.program_id(0)

        @pl.when(i == 0)
        def _():
            acc_ref[...] = jnp.zeros_like(acc_ref)

        @pl.when(i < n_lat)
        def _():
            hh, gg = h_ref[...], g_ref[...]
            r = lax.rsqrt(jnp.mean(hh * hh, axis=-1, keepdims=True) + EPS)
            xhat = hh * r
            err = xhat * gg - t_ref[...]
            dy = err * (1.0 / D)
            dx = dy * gg
            dh_ref[...] = r * (dx - xhat * jnp.mean(dx * xhat, axis=-1, keepdims=True))
            acc_ref[0:1, :] += _sum0(dy * xhat)
            acc_ref[1:2, :] += _sum0(err * err)

        @pl.when(i >= n_lat)
        def _():
            dh_ref[...] = jnp.zeros_like(dh_ref)

    return pl.pallas_call(
        body, name=name, grid=(s // TM,),
        in_specs=[_rows(D), pl.BlockSpec((TM, D), lambda i: (jnp.minimum(i, n_lat - 1), 0)), _whole((1, D))],
        out_specs=[_rows(D), pl.BlockSpec((8, D), lambda i: (0, 0))],
        out_shape=[jax.ShapeDtypeStruct((s, D), F32), jax.ShapeDtypeStruct((8, D), F32)],
        compiler_params=_params("arbitrary"),
    )(h, target, g)


def mod_rows(c_all, w_mod, b_cols, name):
    def body(c_ref, w_ref, b_ref, o_ref):
        cc = c_ref[...]
        o_ref[...] = _dot((cc * jax.nn.sigmoid(cc)).astype(BF16), w_ref[...].astype(BF16)) + b_ref[...]

    return pl.pallas_call(
        body, name=name, grid=(2,),
        in_specs=[pl.BlockSpec((16, D), lambda l: (0, 0)), pl.BlockSpec((None, D, MOD_COLS), lambda l: (l, 0, 0)),
                  pl.BlockSpec((None, 1, MOD_COLS), lambda l: (l, 0, 0))],
        out_specs=pl.BlockSpec((None, 16, MOD_COLS), lambda l: (l, 0, 0)),
        out_shape=jax.ShapeDtypeStruct((2, 16, MOD_COLS), F32),
        compiler_params=_params("parallel"),
    )(c_all, w_mod, b_cols)


def mod_grads(c_all, dmod_cols, w_mod, name):
    def body(c_ref, d_ref, w_ref, dw_ref, dc_ref):
        @pl.when(pl.program_id(0) == 0)
        def _():
            dc_ref[...] = jnp.zeros_like(dc_ref)

        cc = c_ref[...]
        dd = d_ref[...].astype(BF16)
        dw_ref[...] = _dotg((cc * jax.nn.sigmoid(cc)).astype(BF16), dd, TN)
        dc_ref[...] += _dotg(dd, w_ref[...].astype(BF16), NT)

    return pl.pallas_call(
        body, name=name, grid=(2,),
        in_specs=[pl.BlockSpec((16, D), lambda l: (0, 0)), pl.BlockSpec((None, 16, MOD_COLS), lambda l: (l, 0, 0)),
                  pl.BlockSpec((None, D, MOD_COLS), lambda l: (l, 0, 0))],
        out_specs=[pl.BlockSpec((None, D, MOD_COLS), lambda l: (l, 0, 0)), pl.BlockSpec((16, D), lambda l: (0, 0))],
        out_shape=[jax.ShapeDtypeStruct((2, D, MOD_COLS), F32), jax.ShapeDtypeStruct((16, D), F32)],
        compiler_params=_params("arbitrary"),
    )(c_all, dmod_cols, w_mod)


def _row_tile(rows, cols, n_arrays):
    budget = VMEM_LIMIT_BYTES // 4 // (2 * 4 * n_arrays * cols)
    best = None
    for tr in range(16, rows + 1, 16):
        if rows % tr == 0 and tr <= budget:
            best = tr
    return best if best is not None else rows


def elementwise(fn, ins, out_dtypes, name, ex=None):
    rows, cols = ins[0].shape
    tr = _row_tile(rows, cols, len(ins) + len(out_dtypes))

    def body(*refs):
        outs = fn(*[r[...] for r in refs[:len(ins)]])
        for o_ref, o in zip(refs[len(ins):], outs):
            o_ref[...] = o.astype(o_ref.dtype)

    spec = pl.BlockSpec((tr, cols), lambda i: (i, 0))
    outs, got = _grid_call(body, name, rows // tr, [spec] * len(ins), [spec] * len(out_dtypes),
                           [jax.ShapeDtypeStruct((rows, cols), dt) for dt in out_dtypes], ins, "parallel", ex)
    return outs if ex is None else (outs, got)


def _adamw_tile(w, g, m, v):
    m = ADAM_B1 * m + (1.0 - ADAM_B1) * g
    v = ADAM_B2 * v + (1.0 - ADAM_B2) * (g * g)
    m_hat = m / (1.0 - ADAM_B1 ** ADAM_STEP)
    v_hat = v / (1.0 - ADAM_B2 ** ADAM_STEP)
    return -ADAM_LR * (m_hat / (jnp.sqrt(v_hat) + ADAM_EPS) + ADAM_WD * w), m, v


def adamw(w, g, m, v, name, ex=None):
    shape = w.shape
    two_d = (-1, shape[-1]) if w.ndim > 1 else (1, -1)
    outs = elementwise(_adamw_tile, [a.reshape(two_d) for a in (w, g, m, v)], [F32] * 3, name, ex)
    outs, got = outs if ex is not None else (outs, None)
    outs = [o.reshape(shape) for o in outs]
    return outs if ex is None else (outs, got)


def _prefetch_call(body, name, grid, in_specs, out_specs, out_shape, place, args, ex=None):
    if ex is None:
        spec = pltpu.PrefetchScalarGridSpec(num_scalar_prefetch=1, grid=grid, in_specs=in_specs, out_specs=out_specs)
        return pl.pallas_call(body, name=name, grid_spec=spec, out_shape=out_shape,
                              compiler_params=_params(*["parallel"] * len(grid)))(place, *args)
    n_in, n_out, ci, co = len(in_specs), len(out_specs), len(ex["ins"]), len(ex["out_shape"])
    spec = pltpu.PrefetchScalarGridSpec(num_scalar_prefetch=1, grid=grid, in_specs=list(in_specs) + _any(ci),
                                        out_specs=list(out_specs) + _any(co), scratch_shapes=ex["scratch"])
    outs = pl.pallas_call(
        _carrying(body, grid, n_in, n_out, ex, lead=1), name=name, grid_spec=spec, out_shape=list(out_shape) + ex["out_shape"],
        input_output_aliases={1 + n_in + i: n_out + j for i, j in ex["aliases"].items()},
        compiler_params=_params(*["arbitrary"] * len(grid)))(place, *args, *ex["ins"])
    return outs[:n_out], outs[n_out:]


def cast_place(w, layer, place, name):
    _, r, c = w.shape
    tr = _row_tile(r, c, 2)

    def body(p_ref, w_ref, o_ref):
        o_ref[...] = w_ref[...].astype(BF16)

    return _prefetch_call(
        body, name, (r // tr,), [pl.BlockSpec((None, tr, c), lambda i, p: (layer, i, 0))],
        pl.BlockSpec((None, tr, c), lambda i, p: (p[1], i, 0)), jax.ShapeDtypeStruct((N_SLOT, r, c), BF16), place, [w])


def pair_sum(g32, got, place, name, ex=None):
    n_slot, rh, c = got.shape
    tr = _row_tile(rh, c, 4)
    per = rh // tr

    def body(p_ref, a_ref, b_ref, o_ref, o16_ref):
        r = a_ref[...] + b_ref[...].astype(F32)
        o_ref[...] = r
        o16_ref[...] = r.astype(BF16)

    half = pl.BlockSpec((None, tr, c), lambda s, i, p: (s, i, 0))
    return _prefetch_call(
        body, name, (n_slot, per), [pl.BlockSpec((None, tr, c), lambda s, i, p: (s, p[0] * per + i, 0)), half], [half, half],
        [jax.ShapeDtypeStruct(got.shape, F32), jax.ShapeDtypeStruct(got.shape, BF16)], place, [g32, got], ex)


def chip_sum(p32, got, place, name):
    _, rh, c = p32.shape
    tr = _row_tile(rh, c, 5)
    per = rh // tr

    def body(p_ref, m_ref, r0_ref, r1_ref, r2_ref, o_ref):
        o_ref[...] = m_ref[...] + r0_ref[...].astype(F32) + r1_ref[...].astype(F32) + r2_ref[...].astype(F32)

    part = pl.BlockSpec((tr, c), lambda i, p: (i, 0))
    return _prefetch_call(
        body, name, (per,), [pl.BlockSpec((None, tr, c), lambda i, p: (p[1], i, 0)), part, part, part],
        pl.BlockSpec((tr, c), lambda i, p: (p[0] * per + i, 0)), jax.ShapeDtypeStruct((2 * rh, c), F32), place, [p32, *got])


def adamw_layers(w, g0, g1, m, v, name, ex=None):
    _, r, c = w.shape
    tr = _row_tile(r, c, 10)

    def body(w_ref, g0_ref, g1_ref, m_ref, v_ref, g_ref, d_ref, mo_ref, vo_ref):
        g = jnp.where(pl.program_id(0) == 0, g0_ref[...], g1_ref[...])
        g_ref[...] = g
        d_ref[...], mo_ref[...], vo_ref[...] = _adamw_tile(w_ref[...], g, m_ref[...], v_ref[...])

    steps = r // tr
    stacked = pl.BlockSpec((None, tr, c), lambda l, i: (l, i, 0))
    layer0 = pl.BlockSpec((tr, c), lambda l, i: (jnp.where(l == 0, i, steps - 1), 0))
    layer1 = pl.BlockSpec((tr, c), lambda l, i: (jnp.where(l == 0, 0, i), 0))
    outs, got = _grid_call(body, name, (2, steps), [stacked, layer0, layer1, stacked, stacked], [stacked] * 4,
                           [jax.ShapeDtypeStruct(w.shape, F32)] * 4, (w, g0, g1, m, v), "parallel", ex)
    return outs if ex is None else (outs, got)


def sum8(gathered, name):
    def body(*refs):
        n = len(refs) // 2
        for g_ref, o_ref in zip(refs[:n], refs[n:]):
            acc = g_ref[0]
            for dev in range(1, N_DEV):
                acc = acc + g_ref[dev]
            o_ref[...] = acc

    return pl.pallas_call(
        body, name=name,
        out_shape=[jax.ShapeDtypeStruct(a.shape[1:], F32) for a in gathered],
        compiler_params=_params(),
    )(*gathered)


PHASES = ("start", "late", "finish")


def _place():
    return lax.axis_index("x"), lax.axis_index("y"), lax.axis_index("c")


def _any(n):
    return [pl.BlockSpec(memory_space=pl.ANY)] * n


def gather8_exchange(blocks):
    n = len(blocks)

    def copy(outs, sems, ti, k, block, to, src=None):
        dst = outs[ti].at[4 * block[0] + 2 * block[1] + block[2]]
        return pltpu.make_async_remote_copy(src_ref=dst if src is None else src, dst_ref=dst, send_sem=sems[0].at[ti, k],
                                            recv_sem=sems[1].at[ti, k], device_id=to, device_id_type=MESH)

    def first(ins, outs, sems):
        x, y, c = _place()
        local, sent = [], []
        for ti in range(n):
            local.append(pltpu.make_async_copy(ins[ti], outs[ti].at[4 * x + 2 * y + c], sems[2].at[ti]))
            sent.append(copy(outs, sems, ti, 0, (x, y, c), (x, y, 1 - c), src=ins[ti]))
            sent += [copy(outs, sems, ti, 1 + j, (x, y, c), (*chip, c), src=ins[ti]) for j, chip in enumerate(_three_chips(x, y))]
        return local, sent

    def start(ins, outs, sems):
        local, sent = first(ins, outs, sems)
        for cp in local + sent:
            cp.start()

    def passed_on(outs, sems):
        x, y, c = _place()
        return [copy(outs, sems, ti, 4 + j, (*chip, c), (x, y, 1 - c)) for ti in range(n) for j, chip in enumerate(_three_chips(x, y))]

    def late(ins, outs, sems):
        x, y, c = _place()
        on = passed_on(outs, sems)
        for ti in range(n):
            for j, chip in enumerate(_three_chips(x, y)):
                copy(outs, sems, ti, 1 + j, (*chip, c), (x, y, c)).wait_recv()
                on[3 * ti + j].start()

    def finish(ins, outs, sems):
        x, y, c = _place()
        me, sibling = (x, y, c), (x, y, 1 - c)
        local, sent = first(ins, outs, sems)
        for ti in range(n):
            copy(outs, sems, ti, 0, sibling, me).wait_recv()
            for j, chip in enumerate(_three_chips(x, y)):
                copy(outs, sems, ti, 4 + j, (*chip, 1 - c), me).wait_recv()
        for cp in sent + passed_on(outs, sems):
            cp.wait_send()
        for cp in local:
            cp.wait()

    return dict(ins=list(blocks), out_shape=[jax.ShapeDtypeStruct((N_DEV,) + b.shape, b.dtype) for b in blocks], aliases={},
                start=start, late=late, finish=finish,
                scratch=[pltpu.SemaphoreType.DMA((n, 7)), pltpu.SemaphoreType.DMA((n, 7)), pltpu.SemaphoreType.DMA((n,))])


def all_gather(blocks, name):
    return run_exchange(gather8_exchange(blocks), name)


def _three_chips(x, y):
    return [(1 - x, y), (x, 1 - y), (1 - x, 1 - y)]


def gather_exchange(placed):
    n = len(placed)

    def copy(bufs, sems, ti, k, chip, core, to):
        rh = bufs[ti].shape[1] // 2
        half = bufs[ti].at[2 * chip[0] + chip[1], pl.ds(core * rh, rh), :]
        return pltpu.make_async_remote_copy(src_ref=half, dst_ref=half, send_sem=sems[0].at[ti, k], recv_sem=sems[1].at[ti, k],
                                            device_id=to, device_id_type=MESH)

    def sends(bufs, sems):
        x, y, c = _place()
        return [copy(bufs, sems, ti, k, (x, y), c, (*chip, c)) for ti in range(n) for k, chip in enumerate(_three_chips(x, y))]

    def passed_on(bufs, sems):
        x, y, c = _place()
        return [copy(bufs, sems, ti, 3 + k, chip, c, (x, y, 1 - c)) for ti in range(n) for k, chip in enumerate(_three_chips(x, y))]

    def start(ins, bufs, sems):
        for cp in sends(bufs, sems):
            cp.start()

    def late(ins, bufs, sems):
        x, y, c = _place()
        on = passed_on(bufs, sems)
        for ti in range(n):
            for k, chip in enumerate(_three_chips(x, y)):
                copy(bufs, sems, ti, k, chip, c, (x, y, c)).wait_recv()
                on[3 * ti + k].start()

    def finish(ins, bufs, sems):
        x, y, c = _place()
        for ti in range(n):
            for k, chip in enumerate(_three_chips(x, y)):
                copy(bufs, sems, ti, 3 + k, chip, 1 - c, (x, y, c)).wait_recv()
        for cp in sends(bufs, sems) + passed_on(bufs, sems):
            cp.wait_send()

    return dict(ins=list(placed), out_shape=[jax.ShapeDtypeStruct(w.shape, w.dtype) for w in placed],
                aliases={i: i for i in range(n)}, start=start, late=late, finish=finish,
                scratch=[pltpu.SemaphoreType.DMA((n, 6)), pltpu.SemaphoreType.DMA((n, 6))])


def scatter_exchange(p16):
    n = len(p16)

    def copies(ins, got, sems):
        x, y, c = _place()
        return [pltpu.make_async_remote_copy(src_ref=ins[ti].at[2 * chip[0] + chip[1]], dst_ref=got[3 * ti + k],
                                             send_sem=sems[0].at[ti, k], recv_sem=sems[1].at[ti, k], device_id=(*chip, c),
                                             device_id_type=MESH)
                for ti in range(n) for k, chip in enumerate(_three_chips(x, y))]

    def start(ins, got, sems):
        for cp in copies(ins, got, sems):
            cp.start()

    def finish(ins, got, sems):
        for cp in copies(ins, got, sems):
            cp.wait()

    return dict(ins=list(p16), out_shape=[jax.ShapeDtypeStruct(a.shape[1:], BF16) for a in p16 for _ in range(3)], aliases={},
                start=start, finish=finish, scratch=[pltpu.SemaphoreType.DMA((n, 3)), pltpu.SemaphoreType.DMA((n, 3))])


def run_exchange(ex, name):
    ci, co = len(ex["ins"]), len(ex["out_shape"])

    def body(*refs):
        ins, outs, sems = refs[:ci], refs[ci:ci + co], refs[ci + co:]
        for phase in PHASES:
            if phase in ex:
                ex[phase](ins, outs, sems)

    return pl.pallas_call(body, name=name, in_specs=_any(ci), out_specs=_any(co), out_shape=ex["out_shape"],
                          input_output_aliases=ex["aliases"], scratch_shapes=ex["scratch"])(*ex["ins"])


def _carrying(body, grid, n_in, n_out, ex, lead=0):
    ci, co = len(ex["ins"]), len(ex["out_shape"])
    first, last = (0,) * len(grid), tuple(g - 1 for g in grid)
    steps = dict(start=first, late=(grid[0] - 2,) if len(grid) == 1 and grid[0] > 2 else last, finish=last)

    def at(ids):
        return functools.reduce(jnp.logical_and, [pl.program_id(ax) == v for ax, v in enumerate(ids)])

    def carrying(*refs):
        head, refs = refs[:lead], refs[lead:]
        c_in, c_out = refs[n_in:n_in + ci], refs[n_in + ci + n_out:n_in + ci + n_out + co]
        sems = refs[n_in + ci + n_out + co:]
        for phase in PHASES:
            if phase == "finish":
                body(*head, *refs[:n_in], *refs[n_in + ci:n_in + ci + n_out])
            if phase in ex:
                pl.when(at(steps[phase]))(functools.partial(ex[phase], c_in, c_out, sems))

    return carrying


def _grid_call(body, name, grid, in_specs, out_specs, out_shape, args, sem, ex=None):
    grid = (grid,) if isinstance(grid, int) else tuple(grid)
    sems_of = (sem,) * len(grid) if isinstance(sem, str) else tuple(sem)
    n_in, n_out = len(in_specs), len(out_specs)
    if ex is None:
        return pl.pallas_call(body, name=name, grid=grid, in_specs=in_specs, out_specs=out_specs, out_shape=out_shape,
                              compiler_params=_params(*sems_of))(*args), []
    ci, co = len(ex["ins"]), len(ex["out_shape"])
    outs = pl.pallas_call(
        _carrying(body, grid, n_in, n_out, ex), name=name, grid=grid, in_specs=list(in_specs) + _any(ci),
        out_specs=list(out_specs) + _any(co), out_shape=list(out_shape) + ex["out_shape"], scratch_shapes=ex["scratch"],
        input_output_aliases={n_in + i: n_out + j for i, j in ex["aliases"].items()},
        compiler_params=_params(*["arbitrary"] * len(grid)),
    )(*args, *ex["ins"])
    return outs[:n_out], outs[n_out:]


def both(*exchanges):
    exchanges = [ex for ex in exchanges if ex is not None]
    if len(exchanges) < 2:
        return exchanges[0] if exchanges else None
    n_ins = [len(ex["ins"]) for ex in exchanges]
    n_outs = [len(ex["out_shape"]) for ex in exchanges]
    n_sems = [len(ex["scratch"]) for ex in exchanges]

    def parts(seq, counts, k):
        first = sum(counts[:k])
        return seq[first:first + counts[k]]

    def run(phase):
        def go(ins, outs, sems):
            for k, ex in enumerate(exchanges):
                if phase in ex:
                    ex[phase](parts(ins, n_ins, k), parts(outs, n_outs, k), parts(sems, n_sems, k))
        return go

    aliases = {sum(n_ins[:k]) + i: sum(n_outs[:k]) + j for k, ex in enumerate(exchanges) for i, j in ex["aliases"].items()}
    return dict(ins=[a for ex in exchanges for a in ex["ins"]], out_shape=[o for ex in exchanges for o in ex["out_shape"]],
                aliases=aliases, scratch=[s for ex in exchanges for s in ex["scratch"]], **{ph: run(ph) for ph in PHASES})


def split_outputs(got, *exchanges):
    got, out = list(got), []
    for ex in exchanges:
        n = len(ex["out_shape"]) if ex is not None else 0
        out.append(got[:n])
        got = got[n:]
    return out


def pair_exchange(g16):
    n = len(g16)

    def copies(a16, got, sems):
        x, y, c = _place()
        out = []
        for ti in range(n):
            rh = a16[ti].shape[1] // 2
            out.append(pltpu.make_async_remote_copy(
                src_ref=a16[ti].at[:, pl.ds((1 - c) * rh, rh), :], dst_ref=got[ti], send_sem=sems[0].at[ti],
                recv_sem=sems[1].at[ti], device_id=(x, y, 1 - c), device_id_type=MESH))
        return out

    def start(a16, got, sems):
        for cp in copies(a16, got, sems):
            cp.start()

    def finish(a16, got, sems):
        for cp in copies(a16, got, sems):
            cp.wait()

    return dict(ins=list(g16), out_shape=[jax.ShapeDtypeStruct((a.shape[0], a.shape[1] // 2, a.shape[2]), BF16) for a in g16],
                aliases={}, start=start, finish=finish, scratch=[pltpu.SemaphoreType.DMA((n,)), pltpu.SemaphoreType.DMA((n,))])


def _gather_half(buf, chip, core):
    rh = buf.shape[1] // 2
    return buf.at[2 * chip[0] + chip[1], pl.ds(core * rh, rh), :]


def gather_start(placed, name):
    n = len(placed)
    hbm, sem = pl.BlockSpec(memory_space=pltpu.HBM), pl.BlockSpec(memory_space=pltpu.SEMAPHORE)

    def body(*refs):
        bufs, send_sems, recv_sems, token_ref = refs[:n], refs[n], refs[n + 1], refs[-1]
        x, y, c = _place()
        for ti in range(n):
            for k, chip in enumerate(_three_chips(x, y)):
                half = _gather_half(bufs[ti], (x, y), c)
                pltpu.make_async_remote_copy(src_ref=half, dst_ref=half, send_sem=send_sems.at[3 * ti + k],
                                             recv_sem=recv_sems.at[3 * ti + k], device_id=(*chip, c), device_id_type=MESH).start()
        token_ref[...] = jnp.zeros_like(token_ref)

    return pl.pallas_call(
        body, name=name,
        out_shape=(pltpu.SemaphoreType.DMA((3 * n,)), pltpu.SemaphoreType.DMA((3 * n,)), *[pltpu.HBM(w.shape, w.dtype) for w in placed],
                   jax.ShapeDtypeStruct((8, BLK), F32)),
        in_specs=(hbm,) * n, out_specs=(sem, sem, *(hbm,) * n, pl.BlockSpec(memory_space=pltpu.VMEM)),
        input_output_aliases={i: 2 + i for i in range(n)},
        compiler_params=pltpu.CompilerParams(has_side_effects=pltpu.SideEffectType.DATAFLOW_SIDE_EFFECTING),
    )(*[pltpu.with_memory_space_constraint(w, pltpu.HBM) for w in placed])


def gather_wait(send_sems, recv_sems, bufs, after, name):
    n = len(bufs)
    hbm, sem = pl.BlockSpec(memory_space=pltpu.HBM), pl.BlockSpec(memory_space=pltpu.SEMAPHORE)

    def body(*refs):
        bufs, send_sems, recv_sems = refs[:n], refs[n], refs[n + 1]
        x, y, c = _place()
        for ti in range(n):
            for k, chip in enumerate(_three_chips(x, y)):
                mine, theirs = _gather_half(bufs[ti], (x, y), c), _gather_half(bufs[ti], chip, c)
                cp = pltpu.make_async_remote_copy(src_ref=mine, dst_ref=theirs, send_sem=send_sems.at[3 * ti + k],
                                                  recv_sem=recv_sems.at[3 * ti + k], device_id=(*chip, c), device_id_type=MESH)
                cp.wait_send()
                cp.wait_recv()

    return pl.pallas_call(
        body, name=name, out_shape=tuple(pltpu.HBM(w.shape, w.dtype) for w in bufs),
        in_specs=(*(hbm,) * n, sem, sem, *_any(len(after))), out_specs=(hbm,) * n,
        input_output_aliases={i: i for i in range(n)},
        compiler_params=pltpu.CompilerParams(has_side_effects=pltpu.SideEffectType.DATAFLOW_SIDE_EFFECTING),
    )(*bufs, send_sems, recv_sems, *after)


def pass_on_exchange(bufs):
    n = len(bufs)

    def copies(refs, sems, core):
        x, y, c = _place()
        return [pltpu.make_async_remote_copy(src_ref=_gather_half(refs[ti], chip, c if core == "mine" else 1 - c),
                                             dst_ref=_gather_half(refs[ti], chip, c if core == "mine" else 1 - c),
                                             send_sem=sems[0].at[ti, k], recv_sem=sems[1].at[ti, k], device_id=(x, y, 1 - c),
                                             device_id_type=MESH)
                for ti in range(n) for k, chip in enumerate(_three_chips(x, y))]

    def start(ins, refs, sems):
        for cp in copies(refs, sems, "mine"):
            cp.start()

    def finish(ins, refs, sems):
        for cp in copies(refs, sems, "mine"):
            cp.wait_send()
        for cp in copies(refs, sems, "sibling's"):
            cp.wait_recv()

    return dict(ins=list(bufs), out_shape=[jax.ShapeDtypeStruct(w.shape, w.dtype) for w in bufs], aliases={i: i for i in range(n)},
                start=start, finish=finish, scratch=[pltpu.SemaphoreType.DMA((n, 3)), pltpu.SemaphoreType.DMA((n, 3))])


def _scatter_copies(src_ref, lands, send_sems, recv_sems):
    x, y, c = _place()
    return [pltpu.make_async_remote_copy(src_ref=src_ref.at[2 * chip[0] + chip[1]], dst_ref=lands[k], send_sem=send_sems.at[k],
                                         recv_sem=recv_sems.at[k], device_id=(*chip, c), device_id_type=MESH)
            for k, chip in enumerate(_three_chips(x, y))]


def scatter_start(p16, name):
    hbm, sem = pl.BlockSpec(memory_space=pltpu.HBM), pl.BlockSpec(memory_space=pltpu.SEMAPHORE)

    def body(src_ref, l0_ref, l1_ref, l2_ref, send_sems, recv_sems, src_thru, o0_ref, o1_ref, o2_ref, token_ref):
        for cp in _scatter_copies(src_ref, (l0_ref, l1_ref, l2_ref), send_sems, recv_sems):
            cp.start()
        token_ref[...] = jnp.zeros_like(token_ref)

    land = [pltpu.with_memory_space_constraint(lax.empty(p16.shape[1:], BF16), pltpu.HBM) for _ in range(3)]
    return pl.pallas_call(
        body, name=name,
        out_shape=(pltpu.SemaphoreType.DMA((3,)), pltpu.SemaphoreType.DMA((3,)), pltpu.HBM(p16.shape, BF16),
                   *[pltpu.HBM(p16.shape[1:], BF16)] * 3, jax.ShapeDtypeStruct((8, BLK), F32)),
        in_specs=(hbm,) * 4, out_specs=(sem, sem, hbm, hbm, hbm, hbm, pl.BlockSpec(memory_space=pltpu.VMEM)),
        input_output_aliases={0: 2, 1: 3, 2: 4, 3: 5},
        compiler_params=pltpu.CompilerParams(has_side_effects=pltpu.SideEffectType.DATAFLOW_SIDE_EFFECTING),
    )(pltpu.with_memory_space_constraint(p16, pltpu.HBM), *land)


def scatter_wait(send_sems, recv_sems, src_thru, lands, after, name):
    hbm, sem = pl.BlockSpec(memory_space=pltpu.HBM), pl.BlockSpec(memory_space=pltpu.SEMAPHORE)

    def body(src_ref, l0_ref, l1_ref, l2_ref, send_sems, recv_sems, *rest):
        for cp in _scatter_copies(src_ref, (l0_ref, l1_ref, l2_ref), send_sems, recv_sems):
            cp.wait_send()
            cp.wait_recv()

    return pl.pallas_call(
        body, name=name, out_shape=(pltpu.HBM(src_thru.shape, BF16), *[pltpu.HBM(lands[0].shape, BF16)] * 3),
        in_specs=(hbm, hbm, hbm, hbm, sem, sem, *_any(len(after))), out_specs=(hbm,) * 4,
        input_output_aliases={0: 0, 1: 1, 2: 2, 3: 3},
        compiler_params=pltpu.CompilerParams(has_side_effects=pltpu.SideEffectType.DATAFLOW_SIDE_EFFECTING),
    )(src_thru, *lands, send_sems, recv_sems, *after)[1:]


def pair_fill_exchange(halves):
    n = len(halves)

    def copies(bufs, sems, core):
        x, y, c = _place()
        out = []
        for ti in range(n):
            rh = bufs[ti].shape[0] // 2
            rows = bufs[ti].at[pl.ds((c if core == "mine" else 1 - c) * rh, rh), :]
            out.append(pltpu.make_async_remote_copy(src_ref=rows, dst_ref=rows, send_sem=sems[0].at[ti], recv_sem=sems[1].at[ti],
                                                    device_id=(x, y, 1 - c), device_id_type=MESH))
        return out

    def start(ins, bufs, sems):
        for cp in copies(bufs, sems, "mine"):
            cp.start()

    def finish(ins, bufs, sems):
        for cp in copies(bufs, sems, "mine"):
            cp.wait_send()
        for cp in copies(bufs, sems, "sibling's"):
            cp.wait_recv()

    return dict(ins=list(halves), out_shape=[jax.ShapeDtypeStruct(a.shape, a.dtype) for a in halves],
                aliases={i: i for i in range(n)}, start=start, finish=finish,
                scratch=[pltpu.SemaphoreType.DMA((n,)), pltpu.SemaphoreType.DMA((n,))])


def pair_gather(halves, name):
    return run_exchange(pair_fill_exchange(halves), name)


def reduce_small(dm_f1, dm_mix, dm_gate, dm_f2, loss_blk, name):
    def body(f1_ref, mix_ref, gate_ref, f2_ref, l_ref, tot_ref, rows_ref, fin_ref):
        rows_ref[...] = jnp.zeros_like(rows_ref)
        tot_ref[...] = jnp.zeros_like(tot_ref)
        mod_src = [(f1_ref, 0), (f1_ref, 1), (f1_ref, 2), (mix_ref, 0), (mix_ref, 1), (gate_ref, 2),
                   (f2_ref, 0), (f2_ref, 1), (f2_ref, 2)]
        norm_src = [(f1_ref, 3), (mix_ref, 3), (f2_ref, 3)]
        for l in range(2):
            for k, (ref, r) in enumerate(mod_src + norm_src):
                lat = ref[0, l, 0, r:r + 1, :]
                ctx = ref[0, l, 1, r:r + 1, :]
                for dev in range(N_DEV):
                    if dev:
                        lat = lat + ref[dev, l, 0, r:r + 1, :]
                        ctx = ctx + ref[dev, l, 1, r:r + 1, :]
                    if k < N_MOD:
                        rows_ref[l, dev, k:k + 1, :] = ref[dev, l, 0, r:r + 1, :]
                if k < N_MOD:
                    rows_ref[l, N_DEV, k:k + 1, :] = ctx
                tot_ref[l, k:k + 1, :] = lat + ctx
        acc = l_ref[0]
        for dev in range(1, N_DEV):
            acc = acc + l_ref[dev]
        loss = (0.5 / D) * jnp.sum(acc[1:2, :], axis=1, keepdims=True)
        row = lax.broadcasted_iota(jnp.int32, (8, D), 0)
        fin_ref[...] = jnp.where(row == 0, acc[0:1, :], loss)

    return pl.pallas_call(
        body, name=name,
        out_shape=[jax.ShapeDtypeStruct((2, 16, D), F32), jax.ShapeDtypeStruct((2, 16, 16, D), F32),
                   jax.ShapeDtypeStruct((8, D), F32)],
        compiler_params=_params(),
    )(dm_f1, dm_mix, dm_gate, dm_f2, loss_blk)


def rope_tables(t, s):
    rows = t // GRID_W
    row = jnp.repeat(jnp.arange(rows), GRID_W).astype(F32)
    col = jnp.tile(jnp.arange(GRID_W), rows).astype(F32)
    inv = ROPE_BASE ** (-jnp.arange(0, HEAD // 2, 2, dtype=F32) / (HEAD // 2))
    ang = jnp.concatenate([row[:, None] * inv, col[:, None] * inv], axis=-1)
    cos, sin = jnp.cos(ang), jnp.sin(ang)
    cos = jnp.concatenate([jnp.tile(cos, (1, 4)), jnp.ones((s - t, BLK), F32)], axis=0)
    sin = jnp.concatenate([jnp.tile(jnp.concatenate([-sin, sin], axis=1), (1, 2)), jnp.zeros((s - t, BLK), F32)], axis=0)
    return cos, sin


BIG = ("ffn1_in", "ffn1_out", "w_in", "w_out", "ffn2_in", "ffn2_out")
GROUPS = dict(ffn1=("ffn1_in", "ffn1_out"), mix=("w_in", "w_out"), ffn2=("ffn2_in", "ffn2_out"))
GATHER_BEHIND = {("ffn1", 0): [("w_in", 0), ("ffn2_out", 0), ("ffn1_out", 1)], ("proj", 0): [("w_out", 0)],
                 ("mix", 0): [("ffn2_in", 0)], ("ffn2", 0): [("ffn1_in", 1), ("w_in", 1)],
                 ("ffn1", 1): [("ffn2_in", 1), ("w_out", 1)], ("mix", 1): [("ffn2_out", 1)]}


def _slot_major(name, g):
    if name == "w_in":
        return jnp.stack(jnp.split(g, N_SLOT, axis=1), axis=0)
    if name in ("ffn1_in", "ffn2_in"):
        return g
    return g.reshape(N_SLOT, g.shape[0] // N_SLOT, g.shape[1])


def _whole_weight(name, buf):
    if name == "w_in":
        return buf.transpose(1, 0, 2).reshape(D, PROJ_W)
    if name in ("ffn1_in", "ffn2_in"):
        return buf
    return buf.reshape(-1, buf.shape[2])


def local_step(x1, ctx1, target, mods, norms, nfinal, placed, w_pool, pool_scale, sink, place, small_blocks):
    t, s = x1.shape[0], x1.shape[0] + ctx1.shape[0]
    n_lat = t // TM
    cos, sin = rope_tables(t, s)
    tables = mix_tables(t, s)
    wts ={name: list(pair) for name, pair in placed.items()}

    def gather(tensors):
        return gather_exchange([wts[name][l] for name, l in tensors])

    def gathered(tensors, arrays):
        for (name, l), whole in zip(tensors, arrays):
            wts[name][l] = whole

    def weight(name, l):
        return _whole_weight(name, wts[name][l])

    def fwd_ex(grp, l):
        groups = GATHER_BEHIND.get((grp, l))
        return (groups, gather(groups)) if groups else (None, None)

    h = jnp.concatenate([x1, ctx1], axis=0)
    saved = []
    for l in range(2):
        h0 = h
        groups, ex = fwd_ex("ffn1", l)
        (h1, ab1, f1), got = ffn_fwd(h0, mods, norms[0], weight("ffn1_in", l), weight("ffn1_out", l), l, 0, n_lat, f"ffn1_fwd_{l}", ex)
        gathered(groups or [], got)
        groups, ex = fwd_ex("proj", l)
        (u, q, k, v), got = proj_fwd(h1, mods, norms[1], weight("w_in", l), cos, sin, l, n_lat, f"proj_fwd_{l}", ex)
        gathered(groups or [], got)
        groups, ex = fwd_ex("mix", l)
        (h2, cat, lse, mo), got = mix_fwd(h1, q, k, v, u, w_pool, pool_scale, sink, weight("w_out", l), mods, tables, l, t,
                                          f"mix_fwd_{l}", ex)
        gathered(groups or [], got)
        groups, ex = fwd_ex("ffn2", l)
        (h, ab2, f2), got = ffn_fwd(h2, mods, norms[2], weight("ffn2_in", l), weight("ffn2_out", l), l, 6, n_lat, f"ffn2_fwd_{l}", ex)
        gathered(groups or [], got)
        saved.append((h0, ab1, f1, h1, u, q, k, v, cat, lse, mo, h2, ab2, f2))
    dh, loss_blk = loss_head(h, target, nfinal, t, "loss_head")

    halves = {name: [None, None] for name in BIG}
    pending = []

    def summed_in_pair(grp, l, name_a, g_a, name_b, wgrad_b):
        g_b, got_a = wgrad_b(pair_exchange([_slot_major(name_a, g_a[1])]))
        sum_a, got_b = pair_sum(_slot_major(name_a, g_a[0]), got_a[0], place, f"pair_sum_{name_a}_{l}",
                                pair_exchange([_slot_major(name_b, g_b[1])]))
        sums = {name_a: sum_a, name_b: pair_sum(_slot_major(name_b, g_b[0]), got_b[0], place, f"pair_sum_{name_b}_{l}")}
        pending.append((grp, l, [sums[n] for n in GROUPS[grp]]))

    lacking = []

    def riders():
        return (scatter_exchange([p16 for _, p16 in pending[0][2]]) if pending else None,
                pair_fill_exchange([halves[name][l] for name, l in lacking]) if lacking else None)

    def carried(got, exs):
        got, filled = split_outputs(got, *exs)
        for (name, l), whole in zip(list(lacking), filled):
            halves[name][l] = whole
            lacking.remove((name, l))
        if pending:
            grp, l, pairs = pending.pop(0)
            for i, name in enumerate(GROUPS[grp]):
                halves[name][l] = chip_sum(pairs[i][0], got[3 * i:3 * i + 3], place, f"chip_sum_{name}_{l}")
                lacking.append((name, l))

    small = [None, None]
    for l in (1, 0):
        h0, ab1, f1, h1, u, q, k, v, cat, lse, mo, h2, ab2, f2 = saved[l]
        exs = riders()
        (dh, dab, df, n, act, dm_f2), got = ffn_bwd(h2, ab2, f2, dh, mods, norms[2], weight("ffn2_in", l), weight("ffn2_out", l),
                                                    l, 6, n_lat, f"ffn2_bwd_{l}", both(*exs))
        carried(got, exs)
        g_in, _ = wgrad(n, dab, D // 2, FF_COLS, FF_COLS,f"ffn2_in_wgrad_{l}")
        summed_in_pair("ffn2", l, "ffn2_in", g_in, "ffn2_out",
                       lambda ex, a=act, b=df: wgrad(a, b, D_FF // 2, D // 2, None,f"ffn2_out_wgrad_{l}", ex))
        exs = riders()
        (dq, dk, dv, du, dmo, dwp, dps, dsink, dm_gate), got = mix_bwd(
            dh, mo, q, k, v, u, lse, w_pool, pool_scale, sink, weight("w_out", l), mods, tables, l, t, f"mix_bwd_{l}", both(*exs))
        carried(got, exs)
        g_wo, _ = wgrad(cat, dmo, POOL_W + ATTN_W, D, None, f"w_out_wgrad_{l}")
        dh, dp, n, dm_mix = proj_bwd(h1, du, dq, dk, dv, dh, mods, norms[1], weight("w_in", l), cos, sin, l, n_lat, f"proj_bwd_{l}")
        summed_in_pair("mix", l, "w_out", g_wo, "w_in",
                       lambda ex, a=n, b=dp: wgrad(a, b, D, PROJ_W // 2, None, f"w_in_wgrad_{l}", ex))
        exs = riders()
        (dh, dab, df, n, act, dm_f1), got = ffn_bwd(h0, ab1, f1, dh, mods, norms[0], weight("ffn1_in", l), weight("ffn1_out", l),
                                                    l, 0, n_lat, f"ffn1_bwd_{l}", both(*exs))
        carried(got, exs)
        small[l] = dict(dm_f1=dm_f1, dm_mix=dm_mix, dm_gate=dm_gate, dm_f2=dm_f2, dwp=dwp, dps=dps, dsink=dsink)
        if l:
            g_in, _ = wgrad(n, dab, D // 2, FF_COLS, FF_COLS,f"ffn1_in_wgrad_{l}")
            summed_in_pair("ffn1", l, "ffn1_in", g_in, "ffn1_out",
                           lambda ex, a=act, b=df: wgrad(a, b, D_FF // 2, D // 2, None,f"ffn1_out_wgrad_{l}", ex))
    g_out, _ = wgrad(act, df, D_FF // 2, D // 2, None, "ffn1_out_wgrad_0")
    got = run_exchange(pair_exchange([_slot_major("ffn1_out", g_out[1])]), "pair_exchange_ffn1_out_0")
    p32, p16 = pair_sum(_slot_major("ffn1_out", g_out[0]), got[0], place, "pair_sum_ffn1_out_0")
    riding = (gather8_exchange(small_blocks(small, loss_blk)), scatter_exchange([p16]),
              pair_fill_exchange([halves[name][l] for name, l in lacking]))
    g_in, got = wgrad(n, dab, D // 2, FF_COLS, FF_COLS,"ffn1_in_wgrad_0", both(*riding))
    small_all, got, filled = split_outputs(got, *riding)
    for (name, l), whole in zip(lacking, filled):
        halves[name][l] = whole
    (halves["ffn1_out"][0],) = pair_gather([chip_sum(p32, got, place, "chip_sum_ffn1_out_0")], "pair_gather_ffn1_out_0")
    got = run_exchange(pair_exchange([_slot_major("ffn1_in", g_in[1])]), "pair_exchange_ffn1_in_0")
    return dh[:t], halves, pair_sum(_slot_major("ffn1_in", g_in[0]), got[0], place, "pair_sum_ffn1_in_0"), small_all


def _silu_grad(z):
    sg = jax.nn.sigmoid(z)
    return sg * (1 + z * (1 - sg))


def kernel(x, c, ctx, c_ctx, w_mod, b_mod, norm_ffn1, w_ffn1_in, w_ffn1_out, norm_mix, w_in, w_pool, pool_scale, sink, w_out, norm_ffn2, w_ffn2_in, w_ffn2_out, norm_final, loss_target, m_c_ctx, m_w_mod, m_b_mod, m_norm_ffn1, m_w_ffn1_in, m_w_ffn1_out, m_norm_mix, m_w_in, m_w_pool, m_pool_scale, m_sink, m_w_out, m_norm_ffn2, m_w_ffn2_in, m_w_ffn2_out, m_norm_final, v_c_ctx, v_w_mod, v_b_mod, v_norm_ffn1, v_w_ffn1_in, v_w_ffn1_out, v_norm_mix, v_w_in, v_w_pool, v_pool_scale, v_sink, v_w_out, v_norm_ffn2, v_w_ffn2_in, v_w_ffn2_out, v_norm_final):
    px, py, pc = _place()
    slot, me = 2 * px + py, 4 * px + 2 * py + pc
    n_grp = len(POOL_WINDOWS)

    (c_rows,) = all_gather([c.reshape(8, D // 8)], "gather_c")
    c_all = jnp.concatenate([c_rows.reshape(N_DEV, D), c_ctx.reshape(1, D), jnp.zeros((16 - N_DEV - 1, D), F32)], axis=0)
    b_cols = lax.dynamic_slice(b_mod, (0, slot * MOD_COLS), (2, MOD_COLS)).reshape(2, 1, MOD_COLS)
    (mod_parts,) = all_gather([mod_rows(c_all, w_mod, b_cols, "mod_rows")], "gather_mods")
    mods_all = mod_parts[0::2].transpose(1, 2, 0, 3).reshape(2, 16, N_MOD * D)
    mx = lax.dynamic_slice(mods_all, (0, me, 0), (2, 1, N_MOD * D)).reshape(2, N_MOD, D)
    mc = mods_all[:, N_DEV].reshape(2, N_MOD, D)
    pad = jnp.zeros((2, 16 - N_MOD, D), F32)
    mods = jnp.stack([jnp.concatenate([mx, pad], axis=1), jnp.concatenate([mc, pad], axis=1)], axis=1)

    place = jnp.stack([pc, slot]).astype(jnp.int32)
    shards = dict(ffn1_in=w_ffn1_in, ffn1_out=w_ffn1_out, w_in=w_in, w_out=w_out, ffn2_in=w_ffn2_in, ffn2_out=w_ffn2_out)
    first = [("ffn1_in", 0), ("ffn1_out", 0)]
    placed = {name: [None, None] for name in BIG}
    for name, l in first:
        placed[name][l] = cast_place(shards[name], l, place, f"cast_{name}_{l}")
    send_sems, recv_sems, *bufs, token = gather_start([placed[name][l] for name, l in first], "gather_first_start")
    others = [(name, l) for name in BIG for l in range(2) if (name, l) not in first]
    for name, l in others:
        placed[name][l] = cast_place(shards[name], l, place, f"cast_{name}_{l}")
    bufs = gather_wait(send_sems, recv_sems, bufs, [placed[name][l] for name, l in others], "gather_first_wait")
    for (name, l), whole in zip(first, run_exchange(pass_on_exchange(bufs), "gather_first_pass_on")):
        placed[name][l] = whole
    norms = [g.reshape(2, 1, D) for g in (norm_ffn1, norm_mix, norm_ffn2)]
    row_sums = ("dm_f1", "dm_mix", "dm_gate", "dm_f2")

    def small_blocks(small, loss_blk):
        stacked = {k: jnp.stack([small[0][k], small[1][k]]) for k in row_sums + ("dwp", "dps", "dsink")}
        return ([stacked[k].reshape(32, D) for k in row_sums]
                + [stacked["dwp"].reshape(2 * n_grp * GROUP, GROUP), stacked["dps"].reshape(16, POOL_W),
                   stacked["dsink"].reshape(16, BLK), loss_blk])

    dx, halves, last_pair, small_all = local_step(x[0], ctx[0], loss_target[0], mods, norms, norm_final.reshape(1, D), placed,
                                                   w_pool.astype(BF16), pool_scale.reshape(2, 1, POOL_W), sink, place, small_blocks)
    grads = {}

    *g_dm, g_dwp, g_dps, g_dsink, g_loss = small_all
    tot, rows, fin = reduce_small(*[g.reshape(N_DEV, 2, 2, 8, D) for g in g_dm], g_loss, "reduce_small")
    s_dwp, s_dps, s_dsink = sum8([g_dwp, g_dps, g_dsink], "sum_pool_sink")
    grads.update(
        w_pool=s_dwp.reshape(2, n_grp, GROUP, GROUP), pool_scale=s_dps.reshape(2, 8, POOL_W)[:, 0],
        sink=s_dsink.reshape(2, 8, BLK)[:, 0, :N_HEADS], b_mod=tot[:, :N_MOD].reshape(2, N_MOD * D),
        norm_ffn1=tot[:, N_MOD], norm_mix=tot[:, N_MOD + 1], norm_ffn2=tot[:, N_MOD + 2], norm_final=fin[0])
    loss = fin[1, 0]

    dmod_cols = lax.dynamic_slice(rows[:, :, :N_MOD, :].reshape(2, 16, N_MOD * D), (0, 0, slot * MOD_COLS), (2, 16, MOD_COLS))
    grads["w_mod"], dc = mod_grads(c_all, dmod_cols, w_mod, "mod_grads")
    (g_dc,) = all_gather([dc], "gather_dc")
    (s_dc,) = sum8([g_dc], "sum_dc")
    (d_c_ctx,) = elementwise(lambda d, z: (0.5 * d * _silu_grad(z),), [s_dc[N_DEV:N_DEV + 1], c_ctx.reshape(1, D)], [F32], "c_ctx_grad")
    send_sems, recv_sems, src_thru, *lands, token = scatter_start(last_pair[1], "scatter_last_start")
    grads["c_ctx"] = d_c_ctx.reshape(D) + token[0, :1]

    given = dict(c_ctx=(c_ctx, m_c_ctx, v_c_ctx), w_mod=(w_mod, m_w_mod, v_w_mod), b_mod=(b_mod, m_b_mod, v_b_mod),
                 norm_ffn1=(norm_ffn1, m_norm_ffn1, v_norm_ffn1), w_ffn1_in=(w_ffn1_in, m_w_ffn1_in, v_w_ffn1_in),
                 w_ffn1_out=(w_ffn1_out, m_w_ffn1_out, v_w_ffn1_out), norm_mix=(norm_mix, m_norm_mix, v_norm_mix),
                 w_in=(w_in, m_w_in, v_w_in), w_pool=(w_pool, m_w_pool, v_w_pool),
                 pool_scale=(pool_scale, m_pool_scale, v_pool_scale), sink=(sink, m_sink, v_sink), w_out=(w_out, m_w_out, v_w_out),
                 norm_ffn2=(norm_ffn2, m_norm_ffn2, v_norm_ffn2), w_ffn2_in=(w_ffn2_in, m_w_ffn2_in, v_w_ffn2_in),
                 w_ffn2_out=(w_ffn2_out, m_w_ffn2_out, v_w_ffn2_out), norm_final=(norm_final, m_norm_final, v_norm_final))
    shard = {(name, l): halves[name][l] for name in BIG for l in range(2)}

    def update(name):
        w, m, v = given[name]
        if name in BIG or name[2:] in BIG:
            key = name if name in BIG else name[2:]
            return adamw_layers(w, shard[key, 0], shard[key, 1], m, v, f"adamw_{name}")
        return [grads[name], *adamw(w, grads[name], m, v, f"adamw_{name}")]

    done = {name: update(name) for name in given if name != "w_ffn1_in"}
    got = scatter_wait(send_sems, recv_sems, src_thru, lands, [done[name][3] for name in done if name[2:] in BIG or name in BIG]
                       + [done["w_mod"][3]], "scatter_last_wait")
    (shard["ffn1_in", 0],) = pair_gather([chip_sum(last_pair[0], got, place, "chip_sum_ffn1_in_0")], "grad_pair_gather_last")
    done["w_ffn1_in"] = update("w_ffn1_in")
    return (loss, dx[None], *[done[name][i] for i in range(4) for name in given])
```

```python
import functools

import jax
import jax.numpy as jnp
from jax import lax
from jax.experimental import pallas as pl
from jax.experimental.pallas import tpu as pltpu

F32, BF16 = jnp.float32, jnp.bfloat16
D = 1024
D_FF = 2816
N_SLOT = 4
FF_COLS = 2 * D_FF // N_SLOT
N_MOD = 9
MOD_COLS = N_MOD * D // N_SLOT
POOL_W, ATTN_W, KV_W = 512, 512, 128
PROJ_W = POOL_W + ATTN_W + 2 * KV_W
N_HEADS, Q_GROUP, HEAD = 8, 4, 64
GROUP = 128
POOL_WINDOWS = (2, 4, 8, 16)
BLK = 128
QB = 256
WIN = QB + 2 * BLK
GRID_W = 64
ROPE_BASE = 10000.0
EPS = 1e-6
NEG_INF = -1e30
TM = 256
N_DEV = 8
VMEM_LIMIT_BYTES = 56 * 1024 * 1024
WGRAD_VMEM_BYTES = 44 * 1024 * 1024
ADAM_LR, ADAM_B1, ADAM_B2, ADAM_EPS, ADAM_WD, ADAM_STEP = 0.001, 0.9, 0.999, 1e-08, 0.01, 10
MESH = pl.DeviceIdType.MESH
NT = (((1,), (1,)), ((), ()))
TN = (((0,), (0,)), ((), ()))


def _params(*sem):
    return pltpu.CompilerParams(dimension_semantics=sem, vmem_limit_bytes=VMEM_LIMIT_BYTES)


def _whole(shape, lead=()):
    idx = tuple(lead) + (0,) * len(shape)
    return pl.BlockSpec((None,) * len(lead) + tuple(shape), lambda *_: idx, pipeline_mode=pl.Buffered(1))


def _rows(cols, tm=TM):
    return pl.BlockSpec((tm, cols), lambda i: (i, 0))


def _mods_spec(layer, n_lat):
    return pl.BlockSpec((None, None, 16, D), lambda i: (layer, (i >= n_lat).astype(jnp.int32), 0, 0))


def _acc_spec(n_lat):
    return pl.BlockSpec((None, 8, D), lambda i: ((i >= n_lat).astype(jnp.int32), 0, 0))


def _dot(a, b):
    return jnp.dot(a, b, preferred_element_type=F32)


def _dotg(a, b, dims):
    return lax.dot_general(a, b, dims, preferred_element_type=F32)


def _sum0(v):
    return jnp.sum(v, axis=0, keepdims=True)


def _norm_mod(h, g, shift, scale):
    r = lax.rsqrt(jnp.mean(h * h, axis=-1, keepdims=True) + EPS)
    xhat = h * r
    y = xhat * g
    return y * (1 + scale) + shift, xhat, r, y


def _norm_mod_bwd(dn, xhat, r, y, g, scale):
    dy = dn * (1 + scale)
    dx = dy * g
    dh = r * (dx - xhat * jnp.mean(dx * xhat, axis=-1, keepdims=True))
    return _sum0(dn), _sum0(dn * y), _sum0(dy * xhat), dh


def _swap_halves(v):
    w = v.shape[1]
    lane = lax.broadcasted_iota(jnp.int32, v.shape, 1)
    return jnp.where(lane % HEAD < HEAD // 2, pltpu.roll(v, w - HEAD // 2, axis=1), pltpu.roll(v, HEAD // 2, axis=1))


def _tile_lanes(t, width):
    return t if width == t.shape[1] else jnp.concatenate([t] * (width // t.shape[1]), axis=1)


def _rope(v, cos, sin):
    return v * _tile_lanes(cos, v.shape[1]) + _swap_halves(v) * _tile_lanes(sin, v.shape[1])


def _unrope(g, cos, sin):
    return g * _tile_lanes(cos, g.shape[1]) + _swap_halves(g * _tile_lanes(sin, g.shape[1]))


def ffn_fwd(h, mods, g, w4, wo, layer, k0, n_lat, name, ex=None):
    s = h.shape[0]

    def body(h_ref, m_ref, g_ref, w_ref, wo_ref, ho_ref, ab_ref, f_ref):
        hh = h_ref[...]
        n, _, _, _ = _norm_mod(hh, g_ref[...], m_ref[k0:k0 + 1, :], m_ref[k0 + 1:k0 + 2, :])
        nb = n.astype(BF16)
        acc = jnp.zeros((TM, D), F32)
        for j in range(2):
            a = _dot(nb, w_ref[j])
            b = _dot(nb, w_ref[2 + j])
            ab_ref[:, j * FF_COLS:(j + 1) * FF_COLS] = a.astype(BF16)
            ab_ref[:, (2 + j) * FF_COLS:(3 + j) * FF_COLS] = b.astype(BF16)
            act = (a * jax.nn.sigmoid(a) * b).astype(BF16)
            acc = acc + _dot(act, wo_ref[j * FF_COLS:(j + 1) * FF_COLS, :])
        f_ref[...] = acc
        ho_ref[...] = hh + 0.5 * m_ref[k0 + 2:k0 + 3, :] * acc

    return _grid_call(
        body, name, s // TM,
        [_rows(D), _mods_spec(layer, n_lat), _whole((1, D), (layer,)), _whole((N_SLOT, D, FF_COLS)), _whole((D_FF, D))],
        [_rows(D), _rows(2 * D_FF), _rows(D)],
        [jax.ShapeDtypeStruct((s, D), F32), jax.ShapeDtypeStruct((s, 2 * D_FF), BF16), jax.ShapeDtypeStruct((s, D), F32)],
        (h, mods, g, w4, wo), "parallel", ex)


def ffn_bwd(h, ab, f, dh, mods, g, w4, wo, layer, k0, n_lat, name, ex=None):
    s = h.shape[0]

    def body(h_ref, ab_ref, f_ref, dh_ref, m_ref, g_ref, w_ref, wo_ref, dhi_ref, dab_ref, df_ref, n_ref, act_ref, dm_ref):
        i = pl.program_id(0)

        @pl.when((i == 0) | (i == n_lat))
        def _():
            dm_ref[...] = jnp.zeros_like(dm_ref)

        hh, dho, gg = h_ref[...], dh_ref[...], g_ref[...]
        scale, gate = m_ref[k0 + 1:k0 + 2, :], m_ref[k0 + 2:k0 + 3, :]
        n, xhat, r, y = _norm_mod(hh, gg, m_ref[k0:k0 + 1, :], scale)
        n_ref[...] = n.astype(BF16)
        dgate = _sum0(dho * (0.5 * f_ref[...]))
        dfb = ((0.5 * gate) * dho).astype(BF16)
        df_ref[...] = dfb
        dn = jnp.zeros((TM, D), F32)
        for j in range(2):
            a = ab_ref[:, j * FF_COLS:(j + 1) * FF_COLS].astype(F32)
            b = ab_ref[:, (2 + j) * FF_COLS:(3 + j) * FF_COLS].astype(F32)
            sg = jax.nn.sigmoid(a)
            sa = a * sg
            act_ref[:, j * FF_COLS:(j + 1) * FF_COLS] = (sa * b).astype(BF16)
            dact = _dotg(dfb, wo_ref[j * FF_COLS:(j + 1) * FF_COLS, :], NT)
            da = (dact * b * (sg * (1 + a * (1 - sg)))).astype(BF16)
            db = (dact * sa).astype(BF16)
            dab_ref[:, j * FF_COLS:(j + 1) * FF_COLS] = da
            dab_ref[:, (2 + j) * FF_COLS:(3 + j) * FF_COLS] = db
            dn = dn + _dotg(da, w_ref[j], NT) + _dotg(db, w_ref[2 + j], NT)
        dsh, dsc, dg, dhn = _norm_mod_bwd(dn, xhat, r, y, gg, scale)
        dhi_ref[...] = dho + dhn
        dm_ref[0:1, :] += dsh
        dm_ref[1:2, :] += dsc
        dm_ref[2:3, :] += dgate
        dm_ref[3:4, :] += dg

    return _grid_call(
        body, name, s // TM,
        [_rows(D), _rows(2 * D_FF), _rows(D), _rows(D), _mods_spec(layer, n_lat), _whole((1, D), (layer,)),
         _whole((N_SLOT, D, FF_COLS)), _whole((D_FF, D))],
        [_rows(D), _rows(2 * D_FF), _rows(D), _rows(D), _rows(D_FF), _acc_spec(n_lat)],
        [jax.ShapeDtypeStruct((s, D), F32), jax.ShapeDtypeStruct((s, 2 * D_FF), BF16), jax.ShapeDtypeStruct((s, D), BF16),
         jax.ShapeDtypeStruct((s, D), BF16), jax.ShapeDtypeStruct((s, D_FF), BF16), jax.ShapeDtypeStruct((2, 8, D), F32)],
        (h, ab, f, dh, mods, g, w4, wo), "arbitrary", ex)


def _token_tile(s, limit=2176):
    return max(ts for ts in range(16, limit + 1, 16) if s % ts == 0)


def wgrad(a, b, tk, tn, slot_cols, name, ex=None):
    s, k = a.shape
    n = b.shape[1]
    a_bufs, b_bufs = (1 if k == tk else 2), (1 if n == tn else 2)
    whole = 2 * s * (a_bufs * tk + b_bufs * tn) + 2 * 6 * tk * tn
    ts = s if whole <= WGRAD_VMEM_BYTES else _token_tile(s)
    steps = s // ts
    once = dict(pipeline_mode=pl.Buffered(1))

    def body(a_ref, b_ref, o_ref, o16_ref):
        r = _dotg(a_ref[...], b_ref[...], TN)
        si = pl.program_id(2)

        @pl.when(si == 0)
        def _():
            o_ref[...] = r

        @pl.when(si > 0)
        def _():
            o_ref[...] += r

        @pl.when(si == steps - 1)
        def _():
            o16_ref[...] = o_ref[...].astype(BF16)

    n_outer = tn > tk
    grid = (n // tn, k // tk, steps) if n_outer else (k // tk, n // tn, steps)
    ij = (lambda g0, g1: (g1, g0)) if n_outer else (lambda g0, g1: (g0, g1))

    def a_map(g0, g1, si):
        return si, ij(g0, g1)[0]

    def b_map(g0, g1, si):
        return si, ij(g0, g1)[1]

    if slot_cols is None:
        shape, spec = (k, n), pl.BlockSpec((tk, tn), lambda g0, g1, si: ij(g0, g1))
    else:
        per = slot_cols // tn

        def slot_map(g0, g1, si):
            i, j = ij(g0, g1)
            return lax.div(j, per), i, lax.rem(j, per)

        shape, spec = (n // slot_cols, k, slot_cols), pl.BlockSpec((None, tk, tn), slot_map)
    return _grid_call(
        body, name, grid,
        [pl.BlockSpec((ts, tk), a_map, **(once if a_bufs == 1 and steps == 1 else {})),
         pl.BlockSpec((ts, tn), b_map, **(once if b_bufs == 1 and steps == 1 else {}))], [spec, spec],
        [jax.ShapeDtypeStruct(shape, F32), jax.ShapeDtypeStruct(shape, BF16)], (a, b), ("parallel", "parallel", "arbitrary"), ex)


def proj_fwd(h, mods, g, w_in, cos, sin, layer, n_lat, name, ex=None):
    s = h.shape[0]

    def body(h_ref, m_ref, g_ref, w_ref, cos_ref, sin_ref, u_ref, q_ref, k_ref, v_ref):
        n, _, _, _ = _norm_mod(h_ref[...], g_ref[...], m_ref[3:4, :], m_ref[4:5, :])
        p = _dot(n.astype(BF16), w_ref[...])
        cs, sn = cos_ref[...], sin_ref[...]
        u_ref[...] = p[:, :POOL_W]
        q_ref[...] = (_rope(p[:, POOL_W:POOL_W + ATTN_W], cs, sn) * HEAD ** -0.5).astype(BF16)
        k_ref[...] = _rope(p[:, POOL_W + ATTN_W:POOL_W + ATTN_W + KV_W], cs, sn).astype(BF16)
        v_ref[...] = p[:, POOL_W + ATTN_W + KV_W:].astype(BF16)

    return _grid_call(
        body, name, s // TM,
        [_rows(D), _mods_spec(layer, n_lat), _whole((1, D), (layer,)), _whole((D, PROJ_W)), _rows(BLK), _rows(BLK)],
        [_rows(POOL_W), _rows(ATTN_W), _rows(KV_W), _rows(KV_W)],
        [jax.ShapeDtypeStruct((s, POOL_W), F32), jax.ShapeDtypeStruct((s, ATTN_W), BF16),
         jax.ShapeDtypeStruct((s, KV_W), BF16), jax.ShapeDtypeStruct((s, KV_W), BF16)],
        (h, mods, g, w_in, cos, sin), "parallel", ex)


def proj_bwd(h, du, dq, dk, dv, dh, mods, g, w_in, cos, sin, layer, n_lat, name):
    s = h.shape[0]

    def body(h_ref, du_ref, dq_ref, dk_ref, dv_ref, dh_ref, m_ref, g_ref, w_ref, cos_ref, sin_ref,
             dhi_ref, dp_ref, n_ref, dm_ref):
        i = pl.program_id(0)

        @pl.when((i == 0) | (i == n_lat))
        def _():
            dm_ref[...] = jnp.zeros_like(dm_ref)

        gg, scale = g_ref[...], m_ref[4:5, :]
        n, xhat, r, y = _norm_mod(h_ref[...], gg, m_ref[3:4, :], scale)
        n_ref[...] = n.astype(BF16)
        cs, sn = cos_ref[...], sin_ref[...]
        dp = jnp.concatenate([du_ref[...], _unrope(dq_ref[...], cs, sn) * HEAD ** -0.5, _unrope(dk_ref[...], cs, sn),
                              dv_ref[...]], axis=1).astype(BF16)
        dp_ref[...] = dp
        dsh, dsc, dg, dhn = _norm_mod_bwd(_dotg(dp, w_ref[...], NT), xhat, r, y, gg, scale)
        dhi_ref[...] = dh_ref[...] + dhn
        dm_ref[0:1, :] += dsh
        dm_ref[1:2, :] += dsc
        dm_ref[3:4, :] += dg

    return pl.pallas_call(
        body, name=name, grid=(s // TM,),
        in_specs=[_rows(D), _rows(POOL_W), _rows(ATTN_W), _rows(KV_W), _rows(KV_W), _rows(D), _mods_spec(layer, n_lat),
                  _whole((1, D), (layer,)), _whole((D, PROJ_W)), _rows(BLK), _rows(BLK)],
        out_specs=[_rows(D), _rows(PROJ_W), _rows(D), _acc_spec(n_lat)],
        out_shape=[jax.ShapeDtypeStruct((s, D), F32), jax.ShapeDtypeStruct((s, PROJ_W), BF16),
                   jax.ShapeDtypeStruct((s, D), BF16), jax.ShapeDtypeStruct((2, 8, D), F32)],
        compiler_params=_params("arbitrary"),
    )(h, du, dq, dk, dv, dh, mods, g, w_in, cos, sin)


def _window(i, s):
    return pl.multiple_of(jnp.clip(i * QB - BLK, 0, s - WIN), BLK)


def mix_tables(t, s):
    n_lat = t // QB
    blocks = jnp.array([0, 1, n_lat - 1] + list(range(n_lat, s // QB)))[:, None, None]
    ws = jnp.clip(blocks * QB - BLK, 0, s - WIN)
    q = blocks * QB + jnp.arange(QB)[None, :, None]
    k = ws + jnp.arange(WIN)[None, None, :]
    is_lat = blocks < n_lat
    local = jnp.where(is_lat & (k < t) & (jnp.abs(k - q) <= BLK), 0.0, NEG_INF).astype(F32)
    bias = jnp.concatenate([local, jnp.zeros(local.shape[:2] + (s - t,), F32)], axis=2)
    seq_lo, seq_hi = jnp.where(is_lat, 0, t), jnp.where(is_lat, t, s)
    bands, counts = [], []
    for w in POOL_WINDOWS:
        lo, hi = jnp.maximum(q - w // 2, seq_lo), jnp.minimum(q + w - w // 2, seq_hi)
        bands.append((k >= lo) & (k < hi))
        counts.append((hi - lo).astype(F32))
    band = jnp.stack(bands, axis=1).astype(BF16)
    count = jnp.concatenate(counts + [jnp.ones(counts[0].shape[:2] + (BLK - len(counts),), F32)], axis=2)
    return dict(bias=bias, band=band, band_t=band.transpose(0, 1, 3, 2), count=count)


def _case_spec(table, n_lat_blk):
    def kind(i):
        return jnp.where(i < n_lat_blk - 1, jnp.minimum(i, 1), i - n_lat_blk + 3)

    shape = table.shape[1:]
    return pl.BlockSpec((None,) + shape, lambda i: (kind(i),) + (0,) * len(shape))


def _split_dot(band, v):
    return _dot(band, v.astype(BF16))


def _pooled(u_ref, band_ref, cnt_ref, i, ws, gi):
    cols = slice(gi * GROUP, (gi + 1) * GROUP)
    mean = _split_dot(band_ref[gi], u_ref[pl.ds(ws, WIN), cols]) / cnt_ref[:, gi:gi + 1]
    return mean - u_ref[pl.ds(pl.multiple_of(i * QB, QB), QB), cols]


def _head_cols(hd):
    return slice(hd * HEAD, (hd + 1) * HEAD)


def _stack_heads(x, hk, first=0):
    return jnp.concatenate([x[:, first + (Q_GROUP * hk + g) * HEAD:first + (Q_GROUP * hk + g + 1) * HEAD]
                            for g in range(Q_GROUP)], axis=0)


def _biased(scores, bias):
    return (scores.reshape(Q_GROUP, QB, -1) + bias).reshape(Q_GROUP * QB, -1)


def _group_column(vals):
    row = lax.broadcasted_iota(jnp.int32, (Q_GROUP * QB, 1), 0)
    out = jnp.full((Q_GROUP * QB, 1), vals[Q_GROUP - 1], F32)
    for g in range(Q_GROUP - 2, -1, -1):
        out = jnp.where(row < (g + 1) * QB, vals[g], out)
    return out


def _lane_place(cols, width=BLK):
    lane = lax.broadcasted_iota(jnp.int32, (cols[0].shape[0], width), 1)
    out = jnp.zeros((cols[0].shape[0], width), F32)
    for hd, c in enumerate(cols):
        out = jnp.where(lane == hd, c, out)
    return out


def mix_fwd(h, q, k, v, u, w_pool, pool_scale, sink, w_out, mods, tables, layer, t, name, ex=None):
    s = h.shape[0]
    n_lat_blk = t // QB

    def body(h_ref, q_ref, k_ref, v_ref, u_ref, wp_ref, ps_ref, sink_ref, wo_ref, m_ref, bias_ref, band_ref, cnt_ref,
             ho_ref, cat_ref, lse_ref, mo_ref):
        i = pl.program_id(0)
        ws = _window(i, s)
        for gi in range(len(POOL_WINDOWS)):
            mixed = _dot(_pooled(u_ref, band_ref, cnt_ref, i, ws, gi).astype(BF16), wp_ref[gi])
            cat_ref[:, gi * GROUP:(gi + 1) * GROUP] = (mixed * ps_ref[:, gi * GROUP:(gi + 1) * GROUP]).astype(BF16)
        bias = bias_ref[...]
        k_all = jnp.concatenate([k_ref[pl.ds(ws, WIN), :], k_ref[t:s, :]], axis=0)
        v_all = jnp.concatenate([v_ref[pl.ds(ws, WIN), :], v_ref[t:s, :]], axis=0)
        lses = []
        for hk in range(N_HEADS // Q_GROUP):
            kv = _head_cols(hk)
            sc = _biased(_dotg(_stack_heads(q_ref[...], hk), k_all[:, kv], NT), bias)
            sk = _group_column([sink_ref[layer, Q_GROUP * hk + g] for g in range(Q_GROUP)])
            m = jnp.maximum(jnp.max(sc, axis=1, keepdims=True), sk)
            e = jnp.exp(sc - m)
            l = jnp.sum(e, axis=1, keepdims=True) + jnp.exp(sk - m)
            o = _dot(e.astype(BF16), v_all[:, kv]) * (1.0 / l)
            lse = m + jnp.log(l)
            for g in range(Q_GROUP):
                hd = Q_GROUP * hk + g
                cat_ref[:, POOL_W + hd * HEAD:POOL_W + (hd + 1) * HEAD] = o[g * QB:(g + 1) * QB].astype(BF16)
                lses.append(lse[g * QB:(g + 1) * QB])
        lse_ref[...] = _lane_place(lses)
        mo = _dot(cat_ref[...], wo_ref[...])
        mo_ref[...] = mo
        ho_ref[...] = h_ref[...] + m_ref[5:6, :] * mo

    blk = lambda cols: _rows(cols, QB)
    return _grid_call(
        body, name, s // QB,
        [blk(D), blk(ATTN_W), _whole((s, KV_W)), _whole((s, KV_W)), _whole((s, POOL_W)),
         _whole((len(POOL_WINDOWS), GROUP, GROUP), (layer,)), _whole((1, POOL_W), (layer,)),
         pl.BlockSpec(memory_space=pltpu.SMEM), _whole((POOL_W + ATTN_W, D)), _mods_spec(layer, n_lat_blk),
         _case_spec(tables["bias"], n_lat_blk), _case_spec(tables["band"], n_lat_blk), _case_spec(tables["count"], n_lat_blk)],
        [blk(D), blk(POOL_W + ATTN_W), blk(BLK), blk(D)],
        [jax.ShapeDtypeStruct((s, D), F32), jax.ShapeDtypeStruct((s, POOL_W + ATTN_W), BF16), jax.ShapeDtypeStruct((s, BLK), F32),
         jax.ShapeDtypeStruct((s, D), F32)],
        (h, q, k, v, u, w_pool, pool_scale, sink, w_out, mods, tables["bias"], tables["band"], tables["count"]), "parallel", ex)


def mix_bwd(dh, mo, q, k, v, u, lse, w_pool, pool_scale, sink, w_out, mods, tables, layer, t, name, ex=None):
    s = dh.shape[0]
    n_lat_blk = t // QB
    n_grp = len(POOL_WINDOWS)

    def body(dh_ref, mo_ref, q_ref, k_ref, v_ref, u_ref, lse_ref, wp_ref, ps_ref, sink_ref, wo_ref, m_ref,
             bias_ref, band_ref, band_t_ref, cnt_ref,
             dq_ref, dk_ref, dv_ref, du_ref, dmo_ref, dwp_ref, dps_ref, dsink_ref, dm_ref):
        i = pl.program_id(0)

        @pl.when(i == 0)
        def _():
            for ref in (dk_ref, dv_ref, du_ref, dwp_ref, dps_ref, dsink_ref):
                ref[...] = jnp.zeros_like(ref)

        @pl.when((i == 0) | (i == n_lat_blk))
        def _():
            dm_ref[...] = jnp.zeros_like(dm_ref)

        ws = _window(i, s)
        here = pl.ds(pl.multiple_of(i * QB, QB), QB)
        dho = dh_ref[...]
        dm_ref[2:3, :] += _sum0(dho * mo_ref[...])
        dmo = (m_ref[5:6, :] * dho).astype(BF16)
        dmo_ref[...] = dmo
        dcat = _dotg(dmo, wo_ref[...], NT)

        for gi in range(n_grp):
            cols = slice(gi * GROUP, (gi + 1) * GROUP)
            pooled = _pooled(u_ref, band_ref, cnt_ref, i, ws, gi).astype(BF16)
            dpo = dcat[:, cols]
            dps_ref[0:1, cols] += _sum0(dpo * _dot(pooled, wp_ref[gi]))
            dmixed = (dpo * ps_ref[:, cols]).astype(BF16)
            dwp_ref[gi] += _dotg(pooled, dmixed, TN)
            dpooled = _dotg(dmixed, wp_ref[gi], NT)
            du_ref[pl.ds(ws, WIN), cols] += _split_dot(band_t_ref[gi], dpooled / cnt_ref[:, gi:gi + 1])
            du_ref[here, cols] -= dpooled

        bias = bias_ref[...]
        k_all = jnp.concatenate([k_ref[pl.ds(ws, WIN), :], k_ref[t:s, :]], axis=0)
        v_all = jnp.concatenate([v_ref[pl.ds(ws, WIN), :], v_ref[t:s, :]], axis=0)
        qq, lse_all = q_ref[...], lse_ref[...]
        dqs, dsinks, dks, dvs = [], [], [], []
        for hk in range(N_HEADS // Q_GROUP):
            kv = _head_cols(hk)
            q4 = _stack_heads(qq, hk)
            lse = jnp.concatenate([lse_all[:, Q_GROUP * hk + g:Q_GROUP * hk + g + 1] for g in range(Q_GROUP)], axis=0)
            p = jnp.exp(_biased(_dotg(q4, k_all[:, kv], NT), bias) - lse)
            do = _stack_heads(dcat, hk, POOL_W).astype(BF16)
            dp = _dotg(do, v_all[:, kv], NT)
            delta = jnp.sum(p * dp, axis=1, keepdims=True)
            ds = (p * (dp - delta)).astype(BF16)
            sk = _group_column([sink_ref[layer, Q_GROUP * hk + g] for g in range(Q_GROUP)])
            dsk = -jnp.exp(sk - lse) * delta
            dq = _dot(ds, k_all[:, kv])
            for g in range(Q_GROUP):
                dqs.append(dq[g * QB:(g + 1) * QB])
                dsinks.append(_sum0(dsk[g * QB:(g + 1) * QB]))
            dks.append(_dotg(ds, q4, TN))
            dvs.append(_dotg(p.astype(BF16), do, TN))
        dq_ref[...] = jnp.concatenate(dqs, axis=1)
        dk, dv = jnp.concatenate(dks, axis=1), jnp.concatenate(dvs, axis=1)
        dk_ref[pl.ds(ws, WIN), :] += dk[:WIN]
        dv_ref[pl.ds(ws, WIN), :] += dv[:WIN]
        dk_ref[t:s, :] += dk[WIN:]
        dv_ref[t:s, :] += dv[WIN:]
        dsink_ref[0:1, :] += _lane_place(dsinks)

    blk = lambda cols: _rows(cols, QB)
    full = lambda shape: pl.BlockSpec(shape, lambda i: (0,) * len(shape))
    return _grid_call(
        body, name, s // QB,
        [blk(D), blk(D), blk(ATTN_W), _whole((s, KV_W)), _whole((s, KV_W)), _whole((s, POOL_W)),
         blk(BLK), _whole((n_grp, GROUP, GROUP), (layer,)), _whole((1, POOL_W), (layer,)),
         pl.BlockSpec(memory_space=pltpu.SMEM), _whole((POOL_W + ATTN_W, D)), _mods_spec(layer, n_lat_blk)]
        + [_case_spec(tables[key], n_lat_blk) for key in ("bias", "band", "band_t", "count")],
        [blk(ATTN_W), full((s, KV_W)), full((s, KV_W)), full((s, POOL_W)), blk(D),
         full((n_grp, GROUP, GROUP)), full((8, POOL_W)), full((8, BLK)), _acc_spec(n_lat_blk)],
        [jax.ShapeDtypeStruct((s, ATTN_W), F32), jax.ShapeDtypeStruct((s, KV_W), F32),
         jax.ShapeDtypeStruct((s, KV_W), F32), jax.ShapeDtypeStruct((s, POOL_W), F32),
         jax.ShapeDtypeStruct((s, D), BF16), jax.ShapeDtypeStruct((n_grp, GROUP, GROUP), F32),
         jax.ShapeDtypeStruct((8, POOL_W), F32), jax.ShapeDtypeStruct((8, BLK), F32), jax.ShapeDtypeStruct((2, 8, D), F32)],
        (dh, mo, q, k, v, u, lse, w_pool, pool_scale, sink, w_out, mods, tables["bias"], tables["band"], tables["band_t"],
         tables["count"]), "arbitrary", ex)


def loss_head(h, target, g, t, name):
    s = h.shape[0]
    n_lat = t // TM

    def body(h_ref, t_ref, g_ref, dh_ref, acc_ref):
        i = pl.program_id(0)

        @pl.when(i == 0)
        def _():
            acc_ref[...] = jnp.zeros_like(acc_ref)

        @pl.when(i < n_lat)
        def _():
            hh, gg = h_ref[...], g_ref[...]
            r = lax.rsqrt(jnp.mean(hh * hh, axis=-1, keepdims=True) + EPS)
            xhat = hh * r
            err = xhat * gg - t_ref[...]
            dy = err * (1.0 / D)
            dx = dy * gg
            dh_ref[...] = r * (dx - xhat * jnp.mean(dx * xhat, axis=-1, keepdims=True))
            acc_ref[0:1, :] += _sum0(dy * xhat)
            acc_ref[1:2, :] += _sum0(err * err)

        @pl.when(i >= n_lat)
        def _():
            dh_ref[...] = jnp.zeros_like(dh_ref)

    return pl.pallas_call(
        body, name=name, grid=(s // TM,),
        in_specs=[_rows(D), pl.BlockSpec((TM, D), lambda i: (jnp.minimum(i, n_lat - 1), 0)), _whole((1, D))],
        out_specs=[_rows(D), pl.BlockSpec((8, D), lambda i: (0, 0))],
        out_shape=[jax.ShapeDtypeStruct((s, D), F32), jax.ShapeDtypeStruct((8, D), F32)],
        compiler_params=_params("arbitrary"),
    )(h, target, g)


def mod_rows(c_all, w_mod, b_cols, name):
    def body(c_ref, w_ref, b_ref, o_ref):
        cc = c_ref[...]
        o_ref[...] = _dot((cc * jax.nn.sigmoid(cc)).astype(BF16), w_ref[...].astype(BF16)) + b_ref[...]

    return pl.pallas_call(
        body, name=name, grid=(2,),
        in_specs=[pl.BlockSpec((16, D), lambda l: (0, 0)), pl.BlockSpec((None, D, MOD_COLS), lambda l: (l, 0, 0)),
                  pl.BlockSpec((None, 1, MOD_COLS), lambda l: (l, 0, 0))],
        out_specs=pl.BlockSpec((None, 16, MOD_COLS), lambda l: (l, 0, 0)),
        out_shape=jax.ShapeDtypeStruct((2, 16, MOD_COLS), F32),
        compiler_params=_params("parallel"),
    )(c_all, w_mod, b_cols)


def mod_grads(c_all, dmod_cols, w_mod, name):
    def body(c_ref, d_ref, w_ref, dw_ref, dc_ref):
        @pl.when(pl.program_id(0) == 0)
        def _():
            dc_ref[...] = jnp.zeros_like(dc_ref)

        cc = c_ref[...]
        dd = d_ref[...].astype(BF16)
        dw_ref[...] = _dotg((cc * jax.nn.sigmoid(cc)).astype(BF16), dd, TN)
        dc_ref[...] += _dotg(dd, w_ref[...].astype(BF16), NT)

    return pl.pallas_call(
        body, name=name, grid=(2,),
        in_specs=[pl.BlockSpec((16, D), lambda l: (0, 0)), pl.BlockSpec((None, 16, MOD_COLS), lambda l: (l, 0, 0)),
                  pl.BlockSpec((None, D, MOD_COLS), lambda l: (l, 0, 0))],
        out_specs=[pl.BlockSpec((None, D, MOD_COLS), lambda l: (l, 0, 0)), pl.BlockSpec((16, D), lambda l: (0, 0))],
        out_shape=[jax.ShapeDtypeStruct((2, D, MOD_COLS), F32), jax.ShapeDtypeStruct((16, D), F32)],
        compiler_params=_params("arbitrary"),
    )(c_all, dmod_cols, w_mod)


def _row_tile(rows, cols, n_arrays):
    budget = VMEM_LIMIT_BYTES // 4 // (2 * 4 * n_arrays * cols)
    best = None
    for tr in range(16, rows + 1, 16):
        if rows % tr == 0 and tr <= budget:
            best = tr
    return best if best is not None else rows


def elementwise(fn, ins, out_dtypes, name, ex=None):
    rows, cols = ins[0].shape
    tr = _row_tile(rows, cols, len(ins) + len(out_dtypes))

    def body(*refs):
        outs = fn(*[r[...] for r in refs[:len(ins)]])
        for o_ref, o in zip(refs[len(ins):], outs):
            o_ref[...] = o.astype(o_ref.dtype)

    spec = pl.BlockSpec((tr, cols), lambda i: (i, 0))
    outs, got = _grid_call(body, name, rows // tr, [spec] * len(ins), [spec] * len(out_dtypes),
                           [jax.ShapeDtypeStruct((rows, cols), dt) for dt in out_dtypes], ins, "parallel", ex)
    return outs if ex is None else (outs, got)


def _adamw_tile(w, g, m, v):
    m = ADAM_B1 * m + (1.0 - ADAM_B1) * g
    v = ADAM_B2 * v + (1.0 - ADAM_B2) * (g * g)
    m_hat = m / (1.0 - ADAM_B1 ** ADAM_STEP)
    v_hat = v / (1.0 - ADAM_B2 ** ADAM_STEP)
    return -ADAM_LR * (m_hat / (jnp.sqrt(v_hat) + ADAM_EPS) + ADAM_WD * w), m, v


def adamw(w, g, m, v, name, ex=None):
    shape = w.shape
    two_d = (-1, shape[-1]) if w.ndim > 1 else (1, -1)
    outs = elementwise(_adamw_tile, [a.reshape(two_d) for a in (w, g, m, v)], [F32] * 3, name, ex)
    outs, got = outs if ex is not None else (outs, None)
    outs = [o.reshape(shape) for o in outs]
    return outs if ex is None else (outs, got)


def _prefetch_call(body, name, grid, in_specs, out_specs, out_shape, place, args, ex=None):
    if ex is None:
        spec = pltpu.PrefetchScalarGridSpec(num_scalar_prefetch=1, grid=grid, in_specs=in_specs, out_specs=out_specs)
        return pl.pallas_call(body, name=name, grid_spec=spec, out_shape=out_shape,
                              compiler_params=_params(*["parallel"] * len(grid)))(place, *args)
    n_in, n_out, ci, co = len(in_specs), len(out_specs), len(ex["ins"]), len(ex["out_shape"])
    spec = pltpu.PrefetchScalarGridSpec(num_scalar_prefetch=1, grid=grid, in_specs=list(in_specs) + _any(ci),
                                        out_specs=list(out_specs) + _any(co), scratch_shapes=ex["scratch"])
    outs = pl.pallas_call(
        _carrying(body, grid, n_in, n_out, ex, lead=1), name=name, grid_spec=spec, out_shape=list(out_shape) + ex["out_shape"],
        input_output_aliases={1 + n_in + i: n_out + j for i, j in ex["aliases"].items()},
        compiler_params=_params(*["arbitrary"] * len(grid)))(place, *args, *ex["ins"])
    return outs[:n_out], outs[n_out:]


def cast_place(w, layer, place, name):
    _, r, c = w.shape
    tr = _row_tile(r, c, 2)

    def body(p_ref, w_ref, o_ref):
        o_ref[...] = w_ref[...].astype(BF16)

    return _prefetch_call(
        body, name, (r // tr,), [pl.BlockSpec((None, tr, c), lambda i, p: (layer, i, 0))],
        pl.BlockSpec((None, tr, c), lambda i, p: (p[1], i, 0)), jax.ShapeDtypeStruct((N_SLOT, r, c), BF16), place, [w])


def pair_sum(g32, got, place, name, ex=None):
    n_slot, rh, c = got.shape
    tr = _row_tile(rh, c, 4)
    per = rh // tr

    def body(p_ref, a_ref, b_ref, o_ref, o16_ref):
        r = a_ref[...] + b_ref[...].astype(F32)
        o_ref[...] = r
        o16_ref[...] = r.astype(BF16)

    half = pl.BlockSpec((None, tr, c), lambda s, i, p: (s, i, 0))
    return _prefetch_call(
        body, name, (n_slot, per), [pl.BlockSpec((None, tr, c), lambda s, i, p: (s, p[0] * per + i, 0)), half], [half, half],
        [jax.ShapeDtypeStruct(got.shape, F32), jax.ShapeDtypeStruct(got.shape, BF16)], place, [g32, got], ex)


def chip_sum(p32, got, place, name):
    _, rh, c = p32.shape
    tr = _row_tile(rh, c, 5)
    per = rh // tr

    def body(p_ref, m_ref, r0_ref, r1_ref, r2_ref, o_ref):
        o_ref[...] = m_ref[...] + r0_ref[...].astype(F32) + r1_ref[...].astype(F32) + r2_ref[...].astype(F32)

    part = pl.BlockSpec((tr, c), lambda i, p: (i, 0))
    return _prefetch_call(
        body, name, (per,), [pl.BlockSpec((None, tr, c), lambda i, p: (p[1], i, 0)), part, part, part],
        pl.BlockSpec((tr, c), lambda i, p: (p[0] * per + i, 0)), jax.ShapeDtypeStruct((2 * rh, c), F32), place, [p32, *got])


def adamw_layers(w, g0, g1, m, v, name, ex=None):
    _, r, c = w.shape
    tr = _row_tile(r, c, 10)

    def body(w_ref, g0_ref, g1_ref, m_ref, v_ref, g_ref, d_ref, mo_ref, vo_ref):
        g = jnp.where(pl.program_id(0) == 0, g0_ref[...], g1_ref[...])
        g_ref[...] = g
        d_ref[...], mo_ref[...], vo_ref[...] = _adamw_tile(w_ref[...], g, m_ref[...], v_ref[...])

    steps = r // tr
    stacked = pl.BlockSpec((None, tr, c), lambda l, i: (l, i, 0))
    layer0 = pl.BlockSpec((tr, c), lambda l, i: (jnp.where(l == 0, i, steps - 1), 0))
    layer1 = pl.BlockSpec((tr, c), lambda l, i: (jnp.where(l == 0, 0, i), 0))
    outs, got = _grid_call(body, name, (2, steps), [stacked, layer0, layer1, stacked, stacked], [stacked] * 4,
                           [jax.ShapeDtypeStruct(w.shape, F32)] * 4, (w, g0, g1, m, v), "parallel", ex)
    return outs if ex is None else (outs, got)


def sum8(gathered, name):
    def body(*refs):
        n = len(refs) // 2
        for g_ref, o_ref in zip(refs[:n], refs[n:]):
            acc = g_ref[0]
            for dev in range(1, N_DEV):
                acc = acc + g_ref[dev]
            o_ref[...] = acc

    return pl.pallas_call(
        body, name=name,
        out_shape=[jax.ShapeDtypeStruct(a.shape[1:], F32) for a in gathered],
        compiler_params=_params(),
    )(*gathered)


PHASES = ("start", "late", "finish")


def _place():
    return lax.axis_index("x"), lax.axis_index("y"), lax.axis_index("c")


def _any(n):
    return [pl.BlockSpec(memory_space=pl.ANY)] * n


def gather8_exchange(blocks):
    n = len(blocks)

    def copy(outs, sems, ti, k, block, to, src=None):
        dst = outs[ti].at[4 * block[0] + 2 * block[1] + block[2]]
        return pltpu.make_async_remote_copy(src_ref=dst if src is None else src, dst_ref=dst, send_sem=sems[0].at[ti, k],
                                            recv_sem=sems[1].at[ti, k], device_id=to, device_id_type=MESH)

    def first(ins, outs, sems):
        x, y, c = _place()
        local, sent = [], []
        for ti in range(n):
            local.append(pltpu.make_async_copy(ins[ti], outs[ti].at[4 * x + 2 * y + c], sems[2].at[ti]))
            sent.append(copy(outs, sems, ti, 0, (x, y, c), (x, y, 1 - c), src=ins[ti]))
            sent += [copy(outs, sems, ti, 1 + j, (x, y, c), (*chip, c), src=ins[ti]) for j, chip in enumerate(_three_chips(x, y))]
        return local, sent

    def start(ins, outs, sems):
        local, sent = first(ins, outs, sems)
        for cp in local + sent:
            cp.start()

    def passed_on(outs, sems):
        x, y, c = _place()
        return [copy(outs, sems, ti, 4 + j, (*chip, c), (x, y, 1 - c)) for ti in range(n) for j, chip in enumerate(_three_chips(x, y))]

    def late(ins, outs, sems):
        x, y, c = _place()
        on = passed_on(outs, sems)
        for ti in range(n):
            for j, chip in enumerate(_three_chips(x, y)):
                copy(outs, sems, ti, 1 + j, (*chip, c), (x, y, c)).wait_recv()
                on[3 * ti + j].start()

    def finish(ins, outs, sems):
        x, y, c = _place()
        me, sibling = (x, y, c), (x, y, 1 - c)
        local, sent = first(ins, outs, sems)
        for ti in range(n):
            copy(outs, sems, ti, 0, sibling, me).wait_recv()
            for j, chip in enumerate(_three_chips(x, y)):
                copy(outs, sems, ti, 4 + j, (*chip, 1 - c), me).wait_recv()
        for cp in sent + passed_on(outs, sems):
            cp.wait_send()
        for cp in local:
            cp.wait()

    return dict(ins=list(blocks), out_shape=[jax.ShapeDtypeStruct((N_DEV,) + b.shape, b.dtype) for b in blocks], aliases={},
                start=start, late=late, finish=finish,
                scratch=[pltpu.SemaphoreType.DMA((n, 7)), pltpu.SemaphoreType.DMA((n, 7)), pltpu.SemaphoreType.DMA((n,))])


def all_gather(blocks, name):
    return run_exchange(gather8_exchange(blocks), name)


def _three_chips(x, y):
    return [(1 - x, y), (x, 1 - y), (1 - x, 1 - y)]


def gather_exchange(placed):
    n = len(placed)

    def copy(bufs, sems, ti, k, chip, core, to):
        rh = bufs[ti].shape[1] // 2
        half = bufs[ti].at[2 * chip[0] + chip[1], pl.ds(core * rh, rh), :]
        return pltpu.make_async_remote_copy(src_ref=half, dst_ref=half, send_sem=sems[0].at[ti, k], recv_sem=sems[1].at[ti, k],
                                            device_id=to, device_id_type=MESH)

    def sends(bufs, sems):
        x, y, c = _place()
        return [copy(bufs, sems, ti, k, (x, y), c, (*chip, c)) for ti in range(n) for k, chip in enumerate(_three_chips(x, y))]

    def passed_on(bufs, sems):
        x, y, c = _place()
        return [copy(bufs, sems, ti, 3 + k, chip, c, (x, y, 1 - c)) for ti in range(n) for k, chip in enumerate(_three_chips(x, y))]

    def start(ins, bufs, sems):
        for cp in sends(bufs, sems):
            cp.start()

    def late(ins, bufs, sems):
        x, y, c = _place()
        on = passed_on(bufs, sems)
        for ti in range(n):
            for k, chip in enumerate(_three_chips(x, y)):
                copy(bufs, sems, ti, k, chip, c, (x, y, c)).wait_recv()
                on[3 * ti + k].start()

    def finish(ins, bufs, sems):
        x, y, c = _place()
        for ti in range(n):
            for k, chip in enumerate(_three_chips(x, y)):
                copy(bufs, sems, ti, 3 + k, chip, 1 - c, (x, y, c)).wait_recv()
        for cp in sends(bufs, sems) + passed_on(bufs, sems):
            cp.wait_send()

    return dict(ins=list(placed), out_shape=[jax.ShapeDtypeStruct(w.shape, w.dtype) for w in placed],
                aliases={i: i for i in range(n)}, start=start, late=late, finish=finish,
                scratch=[pltpu.SemaphoreType.DMA((n, 6)), pltpu.SemaphoreType.DMA((n, 6))])


def scatter_exchange(p16):
    n = len(p16)

    def copies(ins, got, sems):
        x, y, c = _place()
        return [pltpu.make_async_remote_copy(src_ref=ins[ti].at[2 * chip[0] + chip[1]], dst_ref=got[3 * ti + k],
                                             send_sem=sems[0].at[ti, k], recv_sem=sems[1].at[ti, k], device_id=(*chip, c),
                                             device_id_type=MESH)
                for ti in range(n) for k, chip in enumerate(_three_chips(x, y))]

    def start(ins, got, sems):
        for cp in copies(ins, got, sems):
            cp.start()

    def finish(ins, got, sems):
        for cp in copies(ins, got, sems):
            cp.wait()

    return dict(ins=list(p16), out_shape=[jax.ShapeDtypeStruct(a.shape[1:], BF16) for a in p16 for _ in range(3)], aliases={},
                start=start, finish=finish, scratch=[pltpu.SemaphoreType.DMA((n, 3)), pltpu.SemaphoreType.DMA((n, 3))])


def run_exchange(ex, name):
    ci, co = len(ex["ins"]), len(ex["out_shape"])

    def body(*refs):
        ins, outs, sems = refs[:ci], refs[ci:ci + co], refs[ci + co:]
        for phase in PHASES:
            if phase in ex:
                ex[phase](ins, outs, sems)

    return pl.pallas_call(body, name=name, in_specs=_any(ci), out_specs=_any(co), out_shape=ex["out_shape"],
                          input_output_aliases=ex["aliases"], scratch_shapes=ex["scratch"])(*ex["ins"])


def _carrying(body, grid, n_in, n_out, ex, lead=0):
    ci, co = len(ex["ins"]), len(ex["out_shape"])
    first, last = (0,) * len(grid), tuple(g - 1 for g in grid)
    steps = dict(start=first, late=(grid[0] - 2,) if len(grid) == 1 and grid[0] > 2 else last, finish=last)

    def at(ids):
        return functools.reduce(jnp.logical_and, [pl.program_id(ax) == v for ax, v in enumerate(ids)])

    def carrying(*refs):
        head, refs = refs[:lead], refs[lead:]
        c_in, c_out = refs[n_in:n_in + ci], refs[n_in + ci + n_out:n_in + ci + n_out + co]
        sems = refs[n_in + ci + n_out + co:]
        for phase in PHASES:
            if phase == "finish":
                body(*head, *refs[:n_in], *refs[n_in + ci:n_in + ci + n_out])
            if phase in ex:
                pl.when(at(steps[phase]))(functools.partial(ex[phase], c_in, c_out, sems))

    return carrying


def _grid_call(body, name, grid, in_specs, out_specs, out_shape, args, sem, ex=None):
    grid = (grid,) if isinstance(grid, int) else tuple(grid)
    sems_of = (sem,) * len(grid) if isinstance(sem, str) else tuple(sem)
    n_in, n_out = len(in_specs), len(out_specs)
    if ex is None:
        return pl.pallas_call(body, name=name, grid=grid, in_specs=in_specs, out_specs=out_specs, out_shape=out_shape,
                              compiler_params=_params(*sems_of))(*args), []
    ci, co = len(ex["ins"]), len(ex["out_shape"])
    outs = pl.pallas_call(
        _carrying(body, grid, n_in, n_out, ex), name=name, grid=grid, in_specs=list(in_specs) + _any(ci),
        out_specs=list(out_specs) + _any(co), out_shape=list(out_shape) + ex["out_shape"], scratch_shapes=ex["scratch"],
        input_output_aliases={n_in + i: n_out + j for i, j in ex["aliases"].items()},
        compiler_params=_params(*["arbitrary"] * len(grid)),
    )(*args, *ex["ins"])
    return outs[:n_out], outs[n_out:]


def both(*exchanges):
    exchanges = [ex for ex in exchanges if ex is not None]
    if len(exchanges) < 2:
        return exchanges[0] if exchanges else None
    n_ins = [len(ex["ins"]) for ex in exchanges]
    n_outs = [len(ex["out_shape"]) for ex in exchanges]
    n_sems = [len(ex["scratch"]) for ex in exchanges]

    def parts(seq, counts, k):
        first = sum(counts[:k])
        return seq[first:first + counts[k]]

    def run(phase):
        def go(ins, outs, sems):
            for k, ex in enumerate(exchanges):
                if phase in ex:
                    ex[phase](parts(ins, n_ins, k), parts(outs, n_outs, k), parts(sems, n_sems, k))
        return go

    aliases = {sum(n_ins[:k]) + i: sum(n_outs[:k]) + j for k, ex in enumerate(exchanges) for i, j in ex["aliases"].items()}
    return dict(ins=[a for ex in exchanges for a in ex["ins"]], out_shape=[o for ex in exchanges for o in ex["out_shape"]],
                aliases=aliases, scratch=[s for ex in exchanges for s in ex["scratch"]], **{ph: run(ph) for ph in PHASES})


def split_outputs(got, *exchanges):
    got, out = list(got), []
    for ex in exchanges:
        n = len(ex["out_shape"]) if ex is not None else 0
        out.append(got[:n])
        got = got[n:]
    return out


def pair_exchange(g16):
    n = len(g16)

    def copies(a16, got, sems):
        x, y, c = _place()
        out = []
        for ti in range(n):
            rh = a16[ti].shape[1] // 2
            out.append(pltpu.make_async_remote_copy(
                src_ref=a16[ti].at[:, pl.ds((1 - c) * rh, rh), :], dst_ref=got[ti], send_sem=sems[0].at[ti],
                recv_sem=sems[1].at[ti], device_id=(x, y, 1 - c), device_id_type=MESH))
        return out

    def start(a16, got, sems):
        for cp in copies(a16, got, sems):
            cp.start()

    def finish(a16, got, sems):
        for cp in copies(a16, got, sems):
            cp.wait()

    return dict(ins=list(g16), out_shape=[jax.ShapeDtypeStruct((a.shape[0], a.shape[1] // 2, a.shape[2]), BF16) for a in g16],
                aliases={}, start=start, finish=finish, scratch=[pltpu.SemaphoreType.DMA((n,)), pltpu.SemaphoreType.DMA((n,))])


def _gather_half(buf, chip, core):
    rh = buf.shape[1] // 2
    return buf.at[2 * chip[0] + chip[1], pl.ds(core * rh, rh), :]


def gather_start(placed, name):
    n = len(placed)
    hbm, sem = pl.BlockSpec(memory_space=pltpu.HBM), pl.BlockSpec(memory_space=pltpu.SEMAPHORE)

    def body(*refs):
        bufs, send_sems, recv_sems, token_ref = refs[:n], refs[n], refs[n + 1], refs[-1]
        x, y, c = _place()
        for ti in range(n):
            for k, chip in enumerate(_three_chips(x, y)):
                half = _gather_half(bufs[ti], (x, y), c)
                pltpu.make_async_remote_copy(src_ref=half, dst_ref=half, send_sem=send_sems.at[3 * ti + k],
                                             recv_sem=recv_sems.at[3 * ti + k], device_id=(*chip, c), device_id_type=MESH).start()
        token_ref[...] = jnp.zeros_like(token_ref)

    return pl.pallas_call(
        body, name=name,
        out_shape=(pltpu.SemaphoreType.DMA((3 * n,)), pltpu.SemaphoreType.DMA((3 * n,)), *[pltpu.HBM(w.shape, w.dtype) for w in placed],
                   jax.ShapeDtypeStruct((8, BLK), F32)),
        in_specs=(hbm,) * n, out_specs=(sem, sem, *(hbm,) * n, pl.BlockSpec(memory_space=pltpu.VMEM)),
        input_output_aliases={i: 2 + i for i in range(n)},
        compiler_params=pltpu.CompilerParams(has_side_effects=pltpu.SideEffectType.DATAFLOW_SIDE_EFFECTING),
    )(*[pltpu.with_memory_space_constraint(w, pltpu.HBM) for w in placed])


def gather_wait(send_sems, recv_sems, bufs, after, name):
    n = len(bufs)
    hbm, sem = pl.BlockSpec(memory_space=pltpu.HBM), pl.BlockSpec(memory_space=pltpu.SEMAPHORE)

    def body(*refs):
        bufs, send_sems, recv_sems = refs[:n], refs[n], refs[n + 1]
        x, y, c = _place()
        for ti in range(n):
            for k, chip in enumerate(_three_chips(x, y)):
                mine, theirs = _gather_half(bufs[ti], (x, y), c), _gather_half(bufs[ti], chip, c)
                cp = pltpu.make_async_remote_copy(src_ref=mine, dst_ref=theirs, send_sem=send_sems.at[3 * ti + k],
                                                  recv_sem=recv_sems.at[3 * ti + k], device_id=(*chip, c), device_id_type=MESH)
                cp.wait_send()
                cp.wait_recv()

    return pl.pallas_call(
        body, name=name, out_shape=tuple(pltpu.HBM(w.shape, w.dtype) for w in bufs),
        in_specs=(*(hbm,) * n, sem, sem, *_any(len(after))), out_specs=(hbm,) * n,
        input_output_aliases={i: i for i in range(n)},
        compiler_params=pltpu.CompilerParams(has_side_effects=pltpu.SideEffectType.DATAFLOW_SIDE_EFFECTING),
    )(*bufs, send_sems, recv_sems, *after)


def pass_on_exchange(bufs):
    n = len(bufs)

    def copies(refs, sems, core):
        x, y, c = _place()
        return [pltpu.make_async_remote_copy(src_ref=_gather_half(refs[ti], chip, c if core == "mine" else 1 - c),
                                             dst_ref=_gather_half(refs[ti], chip, c if core == "mine" else 1 - c),
                                             send_sem=sems[0].at[ti, k], recv_sem=sems[1].at[ti, k], device_id=(x, y, 1 - c),
                                             device_id_type=MESH)
                for ti in range(n) for k, chip in enumerate(_three_chips(x, y))]

    def start(ins, refs, sems):
        for cp in copies(refs, sems, "mine"):
            cp.start()

    def finish(ins, refs, sems):
        for cp in copies(refs, sems, "mine"):
            cp.wait_send()
        for cp in copies(refs, sems, "sibling's"):
            cp.wait_recv()

    return dict(ins=list(bufs), out_shape=[jax.ShapeDtypeStruct(w.shape, w.dtype) for w in bufs], aliases={i: i for i in range(n)},
                start=start, finish=finish, scratch=[pltpu.SemaphoreType.DMA((n, 3)), pltpu.SemaphoreType.DMA((n, 3))])


def _scatter_copies(src_ref, lands, send_sems, recv_sems):
    x, y, c = _place()
    return [pltpu.make_async_remote_copy(src_ref=src_ref.at[2 * chip[0] + chip[1]], dst_ref=lands[k], send_sem=send_sems.at[k],
                                         recv_sem=recv_sems.at[k], device_id=(*chip, c), device_id_type=MESH)
            for k, chip in enumerate(_three_chips(x, y))]


def scatter_start(p16, name):
    hbm, sem = pl.BlockSpec(memory_space=pltpu.HBM), pl.BlockSpec(memory_space=pltpu.SEMAPHORE)

    def body(src_ref, l0_ref, l1_ref, l2_ref, send_sems, recv_sems, src_thru, o0_ref, o1_ref, o2_ref, token_ref):
        for cp in _scatter_copies(src_ref, (l0_ref, l1_ref, l2_ref), send_sems, recv_sems):
            cp.start()
        token_ref[...] = jnp.zeros_like(token_ref)

    land = [pltpu.with_memory_space_constraint(lax.empty(p16.shape[1:], BF16), pltpu.HBM) for _ in range(3)]
    return pl.pallas_call(
        body, name=name,
        out_shape=(pltpu.SemaphoreType.DMA((3,)), pltpu.SemaphoreType.DMA((3,)), pltpu.HBM(p16.shape, BF16),
                   *[pltpu.HBM(p16.shape[1:], BF16)] * 3, jax.ShapeDtypeStruct((8, BLK), F32)),
        in_specs=(hbm,) * 4, out_specs=(sem, sem, hbm, hbm, hbm, hbm, pl.BlockSpec(memory_space=pltpu.VMEM)),
        input_output_aliases={0: 2, 1: 3, 2: 4, 3: 5},
        compiler_params=pltpu.CompilerParams(has_side_effects=pltpu.SideEffectType.DATAFLOW_SIDE_EFFECTING),
    )(pltpu.with_memory_space_constraint(p16, pltpu.HBM), *land)


def scatter_wait(send_sems, recv_sems, src_thru, lands, after, name):
    hbm, sem = pl.BlockSpec(memory_space=pltpu.HBM), pl.BlockSpec(memory_space=pltpu.SEMAPHORE)

    def body(src_ref, l0_ref, l1_ref, l2_ref, send_sems, recv_sems, *rest):
        for cp in _scatter_copies(src_ref, (l0_ref, l1_ref, l2_ref), send_sems, recv_sems):
            cp.wait_send()
            cp.wait_recv()

    return pl.pallas_call(
        body, name=name, out_shape=(pltpu.HBM(src_thru.shape, BF16), *[pltpu.HBM(lands[0].shape, BF16)] * 3),
        in_specs=(hbm, hbm, hbm, hbm, sem, sem, *_any(len(after))), out_specs=(hbm,) * 4,
        input_output_aliases={0: 0, 1: 1, 2: 2, 3: 3},
        compiler_params=pltpu.CompilerParams(has_side_effects=pltpu.SideEffectType.DATAFLOW_SIDE_EFFECTING),
    )(src_thru, *lands, send_sems, recv_sems, *after)[1:]


def pair_fill_exchange(halves):
    n = len(halves)

    def copies(bufs, sems, core):
        x, y, c = _place()
        out = []
        for ti in range(n):
            rh = bufs[ti].shape[0] // 2
            rows = bufs[ti].at[pl.ds((c if core == "mine" else 1 - c) * rh, rh), :]
            out.append(pltpu.make_async_remote_copy(src_ref=rows, dst_ref=rows, send_sem=sems[0].at[ti], recv_sem=sems[1].at[ti],
                                                    device_id=(x, y, 1 - c), device_id_type=MESH))
        return out

    def start(ins, bufs, sems):
        for cp in copies(bufs, sems, "mine"):
            cp.start()

    def finish(ins, bufs, sems):
        for cp in copies(bufs, sems, "mine"):
            cp.wait_send()
        for cp in copies(bufs, sems, "sibling's"):
            cp.wait_recv()

    return dict(ins=list(halves), out_shape=[jax.ShapeDtypeStruct(a.shape, a.dtype) for a in halves],
                aliases={i: i for i in range(n)}, start=start, finish=finish,
                scratch=[pltpu.SemaphoreType.DMA((n,)), pltpu.SemaphoreType.DMA((n,))])


def pair_gather(halves, name):
    return run_exchange(pair_fill_exchange(halves), name)


def reduce_small(dm_f1, dm_mix, dm_gate, dm_f2, loss_blk, name):
    def body(f1_ref, mix_ref, gate_ref, f2_ref, l_ref, tot_ref, rows_ref, fin_ref):
        rows_ref[...] = jnp.zeros_like(rows_ref)
        tot_ref[...] = jnp.zeros_like(tot_ref)
        mod_src = [(f1_ref, 0), (f1_ref, 1), (f1_ref, 2), (mix_ref, 0), (mix_ref, 1), (gate_ref, 2),
                   (f2_ref, 0), (f2_ref, 1), (f2_ref, 2)]
        norm_src = [(f1_ref, 3), (mix_ref, 3), (f2_ref, 3)]
        for l in range(2):
            for k, (ref, r) in enumerate(mod_src + norm_src):
                lat = ref[0, l, 0, r:r + 1, :]
                ctx = ref[0, l, 1, r:r + 1, :]
                for dev in range(N_DEV):
                    if dev:
                        lat = lat + ref[dev, l, 0, r:r + 1, :]
                        ctx = ctx + ref[dev, l, 1, r:r + 1, :]
                    if k < N_MOD:
                        rows_ref[l, dev, k:k + 1, :] = ref[dev, l, 0, r:r + 1, :]
                if k < N_MOD:
                    rows_ref[l, N_DEV, k:k + 1, :] = ctx
                tot_ref[l, k:k + 1, :] = lat + ctx
        acc = l_ref[0]
        for dev in range(1, N_DEV):
            acc = acc + l_ref[dev]
        loss = (0.5 / D) * jnp.sum(acc[1:2, :], axis=1, keepdims=True)
        row = lax.broadcasted_iota(jnp.int32, (8, D), 0)
        fin_ref[...] = jnp.where(row == 0, acc[0:1, :], loss)

    return pl.pallas_call(
        body, name=name,
        out_shape=[jax.ShapeDtypeStruct((2, 16, D), F32), jax.ShapeDtypeStruct((2, 16, 16, D), F32),
                   jax.ShapeDtypeStruct((8, D), F32)],
        compiler_params=_params(),
    )(dm_f1, dm_mix, dm_gate, dm_f2, loss_blk)


def rope_tables(t, s):
    rows = t // GRID_W
    row = jnp.repeat(jnp.arange(rows), GRID_W).astype(F32)
    col = jnp.tile(jnp.arange(GRID_W), rows).astype(F32)
    inv = ROPE_BASE ** (-jnp.arange(0, HEAD // 2, 2, dtype=F32) / (HEAD // 2))
    ang = jnp.concatenate([row[:, None] * inv, col[:, None] * inv], axis=-1)
    cos, sin = jnp.cos(ang), jnp.sin(ang)
    cos = jnp.concatenate([jnp.tile(cos, (1, 4)), jnp.ones((s - t, BLK), F32)], axis=0)
    sin = jnp.concatenate([jnp.tile(jnp.concatenate([-sin, sin], axis=1), (1, 2)), jnp.zeros((s - t, BLK), F32)], axis=0)
    return cos, sin


BIG = ("ffn1_in", "ffn1_out", "w_in", "w_out", "ffn2_in", "ffn2_out")
GROUPS = dict(ffn1=("ffn1_in", "ffn1_out"), mix=("w_in", "w_out"), ffn2=("ffn2_in", "ffn2_out"))
GATHER_BEHIND = {("ffn1", 0): [("w_in", 0), ("ffn2_out", 0), ("ffn1_out", 1)], ("proj", 0): [("w_out", 0)],
                 ("mix", 0): [("ffn2_in", 0)], ("ffn2", 0): [("ffn1_in", 1), ("w_in", 1)],
                 ("ffn1", 1): [("ffn2_in", 1), ("w_out", 1)], ("mix", 1): [("ffn2_out", 1)]}


def _slot_major(name, g):
    if name == "w_in":
        return jnp.stack(jnp.split(g, N_SLOT, axis=1), axis=0)
    if name in ("ffn1_in", "ffn2_in"):
        return g
    return g.reshape(N_SLOT, g.shape[0] // N_SLOT, g.shape[1])


def _whole_weight(name, buf):
    if name == "w_in":
        return buf.transpose(1, 0, 2).reshape(D, PROJ_W)
    if name in ("ffn1_in", "ffn2_in"):
        return buf
    return buf.reshape(-1, buf.shape[2])


def local_step(x1, ctx1, target, mods, norms, nfinal, placed, w_pool, pool_scale, sink, place, small_blocks):
    t, s = x1.shape[0], x1.shape[0] + ctx1.shape[0]
    n_lat = t // TM
    cos, sin = rope_tables(t, s)
    tables = mix_tables(t, s)
    wts ={name: list(pair) for name, pair in placed.items()}

    def gather(tensors):
        return gather_exchange([wts[name][l] for name, l in tensors])

    def gathered(tensors, arrays):
        for (name, l), whole in zip(tensors, arrays):
            wts[name][l] = whole

    def weight(name, l):
        return _whole_weight(name, wts[name][l])

    def fwd_ex(grp, l):
        groups = GATHER_BEHIND.get((grp, l))
        return (groups, gather(groups)) if groups else (None, None)

    h = jnp.concatenate([x1, ctx1], axis=0)
    saved = []
    for l in range(2):
        h0 = h
        groups, ex = fwd_ex("ffn1", l)
        (h1, ab1, f1), got = ffn_fwd(h0, mods, norms[0], weight("ffn1_in", l), weight("ffn1_out", l), l, 0, n_lat, f"ffn1_fwd_{l}", ex)
        gathered(groups or [], got)
        groups, ex = fwd_ex("proj", l)
        (u, q, k, v), got = proj_fwd(h1, mods, norms[1], weight("w_in", l), cos, sin, l, n_lat, f"proj_fwd_{l}", ex)
        gathered(groups or [], got)
        groups, ex = fwd_ex("mix", l)
        (h2, cat, lse, mo), got = mix_fwd(h1, q, k, v, u, w_pool, pool_scale, sink, weight("w_out", l), mods, tables, l, t,
                                          f"mix_fwd_{l}", ex)
        gathered(groups or [], got)
        groups, ex = fwd_ex("ffn2", l)
        (h, ab2, f2), got = ffn_fwd(h2, mods, norms[2], weight("ffn2_in", l), weight("ffn2_out", l), l, 6, n_lat, f"ffn2_fwd_{l}", ex)
        gathered(groups or [], got)
        saved.append((h0, ab1, f1, h1, u, q, k, v, cat, lse, mo, h2, ab2, f2))
    dh, loss_blk = loss_head(h, target, nfinal, t, "loss_head")

    halves = {name: [None, None] for name in BIG}
    pending = []

    def summed_in_pair(grp, l, name_a, g_a, name_b, wgrad_b):
        g_b, got_a = wgrad_b(pair_exchange([_slot_major(name_a, g_a[1])]))
        sum_a, got_b = pair_sum(_slot_major(name_a, g_a[0]), got_a[0], place, f"pair_sum_{name_a}_{l}",
                                pair_exchange([_slot_major(name_b, g_b[1])]))
        sums = {name_a: sum_a, name_b: pair_sum(_slot_major(name_b, g_b[0]), got_b[0], place, f"pair_sum_{name_b}_{l}")}
        pending.append((grp, l, [sums[n] for n in GROUPS[grp]]))

    lacking = []

    def riders():
        return (scatter_exchange([p16 for _, p16 in pending[0][2]]) if pending else None,
                pair_fill_exchange([halves[name][l] for name, l in lacking]) if lacking else None)

    def carried(got, exs):
        got, filled = split_outputs(got, *exs)
        for (name, l), whole in zip(list(lacking), filled):
            halves[name][l] = whole
            lacking.remove((name, l))
        if pending:
            grp, l, pairs = pending.pop(0)
            for i, name in enumerate(GROUPS[grp]):
                halves[name][l] = chip_sum(pairs[i][0], got[3 * i:3 * i + 3], place, f"chip_sum_{name}_{l}")
                lacking.append((name, l))

    small = [None, None]
    for l in (1, 0):
        h0, ab1, f1, h1, u, q, k, v, cat, lse, mo, h2, ab2, f2 = saved[l]
        exs = riders()
        (dh, dab, df, n, act, dm_f2), got = ffn_bwd(h2, ab2, f2, dh, mods, norms[2], weight("ffn2_in", l), weight("ffn2_out", l),
                                                    l, 6, n_lat, f"ffn2_bwd_{l}", both(*exs))
        carried(got, exs)
        g_in, _ = wgrad(n, dab, D // 2, FF_COLS, FF_COLS,f"ffn2_in_wgrad_{l}")
        summed_in_pair("ffn2", l, "ffn2_in", g_in, "ffn2_out",
                       lambda ex, a=act, b=df: wgrad(a, b, D_FF // 2, D // 2, None,f"ffn2_out_wgrad_{l}", ex))
        exs = riders()
        (dq, dk, dv, du, dmo, dwp, dps, dsink, dm_gate), got = mix_bwd(
            dh, mo, q, k, v, u, lse, w_pool, pool_scale, sink, weight("w_out", l), mods, tables, l, t, f"mix_bwd_{l}", both(*exs))
        carried(got, exs)
        g_wo, _ = wgrad(cat, dmo, POOL_W + ATTN_W, D, None, f"w_out_wgrad_{l}")
        dh, dp, n, dm_mix = proj_bwd(h1, du, dq, dk, dv, dh, mods, norms[1], weight("w_in", l), cos, sin, l, n_lat, f"proj_bwd_{l}")
        summed_in_pair("mix", l, "w_out", g_wo, "w_in",
                       lambda ex, a=n, b=dp: wgrad(a, b, D, PROJ_W // 2, None, f"w_in_wgrad_{l}", ex))
        exs = riders()
        (dh, dab, df, n, act, dm_f1), got = ffn_bwd(h0, ab1, f1, dh, mods, norms[0], weight("ffn1_in", l), weight("ffn1_out", l),
                                                    l, 0, n_lat, f"ffn1_bwd_{l}", both(*exs))
        carried(got, exs)
        small[l] = dict(dm_f1=dm_f1, dm_mix=dm_mix, dm_gate=dm_gate, dm_f2=dm_f2, dwp=dwp, dps=dps, dsink=dsink)
        if l:
            g_in, _ = wgrad(n, dab, D // 2, FF_COLS, FF_COLS,f"ffn1_in_wgrad_{l}")
            summed_in_pair("ffn1", l, "ffn1_in", g_in, "ffn1_out",
                           lambda ex, a=act, b=df: wgrad(a, b, D_FF // 2, D // 2, None,f"ffn1_out_wgrad_{l}", ex))
    g_out, _ = wgrad(act, df, D_FF // 2, D // 2, None, "ffn1_out_wgrad_0")
    got = run_exchange(pair_exchange([_slot_major("ffn1_out", g_out[1])]), "pair_exchange_ffn1_out_0")
    p32, p16 = pair_sum(_slot_major("ffn1_out", g_out[0]), got[0], place, "pair_sum_ffn1_out_0")
    riding = (gather8_exchange(small_blocks(small, loss_blk)), scatter_exchange([p16]),
              pair_fill_exchange([halves[name][l] for name, l in lacking]))
    g_in, got = wgrad(n, dab, D // 2, FF_COLS, FF_COLS,"ffn1_in_wgrad_0", both(*riding))
    small_all, got, filled = split_outputs(got, *riding)
    for (name, l), whole in zip(lacking, filled):
        halves[name][l] = whole
    (halves["ffn1_out"][0],) = pair_gather([chip_sum(p32, got, place, "chip_sum_ffn1_out_0")], "pair_gather_ffn1_out_0")
    got = run_exchange(pair_exchange([_slot_major("ffn1_in", g_in[1])]), "pair_exchange_ffn1_in_0")
    return dh[:t], halves, pair_sum(_slot_major("ffn1_in", g_in[0]), got[0], place, "pair_sum_ffn1_in_0"), small_all


def _silu_grad(z):
    sg = jax.nn.sigmoid(z)
    return sg * (1 + z * (1 - sg))


def kernel(x, c, ctx, c_ctx, w_mod, b_mod, norm_ffn1, w_ffn1_in, w_ffn1_out, norm_mix, w_in, w_pool, pool_scale, sink, w_out, norm_ffn2, w_ffn2_in, w_ffn2_out, norm_final, loss_target, m_c_ctx, m_w_mod, m_b_mod, m_norm_ffn1, m_w_ffn1_in, m_w_ffn1_out, m_norm_mix, m_w_in, m_w_pool, m_pool_scale, m_sink, m_w_out, m_norm_ffn2, m_w_ffn2_in, m_w_ffn2_out, m_norm_final, v_c_ctx, v_w_mod, v_b_mod, v_norm_ffn1, v_w_ffn1_in, v_w_ffn1_out, v_norm_mix, v_w_in, v_w_pool, v_pool_scale, v_sink, v_w_out, v_norm_ffn2, v_w_ffn2_in, v_w_ffn2_out, v_norm_final):
    px, py, pc = _place()
    slot, me = 2 * px + py, 4 * px + 2 * py + pc
    n_grp = len(POOL_WINDOWS)

    (c_rows,) = all_gather([c.reshape(8, D // 8)], "gather_c")
    c_all = jnp.concatenate([c_rows.reshape(N_DEV, D), c_ctx.reshape(1, D), jnp.zeros((16 - N_DEV - 1, D), F32)], axis=0)
    b_cols = lax.dynamic_slice(b_mod, (0, slot * MOD_COLS), (2, MOD_COLS)).reshape(2, 1, MOD_COLS)
    (mod_parts,) = all_gather([mod_rows(c_all, w_mod, b_cols, "mod_rows")], "gather_mods")
    mods_all = mod_parts[0::2].transpose(1, 2, 0, 3).reshape(2, 16, N_MOD * D)
    mx = lax.dynamic_slice(mods_all, (0, me, 0), (2, 1, N_MOD * D)).reshape(2, N_MOD, D)
    mc = mods_all[:, N_DEV].reshape(2, N_MOD, D)
    pad = jnp.zeros((2, 16 - N_MOD, D), F32)
    mods = jnp.stack([jnp.concatenate([mx, pad], axis=1), jnp.concatenate([mc, pad], axis=1)], axis=1)

    place = jnp.stack([pc, slot]).astype(jnp.int32)
    shards = dict(ffn1_in=w_ffn1_in, ffn1_out=w_ffn1_out, w_in=w_in, w_out=w_out, ffn2_in=w_ffn2_in, ffn2_out=w_ffn2_out)
    first = [("ffn1_in", 0), ("ffn1_out", 0)]
    placed = {name: [None, None] for name in BIG}
    for name, l in first:
        placed[name][l] = cast_place(shards[name], l, place, f"cast_{name}_{l}")
    send_sems, recv_sems, *bufs, token = gather_start([placed[name][l] for name, l in first], "gather_first_start")
    others = [(name, l) for name in BIG for l in range(2) if (name, l) not in first]
    for name, l in others:
        placed[name][l] = cast_place(shards[name], l, place, f"cast_{name}_{l}")
    bufs = gather_wait(send_sems, recv_sems, bufs, [placed[name][l] for name, l in others], "gather_first_wait")
    for (name, l), whole in zip(first, run_exchange(pass_on_exchange(bufs), "gather_first_pass_on")):
        placed[name][l] = whole
    norms = [g.reshape(2, 1, D) for g in (norm_ffn1, norm_mix, norm_ffn2)]
    row_sums = ("dm_f1", "dm_mix", "dm_gate", "dm_f2")

    def small_blocks(small, loss_blk):
        stacked = {k: jnp.stack([small[0][k], small[1][k]]) for k in row_sums + ("dwp", "dps", "dsink")}
        return ([stacked[k].reshape(32, D) for k in row_sums]
                + [stacked["dwp"].reshape(2 * n_grp * GROUP, GROUP), stacked["dps"].reshape(16, POOL_W),
                   stacked["dsink"].reshape(16, BLK), loss_blk])

    dx, halves, last_pair, small_all = local_step(x[0], ctx[0], loss_target[0], mods, norms, norm_final.reshape(1, D), placed,
                                                   w_pool.astype(BF16), pool_scale.reshape(2, 1, POOL_W), sink, place, small_blocks)
    grads = {}

    *g_dm, g_dwp, g_dps, g_dsink, g_loss = small_all
    tot, rows, fin = reduce_small(*[g.reshape(N_DEV, 2, 2, 8, D) for g in g_dm], g_loss, "reduce_small")
    s_dwp, s_dps, s_dsink = sum8([g_dwp, g_dps, g_dsink], "sum_pool_sink")
    grads.update(
        w_pool=s_dwp.reshape(2, n_grp, GROUP, GROUP), pool_scale=s_dps.reshape(2, 8, POOL_W)[:, 0],
        sink=s_dsink.reshape(2, 8, BLK)[:, 0, :N_HEADS], b_mod=tot[:, :N_MOD].reshape(2, N_MOD * D),
        norm_ffn1=tot[:, N_MOD], norm_mix=tot[:, N_MOD + 1], norm_ffn2=tot[:, N_MOD + 2], norm_final=fin[0])
    loss = fin[1, 0]

    dmod_cols = lax.dynamic_slice(rows[:, :, :N_MOD, :].reshape(2, 16, N_MOD * D), (0, 0, slot * MOD_COLS), (2, 16, MOD_COLS))
    grads["w_mod"], dc = mod_grads(c_all, dmod_cols, w_mod, "mod_grads")
    (g_dc,) = all_gather([dc], "gather_dc")
    (s_dc,) = sum8([g_dc], "sum_dc")
    (d_c_ctx,) = elementwise(lambda d, z: (0.5 * d * _silu_grad(z),), [s_dc[N_DEV:N_DEV + 1], c_ctx.reshape(1, D)], [F32], "c_ctx_grad")
    send_sems, recv_sems, src_thru, *lands, token = scatter_start(last_pair[1], "scatter_last_start")
    grads["c_ctx"] = d_c_ctx.reshape(D) + token[0, :1]

    given = dict(c_ctx=(c_ctx, m_c_ctx, v_c_ctx), w_mod=(w_mod, m_w_mod, v_w_mod), b_mod=(b_mod, m_b_mod, v_b_mod),
                 norm_ffn1=(norm_ffn1, m_norm_ffn1, v_norm_ffn1), w_ffn1_in=(w_ffn1_in, m_w_ffn1_in, v_w_ffn1_in),
                 w_ffn1_out=(w_ffn1_out, m_w_ffn1_out, v_w_ffn1_out), norm_mix=(norm_mix, m_norm_mix, v_norm_mix),
                 w_in=(w_in, m_w_in, v_w_in), w_pool=(w_pool, m_w_pool, v_w_pool),
                 pool_scale=(pool_scale, m_pool_scale, v_pool_scale), sink=(sink, m_sink, v_sink), w_out=(w_out, m_w_out, v_w_out),
                 norm_ffn2=(norm_ffn2, m_norm_ffn2, v_norm_ffn2), w_ffn2_in=(w_ffn2_in, m_w_ffn2_in, v_w_ffn2_in),
                 w_ffn2_out=(w_ffn2_out, m_w_ffn2_out, v_w_ffn2_out), norm_final=(norm_final, m_norm_final, v_norm_final))
    shard = {(name, l): halves[name][l] for name in BIG for l in range(2)}

    def update(name):
        w, m, v = given[name]
        if name in BIG or name[2:] in BIG:
            key = name if name in BIG else name[2:]
            return adamw_layers(w, shard[key, 0], shard[key, 1], m, v, f"adamw_{name}")
        return [grads[name], *adamw(w, grads[name], m, v, f"adamw_{name}")]

    done = {name: update(name) for name in given if name != "w_ffn1_in"}
    got = scatter_wait(send_sems, recv_sems, src_thru, lands, [done[name][3] for name in done if name[2:] in BIG or name in BIG]
                       + [done["w_mod"][3]], "scatter_last_wait")
    (shard["ffn1_in", 0],) = pair_gather([chip_sum(last_pair[0], got, place, "chip_sum_ffn1_in_0")], "grad_pair_gather_last")
    done["w_ffn1_in"] = update("w_ffn1_in")
    return (loss, dx[None], *[done[name][i] for i in range(4) for name in given])
```

```python
import functools

import jax
import jax.numpy as jnp
from jax import lax
from jax.experimental import pallas as pl
from jax.experimental.pallas import tpu as pltpu

F32, BF16 = jnp.float32, jnp.bfloat16
D = 1024
D_FF = 2816
N_SLOT = 4
FF_COLS = 2 * D_FF // N_SLOT
N_MOD = 9
MOD_COLS = N_MOD * D // N_SLOT
POOL_W, ATTN_W, KV_W = 512, 512, 128
PROJ_W = POOL_W + ATTN_W + 2 * KV_W
N_HEADS, Q_GROUP, HEAD = 8, 4, 64
GROUP = 128
POOL_WINDOWS = (2, 4, 8, 16)
BLK = 128
QB = 256
WIN = QB + 2 * BLK
GRID_W = 64
ROPE_BASE = 10000.0
EPS = 1e-6
NEG_INF = -1e30
TM = 256
N_DEV = 8
VMEM_LIMIT_BYTES = 56 * 1024 * 1024
WGRAD_VMEM_BYTES = 44 * 1024 * 1024
ADAM_LR, ADAM_B1, ADAM_B2, ADAM_EPS, ADAM_WD, ADAM_STEP = 0.001, 0.9, 0.999, 1e-08, 0.01, 10
MESH = pl.DeviceIdType.MESH
NT = (((1,), (1,)), ((), ()))
TN = (((0,), (0,)), ((), ()))


def _params(*sem):
    return pltpu.CompilerParams(dimension_semantics=sem, vmem_limit_bytes=VMEM_LIMIT_BYTES)


def _whole(shape, lead=()):
    idx = tuple(lead) + (0,) * len(shape)
    return pl.BlockSpec((None,) * len(lead) + tuple(shape), lambda *_: idx, pipeline_mode=pl.Buffered(1))


def _rows(cols, tm=TM):
    return pl.BlockSpec((tm, cols), lambda i: (i, 0))


def _mods_spec(layer, n_lat):
    return pl.BlockSpec((None, None, 16, D), lambda i: (layer, (i >= n_lat).astype(jnp.int32), 0, 0))


def _acc_spec(n_lat):
    return pl.BlockSpec((None, 8, D), lambda i: ((i >= n_lat).astype(jnp.int32), 0, 0))


def _dot(a, b):
    return jnp.dot(a, b, preferred_element_type=F32)


def _dotg(a, b, dims):
    return lax.dot_general(a, b, dims, preferred_element_type=F32)


def _sum0(v):
    return jnp.sum(v, axis=0, keepdims=True)


def _norm_mod(h, g, shift, scale):
    r = lax.rsqrt(jnp.mean(h * h, axis=-1, keepdims=True) + EPS)
    xhat = h * r
    y = xhat * g
    return y * (1 + scale) + shift, xhat, r, y


def _norm_mod_bwd(dn, xhat, r, y, g, scale):
    dy = dn * (1 + scale)
    dx = dy * g
    dh = r * (dx - xhat * jnp.mean(dx * xhat, axis=-1, keepdims=True))
    return _sum0(dn), _sum0(dn * y), _sum0(dy * xhat), dh


def _swap_halves(v):
    w = v.shape[1]
    lane = lax.broadcasted_iota(jnp.int32, v.shape, 1)
    return jnp.where(lane % HEAD < HEAD // 2, pltpu.roll(v, w - HEAD // 2, axis=1), pltpu.roll(v, HEAD // 2, axis=1))


def _tile_lanes(t, width):
    return t if width == t.shape[1] else jnp.concatenate([t] * (width // t.shape[1]), axis=1)


def _rope(v, cos, sin):
    return v * _tile_lanes(cos, v.shape[1]) + _swap_halves(v) * _tile_lanes(sin, v.shape[1])


def _unrope(g, cos, sin):
    return g * _tile_lanes(cos, g.shape[1]) + _swap_halves(g * _tile_lanes(sin, g.shape[1]))


def ffn_fwd(h, mods, g, w4, wo, layer, k0, n_lat, name, ex=None):
    s = h.shape[0]

    def body(h_ref, m_ref, g_ref, w_ref, wo_ref, ho_ref, ab_ref, f_ref):
        hh = h_ref[...]
        n, _, _, _ = _norm_mod(hh, g_ref[...], m_ref[k0:k0 + 1, :], m_ref[k0 + 1:k0 + 2, :])
        nb = n.astype(BF16)
        acc = jnp.zeros((TM, D), F32)
        for j in range(2):
            a = _dot(nb, w_ref[j])
            b = _dot(nb, w_ref[2 + j])
            ab_ref[:, j * FF_COLS:(j + 1) * FF_COLS] = a.astype(BF16)
            ab_ref[:, (2 + j) * FF_COLS:(3 + j) * FF_COLS] = b.astype(BF16)
            act = (a * jax.nn.sigmoid(a) * b).astype(BF16)
            acc = acc + _dot(act, wo_ref[j * FF_COLS:(j + 1) * FF_COLS, :])
        f_ref[...] = acc
        ho_ref[...] = hh + 0.5 * m_ref[k0 + 2:k0 + 3, :] * acc

    return _grid_call(
        body, name, s // TM,
        [_rows(D), _mods_spec(layer, n_lat), _whole((1, D), (layer,)), _whole((N_SLOT, D, FF_COLS)), _whole((D_FF, D))],
        [_rows(D), _rows(2 * D_FF), _rows(D)],
        [jax.ShapeDtypeStruct((s, D), F32), jax.ShapeDtypeStruct((s, 2 * D_FF), BF16), jax.ShapeDtypeStruct((s, D), F32)],
        (h, mods, g, w4, wo), "parallel", ex)


def ffn_bwd(h, ab, f, dh, mods, g, w4, wo, layer, k0, n_lat, name, ex=None):
    s = h.shape[0]

    def body(h_ref, ab_ref, f_ref, dh_ref, m_ref, g_ref, w_ref, wo_ref, dhi_ref, dab_ref, df_ref, n_ref, act_ref, dm_ref):
        i = pl.program_id(0)

        @pl.when((i == 0) | (i == n_lat))
        def _():
            dm_ref[...] = jnp.zeros_like(dm_ref)

        hh, dho, gg = h_ref[...], dh_ref[...], g_ref[...]
        scale, gate = m_ref[k0 + 1:k0 + 2, :], m_ref[k0 + 2:k0 + 3, :]
        n, xhat, r, y = _norm_mod(hh, gg, m_ref[k0:k0 + 1, :], scale)
        n_ref[...] = n.astype(BF16)
        dgate = _sum0(dho * (0.5 * f_ref[...]))
        dfb = ((0.5 * gate) * dho).astype(BF16)
        df_ref[...] = dfb
        dn = jnp.zeros((TM, D), F32)
        for j in range(2):
            a = ab_ref[:, j * FF_COLS:(j + 1) * FF_COLS].astype(F32)
            b = ab_ref[:, (2 + j) * FF_COLS:(3 + j) * FF_COLS].astype(F32)
            sg = jax.nn.sigmoid(a)
            sa = a * sg
            act_ref[:, j * FF_COLS:(j + 1) * FF_COLS] = (sa * b).astype(BF16)
            dact = _dotg(dfb, wo_ref[j * FF_COLS:(j + 1) * FF_COLS, :], NT)
            da = (dact * b * (sg * (1 + a * (1 - sg)))).astype(BF16)
            db = (dact * sa).astype(BF16)
            dab_ref[:, j * FF_COLS:(j + 1) * FF_COLS] = da
            dab_ref[:, (2 + j) * FF_COLS:(3 + j) * FF_COLS] = db
            dn = dn + _dotg(da, w_ref[j], NT) + _dotg(db, w_ref[2 + j], NT)
        dsh, dsc, dg, dhn = _norm_mod_bwd(dn, xhat, r, y, gg, scale)
        dhi_ref[...] = dho + dhn
        dm_ref[0:1, :] += dsh
        dm_ref[1:2, :] += dsc
        dm_ref[2:3, :] += dgate
        dm_ref[3:4, :] += dg

    return _grid_call(
        body, name, s // TM,
        [_rows(D), _rows(2 * D_FF), _rows(D), _rows(D), _mods_spec(layer, n_lat), _whole((1, D), (layer,)),
         _whole((N_SLOT, D, FF_COLS)), _whole((D_FF, D))],
        [_rows(D), _rows(2 * D_FF), _rows(D), _rows(D), _rows(D_FF), _acc_spec(n_lat)],
        [jax.ShapeDtypeStruct((s, D), F32), jax.ShapeDtypeStruct((s, 2 * D_FF), BF16), jax.ShapeDtypeStruct((s, D), BF16),
         jax.ShapeDtypeStruct((s, D), BF16), jax.ShapeDtypeStruct((s, D_FF), BF16), jax.ShapeDtypeStruct((2, 8, D), F32)],
        (h, ab, f, dh, mods, g, w4, wo), "arbitrary", ex)


def _token_tile(s, limit=2176):
    return max(ts for ts in range(16, limit + 1, 16) if s % ts == 0)


def wgrad(a, b, tk, tn, slot_cols, name, ex=None):
    s, k = a.shape
    n = b.shape[1]
    a_bufs, b_bufs = (1 if k == tk else 2), (1 if n == tn else 2)
    whole = 2 * s * (a_bufs * tk + b_bufs * tn) + 2 * 6 * tk * tn
    ts = s if whole <= WGRAD_VMEM_BYTES else _token_tile(s)
    steps = s // ts
    once = dict(pipeline_mode=pl.Buffered(1))

    def body(a_ref, b_ref, o_ref, o16_ref):
        r = _dotg(a_ref[...], b_ref[...], TN)
        si = pl.program_id(2)

        @pl.when(si == 0)
        def _():
            o_ref[...] = r

        @pl.when(si > 0)
        def _():
            o_ref[...] += r

        @pl.when(si == steps - 1)
        def _():
            o16_ref[...] = o_ref[...].astype(BF16)

    n_outer = tn > tk
    grid = (n // tn, k // tk, steps) if n_outer else (k // tk, n // tn, steps)
    ij = (lambda g0, g1: (g1, g0)) if n_outer else (lambda g0, g1: (g0, g1))

    def a_map(g0, g1, si):
        return si, ij(g0, g1)[0]

    def b_map(g0, g1, si):
        return si, ij(g0, g1)[1]

    if slot_cols is None:
        shape, spec = (k, n), pl.BlockSpec((tk, tn), lambda g0, g1, si: ij(g0, g1))
    else:
        per = slot_cols // tn

        def slot_map(g0, g1, si):
            i, j = ij(g0, g1)
            return lax.div(j, per), i, lax.rem(j, per)

        shape, spec = (n // slot_cols, k, slot_cols), pl.BlockSpec((None, tk, tn), slot_map)
    return _grid_call(
        body, name, grid,
        [pl.BlockSpec((ts, tk), a_map, **(once if a_bufs == 1 and steps == 1 else {})),
         pl.BlockSpec((ts, tn), b_map, **(once if b_bufs == 1 and steps == 1 else {}))], [spec, spec],
        [jax.ShapeDtypeStruct(shape, F32), jax.ShapeDtypeStruct(shape, BF16)], (a, b), ("parallel", "parallel", "arbitrary"), ex)


def proj_fwd(h, mods, g, w_in, cos, sin, layer, n_lat, name, ex=None):
    s = h.shape[0]

    def body(h_ref, m_ref, g_ref, w_ref, cos_ref, sin_ref, u_ref, q_ref, k_ref, v_ref):
        n, _, _, _ = _norm_mod(h_ref[...], g_ref[...], m_ref[3:4, :], m_ref[4:5, :])
        p = _dot(n.astype(BF16), w_ref[...])
        cs, sn = cos_ref[...], sin_ref[...]
        u_ref[...] = p[:, :POOL_W]
        q_ref[...] = (_rope(p[:, POOL_W:POOL_W + ATTN_W], cs, sn) * HEAD ** -0.5).astype(BF16)
        k_ref[...] = _rope(p[:, POOL_W + ATTN_W:POOL_W + ATTN_W + KV_W], cs, sn).astype(BF16)
        v_ref[...] = p[:, POOL_W + ATTN_W + KV_W:].astype(BF16)

    return _grid_call(
        body, name, s // TM,
        [_rows(D), _mods_spec(layer, n_lat), _whole((1, D), (layer,)), _whole((D, PROJ_W)), _rows(BLK), _rows(BLK)],
        [_rows(POOL_W), _rows(ATTN_W), _rows(KV_W), _rows(KV_W)],
        [jax.ShapeDtypeStruct((s, POOL_W), F32), jax.ShapeDtypeStruct((s, ATTN_W), BF16),
         jax.ShapeDtypeStruct((s, KV_W), BF16), jax.ShapeDtypeStruct((s, KV_W), BF16)],
        (h, mods, g, w_in, cos, sin), "parallel", ex)


def proj_bwd(h, du, dq, dk, dv, dh, mods, g, w_in, cos, sin, layer, n_lat, name):
    s = h.shape[0]

    def body(h_ref, du_ref, dq_ref, dk_ref, dv_ref, dh_ref, m_ref, g_ref, w_ref, cos_ref, sin_ref,
             dhi_ref, dp_ref, n_ref, dm_ref):
        i = pl.program_id(0)

        @pl.when((i == 0) | (i == n_lat))
        def _():
            dm_ref[...] = jnp.zeros_like(dm_ref)

        gg, scale = g_ref[...], m_ref[4:5, :]
        n, xhat, r, y = _norm_mod(h_ref[...], gg, m_ref[3:4, :], scale)
        n_ref[...] = n.astype(BF16)
        cs, sn = cos_ref[...], sin_ref[...]
        dp = jnp.concatenate([du_ref[...], _unrope(dq_ref[...], cs, sn) * HEAD ** -0.5, _unrope(dk_ref[...], cs, sn),
                              dv_ref[...]], axis=1).astype(BF16)
        dp_ref[...] = dp
        dsh, dsc, dg, dhn = _norm_mod_bwd(_dotg(dp, w_ref[...], NT), xhat, r, y, gg, scale)
        dhi_ref[...] = dh_ref[...] + dhn
        dm_ref[0:1, :] += dsh
        dm_ref[1:2, :] += dsc
        dm_ref[3:4, :] += dg

    return pl.pallas_call(
        body, name=name, grid=(s // TM,),
        in_specs=[_rows(D), _rows(POOL_W), _rows(ATTN_W), _rows(KV_W), _rows(KV_W), _rows(D), _mods_spec(layer, n_lat),
                  _whole((1, D), (layer,)), _whole((D, PROJ_W)), _rows(BLK), _rows(BLK)],
        out_specs=[_rows(D), _rows(PROJ_W), _rows(D), _acc_spec(n_lat)],
        out_shape=[jax.ShapeDtypeStruct((s, D), F32), jax.ShapeDtypeStruct((s, PROJ_W), BF16),
                   jax.ShapeDtypeStruct((s, D), BF16), jax.ShapeDtypeStruct((2, 8, D), F32)],
        compiler_params=_params("arbitrary"),
    )(h, du, dq, dk, dv, dh, mods, g, w_in, cos, sin)


def _window(i, s):
    return pl.multiple_of(jnp.clip(i * QB - BLK, 0, s - WIN), BLK)


def mix_tables(t, s):
    n_lat = t // QB
    blocks = jnp.array([0, 1, n_lat - 1] + list(range(n_lat, s // QB)))[:, None, None]
    ws = jnp.clip(blocks * QB - BLK, 0, s - WIN)
    q = blocks * QB + jnp.arange(QB)[None, :, None]
    k = ws + jnp.arange(WIN)[None, None, :]
    is_lat = blocks < n_lat
    local = jnp.where(is_lat & (k < t) & (jnp.abs(k - q) <= BLK), 0.0, NEG_INF).astype(F32)
    bias = jnp.concatenate([local, jnp.zeros(local.shape[:2] + (s - t,), F32)], axis=2)
    seq_lo, seq_hi = jnp.where(is_lat, 0, t), jnp.where(is_lat, t, s)
    bands, counts = [], []
    for w in POOL_WINDOWS:
        lo, hi = jnp.maximum(q - w // 2, seq_lo), jnp.minimum(q + w - w // 2, seq_hi)
        bands.append((k >= lo) & (k < hi))
        counts.append((hi - lo).astype(F32))
    band = jnp.stack(bands, axis=1).astype(BF16)
    count = jnp.concatenate(counts + [jnp.ones(counts[0].shape[:2] + (BLK - len(counts),), F32)], axis=2)
    return dict(bias=bias, band=band, band_t=band.transpose(0, 1, 3, 2), count=count)


def _case_spec(table, n_lat_blk):
    def kind(i):
        return jnp.where(i < n_lat_blk - 1, jnp.minimum(i, 1), i - n_lat_blk + 3)

    shape = table.shape[1:]
    return pl.BlockSpec((None,) + shape, lambda i: (kind(i),) + (0,) * len(shape))


def _split_dot(band, v):
    return _dot(band, v.astype(BF16))


def _pooled(u_ref, band_ref, cnt_ref, i, ws, gi):
    cols = slice(gi * GROUP, (gi + 1) * GROUP)
    mean = _split_dot(band_ref[gi], u_ref[pl.ds(ws, WIN), cols]) / cnt_ref[:, gi:gi + 1]
    return mean - u_ref[pl.ds(pl.multiple_of(i * QB, QB), QB), cols]


def _head_cols(hd):
    return slice(hd * HEAD, (hd + 1) * HEAD)


def _stack_heads(x, hk, first=0):
    return jnp.concatenate([x[:, first + (Q_GROUP * hk + g) * HEAD:first + (Q_GROUP * hk + g + 1) * HEAD]
                            for g in range(Q_GROUP)], axis=0)


def _biased(scores, bias):
    return (scores.reshape(Q_GROUP, QB, -1) + bias).reshape(Q_GROUP * QB, -1)


def _group_column(vals):
    row = lax.broadcasted_iota(jnp.int32, (Q_GROUP * QB, 1), 0)
    out = jnp.full((Q_GROUP * QB, 1), vals[Q_GROUP - 1], F32)
    for g in range(Q_GROUP - 2, -1, -1):
        out = jnp.where(row < (g + 1) * QB, vals[g], out)
    return out


def _lane_place(cols, width=BLK):
    lane = lax.broadcasted_iota(jnp.int32, (cols[0].shape[0], width), 1)
    out = jnp.zeros((cols[0].shape[0], width), F32)
    for hd, c in enumerate(cols):
        out = jnp.where(lane == hd, c, out)
    return out


def mix_fwd(h, q, k, v, u, w_pool, pool_scale, sink, w_out, mods, tables, layer, t, name, ex=None):
    s = h.shape[0]
    n_lat_blk = t // QB

    def body(h_ref, q_ref, k_ref, v_ref, u_ref, wp_ref, ps_ref, sink_ref, wo_ref, m_ref, bias_ref, band_ref, cnt_ref,
             ho_ref, cat_ref, lse_ref, mo_ref):
        i = pl.program_id(0)
        ws = _window(i, s)
        for gi in range(len(POOL_WINDOWS)):
            mixed = _dot(_pooled(u_ref, band_ref, cnt_ref, i, ws, gi).astype(BF16), wp_ref[gi])
            cat_ref[:, gi * GROUP:(gi + 1) * GROUP] = (mixed * ps_ref[:, gi * GROUP:(gi + 1) * GROUP]).astype(BF16)
        bias = bias_ref[...]
        k_all = jnp.concatenate([k_ref[pl.ds(ws, WIN), :], k_ref[t:s, :]], axis=0)
        v_all = jnp.concatenate([v_ref[pl.ds(ws, WIN), :], v_ref[t:s, :]], axis=0)
        lses = []
        for hk in range(N_HEADS // Q_GROUP):
            kv = _head_cols(hk)
            sc = _biased(_dotg(_stack_heads(q_ref[...], hk), k_all[:, kv], NT), bias)
            sk = _group_column([sink_ref[layer, Q_GROUP * hk + g] for g in range(Q_GROUP)])
            m = jnp.maximum(jnp.max(sc, axis=1, keepdims=True), sk)
            e = jnp.exp(sc - m)
            l = jnp.sum(e, axis=1, keepdims=True) + jnp.exp(sk - m)
            o = _dot(e.astype(BF16), v_all[:, kv]) * (1.0 / l)
            lse = m + jnp.log(l)
            for g in range(Q_GROUP):
                hd = Q_GROUP * hk + g
                cat_ref[:, POOL_W + hd * HEAD:POOL_W + (hd + 1) * HEAD] = o[g * QB:(g + 1) * QB].astype(BF16)
                lses.append(lse[g * QB:(g + 1) * QB])
        lse_ref[...] = _lane_place(lses)
        mo = _dot(cat_ref[...], wo_ref[...])
        mo_ref[...] = mo
        ho_ref[...] = h_ref[...] + m_ref[5:6, :] * mo

    blk = lambda cols: _rows(cols, QB)
    return _grid_call(
        body, name, s // QB,
        [blk(D), blk(ATTN_W), _whole((s, KV_W)), _whole((s, KV_W)), _whole((s, POOL_W)),
         _whole((len(POOL_WINDOWS), GROUP, GROUP), (layer,)), _whole((1, POOL_W), (layer,)),
         pl.BlockSpec(memory_space=pltpu.SMEM), _whole((POOL_W + ATTN_W, D)), _mods_spec(layer, n_lat_blk),
         _case_spec(tables["bias"], n_lat_blk), _case_spec(tables["band"], n_lat_blk), _case_spec(tables["count"], n_lat_blk)],
        [blk(D), blk(POOL_W + ATTN_W), blk(BLK), blk(D)],
        [jax.ShapeDtypeStruct((s, D), F32), jax.ShapeDtypeStruct((s, POOL_W + ATTN_W), BF16), jax.ShapeDtypeStruct((s, BLK), F32),
         jax.ShapeDtypeStruct((s, D), F32)],
        (h, q, k, v, u, w_pool, pool_scale, sink, w_out, mods, tables["bias"], tables["band"], tables["count"]), "parallel", ex)


def mix_bwd(dh, mo, q, k, v, u, lse, w_pool, pool_scale, sink, w_out, mods, tables, layer, t, name, ex=None):
    s = dh.shape[0]
    n_lat_blk = t // QB
    n_grp = len(POOL_WINDOWS)

    def body(dh_ref, mo_ref, q_ref, k_ref, v_ref, u_ref, lse_ref, wp_ref, ps_ref, sink_ref, wo_ref, m_ref,
             bias_ref, band_ref, band_t_ref, cnt_ref,
             dq_ref, dk_ref, dv_ref, du_ref, dmo_ref, dwp_ref, dps_ref, dsink_ref, dm_ref):
        i = pl.program_id(0)

        @pl.when(i == 0)
        def _():
            for ref in (dk_ref, dv_ref, du_ref, dwp_ref, dps_ref, dsink_ref):
                ref[...] = jnp.zeros_like(ref)

        @pl.when((i == 0) | (i == n_lat_blk))
        def _():
            dm_ref[...] = jnp.zeros_like(dm_ref)

        ws = _window(i, s)
        here = pl.ds(pl.multiple_of(i * QB, QB), QB)
        dho = dh_ref[...]
        dm_ref[2:3, :] += _sum0(dho * mo_ref[...])
        dmo = (m_ref[5:6, :] * dho).astype(BF16)
        dmo_ref[...] = dmo
        dcat = _dotg(dmo, wo_ref[...], NT)

        for gi in range(n_grp):
            cols = slice(gi * GROUP, (gi + 1) * GROUP)
            pooled = _pooled(u_ref, band_ref, cnt_ref, i, ws, gi).astype(BF16)
            dpo = dcat[:, cols]
            dps_ref[0:1, cols] += _sum0(dpo * _dot(pooled, wp_ref[gi]))
            dmixed = (dpo * ps_ref[:, cols]).astype(BF16)
            dwp_ref[gi] += _dotg(pooled, dmixed, TN)
            dpooled = _dotg(dmixed, wp_ref[gi], NT)
            du_ref[pl.ds(ws, WIN), cols] += _split_dot(band_t_ref[gi], dpooled / cnt_ref[:, gi:gi + 1])
            du_ref[here, cols] -= dpooled

        bias = bias_ref[...]
        k_all = jnp.concatenate([k_ref[pl.ds(ws, WIN), :], k_ref[t:s, :]], axis=0)
        v_all = jnp.concatenate([v_ref[pl.ds(ws, WIN), :], v_ref[t:s, :]], axis=0)
        qq, lse_all = q_ref[...], lse_ref[...]
        dqs, dsinks, dks, dvs = [], [], [], []
        for hk in range(N_HEADS // Q_GROUP):
            kv = _head_cols(hk)
            q4 = _stack_heads(qq, hk)
            lse = jnp.concatenate([lse_all[:, Q_GROUP * hk + g:Q_GROUP * hk + g + 1] for g in range(Q_GROUP)], axis=0)
            p = jnp.exp(_biased(_dotg(q4, k_all[:, kv], NT), bias) - lse)
            do = _stack_heads(dcat, hk, POOL_W).astype(BF16)
            dp = _dotg(do, v_all[:, kv], NT)
            delta = jnp.sum(p * dp, axis=1, keepdims=True)
            ds = (p * (dp - delta)).astype(BF16)
            sk = _group_column([sink_ref[layer, Q_GROUP * hk + g] for g in range(Q_GROUP)])
            dsk = -jnp.exp(sk - lse) * delta
            dq = _dot(ds, k_all[:, kv])
            for g in range(Q_GROUP):
                dqs.append(dq[g * QB:(g + 1) * QB])
                dsinks.append(_sum0(dsk[g * QB:(g + 1) * QB]))
            dks.append(_dotg(ds, q4, TN))
            dvs.append(_dotg(p.astype(BF16), do, TN))
        dq_ref[...] = jnp.concatenate(dqs, axis=1)
        dk, dv = jnp.concatenate(dks, axis=1), jnp.concatenate(dvs, axis=1)
        dk_ref[pl.ds(ws, WIN), :] += dk[:WIN]
        dv_ref[pl.ds(ws, WIN), :] += dv[:WIN]
        dk_ref[t:s, :] += dk[WIN:]
        dv_ref[t:s, :] += dv[WIN:]
        dsink_ref[0:1, :] += _lane_place(dsinks)

    blk = lambda cols: _rows(cols, QB)
    full = lambda shape: pl.BlockSpec(shape, lambda i: (0,) * len(shape))
    return _grid_call(
        body, name, s // QB,
        [blk(D), blk(D), blk(ATTN_W), _whole((s, KV_W)), _whole((s, KV_W)), _whole((s, POOL_W)),
         blk(BLK), _whole((n_grp, GROUP, GROUP), (layer,)), _whole((1, POOL_W), (layer,)),
         pl.BlockSpec(memory_space=pltpu.SMEM), _whole((POOL_W + ATTN_W, D)), _mods_spec(layer, n_lat_blk)]
        + [_case_spec(tables[key], n_lat_blk) for key in ("bias", "band", "band_t", "count")],
        [blk(ATTN_W), full((s, KV_W)), full((s, KV_W)), full((s, POOL_W)), blk(D),
         full((n_grp, GROUP, GROUP)), full((8, POOL_W)), full((8, BLK)), _acc_spec(n_lat_blk)],
        [jax.ShapeDtypeStruct((s, ATTN_W), F32), jax.ShapeDtypeStruct((s, KV_W), F32),
         jax.ShapeDtypeStruct((s, KV_W), F32), jax.ShapeDtypeStruct((s, POOL_W), F32),
         jax.ShapeDtypeStruct((s, D), BF16), jax.ShapeDtypeStruct((n_grp, GROUP, GROUP), F32),
         jax.ShapeDtypeStruct((8, POOL_W), F32), jax.ShapeDtypeStruct((8, BLK), F32), jax.ShapeDtypeStruct((2, 8, D), F32)],
        (dh, mo, q, k, v, u, lse, w_pool, pool_scale, sink, w_out, mods, tables["bias"], tables["band"], tables["band_t"],
         tables["count"]), "arbitrary", ex)


def loss_head(h, target, g, t, name):
    s = h.shape[0]
    n_lat = t // TM

    def body(h_ref, t_ref, g_ref, dh_ref, acc_ref):
        i = pl.program_id(0)

        @pl.when(i == 0)
        def _():
            acc_ref[...] = jnp.zeros_like(acc_ref)

        @pl.when(i < n_lat)
        def _():
            hh, gg = h_ref[...], g_ref[...]
            r = lax.rsqrt(jnp.mean(hh * hh, axis=-1, keepdims=True) + EPS)
            xhat = hh * r
            err = xhat * gg - t_ref[...]
            dy = err * (1.0 / D)
            dx = dy * gg
            dh_ref[...] = r * (dx - xhat * jnp.mean(dx * xhat, axis=-1, keepdims=True))
            acc_ref[0:1, :] += _sum0(dy * xhat)
            acc_ref[1:2, :] += _sum0(err * err)

        @pl.when(i >= n_lat)
        def _():
            dh_ref[...] = jnp.zeros_like(dh_ref)

    return pl.pallas_call(
        body, name=name, grid=(s // TM,),
        in_specs=[_rows(D), pl.BlockSpec((TM, D), lambda i: (jnp.minimum(i, n_lat - 1), 0)), _whole((1, D))],
        out_specs=[_rows(D), pl.BlockSpec((8, D), lambda i: (0, 0))],
        out_shape=[jax.ShapeDtypeStruct((s, D), F32), jax.ShapeDtypeStruct((8, D), F32)],
        compiler_params=_params("arbitrary"),
    )(h, target, g)


def mod_rows(c_all, w_mod, b_cols, name):
    def body(c_ref, w_ref, b_ref, o_ref):
        cc = c_ref[...]
        o_ref[...] = _dot((cc * jax.nn.sigmoid(cc)).astype(BF16), w_ref[...].astype(BF16)) + b_ref[...]

    return pl.pallas_call(
        body, name=name, grid=(2,),
        in_specs=[pl.BlockSpec((16, D), lambda l: (0, 0)), pl.BlockSpec((None, D, MOD_COLS), lambda l: (l, 0, 0)),
                  pl.BlockSpec((None, 1, MOD_COLS), lambda l: (l, 0, 0))],
        out_specs=pl.BlockSpec((None, 16, MOD_COLS), lambda l: (l, 0, 0)),
        out_shape=jax.ShapeDtypeStruct((2, 16, MOD_COLS), F32),
        compiler_params=_params("parallel"),
    )(c_all, w_mod, b_cols)


def mod_grads(c_all, dmod_cols, w_mod, name):
    def body(c_ref, d_ref, w_ref, dw_ref, dc_ref):
        @pl.when(pl.program_id(0) == 0)
        def _():
            dc_ref[...] = jnp.zeros_like(dc_ref)

        cc = c_ref[...]
        dd = d_ref[...].astype(BF16)
        dw_ref[...] = _dotg((cc * jax.nn.sigmoid(cc)).astype(BF16), dd, TN)
        dc_ref[...] += _dotg(dd, w_ref[...].astype(BF16), NT)

    return pl.pallas_call(
        body, name=name, grid=(2,),
        in_specs=[pl.BlockSpec((16, D), lambda l: (0, 0)), pl.BlockSpec((None, 16, MOD_COLS), lambda l: (l, 0, 0)),
                  pl.BlockSpec((None, D, MOD_COLS), lambda l: (l, 0, 0))],
        out_specs=[pl.BlockSpec((None, D, MOD_COLS), lambda l: (l, 0, 0)), pl.BlockSpec((16, D), lambda l: (0, 0))],
        out_shape=[jax.ShapeDtypeStruct((2, D, MOD_COLS), F32), jax.ShapeDtypeStruct((16, D), F32)],
        compiler_params=_params("arbitrary"),
    )(c_all, dmod_cols, w_mod)


def _row_tile(rows, cols, n_arrays):
    budget = VMEM_LIMIT_BYTES // 4 // (2 * 4 * n_arrays * cols)
    best = None
    for tr in range(16, rows + 1, 16):
        if rows % tr == 0 and tr <= budget:
            best = tr
    return best if best is not None else rows


def elementwise(fn, ins, out_dtypes, name, ex=None):
    rows, cols = ins[0].shape
    tr = _row_tile(rows, cols, len(ins) + len(out_dtypes))

    def body(*refs):
        outs = fn(*[r[...] for r in refs[:len(ins)]])
        for o_ref, o in zip(refs[len(ins):], outs):
            o_ref[...] = o.astype(o_ref.dtype)

    spec = pl.BlockSpec((tr, cols), lambda i: (i, 0))
    outs, got = _grid_call(body, name, rows // tr, [spec] * len(ins), [spec] * len(out_dtypes),
                           [jax.ShapeDtypeStruct((rows, cols), dt) for dt in out_dtypes], ins, "parallel", ex)
    return outs if ex is None else (outs, got)


def _adamw_tile(w, g, m, v):
    m = ADAM_B1 * m + (1.0 - ADAM_B1) * g
    v = ADAM_B2 * v + (1.0 - ADAM_B2) * (g * g)
    m_hat = m / (1.0 - ADAM_B1 ** ADAM_STEP)
    v_hat = v / (1.0 - ADAM_B2 ** ADAM_STEP)
    return -ADAM_LR * (m_hat / (jnp.sqrt(v_hat) + ADAM_EPS) + ADAM_WD * w), m, v


def adamw(w, g, m, v, name, ex=None):
    shape = w.shape
    two_d = (-1, shape[-1]) if w.ndim > 1 else (1, -1)
    outs = elementwise(_adamw_tile, [a.reshape(two_d) for a in (w, g, m, v)], [F32] * 3, name, ex)
    outs, got = outs if ex is not None else (outs, None)
    outs = [o.reshape(shape) for o in outs]
    return outs if ex is None else (outs, got)


def _prefetch_call(body, name, grid, in_specs, out_specs, out_shape, place, args, ex=None):
    if ex is None:
        spec = pltpu.PrefetchScalarGridSpec(num_scalar_prefetch=1, grid=grid, in_specs=in_specs, out_specs=out_specs)
        return pl.pallas_call(body, name=name, grid_spec=spec, out_shape=out_shape,
                              compiler_params=_params(*["parallel"] * len(grid)))(place, *args)
    n_in, n_out, ci, co = len(in_specs), len(out_specs), len(ex["ins"]), len(ex["out_shape"])
    spec = pltpu.PrefetchScalarGridSpec(num_scalar_prefetch=1, grid=grid, in_specs=list(in_specs) + _any(ci),
                                        out_specs=list(out_specs) + _any(co), scratch_shapes=ex["scratch"])
    outs = pl.pallas_call(
        _carrying(body, grid, n_in, n_out, ex, lead=1), name=name, grid_spec=spec, out_shape=list(out_shape) + ex["out_shape"],
        input_output_aliases={1 + n_in + i: n_out + j for i, j in ex["aliases"].items()},
        compiler_params=_params(*["arbitrary"] * len(grid)))(place, *args, *ex["ins"])
    return outs[:n_out], outs[n_out:]


def cast_place(w, layer, place, name):
    _, r, c = w.shape
    tr = _row_tile(r, c, 2)

    def body(p_ref, w_ref, o_ref):
        o_ref[...] = w_ref[...].astype(BF16)

    return _prefetch_call(
        body, name, (r // tr,), [pl.BlockSpec((None, tr, c), lambda i, p: (layer, i, 0))],
        pl.BlockSpec((None, tr, c), lambda i, p: (p[1], i, 0)), jax.ShapeDtypeStruct((N_SLOT, r, c), BF16), place, [w])


def pair_sum(g32, got, place, name, ex=None):
    n_slot, rh, c = got.shape
    tr = _row_tile(rh, c, 4)
    per = rh // tr

    def body(p_ref, a_ref, b_ref, o16_ref):
        o16_ref[...] = (a_ref[...] + b_ref[...].astype(F32)).astype(BF16)

    half = pl.BlockSpec((None, tr, c), lambda s, i, p: (s, i, 0))
    out = _prefetch_call(
        body, name, (n_slot, per), [pl.BlockSpec((None, tr, c), lambda s, i, p: (s, p[0] * per + i, 0)), half], [half],
        [jax.ShapeDtypeStruct(got.shape, BF16)], place, [g32, got], ex)
    return [(g32, got), out[0]] if ex is None else ([(g32, got), out[0][0]], out[1])


def chip_sum(terms, got, place, name):
    g32, sent = terms
    _, rh, c = sent.shape
    tr = _row_tile(rh, c, 6)
    per = rh // tr

    def body(p_ref, a_ref, b_ref, r0_ref, r1_ref, r2_ref, o_ref):
        own = a_ref[...] + b_ref[...].astype(F32)
        o_ref[...] = own + r0_ref[...].astype(F32) + r1_ref[...].astype(F32) + r2_ref[...].astype(F32)

    part = pl.BlockSpec((tr, c), lambda i, p: (i, 0))
    return _prefetch_call(
        body, name, (per,),
        [pl.BlockSpec((None, tr, c), lambda i, p: (p[1], p[0] * per + i, 0)), pl.BlockSpec((None, tr, c), lambda i, p: (p[1], i, 0)),
         part, part, part],
        pl.BlockSpec((tr, c), lambda i, p: (p[0] * per + i, 0)), jax.ShapeDtypeStruct((2 * rh, c), F32), place, [g32, sent, *got])


def adamw_layers(w, g0, g1, m, v, name, ex=None):
    _, r, c = w.shape
    tr = _row_tile(r, c, 10)

    def body(w_ref, g0_ref, g1_ref, m_ref, v_ref, g_ref, d_ref, mo_ref, vo_ref):
        g = jnp.where(pl.program_id(0) == 0, g0_ref[...], g1_ref[...])
        g_ref[...] = g
        d_ref[...], mo_ref[...], vo_ref[...] = _adamw_tile(w_ref[...], g, m_ref[...], v_ref[...])

    steps = r // tr
    stacked = pl.BlockSpec((None, tr, c), lambda l, i: (l, i, 0))
    layer0 = pl.BlockSpec((tr, c), lambda l, i: (jnp.where(l == 0, i, steps - 1), 0))
    layer1 = pl.BlockSpec((tr, c), lambda l, i: (jnp.where(l == 0, 0, i), 0))
    outs, got = _grid_call(body, name, (2, steps), [stacked, layer0, layer1, stacked, stacked], [stacked] * 4,
                           [jax.ShapeDtypeStruct(w.shape, F32)] * 4, (w, g0, g1, m, v), "parallel", ex)
    return outs if ex is None else (outs, got)


def sum8(gathered, name):
    def body(*refs):
        n = len(refs) // 2
        for g_ref, o_ref in zip(refs[:n], refs[n:]):
            acc = g_ref[0]
            for dev in range(1, N_DEV):
                acc = acc + g_ref[dev]
            o_ref[...] = acc

    return pl.pallas_call(
        body, name=name,
        out_shape=[jax.ShapeDtypeStruct(a.shape[1:], F32) for a in gathered],
        compiler_params=_params(),
    )(*gathered)


PHASES = ("start", "late", "finish")


def _place():
    return lax.axis_index("x"), lax.axis_index("y"), lax.axis_index("c")


def _any(n):
    return [pl.BlockSpec(memory_space=pl.ANY)] * n


def gather8_exchange(blocks):
    n = len(blocks)

    def copy(outs, sems, ti, k, block, to, src=None):
        dst = outs[ti].at[4 * block[0] + 2 * block[1] + block[2]]
        return pltpu.make_async_remote_copy(src_ref=dst if src is None else src, dst_ref=dst, send_sem=sems[0].at[ti, k],
                                            recv_sem=sems[1].at[ti, k], device_id=to, device_id_type=MESH)

    def first(ins, outs, sems):
        x, y, c = _place()
        local, sent = [], []
        for ti in range(n):
            local.append(pltpu.make_async_copy(ins[ti], outs[ti].at[4 * x + 2 * y + c], sems[2].at[ti]))
            sent.append(copy(outs, sems, ti, 0, (x, y, c), (x, y, 1 - c), src=ins[ti]))
            sent += [copy(outs, sems, ti, 1 + j, (x, y, c), (*chip, c), src=ins[ti]) for j, chip in enumerate(_three_chips(x, y))]
        return local, sent

    def start(ins, outs, sems):
        local, sent = first(ins, outs, sems)
        for cp in local + sent:
            cp.start()

    def passed_on(outs, sems):
        x, y, c = _place()
        return [copy(outs, sems, ti, 4 + j, (*chip, c), (x, y, 1 - c)) for ti in range(n) for j, chip in enumerate(_three_chips(x, y))]

    def late(ins, outs, sems):
        x, y, c = _place()
        on = passed_on(outs, sems)
        for ti in range(n):
            for j, chip in enumerate(_three_chips(x, y)):
                copy(outs, sems, ti, 1 + j, (*chip, c), (x, y, c)).wait_recv()
                on[3 * ti + j].start()

    def finish(ins, outs, sems):
        x, y, c = _place()
        me, sibling = (x, y, c), (x, y, 1 - c)
        local, sent = first(ins, outs, sems)
        for ti in range(n):
            copy(outs, sems, ti, 0, sibling, me).wait_recv()
            for j, chip in enumerate(_three_chips(x, y)):
                copy(outs, sems, ti, 4 + j, (*chip, 1 - c), me).wait_recv()
        for cp in sent + passed_on(outs, sems):
            cp.wait_send()
        for cp in local:
            cp.wait()

    return dict(ins=list(blocks), out_shape=[jax.ShapeDtypeStruct((N_DEV,) + b.shape, b.dtype) for b in blocks], aliases={},
                start=start, late=late, finish=finish,
                scratch=[pltpu.SemaphoreType.DMA((n, 7)), pltpu.SemaphoreType.DMA((n, 7)), pltpu.SemaphoreType.DMA((n,))])


def all_gather(blocks, name):
    return run_exchange(gather8_exchange(blocks), name)


def _three_chips(x, y):
    return [(1 - x, y), (x, 1 - y), (1 - x, 1 - y)]


def gather_exchange(placed):
    n = len(placed)

    def copy(bufs, sems, ti, k, chip, core, to):
        rh = bufs[ti].shape[1] // 2
        half = bufs[ti].at[2 * chip[0] + chip[1], pl.ds(core * rh, rh), :]
        return pltpu.make_async_remote_copy(src_ref=half, dst_ref=half, send_sem=sems[0].at[ti, k], recv_sem=sems[1].at[ti, k],
                                            device_id=to, device_id_type=MESH)

    def sends(bufs, sems):
        x, y, c = _place()
        return [copy(bufs, sems, ti, k, (x, y), c, (*chip, c)) for ti in range(n) for k, chip in enumerate(_three_chips(x, y))]

    def passed_on(bufs, sems):
        x, y, c = _place()
        return [copy(bufs, sems, ti, 3 + k, chip, c, (x, y, 1 - c)) for ti in range(n) for k, chip in enumerate(_three_chips(x, y))]

    def start(ins, bufs, sems):
        for cp in sends(bufs, sems):
            cp.start()

    def late(ins, bufs, sems):
        x, y, c = _place()
        on = passed_on(bufs, sems)
        for ti in range(n):
            for k, chip in enumerate(_three_chips(x, y)):
                copy(bufs, sems, ti, k, chip, c, (x, y, c)).wait_recv()
                on[3 * ti + k].start()

    def finish(ins, bufs, sems):
        x, y, c = _place()
        for ti in range(n):
            for k, chip in enumerate(_three_chips(x, y)):
                copy(bufs, sems, ti, 3 + k, chip, 1 - c, (x, y, c)).wait_recv()
        for cp in sends(bufs, sems) + passed_on(bufs, sems):
            cp.wait_send()

    return dict(ins=list(placed), out_shape=[jax.ShapeDtypeStruct(w.shape, w.dtype) for w in placed],
                aliases={i: i for i in range(n)}, start=start, late=late, finish=finish,
                scratch=[pltpu.SemaphoreType.DMA((n, 6)), pltpu.SemaphoreType.DMA((n, 6))])


def scatter_exchange(p16):
    n = len(p16)

    def copies(ins, got, sems):
        x, y, c = _place()
        return [pltpu.make_async_remote_copy(src_ref=ins[ti].at[2 * chip[0] + chip[1]], dst_ref=got[3 * ti + k],
                                             send_sem=sems[0].at[ti, k], recv_sem=sems[1].at[ti, k], device_id=(*chip, c),
                                             device_id_type=MESH)
                for ti in range(n) for k, chip in enumerate(_three_chips(x, y))]

    def start(ins, got, sems):
        for cp in copies(ins, got, sems):
            cp.start()

    def finish(ins, got, sems):
        for cp in copies(ins, got, sems):
            cp.wait()

    return dict(ins=list(p16), out_shape=[jax.ShapeDtypeStruct(a.shape[1:], BF16) for a in p16 for _ in range(3)], aliases={},
                start=start, finish=finish, scratch=[pltpu.SemaphoreType.DMA((n, 3)), pltpu.SemaphoreType.DMA((n, 3))])


def run_exchange(ex, name):
    ci, co = len(ex["ins"]), len(ex["out_shape"])

    def body(*refs):
        ins, outs, sems = refs[:ci], refs[ci:ci + co], refs[ci + co:]
        for phase in PHASES:
            if phase in ex:
                ex[phase](ins, outs, sems)

    return pl.pallas_call(body, name=name, in_specs=_any(ci), out_specs=_any(co), out_shape=ex["out_shape"],
                          input_output_aliases=ex["aliases"], scratch_shapes=ex["scratch"])(*ex["ins"])


def _carrying(body, grid, n_in, n_out, ex, lead=0):
    ci, co = len(ex["ins"]), len(ex["out_shape"])
    first, last = (0,) * len(grid), tuple(g - 1 for g in grid)
    steps = dict(start=first, late=(grid[0] - 2,) if len(grid) == 1 and grid[0] > 2 else last, finish=last)

    def at(ids):
        return functools.reduce(jnp.logical_and, [pl.program_id(ax) == v for ax, v in enumerate(ids)])

    def carrying(*refs):
        head, refs = refs[:lead], refs[lead:]
        c_in, c_out = refs[n_in:n_in + ci], refs[n_in + ci + n_out:n_in + ci + n_out + co]
        sems = refs[n_in + ci + n_out + co:]
        for phase in PHASES:
            if phase == "finish":
                body(*head, *refs[:n_in], *refs[n_in + ci:n_in + ci + n_out])
            if phase in ex:
                pl.when(at(steps[phase]))(functools.partial(ex[phase], c_in, c_out, sems))

    return carrying


def _grid_call(body, name, grid, in_specs, out_specs, out_shape, args, sem, ex=None):
    grid = (grid,) if isinstance(grid, int) else tuple(grid)
    sems_of = (sem,) * len(grid) if isinstance(sem, str) else tuple(sem)
    n_in, n_out = len(in_specs), len(out_specs)
    if ex is None:
        return pl.pallas_call(body, name=name, grid=grid, in_specs=in_specs, out_specs=out_specs, out_shape=out_shape,
                              compiler_params=_params(*sems_of))(*args), []
    ci, co = len(ex["ins"]), len(ex["out_shape"])
    outs = pl.pallas_call(
        _carrying(body, grid, n_in, n_out, ex), name=name, grid=grid, in_specs=list(in_specs) + _any(ci),
        out_specs=list(out_specs) + _any(co), out_shape=list(out_shape) + ex["out_shape"], scratch_shapes=ex["scratch"],
        input_output_aliases={n_in + i: n_out + j for i, j in ex["aliases"].items()},
        compiler_params=_params(*["arbitrary"] * len(grid)),
    )(*args, *ex["ins"])
    return outs[:n_out], outs[n_out:]


def both(*exchanges):
    exchanges = [ex for ex in exchanges if ex is not None]
    if len(exchanges) < 2:
        return exchanges[0] if exchanges else None
    n_ins = [len(ex["ins"]) for ex in exchanges]
    n_outs = [len(ex["out_shape"]) for ex in exchanges]
    n_sems = [len(ex["scratch"]) for ex in exchanges]

    def parts(seq, counts, k):
        first = sum(counts[:k])
        return seq[first:first + counts[k]]

    def run(phase):
        def go(ins, outs, sems):
            for k, ex in enumerate(exchanges):
                if phase in ex:
                    ex[phase](parts(ins, n_ins, k), parts(outs, n_outs, k), parts(sems, n_sems, k))
        return go

    aliases = {sum(n_ins[:k]) + i: sum(n_outs[:k]) + j for k, ex in enumerate(exchanges) for i, j in ex["aliases"].items()}
    return dict(ins=[a for ex in exchanges for a in ex["ins"]], out_shape=[o for ex in exchanges for o in ex["out_shape"]],
                aliases=aliases, scratch=[s for ex in exchanges for s in ex["scratch"]], **{ph: run(ph) for ph in PHASES})


def split_outputs(got, *exchanges):
    got, out = list(got), []
    for ex in exchanges:
        n = len(ex["out_shape"]) if ex is not None else 0
        out.append(got[:n])
        got = got[n:]
    return out


def pair_exchange(g16):
    n = len(g16)

    def copies(a16, got, sems):
        x, y, c = _place()
        out = []
        for ti in range(n):
            rh = a16[ti].shape[1] // 2
            out.append(pltpu.make_async_remote_copy(
                src_ref=a16[ti].at[:, pl.ds((1 - c) * rh, rh), :], dst_ref=got[ti], send_sem=sems[0].at[ti],
                recv_sem=sems[1].at[ti], device_id=(x, y, 1 - c), device_id_type=MESH))
        return out

    def start(a16, got, sems):
        for cp in copies(a16, got, sems):
            cp.start()

    def finish(a16, got, sems):
        for cp in copies(a16, got, sems):
            cp.wait()

    return dict(ins=list(g16), out_shape=[jax.ShapeDtypeStruct((a.shape[0], a.shape[1] // 2, a.shape[2]), BF16) for a in g16],
                aliases={}, start=start, finish=finish, scratch=[pltpu.SemaphoreType.DMA((n,)), pltpu.SemaphoreType.DMA((n,))])


def _gather_half(buf, chip, core):
    rh = buf.shape[1] // 2
    return buf.at[2 * chip[0] + chip[1], pl.ds(core * rh, rh), :]


def gather_start(placed, name):
    n = len(placed)
    hbm, sem = pl.BlockSpec(memory_space=pltpu.HBM), pl.BlockSpec(memory_space=pltpu.SEMAPHORE)

    def body(*refs):
        bufs, send_sems, recv_sems, token_ref = refs[:n], refs[n], refs[n + 1], refs[-1]
        x, y, c = _place()
        for ti in range(n):
            for k, chip in enumerate(_three_chips(x, y)):
                half = _gather_half(bufs[ti], (x, y), c)
                pltpu.make_async_remote_copy(src_ref=half, dst_ref=half, send_sem=send_sems.at[3 * ti + k],
                                             recv_sem=recv_sems.at[3 * ti + k], device_id=(*chip, c), device_id_type=MESH).start()
        token_ref[...] = jnp.zeros_like(token_ref)

    return pl.pallas_call(
        body, name=name,
        out_shape=(pltpu.SemaphoreType.DMA((3 * n,)), pltpu.SemaphoreType.DMA((3 * n,)), *[pltpu.HBM(w.shape, w.dtype) for w in placed],
                   jax.ShapeDtypeStruct((8, BLK), F32)),
        in_specs=(hbm,) * n, out_specs=(sem, sem, *(hbm,) * n, pl.BlockSpec(memory_space=pltpu.VMEM)),
        input_output_aliases={i: 2 + i for i in range(n)},
        compiler_params=pltpu.CompilerParams(has_side_effects=pltpu.SideEffectType.DATAFLOW_SIDE_EFFECTING),
    )(*[pltpu.with_memory_space_constraint(w, pltpu.HBM) for w in placed])


def gather_wait(send_sems, recv_sems, bufs, after, name):
    n = len(bufs)
    hbm, sem = pl.BlockSpec(memory_space=pltpu.HBM), pl.BlockSpec(memory_space=pltpu.SEMAPHORE)

    def body(*refs):
        bufs, send_sems, recv_sems = refs[:n], refs[n], refs[n + 1]
        x, y, c = _place()
        for ti in range(n):
            for k, chip in enumerate(_three_chips(x, y)):
                mine, theirs = _gather_half(bufs[ti], (x, y), c), _gather_half(bufs[ti], chip, c)
                cp = pltpu.make_async_remote_copy(src_ref=mine, dst_ref=theirs, send_sem=send_sems.at[3 * ti + k],
                                                  recv_sem=recv_sems.at[3 * ti + k], device_id=(*chip, c), device_id_type=MESH)
                cp.wait_send()
                cp.wait_recv()

    return pl.pallas_call(
        body, name=name, out_shape=tuple(pltpu.HBM(w.shape, w.dtype) for w in bufs),
        in_specs=(*(hbm,) * n, sem, sem, *_any(len(after))), out_specs=(hbm,) * n,
        input_output_aliases={i: i for i in range(n)},
        compiler_params=pltpu.CompilerParams(has_side_effects=pltpu.SideEffectType.DATAFLOW_SIDE_EFFECTING),
    )(*bufs, send_sems, recv_sems, *after)


def pass_on_exchange(bufs):
    n = len(bufs)

    def copies(refs, sems, core):
        x, y, c = _place()
        return [pltpu.make_async_remote_copy(src_ref=_gather_half(refs[ti], chip, c if core == "mine" else 1 - c),
                                             dst_ref=_gather_half(refs[ti], chip, c if core == "mine" else 1 - c),
                                             send_sem=sems[0].at[ti, k], recv_sem=sems[1].at[ti, k], device_id=(x, y, 1 - c),
                                             device_id_type=MESH)
                for ti in range(n) for k, chip in enumerate(_three_chips(x, y))]

    def start(ins, refs, sems):
        for cp in copies(refs, sems, "mine"):
            cp.start()

    def finish(ins, refs, sems):
        for cp in copies(refs, sems, "mine"):
            cp.wait_send()
        for cp in copies(refs, sems, "sibling's"):
            cp.wait_recv()

    return dict(ins=list(bufs), out_shape=[jax.ShapeDtypeStruct(w.shape, w.dtype) for w in bufs], aliases={i: i for i in range(n)},
                start=start, finish=finish, scratch=[pltpu.SemaphoreType.DMA((n, 3)), pltpu.SemaphoreType.DMA((n, 3))])


def _scatter_copies(src_ref, lands, send_sems, recv_sems):
    x, y, c = _place()
    return [pltpu.make_async_remote_copy(src_ref=src_ref.at[2 * chip[0] + chip[1]], dst_ref=lands[k], send_sem=send_sems.at[k],
                                         recv_sem=recv_sems.at[k], device_id=(*chip, c), device_id_type=MESH)
            for k, chip in enumerate(_three_chips(x, y))]


def scatter_start(p16, name):
    hbm, sem = pl.BlockSpec(memory_space=pltpu.HBM), pl.BlockSpec(memory_space=pltpu.SEMAPHORE)

    def body(src_ref, l0_ref, l1_ref, l2_ref, send_sems, recv_sems, src_thru, o0_ref, o1_ref, o2_ref, token_ref):
        for cp in _scatter_copies(src_ref, (l0_ref, l1_ref, l2_ref), send_sems, recv_sems):
            cp.start()
        token_ref[...] = jnp.zeros_like(token_ref)

    land = [pltpu.with_memory_space_constraint(lax.empty(p16.shape[1:], BF16), pltpu.HBM) for _ in range(3)]
    return pl.pallas_call(
        body, name=name,
        out_shape=(pltpu.SemaphoreType.DMA((3,)), pltpu.SemaphoreType.DMA((3,)), pltpu.HBM(p16.shape, BF16),
                   *[pltpu.HBM(p16.shape[1:], BF16)] * 3, jax.ShapeDtypeStruct((8, BLK), F32)),
        in_specs=(hbm,) * 4, out_specs=(sem, sem, hbm, hbm, hbm, hbm, pl.BlockSpec(memory_space=pltpu.VMEM)),
        input_output_aliases={0: 2, 1: 3, 2: 4, 3: 5},
        compiler_params=pltpu.CompilerParams(has_side_effects=pltpu.SideEffectType.DATAFLOW_SIDE_EFFECTING),
    )(pltpu.with_memory_space_constraint(p16, pltpu.HBM), *land)


def scatter_wait(send_sems, recv_sems, src_thru, lands, after, name):
    hbm, sem = pl.BlockSpec(memory_space=pltpu.HBM), pl.BlockSpec(memory_space=pltpu.SEMAPHORE)

    def body(src_ref, l0_ref, l1_ref, l2_ref, send_sems, recv_sems, *rest):
        for cp in _scatter_copies(src_ref, (l0_ref, l1_ref, l2_ref), send_sems, recv_sems):
            cp.wait_send()
            cp.wait_recv()

    return pl.pallas_call(
        body, name=name, out_shape=(pltpu.HBM(src_thru.shape, BF16), *[pltpu.HBM(lands[0].shape, BF16)] * 3),
        in_specs=(hbm, hbm, hbm, hbm, sem, sem, *_any(len(after))), out_specs=(hbm,) * 4,
        input_output_aliases={0: 0, 1: 1, 2: 2, 3: 3},
        compiler_params=pltpu.CompilerParams(has_side_effects=pltpu.SideEffectType.DATAFLOW_SIDE_EFFECTING),
    )(src_thru, *lands, send_sems, recv_sems, *after)[1:]


def pair_fill_exchange(halves):
    n = len(halves)

    def copies(bufs, sems, core):
        x, y, c = _place()
        out = []
        for ti in range(n):
            rh = bufs[ti].shape[0] // 2
            rows = bufs[ti].at[pl.ds((c if core == "mine" else 1 - c) * rh, rh), :]
            out.append(pltpu.make_async_remote_copy(src_ref=rows, dst_ref=rows, send_sem=sems[0].at[ti], recv_sem=sems[1].at[ti],
                                                    device_id=(x, y, 1 - c), device_id_type=MESH))
        return out

    def start(ins, bufs, sems):
        for cp in copies(bufs, sems, "mine"):
            cp.start()

    def finish(ins, bufs, sems):
        for cp in copies(bufs, sems, "mine"):
            cp.wait_send()
        for cp in copies(bufs, sems, "sibling's"):
            cp.wait_recv()

    return dict(ins=list(halves), out_shape=[jax.ShapeDtypeStruct(a.shape, a.dtype) for a in halves],
                aliases={i: i for i in range(n)}, start=start, finish=finish,
                scratch=[pltpu.SemaphoreType.DMA((n,)), pltpu.SemaphoreType.DMA((n,))])


def pair_gather(halves, name):
    return run_exchange(pair_fill_exchange(halves), name)


def reduce_small(dm_f1, dm_mix, dm_gate, dm_f2, loss_blk, name):
    def body(f1_ref, mix_ref, gate_ref, f2_ref, l_ref, tot_ref, rows_ref, fin_ref):
        rows_ref[...] = jnp.zeros_like(rows_ref)
        tot_ref[...] = jnp.zeros_like(tot_ref)
        mod_src = [(f1_ref, 0), (f1_ref, 1), (f1_ref, 2), (mix_ref, 0), (mix_ref, 1), (gate_ref, 2),
                   (f2_ref, 0), (f2_ref, 1), (f2_ref, 2)]
        norm_src = [(f1_ref, 3), (mix_ref, 3), (f2_ref, 3)]
        for l in range(2):
            for k, (ref, r) in enumerate(mod_src + norm_src):
                lat = ref[0, l, 0, r:r + 1, :]
                ctx = ref[0, l, 1, r:r + 1, :]
                for dev in range(N_DEV):
                    if dev:
                        lat = lat + ref[dev, l, 0, r:r + 1, :]
                        ctx = ctx + ref[dev, l, 1, r:r + 1, :]
                    if k < N_MOD:
                        rows_ref[l, dev, k:k + 1, :] = ref[dev, l, 0, r:r + 1, :]
                if k < N_MOD:
                    rows_ref[l, N_DEV, k:k + 1, :] = ctx
                tot_ref[l, k:k + 1, :] = lat + ctx
        acc = l_ref[0]
        for dev in range(1, N_DEV):
            acc = acc + l_ref[dev]
        loss = (0.5 / D) * jnp.sum(acc[1:2, :], axis=1, keepdims=True)
        row = lax.broadcasted_iota(jnp.int32, (8, D), 0)
        fin_ref[...] = jnp.where(row == 0, acc[0:1, :], loss)

    return pl.pallas_call(
        body, name=name,
        out_shape=[jax.ShapeDtypeStruct((2, 16, D), F32), jax.ShapeDtypeStruct((2, 16, 16, D), F32),
                   jax.ShapeDtypeStruct((8, D), F32)],
        compiler_params=_params(),
    )(dm_f1, dm_mix, dm_gate, dm_f2, loss_blk)


def rope_tables(t, s):
    rows = t // GRID_W
    row = jnp.repeat(jnp.arange(rows), GRID_W).astype(F32)
    col = jnp.tile(jnp.arange(GRID_W), rows).astype(F32)
    inv = ROPE_BASE ** (-jnp.arange(0, HEAD // 2, 2, dtype=F32) / (HEAD // 2))
    ang = jnp.concatenate([row[:, None] * inv, col[:, None] * inv], axis=-1)
    cos, sin = jnp.cos(ang), jnp.sin(ang)
    cos = jnp.concatenate([jnp.tile(cos, (1, 4)), jnp.ones((s - t, BLK), F32)], axis=0)
    sin = jnp.concatenate([jnp.tile(jnp.concatenate([-sin, sin], axis=1), (1, 2)), jnp.zeros((s - t, BLK), F32)], axis=0)
    return cos, sin


BIG = ("ffn1_in", "ffn1_out", "w_in", "w_out", "ffn2_in", "ffn2_out")
GROUPS = dict(ffn1=("ffn1_in", "ffn1_out"), mix=("w_in", "w_out"), ffn2=("ffn2_in", "ffn2_out"))
GATHER_BEHIND = {("ffn1", 0): [("w_in", 0), ("ffn2_out", 0), ("ffn1_out", 1)], ("proj", 0): [("w_out", 0)],
                 ("mix", 0): [("ffn2_in", 0)], ("ffn2", 0): [("ffn1_in", 1), ("w_in", 1)],
                 ("ffn1", 1): [("ffn2_in", 1), ("w_out", 1)], ("mix", 1): [("ffn2_out", 1)]}


def _slot_major(name, g):
    if name == "w_in":
        return jnp.stack(jnp.split(g, N_SLOT, axis=1), axis=0)
    if name in ("ffn1_in", "ffn2_in"):
        return g
    return g.reshape(N_SLOT, g.shape[0] // N_SLOT, g.shape[1])


def _whole_weight(name, buf):
    if name == "w_in":
        return buf.transpose(1, 0, 2).reshape(D, PROJ_W)
    if name in ("ffn1_in", "ffn2_in"):
        return buf
    return buf.reshape(-1, buf.shape[2])


def local_step(x1, ctx1, target, mods, norms, nfinal, placed, w_pool, pool_scale, sink, place, small_blocks):
    t, s = x1.shape[0], x1.shape[0] + ctx1.shape[0]
    n_lat = t // TM
    cos, sin = rope_tables(t, s)
    tables = mix_tables(t, s)
    wts ={name: list(pair) for name, pair in placed.items()}

    def gather(tensors):
        return gather_exchange([wts[name][l] for name, l in tensors])

    def gathered(tensors, arrays):
        for (name, l), whole in zip(tensors, arrays):
            wts[name][l] = whole

    def weight(name, l):
        return _whole_weight(name, wts[name][l])

    def fwd_ex(grp, l):
        groups = GATHER_BEHIND.get((grp, l))
        return (groups, gather(groups)) if groups else (None, None)

    h = jnp.concatenate([x1, ctx1], axis=0)
    saved = []
    for l in range(2):
        h0 = h
        groups, ex = fwd_ex("ffn1", l)
        (h1, ab1, f1), got = ffn_fwd(h0, mods, norms[0], weight("ffn1_in", l), weight("ffn1_out", l), l, 0, n_lat, f"ffn1_fwd_{l}", ex)
        gathered(groups or [], got)
        groups, ex = fwd_ex("proj", l)
        (u, q, k, v), got = proj_fwd(h1, mods, norms[1], weight("w_in", l), cos, sin, l, n_lat, f"proj_fwd_{l}", ex)
        gathered(groups or [], got)
        groups, ex = fwd_ex("mix", l)
        (h2, cat, lse, mo), got = mix_fwd(h1, q, k, v, u, w_pool, pool_scale, sink, weight("w_out", l), mods, tables, l, t,
                                          f"mix_fwd_{l}", ex)
        gathered(groups or [], got)
        groups, ex = fwd_ex("ffn2", l)
        (h, ab2, f2), got = ffn_fwd(h2, mods, norms[2], weight("ffn2_in", l), weight("ffn2_out", l), l, 6, n_lat, f"ffn2_fwd_{l}", ex)
        gathered(groups or [], got)
        saved.append((h0, ab1, f1, h1, u, q, k, v, cat, lse, mo, h2, ab2, f2))
    dh, loss_blk = loss_head(h, target, nfinal, t, "loss_head")

    halves = {name: [None, None] for name in BIG}
    pending = []

    def summed_in_pair(grp, l, name_a, g_a, name_b, wgrad_b):
        g_b, got_a = wgrad_b(pair_exchange([_slot_major(name_a, g_a[1])]))
        sum_a, got_b = pair_sum(_slot_major(name_a, g_a[0]), got_a[0], place, f"pair_sum_{name_a}_{l}",
                                pair_exchange([_slot_major(name_b, g_b[1])]))
        sums = {name_a: sum_a, name_b: pair_sum(_slot_major(name_b, g_b[0]), got_b[0], place, f"pair_sum_{name_b}_{l}")}
        pending.append((grp, l, [sums[n] for n in GROUPS[grp]]))

    lacking = []

    def riders():
        return (scatter_exchange([p16 for _, p16 in pending[0][2]]) if pending else None,
                pair_fill_exchange([halves[name][l] for name, l in lacking]) if lacking else None)

    def carried(got, exs):
        got, filled = split_outputs(got, *exs)
        for (name, l), whole in zip(list(lacking), filled):
            halves[name][l] = whole
            lacking.remove((name, l))
        if pending:
            grp, l, pairs = pending.pop(0)
            for i, name in enumerate(GROUPS[grp]):
                halves[name][l] = chip_sum(pairs[i][0], got[3 * i:3 * i + 3], place, f"chip_sum_{name}_{l}")
                lacking.append((name, l))

    small = [None, None]
    for l in (1, 0):
        h0, ab1, f1, h1, u, q, k, v, cat, lse, mo, h2, ab2, f2 = saved[l]
        exs = riders()
        (dh, dab, df, n, act, dm_f2), got = ffn_bwd(h2, ab2, f2, dh, mods, norms[2], weight("ffn2_in", l), weight("ffn2_out", l),
                                                    l, 6, n_lat, f"ffn2_bwd_{l}", both(*exs))
        carried(got, exs)
        g_in, _ = wgrad(n, dab, D // 2, FF_COLS, FF_COLS,f"ffn2_in_wgrad_{l}")
        summed_in_pair("ffn2", l, "ffn2_in", g_in, "ffn2_out",
                       lambda ex, a=act, b=df: wgrad(a, b, D_FF // 2, D // 2, None,f"ffn2_out_wgrad_{l}", ex))
        exs = riders()
        (dq, dk, dv, du, dmo, dwp, dps, dsink, dm_gate), got = mix_bwd(
            dh, mo, q, k, v, u, lse, w_pool, pool_scale, sink, weight("w_out", l), mods, tables, l, t, f"mix_bwd_{l}", both(*exs))
        carried(got, exs)
        g_wo, _ = wgrad(cat, dmo, POOL_W + ATTN_W, D, None, f"w_out_wgrad_{l}")
        dh, dp, n, dm_mix = proj_bwd(h1, du, dq, dk, dv, dh, mods, norms[1], weight("w_in", l), cos, sin, l, n_lat, f"proj_bwd_{l}")
        summed_in_pair("mix", l, "w_out", g_wo, "w_in",
                       lambda ex, a=n, b=dp: wgrad(a, b, D, PROJ_W // 2, None, f"w_in_wgrad_{l}", ex))
        exs = riders()
        (dh, dab, df, n, act, dm_f1), got = ffn_bwd(h0, ab1, f1, dh, mods, norms[0], weight("ffn1_in", l), weight("ffn1_out", l),
                                                    l, 0, n_lat, f"ffn1_bwd_{l}", both(*exs))
        carried(got, exs)
        small[l] = dict(dm_f1=dm_f1, dm_mix=dm_mix, dm_gate=dm_gate, dm_f2=dm_f2, dwp=dwp, dps=dps, dsink=dsink)
        if l:
            g_in, _ = wgrad(n, dab, D // 2, FF_COLS, FF_COLS,f"ffn1_in_wgrad_{l}")
            summed_in_pair("ffn1", l, "ffn1_in", g_in, "ffn1_out",
                           lambda ex, a=act, b=df: wgrad(a, b, D_FF // 2, D // 2, None,f"ffn1_out_wgrad_{l}", ex))
    g_out, _ = wgrad(act, df, D_FF // 2, D // 2, None, "ffn1_out_wgrad_0")
    got = run_exchange(pair_exchange([_slot_major("ffn1_out", g_out[1])]), "pair_exchange_ffn1_out_0")
    p32, p16 = pair_sum(_slot_major("ffn1_out", g_out[0]), got[0], place, "pair_sum_ffn1_out_0")
    riding = (gather8_exchange(small_blocks(small, loss_blk)), scatter_exchange([p16]),
              pair_fill_exchange([halves[name][l] for name, l in lacking]))
    g_in, got = wgrad(n, dab, D // 2, FF_COLS, FF_COLS,"ffn1_in_wgrad_0", both(*riding))
    small_all, got, filled = split_outputs(got, *riding)
    for (name, l), whole in zip(lacking, filled):
        halves[name][l] = whole
    (halves["ffn1_out"][0],) = pair_gather([chip_sum(p32, got, place, "chip_sum_ffn1_out_0")], "pair_gather_ffn1_out_0")
    got = run_exchange(pair_exchange([_slot_major("ffn1_in", g_in[1])]), "pair_exchange_ffn1_in_0")
    return dh[:t], halves, pair_sum(_slot_major("ffn1_in", g_in[0]), got[0], place, "pair_sum_ffn1_in_0"), small_all


def _silu_grad(z):
    sg = jax.nn.sigmoid(z)
    return sg * (1 + z * (1 - sg))


def kernel(x, c, ctx, c_ctx, w_mod, b_mod, norm_ffn1, w_ffn1_in, w_ffn1_out, norm_mix, w_in, w_pool, pool_scale, sink, w_out, norm_ffn2, w_ffn2_in, w_ffn2_out, norm_final, loss_target, m_c_ctx, m_w_mod, m_b_mod, m_norm_ffn1, m_w_ffn1_in, m_w_ffn1_out, m_norm_mix, m_w_in, m_w_pool, m_pool_scale, m_sink, m_w_out, m_norm_ffn2, m_w_ffn2_in, m_w_ffn2_out, m_norm_final, v_c_ctx, v_w_mod, v_b_mod, v_norm_ffn1, v_w_ffn1_in, v_w_ffn1_out, v_norm_mix, v_w_in, v_w_pool, v_pool_scale, v_sink, v_w_out, v_norm_ffn2, v_w_ffn2_in, v_w_ffn2_out, v_norm_final):
    px, py, pc = _place()
    slot, me = 2 * px + py, 4 * px + 2 * py + pc
    n_grp = len(POOL_WINDOWS)

    (c_rows,) = all_gather([c.reshape(8, D // 8)], "gather_c")
    c_all = jnp.concatenate([c_rows.reshape(N_DEV, D), c_ctx.reshape(1, D), jnp.zeros((16 - N_DEV - 1, D), F32)], axis=0)
    b_cols = lax.dynamic_slice(b_mod, (0, slot * MOD_COLS), (2, MOD_COLS)).reshape(2, 1, MOD_COLS)
    (mod_parts,) = all_gather([mod_rows(c_all, w_mod, b_cols, "mod_rows")], "gather_mods")
    mods_all = mod_parts[0::2].transpose(1, 2, 0, 3).reshape(2, 16, N_MOD * D)
    mx = lax.dynamic_slice(mods_all, (0, me, 0), (2, 1, N_MOD * D)).reshape(2, N_MOD, D)
    mc = mods_all[:, N_DEV].reshape(2, N_MOD, D)
    pad = jnp.zeros((2, 16 - N_MOD, D), F32)
    mods = jnp.stack([jnp.concatenate([mx, pad], axis=1), jnp.concatenate([mc, pad], axis=1)], axis=1)

    place = jnp.stack([pc, slot]).astype(jnp.int32)
    shards = dict(ffn1_in=w_ffn1_in, ffn1_out=w_ffn1_out, w_in=w_in, w_out=w_out, ffn2_in=w_ffn2_in, ffn2_out=w_ffn2_out)
    first = [("ffn1_in", 0), ("ffn1_out", 0)]
    placed = {name: [None, None] for name in BIG}
    for name, l in first:
        placed[name][l] = cast_place(shards[name], l, place, f"cast_{name}_{l}")
    send_sems, recv_sems, *bufs, token = gather_start([placed[name][l] for name, l in first], "gather_first_start")
    others = [(name, l) for name in BIG for l in range(2) if (name, l) not in first]
    for name, l in others:
        placed[name][l] = cast_place(shards[name], l, place, f"cast_{name}_{l}")
    bufs = gather_wait(send_sems, recv_sems, bufs, [placed[name][l] for name, l in others], "gather_first_wait")
    for (name, l), whole in zip(first, run_exchange(pass_on_exchange(bufs), "gather_first_pass_on")):
        placed[name][l] = whole
    norms = [g.reshape(2, 1, D) for g in (norm_ffn1, norm_mix, norm_ffn2)]
    row_sums = ("dm_f1", "dm_mix", "dm_gate", "dm_f2")

    def small_blocks(small, loss_blk):
        stacked = {k: jnp.stack([small[0][k], small[1][k]]) for k in row_sums + ("dwp", "dps", "dsink")}
        return ([stacked[k].reshape(32, D) for k in row_sums]
                + [stacked["dwp"].reshape(2 * n_grp * GROUP, GROUP), stacked["dps"].reshape(16, POOL_W),
                   stacked["dsink"].reshape(16, BLK), loss_blk])

    dx, halves, last_pair, small_all = local_step(x[0], ctx[0], loss_target[0], mods, norms, norm_final.reshape(1, D), placed,
                                                   w_pool.astype(BF16), pool_scale.reshape(2, 1, POOL_W), sink, place, small_blocks)
    grads = {}

    *g_dm, g_dwp, g_dps, g_dsink, g_loss = small_all
    tot, rows, fin = reduce_small(*[g.reshape(N_DEV, 2, 2, 8, D) for g in g_dm], g_loss, "reduce_small")
    s_dwp, s_dps, s_dsink = sum8([g_dwp, g_dps, g_dsink], "sum_pool_sink")
    grads.update(
        w_pool=s_dwp.reshape(2, n_grp, GROUP, GROUP), pool_scale=s_dps.reshape(2, 8, POOL_W)[:, 0],
        sink=s_dsink.reshape(2, 8, BLK)[:, 0, :N_HEADS], b_mod=tot[:, :N_MOD].reshape(2, N_MOD * D),
        norm_ffn1=tot[:, N_MOD], norm_mix=tot[:, N_MOD + 1], norm_ffn2=tot[:, N_MOD + 2], norm_final=fin[0])
    loss = fin[1, 0]

    dmod_cols = lax.dynamic_slice(rows[:, :, :N_MOD, :].reshape(2, 16, N_MOD * D), (0, 0, slot * MOD_COLS), (2, 16, MOD_COLS))
    grads["w_mod"], dc = mod_grads(c_all, dmod_cols, w_mod, "mod_grads")
    (g_dc,) = all_gather([dc], "gather_dc")
    (s_dc,) = sum8([g_dc], "sum_dc")
    (d_c_ctx,) = elementwise(lambda d, z: (0.5 * d * _silu_grad(z),), [s_dc[N_DEV:N_DEV + 1], c_ctx.reshape(1, D)], [F32], "c_ctx_grad")
    send_sems, recv_sems, src_thru, *lands, token = scatter_start(last_pair[1], "scatter_last_start")
    grads["c_ctx"] = d_c_ctx.reshape(D) + token[0, :1]

    given = dict(c_ctx=(c_ctx, m_c_ctx, v_c_ctx), w_mod=(w_mod, m_w_mod, v_w_mod), b_mod=(b_mod, m_b_mod, v_b_mod),
                 norm_ffn1=(norm_ffn1, m_norm_ffn1, v_norm_ffn1), w_ffn1_in=(w_ffn1_in, m_w_ffn1_in, v_w_ffn1_in),
                 w_ffn1_out=(w_ffn1_out, m_w_ffn1_out, v_w_ffn1_out), norm_mix=(norm_mix, m_norm_mix, v_norm_mix),
                 w_in=(w_in, m_w_in, v_w_in), w_pool=(w_pool, m_w_pool, v_w_pool),
                 pool_scale=(pool_scale, m_pool_scale, v_pool_scale), sink=(sink, m_sink, v_sink), w_out=(w_out, m_w_out, v_w_out),
                 norm_ffn2=(norm_ffn2, m_norm_ffn2, v_norm_ffn2), w_ffn2_in=(w_ffn2_in, m_w_ffn2_in, v_w_ffn2_in),
                 w_ffn2_out=(w_ffn2_out, m_w_ffn2_out, v_w_ffn2_out), norm_final=(norm_final, m_norm_final, v_norm_final))
    shard = {(name, l): halves[name][l] for name in BIG for l in range(2)}

    def update(name):
        w, m, v = given[name]
        if name in BIG or name[2:] in BIG:
            key = name if name in BIG else name[2:]
            return adamw_layers(w, shard[key, 0], shard[key, 1], m, v, f"adamw_{name}")
        return [grads[name], *adamw(w, grads[name], m, v, f"adamw_{name}")]

    done = {name: update(name) for name in given if name != "w_ffn1_in"}
    got = scatter_wait(send_sems, recv_sems, src_thru, lands, [done[name][3] for name in done if name[2:] in BIG or name in BIG]
                       + [done["w_mod"][3]], "scatter_last_wait")
    (shard["ffn1_in", 0],) = pair_gather([chip_sum(last_pair[0], got, place, "chip_sum_ffn1_in_0")], "grad_pair_gather_last")
    done["w_ffn1_in"] = update("w_ffn1_in")
    return (loss, dx[None], *[done[name][i] for i in range(4) for name in given])
```

```python
import functools

import jax
import jax.numpy as jnp
from jax import lax
from jax.experimental import pallas as pl
from jax.experimental.pallas import tpu as pltpu

F32, BF16 = jnp.float32, jnp.bfloat16
D = 1024
D_FF = 2816
N_SLOT = 4
FF_COLS = 2 * D_FF // N_SLOT
N_MOD = 9
MOD_COLS = N_MOD * D // N_SLOT
POOL_W, ATTN_W, KV_W = 512, 512, 128
PROJ_W = POOL_W + ATTN_W + 2 * KV_W
N_HEADS, Q_GROUP, HEAD = 8, 4, 64
GROUP = 128
POOL_WINDOWS = (2, 4, 8, 16)
BLK = 128
QB = 256
WIN = QB + 2 * BLK
GRID_W = 64
ROPE_BASE = 10000.0
EPS = 1e-6
NEG_INF = -1e30
TM = 256
N_DEV = 8
VMEM_LIMIT_BYTES = 56 * 1024 * 1024
WGRAD_VMEM_BYTES = 44 * 1024 * 1024
ADAM_LR, ADAM_B1, ADAM_B2, ADAM_EPS, ADAM_WD, ADAM_STEP = 0.001, 0.9, 0.999, 1e-08, 0.01, 10
MESH = pl.DeviceIdType.MESH
NT = (((1,), (1,)), ((), ()))
TN = (((0,), (0,)), ((), ()))


def _params(*sem):
    return pltpu.CompilerParams(dimension_semantics=sem, vmem_limit_bytes=VMEM_LIMIT_BYTES)


def _whole(shape, lead=()):
    idx = tuple(lead) + (0,) * len(shape)
    return pl.BlockSpec((None,) * len(lead) + tuple(shape), lambda *_: idx, pipeline_mode=pl.Buffered(1))


def _rows(cols, tm=TM):
    return pl.BlockSpec((tm, cols), lambda i: (i, 0))


def _mods_spec(layer, n_lat):
    return pl.BlockSpec((None, None, 16, D), lambda i: (layer, (i >= n_lat).astype(jnp.int32), 0, 0))


def _acc_spec(n_lat):
    return pl.BlockSpec((None, 8, D), lambda i: ((i >= n_lat).astype(jnp.int32), 0, 0))


def _dot(a, b):
    return jnp.dot(a, b, preferred_element_type=F32)


def _dotg(a, b, dims):
    return lax.dot_general(a, b, dims, preferred_element_type=F32)


def _sum0(v):
    return jnp.sum(v, axis=0, keepdims=True)


def _norm_mod(h, g, shift, scale):
    r = lax.rsqrt(jnp.mean(h * h, axis=-1, keepdims=True) + EPS)
    xhat = h * r
    y = xhat * g
    return y * (1 + scale) + shift, xhat, r, y


def _norm_mod_bwd(dn, xhat, r, y, g, scale):
    dy = dn * (1 + scale)
    dx = dy * g
    dh = r * (dx - xhat * jnp.mean(dx * xhat, axis=-1, keepdims=True))
    return _sum0(dn), _sum0(dn * y), _sum0(dy * xhat), dh


def _swap_halves(v):
    w = v.shape[1]
    lane = lax.broadcasted_iota(jnp.int32, v.shape, 1)
    return jnp.where(lane % HEAD < HEAD // 2, pltpu.roll(v, w - HEAD // 2, axis=1), pltpu.roll(v, HEAD // 2, axis=1))


def _tile_lanes(t, width):
    return t if width == t.shape[1] else jnp.concatenate([t] * (width // t.shape[1]), axis=1)


def _rope(v, cos, sin):
    return v * _tile_lanes(cos, v.shape[1]) + _swap_halves(v) * _tile_lanes(sin, v.shape[1])


def _unrope(g, cos, sin):
    return g * _tile_lanes(cos, g.shape[1]) + _swap_halves(g * _tile_lanes(sin, g.shape[1]))


def ffn_fwd(h, mods, g, w4, wo, layer, k0, n_lat, name, ex=None):
    s = h.shape[0]

    def body(h_ref, m_ref, g_ref, w_ref, wo_ref, ho_ref, ab_ref, f_ref):
        hh = h_ref[...]
        n, _, _, _ = _norm_mod(hh, g_ref[...], m_ref[k0:k0 + 1, :], m_ref[k0 + 1:k0 + 2, :])
        nb = n.astype(BF16)
        acc = jnp.zeros((TM, D), F32)
        for j in range(2):
            a = _dot(nb, w_ref[j])
            b = _dot(nb, w_ref[2 + j])
            ab_ref[:, j * FF_COLS:(j + 1) * FF_COLS] = a.astype(BF16)
            ab_ref[:, (2 + j) * FF_COLS:(3 + j) * FF_COLS] = b.astype(BF16)
            act = (a * jax.nn.sigmoid(a) * b).astype(BF16)
            acc = acc + _dot(act, wo_ref[j * FF_COLS:(j + 1) * FF_COLS, :])
        f_ref[...] = acc
        ho_ref[...] = hh + 0.5 * m_ref[k0 + 2:k0 + 3, :] * acc

    return _grid_call(
        body, name, s // TM,
        [_rows(D), _mods_spec(layer, n_lat), _whole((1, D), (layer,)), _whole((N_SLOT, D, FF_COLS)), _whole((D_FF, D))],
        [_rows(D), _rows(2 * D_FF), _rows(D)],
        [jax.ShapeDtypeStruct((s, D), F32), jax.ShapeDtypeStruct((s, 2 * D_FF), BF16), jax.ShapeDtypeStruct((s, D), F32)],
        (h, mods, g, w4, wo), "parallel", ex)


def ffn_bwd(h, ab, f, dh, mods, g, w4, wo, layer, k0, n_lat, name, ex=None):
    s = h.shape[0]

    def body(h_ref, ab_ref, f_ref, dh_ref, m_ref, g_ref, w_ref, wo_ref, dhi_ref, dab_ref, df_ref, n_ref, act_ref, dm_ref):
        i = pl.program_id(0)

        @pl.when((i == 0) | (i == n_lat))
        def _():
            dm_ref[...] = jnp.zeros_like(dm_ref)

        hh, dho, gg = h_ref[...], dh_ref[...], g_ref[...]
        scale, gate = m_ref[k0 + 1:k0 + 2, :], m_ref[k0 + 2:k0 + 3, :]
        n, xhat, r, y = _norm_mod(hh, gg, m_ref[k0:k0 + 1, :], scale)
        n_ref[...] = n.astype(BF16)
        dgate = _sum0(dho * (0.5 * f_ref[...]))
        dfb = ((0.5 * gate) * dho).astype(BF16)
        df_ref[...] = dfb
        dn = jnp.zeros((TM, D), F32)
        for j in range(2):
            a = ab_ref[:, j * FF_COLS:(j + 1) * FF_COLS].astype(F32)
            b = ab_ref[:, (2 + j) * FF_COLS:(3 + j) * FF_COLS].astype(F32)
            sg = jax.nn.sigmoid(a)
            sa = a * sg
            act_ref[:, j * FF_COLS:(j + 1) * FF_COLS] = (sa * b).astype(BF16)
            dact = _dotg(dfb, wo_ref[j * FF_COLS:(j + 1) * FF_COLS, :], NT)
            da = (dact * b * (sg * (1 + a * (1 - sg)))).astype(BF16)
            db = (dact * sa).astype(BF16)
            dab_ref[:, j * FF_COLS:(j + 1) * FF_COLS] = da
            dab_ref[:, (2 + j) * FF_COLS:(3 + j) * FF_COLS] = db
            dn = dn + _dotg(da, w_ref[j], NT) + _dotg(db, w_ref[2 + j], NT)
        dsh, dsc, dg, dhn = _norm_mod_bwd(dn, xhat, r, y, gg, scale)
        dhi_ref[...] = dho + dhn
        dm_ref[0:1, :] += dsh
        dm_ref[1:2, :] += dsc
        dm_ref[2:3, :] += dgate
        dm_ref[3:4, :] += dg

    return _grid_call(
        body, name, s // TM,
        [_rows(D), _rows(2 * D_FF), _rows(D), _rows(D), _mods_spec(layer, n_lat), _whole((1, D), (layer,)),
         _whole((N_SLOT, D, FF_COLS)), _whole((D_FF, D))],
        [_rows(D), _rows(2 * D_FF), _rows(D), _rows(D), _rows(D_FF), _acc_spec(n_lat)],
        [jax.ShapeDtypeStruct((s, D), F32), jax.ShapeDtypeStruct((s, 2 * D_FF), BF16), jax.ShapeDtypeStruct((s, D), BF16),
         jax.ShapeDtypeStruct((s, D), BF16), jax.ShapeDtypeStruct((s, D_FF), BF16), jax.ShapeDtypeStruct((2, 8, D), F32)],
        (h, ab, f, dh, mods, g, w4, wo), "arbitrary", ex)


def _token_tile(s, limit=2176):
    return max(ts for ts in range(16, limit + 1, 16) if s % ts == 0)


def wgrad(a, b, tk, tn, slot_cols, name, ex=None):
    s, k = a.shape
    n = b.shape[1]
    a_bufs, b_bufs = (1 if k == tk else 2), (1 if n == tn else 2)
    whole = 2 * s * (a_bufs * tk + b_bufs * tn) + 2 * 6 * tk * tn
    ts = s if whole <= WGRAD_VMEM_BYTES else _token_tile(s)
    steps = s // ts
    once = dict(pipeline_mode=pl.Buffered(1))

    def body(a_ref, b_ref, o_ref, o16_ref):
        r = _dotg(a_ref[...], b_ref[...], TN)
        si = pl.program_id(2)

        @pl.when(si == 0)
        def _():
            o_ref[...] = r

        @pl.when(si > 0)
        def _():
            o_ref[...] += r

        @pl.when(si == steps - 1)
        def _():
            o16_ref[...] = o_ref[...].astype(BF16)

    n_outer = tn > tk
    grid = (n // tn, k // tk, steps) if n_outer else (k // tk, n // tn, steps)
    ij = (lambda g0, g1: (g1, g0)) if n_outer else (lambda g0, g1: (g0, g1))

    def a_map(g0, g1, si):
        return si, ij(g0, g1)[0]

    def b_map(g0, g1, si):
        return si, ij(g0, g1)[1]

    if slot_cols is None:
        shape, spec = (k, n), pl.BlockSpec((tk, tn), lambda g0, g1, si: ij(g0, g1))
    else:
        per = slot_cols // tn

        def slot_map(g0, g1, si):
            i, j = ij(g0, g1)
            return lax.div(j, per), i, lax.rem(j, per)

        shape, spec = (n // slot_cols, k, slot_cols), pl.BlockSpec((None, tk, tn), slot_map)
    return _grid_call(
        body, name, grid,
        [pl.BlockSpec((ts, tk), a_map, **(once if a_bufs == 1 and steps == 1 else {})),
         pl.BlockSpec((ts, tn), b_map, **(once if b_bufs == 1 and steps == 1 else {}))], [spec, spec],
        [jax.ShapeDtypeStruct(shape, F32), jax.ShapeDtypeStruct(shape, BF16)], (a, b), ("parallel", "parallel", "arbitrary"), ex)


def proj_fwd(h, mods, g, w_in, cos, sin, layer, n_lat, name, ex=None):
    s = h.shape[0]

    def body(h_ref, m_ref, g_ref, w_ref, cos_ref, sin_ref, u_ref, q_ref, k_ref, v_ref):
        n, _, _, _ = _norm_mod(h_ref[...], g_ref[...], m_ref[3:4, :], m_ref[4:5, :])
        p = _dot(n.astype(BF16), w_ref[...])
        cs, sn = cos_ref[...], sin_ref[...]
        u_ref[...] = p[:, :POOL_W]
        q_ref[...] = (_rope(p[:, POOL_W:POOL_W + ATTN_W], cs, sn) * HEAD ** -0.5).astype(BF16)
        k_ref[...] = _rope(p[:, POOL_W + ATTN_W:POOL_W + ATTN_W + KV_W], cs, sn).astype(BF16)
        v_ref[...] = p[:, POOL_W + ATTN_W + KV_W:].astype(BF16)

    return _grid_call(
        body, name, s // TM,
        [_rows(D), _mods_spec(layer, n_lat), _whole((1, D), (layer,)), _whole((D, PROJ_W)), _rows(BLK), _rows(BLK)],
        [_rows(POOL_W), _rows(ATTN_W), _rows(KV_W), _rows(KV_W)],
        [jax.ShapeDtypeStruct((s, POOL_W), F32), jax.ShapeDtypeStruct((s, ATTN_W), BF16),
         jax.ShapeDtypeStruct((s, KV_W), BF16), jax.ShapeDtypeStruct((s, KV_W), BF16)],
        (h, mods, g, w_in, cos, sin), "parallel", ex)


def proj_bwd(h, du, dq, dk, dv, dh, mods, g, w_in, cos, sin, layer, n_lat, name):
    s = h.shape[0]

    def body(h_ref, du_ref, dq_ref, dk_ref, dv_ref, dh_ref, m_ref, g_ref, w_ref, cos_ref, sin_ref,
             dhi_ref, dp_ref, n_ref, dm_ref):
        i = pl.program_id(0)

        @pl.when((i == 0) | (i == n_lat))
        def _():
            dm_ref[...] = jnp.zeros_like(dm_ref)

        gg, scale = g_ref[...], m_ref[4:5, :]
        n, xhat, r, y = _norm_mod(h_ref[...], gg, m_ref[3:4, :], scale)
        n_ref[...] = n.astype(BF16)
        cs, sn = cos_ref[...], sin_ref[...]
        dp = jnp.concatenate([du_ref[...], _unrope(dq_ref[...], cs, sn) * HEAD ** -0.5, _unrope(dk_ref[...], cs, sn),
                              dv_ref[...]], axis=1).astype(BF16)
        dp_ref[...] = dp
        dsh, dsc, dg, dhn = _norm_mod_bwd(_dotg(dp, w_ref[...], NT), xhat, r, y, gg, scale)
        dhi_ref[...] = dh_ref[...] + dhn
        dm_ref[0:1, :] += dsh
        dm_ref[1:2, :] += dsc
        dm_ref[3:4, :] += dg

    return pl.pallas_call(
        body, name=name, grid=(s // TM,),
        in_specs=[_rows(D), _rows(POOL_W), _rows(ATTN_W), _rows(KV_W), _rows(KV_W), _rows(D), _mods_spec(layer, n_lat),
                  _whole((1, D), (layer,)), _whole((D, PROJ_W)), _rows(BLK), _rows(BLK)],
        out_specs=[_rows(D), _rows(PROJ_W), _rows(D), _acc_spec(n_lat)],
        out_shape=[jax.ShapeDtypeStruct((s, D), F32), jax.ShapeDtypeStruct((s, PROJ_W), BF16),
                   jax.ShapeDtypeStruct((s, D), BF16), jax.ShapeDtypeStruct((2, 8, D), F32)],
        compiler_params=_params("arbitrary"),
    )(h, du, dq, dk, dv, dh, mods, g, w_in, cos, sin)


def _window(i, s):
    return pl.multiple_of(jnp.clip(i * QB - BLK, 0, s - WIN), BLK)


def mix_tables(t, s):
    n_lat = t // QB
    blocks = jnp.array([0, 1, n_lat - 1] + list(range(n_lat, s // QB)))[:, None, None]
    ws = jnp.clip(blocks * QB - BLK, 0, s - WIN)
    q = blocks * QB + jnp.arange(QB)[None, :, None]
    k = ws + jnp.arange(WIN)[None, None, :]
    is_lat = blocks < n_lat
    local = jnp.where(is_lat & (k < t) & (jnp.abs(k - q) <= BLK), 0.0, NEG_INF).astype(F32)
    bias = jnp.concatenate([local, jnp.zeros(local.shape[:2] + (s - t,), F32)], axis=2)
    seq_lo, seq_hi = jnp.where(is_lat, 0, t), jnp.where(is_lat, t, s)
    bands, counts = [], []
    for w in POOL_WINDOWS:
        lo, hi = jnp.maximum(q - w // 2, seq_lo), jnp.minimum(q + w - w // 2, seq_hi)
        bands.append((k >= lo) & (k < hi))
        counts.append((hi - lo).astype(F32))
    band = jnp.stack(bands, axis=1).astype(BF16)
    count = jnp.concatenate(counts + [jnp.ones(counts[0].shape[:2] + (BLK - len(counts),), F32)], axis=2)
    return dict(bias=bias, band=band, band_t=band.transpose(0, 1, 3, 2), count=count)


def _case_spec(table, n_lat_blk):
    def kind(i):
        return jnp.where(i < n_lat_blk - 1, jnp.minimum(i, 1), i - n_lat_blk + 3)

    shape = table.shape[1:]
    return pl.BlockSpec((None,) + shape, lambda i: (kind(i),) + (0,) * len(shape))


def _split_dot(band, v):
    return _dot(band, v.astype(BF16))


def _pooled(u_ref, band_ref, cnt_ref, i, ws, gi):
    cols = slice(gi * GROUP, (gi + 1) * GROUP)
    mean = _split_dot(band_ref[gi], u_ref[pl.ds(ws, WIN), cols]) / cnt_ref[:, gi:gi + 1]
    return mean - u_ref[pl.ds(pl.multiple_of(i * QB, QB), QB), cols]


def _head_cols(hd):
    return slice(hd * HEAD, (hd + 1) * HEAD)


def _stack_heads(x, hk, first=0):
    return jnp.concatenate([x[:, first + (Q_GROUP * hk + g) * HEAD:first + (Q_GROUP * hk + g + 1) * HEAD]
                            for g in range(Q_GROUP)], axis=0)


def _biased(scores, bias):
    return (scores.reshape(Q_GROUP, QB, -1) + bias).reshape(Q_GROUP * QB, -1)


def _group_column(vals):
    row = lax.broadcasted_iota(jnp.int32, (Q_GROUP * QB, 1), 0)
    out = jnp.full((Q_GROUP * QB, 1), vals[Q_GROUP - 1], F32)
    for g in range(Q_GROUP - 2, -1, -1):
        out = jnp.where(row < (g + 1) * QB, vals[g], out)
    return out


def _lane_place(cols, width=BLK):
    lane = lax.broadcasted_iota(jnp.int32, (cols[0].shape[0], width), 1)
    out = jnp.zeros((cols[0].shape[0], width), F32)
    for hd, c in enumerate(cols):
        out = jnp.where(lane == hd, c, out)
    return out


def mix_fwd(h, q, k, v, u, w_pool, pool_scale, sink, w_out, mods, tables, layer, t, name, ex=None):
    s = h.shape[0]
    n_lat_blk = t // QB

    def body(h_ref, q_ref, k_ref, v_ref, u_ref, wp_ref, ps_ref, sink_ref, wo_ref, m_ref, bias_ref, band_ref, cnt_ref,
             ho_ref, cat_ref, lse_ref, mo_ref):
        i = pl.program_id(0)
        ws = _window(i, s)
        for gi in range(len(POOL_WINDOWS)):
            mixed = _dot(_pooled(u_ref, band_ref, cnt_ref, i, ws, gi).astype(BF16), wp_ref[gi])
            cat_ref[:, gi * GROUP:(gi + 1) * GROUP] = (mixed * ps_ref[:, gi * GROUP:(gi + 1) * GROUP]).astype(BF16)
        bias = bias_ref[...]
        k_all = jnp.concatenate([k_ref[pl.ds(ws, WIN), :], k_ref[t:s, :]], axis=0)
        v_all = jnp.concatenate([v_ref[pl.ds(ws, WIN), :], v_ref[t:s, :]], axis=0)
        lses = []
        for hk in range(N_HEADS // Q_GROUP):
            kv = _head_cols(hk)
            sc = _biased(_dotg(_stack_heads(q_ref[...], hk), k_all[:, kv], NT), bias)
            sk = _group_column([sink_ref[layer, Q_GROUP * hk + g] for g in range(Q_GROUP)])
            m = jnp.maximum(jnp.max(sc, axis=1, keepdims=True), sk)
            e = jnp.exp(sc - m)
            l = jnp.sum(e, axis=1, keepdims=True) + jnp.exp(sk - m)
            o = _dot(e.astype(BF16), v_all[:, kv]) * (1.0 / l)
            lse = m + jnp.log(l)
            for g in range(Q_GROUP):
                hd = Q_GROUP * hk + g
                cat_ref[:, POOL_W + hd * HEAD:POOL_W + (hd + 1) * HEAD] = o[g * QB:(g + 1) * QB].astype(BF16)
                lses.append(lse[g * QB:(g + 1) * QB])
        lse_ref[...] = _lane_place(lses)
        mo = _dot(cat_ref[...], wo_ref[...])
        mo_ref[...] = mo
        ho_ref[...] = h_ref[...] + m_ref[5:6, :] * mo

    blk = lambda cols: _rows(cols, QB)
    return _grid_call(
        body, name, s // QB,
        [blk(D), blk(ATTN_W), _whole((s, KV_W)), _whole((s, KV_W)), _whole((s, POOL_W)),
         _whole((len(POOL_WINDOWS), GROUP, GROUP), (layer,)), _whole((1, POOL_W), (layer,)),
         pl.BlockSpec(memory_space=pltpu.SMEM), _whole((POOL_W + ATTN_W, D)), _mods_spec(layer, n_lat_blk),
         _case_spec(tables["bias"], n_lat_blk), _case_spec(tables["band"], n_lat_blk), _case_spec(tables["count"], n_lat_blk)],
        [blk(D), blk(POOL_W + ATTN_W), blk(BLK), blk(D)],
        [jax.ShapeDtypeStruct((s, D), F32), jax.ShapeDtypeStruct((s, POOL_W + ATTN_W), BF16), jax.ShapeDtypeStruct((s, BLK), F32),
         jax.ShapeDtypeStruct((s, D), F32)],
        (h, q, k, v, u, w_pool, pool_scale, sink, w_out, mods, tables["bias"], tables["band"], tables["count"]), "parallel", ex)


def mix_bwd(dh, mo, q, k, v, u, lse, w_pool, pool_scale, sink, w_out, mods, tables, layer, t, name, ex=None):
    s = dh.shape[0]
    n_lat_blk = t // QB
    n_grp = len(POOL_WINDOWS)

    def body(dh_ref, mo_ref, q_ref, k_ref, v_ref, u_ref, lse_ref, wp_ref, ps_ref, sink_ref, wo_ref, m_ref,
             bias_ref, band_ref, band_t_ref, cnt_ref,
             dq_ref, dk_ref, dv_ref, du_ref, dmo_ref, dwp_ref, dps_ref, dsink_ref, dm_ref):
        i = pl.program_id(0)

        @pl.when(i == 0)
        def _():
            for ref in (dk_ref, dv_ref, du_ref, dwp_ref, dps_ref, dsink_ref):
                ref[...] = jnp.zeros_like(ref)

        @pl.when((i == 0) | (i == n_lat_blk))
        def _():
            dm_ref[...] = jnp.zeros_like(dm_ref)

        ws = _window(i, s)
        here = pl.ds(pl.multiple_of(i * QB, QB), QB)
        dho = dh_ref[...]
        dm_ref[2:3, :] += _sum0(dho * mo_ref[...])
        dmo = (m_ref[5:6, :] * dho).astype(BF16)
        dmo_ref[...] = dmo
        dcat = _dotg(dmo, wo_ref[...], NT)

        for gi in range(n_grp):
            cols = slice(gi * GROUP, (gi + 1) * GROUP)
            pooled = _pooled(u_ref, band_ref, cnt_ref, i, ws, gi).astype(BF16)
            dpo = dcat[:, cols]
            dps_ref[0:1, cols] += _sum0(dpo * _dot(pooled, wp_ref[gi]))
            dmixed = (dpo * ps_ref[:, cols]).astype(BF16)
            dwp_ref[gi] += _dotg(pooled, dmixed, TN)
            dpooled = _dotg(dmixed, wp_ref[gi], NT)
            du_ref[pl.ds(ws, WIN), cols] += _split_dot(band_t_ref[gi], dpooled / cnt_ref[:, gi:gi + 1])
            du_ref[here, cols] -= dpooled

        bias = bias_ref[...]
        k_all = jnp.concatenate([k_ref[pl.ds(ws, WIN), :], k_ref[t:s, :]], axis=0)
        v_all = jnp.concatenate([v_ref[pl.ds(ws, WIN), :], v_ref[t:s, :]], axis=0)
        qq, lse_all = q_ref[...], lse_ref[...]
        dqs, dsinks, dks, dvs = [], [], [], []
        for hk in range(N_HEADS // Q_GROUP):
            kv = _head_cols(hk)
            q4 = _stack_heads(qq, hk)
            lse = jnp.concatenate([lse_all[:, Q_GROUP * hk + g:Q_GROUP * hk + g + 1] for g in range(Q_GROUP)], axis=0)
            p = jnp.exp(_biased(_dotg(q4, k_all[:, kv], NT), bias) - lse)
            do = _stack_heads(dcat, hk, POOL_W).astype(BF16)
            dp = _dotg(do, v_all[:, kv], NT)
            delta = jnp.sum(p * dp, axis=1, keepdims=True)
            ds = (p * (dp - delta)).astype(BF16)
            sk = _group_column([sink_ref[layer, Q_GROUP * hk + g] for g in range(Q_GROUP)])
            dsk = -jnp.exp(sk - lse) * delta
            dq = _dot(ds, k_all[:, kv])
            for g in range(Q_GROUP):
                dqs.append(dq[g * QB:(g + 1) * QB])
                dsinks.append(_sum0(dsk[g * QB:(g + 1) * QB]))
            dks.append(_dotg(ds, q4, TN))
            dvs.append(_dotg(p.astype(BF16), do, TN))
        dq_ref[...] = jnp.concatenate(dqs, axis=1)
        dk, dv = jnp.concatenate(dks, axis=1), jnp.concatenate(dvs, axis=1)
        dk_ref[pl.ds(ws, WIN), :] += dk[:WIN]
        dv_ref[pl.ds(ws, WIN), :] += dv[:WIN]
        dk_ref[t:s, :] += dk[WIN:]
        dv_ref[t:s, :] += dv[WIN:]
        dsink_ref[0:1, :] += _lane_place(dsinks)

    blk = lambda cols: _rows(cols, QB)
    full = lambda shape: pl.BlockSpec(shape, lambda i: (0,) * len(shape))
    return _grid_call(
        body, name, s // QB,
        [blk(D), blk(D), blk(ATTN_W), _whole((s, KV_W)), _whole((s, KV_W)), _whole((s, POOL_W)),
         blk(BLK), _whole((n_grp, GROUP, GROUP), (layer,)), _whole((1, POOL_W), (layer,)),
         pl.BlockSpec(memory_space=pltpu.SMEM), _whole((POOL_W + ATTN_W, D)), _mods_spec(layer, n_lat_blk)]
        + [_case_spec(tables[key], n_lat_blk) for key in ("bias", "band", "band_t", "count")],
        [blk(ATTN_W), full((s, KV_W)), full((s, KV_W)), full((s, POOL_W)), blk(D),
         full((n_grp, GROUP, GROUP)), full((8, POOL_W)), full((8, BLK)), _acc_spec(n_lat_blk)],
        [jax.ShapeDtypeStruct((s, ATTN_W), F32), jax.ShapeDtypeStruct((s, KV_W), F32),
         jax.ShapeDtypeStruct((s, KV_W), F32), jax.ShapeDtypeStruct((s, POOL_W), F32),
         jax.ShapeDtypeStruct((s, D), BF16), jax.ShapeDtypeStruct((n_grp, GROUP, GROUP), F32),
         jax.ShapeDtypeStruct((8, POOL_W), F32), jax.ShapeDtypeStruct((8, BLK), F32), jax.ShapeDtypeStruct((2, 8, D), F32)],
        (dh, mo, q, k, v, u, lse, w_pool, pool_scale, sink, w_out, mods, tables["bias"], tables["band"], tables["band_t"],
         tables["count"]), "arbitrary", ex)


def loss_head(h, target, g, t, name):
    s = h.shape[0]
    n_lat = t // TM

    def body(h_ref, t_ref, g_ref, dh_ref, acc_ref):
        i = pl.program_id(0)

        @pl.when(i == 0)
        def _():
            acc_ref[...] = jnp.zeros_like(acc_ref)

        @pl.when(i < n_lat)
        def _():
            hh, gg = h_ref[...], g_ref[...]
            r = lax.rsqrt(jnp.mean(hh * hh, axis=-1, keepdims=True) + EPS)
            xhat = hh * r
            err = xhat * gg - t_ref[...]
            dy = err * (1.0 / D)
            dx = dy * gg
            dh_ref[...] = r * (dx - xhat * jnp.mean(dx * xhat, axis=-1, keepdims=True))
            acc_ref[0:1, :] += _sum0(dy * xhat)
            acc_ref[1:2, :] += _sum0(err * err)

        @pl.when(i >= n_lat)
        def _():
            dh_ref[...] = jnp.zeros_like(dh_ref)

    return pl.pallas_call(
        body, name=name, grid=(s // TM,),
        in_specs=[_rows(D), pl.BlockSpec((TM, D), lambda i: (jnp.minimum(i, n_lat - 1), 0)), _whole((1, D))],
        out_specs=[_rows(D), pl.BlockSpec((8, D), lambda i: (0, 0))],
        out_shape=[jax.ShapeDtypeStruct((s, D), F32), jax.ShapeDtypeStruct((8, D), F32)],
        compiler_params=_params("arbitrary"),
    )(h, target, g)


def mod_rows(c_all, w_mod, b_cols, name):
    def body(c_ref, w_ref, b_ref, o_ref):
        cc = c_ref[...]
        o_ref[...] = _dot((cc * jax.nn.sigmoid(cc)).astype(BF16), w_ref[...].astype(BF16)) + b_ref[...]

    return pl.pallas_call(
        body, name=name, grid=(2,),
        in_specs=[pl.BlockSpec((16, D), lambda l: (0, 0)), pl.BlockSpec((None, D, MOD_COLS), lambda l: (l, 0, 0)),
                  pl.BlockSpec((None, 1, MOD_COLS), lambda l: (l, 0, 0))],
        out_specs=pl.BlockSpec((None, 16, MOD_COLS), lambda l: (l, 0, 0)),
        out_shape=jax.ShapeDtypeStruct((2, 16, MOD_COLS), F32),
        compiler_params=_params("parallel"),
    )(c_all, w_mod, b_cols)


def mod_grads(c_all, dmod_cols, w_mod, name):
    def body(c_ref, d_ref, w_ref, dw_ref, dc_ref):
        @pl.when(pl.program_id(0) == 0)
        def _():
            dc_ref[...] = jnp.zeros_like(dc_ref)

        cc = c_ref[...]
        dd = d_ref[...].astype(BF16)
        dw_ref[...] = _dotg((cc * jax.nn.sigmoid(cc)).astype(BF16), dd, TN)
        dc_ref[...] += _dotg(dd, w_ref[...].astype(BF16), NT)

    return pl.pallas_call(
        body, name=name, grid=(2,),
        in_specs=[pl.BlockSpec((16, D), lambda l: (0, 0)), pl.BlockSpec((None, 16, MOD_COLS), lambda l: (l, 0, 0)),
                  pl.BlockSpec((None, D, MOD_COLS), lambda l: (l, 0, 0))],
        out_specs=[pl.BlockSpec((None, D, MOD_COLS), lambda l: (l, 0, 0)), pl.BlockSpec((16, D), lambda l: (0, 0))],
        out_shape=[jax.ShapeDtypeStruct((2, D, MOD_COLS), F32), jax.ShapeDtypeStruct((16, D), F32)],
        compiler_params=_params("arbitrary"),
    )(c_all, dmod_cols, w_mod)


def _row_tile(rows, cols, n_arrays):
    budget = VMEM_LIMIT_BYTES // 4 // (2 * 4 * n_arrays * cols)
    best = None
    for tr in range(16, rows + 1, 16):
        if rows % tr == 0 and tr <= budget:
            best = tr
    return best if best is not None else rows


def elementwise(fn, ins, out_dtypes, name, ex=None):
    rows, cols = ins[0].shape
    tr = _row_tile(rows, cols, len(ins) + len(out_dtypes))

    def body(*refs):
        outs = fn(*[r[...] for r in refs[:len(ins)]])
        for o_ref, o in zip(refs[len(ins):], outs):
            o_ref[...] = o.astype(o_ref.dtype)

    spec = pl.BlockSpec((tr, cols), lambda i: (i, 0))
    outs, got = _grid_call(body, name, rows // tr, [spec] * len(ins), [spec] * len(out_dtypes),
                           [jax.ShapeDtypeStruct((rows, cols), dt) for dt in out_dtypes], ins, "parallel", ex)
    return outs if ex is None else (outs, got)


def _adamw_tile(w, g, m, v):
    m = ADAM_B1 * m + (1.0 - ADAM_B1) * g
    v = ADAM_B2 * v + (1.0 - ADAM_B2) * (g * g)
    m_hat = m / (1.0 - ADAM_B1 ** ADAM_STEP)
    v_hat = v / (1.0 - ADAM_B2 ** ADAM_STEP)
    return -ADAM_LR * (m_hat / (jnp.sqrt(v_hat) + ADAM_EPS) + ADAM_WD * w), m, v


def adamw(w, g, m, v, name, ex=None):
    shape = w.shape
    two_d = (-1, shape[-1]) if w.ndim > 1 else (1, -1)
    outs = elementwise(_adamw_tile, [a.reshape(two_d) for a in (w, g, m, v)], [F32] * 3, name, ex)
    outs, got = outs if ex is not None else (outs, None)
    outs = [o.reshape(shape) for o in outs]
    return outs if ex is None else (outs, got)


def _prefetch_call(body, name, grid, in_specs, out_specs, out_shape, place, args, ex=None):
    if ex is None:
        spec = pltpu.PrefetchScalarGridSpec(num_scalar_prefetch=1, grid=grid, in_specs=in_specs, out_specs=out_specs)
        return pl.pallas_call(body, name=name, grid_spec=spec, out_shape=out_shape,
                              compiler_params=_params(*["parallel"] * len(grid)))(place, *args)
    n_in, n_out, ci, co = len(in_specs), len(out_specs), len(ex["ins"]), len(ex["out_shape"])
    spec = pltpu.PrefetchScalarGridSpec(num_scalar_prefetch=1, grid=grid, in_specs=list(in_specs) + _any(ci),
                                        out_specs=list(out_specs) + _any(co), scratch_shapes=ex["scratch"])
    outs = pl.pallas_call(
        _carrying(body, grid, n_in, n_out, ex, lead=1), name=name, grid_spec=spec, out_shape=list(out_shape) + ex["out_shape"],
        input_output_aliases={1 + n_in + i: n_out + j for i, j in ex["aliases"].items()},
        compiler_params=_params(*["arbitrary"] * len(grid)))(place, *args, *ex["ins"])
    return outs[:n_out], outs[n_out:]


def cast_place(w, layer, place, name):
    _, r, c = w.shape
    tr = _row_tile(r, c, 2)

    def body(p_ref, w_ref, o_ref):
        o_ref[...] = w_ref[...].astype(BF16)

    return _prefetch_call(
        body, name, (r // tr,), [pl.BlockSpec((None, tr, c), lambda i, p: (layer, i, 0))],
        pl.BlockSpec((None, tr, c), lambda i, p: (p[1], i, 0)), jax.ShapeDtypeStruct((N_SLOT, r, c), BF16), place, [w])


def pair_sum(g32, got, place, name, ex=None):
    n_slot, rh, c = got.shape
    tr = _row_tile(rh, c, 4)
    per = rh // tr

    def body(p_ref, a_ref, b_ref, o16_ref):
        o16_ref[...] = (a_ref[...] + b_ref[...].astype(F32)).astype(BF16)

    half = pl.BlockSpec((None, tr, c), lambda s, i, p: (s, i, 0))
    out = _prefetch_call(
        body, name, (n_slot, per), [pl.BlockSpec((None, tr, c), lambda s, i, p: (s, p[0] * per + i, 0)), half], [half],
        [jax.ShapeDtypeStruct(got.shape, BF16)], place, [g32, got], ex)
    return [(g32, got), out[0]] if ex is None else ([(g32, got), out[0][0]], out[1])


def chip_sum(terms, got, place, name):
    g32, sent = terms
    _, rh, c = sent.shape
    tr = _row_tile(rh, c, 6)
    per = rh // tr

    def body(p_ref, a_ref, b_ref, r0_ref, r1_ref, r2_ref, o_ref):
        own = a_ref[...] + b_ref[...].astype(F32)
        o_ref[...] = own + r0_ref[...].astype(F32) + r1_ref[...].astype(F32) + r2_ref[...].astype(F32)

    part = pl.BlockSpec((tr, c), lambda i, p: (i, 0))
    return _prefetch_call(
        body, name, (per,),
        [pl.BlockSpec((None, tr, c), lambda i, p: (p[1], p[0] * per + i, 0)), pl.BlockSpec((None, tr, c), lambda i, p: (p[1], i, 0)),
         part, part, part],
        pl.BlockSpec((tr, c), lambda i, p: (p[0] * per + i, 0)), jax.ShapeDtypeStruct((2 * rh, c), F32), place, [g32, sent, *got])


def adamw_layers(w, g0, g1, m, v, name, ex=None):
    _, r, c = w.shape
    tr = _row_tile(r, c, 10)

    def body(w_ref, g0_ref, g1_ref, m_ref, v_ref, g_ref, d_ref, mo_ref, vo_ref):
        g = jnp.where(pl.program_id(0) == 0, g0_ref[...], g1_ref[...])
        g_ref[...] = g
        d_ref[...], mo_ref[...], vo_ref[...] = _adamw_tile(w_ref[...], g, m_ref[...], v_ref[...])

    steps = r // tr
    stacked = pl.BlockSpec((None, tr, c), lambda l, i: (l, i, 0))
    layer0 = pl.BlockSpec((tr, c), lambda l, i: (jnp.where(l == 0, i, steps - 1), 0))
    layer1 = pl.BlockSpec((tr, c), lambda l, i: (jnp.where(l == 0, 0, i), 0))
    outs, got = _grid_call(body, name, (2, steps), [stacked, layer0, layer1, stacked, stacked], [stacked] * 4,
                           [jax.ShapeDtypeStruct(w.shape, F32)] * 4, (w, g0, g1, m, v), "parallel", ex)
    return outs if ex is None else (outs, got)


def sum8(gathered, name):
    def body(*refs):
        n = len(refs) // 2
        for g_ref, o_ref in zip(refs[:n], refs[n:]):
            acc = g_ref[0]
            for dev in range(1, N_DEV):
                acc = acc + g_ref[dev]
            o_ref[...] = acc

    return pl.pallas_call(
        body, name=name,
        out_shape=[jax.ShapeDtypeStruct(a.shape[1:], F32) for a in gathered],
        compiler_params=_params(),
    )(*gathered)


PHASES = ("start", "late", "finish")


def _place():
    return lax.axis_index("x"), lax.axis_index("y"), lax.axis_index("c")


def _any(n):
    return [pl.BlockSpec(memory_space=pl.ANY)] * n


def gather8_exchange(blocks):
    n = len(blocks)

    def copy(outs, sems, ti, k, block, to, src=None):
        dst = outs[ti].at[4 * block[0] + 2 * block[1] + block[2]]
        return pltpu.make_async_remote_copy(src_ref=dst if src is None else src, dst_ref=dst, send_sem=sems[0].at[ti, k],
                                            recv_sem=sems[1].at[ti, k], device_id=to, device_id_type=MESH)

    def first(ins, outs, sems):
        x, y, c = _place()
        local, sent = [], []
        for ti in range(n):
            local.append(pltpu.make_async_copy(ins[ti], outs[ti].at[4 * x + 2 * y + c], sems[2].at[ti]))
            sent.append(copy(outs, sems, ti, 0, (x, y, c), (x, y, 1 - c), src=ins[ti]))
            sent += [copy(outs, sems, ti, 1 + j, (x, y, c), (*chip, c), src=ins[ti]) for j, chip in enumerate(_three_chips(x, y))]
        return local, sent

    def start(ins, outs, sems):
        local, sent = first(ins, outs, sems)
        for cp in local + sent:
            cp.start()

    def passed_on(outs, sems):
        x, y, c = _place()
        return [copy(outs, sems, ti, 4 + j, (*chip, c), (x, y, 1 - c)) for ti in range(n) for j, chip in enumerate(_three_chips(x, y))]

    def late(ins, outs, sems):
        x, y, c = _place()
        on = passed_on(outs, sems)
        for ti in range(n):
            for j, chip in enumerate(_three_chips(x, y)):
                copy(outs, sems, ti, 1 + j, (*chip, c), (x, y, c)).wait_recv()
                on[3 * ti + j].start()

    def finish(ins, outs, sems):
        x, y, c = _place()
        me, sibling = (x, y, c), (x, y, 1 - c)
        local, sent = first(ins, outs, sems)
        for ti in range(n):
            copy(outs, sems, ti, 0, sibling, me).wait_recv()
            for j, chip in enumerate(_three_chips(x, y)):
                copy(outs, sems, ti, 4 + j, (*chip, 1 - c), me).wait_recv()
        for cp in sent + passed_on(outs, sems):
            cp.wait_send()
        for cp in local:
            cp.wait()

    return dict(ins=list(blocks), out_shape=[jax.ShapeDtypeStruct((N_DEV,) + b.shape, b.dtype) for b in blocks], aliases={},
                start=start, late=late, finish=finish,
                scratch=[pltpu.SemaphoreType.DMA((n, 7)), pltpu.SemaphoreType.DMA((n, 7)), pltpu.SemaphoreType.DMA((n,))])


def all_gather(blocks, name):
    return run_exchange(gather8_exchange(blocks), name)


def _three_chips(x, y):
    return [(1 - x, y), (x, 1 - y), (1 - x, 1 - y)]


def gather_exchange(placed):
    n = len(placed)

    def copy(bufs, sems, ti, k, chip, core, to):
        rh = bufs[ti].shape[1] // 2
        half = bufs[ti].at[2 * chip[0] + chip[1], pl.ds(core * rh, rh), :]
        return pltpu.make_async_remote_copy(src_ref=half, dst_ref=half, send_sem=sems[0].at[ti, k], recv_sem=sems[1].at[ti, k],
                                            device_id=to, device_id_type=MESH)

    def sends(bufs, sems):
        x, y, c = _place()
        return [copy(bufs, sems, ti, k, (x, y), c, (*chip, c)) for ti in range(n) for k, chip in enumerate(_three_chips(x, y))]

    def passed_on(bufs, sems):
        x, y, c = _place()
        return [copy(bufs, sems, ti, 3 + k, chip, c, (x, y, 1 - c)) for ti in range(n) for k, chip in enumerate(_three_chips(x, y))]

    def start(ins, bufs, sems):
        for cp in sends(bufs, sems):
            cp.start()

    def late(ins, bufs, sems):
        x, y, c = _place()
        on = passed_on(bufs, sems)
        for ti in range(n):
            for k, chip in enumerate(_three_chips(x, y)):
                copy(bufs, sems, ti, k, chip, c, (x, y, c)).wait_recv()
                on[3 * ti + k].start()

    def finish(ins, bufs, sems):
        x, y, c = _place()
        for ti in range(n):
            for k, chip in enumerate(_three_chips(x, y)):
                copy(bufs, sems, ti, 3 + k, chip, 1 - c, (x, y, c)).wait_recv()
        for cp in sends(bufs, sems) + passed_on(bufs, sems):
            cp.wait_send()

    return dict(ins=list(placed), out_shape=[jax.ShapeDtypeStruct(w.shape, w.dtype) for w in placed],
                aliases={i: i for i in range(n)}, start=start, late=late, finish=finish,
                scratch=[pltpu.SemaphoreType.DMA((n, 6)), pltpu.SemaphoreType.DMA((n, 6))])


def scatter_exchange(p16):
    n = len(p16)

    def copies(ins, got, sems):
        x, y, c = _place()
        return [pltpu.make_async_remote_copy(src_ref=ins[ti].at[2 * chip[0] + chip[1]], dst_ref=got[3 * ti + k],
                                             send_sem=sems[0].at[ti, k], recv_sem=sems[1].at[ti, k], device_id=(*chip, c),
                                             device_id_type=MESH)
                for ti in range(n) for k, chip in enumerate(_three_chips(x, y))]

    def start(ins, got, sems):
        for cp in copies(ins, got, sems):
            cp.start()

    def finish(ins, got, sems):
        for cp in copies(ins, got, sems):
            cp.wait()

    return dict(ins=list(p16), out_shape=[jax.ShapeDtypeStruct(a.shape[1:], BF16) for a in p16 for _ in range(3)], aliases={},
                start=start, finish=finish, scratch=[pltpu.SemaphoreType.DMA((n, 3)), pltpu.SemaphoreType.DMA((n, 3))])


def run_exchange(ex, name):
    ci, co = len(ex["ins"]), len(ex["out_shape"])

    def body(*refs):
        ins, outs, sems = refs[:ci], refs[ci:ci + co], refs[ci + co:]
        for phase in PHASES:
            if phase in ex:
                ex[phase](ins, outs, sems)

    return pl.pallas_call(body, name=name, in_specs=_any(ci), out_specs=_any(co), out_shape=ex["out_shape"],
                          input_output_aliases=ex["aliases"], scratch_shapes=ex["scratch"])(*ex["ins"])


def _carrying(body, grid, n_in, n_out, ex, lead=0):
    ci, co = len(ex["ins"]), len(ex["out_shape"])
    first, last = (0,) * len(grid), tuple(g - 1 for g in grid)
    steps = dict(start=first, late=(grid[0] - 2,) if len(grid) == 1 and grid[0] > 2 else last, finish=last)

    def at(ids):
        return functools.reduce(jnp.logical_and, [pl.program_id(ax) == v for ax, v in enumerate(ids)])

    def carrying(*refs):
        head, refs = refs[:lead], refs[lead:]
        c_in, c_out = refs[n_in:n_in + ci], refs[n_in + ci + n_out:n_in + ci + n_out + co]
        sems = refs[n_in + ci + n_out + co:]
        for phase in PHASES:
            if phase == "finish":
                body(*head, *refs[:n_in], *refs[n_in + ci:n_in + ci + n_out])
            if phase in ex:
                pl.when(at(steps[phase]))(functools.partial(ex[phase], c_in, c_out, sems))

    return carrying


def _grid_call(body, name, grid, in_specs, out_specs, out_shape, args, sem, ex=None):
    grid = (grid,) if isinstance(grid, int) else tuple(grid)
    sems_of = (sem,) * len(grid) if isinstance(sem, str) else tuple(sem)
    n_in, n_out = len(in_specs), len(out_specs)
    if ex is None:
        return pl.pallas_call(body, name=name, grid=grid, in_specs=in_specs, out_specs=out_specs, out_shape=out_shape,
                              compiler_params=_params(*sems_of))(*args), []
    ci, co = len(ex["ins"]), len(ex["out_shape"])
    outs = pl.pallas_call(
        _carrying(body, grid, n_in, n_out, ex), name=name, grid=grid, in_specs=list(in_specs) + _any(ci),
        out_specs=list(out_specs) + _any(co), out_shape=list(out_shape) + ex["out_shape"], scratch_shapes=ex["scratch"],
        input_output_aliases={n_in + i: n_out + j for i, j in ex["aliases"].items()},
        compiler_params=_params(*["arbitrary"] * len(grid)),
    )(*args, *ex["ins"])
    return outs[:n_out], outs[n_out:]


def both(*exchanges):
    exchanges = [ex for ex in exchanges if ex is not None]
    if len(exchanges) < 2:
        return exchanges[0] if exchanges else None
    n_ins = [len(ex["ins"]) for ex in exchanges]
    n_outs = [len(ex["out_shape"]) for ex in exchanges]
    n_sems = [len(ex["scratch"]) for ex in exchanges]

    def parts(seq, counts, k):
        first = sum(counts[:k])
        return seq[first:first + counts[k]]

    def run(phase):
        def go(ins, outs, sems):
            for k, ex in enumerate(exchanges):
                if phase in ex:
                    ex[phase](parts(ins, n_ins, k), parts(outs, n_outs, k), parts(sems, n_sems, k))
        return go

    aliases = {sum(n_ins[:k]) + i: sum(n_outs[:k]) + j for k, ex in enumerate(exchanges) for i, j in ex["aliases"].items()}
    return dict(ins=[a for ex in exchanges for a in ex["ins"]], out_shape=[o for ex in exchanges for o in ex["out_shape"]],
                aliases=aliases, scratch=[s for ex in exchanges for s in ex["scratch"]], **{ph: run(ph) for ph in PHASES})


def split_outputs(got, *exchanges):
    got, out = list(got), []
    for ex in exchanges:
        n = len(ex["out_shape"]) if ex is not None else 0
        out.append(got[:n])
        got = got[n:]
    return out


def pair_exchange(g16):
    n = len(g16)

    def copies(a16, got, sems):
        x, y, c = _place()
        out = []
        for ti in range(n):
            rh = a16[ti].shape[1] // 2
            out.append(pltpu.make_async_remote_copy(
                src_ref=a16[ti].at[:, pl.ds((1 - c) * rh, rh), :], dst_ref=got[ti], send_sem=sems[0].at[ti],
                recv_sem=sems[1].at[ti], device_id=(x, y, 1 - c), device_id_type=MESH))
        return out

    def start(a16, got, sems):
        for cp in copies(a16, got, sems):
            cp.start()

    def finish(a16, got, sems):
        for cp in copies(a16, got, sems):
            cp.wait()

    return dict(ins=list(g16), out_shape=[jax.ShapeDtypeStruct((a.shape[0], a.shape[1] // 2, a.shape[2]), BF16) for a in g16],
                aliases={}, start=start, finish=finish, scratch=[pltpu.SemaphoreType.DMA((n,)), pltpu.SemaphoreType.DMA((n,))])


def _gather_half(buf, chip, core):
    rh = buf.shape[1] // 2
    return buf.at[2 * chip[0] + chip[1], pl.ds(core * rh, rh), :]


def gather_start(placed, name):
    n = len(placed)
    hbm, sem = pl.BlockSpec(memory_space=pltpu.HBM), pl.BlockSpec(memory_space=pltpu.SEMAPHORE)

    def body(*refs):
        bufs, send_sems, recv_sems, token_ref = refs[:n], refs[n], refs[n + 1], refs[-1]
        x, y, c = _place()
        for ti in range(n):
            for k, chip in enumerate(_three_chips(x, y)):
                half = _gather_half(bufs[ti], (x, y), c)
                pltpu.make_async_remote_copy(src_ref=half, dst_ref=half, send_sem=send_sems.at[3 * ti + k],
                                             recv_sem=recv_sems.at[3 * ti + k], device_id=(*chip, c), device_id_type=MESH).start()
        token_ref[...] = jnp.zeros_like(token_ref)

    return pl.pallas_call(
        body, name=name,
        out_shape=(pltpu.SemaphoreType.DMA((3 * n,)), pltpu.SemaphoreType.DMA((3 * n,)), *[pltpu.HBM(w.shape, w.dtype) for w in placed],
                   jax.ShapeDtypeStruct((8, BLK), F32)),
        in_specs=(hbm,) * n, out_specs=(sem, sem, *(hbm,) * n, pl.BlockSpec(memory_space=pltpu.VMEM)),
        input_output_aliases={i: 2 + i for i in range(n)},
        compiler_params=pltpu.CompilerParams(has_side_effects=pltpu.SideEffectType.DATAFLOW_SIDE_EFFECTING),
    )(*[pltpu.with_memory_space_constraint(w, pltpu.HBM) for w in placed])


def gather_wait(send_sems, recv_sems, bufs, after, name):
    n = len(bufs)
    hbm, sem = pl.BlockSpec(memory_space=pltpu.HBM), pl.BlockSpec(memory_space=pltpu.SEMAPHORE)

    def body(*refs):
        bufs, send_sems, recv_sems = refs[:n], refs[n], refs[n + 1]
        x, y, c = _place()
        for ti in range(n):
            for k, chip in enumerate(_three_chips(x, y)):
                mine, theirs = _gather_half(bufs[ti], (x, y), c), _gather_half(bufs[ti], chip, c)
                cp = pltpu.make_async_remote_copy(src_ref=mine, dst_ref=theirs, send_sem=send_sems.at[3 * ti + k],
                                                  recv_sem=recv_sems.at[3 * ti + k], device_id=(*chip, c), device_id_type=MESH)
                cp.wait_send()
                cp.wait_recv()

    return pl.pallas_call(
        body, name=name, out_shape=tuple(pltpu.HBM(w.shape, w.dtype) for w in bufs),
        in_specs=(*(hbm,) * n, sem, sem, *_any(len(after))), out_specs=(hbm,) * n,
        input_output_aliases={i: i for i in range(n)},
        compiler_params=pltpu.CompilerParams(has_side_effects=pltpu.SideEffectType.DATAFLOW_SIDE_EFFECTING),
    )(*bufs, send_sems, recv_sems, *after)


def pass_on_exchange(bufs):
    n = len(bufs)

    def copies(refs, sems, core):
        x, y, c = _place()
        return [pltpu.make_async_remote_copy(src_ref=_gather_half(refs[ti], chip, c if core == "mine" else 1 - c),
                                             dst_ref=_gather_half(refs[ti], chip, c if core == "mine" else 1 - c),
                                             send_sem=sems[0].at[ti, k], recv_sem=sems[1].at[ti, k], device_id=(x, y, 1 - c),
                                             device_id_type=MESH)
                for ti in range(n) for k, chip in enumerate(_three_chips(x, y))]

    def start(ins, refs, sems):
        for cp in copies(refs, sems, "mine"):
            cp.start()

    def finish(ins, refs, sems):
        for cp in copies(refs, sems, "mine"):
            cp.wait_send()
        for cp in copies(refs, sems, "sibling's"):
            cp.wait_recv()

    return dict(ins=list(bufs), out_shape=[jax.ShapeDtypeStruct(w.shape, w.dtype) for w in bufs], aliases={i: i for i in range(n)},
                start=start, finish=finish, scratch=[pltpu.SemaphoreType.DMA((n, 3)), pltpu.SemaphoreType.DMA((n, 3))])


def _scatter_copies(src_ref, lands, send_sems, recv_sems):
    x, y, c = _place()
    return [pltpu.make_async_remote_copy(src_ref=src_ref.at[2 * chip[0] + chip[1]], dst_ref=lands[k], send_sem=send_sems.at[k],
                                         recv_sem=recv_sems.at[k], device_id=(*chip, c), device_id_type=MESH)
            for k, chip in enumerate(_three_chips(x, y))]


def scatter_start(p16, name):
    hbm, sem = pl.BlockSpec(memory_space=pltpu.HBM), pl.BlockSpec(memory_space=pltpu.SEMAPHORE)

    def body(src_ref, l0_ref, l1_ref, l2_ref, send_sems, recv_sems, src_thru, o0_ref, o1_ref, o2_ref, token_ref):
        for cp in _scatter_copies(src_ref, (l0_ref, l1_ref, l2_ref), send_sems, recv_sems):
            cp.start()
        token_ref[...] = jnp.zeros_like(token_ref)

    land = [pltpu.with_memory_space_constraint(lax.empty(p16.shape[1:], BF16), pltpu.HBM) for _ in range(3)]
    return pl.pallas_call(
        body, name=name,
        out_shape=(pltpu.SemaphoreType.DMA((3,)), pltpu.SemaphoreType.DMA((3,)), pltpu.HBM(p16.shape, BF16),
                   *[pltpu.HBM(p16.shape[1:], BF16)] * 3, jax.ShapeDtypeStruct((8, BLK), F32)),
        in_specs=(hbm,) * 4, out_specs=(sem, sem, hbm, hbm, hbm, hbm, pl.BlockSpec(memory_space=pltpu.VMEM)),
        input_output_aliases={0: 2, 1: 3, 2: 4, 3: 5},
        compiler_params=pltpu.CompilerParams(has_side_effects=pltpu.SideEffectType.DATAFLOW_SIDE_EFFECTING),
    )(pltpu.with_memory_space_constraint(p16, pltpu.HBM), *land)


def scatter_wait(send_sems, recv_sems, src_thru, lands, after, name):
    hbm, sem = pl.BlockSpec(memory_space=pltpu.HBM), pl.BlockSpec(memory_space=pltpu.SEMAPHORE)

    def body(src_ref, l0_ref, l1_ref, l2_ref, send_sems, recv_sems, *rest):
        for cp in _scatter_copies(src_ref, (l0_ref, l1_ref, l2_ref), send_sems, recv_sems):
            cp.wait_send()
            cp.wait_recv()

    return pl.pallas_call(
        body, name=name, out_shape=(pltpu.HBM(src_thru.shape, BF16), *[pltpu.HBM(lands[0].shape, BF16)] * 3),
        in_specs=(hbm, hbm, hbm, hbm, sem, sem, *_any(len(after))), out_specs=(hbm,) * 4,
        input_output_aliases={0: 0, 1: 1, 2: 2, 3: 3},
        compiler_params=pltpu.CompilerParams(has_side_effects=pltpu.SideEffectType.DATAFLOW_SIDE_EFFECTING),
    )(src_thru, *lands, send_sems, recv_sems, *after)[1:]


def pair_fill_exchange(halves):
    n = len(halves)

    def copies(bufs, sems, core):
        x, y, c = _place()
        out = []
        for ti in range(n):
            rh = bufs[ti].shape[0] // 2
            rows = bufs[ti].at[pl.ds((c if core == "mine" else 1 - c) * rh, rh), :]
            out.append(pltpu.make_async_remote_copy(src_ref=rows, dst_ref=rows, send_sem=sems[0].at[ti], recv_sem=sems[1].at[ti],
                                                    device_id=(x, y, 1 - c), device_id_type=MESH))
        return out

    def start(ins, bufs, sems):
        for cp in copies(bufs, sems, "mine"):
            cp.start()

    def finish(ins, bufs, sems):
        for cp in copies(bufs, sems, "mine"):
            cp.wait_send()
        for cp in copies(bufs, sems, "sibling's"):
            cp.wait_recv()

    return dict(ins=list(halves), out_shape=[jax.ShapeDtypeStruct(a.shape, a.dtype) for a in halves],
                aliases={i: i for i in range(n)}, start=start, finish=finish,
                scratch=[pltpu.SemaphoreType.DMA((n,)), pltpu.SemaphoreType.DMA((n,))])


def pair_gather(halves, name):
    return run_exchange(pair_fill_exchange(halves), name)


def reduce_small(dm_f1, dm_mix, dm_gate, dm_f2, loss_blk, name):
    def body(f1_ref, mix_ref, gate_ref, f2_ref, l_ref, tot_ref, rows_ref, fin_ref):
        rows_ref[...] = jnp.zeros_like(rows_ref)
        tot_ref[...] = jnp.zeros_like(tot_ref)
        mod_src = [(f1_ref, 0), (f1_ref, 1), (f1_ref, 2), (mix_ref, 0), (mix_ref, 1), (gate_ref, 2),
                   (f2_ref, 0), (f2_ref, 1), (f2_ref, 2)]
        norm_src = [(f1_ref, 3), (mix_ref, 3), (f2_ref, 3)]
        for l in range(2):
            for k, (ref, r) in enumerate(mod_src + norm_src):
                lat = ref[0, l, 0, r:r + 1, :]
                ctx = ref[0, l, 1, r:r + 1, :]
                for dev in range(N_DEV):
                    if dev:
                        lat = lat + ref[dev, l, 0, r:r + 1, :]
                        ctx = ctx + ref[dev, l, 1, r:r + 1, :]
                    if k < N_MOD:
                        rows_ref[l, dev, k:k + 1, :] = ref[dev, l, 0, r:r + 1, :]
                if k < N_MOD:
                    rows_ref[l, N_DEV, k:k + 1, :] = ctx
                tot_ref[l, k:k + 1, :] = lat + ctx
        acc = l_ref[0]
        for dev in range(1, N_DEV):
            acc = acc + l_ref[dev]
        loss = (0.5 / D) * jnp.sum(acc[1:2, :], axis=1, keepdims=True)
        row = lax.broadcasted_iota(jnp.int32, (8, D), 0)
        fin_ref[...] = jnp.where(row == 0, acc[0:1, :], loss)

    return pl.pallas_call(
        body, name=name,
        out_shape=[jax.ShapeDtypeStruct((2, 16, D), F32), jax.ShapeDtypeStruct((2, 16, 16, D), F32),
                   jax.ShapeDtypeStruct((8, D), F32)],
        compiler_params=_params(),
    )(dm_f1, dm_mix, dm_gate, dm_f2, loss_blk)


def rope_tables(t, s):
    rows = t // GRID_W
    row = jnp.repeat(jnp.arange(rows), GRID_W).astype(F32)
    col = jnp.tile(jnp.arange(GRID_W), rows).astype(F32)
    inv = ROPE_BASE ** (-jnp.arange(0, HEAD // 2, 2, dtype=F32) / (HEAD // 2))
    ang = jnp.concatenate([row[:, None] * inv, col[:, None] * inv], axis=-1)
    cos, sin = jnp.cos(ang), jnp.sin(ang)
    cos = jnp.concatenate([jnp.tile(cos, (1, 4)), jnp.ones((s - t, BLK), F32)], axis=0)
    sin = jnp.concatenate([jnp.tile(jnp.concatenate([-sin, sin], axis=1), (1, 2)), jnp.zeros((s - t, BLK), F32)], axis=0)
    return cos, sin


BIG = ("ffn1_in", "ffn1_out", "w_in", "w_out", "ffn2_in", "ffn2_out")
GROUPS = dict(ffn1=("ffn1_in", "ffn1_out"), mix=("w_in", "w_out"), ffn2=("ffn2_in", "ffn2_out"))
GATHER_BEHIND = {("ffn1", 0): [("w_in", 0), ("ffn2_out", 0), ("ffn1_out", 1)], ("proj", 0): [("w_out", 0)],
                 ("mix", 0): [("ffn2_in", 0)], ("ffn2", 0): [("ffn1_in", 1), ("w_in", 1)],
                 ("ffn1", 1): [("ffn2_in", 1), ("w_out", 1)], ("mix", 1): [("ffn2_out", 1)]}


def _slot_major(name, g):
    if name == "w_in":
        return jnp.stack(jnp.split(g, N_SLOT, axis=1), axis=0)
    if name in ("ffn1_in", "ffn2_in"):
        return g
    return g.reshape(N_SLOT, g.shape[0] // N_SLOT, g.shape[1])


def _whole_weight(name, buf):
    if name == "w_in":
        return buf.transpose(1, 0, 2).reshape(D, PROJ_W)
    if name in ("ffn1_in", "ffn2_in"):
        return buf
    return buf.reshape(-1, buf.shape[2])


def local_step(x1, ctx1, target, mods, norms, nfinal, placed, w_pool, pool_scale, sink, place, small_blocks):
    t, s = x1.shape[0], x1.shape[0] + ctx1.shape[0]
    n_lat = t // TM
    cos, sin = rope_tables(t, s)
    tables = mix_tables(t, s)
    wts ={name: list(pair) for name, pair in placed.items()}

    def gather(tensors):
        return gather_exchange([wts[name][l] for name, l in tensors])

    def gathered(tensors, arrays):
        for (name, l), whole in zip(tensors, arrays):
            wts[name][l] = whole

    def weight(name, l):
        return _whole_weight(name, wts[name][l])

    def fwd_ex(grp, l):
        groups = GATHER_BEHIND.get((grp, l))
        return (groups, gather(groups)) if groups else (None, None)

    h = jnp.concatenate([x1, ctx1], axis=0)
    saved = []
    for l in range(2):
        h0 = h
        groups, ex = fwd_ex("ffn1", l)
        (h1, ab1, f1), got = ffn_fwd(h0, mods, norms[0], weight("ffn1_in", l), weight("ffn1_out", l), l, 0, n_lat, f"ffn1_fwd_{l}", ex)
        gathered(groups or [], got)
        groups, ex = fwd_ex("proj", l)
        (u, q, k, v), got = proj_fwd(h1, mods, norms[1], weight("w_in", l), cos, sin, l, n_lat, f"proj_fwd_{l}", ex)
        gathered(groups or [], got)
        groups, ex = fwd_ex("mix", l)
        (h2, cat, lse, mo), got = mix_fwd(h1, q, k, v, u, w_pool, pool_scale, sink, weight("w_out", l), mods, tables, l, t,
                                          f"mix_fwd_{l}", ex)
        gathered(groups or [], got)
        groups, ex = fwd_ex("ffn2", l)
        (h, ab2, f2), got = ffn_fwd(h2, mods, norms[2], weight("ffn2_in", l), weight("ffn2_out", l), l, 6, n_lat, f"ffn2_fwd_{l}", ex)
        gathered(groups or [], got)
        saved.append((h0, ab1, f1, h1, u, q, k, v, cat, lse, mo, h2, ab2, f2))
    dh, loss_blk = loss_head(h, target, nfinal, t, "loss_head")

    halves = {name: [None, None] for name in BIG}
    pending = []

    def summed_in_pair(grp, l, name_a, g_a, name_b, wgrad_b):
        g_b, got_a = wgrad_b(pair_exchange([_slot_major(name_a, g_a[1])]))
        sum_a, got_b = pair_sum(_slot_major(name_a, g_a[0]), got_a[0], place, f"pair_sum_{name_a}_{l}",
                                pair_exchange([_slot_major(name_b, g_b[1])]))
        sums = {name_a: sum_a, name_b: pair_sum(_slot_major(name_b, g_b[0]), got_b[0], place, f"pair_sum_{name_b}_{l}")}
        pending.append((grp, l, [sums[n] for n in GROUPS[grp]]))

    lacking = []

    def riders():
        return (scatter_exchange([p16 for _, p16 in pending[0][2]]) if pending else None,
                pair_fill_exchange([halves[name][l] for name, l in lacking]) if lacking else None)

    def carried(got, exs):
        got, filled = split_outputs(got, *exs)
        for (name, l), whole in zip(list(lacking), filled):
            halves[name][l] = whole
            lacking.remove((name, l))
        if pending:
            grp, l, pairs = pending.pop(0)
            for i, name in enumerate(GROUPS[grp]):
                halves[name][l] = chip_sum(pairs[i][0], got[3 * i:3 * i + 3], place, f"chip_sum_{name}_{l}")
                lacking.append((name, l))

    small = [None, None]
    for l in (1, 0):
        h0, ab1, f1, h1, u, q, k, v, cat, lse, mo, h2, ab2, f2 = saved[l]
        exs = riders()
        (dh, dab, df, n, act, dm_f2), got = ffn_bwd(h2, ab2, f2, dh, mods, norms[2], weight("ffn2_in", l), weight("ffn2_out", l),
                                                    l, 6, n_lat, f"ffn2_bwd_{l}", both(*exs))
        carried(got, exs)
        g_in, _ = wgrad(n, dab, D // 2, FF_COLS, FF_COLS, f"ffn2_in_wgrad_{l}")
        summed_in_pair("ffn2", l, "ffn2_in", g_in, "ffn2_out",
                       lambda ex, a=act, b=df: wgrad(a, b, D_FF // 2, D // 2, None, f"ffn2_out_wgrad_{l}", ex))
        exs = riders()
        (dq, dk, dv, du, dmo, dwp, dps, dsink, dm_gate), got = mix_bwd(
            dh, mo, q, k, v, u, lse, w_pool, pool_scale, sink, weight("w_out", l), mods, tables, l, t, f"mix_bwd_{l}", both(*exs))
        carried(got, exs)
        g_wo, _ = wgrad(cat, dmo, POOL_W + ATTN_W, D, None, f"w_out_wgrad_{l}")
        dh, dp, n, dm_mix = proj_bwd(h1, du, dq, dk, dv, dh, mods, norms[1], weight("w_in", l), cos, sin, l, n_lat, f"proj_bwd_{l}")
        summed_in_pair("mix", l, "w_out", g_wo, "w_in",
                       lambda ex, a=n, b=dp: wgrad(a, b, D, PROJ_W // 2, None, f"w_in_wgrad_{l}", ex))
        exs = riders()
        (dh, dab, df, n, act, dm_f1), got = ffn_bwd(h0, ab1, f1, dh, mods, norms[0], weight("ffn1_in", l), weight("ffn1_out", l),
                                                    l, 0, n_lat, f"ffn1_bwd_{l}", both(*exs))
        carried(got, exs)
        small[l] = dict(dm_f1=dm_f1, dm_mix=dm_mix, dm_gate=dm_gate, dm_f2=dm_f2, dwp=dwp, dps=dps, dsink=dsink)
        if l:
            g_in, _ = wgrad(n, dab, D // 2, FF_COLS, FF_COLS, f"ffn1_in_wgrad_{l}")
            summed_in_pair("ffn1", l, "ffn1_in", g_in, "ffn1_out",
                           lambda ex, a=act, b=df: wgrad(a, b, D_FF // 2, D // 2, None, f"ffn1_out_wgrad_{l}", ex))
    g_out, _ = wgrad(act, df, D_FF // 2, D // 2, None, "ffn1_out_wgrad_0")
    riding = (gather8_exchange(small_blocks(small, loss_blk)), pair_exchange([_slot_major("ffn1_out", g_out[1])]),
              pair_fill_exchange([halves[name][l] for name, l in lacking]))
    g_in, got = wgrad(n, dab, D // 2, FF_COLS, FF_COLS, "ffn1_in_wgrad_0", both(*riding))
    small_all, got_out, filled = split_outputs(got, *riding)
    for (name, l), whole in zip(lacking, filled):
        halves[name][l] = whole
    got_in = run_exchange(pair_exchange([_slot_major("ffn1_in", g_in[1])]), "pair_exchange_ffn1_in_0")
    last = {"ffn1_in": pair_sum(_slot_major("ffn1_in", g_in[0]), got_in[0], place, "pair_sum_ffn1_in_0"),
            "ffn1_out": pair_sum(_slot_major("ffn1_out", g_out[0]), got_out[0], place, "pair_sum_ffn1_out_0")}
    return dh[:t], halves, last, small_all


def _silu_grad(z):
    sg = jax.nn.sigmoid(z)
    return sg * (1 + z * (1 - sg))


def kernel(x, c, ctx, c_ctx, w_mod, b_mod, norm_ffn1, w_ffn1_in, w_ffn1_out, norm_mix, w_in, w_pool, pool_scale, sink, w_out, norm_ffn2, w_ffn2_in, w_ffn2_out, norm_final, loss_target, m_c_ctx, m_w_mod, m_b_mod, m_norm_ffn1, m_w_ffn1_in, m_w_ffn1_out, m_norm_mix, m_w_in, m_w_pool, m_pool_scale, m_sink, m_w_out, m_norm_ffn2, m_w_ffn2_in, m_w_ffn2_out, m_norm_final, v_c_ctx, v_w_mod, v_b_mod, v_norm_ffn1, v_w_ffn1_in, v_w_ffn1_out, v_norm_mix, v_w_in, v_w_pool, v_pool_scale, v_sink, v_w_out, v_norm_ffn2, v_w_ffn2_in, v_w_ffn2_out, v_norm_final):
    px, py, pc = _place()
    slot, me = 2 * px + py, 4 * px + 2 * py + pc
    n_grp = len(POOL_WINDOWS)

    (c_rows,) = all_gather([c.reshape(8, D // 8)], "gather_c")
    c_all = jnp.concatenate([c_rows.reshape(N_DEV, D), c_ctx.reshape(1, D), jnp.zeros((16 - N_DEV - 1, D), F32)], axis=0)
    b_cols = lax.dynamic_slice(b_mod, (0, slot * MOD_COLS), (2, MOD_COLS)).reshape(2, 1, MOD_COLS)
    (mod_parts,) = all_gather([mod_rows(c_all, w_mod, b_cols, "mod_rows")], "gather_mods")
    mods_all = mod_parts[0::2].transpose(1, 2, 0, 3).reshape(2, 16, N_MOD * D)
    mx = lax.dynamic_slice(mods_all, (0, me, 0), (2, 1, N_MOD * D)).reshape(2, N_MOD, D)
    mc = mods_all[:, N_DEV].reshape(2, N_MOD, D)
    pad = jnp.zeros((2, 16 - N_MOD, D), F32)
    mods = jnp.stack([jnp.concatenate([mx, pad], axis=1), jnp.concatenate([mc, pad], axis=1)], axis=1)

    place = jnp.stack([pc, slot]).astype(jnp.int32)
    shards = dict(ffn1_in=w_ffn1_in, ffn1_out=w_ffn1_out, w_in=w_in, w_out=w_out, ffn2_in=w_ffn2_in, ffn2_out=w_ffn2_out)
    first = [("ffn1_in", 0), ("ffn1_out", 0)]
    placed = {name: [None, None] for name in BIG}
    for name, l in first:
        placed[name][l] = cast_place(shards[name], l, place, f"cast_{name}_{l}")
    send_sems, recv_sems, *bufs, token = gather_start([placed[name][l] for name, l in first], "gather_first_start")
    others = [(name, l) for name in BIG for l in range(2) if (name, l) not in first]
    for name, l in others:
        placed[name][l] = cast_place(shards[name], l, place, f"cast_{name}_{l}")
    bufs = gather_wait(send_sems, recv_sems, bufs, [placed[name][l] for name, l in others], "gather_first_wait")
    for (name, l), whole in zip(first, run_exchange(pass_on_exchange(bufs), "gather_first_pass_on")):
        placed[name][l] = whole
    norms = [g.reshape(2, 1, D) for g in (norm_ffn1, norm_mix, norm_ffn2)]
    row_sums = ("dm_f1", "dm_mix", "dm_gate", "dm_f2")

    def small_blocks(small, loss_blk):
        stacked = {k: jnp.stack([small[0][k], small[1][k]]) for k in row_sums + ("dwp", "dps", "dsink")}
        return ([stacked[k].reshape(32, D) for k in row_sums]
                + [stacked["dwp"].reshape(2 * n_grp * GROUP, GROUP), stacked["dps"].reshape(16, POOL_W),
                   stacked["dsink"].reshape(16, BLK), loss_blk])

    dx, halves, last, small_all = local_step(x[0], ctx[0], loss_target[0], mods, norms, norm_final.reshape(1, D), placed,
                                                   w_pool.astype(BF16), pool_scale.reshape(2, 1, POOL_W), sink, place, small_blocks)
    grads = {}

    *g_dm, g_dwp, g_dps, g_dsink, g_loss = small_all
    tot, rows, fin = reduce_small(*[g.reshape(N_DEV, 2, 2, 8, D) for g in g_dm], g_loss, "reduce_small")
    s_dwp, s_dps, s_dsink = sum8([g_dwp, g_dps, g_dsink], "sum_pool_sink")
    grads.update(
        w_pool=s_dwp.reshape(2, n_grp, GROUP, GROUP), pool_scale=s_dps.reshape(2, 8, POOL_W)[:, 0],
        sink=s_dsink.reshape(2, 8, BLK)[:, 0, :N_HEADS], b_mod=tot[:, :N_MOD].reshape(2, N_MOD * D),
        norm_ffn1=tot[:, N_MOD], norm_mix=tot[:, N_MOD + 1], norm_ffn2=tot[:, N_MOD + 2], norm_final=fin[0])
    loss = fin[1, 0]

    dmod_cols = lax.dynamic_slice(rows[:, :, :N_MOD, :].reshape(2, 16, N_MOD * D), (0, 0, slot * MOD_COLS), (2, 16, MOD_COLS))
    grads["w_mod"], dc = mod_grads(c_all, dmod_cols, w_mod, "mod_grads")
    (g_dc,) = all_gather([dc], "gather_dc")
    (s_dc,) = sum8([g_dc], "sum_dc")
    (d_c_ctx,) = elementwise(lambda d, z: (0.5 * d * _silu_grad(z),), [s_dc[N_DEV:N_DEV + 1], c_ctx.reshape(1, D)], [F32], "c_ctx_grad")
    started = {name: scatter_start(last[name][1], f"scatter_last_start_{name}") for name in last}
    grads["c_ctx"] = d_c_ctx.reshape(D) + sum(st[-1][0, :1] for st in started.values())

    given = dict(c_ctx=(c_ctx, m_c_ctx, v_c_ctx), w_mod=(w_mod, m_w_mod, v_w_mod), b_mod=(b_mod, m_b_mod, v_b_mod),
                 norm_ffn1=(norm_ffn1, m_norm_ffn1, v_norm_ffn1), w_ffn1_in=(w_ffn1_in, m_w_ffn1_in, v_w_ffn1_in),
                 w_ffn1_out=(w_ffn1_out, m_w_ffn1_out, v_w_ffn1_out), norm_mix=(norm_mix, m_norm_mix, v_norm_mix),
                 w_in=(w_in, m_w_in, v_w_in), w_pool=(w_pool, m_w_pool, v_w_pool),
                 pool_scale=(pool_scale, m_pool_scale, v_pool_scale), sink=(sink, m_sink, v_sink), w_out=(w_out, m_w_out, v_w_out),
                 norm_ffn2=(norm_ffn2, m_norm_ffn2, v_norm_ffn2), w_ffn2_in=(w_ffn2_in, m_w_ffn2_in, v_w_ffn2_in),
                 w_ffn2_out=(w_ffn2_out, m_w_ffn2_out, v_w_ffn2_out), norm_final=(norm_final, m_norm_final, v_norm_final))
    shard = {(name, l): halves[name][l] for name in BIG for l in range(2)}

    def update(name):
        w, m, v = given[name]
        if name in BIG or name[2:] in BIG:
            key = name if name in BIG else name[2:]
            return adamw_layers(w, shard[key, 0], shard[key, 1], m, v, f"adamw_{name}")
        return [grads[name], *adamw(w, grads[name], m, v, f"adamw_{name}")]

    done = {name: update(name) for name in given if name[2:] not in last}
    between = [done[name][3] for name in done if name[2:] in BIG or name in BIG] + [done["w_mod"][3]]
    summed = []
    for name, (send_sems, recv_sems, src_thru, *lands, _) in started.items():
        got = scatter_wait(send_sems, recv_sems, src_thru, lands, between, f"scatter_last_wait_{name}")
        summed.append(chip_sum(last[name][0], got, place, f"chip_sum_{name}_0"))
    for name, whole in zip(started, pair_gather(summed, "grad_pair_gather_last")):
        shard[name, 0] = whole
        done["w_" + name] = update("w_" + name)
    return (loss, dx[None], *[done[name][i] for i in range(4) for name in given])
```

```python
import functools

import jax
import jax.numpy as jnp
from jax import lax
from jax.experimental import pallas as pl
from jax.experimental.pallas import tpu as pltpu

F32, BF16 = jnp.float32, jnp.bfloat16
D = 1024
D_FF = 2816
N_SLOT = 4
FF_COLS = 2 * D_FF // N_SLOT
N_MOD = 9
MOD_COLS = N_MOD * D // N_SLOT
POOL_W, ATTN_W, KV_W = 512, 512, 128
PROJ_W = POOL_W + ATTN_W + 2 * KV_W
N_HEADS, Q_GROUP, HEAD = 8, 4, 64
GROUP = 128
POOL_WINDOWS = (2, 4, 8, 16)
BLK = 128
QB = 256
WIN = QB + 2 * BLK
GRID_W = 64
ROPE_BASE = 10000.0
EPS = 1e-6
NEG_INF = -1e30
TM = 256
N_DEV = 8
VMEM_LIMIT_BYTES = 56 * 1024 * 1024
WGRAD_VMEM_BYTES = 44 * 1024 * 1024
ADAM_LR, ADAM_B1, ADAM_B2, ADAM_EPS, ADAM_WD, ADAM_STEP = 0.001, 0.9, 0.999, 1e-08, 0.01, 10
MESH = pl.DeviceIdType.MESH
NT = (((1,), (1,)), ((), ()))
TN = (((0,), (0,)), ((), ()))


def _params(*sem):
    return pltpu.CompilerParams(dimension_semantics=sem, vmem_limit_bytes=VMEM_LIMIT_BYTES)


def _whole(shape, lead=()):
    idx = tuple(lead) + (0,) * len(shape)
    return pl.BlockSpec((None,) * len(lead) + tuple(shape), lambda *_: idx, pipeline_mode=pl.Buffered(1))


def _rows(cols, tm=TM):
    return pl.BlockSpec((tm, cols), lambda i: (i, 0))


def _mods_spec(layer, n_lat):
    return pl.BlockSpec((None, None, 16, D), lambda i: (layer, (i >= n_lat).astype(jnp.int32), 0, 0))


def _acc_spec(n_lat):
    return pl.BlockSpec((None, 8, D), lambda i: ((i >= n_lat).astype(jnp.int32), 0, 0))


def _dot(a, b):
    return jnp.dot(a, b, preferred_element_type=F32)


def _dotg(a, b, dims):
    return lax.dot_general(a, b, dims, preferred_element_type=F32)


def _sum0(v):
    return jnp.sum(v, axis=0, keepdims=True)


def _norm_mod(h, g, shift, scale):
    r = lax.rsqrt(jnp.mean(h * h, axis=-1, keepdims=True) + EPS)
    xhat = h * r
    y = xhat * g
    return y * (1 + scale) + shift, xhat, r, y


def _norm_mod_bwd(dn, xhat, r, y, g, scale):
    dy = dn * (1 + scale)
    dx = dy * g
    dh = r * (dx - xhat * jnp.mean(dx * xhat, axis=-1, keepdims=True))
    return _sum0(dn), _sum0(dn * y), _sum0(dy * xhat), dh


def _swap_halves(v):
    w = v.shape[1]
    lane = lax.broadcasted_iota(jnp.int32, v.shape, 1)
    return jnp.where(lane % HEAD < HEAD // 2, pltpu.roll(v, w - HEAD // 2, axis=1), pltpu.roll(v, HEAD // 2, axis=1))


def _tile_lanes(t, width):
    return t if width == t.shape[1] else jnp.concatenate([t] * (width // t.shape[1]), axis=1)


def _rope(v, cos, sin):
    return v * _tile_lanes(cos, v.shape[1]) + _swap_halves(v) * _tile_lanes(sin, v.shape[1])


def _unrope(g, cos, sin):
    return g * _tile_lanes(cos, g.shape[1]) + _swap_halves(g * _tile_lanes(sin, g.shape[1]))


def ffn_fwd(h, mods, g, w4, wo, layer, k0, n_lat, name, ex=None):
    s = h.shape[0]

    def body(h_ref, m_ref, g_ref, w_ref, wo_ref, ho_ref, ab_ref, f_ref):
        hh = h_ref[...]
        n, _, _, _ = _norm_mod(hh, g_ref[...], m_ref[k0:k0 + 1, :], m_ref[k0 + 1:k0 + 2, :])
        nb = n.astype(BF16)
        acc = jnp.zeros((TM, D), F32)
        for j in range(2):
            a = _dot(nb, w_ref[j])
            b = _dot(nb, w_ref[2 + j])
            ab_ref[:, j * FF_COLS:(j + 1) * FF_COLS] = a.astype(BF16)
            ab_ref[:, (2 + j) * FF_COLS:(3 + j) * FF_COLS] = b.astype(BF16)
            act = (a * jax.nn.sigmoid(a) * b).astype(BF16)
            acc = acc + _dot(act, wo_ref[j * FF_COLS:(j + 1) * FF_COLS, :])
        f_ref[...] = acc
        ho_ref[...] = hh + 0.5 * m_ref[k0 + 2:k0 + 3, :] * acc

    return _grid_call(
        body, name, s // TM,
        [_rows(D), _mods_spec(layer, n_lat), _whole((1, D), (layer,)), _whole((N_SLOT, D, FF_COLS)), _whole((D_FF, D))],
        [_rows(D), _rows(2 * D_FF), _rows(D)],
        [jax.ShapeDtypeStruct((s, D), F32), jax.ShapeDtypeStruct((s, 2 * D_FF), BF16), jax.ShapeDtypeStruct((s, D), F32)],
        (h, mods, g, w4, wo), "parallel", ex)


def ffn_bwd(h, ab, f, dh, mods, g, w4, wo, layer, k0, n_lat, name, ex=None):
    s = h.shape[0]

    def body(h_ref, ab_ref, f_ref, dh_ref, m_ref, g_ref, w_ref, wo_ref, dhi_ref, dab_ref, df_ref, n_ref, act_ref, dm_ref):
        i = pl.program_id(0)

        @pl.when((i == 0) | (i == n_lat))
        def _():
            dm_ref[...] = jnp.zeros_like(dm_ref)

        hh, dho, gg = h_ref[...], dh_ref[...], g_ref[...]
        scale, gate = m_ref[k0 + 1:k0 + 2, :], m_ref[k0 + 2:k0 + 3, :]
        n, xhat, r, y = _norm_mod(hh, gg, m_ref[k0:k0 + 1, :], scale)
        n_ref[...] = n.astype(BF16)
        dgate = _sum0(dho * (0.5 * f_ref[...]))
        dfb = ((0.5 * gate) * dho).astype(BF16)
        df_ref[...] = dfb
        dn = jnp.zeros((TM, D), F32)
        for j in range(2):
            a = ab_ref[:, j * FF_COLS:(j + 1) * FF_COLS].astype(F32)
            b = ab_ref[:, (2 + j) * FF_COLS:(3 + j) * FF_COLS].astype(F32)
            sg = jax.nn.sigmoid(a)
            sa = a * sg
            act_ref[:, j * FF_COLS:(j + 1) * FF_COLS] = (sa * b).astype(BF16)
            dact = _dotg(dfb, wo_ref[j * FF_COLS:(j + 1) * FF_COLS, :], NT)
            da = (dact * b * (sg * (1 + a * (1 - sg)))).astype(BF16)
            db = (dact * sa).astype(BF16)
            dab_ref[:, j * FF_COLS:(j + 1) * FF_COLS] = da
            dab_ref[:, (2 + j) * FF_COLS:(3 + j) * FF_COLS] = db
            dn = dn + _dotg(da, w_ref[j], NT) + _dotg(db, w_ref[2 + j], NT)
        dsh, dsc, dg, dhn = _norm_mod_bwd(dn, xhat, r, y, gg, scale)
        dhi_ref[...] = dho + dhn
        dm_ref[0:1, :] += dsh
        dm_ref[1:2, :] += dsc
        dm_ref[2:3, :] += dgate
        dm_ref[3:4, :] += dg

    return _grid_call(
        body, name, s // TM,
        [_rows(D), _rows(2 * D_FF), _rows(D), _rows(D), _mods_spec(layer, n_lat), _whole((1, D), (layer,)),
         _whole((N_SLOT, D, FF_COLS)), _whole((D_FF, D))],
        [_rows(D), _rows(2 * D_FF), _rows(D), _rows(D), _rows(D_FF), _acc_spec(n_lat)],
        [jax.ShapeDtypeStruct((s, D), F32), jax.ShapeDtypeStruct((s, 2 * D_FF), BF16), jax.ShapeDtypeStruct((s, D), BF16),
         jax.ShapeDtypeStruct((s, D), BF16), jax.ShapeDtypeStruct((s, D_FF), BF16), jax.ShapeDtypeStruct((2, 8, D), F32)],
        (h, ab, f, dh, mods, g, w4, wo), "arbitrary", ex)


def _token_tile(s, limit=2176):
    return max(ts for ts in range(16, limit + 1, 16) if s % ts == 0)


def wgrad(a, b, tk, tn, slot_cols, name, ex=None):
    s, k = a.shape
    n = b.shape[1]
    a_bufs, b_bufs = (1 if k == tk else 2), (1 if n == tn else 2)
    whole = 2 * s * (a_bufs * tk + b_bufs * tn) + 2 * 6 * tk * tn
    ts = s if whole <= WGRAD_VMEM_BYTES else _token_tile(s)
    steps = s // ts
    once = dict(pipeline_mode=pl.Buffered(1))

    def body(a_ref, b_ref, o_ref, o16_ref):
        r = _dotg(a_ref[...], b_ref[...], TN)
        si = pl.program_id(2)

        @pl.when(si == 0)
        def _():
            o_ref[...] = r

        @pl.when(si > 0)
        def _():
            o_ref[...] += r

        @pl.when(si == steps - 1)
        def _():
            o16_ref[...] = o_ref[...].astype(BF16)

    n_outer = tn > tk
    grid = (n // tn, k // tk, steps) if n_outer else (k // tk, n // tn, steps)
    ij = (lambda g0, g1: (g1, g0)) if n_outer else (lambda g0, g1: (g0, g1))

    def a_map(g0, g1, si):
        return si, ij(g0, g1)[0]

    def b_map(g0, g1, si):
        return si, ij(g0, g1)[1]

    if slot_cols is None:
        shape, spec = (k, n), pl.BlockSpec((tk, tn), lambda g0, g1, si: ij(g0, g1))
    else:
        per = slot_cols // tn

        def slot_map(g0, g1, si):
            i, j = ij(g0, g1)
            return lax.div(j, per), i, lax.rem(j, per)

        shape, spec = (n // slot_cols, k, slot_cols), pl.BlockSpec((None, tk, tn), slot_map)
    return _grid_call(
        body, name, grid,
        [pl.BlockSpec((ts, tk), a_map, **(once if a_bufs == 1 and steps == 1 else {})),
         pl.BlockSpec((ts, tn), b_map, **(once if b_bufs == 1 and steps == 1 else {}))], [spec, spec],
        [jax.ShapeDtypeStruct(shape, F32), jax.ShapeDtypeStruct(shape, BF16)], (a, b), ("parallel", "parallel", "arbitrary"), ex)


def proj_fwd(h, mods, g, w_in, cos, sin, layer, n_lat, name, ex=None):
    s = h.shape[0]

    def body(h_ref, m_ref, g_ref, w_ref, cos_ref, sin_ref, u_ref, q_ref, k_ref, v_ref):
        n, _, _, _ = _norm_mod(h_ref[...], g_ref[...], m_ref[3:4, :], m_ref[4:5, :])
        p = _dot(n.astype(BF16), w_ref[...])
        cs, sn = cos_ref[...], sin_ref[...]
        u_ref[...] = p[:, :POOL_W]
        q_ref[...] = (_rope(p[:, POOL_W:POOL_W + ATTN_W], cs, sn) * HEAD ** -0.5).astype(BF16)
        k_ref[...] = _rope(p[:, POOL_W + ATTN_W:POOL_W + ATTN_W + KV_W], cs, sn).astype(BF16)
        v_ref[...] = p[:, POOL_W + ATTN_W + KV_W:].astype(BF16)

    return _grid_call(
        body, name, s // TM,
        [_rows(D), _mods_spec(layer, n_lat), _whole((1, D), (layer,)), _whole((D, PROJ_W)), _rows(BLK), _rows(BLK)],
        [_rows(POOL_W), _rows(ATTN_W), _rows(KV_W), _rows(KV_W)],
        [jax.ShapeDtypeStruct((s, POOL_W), F32), jax.ShapeDtypeStruct((s, ATTN_W), BF16),
         jax.ShapeDtypeStruct((s, KV_W), BF16), jax.ShapeDtypeStruct((s, KV_W), BF16)],
        (h, mods, g, w_in, cos, sin), "parallel", ex)


def proj_bwd(h, du, dq, dk, dv, dh, mods, g, w_in, cos, sin, layer, n_lat, name):
    s = h.shape[0]

    def body(h_ref, du_ref, dq_ref, dk_ref, dv_ref, dh_ref, m_ref, g_ref, w_ref, cos_ref, sin_ref,
             dhi_ref, dp_ref, n_ref, dm_ref):
        i = pl.program_id(0)

        @pl.when((i == 0) | (i == n_lat))
        def _():
            dm_ref[...] = jnp.zeros_like(dm_ref)

        gg, scale = g_ref[...], m_ref[4:5, :]
        n, xhat, r, y = _norm_mod(h_ref[...], gg, m_ref[3:4, :], scale)
        n_ref[...] = n.astype(BF16)
        cs, sn = cos_ref[...], sin_ref[...]
        dp = jnp.concatenate([du_ref[...], _unrope(dq_ref[...], cs, sn) * HEAD ** -0.5, _unrope(dk_ref[...], cs, sn),
                              dv_ref[...]], axis=1).astype(BF16)
        dp_ref[...] = dp
        dsh, dsc, dg, dhn = _norm_mod_bwd(_dotg(dp, w_ref[...], NT), xhat, r, y, gg, scale)
        dhi_ref[...] = dh_ref[...] + dhn
        dm_ref[0:1, :] += dsh
        dm_ref[1:2, :] += dsc
        dm_ref[3:4, :] += dg

    return pl.pallas_call(
        body, name=name, grid=(s // TM,),
        in_specs=[_rows(D), _rows(POOL_W), _rows(ATTN_W), _rows(KV_W), _rows(KV_W), _rows(D), _mods_spec(layer, n_lat),
                  _whole((1, D), (layer,)), _whole((D, PROJ_W)), _rows(BLK), _rows(BLK)],
        out_specs=[_rows(D), _rows(PROJ_W), _rows(D), _acc_spec(n_lat)],
        out_shape=[jax.ShapeDtypeStruct((s, D), F32), jax.ShapeDtypeStruct((s, PROJ_W), BF16),
                   jax.ShapeDtypeStruct((s, D), BF16), jax.ShapeDtypeStruct((2, 8, D), F32)],
        compiler_params=_params("arbitrary"),
    )(h, du, dq, dk, dv, dh, mods, g, w_in, cos, sin)


def _window(i, s):
    return pl.multiple_of(jnp.clip(i * QB - BLK, 0, s - WIN), BLK)


def mix_tables(t, s):
    n_lat = t // QB
    blocks = jnp.array([0, 1, n_lat - 1] + list(range(n_lat, s // QB)))[:, None, None]
    ws = jnp.clip(blocks * QB - BLK, 0, s - WIN)
    q = blocks * QB + jnp.arange(QB)[None, :, None]
    k = ws + jnp.arange(WIN)[None, None, :]
    is_lat = blocks < n_lat
    local = jnp.where(is_lat & (k < t) & (jnp.abs(k - q) <= BLK), 0.0, NEG_INF).astype(F32)
    bias = jnp.concatenate([local, jnp.zeros(local.shape[:2] + (s - t,), F32)], axis=2)
    seq_lo, seq_hi = jnp.where(is_lat, 0, t), jnp.where(is_lat, t, s)
    bands, counts = [], []
    for w in POOL_WINDOWS:
        lo, hi = jnp.maximum(q - w // 2, seq_lo), jnp.minimum(q + w - w // 2, seq_hi)
        bands.append((k >= lo) & (k < hi))
        counts.append((hi - lo).astype(F32))
    band = jnp.stack(bands, axis=1).astype(BF16)
    count = jnp.concatenate(counts + [jnp.ones(counts[0].shape[:2] + (BLK - len(counts),), F32)], axis=2)
    return dict(bias=bias, band=band, band_t=band.transpose(0, 1, 3, 2), count=count)


def _case_spec(table, n_lat_blk):
    def kind(i):
        return jnp.where(i < n_lat_blk - 1, jnp.minimum(i, 1), i - n_lat_blk + 3)

    shape = table.shape[1:]
    return pl.BlockSpec((None,) + shape, lambda i: (kind(i),) + (0,) * len(shape))


def _split_dot(band, v):
    return _dot(band, v.astype(BF16))


def _pooled(u_ref, band_ref, cnt_ref, i, ws, gi):
    cols = slice(gi * GROUP, (gi + 1) * GROUP)
    mean = _split_dot(band_ref[gi], u_ref[pl.ds(ws, WIN), cols]) / cnt_ref[:, gi:gi + 1]
    return mean - u_ref[pl.ds(pl.multiple_of(i * QB, QB), QB), cols]


def _head_cols(hd):
    return slice(hd * HEAD, (hd + 1) * HEAD)


def _stack_heads(x, hk, first=0):
    return jnp.concatenate([x[:, first + (Q_GROUP * hk + g) * HEAD:first + (Q_GROUP * hk + g + 1) * HEAD]
                            for g in range(Q_GROUP)], axis=0)


def _biased(scores, bias):
    return (scores.reshape(Q_GROUP, QB, -1) + bias).reshape(Q_GROUP * QB, -1)


def _group_column(vals):
    row = lax.broadcasted_iota(jnp.int32, (Q_GROUP * QB, 1), 0)
    out = jnp.full((Q_GROUP * QB, 1), vals[Q_GROUP - 1], F32)
    for g in range(Q_GROUP - 2, -1, -1):
        out = jnp.where(row < (g + 1) * QB, vals[g], out)
    return out


def _lane_place(cols, width=BLK):
    lane = lax.broadcasted_iota(jnp.int32, (cols[0].shape[0], width), 1)
    out = jnp.zeros((cols[0].shape[0], width), F32)
    for hd, c in enumerate(cols):
        out = jnp.where(lane == hd, c, out)
    return out


def mix_fwd(h, q, k, v, u, w_pool, pool_scale, sink, w_out, mods, tables, layer, t, name, ex=None):
    s = h.shape[0]
    n_lat_blk = t // QB

    def body(h_ref, q_ref, k_ref, v_ref, u_ref, wp_ref, ps_ref, sink_ref, wo_ref, m_ref, bias_ref, band_ref, cnt_ref,
             ho_ref, cat_ref, lse_ref, mo_ref):
        i = pl.program_id(0)
        ws = _window(i, s)
        for gi in range(len(POOL_WINDOWS)):
            mixed = _dot(_pooled(u_ref, band_ref, cnt_ref, i, ws, gi).astype(BF16), wp_ref[gi])
            cat_ref[:, gi * GROUP:(gi + 1) * GROUP] = (mixed * ps_ref[:, gi * GROUP:(gi + 1) * GROUP]).astype(BF16)
        bias = bias_ref[...]
        k_all = jnp.concatenate([k_ref[pl.ds(ws, WIN), :], k_ref[t:s, :]], axis=0)
        v_all = jnp.concatenate([v_ref[pl.ds(ws, WIN), :], v_ref[t:s, :]], axis=0)
        lses = []
        for hk in range(N_HEADS // Q_GROUP):
            kv = _head_cols(hk)
            sc = _biased(_dotg(_stack_heads(q_ref[...], hk), k_all[:, kv], NT), bias)
            sk = _group_column([sink_ref[layer, Q_GROUP * hk + g] for g in range(Q_GROUP)])
            m = jnp.maximum(jnp.max(sc, axis=1, keepdims=True), sk)
            e = jnp.exp(sc - m)
            l = jnp.sum(e, axis=1, keepdims=True) + jnp.exp(sk - m)
            o = _dot(e.astype(BF16), v_all[:, kv]) * (1.0 / l)
            lse = m + jnp.log(l)
            for g in range(Q_GROUP):
                hd = Q_GROUP * hk + g
                cat_ref[:, POOL_W + hd * HEAD:POOL_W + (hd + 1) * HEAD] = o[g * QB:(g + 1) * QB].astype(BF16)
                lses.append(lse[g * QB:(g + 1) * QB])
        lse_ref[...] = _lane_place(lses)
        mo = _dot(cat_ref[...], wo_ref[...])
        mo_ref[...] = mo
        ho_ref[...] = h_ref[...] + m_ref[5:6, :] * mo

    blk = lambda cols: _rows(cols, QB)
    return _grid_call(
        body, name, s // QB,
        [blk(D), blk(ATTN_W), _whole((s, KV_W)), _whole((s, KV_W)), _whole((s, POOL_W)),
         _whole((len(POOL_WINDOWS), GROUP, GROUP), (layer,)), _whole((1, POOL_W), (layer,)),
         pl.BlockSpec(memory_space=pltpu.SMEM), _whole((POOL_W + ATTN_W, D)), _mods_spec(layer, n_lat_blk),
         _case_spec(tables["bias"], n_lat_blk), _case_spec(tables["band"], n_lat_blk), _case_spec(tables["count"], n_lat_blk)],
        [blk(D), blk(POOL_W + ATTN_W), blk(BLK), blk(D)],
        [jax.ShapeDtypeStruct((s, D), F32), jax.ShapeDtypeStruct((s, POOL_W + ATTN_W), BF16), jax.ShapeDtypeStruct((s, BLK), F32),
         jax.ShapeDtypeStruct((s, D), F32)],
        (h, q, k, v, u, w_pool, pool_scale, sink, w_out, mods, tables["bias"], tables["band"], tables["count"]), "parallel", ex)


def mix_bwd(dh, mo, q, k, v, u, lse, w_pool, pool_scale, sink, w_out, mods, tables, layer, t, name, ex=None):
    s = dh.shape[0]
    n_lat_blk = t // QB
    n_grp = len(POOL_WINDOWS)

    def body(dh_ref, mo_ref, q_ref, k_ref, v_ref, u_ref, lse_ref, wp_ref, ps_ref, sink_ref, wo_ref, m_ref,
             bias_ref, band_ref, band_t_ref, cnt_ref,
             dq_ref, dk_ref, dv_ref, du_ref, dmo_ref, dwp_ref, dps_ref, dsink_ref, dm_ref):
        i = pl.program_id(0)

        @pl.when(i == 0)
        def _():
            for ref in (dk_ref, dv_ref, du_ref, dwp_ref, dps_ref, dsink_ref):
                ref[...] = jnp.zeros_like(ref)

        @pl.when((i == 0) | (i == n_lat_blk))
        def _():
            dm_ref[...] = jnp.zeros_like(dm_ref)

        ws = _window(i, s)
        here = pl.ds(pl.multiple_of(i * QB, QB), QB)
        dho = dh_ref[...]
        dm_ref[2:3, :] += _sum0(dho * mo_ref[...])
        dmo = (m_ref[5:6, :] * dho).astype(BF16)
        dmo_ref[...] = dmo
        dcat = _dotg(dmo, wo_ref[...], NT)

        for gi in range(n_grp):
            cols = slice(gi * GROUP, (gi + 1) * GROUP)
            pooled = _pooled(u_ref, band_ref, cnt_ref, i, ws, gi).astype(BF16)
            dpo = dcat[:, cols]
            dps_ref[0:1, cols] += _sum0(dpo * _dot(pooled, wp_ref[gi]))
            dmixed = (dpo * ps_ref[:, cols]).astype(BF16)
            dwp_ref[gi] += _dotg(pooled, dmixed, TN)
            dpooled = _dotg(dmixed, wp_ref[gi], NT)
            du_ref[pl.ds(ws, WIN), cols] += _split_dot(band_t_ref[gi], dpooled / cnt_ref[:, gi:gi + 1])
            du_ref[here, cols] -= dpooled

        bias = bias_ref[...]
        k_all = jnp.concatenate([k_ref[pl.ds(ws, WIN), :], k_ref[t:s, :]], axis=0)
        v_all = jnp.concatenate([v_ref[pl.ds(ws, WIN), :], v_ref[t:s, :]], axis=0)
        qq, lse_all = q_ref[...], lse_ref[...]
        dqs, dsinks, dks, dvs = [], [], [], []
        for hk in range(N_HEADS // Q_GROUP):
            kv = _head_cols(hk)
            q4 = _stack_heads(qq, hk)
            lse = jnp.concatenate([lse_all[:, Q_GROUP * hk + g:Q_GROUP * hk + g + 1] for g in range(Q_GROUP)], axis=0)
            p = jnp.exp(_biased(_dotg(q4, k_all[:, kv], NT), bias) - lse)
            do = _stack_heads(dcat, hk, POOL_W).astype(BF16)
            dp = _dotg(do, v_all[:, kv], NT)
            delta = jnp.sum(p * dp, axis=1, keepdims=True)
            ds = (p * (dp - delta)).astype(BF16)
            sk = _group_column([sink_ref[layer, Q_GROUP * hk + g] for g in range(Q_GROUP)])
            dsk = -jnp.exp(sk - lse) * delta
            dq = _dot(ds, k_all[:, kv])
            for g in range(Q_GROUP):
                dqs.append(dq[g * QB:(g + 1) * QB])
                dsinks.append(_sum0(dsk[g * QB:(g + 1) * QB]))
            dks.append(_dotg(ds, q4, TN))
            dvs.append(_dotg(p.astype(BF16), do, TN))
        dq_ref[...] = jnp.concatenate(dqs, axis=1)
        dk, dv = jnp.concatenate(dks, axis=1), jnp.concatenate(dvs, axis=1)
        dk_ref[pl.ds(ws, WIN), :] += dk[:WIN]
        dv_ref[pl.ds(ws, WIN), :] += dv[:WIN]
        dk_ref[t:s, :] += dk[WIN:]
        dv_ref[t:s, :] += dv[WIN:]
        dsink_ref[0:1, :] += _lane_place(dsinks)

    blk = lambda cols: _rows(cols, QB)
    full = lambda shape: pl.BlockSpec(shape, lambda i: (0,) * len(shape))
    return _grid_call(
        body, name, s // QB,
        [blk(D), blk(D), blk(ATTN_W), _whole((s, KV_W)), _whole((s, KV_W)), _whole((s, POOL_W)),
         blk(BLK), _whole((n_grp, GROUP, GROUP), (layer,)), _whole((1, POOL_W), (layer,)),
         pl.BlockSpec(memory_space=pltpu.SMEM), _whole((POOL_W + ATTN_W, D)), _mods_spec(layer, n_lat_blk)]
        + [_case_spec(tables[key], n_lat_blk) for key in ("bias", "band", "band_t", "count")],
        [blk(ATTN_W), full((s, KV_W)), full((s, KV_W)), full((s, POOL_W)), blk(D),
         full((n_grp, GROUP, GROUP)), full((8, POOL_W)), full((8, BLK)), _acc_spec(n_lat_blk)],
        [jax.ShapeDtypeStruct((s, ATTN_W), F32), jax.ShapeDtypeStruct((s, KV_W), F32),
         jax.ShapeDtypeStruct((s, KV_W), F32), jax.ShapeDtypeStruct((s, POOL_W), F32),
         jax.ShapeDtypeStruct((s, D), BF16), jax.ShapeDtypeStruct((n_grp, GROUP, GROUP), F32),
         jax.ShapeDtypeStruct((8, POOL_W), F32), jax.ShapeDtypeStruct((8, BLK), F32), jax.ShapeDtypeStruct((2, 8, D), F32)],
        (dh, mo, q, k, v, u, lse, w_pool, pool_scale, sink, w_out, mods, tables["bias"], tables["band"], tables["band_t"],
         tables["count"]), "arbitrary", ex)


def loss_head(h, target, g, t, name):
    s = h.shape[0]
    n_lat = t // TM

    def body(h_ref, t_ref, g_ref, dh_ref, acc_ref):
        i = pl.program_id(0)

        @pl.when(i == 0)
        def _():
            acc_ref[...] = jnp.zeros_like(acc_ref)

        @pl.when(i < n_lat)
        def _():
            hh, gg = h_ref[...], g_ref[...]
            r = lax.rsqrt(jnp.mean(hh * hh, axis=-1, keepdims=True) + EPS)
            xhat = hh * r
            err = xhat * gg - t_ref[...]
            dy = err * (1.0 / D)
            dx = dy * gg
            dh_ref[...] = r * (dx - xhat * jnp.mean(dx * xhat, axis=-1, keepdims=True))
            acc_ref[0:1, :] += _sum0(dy * xhat)
            acc_ref[1:2, :] += _sum0(err * err)

        @pl.when(i >= n_lat)
        def _():
            dh_ref[...] = jnp.zeros_like(dh_ref)

    return pl.pallas_call(
        body, name=name, grid=(s // TM,),
        in_specs=[_rows(D), pl.BlockSpec((TM, D), lambda i: (jnp.minimum(i, n_lat - 1), 0)), _whole((1, D))],
        out_specs=[_rows(D), pl.BlockSpec((8, D), lambda i: (0, 0))],
        out_shape=[jax.ShapeDtypeStruct((s, D), F32), jax.ShapeDtypeStruct((8, D), F32)],
        compiler_params=_params("arbitrary"),
    )(h, target, g)


def mod_rows(c_all, w_mod, b_cols, name):
    def body(c_ref, w_ref, b_ref, o_ref):
        cc = c_ref[...]
        o_ref[...] = _dot((cc * jax.nn.sigmoid(cc)).astype(BF16), w_ref[...].astype(BF16)) + b_ref[...]

    return pl.pallas_call(
        body, name=name, grid=(2,),
        in_specs=[pl.BlockSpec((16, D), lambda l: (0, 0)), pl.BlockSpec((None, D, MOD_COLS), lambda l: (l, 0, 0)),
                  pl.BlockSpec((None, 1, MOD_COLS), lambda l: (l, 0, 0))],
        out_specs=pl.BlockSpec((None, 16, MOD_COLS), lambda l: (l, 0, 0)),
        out_shape=jax.ShapeDtypeStruct((2, 16, MOD_COLS), F32),
        compiler_params=_params("parallel"),
    )(c_all, w_mod, b_cols)


def mod_grads(c_all, dmod_cols, w_mod, name):
    def body(c_ref, d_ref, w_ref, dw_ref, dc_ref):
        @pl.when(pl.program_id(0) == 0)
        def _():
            dc_ref[...] = jnp.zeros_like(dc_ref)

        cc = c_ref[...]
        dd = d_ref[...].astype(BF16)
        dw_ref[...] = _dotg((cc * jax.nn.sigmoid(cc)).astype(BF16), dd, TN)
        dc_ref[...] += _dotg(dd, w_ref[...].astype(BF16), NT)

    return pl.pallas_call(
        body, name=name, grid=(2,),
        in_specs=[pl.BlockSpec((16, D), lambda l: (0, 0)), pl.BlockSpec((None, 16, MOD_COLS), lambda l: (l, 0, 0)),
                  pl.BlockSpec((None, D, MOD_COLS), lambda l: (l, 0, 0))],
        out_specs=[pl.BlockSpec((None, D, MOD_COLS), lambda l: (l, 0, 0)), pl.BlockSpec((16, D), lambda l: (0, 0))],
        out_shape=[jax.ShapeDtypeStruct((2, D, MOD_COLS), F32), jax.ShapeDtypeStruct((16, D), F32)],
        compiler_params=_params("arbitrary"),
    )(c_all, dmod_cols, w_mod)


def _row_tile(rows, cols, n_arrays):
    budget = VMEM_LIMIT_BYTES // 4 // (2 * 4 * n_arrays * cols)
    best = None
    for tr in range(16, rows + 1, 16):
        if rows % tr == 0 and tr <= budget:
            best = tr
    return best if best is not None else rows


def elementwise(fn, ins, out_dtypes, name, ex=None):
    rows, cols = ins[0].shape
    tr = _row_tile(rows, cols, len(ins) + len(out_dtypes))

    def body(*refs):
        outs = fn(*[r[...] for r in refs[:len(ins)]])
        for o_ref, o in zip(refs[len(ins):], outs):
            o_ref[...] = o.astype(o_ref.dtype)

    spec = pl.BlockSpec((tr, cols), lambda i: (i, 0))
    outs, got = _grid_call(body, name, rows // tr, [spec] * len(ins), [spec] * len(out_dtypes),
                           [jax.ShapeDtypeStruct((rows, cols), dt) for dt in out_dtypes], ins, "parallel", ex)
    return outs if ex is None else (outs, got)


def _adamw_tile(w, g, m, v):
    m = ADAM_B1 * m + (1.0 - ADAM_B1) * g
    v = ADAM_B2 * v + (1.0 - ADAM_B2) * (g * g)
    m_hat = m / (1.0 - ADAM_B1 ** ADAM_STEP)
    v_hat = v / (1.0 - ADAM_B2 ** ADAM_STEP)
    return -ADAM_LR * (m_hat / (jnp.sqrt(v_hat) + ADAM_EPS) + ADAM_WD * w), m, v


def adamw(w, g, m, v, name, ex=None):
    shape = w.shape
    two_d = (-1, shape[-1]) if w.ndim > 1 else (1, -1)
    outs = elementwise(_adamw_tile, [a.reshape(two_d) for a in (w, g, m, v)], [F32] * 3, name, ex)
    outs, got = outs if ex is not None else (outs, None)
    outs = [o.reshape(shape) for o in outs]
    return outs if ex is None else (outs, got)


def _prefetch_call(body, name, grid, in_specs, out_specs, out_shape, place, args, ex=None):
    if ex is None:
        spec = pltpu.PrefetchScalarGridSpec(num_scalar_prefetch=1, grid=grid, in_specs=in_specs, out_specs=out_specs)
        return pl.pallas_call(body, name=name, grid_spec=spec, out_shape=out_shape,
                              compiler_params=_params(*["parallel"] * len(grid)))(place, *args)
    n_in, n_out, ci, co = len(in_specs), len(out_specs), len(ex["ins"]), len(ex["out_shape"])
    spec = pltpu.PrefetchScalarGridSpec(num_scalar_prefetch=1, grid=grid, in_specs=list(in_specs) + _any(ci),
                                        out_specs=list(out_specs) + _any(co), scratch_shapes=ex["scratch"])
    outs = pl.pallas_call(
        _carrying(body, grid, n_in, n_out, ex, lead=1), name=name, grid_spec=spec, out_shape=list(out_shape) + ex["out_shape"],
        input_output_aliases={1 + n_in + i: n_out + j for i, j in ex["aliases"].items()},
        compiler_params=_params(*["arbitrary"] * len(grid)))(place, *args, *ex["ins"])
    return outs[:n_out], outs[n_out:]


def cast_place(w, layer, place, name):
    _, r, c = w.shape
    tr = _row_tile(r, c, 2)

    def body(p_ref, w_ref, o_ref):
        o_ref[...] = w_ref[...].astype(BF16)

    return _prefetch_call(
        body, name, (r // tr,), [pl.BlockSpec((None, tr, c), lambda i, p: (layer, i, 0))],
        pl.BlockSpec((None, tr, c), lambda i, p: (p[1], i, 0)), jax.ShapeDtypeStruct((N_SLOT, r, c), BF16), place, [w])


def pair_sum(g32, got, place, name, ex=None):
    n_slot, rh, c = got.shape
    tr = _row_tile(rh, c, 4)
    per = rh // tr

    def body(p_ref, a_ref, b_ref, o16_ref):
        o16_ref[...] = (a_ref[...] + b_ref[...].astype(F32)).astype(BF16)

    half = pl.BlockSpec((None, tr, c), lambda s, i, p: (s, i, 0))
    out = _prefetch_call(
        body, name, (n_slot, per), [pl.BlockSpec((None, tr, c), lambda s, i, p: (s, p[0] * per + i, 0)), half], [half],
        [jax.ShapeDtypeStruct(got.shape, BF16)], place, [g32, got], ex)
    return [(g32, got), out[0]] if ex is None else ([(g32, got), out[0][0]], out[1])


def chip_sum(terms, got, place, name):
    g32, sent = terms
    _, rh, c = sent.shape
    tr = _row_tile(rh, c, 6)
    per = rh // tr

    def body(p_ref, a_ref, b_ref, r0_ref, r1_ref, r2_ref, o_ref):
        own = a_ref[...] + b_ref[...].astype(F32)
        o_ref[...] = own + r0_ref[...].astype(F32) + r1_ref[...].astype(F32) + r2_ref[...].astype(F32)

    part = pl.BlockSpec((tr, c), lambda i, p: (i, 0))
    return _prefetch_call(
        body, name, (per,),
        [pl.BlockSpec((None, tr, c), lambda i, p: (p[1], p[0] * per + i, 0)), pl.BlockSpec((None, tr, c), lambda i, p: (p[1], i, 0)),
         part, part, part],
        pl.BlockSpec((tr, c), lambda i, p: (p[0] * per + i, 0)), jax.ShapeDtypeStruct((2 * rh, c), F32), place, [g32, sent, *got])


def adamw_layers(w, g0, g1, m, v, name, ex=None):
    _, r, c = w.shape
    tr = _row_tile(r, c, 10)

    def body(w_ref, g0_ref, g1_ref, m_ref, v_ref, g_ref, d_ref, mo_ref, vo_ref):
        g = jnp.where(pl.program_id(0) == 0, g0_ref[...], g1_ref[...])
        g_ref[...] = g
        d_ref[...], mo_ref[...], vo_ref[...] = _adamw_tile(w_ref[...], g, m_ref[...], v_ref[...])

    steps = r // tr
    stacked = pl.BlockSpec((None, tr, c), lambda l, i: (l, i, 0))
    layer0 = pl.BlockSpec((tr, c), lambda l, i: (jnp.where(l == 0, i, steps - 1), 0))
    layer1 = pl.BlockSpec((tr, c), lambda l, i: (jnp.where(l == 0, 0, i), 0))
    outs, got = _grid_call(body, name, (2, steps), [stacked, layer0, layer1, stacked, stacked], [stacked] * 4,
                           [jax.ShapeDtypeStruct(w.shape, F32)] * 4, (w, g0, g1, m, v), "parallel", ex)
    return outs if ex is None else (outs, got)


def sum8(gathered, name):
    def body(*refs):
        n = len(refs) // 2
        for g_ref, o_ref in zip(refs[:n], refs[n:]):
            acc = g_ref[0]
            for dev in range(1, N_DEV):
                acc = acc + g_ref[dev]
            o_ref[...] = acc

    return pl.pallas_call(
        body, name=name,
        out_shape=[jax.ShapeDtypeStruct(a.shape[1:], F32) for a in gathered],
        compiler_params=_params(),
    )(*gathered)


PHASES = ("start", "late", "finish")


def _place():
    return lax.axis_index("x"), lax.axis_index("y"), lax.axis_index("c")


def _any(n):
    return [pl.BlockSpec(memory_space=pl.ANY)] * n


def gather8_exchange(blocks):
    n = len(blocks)

    def copy(outs, sems, ti, k, block, to, src=None):
        dst = outs[ti].at[4 * block[0] + 2 * block[1] + block[2]]
        return pltpu.make_async_remote_copy(src_ref=dst if src is None else src, dst_ref=dst, send_sem=sems[0].at[ti, k],
                                            recv_sem=sems[1].at[ti, k], device_id=to, device_id_type=MESH)

    def first(ins, outs, sems):
        x, y, c = _place()
        local, sent = [], []
        for ti in range(n):
            local.append(pltpu.make_async_copy(ins[ti], outs[ti].at[4 * x + 2 * y + c], sems[2].at[ti]))
            sent.append(copy(outs, sems, ti, 0, (x, y, c), (x, y, 1 - c), src=ins[ti]))
            sent += [copy(outs, sems, ti, 1 + j, (x, y, c), (*chip, c), src=ins[ti]) for j, chip in enumerate(_three_chips(x, y))]
        return local, sent

    def start(ins, outs, sems):
        local, sent = first(ins, outs, sems)
        for cp in local + sent:
            cp.start()

    def passed_on(outs, sems):
        x, y, c = _place()
        return [copy(outs, sems, ti, 4 + j, (*chip, c), (x, y, 1 - c)) for ti in range(n) for j, chip in enumerate(_three_chips(x, y))]

    def late(ins, outs, sems):
        x, y, c = _place()
        on = passed_on(outs, sems)
        for ti in range(n):
            for j, chip in enumerate(_three_chips(x, y)):
                copy(outs, sems, ti, 1 + j, (*chip, c), (x, y, c)).wait_recv()
                on[3 * ti + j].start()

    def finish(ins, outs, sems):
        x, y, c = _place()
        me, sibling = (x, y, c), (x, y, 1 - c)
        local, sent = first(ins, outs, sems)
        for ti in range(n):
            copy(outs, sems, ti, 0, sibling, me).wait_recv()
            for j, chip in enumerate(_three_chips(x, y)):
                copy(outs, sems, ti, 4 + j, (*chip, 1 - c), me).wait_recv()
        for cp in sent + passed_on(outs, sems):
            cp.wait_send()
        for cp in local:
            cp.wait()

    return dict(ins=list(blocks), out_shape=[jax.ShapeDtypeStruct((N_DEV,) + b.shape, b.dtype) for b in blocks], aliases={},
                start=start, late=late, finish=finish,
                scratch=[pltpu.SemaphoreType.DMA((n, 7)), pltpu.SemaphoreType.DMA((n, 7)), pltpu.SemaphoreType.DMA((n,))])


def all_gather(blocks, name):
    return run_exchange(gather8_exchange(blocks), name)


def _three_chips(x, y):
    return [(1 - x, y), (x, 1 - y), (1 - x, 1 - y)]


def gather_exchange(placed):
    n = len(placed)

    def copy(bufs, sems, ti, k, chip, core, to):
        rh = bufs[ti].shape[1] // 2
        half = bufs[ti].at[2 * chip[0] + chip[1], pl.ds(core * rh, rh), :]
        return pltpu.make_async_remote_copy(src_ref=half, dst_ref=half, send_sem=sems[0].at[ti, k], recv_sem=sems[1].at[ti, k],
                                            device_id=to, device_id_type=MESH)

    def sends(bufs, sems):
        x, y, c = _place()
        return [copy(bufs, sems, ti, k, (x, y), c, (*chip, c)) for ti in range(n) for k, chip in enumerate(_three_chips(x, y))]

    def passed_on(bufs, sems):
        x, y, c = _place()
        return [copy(bufs, sems, ti, 3 + k, chip, c, (x, y, 1 - c)) for ti in range(n) for k, chip in enumerate(_three_chips(x, y))]

    def start(ins, bufs, sems):
        for cp in sends(bufs, sems):
            cp.start()

    def late(ins, bufs, sems):
        x, y, c = _place()
        on = passed_on(bufs, sems)
        for ti in range(n):
            for k, chip in enumerate(_three_chips(x, y)):
                copy(bufs, sems, ti, k, chip, c, (x, y, c)).wait_recv()
                on[3 * ti + k].start()

    def finish(ins, bufs, sems):
        x, y, c = _place()
        for ti in range(n):
            for k, chip in enumerate(_three_chips(x, y)):
                copy(bufs, sems, ti, 3 + k, chip, 1 - c, (x, y, c)).wait_recv()
        for cp in sends(bufs, sems) + passed_on(bufs, sems):
            cp.wait_send()

    return dict(ins=list(placed), out_shape=[jax.ShapeDtypeStruct(w.shape, w.dtype) for w in placed],
                aliases={i: i for i in range(n)}, start=start, late=late, finish=finish,
                scratch=[pltpu.SemaphoreType.DMA((n, 6)), pltpu.SemaphoreType.DMA((n, 6))])


def scatter_exchange(p16):
    n = len(p16)

    def copies(ins, got, sems):
        x, y, c = _place()
        return [pltpu.make_async_remote_copy(src_ref=ins[ti].at[2 * chip[0] + chip[1]], dst_ref=got[3 * ti + k],
                                             send_sem=sems[0].at[ti, k], recv_sem=sems[1].at[ti, k], device_id=(*chip, c),
                                             device_id_type=MESH)
                for ti in range(n) for k, chip in enumerate(_three_chips(x, y))]

    def start(ins, got, sems):
        for cp in copies(ins, got, sems):
            cp.start()

    def finish(ins, got, sems):
        for cp in copies(ins, got, sems):
            cp.wait()

    return dict(ins=list(p16), out_shape=[jax.ShapeDtypeStruct(a.shape[1:], BF16) for a in p16 for _ in range(3)], aliases={},
                start=start, finish=finish, scratch=[pltpu.SemaphoreType.DMA((n, 3)), pltpu.SemaphoreType.DMA((n, 3))])


def run_exchange(ex, name):
    ci, co = len(ex["ins"]), len(ex["out_shape"])

    def body(*refs):
        ins, outs, sems = refs[:ci], refs[ci:ci + co], refs[ci + co:]
        for phase in PHASES:
            if phase in ex:
                ex[phase](ins, outs, sems)

    return pl.pallas_call(body, name=name, in_specs=_any(ci), out_specs=_any(co), out_shape=ex["out_shape"],
                          input_output_aliases=ex["aliases"], scratch_shapes=ex["scratch"])(*ex["ins"])


def _carrying(body, grid, n_in, n_out, ex, lead=0):
    ci, co = len(ex["ins"]), len(ex["out_shape"])
    first, last = (0,) * len(grid), tuple(g - 1 for g in grid)
    steps = dict(start=first, late=(grid[0] - 2,) if len(grid) == 1 and grid[0] > 2 else last, finish=last)

    def at(ids):
        return functools.reduce(jnp.logical_and, [pl.program_id(ax) == v for ax, v in enumerate(ids)])

    def carrying(*refs):
        head, refs = refs[:lead], refs[lead:]
        c_in, c_out = refs[n_in:n_in + ci], refs[n_in + ci + n_out:n_in + ci + n_out + co]
        sems = refs[n_in + ci + n_out + co:]
        for phase in PHASES:
            if phase == "finish":
                body(*head, *refs[:n_in], *refs[n_in + ci:n_in + ci + n_out])
            if phase in ex:
                pl.when(at(steps[phase]))(functools.partial(ex[phase], c_in, c_out, sems))

    return carrying


def _grid_call(body, name, grid, in_specs, out_specs, out_shape, args, sem, ex=None):
    grid = (grid,) if isinstance(grid, int) else tuple(grid)
    sems_of = (sem,) * len(grid) if isinstance(sem, str) else tuple(sem)
    n_in, n_out = len(in_specs), len(out_specs)
    if ex is None:
        return pl.pallas_call(body, name=name, grid=grid, in_specs=in_specs, out_specs=out_specs, out_shape=out_shape,
                              compiler_params=_params(*sems_of))(*args), []
    ci, co = len(ex["ins"]), len(ex["out_shape"])
    outs = pl.pallas_call(
        _carrying(body, grid, n_in, n_out, ex), name=name, grid=grid, in_specs=list(in_specs) + _any(ci),
        out_specs=list(out_specs) + _any(co), out_shape=list(out_shape) + ex["out_shape"], scratch_shapes=ex["scratch"],
        input_output_aliases={n_in + i: n_out + j for i, j in ex["aliases"].items()},
        compiler_params=_params(*["arbitrary"] * len(grid)),
    )(*args, *ex["ins"])
    return outs[:n_out], outs[n_out:]


def both(*exchanges):
    exchanges = [ex for ex in exchanges if ex is not None]
    if len(exchanges) < 2:
        return exchanges[0] if exchanges else None
    n_ins = [len(ex["ins"]) for ex in exchanges]
    n_outs = [len(ex["out_shape"]) for ex in exchanges]
    n_sems = [len(ex["scratch"]) for ex in exchanges]

    def parts(seq, counts, k):
        first = sum(counts[:k])
        return seq[first:first + counts[k]]

    def run(phase):
        def go(ins, outs, sems):
            for k, ex in enumerate(exchanges):
                if phase in ex:
                    ex[phase](parts(ins, n_ins, k), parts(outs, n_outs, k), parts(sems, n_sems, k))
        return go

    aliases = {sum(n_ins[:k]) + i: sum(n_outs[:k]) + j for k, ex in enumerate(exchanges) for i, j in ex["aliases"].items()}
    return dict(ins=[a for ex in exchanges for a in ex["ins"]], out_shape=[o for ex in exchanges for o in ex["out_shape"]],
                aliases=aliases, scratch=[s for ex in exchanges for s in ex["scratch"]], **{ph: run(ph) for ph in PHASES})


def split_outputs(got, *exchanges):
    got, out = list(got), []
    for ex in exchanges:
        n = len(ex["out_shape"]) if ex is not None else 0
        out.append(got[:n])
        got = got[n:]
    return out


def pair_exchange(g16):
    n = len(g16)

    def copies(a16, got, sems):
        x, y, c = _place()
        out = []
        for ti in range(n):
            rh = a16[ti].shape[1] // 2
            out.append(pltpu.make_async_remote_copy(
                src_ref=a16[ti].at[:, pl.ds((1 - c) * rh, rh), :], dst_ref=got[ti], send_sem=sems[0].at[ti],
                recv_sem=sems[1].at[ti], device_id=(x, y, 1 - c), device_id_type=MESH))
        return out

    def start(a16, got, sems):
        for cp in copies(a16, got, sems):
            cp.start()

    def finish(a16, got, sems):
        for cp in copies(a16, got, sems):
            cp.wait()

    return dict(ins=list(g16), out_shape=[jax.ShapeDtypeStruct((a.shape[0], a.shape[1] // 2, a.shape[2]), BF16) for a in g16],
                aliases={}, start=start, finish=finish, scratch=[pltpu.SemaphoreType.DMA((n,)), pltpu.SemaphoreType.DMA((n,))])


def _gather_half(buf, chip, core):
    rh = buf.shape[1] // 2
    return buf.at[2 * chip[0] + chip[1], pl.ds(core * rh, rh), :]


def gather_start(placed, name):
    n = len(placed)
    hbm, sem = pl.BlockSpec(memory_space=pltpu.HBM), pl.BlockSpec(memory_space=pltpu.SEMAPHORE)

    def body(*refs):
        bufs, send_sems, recv_sems, token_ref = refs[:n], refs[n], refs[n + 1], refs[-1]
        x, y, c = _place()
        for ti in range(n):
            for k, chip in enumerate(_three_chips(x, y)):
                half = _gather_half(bufs[ti], (x, y), c)
                pltpu.make_async_remote_copy(src_ref=half, dst_ref=half, send_sem=send_sems.at[3 * ti + k],
                                             recv_sem=recv_sems.at[3 * ti + k], device_id=(*chip, c), device_id_type=MESH).start()
        token_ref[...] = jnp.zeros_like(token_ref)

    return pl.pallas_call(
        body, name=name,
        out_shape=(pltpu.SemaphoreType.DMA((3 * n,)), pltpu.SemaphoreType.DMA((3 * n,)), *[pltpu.HBM(w.shape, w.dtype) for w in placed],
                   jax.ShapeDtypeStruct((8, BLK), F32)),
        in_specs=(hbm,) * n, out_specs=(sem, sem, *(hbm,) * n, pl.BlockSpec(memory_space=pltpu.VMEM)),
        input_output_aliases={i: 2 + i for i in range(n)},
        compiler_params=pltpu.CompilerParams(has_side_effects=pltpu.SideEffectType.DATAFLOW_SIDE_EFFECTING),
    )(*[pltpu.with_memory_space_constraint(w, pltpu.HBM) for w in placed])


def gather_wait(send_sems, recv_sems, bufs, after, name):
    n = len(bufs)
    hbm, sem = pl.BlockSpec(memory_space=pltpu.HBM), pl.BlockSpec(memory_space=pltpu.SEMAPHORE)

    def body(*refs):
        bufs, send_sems, recv_sems = refs[:n], refs[n], refs[n + 1]
        x, y, c = _place()
        for ti in range(n):
            for k, chip in enumerate(_three_chips(x, y)):
                mine, theirs = _gather_half(bufs[ti], (x, y), c), _gather_half(bufs[ti], chip, c)
                cp = pltpu.make_async_remote_copy(src_ref=mine, dst_ref=theirs, send_sem=send_sems.at[3 * ti + k],
                                                  recv_sem=recv_sems.at[3 * ti + k], device_id=(*chip, c), device_id_type=MESH)
                cp.wait_send()
                cp.wait_recv()

    return pl.pallas_call(
        body, name=name, out_shape=tuple(pltpu.HBM(w.shape, w.dtype) for w in bufs),
        in_specs=(*(hbm,) * n, sem, sem, *_any(len(after))), out_specs=(hbm,) * n,
        input_output_aliases={i: i for i in range(n)},
        compiler_params=pltpu.CompilerParams(has_side_effects=pltpu.SideEffectType.DATAFLOW_SIDE_EFFECTING),
    )(*bufs, send_sems, recv_sems, *after)


def pass_on_exchange(bufs):
    n = len(bufs)

    def copies(refs, sems, core):
        x, y, c = _place()
        return [pltpu.make_async_remote_copy(src_ref=_gather_half(refs[ti], chip, c if core == "mine" else 1 - c),
                                             dst_ref=_gather_half(refs[ti], chip, c if core == "mine" else 1 - c),
                                             send_sem=sems[0].at[ti, k], recv_sem=sems[1].at[ti, k], device_id=(x, y, 1 - c),
                                             device_id_type=MESH)
                for ti in range(n) for k, chip in enumerate(_three_chips(x, y))]

    def start(ins, refs, sems):
        for cp in copies(refs, sems, "mine"):
            cp.start()

    def finish(ins, refs, sems):
        for cp in copies(refs, sems, "mine"):
            cp.wait_send()
        for cp in copies(refs, sems, "sibling's"):
            cp.wait_recv()

    return dict(ins=list(bufs), out_shape=[jax.ShapeDtypeStruct(w.shape, w.dtype) for w in bufs], aliases={i: i for i in range(n)},
                start=start, finish=finish, scratch=[pltpu.SemaphoreType.DMA((n, 3)), pltpu.SemaphoreType.DMA((n, 3))])


def _scatter_copies(src_ref, lands, send_sems, recv_sems):
    x, y, c = _place()
    return [pltpu.make_async_remote_copy(src_ref=src_ref.at[2 * chip[0] + chip[1]], dst_ref=lands[k], send_sem=send_sems.at[k],
                                         recv_sem=recv_sems.at[k], device_id=(*chip, c), device_id_type=MESH)
            for k, chip in enumerate(_three_chips(x, y))]


def scatter_start(p16, name):
    hbm, sem = pl.BlockSpec(memory_space=pltpu.HBM), pl.BlockSpec(memory_space=pltpu.SEMAPHORE)

    def body(src_ref, l0_ref, l1_ref, l2_ref, send_sems, recv_sems, src_thru, o0_ref, o1_ref, o2_ref, token_ref):
        for cp in _scatter_copies(src_ref, (l0_ref, l1_ref, l2_ref), send_sems, recv_sems):
            cp.start()
        token_ref[...] = jnp.zeros_like(token_ref)

    land = [pltpu.with_memory_space_constraint(lax.empty(p16.shape[1:], BF16), pltpu.HBM) for _ in range(3)]
    return pl.pallas_call(
        body, name=name,
        out_shape=(pltpu.SemaphoreType.DMA((3,)), pltpu.SemaphoreType.DMA((3,)), pltpu.HBM(p16.shape, BF16),
                   *[pltpu.HBM(p16.shape[1:], BF16)] * 3, jax.ShapeDtypeStruct((8, BLK), F32)),
        in_specs=(hbm,) * 4, out_specs=(sem, sem, hbm, hbm, hbm, hbm, pl.BlockSpec(memory_space=pltpu.VMEM)),
        input_output_aliases={0: 2, 1: 3, 2: 4, 3: 5},
        compiler_params=pltpu.CompilerParams(has_side_effects=pltpu.SideEffectType.DATAFLOW_SIDE_EFFECTING),
    )(pltpu.with_memory_space_constraint(p16, pltpu.HBM), *land)


def scatter_wait(send_sems, recv_sems, src_thru, lands, after, name):
    hbm, sem = pl.BlockSpec(memory_space=pltpu.HBM), pl.BlockSpec(memory_space=pltpu.SEMAPHORE)

    def body(src_ref, l0_ref, l1_ref, l2_ref, send_sems, recv_sems, *rest):
        for cp in _scatter_copies(src_ref, (l0_ref, l1_ref, l2_ref), send_sems, recv_sems):
            cp.wait_send()
            cp.wait_recv()

    return pl.pallas_call(
        body, name=name, out_shape=(pltpu.HBM(src_thru.shape, BF16), *[pltpu.HBM(lands[0].shape, BF16)] * 3),
        in_specs=(hbm, hbm, hbm, hbm, sem, sem, *_any(len(after))), out_specs=(hbm,) * 4,
        input_output_aliases={0: 0, 1: 1, 2: 2, 3: 3},
        compiler_params=pltpu.CompilerParams(has_side_effects=pltpu.SideEffectType.DATAFLOW_SIDE_EFFECTING),
    )(src_thru, *lands, send_sems, recv_sems, *after)[1:]


def pair_fill_exchange(halves):
    n = len(halves)

    def copies(bufs, sems, core):
        x, y, c = _place()
        out = []
        for ti in range(n):
            rh = bufs[ti].shape[0] // 2
            rows = bufs[ti].at[pl.ds((c if core == "mine" else 1 - c) * rh, rh), :]
            out.append(pltpu.make_async_remote_copy(src_ref=rows, dst_ref=rows, send_sem=sems[0].at[ti], recv_sem=sems[1].at[ti],
                                                    device_id=(x, y, 1 - c), device_id_type=MESH))
        return out

    def start(ins, bufs, sems):
        for cp in copies(bufs, sems, "mine"):
            cp.start()

    def finish(ins, bufs, sems):
        for cp in copies(bufs, sems, "mine"):
            cp.wait_send()
        for cp in copies(bufs, sems, "sibling's"):
            cp.wait_recv()

    return dict(ins=list(halves), out_shape=[jax.ShapeDtypeStruct(a.shape, a.dtype) for a in halves],
                aliases={i: i for i in range(n)}, start=start, finish=finish,
                scratch=[pltpu.SemaphoreType.DMA((n,)), pltpu.SemaphoreType.DMA((n,))])


def pair_gather(halves, name):
    return run_exchange(pair_fill_exchange(halves), name)


def reduce_small(dm_f1, dm_mix, dm_gate, dm_f2, loss_blk, name):
    def body(f1_ref, mix_ref, gate_ref, f2_ref, l_ref, tot_ref, rows_ref, fin_ref):
        rows_ref[...] = jnp.zeros_like(rows_ref)
        tot_ref[...] = jnp.zeros_like(tot_ref)
        mod_src = [(f1_ref, 0), (f1_ref, 1), (f1_ref, 2), (mix_ref, 0), (mix_ref, 1), (gate_ref, 2),
                   (f2_ref, 0), (f2_ref, 1), (f2_ref, 2)]
        norm_src = [(f1_ref, 3), (mix_ref, 3), (f2_ref, 3)]
        for l in range(2):
            for k, (ref, r) in enumerate(mod_src + norm_src):
                lat = ref[0, l, 0, r:r + 1, :]
                ctx = ref[0, l, 1, r:r + 1, :]
                for dev in range(N_DEV):
                    if dev:
                        lat = lat + ref[dev, l, 0, r:r + 1, :]
                        ctx = ctx + ref[dev, l, 1, r:r + 1, :]
                    if k < N_MOD:
                        rows_ref[l, dev, k:k + 1, :] = ref[dev, l, 0, r:r + 1, :]
                if k < N_MOD:
                    rows_ref[l, N_DEV, k:k + 1, :] = ctx
                tot_ref[l, k:k + 1, :] = lat + ctx
        acc = l_ref[0]
        for dev in range(1, N_DEV):
            acc = acc + l_ref[dev]
        loss = (0.5 / D) * jnp.sum(acc[1:2, :], axis=1, keepdims=True)
        row = lax.broadcasted_iota(jnp.int32, (8, D), 0)
        fin_ref[...] = jnp.where(row == 0, acc[0:1, :], loss)

    return pl.pallas_call(
        body, name=name,
        out_shape=[jax.ShapeDtypeStruct((2, 16, D), F32), jax.ShapeDtypeStruct((2, 16, 16, D), F32),
                   jax.ShapeDtypeStruct((8, D), F32)],
        compiler_params=_params(),
    )(dm_f1, dm_mix, dm_gate, dm_f2, loss_blk)


def rope_tables(t, s):
    rows = t // GRID_W
    row = jnp.repeat(jnp.arange(rows), GRID_W).astype(F32)
    col = jnp.tile(jnp.arange(GRID_W), rows).astype(F32)
    inv = ROPE_BASE ** (-jnp.arange(0, HEAD // 2, 2, dtype=F32) / (HEAD // 2))
    ang = jnp.concatenate([row[:, None] * inv, col[:, None] * inv], axis=-1)
    cos, sin = jnp.cos(ang), jnp.sin(ang)
    cos = jnp.concatenate([jnp.tile(cos, (1, 4)), jnp.ones((s - t, BLK), F32)], axis=0)
    sin = jnp.concatenate([jnp.tile(jnp.concatenate([-sin, sin], axis=1), (1, 2)), jnp.zeros((s - t, BLK), F32)], axis=0)
    return cos, sin


BIG = ("ffn1_in", "ffn1_out", "w_in", "w_out", "ffn2_in", "ffn2_out")
GROUPS = dict(ffn1=("ffn1_in", "ffn1_out"), mix=("w_in", "w_out"), ffn2=("ffn2_in", "ffn2_out"))
GATHER_BEHIND = {("ffn1", 0): [("w_in", 0), ("ffn2_out", 0), ("ffn1_out", 1)], ("proj", 0): [("w_out", 0)],
                 ("mix", 0): [("ffn2_in", 0)], ("ffn2", 0): [("ffn1_in", 1), ("w_in", 1)],
                 ("ffn1", 1): [("ffn2_in", 1), ("w_out", 1)], ("mix", 1): [("ffn2_out", 1)]}


def _slot_major(name, g):
    if name == "w_in":
        return jnp.stack(jnp.split(g, N_SLOT, axis=1), axis=0)
    if name in ("ffn1_in", "ffn2_in"):
        return g
    return g.reshape(N_SLOT, g.shape[0] // N_SLOT, g.shape[1])


def _whole_weight(name, buf):
    if name == "w_in":
        return buf.transpose(1, 0, 2).reshape(D, PROJ_W)
    if name in ("ffn1_in", "ffn2_in"):
        return buf
    return buf.reshape(-1, buf.shape[2])


def local_step(x1, ctx1, target, mods, norms, nfinal, placed, w_pool, pool_scale, sink, place, small_blocks):
    t, s = x1.shape[0], x1.shape[0] + ctx1.shape[0]
    n_lat = t // TM
    cos, sin = rope_tables(t, s)
    tables = mix_tables(t, s)
    wts ={name: list(pair) for name, pair in placed.items()}

    def gather(tensors):
        return gather_exchange([wts[name][l] for name, l in tensors])

    def gathered(tensors, arrays):
        for (name, l), whole in zip(tensors, arrays):
            wts[name][l] = whole

    def weight(name, l):
        return _whole_weight(name, wts[name][l])

    def fwd_ex(grp, l):
        groups = GATHER_BEHIND.get((grp, l))
        return (groups, gather(groups)) if groups else (None, None)

    h = jnp.concatenate([x1, ctx1], axis=0)
    saved = []
    for l in range(2):
        h0 = h
        groups, ex = fwd_ex("ffn1", l)
        (h1, ab1, f1), got = ffn_fwd(h0, mods, norms[0], weight("ffn1_in", l), weight("ffn1_out", l), l, 0, n_lat, f"ffn1_fwd_{l}", ex)
        gathered(groups or [], got)
        groups, ex = fwd_ex("proj", l)
        (u, q, k, v), got = proj_fwd(h1, mods, norms[1], weight("w_in", l), cos, sin, l, n_lat, f"proj_fwd_{l}", ex)
        gathered(groups or [], got)
        groups, ex = fwd_ex("mix", l)
        (h2, cat, lse, mo), got = mix_fwd(h1, q, k, v, u, w_pool, pool_scale, sink, weight("w_out", l), mods, tables, l, t,
                                          f"mix_fwd_{l}", ex)
        gathered(groups or [], got)
        groups, ex = fwd_ex("ffn2", l)
        (h, ab2, f2), got = ffn_fwd(h2, mods, norms[2], weight("ffn2_in", l), weight("ffn2_out", l), l, 6, n_lat, f"ffn2_fwd_{l}", ex)
        gathered(groups or [], got)
        saved.append((h0, ab1, f1, h1, u, q, k, v, cat, lse, mo, h2, ab2, f2))
    dh, loss_blk = loss_head(h, target, nfinal, t, "loss_head")

    halves = {name: [None, None] for name in BIG}
    pending = []

    def summed_in_pair(grp, l, name_a, g_a, name_b, wgrad_b):
        g_b, got_a = wgrad_b(pair_exchange([_slot_major(name_a, g_a[1])]))
        sum_a, got_b = pair_sum(_slot_major(name_a, g_a[0]), got_a[0], place, f"pair_sum_{name_a}_{l}",
                                pair_exchange([_slot_major(name_b, g_b[1])]))
        sums = {name_a: sum_a, name_b: pair_sum(_slot_major(name_b, g_b[0]), got_b[0], place, f"pair_sum_{name_b}_{l}")}
        pending.append((grp, l, [sums[n] for n in GROUPS[grp]]))

    lacking = []

    def riders():
        return (scatter_exchange([p16 for _, p16 in pending[0][2]]) if pending else None,
                pair_fill_exchange([halves[name][l] for name, l in lacking]) if lacking else None)

    def carried(got, exs):
        got, filled = split_outputs(got, *exs)
        for (name, l), whole in zip(list(lacking), filled):
            halves[name][l] = whole
            lacking.remove((name, l))
        if pending:
            grp, l, pairs = pending.pop(0)
            for i, name in enumerate(GROUPS[grp]):
                halves[name][l] = chip_sum(pairs[i][0], got[3 * i:3 * i + 3], place, f"chip_sum_{name}_{l}")
                lacking.append((name, l))

    small = [None, None]
    for l in (1, 0):
        h0, ab1, f1, h1, u, q, k, v, cat, lse, mo, h2, ab2, f2 = saved[l]
        exs = riders()
        (dh, dab, df, n, act, dm_f2), got = ffn_bwd(h2, ab2, f2, dh, mods, norms[2], weight("ffn2_in", l), weight("ffn2_out", l),
                                                    l, 6, n_lat, f"ffn2_bwd_{l}", both(*exs))
        carried(got, exs)
        g_in, _ = wgrad(n, dab, D // 2, FF_COLS, FF_COLS, f"ffn2_in_wgrad_{l}")
        summed_in_pair("ffn2", l, "ffn2_in", g_in, "ffn2_out",
                       lambda ex, a=act, b=df: wgrad(a, b, D_FF // 2, D // 2, None, f"ffn2_out_wgrad_{l}", ex))
        exs = riders()
        (dq, dk, dv, du, dmo, dwp, dps, dsink, dm_gate), got = mix_bwd(
            dh, mo, q, k, v, u, lse, w_pool, pool_scale, sink, weight("w_out", l), mods, tables, l, t, f"mix_bwd_{l}", both(*exs))
        carried(got, exs)
        g_wo, _ = wgrad(cat, dmo, POOL_W + ATTN_W, D, None, f"w_out_wgrad_{l}")
        dh, dp, n, dm_mix = proj_bwd(h1, du, dq, dk, dv, dh, mods, norms[1], weight("w_in", l), cos, sin, l, n_lat, f"proj_bwd_{l}")
        summed_in_pair("mix", l, "w_out", g_wo, "w_in",
                       lambda ex, a=n, b=dp: wgrad(a, b, D, PROJ_W // 2, None, f"w_in_wgrad_{l}", ex))
        exs = riders()
        (dh, dab, df, n, act, dm_f1), got = ffn_bwd(h0, ab1, f1, dh, mods, norms[0], weight("ffn1_in", l), weight("ffn1_out", l),
                                                    l, 0, n_lat, f"ffn1_bwd_{l}", both(*exs))
        carried(got, exs)
        small[l] = dict(dm_f1=dm_f1, dm_mix=dm_mix, dm_gate=dm_gate, dm_f2=dm_f2, dwp=dwp, dps=dps, dsink=dsink)
        if l:
            g_in, _ = wgrad(n, dab, D // 2, FF_COLS, FF_COLS, f"ffn1_in_wgrad_{l}")
            summed_in_pair("ffn1", l, "ffn1_in", g_in, "ffn1_out",
                           lambda ex, a=act, b=df: wgrad(a, b, D_FF // 2, D // 2, None, f"ffn1_out_wgrad_{l}", ex))
    g_out, _ = wgrad(act, df, D_FF // 2, D // 2, None, "ffn1_out_wgrad_0")
    riding = (gather8_exchange(small_blocks(small, loss_blk)), pair_exchange([_slot_major("ffn1_out", g_out[1])]),
              pair_fill_exchange([halves[name][l] for name, l in lacking]))
    g_in, got = wgrad(n, dab, D // 2, FF_COLS, FF_COLS, "ffn1_in_wgrad_0", both(*riding))
    small_all, got_out, filled = split_outputs(got, *riding)
    for (name, l), whole in zip(lacking, filled):
        halves[name][l] = whole
    got_in = run_exchange(pair_exchange([_slot_major("ffn1_in", g_in[1])]), "pair_exchange_ffn1_in_0")
    last = {"ffn1_in": pair_sum(_slot_major("ffn1_in", g_in[0]), got_in[0], place, "pair_sum_ffn1_in_0"),
            "ffn1_out": pair_sum(_slot_major("ffn1_out", g_out[0]), got_out[0], place, "pair_sum_ffn1_out_0")}
    return dh[:t], halves, last, small_all


def _silu_grad(z):
    sg = jax.nn.sigmoid(z)
    return sg * (1 + z * (1 - sg))


def kernel(x, c, ctx, c_ctx, w_mod, b_mod, norm_ffn1, w_ffn1_in, w_ffn1_out, norm_mix, w_in, w_pool, pool_scale, sink, w_out, norm_ffn2, w_ffn2_in, w_ffn2_out, norm_final, loss_target, m_c_ctx, m_w_mod, m_b_mod, m_norm_ffn1, m_w_ffn1_in, m_w_ffn1_out, m_norm_mix, m_w_in, m_w_pool, m_pool_scale, m_sink, m_w_out, m_norm_ffn2, m_w_ffn2_in, m_w_ffn2_out, m_norm_final, v_c_ctx, v_w_mod, v_b_mod, v_norm_ffn1, v_w_ffn1_in, v_w_ffn1_out, v_norm_mix, v_w_in, v_w_pool, v_pool_scale, v_sink, v_w_out, v_norm_ffn2, v_w_ffn2_in, v_w_ffn2_out, v_norm_final):
    px, py, pc = _place()
    slot, me = 2 * px + py, 4 * px + 2 * py + pc
    n_grp = len(POOL_WINDOWS)

    (c_rows,) = all_gather([c.reshape(8, D // 8)], "gather_c")
    c_all = jnp.concatenate([c_rows.reshape(N_DEV, D), c_ctx.reshape(1, D), jnp.zeros((16 - N_DEV - 1, D), F32)], axis=0)

    place = jnp.stack([pc, slot]).astype(jnp.int32)
    shards = dict(ffn1_in=w_ffn1_in, ffn1_out=w_ffn1_out, w_in=w_in, w_out=w_out, ffn2_in=w_ffn2_in, ffn2_out=w_ffn2_out)
    first = [("ffn1_in", 0), ("ffn1_out", 0)]
    placed = {name: [None, None] for name in BIG}
    for name, l in first:
        placed[name][l] = cast_place(shards[name], l, place, f"cast_{name}_{l}")
    send_sems, recv_sems, *bufs, token = gather_start([placed[name][l] for name, l in first], "gather_first_start")

    b_cols = lax.dynamic_slice(b_mod, (0, slot * MOD_COLS), (2, MOD_COLS)).reshape(2, 1, MOD_COLS)
    mod_cols = mod_rows(c_all, w_mod, b_cols, "mod_rows")
    others = [(name, l) for name in BIG for l in range(2) if (name, l) not in first]
    for name, l in others:
        placed[name][l] = cast_place(shards[name], l, place, f"cast_{name}_{l}")
    (mod_parts,) = all_gather([mod_cols], "gather_mods")
    mods_all = mod_parts[0::2].transpose(1, 2, 0, 3).reshape(2, 16, N_MOD * D)
    mx = lax.dynamic_slice(mods_all, (0, me, 0), (2, 1, N_MOD * D)).reshape(2, N_MOD, D)
    mc = mods_all[:, N_DEV].reshape(2, N_MOD, D)
    pad = jnp.zeros((2, 16 - N_MOD, D), F32)
    mods = jnp.stack([jnp.concatenate([mx, pad], axis=1), jnp.concatenate([mc, pad], axis=1)], axis=1)

    bufs = gather_wait(send_sems, recv_sems, bufs, [placed[name][l] for name, l in others] + [mod_parts], "gather_first_wait")
    for (name, l), whole in zip(first, run_exchange(pass_on_exchange(bufs), "gather_first_pass_on")):
        placed[name][l] = whole
    norms = [g.reshape(2, 1, D) for g in (norm_ffn1, norm_mix, norm_ffn2)]
    row_sums = ("dm_f1", "dm_mix", "dm_gate", "dm_f2")

    def small_blocks(small, loss_blk):
        stacked = {k: jnp.stack([small[0][k], small[1][k]]) for k in row_sums + ("dwp", "dps", "dsink")}
        return ([stacked[k].reshape(32, D) for k in row_sums]
                + [stacked["dwp"].reshape(2 * n_grp * GROUP, GROUP), stacked["dps"].reshape(16, POOL_W),
                   stacked["dsink"].reshape(16, BLK), loss_blk])

    dx, halves, last, small_all = local_step(x[0], ctx[0], loss_target[0], mods, norms, norm_final.reshape(1, D), placed,
                                                   w_pool.astype(BF16), pool_scale.reshape(2, 1, POOL_W), sink, place, small_blocks)
    grads = {}

    *g_dm, g_dwp, g_dps, g_dsink, g_loss = small_all
    tot, rows, fin = reduce_small(*[g.reshape(N_DEV, 2, 2, 8, D) for g in g_dm], g_loss, "reduce_small")
    s_dwp, s_dps, s_dsink = sum8([g_dwp, g_dps, g_dsink], "sum_pool_sink")
    grads.update(
        w_pool=s_dwp.reshape(2, n_grp, GROUP, GROUP), pool_scale=s_dps.reshape(2, 8, POOL_W)[:, 0],
        sink=s_dsink.reshape(2, 8, BLK)[:, 0, :N_HEADS], b_mod=tot[:, :N_MOD].reshape(2, N_MOD * D),
        norm_ffn1=tot[:, N_MOD], norm_mix=tot[:, N_MOD + 1], norm_ffn2=tot[:, N_MOD + 2], norm_final=fin[0])
    loss = fin[1, 0]

    dmod_cols = lax.dynamic_slice(rows[:, :, :N_MOD, :].reshape(2, 16, N_MOD * D), (0, 0, slot * MOD_COLS), (2, 16, MOD_COLS))
    grads["w_mod"], dc = mod_grads(c_all, dmod_cols, w_mod, "mod_grads")
    (g_dc,) = all_gather([dc], "gather_dc")
    (s_dc,) = sum8([g_dc], "sum_dc")
    (d_c_ctx,) = elementwise(lambda d, z: (0.5 * d * _silu_grad(z),), [s_dc[N_DEV:N_DEV + 1], c_ctx.reshape(1, D)], [F32], "c_ctx_grad")
    started = {name: scatter_start(last[name][1], f"scatter_last_start_{name}") for name in last}
    grads["c_ctx"] = d_c_ctx.reshape(D) + sum(st[-1][0, :1] for st in started.values())

    given = dict(c_ctx=(c_ctx, m_c_ctx, v_c_ctx), w_mod=(w_mod, m_w_mod, v_w_mod), b_mod=(b_mod, m_b_mod, v_b_mod),
                 norm_ffn1=(norm_ffn1, m_norm_ffn1, v_norm_ffn1), w_ffn1_in=(w_ffn1_in, m_w_ffn1_in, v_w_ffn1_in),
                 w_ffn1_out=(w_ffn1_out, m_w_ffn1_out, v_w_ffn1_out), norm_mix=(norm_mix, m_norm_mix, v_norm_mix),
                 w_in=(w_in, m_w_in, v_w_in), w_pool=(w_pool, m_w_pool, v_w_pool),
                 pool_scale=(pool_scale, m_pool_scale, v_pool_scale), sink=(sink, m_sink, v_sink), w_out=(w_out, m_w_out, v_w_out),
                 norm_ffn2=(norm_ffn2, m_norm_ffn2, v_norm_ffn2), w_ffn2_in=(w_ffn2_in, m_w_ffn2_in, v_w_ffn2_in),
                 w_ffn2_out=(w_ffn2_out, m_w_ffn2_out, v_w_ffn2_out), norm_final=(norm_final, m_norm_final, v_norm_final))
    shard = {(name, l): halves[name][l] for name in BIG for l in range(2)}

    def update(name):
        w, m, v = given[name]
        if name in BIG or name[2:] in BIG:
            key = name if name in BIG else name[2:]
            return adamw_layers(w, shard[key, 0], shard[key, 1], m, v, f"adamw_{name}")
        return [grads[name], *adamw(w, grads[name], m, v, f"adamw_{name}")]

    done = {name: update(name) for name in given if name[2:] not in last}
    between = [done[name][3] for name in done if name[2:] in BIG or name in BIG] + [done["w_mod"][3]]
    summed = []
    for name, (send_sems, recv_sems, src_thru, *lands, _) in started.items():
        got = scatter_wait(send_sems, recv_sems, src_thru, lands, between, f"scatter_last_wait_{name}")
        summed.append(chip_sum(last[name][0], got, place, f"chip_sum_{name}_0"))
    for name, whole in zip(started, pair_gather(summed, "grad_pair_gather_last")):
        shard[name, 0] = whole
        done["w_" + name] = update("w_" + name)
    return (loss, dx[None], *[done[name][i] for i in range(4) for name in given])
```

```python
import functools

import jax
import jax.numpy as jnp
from jax import lax
from jax.experimental import pallas as pl
from jax.experimental.pallas import tpu as pltpu

F32, BF16 = jnp.float32, jnp.bfloat16
D = 1024
D_FF = 2816
N_SLOT = 4
FF_COLS = 2 * D_FF // N_SLOT
N_MOD = 9
MOD_COLS = N_MOD * D // N_SLOT
POOL_W, ATTN_W, KV_W = 512, 512, 128
PROJ_W = POOL_W + ATTN_W + 2 * KV_W
N_HEADS, Q_GROUP, HEAD = 8, 4, 64
GROUP = 128
POOL_WINDOWS = (2, 4, 8, 16)
BLK = 128
QB = 256
WIN = QB + 2 * BLK
GRID_W = 64
ROPE_BASE = 10000.0
EPS = 1e-6
NEG_INF = -1e30
TM = 256
N_DEV = 8
VMEM_LIMIT_BYTES = 56 * 1024 * 1024
WGRAD_VMEM_BYTES = 44 * 1024 * 1024
ADAM_LR, ADAM_B1, ADAM_B2, ADAM_EPS, ADAM_WD, ADAM_STEP = 0.001, 0.9, 0.999, 1e-08, 0.01, 10
MESH = pl.DeviceIdType.MESH
NT = (((1,), (1,)), ((), ()))
TN = (((0,), (0,)), ((), ()))


def _params(*sem):
    return pltpu.CompilerParams(dimension_semantics=sem, vmem_limit_bytes=VMEM_LIMIT_BYTES)


def _whole(shape, lead=()):
    idx = tuple(lead) + (0,) * len(shape)
    return pl.BlockSpec((None,) * len(lead) + tuple(shape), lambda *_: idx, pipeline_mode=pl.Buffered(1))


def _rows(cols, tm=TM):
    return pl.BlockSpec((tm, cols), lambda i: (i, 0))


def _mods_spec(layer, n_lat):
    return pl.BlockSpec((None, None, 16, D), lambda i: (layer, (i >= n_lat).astype(jnp.int32), 0, 0))


def _acc_spec(n_lat):
    return pl.BlockSpec((None, 8, D), lambda i: ((i >= n_lat).astype(jnp.int32), 0, 0))


def _dot(a, b):
    return jnp.dot(a, b, preferred_element_type=F32)


def _dotg(a, b, dims):
    return lax.dot_general(a, b, dims, preferred_element_type=F32)


def _sum0(v):
    return jnp.sum(v, axis=0, keepdims=True)


def _norm_mod(h, g, shift, scale):
    r = lax.rsqrt(jnp.mean(h * h, axis=-1, keepdims=True) + EPS)
    xhat = h * r
    y = xhat * g
    return y * (1 + scale) + shift, xhat, r, y


def _norm_mod_bwd(dn, xhat, r, y, g, scale):
    dy = dn * (1 + scale)
    dx = dy * g
    dh = r * (dx - xhat * jnp.mean(dx * xhat, axis=-1, keepdims=True))
    return _sum0(dn), _sum0(dn * y), _sum0(dy * xhat), dh


def _swap_halves(v):
    w = v.shape[1]
    lane = lax.broadcasted_iota(jnp.int32, v.shape, 1)
    return jnp.where(lane % HEAD < HEAD // 2, pltpu.roll(v, w - HEAD // 2, axis=1), pltpu.roll(v, HEAD // 2, axis=1))


def _tile_lanes(t, width):
    return t if width == t.shape[1] else jnp.concatenate([t] * (width // t.shape[1]), axis=1)


def _rope(v, cos, sin):
    return v * _tile_lanes(cos, v.shape[1]) + _swap_halves(v) * _tile_lanes(sin, v.shape[1])


def _unrope(g, cos, sin):
    return g * _tile_lanes(cos, g.shape[1]) + _swap_halves(g * _tile_lanes(sin, g.shape[1]))


def ffn_fwd(h, mods, g, w4, wo, layer, k0, n_lat, name, ex=None):
    s = h.shape[0]

    def body(h_ref, m_ref, g_ref, w_ref, wo_ref, ho_ref, ab_ref, f_ref):
        hh = h_ref[...]
        n, _, _, _ = _norm_mod(hh, g_ref[...], m_ref[k0:k0 + 1, :], m_ref[k0 + 1:k0 + 2, :])
        nb = n.astype(BF16)
        acc = jnp.zeros((TM, D), F32)
        for j in range(2):
            a = _dot(nb, w_ref[j])
            b = _dot(nb, w_ref[2 + j])
            ab_ref[:, j * FF_COLS:(j + 1) * FF_COLS] = a.astype(BF16)
            ab_ref[:, (2 + j) * FF_COLS:(3 + j) * FF_COLS] = b.astype(BF16)
            act = (a * jax.nn.sigmoid(a) * b).astype(BF16)
            acc = acc + _dot(act, wo_ref[j * FF_COLS:(j + 1) * FF_COLS, :])
        f_ref[...] = acc
        ho_ref[...] = hh + 0.5 * m_ref[k0 + 2:k0 + 3, :] * acc

    return _grid_call(
        body, name, s // TM,
        [_rows(D), _mods_spec(layer, n_lat), _whole((1, D), (layer,)), _whole((N_SLOT, D, FF_COLS)), _whole((D_FF, D))],
        [_rows(D), _rows(2 * D_FF), _rows(D)],
        [jax.ShapeDtypeStruct((s, D), F32), jax.ShapeDtypeStruct((s, 2 * D_FF), BF16), jax.ShapeDtypeStruct((s, D), F32)],
        (h, mods, g, w4, wo), "parallel", ex)


def ffn_bwd(h, ab, f, dh, mods, g, w4, wo, layer, k0, n_lat, name, ex=None):
    s = h.shape[0]

    def body(h_ref, ab_ref, f_ref, dh_ref, m_ref, g_ref, w_ref, wo_ref, dhi_ref, dab_ref, df_ref, n_ref, act_ref, dm_ref):
        i = pl.program_id(0)

        @pl.when((i == 0) | (i == n_lat))
        def _():
            dm_ref[...] = jnp.zeros_like(dm_ref)

        hh, dho, gg = h_ref[...], dh_ref[...], g_ref[...]
        scale, gate = m_ref[k0 + 1:k0 + 2, :], m_ref[k0 + 2:k0 + 3, :]
        n, xhat, r, y = _norm_mod(hh, gg, m_ref[k0:k0 + 1, :], scale)
        n_ref[...] = n.astype(BF16)
        dgate = _sum0(dho * (0.5 * f_ref[...]))
        dfb = ((0.5 * gate) * dho).astype(BF16)
        df_ref[...] = dfb
        dn = jnp.zeros((TM, D), F32)
        for j in range(2):
            a = ab_ref[:, j * FF_COLS:(j + 1) * FF_COLS].astype(F32)
            b = ab_ref[:, (2 + j) * FF_COLS:(3 + j) * FF_COLS].astype(F32)
            sg = jax.nn.sigmoid(a)
            sa = a * sg
            act_ref[:, j * FF_COLS:(j + 1) * FF_COLS] = (sa * b).astype(BF16)
            dact = _dotg(dfb, wo_ref[j * FF_COLS:(j + 1) * FF_COLS, :], NT)
            da = (dact * b * (sg * (1 + a * (1 - sg)))).astype(BF16)
            db = (dact * sa).astype(BF16)
            dab_ref[:, j * FF_COLS:(j + 1) * FF_COLS] = da
            dab_ref[:, (2 + j) * FF_COLS:(3 + j) * FF_COLS] = db
            dn = dn + _dotg(da, w_ref[j], NT) + _dotg(db, w_ref[2 + j], NT)
        dsh, dsc, dg, dhn = _norm_mod_bwd(dn, xhat, r, y, gg, scale)
        dhi_ref[...] = dho + dhn
        dm_ref[0:1, :] += dsh
        dm_ref[1:2, :] += dsc
        dm_ref[2:3, :] += dgate
        dm_ref[3:4, :] += dg

    return _grid_call(
        body, name, s // TM,
        [_rows(D), _rows(2 * D_FF), _rows(D), _rows(D), _mods_spec(layer, n_lat), _whole((1, D), (layer,)),
         _whole((N_SLOT, D, FF_COLS)), _whole((D_FF, D))],
        [_rows(D), _rows(2 * D_FF), _rows(D), _rows(D), _rows(D_FF), _acc_spec(n_lat)],
        [jax.ShapeDtypeStruct((s, D), F32), jax.ShapeDtypeStruct((s, 2 * D_FF), BF16), jax.ShapeDtypeStruct((s, D), BF16),
         jax.ShapeDtypeStruct((s, D), BF16), jax.ShapeDtypeStruct((s, D_FF), BF16), jax.ShapeDtypeStruct((2, 8, D), F32)],
        (h, ab, f, dh, mods, g, w4, wo), "arbitrary", ex)


def _token_tile(s, limit=2176):
    return max(ts for ts in range(16, limit + 1, 16) if s % ts == 0)


def wgrad(a, b, tk, tn, slot_cols, name, ex=None):
    s, k = a.shape
    n = b.shape[1]
    a_bufs, b_bufs = (1 if k == tk else 2), (1 if n == tn else 2)
    whole = 2 * s * (a_bufs * tk + b_bufs * tn) + 2 * 6 * tk * tn
    ts = s if whole <= WGRAD_VMEM_BYTES else _token_tile(s)
    steps = s // ts
    once = dict(pipeline_mode=pl.Buffered(1))

    def body(a_ref, b_ref, o_ref, o16_ref):
        r = _dotg(a_ref[...], b_ref[...], TN)
        si = pl.program_id(2)

        @pl.when(si == 0)
        def _():
            o_ref[...] = r

        @pl.when(si > 0)
        def _():
            o_ref[...] += r

        @pl.when(si == steps - 1)
        def _():
            o16_ref[...] = o_ref[...].astype(BF16)

    n_outer = tn > tk
    grid = (n // tn, k // tk, steps) if n_outer else (k // tk, n // tn, steps)
    ij = (lambda g0, g1: (g1, g0)) if n_outer else (lambda g0, g1: (g0, g1))

    def a_map(g0, g1, si):
        return si, ij(g0, g1)[0]

    def b_map(g0, g1, si):
        return si, ij(g0, g1)[1]

    if slot_cols is None:
        shape, spec = (k, n), pl.BlockSpec((tk, tn), lambda g0, g1, si: ij(g0, g1))
    else:
        per = slot_cols // tn

        def slot_map(g0, g1, si):
            i, j = ij(g0, g1)
            return lax.div(j, per), i, lax.rem(j, per)

        shape, spec = (n // slot_cols, k, slot_cols), pl.BlockSpec((None, tk, tn), slot_map)
    return _grid_call(
        body, name, grid,
        [pl.BlockSpec((ts, tk), a_map, **(once if a_bufs == 1 and steps == 1 else {})),
         pl.BlockSpec((ts, tn), b_map, **(once if b_bufs == 1 and steps == 1 else {}))], [spec, spec],
        [jax.ShapeDtypeStruct(shape, F32), jax.ShapeDtypeStruct(shape, BF16)], (a, b), ("parallel", "parallel", "arbitrary"), ex)


def proj_fwd(h, mods, g, w_in, cos, sin, layer, n_lat, name, ex=None):
    s = h.shape[0]

    def body(h_ref, m_ref, g_ref, w_ref, cos_ref, sin_ref, u_ref, q_ref, k_ref, v_ref):
        n, _, _, _ = _norm_mod(h_ref[...], g_ref[...], m_ref[3:4, :], m_ref[4:5, :])
        p = _dot(n.astype(BF16), w_ref[...])
        cs, sn = cos_ref[...], sin_ref[...]
        u_ref[...] = p[:, :POOL_W]
        q_ref[...] = (_rope(p[:, POOL_W:POOL_W + ATTN_W], cs, sn) * HEAD ** -0.5).astype(BF16)
        k_ref[...] = _rope(p[:, POOL_W + ATTN_W:POOL_W + ATTN_W + KV_W], cs, sn).astype(BF16)
        v_ref[...] = p[:, POOL_W + ATTN_W + KV_W:].astype(BF16)

    return _grid_call(
        body, name, s // TM,
        [_rows(D), _mods_spec(layer, n_lat), _whole((1, D), (layer,)), _whole((D, PROJ_W)), _rows(BLK), _rows(BLK)],
        [_rows(POOL_W), _rows(ATTN_W), _rows(KV_W), _rows(KV_W)],
        [jax.ShapeDtypeStruct((s, POOL_W), F32), jax.ShapeDtypeStruct((s, ATTN_W), BF16),
         jax.ShapeDtypeStruct((s, KV_W), BF16), jax.ShapeDtypeStruct((s, KV_W), BF16)],
        (h, mods, g, w_in, cos, sin), "parallel", ex)


def proj_bwd(h, du, dq, dk, dv, dh, mods, g, w_in, cos, sin, layer, n_lat, name):
    s = h.shape[0]

    def body(h_ref, du_ref, dq_ref, dk_ref, dv_ref, dh_ref, m_ref, g_ref, w_ref, cos_ref, sin_ref,
             dhi_ref, dp_ref, n_ref, dm_ref):
        i = pl.program_id(0)

        @pl.when((i == 0) | (i == n_lat))
        def _():
            dm_ref[...] = jnp.zeros_like(dm_ref)

        gg, scale = g_ref[...], m_ref[4:5, :]
        n, xhat, r, y = _norm_mod(h_ref[...], gg, m_ref[3:4, :], scale)
        n_ref[...] = n.astype(BF16)
        cs, sn = cos_ref[...], sin_ref[...]
        dp = jnp.concatenate([du_ref[...], _unrope(dq_ref[...], cs, sn) * HEAD ** -0.5, _unrope(dk_ref[...], cs, sn),
                              dv_ref[...]], axis=1).astype(BF16)
        dp_ref[...] = dp
        dsh, dsc, dg, dhn = _norm_mod_bwd(_dotg(dp, w_ref[...], NT), xhat, r, y, gg, scale)
        dhi_ref[...] = dh_ref[...] + dhn
        dm_ref[0:1, :] += dsh
        dm_ref[1:2, :] += dsc
        dm_ref[3:4, :] += dg

    return pl.pallas_call(
        body, name=name, grid=(s // TM,),
        in_specs=[_rows(D), _rows(POOL_W), _rows(ATTN_W), _rows(KV_W), _rows(KV_W), _rows(D), _mods_spec(layer, n_lat),
                  _whole((1, D), (layer,)), _whole((D, PROJ_W)), _rows(BLK), _rows(BLK)],
        out_specs=[_rows(D), _rows(PROJ_W), _rows(D), _acc_spec(n_lat)],
        out_shape=[jax.ShapeDtypeStruct((s, D), F32), jax.ShapeDtypeStruct((s, PROJ_W), BF16),
                   jax.ShapeDtypeStruct((s, D), BF16), jax.ShapeDtypeStruct((2, 8, D), F32)],
        compiler_params=_params("arbitrary"),
    )(h, du, dq, dk, dv, dh, mods, g, w_in, cos, sin)


def _window(i, s):
    return pl.multiple_of(jnp.clip(i * QB - BLK, 0, s - WIN), BLK)


def mix_tables(t, s):
    n_lat = t // QB
    blocks = jnp.array([0, 1, n_lat - 1] + list(range(n_lat, s // QB)))[:, None, None]
    ws = jnp.clip(blocks * QB - BLK, 0, s - WIN)
    q = blocks * QB + jnp.arange(QB)[None, :, None]
    k = ws + jnp.arange(WIN)[None, None, :]
    is_lat = blocks < n_lat
    local = jnp.where(is_lat & (k < t) & (jnp.abs(k - q) <= BLK), 0.0, NEG_INF).astype(F32)
    bias = jnp.concatenate([local, jnp.zeros(local.shape[:2] + (s - t,), F32)], axis=2)
    seq_lo, seq_hi = jnp.where(is_lat, 0, t), jnp.where(is_lat, t, s)
    bands, counts = [], []
    for w in POOL_WINDOWS:
        lo, hi = jnp.maximum(q - w // 2, seq_lo), jnp.minimum(q + w - w // 2, seq_hi)
        bands.append((k >= lo) & (k < hi))
        counts.append((hi - lo).astype(F32))
    band = jnp.stack(bands, axis=1).astype(BF16)
    count = jnp.concatenate(counts + [jnp.ones(counts[0].shape[:2] + (BLK - len(counts),), F32)], axis=2)
    return dict(bias=bias, band=band, band_t=band.transpose(0, 1, 3, 2), count=count)


def _case_spec(table, n_lat_blk):
    def kind(i):
        return jnp.where(i < n_lat_blk - 1, jnp.minimum(i, 1), i - n_lat_blk + 3)

    shape = table.shape[1:]
    return pl.BlockSpec((None,) + shape, lambda i: (kind(i),) + (0,) * len(shape))


def _split_dot(band, v):
    return _dot(band, v.astype(BF16))


def _pooled(u_ref, band_ref, cnt_ref, i, ws, gi):
    cols = slice(gi * GROUP, (gi + 1) * GROUP)
    mean = _split_dot(band_ref[gi], u_ref[pl.ds(ws, WIN), cols]) / cnt_ref[:, gi:gi + 1]
    return mean - u_ref[pl.ds(pl.multiple_of(i * QB, QB), QB), cols]


def _head_cols(hd):
    return slice(hd * HEAD, (hd + 1) * HEAD)


def _stack_heads(x, hk, first=0):
    return jnp.concatenate([x[:, first + (Q_GROUP * hk + g) * HEAD:first + (Q_GROUP * hk + g + 1) * HEAD]
                            for g in range(Q_GROUP)], axis=0)


def _biased(scores, bias):
    return (scores.reshape(Q_GROUP, QB, -1) + bias).reshape(Q_GROUP * QB, -1)


def _group_column(vals):
    row = lax.broadcasted_iota(jnp.int32, (Q_GROUP * QB, 1), 0)
    out = jnp.full((Q_GROUP * QB, 1), vals[Q_GROUP - 1], F32)
    for g in range(Q_GROUP - 2, -1, -1):
        out = jnp.where(row < (g + 1) * QB, vals[g], out)
    return out


def _lane_place(cols, width=BLK):
    lane = lax.broadcasted_iota(jnp.int32, (cols[0].shape[0], width), 1)
    out = jnp.zeros((cols[0].shape[0], width), F32)
    for hd, c in enumerate(cols):
        out = jnp.where(lane == hd, c, out)
    return out


def mix_fwd(h, q, k, v, u, w_pool, pool_scale, sink, w_out, mods, tables, layer, t, name, ex=None):
    s = h.shape[0]
    n_lat_blk = t // QB

    def body(h_ref, q_ref, k_ref, v_ref, u_ref, wp_ref, ps_ref, sink_ref, wo_ref, m_ref, bias_ref, band_ref, cnt_ref,
             ho_ref, cat_ref, lse_ref, mo_ref):
        i = pl.program_id(0)
        ws = _window(i, s)
        for gi in range(len(POOL_WINDOWS)):
            mixed = _dot(_pooled(u_ref, band_ref, cnt_ref, i, ws, gi).astype(BF16), wp_ref[gi])
            cat_ref[:, gi * GROUP:(gi + 1) * GROUP] = (mixed * ps_ref[:, gi * GROUP:(gi + 1) * GROUP]).astype(BF16)
        bias = bias_ref[...]
        k_all = jnp.concatenate([k_ref[pl.ds(ws, WIN), :], k_ref[t:s, :]], axis=0)
        v_all = jnp.concatenate([v_ref[pl.ds(ws, WIN), :], v_ref[t:s, :]], axis=0)
        lses = []
        for hk in range(N_HEADS // Q_GROUP):
            kv = _head_cols(hk)
            sc = _biased(_dotg(_stack_heads(q_ref[...], hk), k_all[:, kv], NT), bias)
            sk = _group_column([sink_ref[layer, Q_GROUP * hk + g] for g in range(Q_GROUP)])
            m = jnp.maximum(jnp.max(sc, axis=1, keepdims=True), sk)
            e = jnp.exp(sc - m)
            l = jnp.sum(e, axis=1, keepdims=True) + jnp.exp(sk - m)
            o = _dot(e.astype(BF16), v_all[:, kv]) * (1.0 / l)
            lse = m + jnp.log(l)
            for g in range(Q_GROUP):
                hd = Q_GROUP * hk + g
                cat_ref[:, POOL_W + hd * HEAD:POOL_W + (hd + 1) * HEAD] = o[g * QB:(g + 1) * QB].astype(BF16)
                lses.append(lse[g * QB:(g + 1) * QB])
        lse_ref[...] = _lane_place(lses)
        mo = _dot(cat_ref[...], wo_ref[...])
        mo_ref[...] = mo
        ho_ref[...] = h_ref[...] + m_ref[5:6, :] * mo

    blk = lambda cols: _rows(cols, QB)
    return _grid_call(
        body, name, s // QB,
        [blk(D), blk(ATTN_W), _whole((s, KV_W)), _whole((s, KV_W)), _whole((s, POOL_W)),
         _whole((len(POOL_WINDOWS), GROUP, GROUP), (layer,)), _whole((1, POOL_W), (layer,)),
         pl.BlockSpec(memory_space=pltpu.SMEM), _whole((POOL_W + ATTN_W, D)), _mods_spec(layer, n_lat_blk),
         _case_spec(tables["bias"], n_lat_blk), _case_spec(tables["band"], n_lat_blk), _case_spec(tables["count"], n_lat_blk)],
        [blk(D), blk(POOL_W + ATTN_W), blk(BLK), blk(D)],
        [jax.ShapeDtypeStruct((s, D), F32), jax.ShapeDtypeStruct((s, POOL_W + ATTN_W), BF16), jax.ShapeDtypeStruct((s, BLK), F32),
         jax.ShapeDtypeStruct((s, D), F32)],
        (h, q, k, v, u, w_pool, pool_scale, sink, w_out, mods, tables["bias"], tables["band"], tables["count"]), "parallel", ex)


def mix_bwd(dh, mo, q, k, v, u, lse, w_pool, pool_scale, sink, w_out, mods, tables, layer, t, name, ex=None):
    s = dh.shape[0]
    n_lat_blk = t // QB
    n_grp = len(POOL_WINDOWS)

    def body(dh_ref, mo_ref, q_ref, k_ref, v_ref, u_ref, lse_ref, wp_ref, ps_ref, sink_ref, wo_ref, m_ref,
             bias_ref, band_ref, band_t_ref, cnt_ref,
             dq_ref, dk_ref, dv_ref, du_ref, dmo_ref, dwp_ref, dps_ref, dsink_ref, dm_ref):
        i = pl.program_id(0)

        @pl.when(i == 0)
        def _():
            for ref in (dk_ref, dv_ref, du_ref, dwp_ref, dps_ref, dsink_ref):
                ref[...] = jnp.zeros_like(ref)

        @pl.when((i == 0) | (i == n_lat_blk))
        def _():
            dm_ref[...] = jnp.zeros_like(dm_ref)

        ws = _window(i, s)
        here = pl.ds(pl.multiple_of(i * QB, QB), QB)
        dho = dh_ref[...]
        dm_ref[2:3, :] += _sum0(dho * mo_ref[...])
        dmo = (m_ref[5:6, :] * dho).astype(BF16)
        dmo_ref[...] = dmo
        dcat = _dotg(dmo, wo_ref[...], NT)

        for gi in range(n_grp):
            cols = slice(gi * GROUP, (gi + 1) * GROUP)
            pooled = _pooled(u_ref, band_ref, cnt_ref, i, ws, gi).astype(BF16)
            dpo = dcat[:, cols]
            dps_ref[0:1, cols] += _sum0(dpo * _dot(pooled, wp_ref[gi]))
            dmixed = (dpo * ps_ref[:, cols]).astype(BF16)
            dwp_ref[gi] += _dotg(pooled, dmixed, TN)
            dpooled = _dotg(dmixed, wp_ref[gi], NT)
            du_ref[pl.ds(ws, WIN), cols] += _split_dot(band_t_ref[gi], dpooled / cnt_ref[:, gi:gi + 1])
            du_ref[here, cols] -= dpooled

        bias = bias_ref[...]
        k_all = jnp.concatenate([k_ref[pl.ds(ws, WIN), :], k_ref[t:s, :]], axis=0)
        v_all = jnp.concatenate([v_ref[pl.ds(ws, WIN), :], v_ref[t:s, :]], axis=0)
        qq, lse_all = q_ref[...], lse_ref[...]
        dqs, dsinks, dks, dvs = [], [], [], []
        for hk in range(N_HEADS // Q_GROUP):
            kv = _head_cols(hk)
            q4 = _stack_heads(qq, hk)
            lse = jnp.concatenate([lse_all[:, Q_GROUP * hk + g:Q_GROUP * hk + g + 1] for g in range(Q_GROUP)], axis=0)
            p = jnp.exp(_biased(_dotg(q4, k_all[:, kv], NT), bias) - lse)
            do = _stack_heads(dcat, hk, POOL_W).astype(BF16)
            dp = _dotg(do, v_all[:, kv], NT)
            delta = jnp.sum(p * dp, axis=1, keepdims=True)
            ds = (p * (dp - delta)).astype(BF16)
            sk = _group_column([sink_ref[layer, Q_GROUP * hk + g] for g in range(Q_GROUP)])
            dsk = -jnp.exp(sk - lse) * delta
            dq = _dot(ds, k_all[:, kv])
            for g in range(Q_GROUP):
                dqs.append(dq[g * QB:(g + 1) * QB])
                dsinks.append(_sum0(dsk[g * QB:(g + 1) * QB]))
            dks.append(_dotg(ds, q4, TN))
            dvs.append(_dotg(p.astype(BF16), do, TN))
        dq_ref[...] = jnp.concatenate(dqs, axis=1)
        dk, dv = jnp.concatenate(dks, axis=1), jnp.concatenate(dvs, axis=1)
        dk_ref[pl.ds(ws, WIN), :] += dk[:WIN]
        dv_ref[pl.ds(ws, WIN), :] += dv[:WIN]
        dk_ref[t:s, :] += dk[WIN:]
        dv_ref[t:s, :] += dv[WIN:]
        dsink_ref[0:1, :] += _lane_place(dsinks)

    blk = lambda cols: _rows(cols, QB)
    full = lambda shape: pl.BlockSpec(shape, lambda i: (0,) * len(shape))
    return _grid_call(
        body, name, s // QB,
        [blk(D), blk(D), blk(ATTN_W), _whole((s, KV_W)), _whole((s, KV_W)), _whole((s, POOL_W)),
         blk(BLK), _whole((n_grp, GROUP, GROUP), (layer,)), _whole((1, POOL_W), (layer,)),
         pl.BlockSpec(memory_space=pltpu.SMEM), _whole((POOL_W + ATTN_W, D)), _mods_spec(layer, n_lat_blk)]
        + [_case_spec(tables[key], n_lat_blk) for key in ("bias", "band", "band_t", "count")],
        [blk(ATTN_W), full((s, KV_W)), full((s, KV_W)), full((s, POOL_W)), blk(D),
         full((n_grp, GROUP, GROUP)), full((8, POOL_W)), full((8, BLK)), _acc_spec(n_lat_blk)],
        [jax.ShapeDtypeStruct((s, ATTN_W), F32), jax.ShapeDtypeStruct((s, KV_W), F32),
         jax.ShapeDtypeStruct((s, KV_W), F32), jax.ShapeDtypeStruct((s, POOL_W), F32),
         jax.ShapeDtypeStruct((s, D), BF16), jax.ShapeDtypeStruct((n_grp, GROUP, GROUP), F32),
         jax.ShapeDtypeStruct((8, POOL_W), F32), jax.ShapeDtypeStruct((8, BLK), F32), jax.ShapeDtypeStruct((2, 8, D), F32)],
        (dh, mo, q, k, v, u, lse, w_pool, pool_scale, sink, w_out, mods, tables["bias"], tables["band"], tables["band_t"],
         tables["count"]), "arbitrary", ex)


def loss_head(h, target, g, t, name):
    s = h.shape[0]
    n_lat = t // TM

    def body(h_ref, t_ref, g_ref, dh_ref, acc_ref):
        i = pl.program_id(0)

        @pl.when(i == 0)
        def _():
            acc_ref[...] = jnp.zeros_like(acc_ref)

        @pl.when(i < n_lat)
        def _():
            hh, gg = h_ref[...], g_ref[...]
            r = lax.rsqrt(jnp.mean(hh * hh, axis=-1, keepdims=True) + EPS)
            xhat = hh * r
            err = xhat * gg - t_ref[...]
            dy = err * (1.0 / D)
            dx = dy * gg
            dh_ref[...] = r * (dx - xhat * jnp.mean(dx * xhat, axis=-1, keepdims=True))
            acc_ref[0:1, :] += _sum0(dy * xhat)
            acc_ref[1:2, :] += _sum0(err * err)

        @pl.when(i >= n_lat)
        def _():
            dh_ref[...] = jnp.zeros_like(dh_ref)

    return pl.pallas_call(
        body, name=name, grid=(s // TM,),
        in_specs=[_rows(D), pl.BlockSpec((TM, D), lambda i: (jnp.minimum(i, n_lat - 1), 0)), _whole((1, D))],
        out_specs=[_rows(D), pl.BlockSpec((8, D), lambda i: (0, 0))],
        out_shape=[jax.ShapeDtypeStruct((s, D), F32), jax.ShapeDtypeStruct((8, D), F32)],
        compiler_params=_params("arbitrary"),
    )(h, target, g)


def mod_rows(c_all, w_mod, b_cols, name):
    def body(c_ref, w_ref, b_ref, o_ref):
        cc = c_ref[...]
        o_ref[...] = _dot((cc * jax.nn.sigmoid(cc)).astype(BF16), w_ref[...].astype(BF16)) + b_ref[...]

    return pl.pallas_call(
        body, name=name, grid=(2,),
        in_specs=[pl.BlockSpec((16, D), lambda l: (0, 0)), pl.BlockSpec((None, D, MOD_COLS), lambda l: (l, 0, 0)),
                  pl.BlockSpec((None, 1, MOD_COLS), lambda l: (l, 0, 0))],
        out_specs=pl.BlockSpec((None, 16, MOD_COLS), lambda l: (l, 0, 0)),
        out_shape=jax.ShapeDtypeStruct((2, 16, MOD_COLS), F32),
        compiler_params=_params("parallel"),
    )(c_all, w_mod, b_cols)


def mod_grads(c_all, dmod_cols, w_mod, name):
    def body(c_ref, d_ref, w_ref, dw_ref, dc_ref):
        @pl.when(pl.program_id(0) == 0)
        def _():
            dc_ref[...] = jnp.zeros_like(dc_ref)

        cc = c_ref[...]
        dd = d_ref[...].astype(BF16)
        dw_ref[...] = _dotg((cc * jax.nn.sigmoid(cc)).astype(BF16), dd, TN)
        dc_ref[...] += _dotg(dd, w_ref[...].astype(BF16), NT)

    return pl.pallas_call(
        body, name=name, grid=(2,),
        in_specs=[pl.BlockSpec((16, D), lambda l: (0, 0)), pl.BlockSpec((None, 16, MOD_COLS), lambda l: (l, 0, 0)),
                  pl.BlockSpec((None, D, MOD_COLS), lambda l: (l, 0, 0))],
        out_specs=[pl.BlockSpec((None, D, MOD_COLS), lambda l: (l, 0, 0)), pl.BlockSpec((16, D), lambda l: (0, 0))],
        out_shape=[jax.ShapeDtypeStruct((2, D, MOD_COLS), F32), jax.ShapeDtypeStruct((16, D), F32)],
        compiler_params=_params("arbitrary"),
    )(c_all, dmod_cols, w_mod)


def _row_tile(rows, cols, n_arrays):
    budget = VMEM_LIMIT_BYTES // 4 // (2 * 4 * n_arrays * cols)
    best = None
    for tr in range(16, rows + 1, 16):
        if rows % tr == 0 and tr <= budget:
            best = tr
    return best if best is not None else rows


def elementwise(fn, ins, out_dtypes, name, ex=None):
    rows, cols = ins[0].shape
    tr = _row_tile(rows, cols, len(ins) + len(out_dtypes))

    def body(*refs):
        outs = fn(*[r[...] for r in refs[:len(ins)]])
        for o_ref, o in zip(refs[len(ins):], outs):
            o_ref[...] = o.astype(o_ref.dtype)

    spec = pl.BlockSpec((tr, cols), lambda i: (i, 0))
    outs, got = _grid_call(body, name, rows // tr, [spec] * len(ins), [spec] * len(out_dtypes),
                           [jax.ShapeDtypeStruct((rows, cols), dt) for dt in out_dtypes], ins, "parallel", ex)
    return outs if ex is None else (outs, got)


def _adamw_tile(w, g, m, v):
    m = ADAM_B1 * m + (1.0 - ADAM_B1) * g
    v = ADAM_B2 * v + (1.0 - ADAM_B2) * (g * g)
    m_hat = m / (1.0 - ADAM_B1 ** ADAM_STEP)
    v_hat = v / (1.0 - ADAM_B2 ** ADAM_STEP)
    return -ADAM_LR * (m_hat / (jnp.sqrt(v_hat) + ADAM_EPS) + ADAM_WD * w), m, v


def adamw(w, g, m, v, name, ex=None):
    shape = w.shape
    two_d = (-1, shape[-1]) if w.ndim > 1 else (1, -1)
    outs = elementwise(_adamw_tile, [a.reshape(two_d) for a in (w, g, m, v)], [F32] * 3, name, ex)
    outs, got = outs if ex is not None else (outs, None)
    outs = [o.reshape(shape) for o in outs]
    return outs if ex is None else (outs, got)


def _prefetch_call(body, name, grid, in_specs, out_specs, out_shape, place, args, ex=None):
    if ex is None:
        spec = pltpu.PrefetchScalarGridSpec(num_scalar_prefetch=1, grid=grid, in_specs=in_specs, out_specs=out_specs)
        return pl.pallas_call(body, name=name, grid_spec=spec, out_shape=out_shape,
                              compiler_params=_params(*["parallel"] * len(grid)))(place, *args)
    n_in, n_out, ci, co = len(in_specs), len(out_specs), len(ex["ins"]), len(ex["out_shape"])
    spec = pltpu.PrefetchScalarGridSpec(num_scalar_prefetch=1, grid=grid, in_specs=list(in_specs) + _any(ci),
                                        out_specs=list(out_specs) + _any(co), scratch_shapes=ex["scratch"])
    outs = pl.pallas_call(
        _carrying(body, grid, n_in, n_out, ex, lead=1), name=name, grid_spec=spec, out_shape=list(out_shape) + ex["out_shape"],
        input_output_aliases={1 + n_in + i: n_out + j for i, j in ex["aliases"].items()},
        compiler_params=_params(*["arbitrary"] * len(grid)))(place, *args, *ex["ins"])
    return outs[:n_out], outs[n_out:]


def cast_place(w, layer, place, name):
    _, r, c = w.shape
    tr = _row_tile(r, c, 2)

    def body(p_ref, w_ref, o_ref):
        o_ref[...] = w_ref[...].astype(BF16)

    return _prefetch_call(
        body, name, (r // tr,), [pl.BlockSpec((None, tr, c), lambda i, p: (layer, i, 0))],
        pl.BlockSpec((None, tr, c), lambda i, p: (p[1], i, 0)), jax.ShapeDtypeStruct((N_SLOT, r, c), BF16), place, [w])


def pair_sum(g32, got, place, name, ex=None):
    n_slot, rh, c = got.shape
    tr = _row_tile(rh, c, 4)
    per = rh // tr

    def body(p_ref, a_ref, b_ref, o16_ref):
        o16_ref[...] = (a_ref[...] + b_ref[...].astype(F32)).astype(BF16)

    half = pl.BlockSpec((None, tr, c), lambda s, i, p: (s, i, 0))
    out = _prefetch_call(
        body, name, (n_slot, per), [pl.BlockSpec((None, tr, c), lambda s, i, p: (s, p[0] * per + i, 0)), half], [half],
        [jax.ShapeDtypeStruct(got.shape, BF16)], place, [g32, got], ex)
    return [(g32, got), out[0]] if ex is None else ([(g32, got), out[0][0]], out[1])


def chip_sum(terms, got, place, name):
    g32, sent = terms
    _, rh, c = sent.shape
    tr = _row_tile(rh, c, 6)
    per = rh // tr

    def body(p_ref, a_ref, b_ref, r0_ref, r1_ref, r2_ref, o_ref):
        own = a_ref[...] + b_ref[...].astype(F32)
        o_ref[...] = own + r0_ref[...].astype(F32) + r1_ref[...].astype(F32) + r2_ref[...].astype(F32)

    part = pl.BlockSpec((tr, c), lambda i, p: (i, 0))
    return _prefetch_call(
        body, name, (per,),
        [pl.BlockSpec((None, tr, c), lambda i, p: (p[1], p[0] * per + i, 0)), pl.BlockSpec((None, tr, c), lambda i, p: (p[1], i, 0)),
         part, part, part],
        pl.BlockSpec((tr, c), lambda i, p: (p[0] * per + i, 0)), jax.ShapeDtypeStruct((2 * rh, c), F32), place, [g32, sent, *got])


def adamw_layers(w, g0, g1, m, v, name, ex=None):
    _, r, c = w.shape
    tr = _row_tile(r, c, 10)

    def body(w_ref, g0_ref, g1_ref, m_ref, v_ref, g_ref, d_ref, mo_ref, vo_ref):
        g = jnp.where(pl.program_id(0) == 0, g0_ref[...], g1_ref[...])
        g_ref[...] = g
        d_ref[...], mo_ref[...], vo_ref[...] = _adamw_tile(w_ref[...], g, m_ref[...], v_ref[...])

    steps = r // tr
    stacked = pl.BlockSpec((None, tr, c), lambda l, i: (l, i, 0))
    layer0 = pl.BlockSpec((tr, c), lambda l, i: (jnp.where(l == 0, i, steps - 1), 0))
    layer1 = pl.BlockSpec((tr, c), lambda l, i: (jnp.where(l == 0, 0, i), 0))
    outs, got = _grid_call(body, name, (2, steps), [stacked, layer0, layer1, stacked, stacked], [stacked] * 4,
                           [jax.ShapeDtypeStruct(w.shape, F32)] * 4, (w, g0, g1, m, v), "parallel", ex)
    return outs if ex is None else (outs, got)


def sum8(gathered, name):
    def body(*refs):
        n = len(refs) // 2
        for g_ref, o_ref in zip(refs[:n], refs[n:]):
            acc = g_ref[0]
            for dev in range(1, N_DEV):
                acc = acc + g_ref[dev]
            o_ref[...] = acc

    return pl.pallas_call(
        body, name=name,
        out_shape=[jax.ShapeDtypeStruct(a.shape[1:], F32) for a in gathered],
        compiler_params=_params(),
    )(*gathered)


PHASES = ("start", "late", "finish")


def _place():
    return lax.axis_index("x"), lax.axis_index("y"), lax.axis_index("c")


def _any(n):
    return [pl.BlockSpec(memory_space=pl.ANY)] * n


def gather8_exchange(blocks):
    n = len(blocks)

    def copy(outs, sems, ti, k, block, to, src=None):
        dst = outs[ti].at[4 * block[0] + 2 * block[1] + block[2]]
        return pltpu.make_async_remote_copy(src_ref=dst if src is None else src, dst_ref=dst, send_sem=sems[0].at[ti, k],
                                            recv_sem=sems[1].at[ti, k], device_id=to, device_id_type=MESH)

    def first(ins, outs, sems):
        x, y, c = _place()
        local, sent = [], []
        for ti in range(n):
            local.append(pltpu.make_async_copy(ins[ti], outs[ti].at[4 * x + 2 * y + c], sems[2].at[ti]))
            sent.append(copy(outs, sems, ti, 0, (x, y, c), (x, y, 1 - c), src=ins[ti]))
            sent += [copy(outs, sems, ti, 1 + j, (x, y, c), (*chip, c), src=ins[ti]) for j, chip in enumerate(_three_chips(x, y))]
        return local, sent

    def start(ins, outs, sems):
        local, sent = first(ins, outs, sems)
        for cp in local + sent:
            cp.start()

    def passed_on(outs, sems):
        x, y, c = _place()
        return [copy(outs, sems, ti, 4 + j, (*chip, c), (x, y, 1 - c)) for ti in range(n) for j, chip in enumerate(_three_chips(x, y))]

    def late(ins, outs, sems):
        x, y, c = _place()
        on = passed_on(outs, sems)
        for ti in range(n):
            for j, chip in enumerate(_three_chips(x, y)):
                copy(outs, sems, ti, 1 + j, (*chip, c), (x, y, c)).wait_recv()
                on[3 * ti + j].start()

    def finish(ins, outs, sems):
        x, y, c = _place()
        me, sibling = (x, y, c), (x, y, 1 - c)
        local, sent = first(ins, outs, sems)
        for ti in range(n):
            copy(outs, sems, ti, 0, sibling, me).wait_recv()
            for j, chip in enumerate(_three_chips(x, y)):
                copy(outs, sems, ti, 4 + j, (*chip, 1 - c), me).wait_recv()
        for cp in sent + passed_on(outs, sems):
            cp.wait_send()
        for cp in local:
            cp.wait()

    return dict(ins=list(blocks), out_shape=[jax.ShapeDtypeStruct((N_DEV,) + b.shape, b.dtype) for b in blocks], aliases={},
                start=start, late=late, finish=finish,
                scratch=[pltpu.SemaphoreType.DMA((n, 7)), pltpu.SemaphoreType.DMA((n, 7)), pltpu.SemaphoreType.DMA((n,))])


def all_gather(blocks, name):
    return run_exchange(gather8_exchange(blocks), name)


def _three_chips(x, y):
    return [(1 - x, y), (x, 1 - y), (1 - x, 1 - y)]


def gather_exchange(placed):
    n = len(placed)

    def copy(bufs, sems, ti, k, chip, core, to):
        rh = bufs[ti].shape[1] // 2
        half = bufs[ti].at[2 * chip[0] + chip[1], pl.ds(core * rh, rh), :]
        return pltpu.make_async_remote_copy(src_ref=half, dst_ref=half, send_sem=sems[0].at[ti, k], recv_sem=sems[1].at[ti, k],
                                            device_id=to, device_id_type=MESH)

    def sends(bufs, sems):
        x, y, c = _place()
        return [copy(bufs, sems, ti, k, (x, y), c, (*chip, c)) for ti in range(n) for k, chip in enumerate(_three_chips(x, y))]

    def passed_on(bufs, sems):
        x, y, c = _place()
        return [copy(bufs, sems, ti, 3 + k, chip, c, (x, y, 1 - c)) for ti in range(n) for k, chip in enumerate(_three_chips(x, y))]

    def start(ins, bufs, sems):
        for cp in sends(bufs, sems):
            cp.start()

    def late(ins, bufs, sems):
        x, y, c = _place()
        on = passed_on(bufs, sems)
        for ti in range(n):
            for k, chip in enumerate(_three_chips(x, y)):
                copy(bufs, sems, ti, k, chip, c, (x, y, c)).wait_recv()
                on[3 * ti + k].start()

    def finish(ins, bufs, sems):
        x, y, c = _place()
        for ti in range(n):
            for k, chip in enumerate(_three_chips(x, y)):
                copy(bufs, sems, ti, 3 + k, chip, 1 - c, (x, y, c)).wait_recv()
        for cp in sends(bufs, sems) + passed_on(bufs, sems):
            cp.wait_send()

    return dict(ins=list(placed), out_shape=[jax.ShapeDtypeStruct(w.shape, w.dtype) for w in placed],
                aliases={i: i for i in range(n)}, start=start, late=late, finish=finish,
                scratch=[pltpu.SemaphoreType.DMA((n, 6)), pltpu.SemaphoreType.DMA((n, 6))])


def scatter_exchange(p16):
    n = len(p16)

    def copies(ins, got, sems):
        x, y, c = _place()
        return [pltpu.make_async_remote_copy(src_ref=ins[ti].at[2 * chip[0] + chip[1]], dst_ref=got[3 * ti + k],
                                             send_sem=sems[0].at[ti, k], recv_sem=sems[1].at[ti, k], device_id=(*chip, c),
                                             device_id_type=MESH)
                for ti in range(n) for k, chip in enumerate(_three_chips(x, y))]

    def start(ins, got, sems):
        for cp in copies(ins, got, sems):
            cp.start()

    def finish(ins, got, sems):
        for cp in copies(ins, got, sems):
            cp.wait()

    return dict(ins=list(p16), out_shape=[jax.ShapeDtypeStruct(a.shape[1:], BF16) for a in p16 for _ in range(3)], aliases={},
                start=start, finish=finish, scratch=[pltpu.SemaphoreType.DMA((n, 3)), pltpu.SemaphoreType.DMA((n, 3))])


def run_exchange(ex, name):
    ci, co = len(ex["ins"]), len(ex["out_shape"])

    def body(*refs):
        ins, outs, sems = refs[:ci], refs[ci:ci + co], refs[ci + co:]
        for phase in PHASES:
            if phase in ex:
                ex[phase](ins, outs, sems)

    return pl.pallas_call(body, name=name, in_specs=_any(ci), out_specs=_any(co), out_shape=ex["out_shape"],
                          input_output_aliases=ex["aliases"], scratch_shapes=ex["scratch"])(*ex["ins"])


def _carrying(body, grid, n_in, n_out, ex, lead=0):
    ci, co = len(ex["ins"]), len(ex["out_shape"])
    first, last = (0,) * len(grid), tuple(g - 1 for g in grid)
    steps = dict(start=first, late=(grid[0] - 2,) if len(grid) == 1 and grid[0] > 2 else last, finish=last)

    def at(ids):
        return functools.reduce(jnp.logical_and, [pl.program_id(ax) == v for ax, v in enumerate(ids)])

    def carrying(*refs):
        head, refs = refs[:lead], refs[lead:]
        c_in, c_out = refs[n_in:n_in + ci], refs[n_in + ci + n_out:n_in + ci + n_out + co]
        sems = refs[n_in + ci + n_out + co:]
        for phase in PHASES:
            if phase == "finish":
                body(*head, *refs[:n_in], *refs[n_in + ci:n_in + ci + n_out])
            if phase in ex:
                pl.when(at(steps[phase]))(functools.partial(ex[phase], c_in, c_out, sems))

    return carrying


def _grid_call(body, name, grid, in_specs, out_specs, out_shape, args, sem, ex=None):
    grid = (grid,) if isinstance(grid, int) else tuple(grid)
    sems_of = (sem,) * len(grid) if isinstance(sem, str) else tuple(sem)
    n_in, n_out = len(in_specs), len(out_specs)
    if ex is None:
        return pl.pallas_call(body, name=name, grid=grid, in_specs=in_specs, out_specs=out_specs, out_shape=out_shape,
                              compiler_params=_params(*sems_of))(*args), []
    ci, co = len(ex["ins"]), len(ex["out_shape"])
    outs = pl.pallas_call(
        _carrying(body, grid, n_in, n_out, ex), name=name, grid=grid, in_specs=list(in_specs) + _any(ci),
        out_specs=list(out_specs) + _any(co), out_shape=list(out_shape) + ex["out_shape"], scratch_shapes=ex["scratch"],
        input_output_aliases={n_in + i: n_out + j for i, j in ex["aliases"].items()},
        compiler_params=_params(*["arbitrary"] * len(grid)),
    )(*args, *ex["ins"])
    return outs[:n_out], outs[n_out:]


def both(*exchanges):
    exchanges = [ex for ex in exchanges if ex is not None]
    if len(exchanges) < 2:
        return exchanges[0] if exchanges else None
    n_ins = [len(ex["ins"]) for ex in exchanges]
    n_outs = [len(ex["out_shape"]) for ex in exchanges]
    n_sems = [len(ex["scratch"]) for ex in exchanges]

    def parts(seq, counts, k):
        first = sum(counts[:k])
        return seq[first:first + counts[k]]

    def run(phase):
        def go(ins, outs, sems):
            for k, ex in enumerate(exchanges):
                if phase in ex:
                    ex[phase](parts(ins, n_ins, k), parts(outs, n_outs, k), parts(sems, n_sems, k))
        return go

    aliases = {sum(n_ins[:k]) + i: sum(n_outs[:k]) + j for k, ex in enumerate(exchanges) for i, j in ex["aliases"].items()}
    return dict(ins=[a for ex in exchanges for a in ex["ins"]], out_shape=[o for ex in exchanges for o in ex["out_shape"]],
                aliases=aliases, scratch=[s for ex in exchanges for s in ex["scratch"]], **{ph: run(ph) for ph in PHASES})


def split_outputs(got, *exchanges):
    got, out = list(got), []
    for ex in exchanges:
        n = len(ex["out_shape"]) if ex is not None else 0
        out.append(got[:n])
        got = got[n:]
    return out


def pair_exchange(g16):
    n = len(g16)

    def copies(a16, got, sems):
        x, y, c = _place()
        out = []
        for ti in range(n):
            rh = a16[ti].shape[1] // 2
            out.append(pltpu.make_async_remote_copy(
                src_ref=a16[ti].at[:, pl.ds((1 - c) * rh, rh), :], dst_ref=got[ti], send_sem=sems[0].at[ti],
                recv_sem=sems[1].at[ti], device_id=(x, y, 1 - c), device_id_type=MESH))
        return out

    def start(a16, got, sems):
        for cp in copies(a16, got, sems):
            cp.start()

    def finish(a16, got, sems):
        for cp in copies(a16, got, sems):
            cp.wait()

    return dict(ins=list(g16), out_shape=[jax.ShapeDtypeStruct((a.shape[0], a.shape[1] // 2, a.shape[2]), BF16) for a in g16],
                aliases={}, start=start, finish=finish, scratch=[pltpu.SemaphoreType.DMA((n,)), pltpu.SemaphoreType.DMA((n,))])


def _gather_half(buf, chip, core):
    rh = buf.shape[1] // 2
    return buf.at[2 * chip[0] + chip[1], pl.ds(core * rh, rh), :]


def gather_start(placed, name):
    n = len(placed)
    hbm, sem = pl.BlockSpec(memory_space=pltpu.HBM), pl.BlockSpec(memory_space=pltpu.SEMAPHORE)

    def body(*refs):
        bufs, send_sems, recv_sems, token_ref = refs[:n], refs[n], refs[n + 1], refs[-1]
        x, y, c = _place()
        for ti in range(n):
            for k, chip in enumerate(_three_chips(x, y)):
                half = _gather_half(bufs[ti], (x, y), c)
                pltpu.make_async_remote_copy(src_ref=half, dst_ref=half, send_sem=send_sems.at[3 * ti + k],
                                             recv_sem=recv_sems.at[3 * ti + k], device_id=(*chip, c), device_id_type=MESH).start()
        token_ref[...] = jnp.zeros_like(token_ref)

    return pl.pallas_call(
        body, name=name,
        out_shape=(pltpu.SemaphoreType.DMA((3 * n,)), pltpu.SemaphoreType.DMA((3 * n,)), *[pltpu.HBM(w.shape, w.dtype) for w in placed],
                   jax.ShapeDtypeStruct((8, BLK), F32)),
        in_specs=(hbm,) * n, out_specs=(sem, sem, *(hbm,) * n, pl.BlockSpec(memory_space=pltpu.VMEM)),
        input_output_aliases={i: 2 + i for i in range(n)},
        compiler_params=pltpu.CompilerParams(has_side_effects=pltpu.SideEffectType.DATAFLOW_SIDE_EFFECTING),
    )(*[pltpu.with_memory_space_constraint(w, pltpu.HBM) for w in placed])


def gather_wait(send_sems, recv_sems, bufs, after, name):
    n = len(bufs)
    hbm, sem = pl.BlockSpec(memory_space=pltpu.HBM), pl.BlockSpec(memory_space=pltpu.SEMAPHORE)

    def body(*refs):
        bufs, send_sems, recv_sems = refs[:n], refs[n], refs[n + 1]
        x, y, c = _place()
        for ti in range(n):
            for k, chip in enumerate(_three_chips(x, y)):
                mine, theirs = _gather_half(bufs[ti], (x, y), c), _gather_half(bufs[ti], chip, c)
                cp = pltpu.make_async_remote_copy(src_ref=mine, dst_ref=theirs, send_sem=send_sems.at[3 * ti + k],
                                                  recv_sem=recv_sems.at[3 * ti + k], device_id=(*chip, c), device_id_type=MESH)
                cp.wait_send()
                cp.wait_recv()

    return pl.pallas_call(
        body, name=name, out_shape=tuple(pltpu.HBM(w.shape, w.dtype) for w in bufs),
        in_specs=(*(hbm,) * n, sem, sem, *_any(len(after))), out_specs=(hbm,) * n,
        input_output_aliases={i: i for i in range(n)},
        compiler_params=pltpu.CompilerParams(has_side_effects=pltpu.SideEffectType.DATAFLOW_SIDE_EFFECTING),
    )(*bufs, send_sems, recv_sems, *after)


def pass_on_exchange(bufs):
    n = len(bufs)

    def copies(refs, sems, core):
        x, y, c = _place()
        return [pltpu.make_async_remote_copy(src_ref=_gather_half(refs[ti], chip, c if core == "mine" else 1 - c),
                                             dst_ref=_gather_half(refs[ti], chip, c if core == "mine" else 1 - c),
                                             send_sem=sems[0].at[ti, k], recv_sem=sems[1].at[ti, k], device_id=(x, y, 1 - c),
                                             device_id_type=MESH)
                for ti in range(n) for k, chip in enumerate(_three_chips(x, y))]

    def start(ins, refs, sems):
        for cp in copies(refs, sems, "mine"):
            cp.start()

    def finish(ins, refs, sems):
        for cp in copies(refs, sems, "mine"):
            cp.wait_send()
        for cp in copies(refs, sems, "sibling's"):
            cp.wait_recv()

    return dict(ins=list(bufs), out_shape=[jax.ShapeDtypeStruct(w.shape, w.dtype) for w in bufs], aliases={i: i for i in range(n)},
                start=start, finish=finish, scratch=[pltpu.SemaphoreType.DMA((n, 3)), pltpu.SemaphoreType.DMA((n, 3))])


def _scatter_copies(src_ref, lands, send_sems, recv_sems):
    x, y, c = _place()
    return [pltpu.make_async_remote_copy(src_ref=src_ref.at[2 * chip[0] + chip[1]], dst_ref=lands[k], send_sem=send_sems.at[k],
                                         recv_sem=recv_sems.at[k], device_id=(*chip, c), device_id_type=MESH)
            for k, chip in enumerate(_three_chips(x, y))]


def scatter_start(p16, name):
    hbm, sem = pl.BlockSpec(memory_space=pltpu.HBM), pl.BlockSpec(memory_space=pltpu.SEMAPHORE)

    def body(src_ref, l0_ref, l1_ref, l2_ref, send_sems, recv_sems, src_thru, o0_ref, o1_ref, o2_ref, token_ref):
        for cp in _scatter_copies(src_ref, (l0_ref, l1_ref, l2_ref), send_sems, recv_sems):
            cp.start()
        token_ref[...] = jnp.zeros_like(token_ref)

    land = [pltpu.with_memory_space_constraint(lax.empty(p16.shape[1:], BF16), pltpu.HBM) for _ in range(3)]
    return pl.pallas_call(
        body, name=name,
        out_shape=(pltpu.SemaphoreType.DMA((3,)), pltpu.SemaphoreType.DMA((3,)), pltpu.HBM(p16.shape, BF16),
                   *[pltpu.HBM(p16.shape[1:], BF16)] * 3, jax.ShapeDtypeStruct((8, BLK), F32)),
        in_specs=(hbm,) * 4, out_specs=(sem, sem, hbm, hbm, hbm, hbm, pl.BlockSpec(memory_space=pltpu.VMEM)),
        input_output_aliases={0: 2, 1: 3, 2: 4, 3: 5},
        compiler_params=pltpu.CompilerParams(has_side_effects=pltpu.SideEffectType.DATAFLOW_SIDE_EFFECTING),
    )(pltpu.with_memory_space_constraint(p16, pltpu.HBM), *land)


def scatter_wait(send_sems, recv_sems, src_thru, lands, after, name):
    hbm, sem = pl.BlockSpec(memory_space=pltpu.HBM), pl.BlockSpec(memory_space=pltpu.SEMAPHORE)

    def body(src_ref, l0_ref, l1_ref, l2_ref, send_sems, recv_sems, *rest):
        for cp in _scatter_copies(src_ref, (l0_ref, l1_ref, l2_ref), send_sems, recv_sems):
            cp.wait_send()
            cp.wait_recv()

    return pl.pallas_call(
        body, name=name, out_shape=(pltpu.HBM(src_thru.shape, BF16), *[pltpu.HBM(lands[0].shape, BF16)] * 3),
        in_specs=(hbm, hbm, hbm, hbm, sem, sem, *_any(len(after))), out_specs=(hbm,) * 4,
        input_output_aliases={0: 0, 1: 1, 2: 2, 3: 3},
        compiler_params=pltpu.CompilerParams(has_side_effects=pltpu.SideEffectType.DATAFLOW_SIDE_EFFECTING),
    )(src_thru, *lands, send_sems, recv_sems, *after)[1:]


def pair_fill_exchange(halves):
    n = len(halves)

    def copies(bufs, sems, core):
        x, y, c = _place()
        out = []
        for ti in range(n):
            rh = bufs[ti].shape[0] // 2
            rows = bufs[ti].at[pl.ds((c if core == "mine" else 1 - c) * rh, rh), :]
            out.append(pltpu.make_async_remote_copy(src_ref=rows, dst_ref=rows, send_sem=sems[0].at[ti], recv_sem=sems[1].at[ti],
                                                    device_id=(x, y, 1 - c), device_id_type=MESH))
        return out

    def start(ins, bufs, sems):
        for cp in copies(bufs, sems, "mine"):
            cp.start()

    def finish(ins, bufs, sems):
        for cp in copies(bufs, sems, "mine"):
            cp.wait_send()
        for cp in copies(bufs, sems, "sibling's"):
            cp.wait_recv()

    return dict(ins=list(halves), out_shape=[jax.ShapeDtypeStruct(a.shape, a.dtype) for a in halves],
                aliases={i: i for i in range(n)}, start=start, finish=finish,
                scratch=[pltpu.SemaphoreType.DMA((n,)), pltpu.SemaphoreType.DMA((n,))])


def pair_gather(halves, name):
    return run_exchange(pair_fill_exchange(halves), name)


def reduce_small(dm_f1, dm_mix, dm_gate, dm_f2, loss_blk, name):
    def body(f1_ref, mix_ref, gate_ref, f2_ref, l_ref, tot_ref, rows_ref, fin_ref):
        rows_ref[...] = jnp.zeros_like(rows_ref)
        tot_ref[...] = jnp.zeros_like(tot_ref)
        mod_src = [(f1_ref, 0), (f1_ref, 1), (f1_ref, 2), (mix_ref, 0), (mix_ref, 1), (gate_ref, 2),
                   (f2_ref, 0), (f2_ref, 1), (f2_ref, 2)]
        norm_src = [(f1_ref, 3), (mix_ref, 3), (f2_ref, 3)]
        for l in range(2):
            for k, (ref, r) in enumerate(mod_src + norm_src):
                lat = ref[0, l, 0, r:r + 1, :]
                ctx = ref[0, l, 1, r:r + 1, :]
                for dev in range(N_DEV):
                    if dev:
                        lat = lat + ref[dev, l, 0, r:r + 1, :]
                        ctx = ctx + ref[dev, l, 1, r:r + 1, :]
                    if k < N_MOD:
                        rows_ref[l, dev, k:k + 1, :] = ref[dev, l, 0, r:r + 1, :]
                if k < N_MOD:
                    rows_ref[l, N_DEV, k:k + 1, :] = ctx
                tot_ref[l, k:k + 1, :] = lat + ctx
        acc = l_ref[0]
        for dev in range(1, N_DEV):
            acc = acc + l_ref[dev]
        loss = (0.5 / D) * jnp.sum(acc[1:2, :], axis=1, keepdims=True)
        row = lax.broadcasted_iota(jnp.int32, (8, D), 0)
        fin_ref[...] = jnp.where(row == 0, acc[0:1, :], loss)

    return pl.pallas_call(
        body, name=name,
        out_shape=[jax.ShapeDtypeStruct((2, 16, D), F32), jax.ShapeDtypeStruct((2, 16, 16, D), F32),
                   jax.ShapeDtypeStruct((8, D), F32)],
        compiler_params=_params(),
    )(dm_f1, dm_mix, dm_gate, dm_f2, loss_blk)


def rope_tables(t, s):
    rows = t // GRID_W
    row = jnp.repeat(jnp.arange(rows), GRID_W).astype(F32)
    col = jnp.tile(jnp.arange(GRID_W), rows).astype(F32)
    inv = ROPE_BASE ** (-jnp.arange(0, HEAD // 2, 2, dtype=F32) / (HEAD // 2))
    ang = jnp.concatenate([row[:, None] * inv, col[:, None] * inv], axis=-1)
    cos, sin = jnp.cos(ang), jnp.sin(ang)
    cos = jnp.concatenate([jnp.tile(cos, (1, 4)), jnp.ones((s - t, BLK), F32)], axis=0)
    sin = jnp.concatenate([jnp.tile(jnp.concatenate([-sin, sin], axis=1), (1, 2)), jnp.zeros((s - t, BLK), F32)], axis=0)
    return cos, sin


BIG = ("ffn1_in", "ffn1_out", "w_in", "w_out", "ffn2_in", "ffn2_out")
GROUPS = dict(ffn1=("ffn1_in", "ffn1_out"), mix=("w_in", "w_out"), ffn2=("ffn2_in", "ffn2_out"))
GATHER_BEHIND = {("ffn1", 0): [("w_in", 0), ("ffn2_out", 0), ("ffn1_out", 1)], ("proj", 0): [("w_out", 0)],
                 ("mix", 0): [("ffn2_in", 0)], ("ffn2", 0): [("ffn1_in", 1), ("w_in", 1)],
                 ("ffn1", 1): [("ffn2_in", 1), ("w_out", 1)], ("mix", 1): [("ffn2_out", 1)]}


def _slot_major(name, g):
    if name == "w_in":
        return jnp.stack(jnp.split(g, N_SLOT, axis=1), axis=0)
    if name in ("ffn1_in", "ffn2_in"):
        return g
    return g.reshape(N_SLOT, g.shape[0] // N_SLOT, g.shape[1])


def _whole_weight(name, buf):
    if name == "w_in":
        return buf.transpose(1, 0, 2).reshape(D, PROJ_W)
    if name in ("ffn1_in", "ffn2_in"):
        return buf
    return buf.reshape(-1, buf.shape[2])


def local_step(x1, ctx1, target, mods, norms, nfinal, placed, w_pool, pool_scale, sink, place, small_blocks):
    t, s = x1.shape[0], x1.shape[0] + ctx1.shape[0]
    n_lat = t // TM
    cos, sin = rope_tables(t, s)
    tables = mix_tables(t, s)
    wts ={name: list(pair) for name, pair in placed.items()}

    def gather(tensors):
        return gather_exchange([wts[name][l] for name, l in tensors])

    def gathered(tensors, arrays):
        for (name, l), whole in zip(tensors, arrays):
            wts[name][l] = whole

    def weight(name, l):
        return _whole_weight(name, wts[name][l])

    def fwd_ex(grp, l):
        groups = GATHER_BEHIND.get((grp, l))
        return (groups, gather(groups)) if groups else (None, None)

    h = jnp.concatenate([x1, ctx1], axis=0)
    saved = []
    for l in range(2):
        h0 = h
        groups, ex = fwd_ex("ffn1", l)
        (h1, ab1, f1), got = ffn_fwd(h0, mods, norms[0], weight("ffn1_in", l), weight("ffn1_out", l), l, 0, n_lat, f"ffn1_fwd_{l}", ex)
        gathered(groups or [], got)
        groups, ex = fwd_ex("proj", l)
        (u, q, k, v), got = proj_fwd(h1, mods, norms[1], weight("w_in", l), cos, sin, l, n_lat, f"proj_fwd_{l}", ex)
        gathered(groups or [], got)
        groups, ex = fwd_ex("mix", l)
        (h2, cat, lse, mo), got = mix_fwd(h1, q, k, v, u, w_pool, pool_scale, sink, weight("w_out", l), mods, tables, l, t,
                                          f"mix_fwd_{l}", ex)
        gathered(groups or [], got)
        groups, ex = fwd_ex("ffn2", l)
        (h, ab2, f2), got = ffn_fwd(h2, mods, norms[2], weight("ffn2_in", l), weight("ffn2_out", l), l, 6, n_lat, f"ffn2_fwd_{l}", ex)
        gathered(groups or [], got)
        saved.append((h0, ab1, f1, h1, u, q, k, v, cat, lse, mo, h2, ab2, f2))
    dh, loss_blk = loss_head(h, target, nfinal, t, "loss_head")

    halves = {name: [None, None] for name in BIG}
    pending = []

    def summed_in_pair(grp, l, name_a, g_a, name_b, wgrad_b):
        g_b, got_a = wgrad_b(pair_exchange([_slot_major(name_a, g_a[1])]))
        sum_a, got_b = pair_sum(_slot_major(name_a, g_a[0]), got_a[0], place, f"pair_sum_{name_a}_{l}",
                                pair_exchange([_slot_major(name_b, g_b[1])]))
        sums = {name_a: sum_a, name_b: pair_sum(_slot_major(name_b, g_b[0]), got_b[0], place, f"pair_sum_{name_b}_{l}")}
        pending.append((grp, l, [sums[n] for n in GROUPS[grp]]))

    lacking = []

    def riders():
        return (scatter_exchange([p16 for _, p16 in pending[0][2]]) if pending else None,
                pair_fill_exchange([halves[name][l] for name, l in lacking]) if lacking else None)

    def carried(got, exs):
        got, filled = split_outputs(got, *exs)
        for (name, l), whole in zip(list(lacking), filled):
            halves[name][l] = whole
            lacking.remove((name, l))
        if pending:
            grp, l, pairs = pending.pop(0)
            for i, name in enumerate(GROUPS[grp]):
                halves[name][l] = chip_sum(pairs[i][0], got[3 * i:3 * i + 3], place, f"chip_sum_{name}_{l}")
                lacking.append((name, l))

    small = [None, None]
    for l in (1, 0):
        h0, ab1, f1, h1, u, q, k, v, cat, lse, mo, h2, ab2, f2 = saved[l]
        exs = riders()
        (dh, dab, df, n, act, dm_f2), got = ffn_bwd(h2, ab2, f2, dh, mods, norms[2], weight("ffn2_in", l), weight("ffn2_out", l),
                                                    l, 6, n_lat, f"ffn2_bwd_{l}", both(*exs))
        carried(got, exs)
        g_in, _ = wgrad(n, dab, D // 2, FF_COLS, FF_COLS, f"ffn2_in_wgrad_{l}")
        summed_in_pair("ffn2", l, "ffn2_in", g_in, "ffn2_out",
                       lambda ex, a=act, b=df: wgrad(a, b, D_FF // 2, D // 2, None, f"ffn2_out_wgrad_{l}", ex))
        exs = riders()
        (dq, dk, dv, du, dmo, dwp, dps, dsink, dm_gate), got = mix_bwd(
            dh, mo, q, k, v, u, lse, w_pool, pool_scale, sink, weight("w_out", l), mods, tables, l, t, f"mix_bwd_{l}", both(*exs))
        carried(got, exs)
        g_wo, _ = wgrad(cat, dmo, POOL_W + ATTN_W, D, None, f"w_out_wgrad_{l}")
        dh, dp, n, dm_mix = proj_bwd(h1, du, dq, dk, dv, dh, mods, norms[1], weight("w_in", l), cos, sin, l, n_lat, f"proj_bwd_{l}")
        summed_in_pair("mix", l, "w_out", g_wo, "w_in",
                       lambda ex, a=n, b=dp: wgrad(a, b, D, PROJ_W // 2, None, f"w_in_wgrad_{l}", ex))
        exs = riders()
        (dh, dab, df, n, act, dm_f1), got = ffn_bwd(h0, ab1, f1, dh, mods, norms[0], weight("ffn1_in", l), weight("ffn1_out", l),
                                                    l, 0, n_lat, f"ffn1_bwd_{l}", both(*exs))
        carried(got, exs)
        small[l] = dict(dm_f1=dm_f1, dm_mix=dm_mix, dm_gate=dm_gate, dm_f2=dm_f2, dwp=dwp, dps=dps, dsink=dsink)
        if l:
            g_in, _ = wgrad(n, dab, D // 2, FF_COLS, FF_COLS, f"ffn1_in_wgrad_{l}")
            summed_in_pair("ffn1", l, "ffn1_in", g_in, "ffn1_out",
                           lambda ex, a=act, b=df: wgrad(a, b, D_FF // 2, D // 2, None, f"ffn1_out_wgrad_{l}", ex))
    g_out, _ = wgrad(act, df, D_FF // 2, D // 2, None, "ffn1_out_wgrad_0")
    riding = (gather8_exchange(small_blocks(small, loss_blk)), pair_exchange([_slot_major("ffn1_out", g_out[1])]),
              pair_fill_exchange([halves[name][l] for name, l in lacking]))
    g_in, got = wgrad(n, dab, D // 2, FF_COLS, FF_COLS, "ffn1_in_wgrad_0", both(*riding))
    small_all, got_out, filled = split_outputs(got, *riding)
    for (name, l), whole in zip(lacking, filled):
        halves[name][l] = whole
    got_in = run_exchange(pair_exchange([_slot_major("ffn1_in", g_in[1])]), "pair_exchange_ffn1_in_0")
    last = {"ffn1_in": pair_sum(_slot_major("ffn1_in", g_in[0]), got_in[0], place, "pair_sum_ffn1_in_0"),
            "ffn1_out": pair_sum(_slot_major("ffn1_out", g_out[0]), got_out[0], place, "pair_sum_ffn1_out_0")}
    return dh[:t], halves, last, small_all


def _silu_grad(z):
    sg = jax.nn.sigmoid(z)
    return sg * (1 + z * (1 - sg))


def kernel(x, c, ctx, c_ctx, w_mod, b_mod, norm_ffn1, w_ffn1_in, w_ffn1_out, norm_mix, w_in, w_pool, pool_scale, sink, w_out, norm_ffn2, w_ffn2_in, w_ffn2_out, norm_final, loss_target, m_c_ctx, m_w_mod, m_b_mod, m_norm_ffn1, m_w_ffn1_in, m_w_ffn1_out, m_norm_mix, m_w_in, m_w_pool, m_pool_scale, m_sink, m_w_out, m_norm_ffn2, m_w_ffn2_in, m_w_ffn2_out, m_norm_final, v_c_ctx, v_w_mod, v_b_mod, v_norm_ffn1, v_w_ffn1_in, v_w_ffn1_out, v_norm_mix, v_w_in, v_w_pool, v_pool_scale, v_sink, v_w_out, v_norm_ffn2, v_w_ffn2_in, v_w_ffn2_out, v_norm_final):
    px, py, pc = _place()
    slot, me = 2 * px + py, 4 * px + 2 * py + pc
    n_grp = len(POOL_WINDOWS)

    (c_rows,) = all_gather([c.reshape(8, D // 8)], "gather_c")
    c_all = jnp.concatenate([c_rows.reshape(N_DEV, D), c_ctx.reshape(1, D), jnp.zeros((16 - N_DEV - 1, D), F32)], axis=0)

    place = jnp.stack([pc, slot]).astype(jnp.int32)
    shards = dict(ffn1_in=w_ffn1_in, ffn1_out=w_ffn1_out, w_in=w_in, w_out=w_out, ffn2_in=w_ffn2_in, ffn2_out=w_ffn2_out)
    first = [("ffn1_in", 0), ("ffn1_out", 0)]
    placed = {name: [None, None] for name in BIG}
    for name, l in first:
        placed[name][l] = cast_place(shards[name], l, place, f"cast_{name}_{l}")
    send_sems, recv_sems, *bufs, token = gather_start([placed[name][l] for name, l in first], "gather_first_start")

    b_cols = lax.dynamic_slice(b_mod, (0, slot * MOD_COLS), (2, MOD_COLS)).reshape(2, 1, MOD_COLS)
    c_all = c_all + token[0, 0]
    mod_cols = mod_rows(c_all, w_mod, b_cols, "mod_rows")
    others = [(name, l) for name in BIG for l in range(2) if (name, l) not in first]
    for name, l in others:
        placed[name][l] = cast_place(shards[name], l, place, f"cast_{name}_{l}")
    (mod_parts,) = all_gather([mod_cols], "gather_mods")
    mods_all = mod_parts[0::2].transpose(1, 2, 0, 3).reshape(2, 16, N_MOD * D)
    mx = lax.dynamic_slice(mods_all, (0, me, 0), (2, 1, N_MOD * D)).reshape(2, N_MOD, D)
    mc = mods_all[:, N_DEV].reshape(2, N_MOD, D)
    pad = jnp.zeros((2, 16 - N_MOD, D), F32)
    mods = jnp.stack([jnp.concatenate([mx, pad], axis=1), jnp.concatenate([mc, pad], axis=1)], axis=1)

    bufs = gather_wait(send_sems, recv_sems, bufs, [placed[name][l] for name, l in others] + [mod_parts], "gather_first_wait")
    for (name, l), whole in zip(first, run_exchange(pass_on_exchange(bufs), "gather_first_pass_on")):
        placed[name][l] = whole
    norms = [g.reshape(2, 1, D) for g in (norm_ffn1, norm_mix, norm_ffn2)]
    row_sums = ("dm_f1", "dm_mix", "dm_gate", "dm_f2")

    def small_blocks(small, loss_blk):
        stacked = {k: jnp.stack([small[0][k], small[1][k]]) for k in row_sums + ("dwp", "dps", "dsink")}
        return ([stacked[k].reshape(32, D) for k in row_sums]
                + [stacked["dwp"].reshape(2 * n_grp * GROUP, GROUP), stacked["dps"].reshape(16, POOL_W),
                   stacked["dsink"].reshape(16, BLK), loss_blk])

    dx, halves, last, small_all = local_step(x[0], ctx[0], loss_target[0], mods, norms, norm_final.reshape(1, D), placed,
                                                   w_pool.astype(BF16), pool_scale.reshape(2, 1, POOL_W), sink, place, small_blocks)
    grads = {}

    *g_dm, g_dwp, g_dps, g_dsink, g_loss = small_all
    tot, rows, fin = reduce_small(*[g.reshape(N_DEV, 2, 2, 8, D) for g in g_dm], g_loss, "reduce_small")
    s_dwp, s_dps, s_dsink = sum8([g_dwp, g_dps, g_dsink], "sum_pool_sink")
    grads.update(
        w_pool=s_dwp.reshape(2, n_grp, GROUP, GROUP), pool_scale=s_dps.reshape(2, 8, POOL_W)[:, 0],
        sink=s_dsink.reshape(2, 8, BLK)[:, 0, :N_HEADS], b_mod=tot[:, :N_MOD].reshape(2, N_MOD * D),
        norm_ffn1=tot[:, N_MOD], norm_mix=tot[:, N_MOD + 1], norm_ffn2=tot[:, N_MOD + 2], norm_final=fin[0])
    loss = fin[1, 0]

    dmod_cols = lax.dynamic_slice(rows[:, :, :N_MOD, :].reshape(2, 16, N_MOD * D), (0, 0, slot * MOD_COLS), (2, 16, MOD_COLS))
    grads["w_mod"], dc = mod_grads(c_all, dmod_cols, w_mod, "mod_grads")
    (g_dc,) = all_gather([dc], "gather_dc")
    (s_dc,) = sum8([g_dc], "sum_dc")
    (d_c_ctx,) = elementwise(lambda d, z: (0.5 * d * _silu_grad(z),), [s_dc[N_DEV:N_DEV + 1], c_ctx.reshape(1, D)], [F32], "c_ctx_grad")
    started = {name: scatter_start(last[name][1], f"scatter_last_start_{name}") for name in last}
    grads["c_ctx"] = d_c_ctx.reshape(D) + sum(st[-1][0, :1] for st in started.values())

    given = dict(c_ctx=(c_ctx, m_c_ctx, v_c_ctx), w_mod=(w_mod, m_w_mod, v_w_mod), b_mod=(b_mod, m_b_mod, v_b_mod),
                 norm_ffn1=(norm_ffn1, m_norm_ffn1, v_norm_ffn1), w_ffn1_in=(w_ffn1_in, m_w_ffn1_in, v_w_ffn1_in),
                 w_ffn1_out=(w_ffn1_out, m_w_ffn1_out, v_w_ffn1_out), norm_mix=(norm_mix, m_norm_mix, v_norm_mix),
                 w_in=(w_in, m_w_in, v_w_in), w_pool=(w_pool, m_w_pool, v_w_pool),
                 pool_scale=(pool_scale, m_pool_scale, v_pool_scale), sink=(sink, m_sink, v_sink), w_out=(w_out, m_w_out, v_w_out),
                 norm_ffn2=(norm_ffn2, m_norm_ffn2, v_norm_ffn2), w_ffn2_in=(w_ffn2_in, m_w_ffn2_in, v_w_ffn2_in),
                 w_ffn2_out=(w_ffn2_out, m_w_ffn2_out, v_w_ffn2_out), norm_final=(norm_final, m_norm_final, v_norm_final))
    shard = {(name, l): halves[name][l] for name in BIG for l in range(2)}

    def update(name):
        w, m, v = given[name]
        if name in BIG or name[2:] in BIG:
            key = name if name in BIG else name[2:]
            return adamw_layers(w, shard[key, 0], shard[key, 1], m, v, f"adamw_{name}")
        return [grads[name], *adamw(w, grads[name], m, v, f"adamw_{name}")]

    done = {name: update(name) for name in given if name[2:] not in last}
    between = [done[name][3] for name in done if name[2:] in BIG or name in BIG] + [done["w_mod"][3]]
    summed = []
    for name, (send_sems, recv_sems, src_thru, *lands, _) in started.items():
        got = scatter_wait(send_sems, recv_sems, src_thru, lands, between, f"scatter_last_wait_{name}")
        summed.append(chip_sum(last[name][0], got, place, f"chip_sum_{name}_0"))
    for name, whole in zip(started, pair_gather(summed, "grad_pair_gather_last")):
        shard[name, 0] = whole
        done["w_" + name] = update("w_" + name)
    return (loss, dx[None], *[done[name][i] for i in range(4) for name in given])
```
